```python
import math
import jax
import jax.numpy as jnp
from jax import lax
import numpy as np

D_MODEL = 1024
BATCH = 8
SEQ = 8192
DEPTH = 4

N_AB = (DEPTH + 1) // 2
N_CD = DEPTH // 2

SSD_HEADS = 8
SSD_HEAD_DIM = 64
SSD_D_INNER = SSD_HEADS * SSD_HEAD_DIM
SSD_GROUPS = 2
SSD_D_STATE = 64
SSD_CONV = 4
SSD_CHUNK = 128
SSD_CONV_DIM = SSD_D_INNER + 2 * SSD_GROUPS * SSD_D_STATE

HG_HEADS = 4
HG_KEY_DIM = 128
HG_VAL_DIM = 128
HG_WIDTH = HG_HEADS * HG_KEY_DIM
HG_CHUNK = 64

SWA_Q_HEADS = 8
SWA_KV_HEADS = 2
SWA_HEAD_DIM = 64
SWA_WINDOW = 128
SWA_BLOCK = 128

RG_WIDTH = 512
RG_BLOCKS = 8
RG_BLOCK_DIM = RG_WIDTH // RG_BLOCKS
RG_CONV = 4
RG_C = 8.0

FFN_DIM = 2816
FFN_CONV = 3

LN_EPS = 1e-5
RMS_EPS = 1e-6
MASK_VALUE = -1e9
ALPHA = (2 * DEPTH) ** 0.25
BETA = (8 * DEPTH) ** -0.25

AB_SIZES = (SSD_D_INNER, SSD_CONV_DIM, SSD_HEADS, HG_WIDTH, HG_WIDTH, HG_HEADS * HG_VAL_DIM, HG_HEADS * HG_VAL_DIM)
AB_IN = sum(AB_SIZES)
AB_OUT_IN = SSD_D_INNER + HG_HEADS * HG_VAL_DIM
CD_SIZES = (SWA_Q_HEADS * SWA_HEAD_DIM, SWA_KV_HEADS * SWA_HEAD_DIM, SWA_KV_HEADS * SWA_HEAD_DIM, RG_WIDTH, RG_WIDTH)
CD_IN = sum(CD_SIZES)
CD_OUT_IN = SWA_Q_HEADS * SWA_HEAD_DIM + RG_WIDTH

kernel_name = 'hybrid_ssd_hgrn2_swa_rglru_deepnorm'


def _layer_norm(x, g, b):
    xf = x.astype(jnp.float32)
    mu = jnp.mean(xf, -1, keepdims=True)
    var = jnp.mean(jnp.square(xf - mu), -1, keepdims=True)
    return ((xf - mu) * lax.rsqrt(var + LN_EPS) * g.astype(jnp.float32) + b.astype(jnp.float32)).astype(x.dtype)


def _rms_norm(x, w):
    xf = x.astype(jnp.float32)
    return xf * lax.rsqrt(jnp.mean(jnp.square(xf), -1, keepdims=True) + RMS_EPS) * w.astype(jnp.float32)


def _split(h, sizes):
    return jnp.split(h, np.cumsum(sizes)[:-1].tolist(), axis=-1)


def _causal_dwconv(x, w, b):
    width = w.shape[0]
    y = lax.conv_general_dilated(x, w[:, None, :].astype(x.dtype), window_strides=(1,),
                                 padding=((width - 1, 0),), dimension_numbers=('NWC', 'WIO', 'NWC'),
                                 feature_group_count=x.shape[-1])
    return y + b.astype(x.dtype)


def _masked_exp(diff, mask):
    return jnp.where(mask, jnp.exp(jnp.where(mask, diff, 0.0)), 0.0)


def _decay_matrix(cs):
    t = cs.shape[-1]
    mask = jnp.tril(jnp.ones((t, t), dtype=bool))
    return _masked_exp(cs[..., :, None] - cs[..., None, :], mask)


def _ssd_scan(x, dt, a, bm, cm):
    b, s, h, p = x.shape
    g, n = bm.shape[-2:]
    L = SSD_CHUNK
    nc = s // L
    rep = h // g
    bh = jnp.repeat(bm, rep, axis=2).reshape(b, nc, L, h, n)
    ch = jnp.repeat(cm, rep, axis=2).reshape(b, nc, L, h, n)
    xc = (x * dt[..., None]).reshape(b, nc, L, h, p)
    a_cs = jnp.cumsum((dt * a).reshape(b, nc, L, h).transpose(0, 3, 1, 2), axis=-1)
    scores = jnp.einsum('bclhn,bcshn->bhcls', ch, bh) * _decay_matrix(a_cs)
    y_diag = jnp.einsum('bhcls,bcshp->bclhp', scores, xc)
    decay_states = jnp.exp(a_cs[..., -1:] - a_cs)
    states = jnp.einsum('bclhn,bhcl,bclhp->bchpn', bh, decay_states, xc)
    states = jnp.concatenate([jnp.zeros_like(states[:, :1]), states], axis=1)
    chunk_cs = jnp.cumsum(jnp.pad(a_cs[..., -1], ((0, 0), (0, 0), (1, 0))), axis=-1)
    states = jnp.einsum('bhzc,bchpn->bzhpn', _decay_matrix(chunk_cs), states)[:, :-1]
    y_off = jnp.einsum('bclhn,bchpn,bhcl->bclhp', ch, states, jnp.exp(a_cs))
    return (y_diag + y_off).reshape(b, s, h, p)


def _ssd_mixer(z, xbc, dt_raw, conv_w, conv_b, dt_bias, a_log, d_skip, norm_w):
    bsz, seq, _ = z.shape
    xbc = jax.nn.silu(_causal_dwconv(xbc, conv_w, conv_b)).astype(jnp.float32)
    xs, bm, cm = _split(xbc, (SSD_D_INNER, SSD_GROUPS * SSD_D_STATE, SSD_GROUPS * SSD_D_STATE))
    xs = xs.reshape(bsz, seq, SSD_HEADS, SSD_HEAD_DIM)
    bm = bm.reshape(bsz, seq, SSD_GROUPS, SSD_D_STATE)
    cm = cm.reshape(bsz, seq, SSD_GROUPS, SSD_D_STATE)
    dt = jax.nn.softplus(dt_raw.astype(jnp.float32) + dt_bias.astype(jnp.float32))
    a = -jnp.exp(a_log.astype(jnp.float32))
    y = _ssd_scan(xs, dt, a, bm, cm) + d_skip.astype(jnp.float32)[:, None] * xs
    y = y.reshape(bsz, seq, SSD_D_INNER) * jax.nn.silu(z.astype(jnp.float32))
    y = _rms_norm(y.reshape(bsz, seq, SSD_GROUPS, -1), norm_w.reshape(SSD_GROUPS, -1))
    return y.reshape(bsz, seq, SSD_D_INNER)


def _hgrn2_scan(q, k, v, log_f):
    b, s, h, dk = q.shape
    dv = v.shape[-1]
    L = HG_CHUNK
    nc = s // L
    to_chunks = lambda t: t.reshape(b, nc, L, h, t.shape[-1]).transpose(1, 0, 3, 2, 4)
    mask = jnp.tril(jnp.ones((L, L), dtype=bool))[:, :, None]

    def step(state, inp):
        qc, kc, vc, gc = inp
        bc = jnp.cumsum(gc, axis=2)
        decay = _masked_exp(bc[:, :, :, None, :] - bc[:, :, None, :, :], mask)
        attn = jnp.einsum('bhtk,bhsk,bhtsk->bhts', qc, kc, decay)
        out = jnp.einsum('bhts,bhsv->bhtv', attn, vc) + jnp.einsum('bhtk,bhkv->bhtv', qc * jnp.exp(bc), state)
        b_last = bc[:, :, -1:, :]
        state = jnp.exp(b_last[:, :, 0])[..., None] * state + jnp.einsum('bhsk,bhsv->bhkv', kc * jnp.exp(b_last - bc), vc)
        return state, out

    state0 = jnp.zeros((b, h, dk, dv), jnp.float32)
    _, out = lax.scan(step, state0, (to_chunks(q), to_chunks(k), to_chunks(v), to_chunks(log_f)))
    return out.transpose(1, 0, 3, 2, 4).reshape(b, s, h, dv)


def _hgrn2_mixer(hq, hf, hi, hg, lb, norm_w):
    bsz, seq, _ = hq.shape
    q = jax.nn.silu(hq.astype(jnp.float32)).reshape(bsz, seq, HG_HEADS, HG_KEY_DIM)
    fx = hf.astype(jnp.float32).reshape(bsz, seq, HG_HEADS, HG_KEY_DIM)
    lb = lb.reshape(HG_HEADS, HG_KEY_DIM)
    log_f = jnp.log(lb + (1.0 - lb) * jax.nn.sigmoid(fx))
    k = (1.0 - lb) * jax.nn.sigmoid(-fx)
    v = hi.astype(jnp.float32).reshape(bsz, seq, HG_HEADS, HG_VAL_DIM)
    o = _hgrn2_scan(q, k, v, log_f)
    o = _rms_norm(o, norm_w) * jax.nn.silu(hg.astype(jnp.float32).reshape(bsz, seq, HG_HEADS, HG_VAL_DIM))
    return o.reshape(bsz, seq, HG_HEADS * HG_VAL_DIM)


def _swa_sink_attention(q, k, v, sinks):
    b, s, hq, d = q.shape
    hkv = k.shape[2]
    grp = hq // hkv
    T = SWA_BLOCK
    nb = s // T
    qb = q.reshape(b, nb, T, hkv, grp, d)

    def banded(t):
        tb = t.reshape(b, nb, T, hkv, d)
        prev = jnp.pad(tb, ((0, 0), (1, 0), (0, 0), (0, 0), (0, 0)))[:, :-1]
        return jnp.concatenate([prev, tb], axis=2)

    kb, vb = banded(k), banded(v)
    scores = jnp.einsum('bnqhgd,bnkhd->bnhgqk', qb, kb).astype(jnp.float32) * (d ** -0.5)
    rel = (jnp.arange(T)[:, None] + T) - jnp.arange(2 * T)[None, :]
    band = (rel >= 0) & (rel < SWA_WINDOW)
    valid = (jnp.arange(nb)[:, None] > 0) | (jnp.arange(2 * T)[None, :] >= T)
    mask = band[None] & valid[:, None, :]
    scores = jnp.where(mask[None, :, None, None], scores, MASK_VALUE)
    sink = sinks.astype(jnp.float32).reshape(hkv, grp)
    sink_col = jnp.broadcast_to(sink[None, None, :, :, None, None], scores.shape[:-1] + (1,))
    probs = jax.nn.softmax(jnp.concatenate([scores, sink_col], axis=-1), axis=-1)[..., :-1]
    out = jnp.einsum('bnhgqk,bnkhd->bnqhgd', probs.astype(v.dtype), vb)
    return out.reshape(b, s, hq * d)


def _rglru_mixer(gate, xr, conv_w, conv_b, wa, ba, wx, bx, lam):
    bsz, seq, _ = xr.shape
    xc = _causal_dwconv(xr, conv_w, conv_b).astype(jnp.float32)
    xblk = xc.reshape(bsz, seq, RG_BLOCKS, RG_BLOCK_DIM)
    r = jax.nn.sigmoid(jnp.einsum('bsgi,gij->bsgj', xblk, wa.astype(jnp.float32)).reshape(bsz, seq, RG_WIDTH) + ba.astype(jnp.float32))
    i = jax.nn.sigmoid(jnp.einsum('bsgi,gij->bsgj', xblk, wx.astype(jnp.float32)).reshape(bsz, seq, RG_WIDTH) + bx.astype(jnp.float32))
    log_a = -RG_C * r * jax.nn.softplus(-lam.astype(jnp.float32))
    a = jnp.exp(log_a)
    u = jnp.sqrt(jnp.maximum(-jnp.expm1(2.0 * log_a), 0.0)) * (i * xc)

    def combine(c1, c2):
        a1, u1 = c1
        a2, u2 = c2
        return a1 * a2, a2 * u1 + u2

    _, h = lax.associative_scan(combine, (a, u), axis=1)
    return h * jax.nn.gelu(gate.astype(jnp.float32))


def _ab_mixer(x, w_in, conv_w, conv_b, dt_bias, a_log, d_skip, ssd_norm_w, lb, hg_norm_w, w_out):
    h = x @ w_in
    z, xbc, dt_raw, hq, hf, hi, hg = _split(h, AB_SIZES)
    y_a = _ssd_mixer(z, xbc, dt_raw, conv_w, conv_b, dt_bias, a_log, d_skip, ssd_norm_w)
    y_b = _hgrn2_mixer(hq, hf, hi, hg, lb, hg_norm_w)
    return jnp.concatenate([y_a, y_b], axis=-1).astype(x.dtype) @ w_out


def _cd_mixer(x, w_in, sinks, conv_w, conv_b, wa, ba, wx, bx, lam, w_out):
    bsz, seq, _ = x.shape
    h = x @ w_in
    q, k, v, gate, xr = _split(h, CD_SIZES)
    y_c = _swa_sink_attention(q.reshape(bsz, seq, SWA_Q_HEADS, SWA_HEAD_DIM),
                              k.reshape(bsz, seq, SWA_KV_HEADS, SWA_HEAD_DIM),
                              v.reshape(bsz, seq, SWA_KV_HEADS, SWA_HEAD_DIM), sinks)
    y_d = _rglru_mixer(gate, xr, conv_w, conv_b, wa, ba, wx, bx, lam)
    return jnp.concatenate([y_c.astype(jnp.float32), y_d], axis=-1).astype(x.dtype) @ w_out


def _conv_ffn(x, w_up, conv_w, conv_b, w_down):
    h = _causal_dwconv(x @ w_up, conv_w, conv_b)
    g, u = jnp.split(h, 2, axis=-1)
    return (jax.nn.silu(g) * u) @ w_down


def _fwd_setup_inputs(seed: int = 0) -> dict:
    key = jax.random.key(seed)
    keys = list(jax.random.split(key, 40))

    def nrm(shape, scale):
        return jax.random.normal(keys.pop(), shape, jnp.float32) * scale

    def unif(shape, lo, hi):
        return jax.random.uniform(keys.pop(), shape, jnp.float32, minval=lo, maxval=hi)

    x = nrm((BATCH, SEQ, D_MODEL), 1.0)
    ab_w_in = nrm((N_AB, D_MODEL, AB_IN), D_MODEL ** -0.5)
    ssd_conv_w = nrm((N_AB, SSD_CONV, SSD_CONV_DIM), SSD_CONV ** -0.5)
    ssd_conv_b = nrm((N_AB, SSD_CONV_DIM), 0.02)
    dt0 = jnp.exp(unif((N_AB, SSD_HEADS), math.log(1e-3), math.log(1e-1)))
    ssd_dt_bias = dt0 + jnp.log(-jnp.expm1(-dt0))
    ssd_a_log = jnp.log(unif((N_AB, SSD_HEADS), 1.0, 16.0))
    ssd_d = 1.0 + nrm((N_AB, SSD_HEADS), 0.1)
    ssd_norm_w = 1.0 + nrm((N_AB, SSD_D_INNER), 0.1)
    hg_lower = nrm((N_AB, HG_WIDTH), 1.0)
    hg_norm_w = 1.0 + nrm((N_AB, HG_VAL_DIM), 0.1)
    ab_w_out = nrm((N_AB, AB_OUT_IN, D_MODEL), (AB_OUT_IN ** -0.5) * BETA)
    cd_w_in = nrm((N_CD, D_MODEL, CD_IN), D_MODEL ** -0.5)
    swa_sinks = nrm((N_CD, SWA_Q_HEADS), 0.5)
    rg_conv_w = nrm((N_CD, RG_CONV, RG_WIDTH), RG_CONV ** -0.5)
    rg_conv_b = nrm((N_CD, RG_WIDTH), 0.02)
    rg_wa = nrm((N_CD, RG_BLOCKS, RG_BLOCK_DIM, RG_BLOCK_DIM), RG_BLOCK_DIM ** -0.5)
    rg_ba = nrm((N_CD, RG_WIDTH), 0.02)
    rg_wx = nrm((N_CD, RG_BLOCKS, RG_BLOCK_DIM, RG_BLOCK_DIM), RG_BLOCK_DIM ** -0.5)
    rg_bx = nrm((N_CD, RG_WIDTH), 0.02)
    sig = unif((N_CD, RG_WIDTH), 0.9, 0.999) ** (1.0 / RG_C)
    rg_lambda = jnp.log(sig) - jnp.log1p(-sig)
    cd_w_out = nrm((N_CD, CD_OUT_IN, D_MODEL), (CD_OUT_IN ** -0.5) * BETA)
    ffn_w_up = nrm((DEPTH, D_MODEL, 2 * FFN_DIM), D_MODEL ** -0.5)
    ffn_conv_w = nrm((DEPTH, FFN_CONV, 2 * FFN_DIM), FFN_CONV ** -0.5)
    ffn_conv_b = nrm((DEPTH, 2 * FFN_DIM), 0.02)
    ffn_w_down = nrm((DEPTH, FFN_DIM, D_MODEL), (FFN_DIM ** -0.5) * BETA)
    ln_g = 1.0 + nrm((DEPTH, 2, D_MODEL), 0.05)
    ln_b = nrm((DEPTH, 2, D_MODEL), 0.02)
    return {'x': x, 'ab_w_in': ab_w_in, 'ssd_conv_w': ssd_conv_w, 'ssd_conv_b': ssd_conv_b,
            'ssd_dt_bias': ssd_dt_bias, 'ssd_a_log': ssd_a_log, 'ssd_d': ssd_d, 'ssd_norm_w': ssd_norm_w,
            'hg_lower': hg_lower, 'hg_norm_w': hg_norm_w, 'ab_w_out': ab_w_out, 'cd_w_in': cd_w_in,
            'swa_sinks': swa_sinks, 'rg_conv_w': rg_conv_w, 'rg_conv_b': rg_conv_b, 'rg_wa': rg_wa,
            'rg_ba': rg_ba, 'rg_wx': rg_wx, 'rg_bx': rg_bx, 'rg_lambda': rg_lambda, 'cd_w_out': cd_w_out,
            'ffn_w_up': ffn_w_up, 'ffn_conv_w': ffn_conv_w, 'ffn_conv_b': ffn_conv_b, 'ffn_w_down': ffn_w_down,
            'ln_g': ln_g, 'ln_b': ln_b}


def _fwd_reference(x, ab_w_in, ssd_conv_w, ssd_conv_b, ssd_dt_bias, ssd_a_log, ssd_d, ssd_norm_w,
              hg_lower, hg_norm_w, ab_w_out, cd_w_in, swa_sinks, rg_conv_w, rg_conv_b, rg_wa,
              rg_ba, rg_wx, rg_bx, rg_lambda, cd_w_out, ffn_w_up, ffn_conv_w, ffn_conv_b, ffn_w_down,
              ln_g, ln_b):
    sm = jax.nn.softmax(hg_lower.astype(jnp.float32), axis=0)
    lb_all = jnp.clip(jnp.cumsum(sm, axis=0) - sm[0], 0.0, 1.0)
    for layer in range(DEPTH):
        j = layer // 2
        if layer % 2 == 0:
            m = _ab_mixer(x, ab_w_in[j], ssd_conv_w[j], ssd_conv_b[j], ssd_dt_bias[j], ssd_a_log[j],
                          ssd_d[j], ssd_norm_w[j], lb_all[j], hg_norm_w[j], ab_w_out[j])
        else:
            m = _cd_mixer(x, cd_w_in[j], swa_sinks[j], rg_conv_w[j], rg_conv_b[j], rg_wa[j], rg_ba[j],
                          rg_wx[j], rg_bx[j], rg_lambda[j], cd_w_out[j])
        x = _layer_norm(ALPHA * x + m, ln_g[layer, 0], ln_b[layer, 0])
        f = _conv_ffn(x, ffn_w_up[layer], ffn_conv_w[layer], ffn_conv_b[layer], ffn_w_down[layer])
        x = _layer_norm(ALPHA * x + f, ln_g[layer, 1], ln_b[layer, 1])
    return x


import jax as _jax
import jax.numpy as _jnp

TWIN_FORMAT = 'train_step'
FWD_PARAMS = ['x', 'ab_w_in', 'ssd_conv_w', 'ssd_conv_b', 'ssd_dt_bias', 'ssd_a_log', 'ssd_d', 'ssd_norm_w', 'hg_lower', 'hg_norm_w', 'ab_w_out', 'cd_w_in', 'swa_sinks', 'rg_conv_w', 'rg_conv_b', 'rg_wa', 'rg_ba', 'rg_wx', 'rg_bx', 'rg_lambda', 'cd_w_out', 'ffn_w_up', 'ffn_conv_w', 'ffn_conv_b', 'ffn_w_down', 'ln_g', 'ln_b']
TWIN_WEIGHTS = ['ab_w_in', 'ssd_conv_w', 'ssd_conv_b', 'ssd_dt_bias', 'ssd_a_log', 'ssd_d', 'ssd_norm_w', 'hg_lower', 'hg_norm_w', 'ab_w_out', 'cd_w_in', 'swa_sinks', 'rg_conv_w', 'rg_conv_b', 'rg_wa', 'rg_ba', 'rg_wx', 'rg_bx', 'rg_lambda', 'cd_w_out', 'ffn_w_up', 'ffn_conv_w', 'ffn_conv_b', 'ffn_w_down', 'ln_g', 'ln_b']
TWIN_DIFF_INPUT = 'x'
TWIN_INPUTS = ['x', 'ab_w_in', 'ssd_conv_w', 'ssd_conv_b', 'ssd_dt_bias', 'ssd_a_log', 'ssd_d', 'ssd_norm_w', 'hg_lower', 'hg_norm_w', 'ab_w_out', 'cd_w_in', 'swa_sinks', 'rg_conv_w', 'rg_conv_b', 'rg_wa', 'rg_ba', 'rg_wx', 'rg_bx', 'rg_lambda', 'cd_w_out', 'ffn_w_up', 'ffn_conv_w', 'ffn_conv_b', 'ffn_w_down', 'ln_g', 'ln_b', 'loss_target', 'm_ab_w_in', 'm_ssd_conv_w', 'm_ssd_conv_b', 'm_ssd_dt_bias', 'm_ssd_a_log', 'm_ssd_d', 'm_ssd_norm_w', 'm_hg_lower', 'm_hg_norm_w', 'm_ab_w_out', 'm_cd_w_in', 'm_swa_sinks', 'm_rg_conv_w', 'm_rg_conv_b', 'm_rg_wa', 'm_rg_ba', 'm_rg_wx', 'm_rg_bx', 'm_rg_lambda', 'm_cd_w_out', 'm_ffn_w_up', 'm_ffn_conv_w', 'm_ffn_conv_b', 'm_ffn_w_down', 'm_ln_g', 'm_ln_b', 'v_ab_w_in', 'v_ssd_conv_w', 'v_ssd_conv_b', 'v_ssd_dt_bias', 'v_ssd_a_log', 'v_ssd_d', 'v_ssd_norm_w', 'v_hg_lower', 'v_hg_norm_w', 'v_ab_w_out', 'v_cd_w_in', 'v_swa_sinks', 'v_rg_conv_w', 'v_rg_conv_b', 'v_rg_wa', 'v_rg_ba', 'v_rg_wx', 'v_rg_bx', 'v_rg_lambda', 'v_cd_w_out', 'v_ffn_w_up', 'v_ffn_conv_w', 'v_ffn_conv_b', 'v_ffn_w_down', 'v_ln_g', 'v_ln_b']
TWIN_OUTPUTS = ['loss', 'grad_x', 'grad_ab_w_in', 'grad_ssd_conv_w', 'grad_ssd_conv_b', 'grad_ssd_dt_bias', 'grad_ssd_a_log', 'grad_ssd_d', 'grad_ssd_norm_w', 'grad_hg_lower', 'grad_hg_norm_w', 'grad_ab_w_out', 'grad_cd_w_in', 'grad_swa_sinks', 'grad_rg_conv_w', 'grad_rg_conv_b', 'grad_rg_wa', 'grad_rg_ba', 'grad_rg_wx', 'grad_rg_bx', 'grad_rg_lambda', 'grad_cd_w_out', 'grad_ffn_w_up', 'grad_ffn_conv_w', 'grad_ffn_conv_b', 'grad_ffn_w_down', 'grad_ln_g', 'grad_ln_b', 'delta_ab_w_in', 'delta_ssd_conv_w', 'delta_ssd_conv_b', 'delta_ssd_dt_bias', 'delta_ssd_a_log', 'delta_ssd_d', 'delta_ssd_norm_w', 'delta_hg_lower', 'delta_hg_norm_w', 'delta_ab_w_out', 'delta_cd_w_in', 'delta_swa_sinks', 'delta_rg_conv_w', 'delta_rg_conv_b', 'delta_rg_wa', 'delta_rg_ba', 'delta_rg_wx', 'delta_rg_bx', 'delta_rg_lambda', 'delta_cd_w_out', 'delta_ffn_w_up', 'delta_ffn_conv_w', 'delta_ffn_conv_b', 'delta_ffn_w_down', 'delta_ln_g', 'delta_ln_b', 'new_m_ab_w_in', 'new_m_ssd_conv_w', 'new_m_ssd_conv_b', 'new_m_ssd_dt_bias', 'new_m_ssd_a_log', 'new_m_ssd_d', 'new_m_ssd_norm_w', 'new_m_hg_lower', 'new_m_hg_norm_w', 'new_m_ab_w_out', 'new_m_cd_w_in', 'new_m_swa_sinks', 'new_m_rg_conv_w', 'new_m_rg_conv_b', 'new_m_rg_wa', 'new_m_rg_ba', 'new_m_rg_wx', 'new_m_rg_bx', 'new_m_rg_lambda', 'new_m_cd_w_out', 'new_m_ffn_w_up', 'new_m_ffn_conv_w', 'new_m_ffn_conv_b', 'new_m_ffn_w_down', 'new_m_ln_g', 'new_m_ln_b', 'new_v_ab_w_in', 'new_v_ssd_conv_w', 'new_v_ssd_conv_b', 'new_v_ssd_dt_bias', 'new_v_ssd_a_log', 'new_v_ssd_d', 'new_v_ssd_norm_w', 'new_v_hg_lower', 'new_v_hg_norm_w', 'new_v_ab_w_out', 'new_v_cd_w_in', 'new_v_swa_sinks', 'new_v_rg_conv_w', 'new_v_rg_conv_b', 'new_v_rg_wa', 'new_v_rg_ba', 'new_v_rg_wx', 'new_v_rg_bx', 'new_v_rg_lambda', 'new_v_cd_w_out', 'new_v_ffn_w_up', 'new_v_ffn_conv_w', 'new_v_ffn_conv_b', 'new_v_ffn_w_down', 'new_v_ln_g', 'new_v_ln_b']
TWIN_LEAF_KINDS = {'loss': 'loss', 'grad_x': 'grad_x', 'grad_ab_w_in': 'grad_w', 'grad_ssd_conv_w': 'grad_w', 'grad_ssd_conv_b': 'grad_w', 'grad_ssd_dt_bias': 'grad_w', 'grad_ssd_a_log': 'grad_w', 'grad_ssd_d': 'grad_w', 'grad_ssd_norm_w': 'grad_w', 'grad_hg_lower': 'grad_w', 'grad_hg_norm_w': 'grad_w', 'grad_ab_w_out': 'grad_w', 'grad_cd_w_in': 'grad_w', 'grad_swa_sinks': 'grad_w', 'grad_rg_conv_w': 'grad_w', 'grad_rg_conv_b': 'grad_w', 'grad_rg_wa': 'grad_w', 'grad_rg_ba': 'grad_w', 'grad_rg_wx': 'grad_w', 'grad_rg_bx': 'grad_w', 'grad_rg_lambda': 'grad_w', 'grad_cd_w_out': 'grad_w', 'grad_ffn_w_up': 'grad_w', 'grad_ffn_conv_w': 'grad_w', 'grad_ffn_conv_b': 'grad_w', 'grad_ffn_w_down': 'grad_w', 'grad_ln_g': 'grad_w', 'grad_ln_b': 'grad_w', 'delta_ab_w_in': 'delta_w', 'delta_ssd_conv_w': 'delta_w', 'delta_ssd_conv_b': 'delta_w', 'delta_ssd_dt_bias': 'delta_w', 'delta_ssd_a_log': 'delta_w', 'delta_ssd_d': 'delta_w', 'delta_ssd_norm_w': 'delta_w', 'delta_hg_lower': 'delta_w', 'delta_hg_norm_w': 'delta_w', 'delta_ab_w_out': 'delta_w', 'delta_cd_w_in': 'delta_w', 'delta_swa_sinks': 'delta_w', 'delta_rg_conv_w': 'delta_w', 'delta_rg_conv_b': 'delta_w', 'delta_rg_wa': 'delta_w', 'delta_rg_ba': 'delta_w', 'delta_rg_wx': 'delta_w', 'delta_rg_bx': 'delta_w', 'delta_rg_lambda': 'delta_w', 'delta_cd_w_out': 'delta_w', 'delta_ffn_w_up': 'delta_w', 'delta_ffn_conv_w': 'delta_w', 'delta_ffn_conv_b': 'delta_w', 'delta_ffn_w_down': 'delta_w', 'delta_ln_g': 'delta_w', 'delta_ln_b': 'delta_w', 'new_m_ab_w_in': 'new_m', 'new_m_ssd_conv_w': 'new_m', 'new_m_ssd_conv_b': 'new_m', 'new_m_ssd_dt_bias': 'new_m', 'new_m_ssd_a_log': 'new_m', 'new_m_ssd_d': 'new_m', 'new_m_ssd_norm_w': 'new_m', 'new_m_hg_lower': 'new_m', 'new_m_hg_norm_w': 'new_m', 'new_m_ab_w_out': 'new_m', 'new_m_cd_w_in': 'new_m', 'new_m_swa_sinks': 'new_m', 'new_m_rg_conv_w': 'new_m', 'new_m_rg_conv_b': 'new_m', 'new_m_rg_wa': 'new_m', 'new_m_rg_ba': 'new_m', 'new_m_rg_wx': 'new_m', 'new_m_rg_bx': 'new_m', 'new_m_rg_lambda': 'new_m', 'new_m_cd_w_out': 'new_m', 'new_m_ffn_w_up': 'new_m', 'new_m_ffn_conv_w': 'new_m', 'new_m_ffn_conv_b': 'new_m', 'new_m_ffn_w_down': 'new_m', 'new_m_ln_g': 'new_m', 'new_m_ln_b': 'new_m', 'new_v_ab_w_in': 'new_v', 'new_v_ssd_conv_w': 'new_v', 'new_v_ssd_conv_b': 'new_v', 'new_v_ssd_dt_bias': 'new_v', 'new_v_ssd_a_log': 'new_v', 'new_v_ssd_d': 'new_v', 'new_v_ssd_norm_w': 'new_v', 'new_v_hg_lower': 'new_v', 'new_v_hg_norm_w': 'new_v', 'new_v_ab_w_out': 'new_v', 'new_v_cd_w_in': 'new_v', 'new_v_swa_sinks': 'new_v', 'new_v_rg_conv_w': 'new_v', 'new_v_rg_conv_b': 'new_v', 'new_v_rg_wa': 'new_v', 'new_v_rg_ba': 'new_v', 'new_v_rg_wx': 'new_v', 'new_v_rg_bx': 'new_v', 'new_v_rg_lambda': 'new_v', 'new_v_cd_w_out': 'new_v', 'new_v_ffn_w_up': 'new_v', 'new_v_ffn_conv_w': 'new_v', 'new_v_ffn_conv_b': 'new_v', 'new_v_ffn_w_down': 'new_v', 'new_v_ln_g': 'new_v', 'new_v_ln_b': 'new_v'}


def _forward(args):
    return _fwd_reference(*[args[k] for k in FWD_PARAMS])


def _output_shape():
    def fwd():
        inp = _fwd_setup_inputs(0)
        return _fwd_reference(*[inp[k] for k in FWD_PARAMS])
    out = _jax.eval_shape(fwd)
    return out.shape, out.dtype

N_MICROBATCH = 1
ADAM_LR = 0.001
ADAM_B1 = 0.9
ADAM_B2 = 0.999
ADAM_EPS = 1e-08
ADAM_WD = 0.01
ADAM_STEP = 10
PER_EXAMPLE_BATCH_AXIS = {'x': 0, 'loss_target': 0}
SHARED_INPUTS = []
_WEIGHT_DTYPES = {'ab_w_in': _jnp.float32, 'ssd_conv_w': _jnp.float32, 'ssd_conv_b': _jnp.float32, 'ssd_dt_bias': _jnp.float32, 'ssd_a_log': _jnp.float32, 'ssd_d': _jnp.float32, 'ssd_norm_w': _jnp.float32, 'hg_lower': _jnp.float32, 'hg_norm_w': _jnp.float32, 'ab_w_out': _jnp.float32, 'cd_w_in': _jnp.float32, 'swa_sinks': _jnp.float32, 'rg_conv_w': _jnp.float32, 'rg_conv_b': _jnp.float32, 'rg_wa': _jnp.float32, 'rg_ba': _jnp.float32, 'rg_wx': _jnp.float32, 'rg_bx': _jnp.float32, 'rg_lambda': _jnp.float32, 'cd_w_out': _jnp.float32, 'ffn_w_up': _jnp.float32, 'ffn_conv_w': _jnp.float32, 'ffn_conv_b': _jnp.float32, 'ffn_w_down': _jnp.float32, 'ln_g': _jnp.float32, 'ln_b': _jnp.float32}
MOMENT_SCALE = {'ab_w_in': 4.445715e-02, 'ssd_conv_w': 6.267118e-02, 'ssd_conv_b': 1.127790e-01, 'ssd_dt_bias': 1.184962e-01, 'ssd_a_log': 3.055849e-01, 'ssd_d': 4.218662e-01, 'ssd_norm_w': 7.317809e-02, 'hg_lower': 3.009888e-03, 'hg_norm_w': 8.169608e-02, 'ab_w_out': 1.432267e-01, 'cd_w_in': 2.908382e-02, 'swa_sinks': 1.090458e-02, 'rg_conv_w': 5.046870e-02, 'rg_conv_b': 6.431504e-01, 'rg_wa': 2.154773e-02, 'rg_ba': 1.411055e-02, 'rg_wx': 4.019275e-02, 'rg_bx': 1.714320e-02, 'rg_lambda': 2.567135e-02, 'cd_w_out': 8.206034e-02, 'ffn_w_up': 2.418280e-02, 'ffn_conv_w': 2.406518e-02, 'ffn_conv_b': 3.245757e-02, 'ffn_w_down': 9.397831e-02, 'ln_g': 2.401519e+01, 'ln_b': 1.953987e+00}


def _to_microbatches(a, axis):
    t = _jnp.moveaxis(a, axis, 0)
    t = t.reshape((N_MICROBATCH, t.shape[0] // N_MICROBATCH) + t.shape[1:])
    return _jnp.moveaxis(t, 1, axis + 1)


def setup_inputs(seed: int = 0) -> dict:
    inp = _fwd_setup_inputs(seed)
    key = _jax.random.fold_in(_jax.random.key(seed), 7919)
    shape, _ = _output_shape()
    out = dict(inp)
    out["loss_target"] = _jax.random.normal(_jax.random.fold_in(key, 0), shape, _jnp.float32)
    for i, name in enumerate(TWIN_WEIGHTS):
        w = inp[name].astype(_jnp.float32)
        if MOMENT_SCALE is None:
            s = _jnp.sqrt(_jnp.mean(_jnp.square(w)) + 1e-30)
        else:
            s = MOMENT_SCALE[name]
        km, kv = _jax.random.split(_jax.random.fold_in(key, i + 1))
        out[name] = w
        out["m_" + name] = s * _jax.random.normal(km, w.shape, _jnp.float32)
        out["v_" + name] = (s * s) * _jax.random.uniform(kv, w.shape, _jnp.float32, 0.5, 1.5)
    if N_MICROBATCH > 1:
        for name, axis in PER_EXAMPLE_BATCH_AXIS.items():
            out[name] = _to_microbatches(out[name], axis)
    return {'x': out['x'], 'ab_w_in': out['ab_w_in'], 'ssd_conv_w': out['ssd_conv_w'], 'ssd_conv_b': out['ssd_conv_b'], 'ssd_dt_bias': out['ssd_dt_bias'], 'ssd_a_log': out['ssd_a_log'], 'ssd_d': out['ssd_d'], 'ssd_norm_w': out['ssd_norm_w'], 'hg_lower': out['hg_lower'], 'hg_norm_w': out['hg_norm_w'], 'ab_w_out': out['ab_w_out'], 'cd_w_in': out['cd_w_in'], 'swa_sinks': out['swa_sinks'], 'rg_conv_w': out['rg_conv_w'], 'rg_conv_b': out['rg_conv_b'], 'rg_wa': out['rg_wa'], 'rg_ba': out['rg_ba'], 'rg_wx': out['rg_wx'], 'rg_bx': out['rg_bx'], 'rg_lambda': out['rg_lambda'], 'cd_w_out': out['cd_w_out'], 'ffn_w_up': out['ffn_w_up'], 'ffn_conv_w': out['ffn_conv_w'], 'ffn_conv_b': out['ffn_conv_b'], 'ffn_w_down': out['ffn_w_down'], 'ln_g': out['ln_g'], 'ln_b': out['ln_b'], 'loss_target': out['loss_target'], 'm_ab_w_in': out['m_ab_w_in'], 'm_ssd_conv_w': out['m_ssd_conv_w'], 'm_ssd_conv_b': out['m_ssd_conv_b'], 'm_ssd_dt_bias': out['m_ssd_dt_bias'], 'm_ssd_a_log': out['m_ssd_a_log'], 'm_ssd_d': out['m_ssd_d'], 'm_ssd_norm_w': out['m_ssd_norm_w'], 'm_hg_lower': out['m_hg_lower'], 'm_hg_norm_w': out['m_hg_norm_w'], 'm_ab_w_out': out['m_ab_w_out'], 'm_cd_w_in': out['m_cd_w_in'], 'm_swa_sinks': out['m_swa_sinks'], 'm_rg_conv_w': out['m_rg_conv_w'], 'm_rg_conv_b': out['m_rg_conv_b'], 'm_rg_wa': out['m_rg_wa'], 'm_rg_ba': out['m_rg_ba'], 'm_rg_wx': out['m_rg_wx'], 'm_rg_bx': out['m_rg_bx'], 'm_rg_lambda': out['m_rg_lambda'], 'm_cd_w_out': out['m_cd_w_out'], 'm_ffn_w_up': out['m_ffn_w_up'], 'm_ffn_conv_w': out['m_ffn_conv_w'], 'm_ffn_conv_b': out['m_ffn_conv_b'], 'm_ffn_w_down': out['m_ffn_w_down'], 'm_ln_g': out['m_ln_g'], 'm_ln_b': out['m_ln_b'], 'v_ab_w_in': out['v_ab_w_in'], 'v_ssd_conv_w': out['v_ssd_conv_w'], 'v_ssd_conv_b': out['v_ssd_conv_b'], 'v_ssd_dt_bias': out['v_ssd_dt_bias'], 'v_ssd_a_log': out['v_ssd_a_log'], 'v_ssd_d': out['v_ssd_d'], 'v_ssd_norm_w': out['v_ssd_norm_w'], 'v_hg_lower': out['v_hg_lower'], 'v_hg_norm_w': out['v_hg_norm_w'], 'v_ab_w_out': out['v_ab_w_out'], 'v_cd_w_in': out['v_cd_w_in'], 'v_swa_sinks': out['v_swa_sinks'], 'v_rg_conv_w': out['v_rg_conv_w'], 'v_rg_conv_b': out['v_rg_conv_b'], 'v_rg_wa': out['v_rg_wa'], 'v_rg_ba': out['v_rg_ba'], 'v_rg_wx': out['v_rg_wx'], 'v_rg_bx': out['v_rg_bx'], 'v_rg_lambda': out['v_rg_lambda'], 'v_cd_w_out': out['v_cd_w_out'], 'v_ffn_w_up': out['v_ffn_w_up'], 'v_ffn_conv_w': out['v_ffn_conv_w'], 'v_ffn_conv_b': out['v_ffn_conv_b'], 'v_ffn_w_down': out['v_ffn_w_down'], 'v_ln_g': out['v_ln_g'], 'v_ln_b': out['v_ln_b']}


def _loss(weights, diff, rest, loss_target):
    with _jax.named_scope("forward"):
        args = {**rest, TWIN_DIFF_INPUT: diff, **{k: w.astype(_WEIGHT_DTYPES[k]) for k, w in weights.items()}}
        y = _forward(args)
    with _jax.named_scope("loss_head"):
        err = _jnp.square(y.astype(_jnp.float32) - loss_target)
        return 0.5 * _jnp.sum(_jnp.mean(err, axis=-1)) if err.ndim else 0.5 * err


def _adamw(w, g, m, v):
    m = ADAM_B1 * m + (1.0 - ADAM_B1) * g
    v = ADAM_B2 * v + (1.0 - ADAM_B2) * _jnp.square(g)
    m_hat = m / (1.0 - ADAM_B1 ** ADAM_STEP)
    v_hat = v / (1.0 - ADAM_B2 ** ADAM_STEP)
    delta = -ADAM_LR * (m_hat / (_jnp.sqrt(v_hat) + ADAM_EPS) + ADAM_WD * w)
    return delta, m, v


def reference(x, ab_w_in, ssd_conv_w, ssd_conv_b, ssd_dt_bias, ssd_a_log, ssd_d, ssd_norm_w, hg_lower, hg_norm_w, ab_w_out, cd_w_in, swa_sinks, rg_conv_w, rg_conv_b, rg_wa, rg_ba, rg_wx, rg_bx, rg_lambda, cd_w_out, ffn_w_up, ffn_conv_w, ffn_conv_b, ffn_w_down, ln_g, ln_b, loss_target, m_ab_w_in, m_ssd_conv_w, m_ssd_conv_b, m_ssd_dt_bias, m_ssd_a_log, m_ssd_d, m_ssd_norm_w, m_hg_lower, m_hg_norm_w, m_ab_w_out, m_cd_w_in, m_swa_sinks, m_rg_conv_w, m_rg_conv_b, m_rg_wa, m_rg_ba, m_rg_wx, m_rg_bx, m_rg_lambda, m_cd_w_out, m_ffn_w_up, m_ffn_conv_w, m_ffn_conv_b, m_ffn_w_down, m_ln_g, m_ln_b, v_ab_w_in, v_ssd_conv_w, v_ssd_conv_b, v_ssd_dt_bias, v_ssd_a_log, v_ssd_d, v_ssd_norm_w, v_hg_lower, v_hg_norm_w, v_ab_w_out, v_cd_w_in, v_swa_sinks, v_rg_conv_w, v_rg_conv_b, v_rg_wa, v_rg_ba, v_rg_wx, v_rg_bx, v_rg_lambda, v_cd_w_out, v_ffn_w_up, v_ffn_conv_w, v_ffn_conv_b, v_ffn_w_down, v_ln_g, v_ln_b):
    given = dict(x=x, ab_w_in=ab_w_in, ssd_conv_w=ssd_conv_w, ssd_conv_b=ssd_conv_b, ssd_dt_bias=ssd_dt_bias, ssd_a_log=ssd_a_log, ssd_d=ssd_d, ssd_norm_w=ssd_norm_w, hg_lower=hg_lower, hg_norm_w=hg_norm_w, ab_w_out=ab_w_out, cd_w_in=cd_w_in, swa_sinks=swa_sinks, rg_conv_w=rg_conv_w, rg_conv_b=rg_conv_b, rg_wa=rg_wa, rg_ba=rg_ba, rg_wx=rg_wx, rg_bx=rg_bx, rg_lambda=rg_lambda, cd_w_out=cd_w_out, ffn_w_up=ffn_w_up, ffn_conv_w=ffn_conv_w, ffn_conv_b=ffn_conv_b, ffn_w_down=ffn_w_down, ln_g=ln_g, ln_b=ln_b, loss_target=loss_target, m_ab_w_in=m_ab_w_in, m_ssd_conv_w=m_ssd_conv_w, m_ssd_conv_b=m_ssd_conv_b, m_ssd_dt_bias=m_ssd_dt_bias, m_ssd_a_log=m_ssd_a_log, m_ssd_d=m_ssd_d, m_ssd_norm_w=m_ssd_norm_w, m_hg_lower=m_hg_lower, m_hg_norm_w=m_hg_norm_w, m_ab_w_out=m_ab_w_out, m_cd_w_in=m_cd_w_in, m_swa_sinks=m_swa_sinks, m_rg_conv_w=m_rg_conv_w, m_rg_conv_b=m_rg_conv_b, m_rg_wa=m_rg_wa, m_rg_ba=m_rg_ba, m_rg_wx=m_rg_wx, m_rg_bx=m_rg_bx, m_rg_lambda=m_rg_lambda, m_cd_w_out=m_cd_w_out, m_ffn_w_up=m_ffn_w_up, m_ffn_conv_w=m_ffn_conv_w, m_ffn_conv_b=m_ffn_conv_b, m_ffn_w_down=m_ffn_w_down, m_ln_g=m_ln_g, m_ln_b=m_ln_b, v_ab_w_in=v_ab_w_in, v_ssd_conv_w=v_ssd_conv_w, v_ssd_conv_b=v_ssd_conv_b, v_ssd_dt_bias=v_ssd_dt_bias, v_ssd_a_log=v_ssd_a_log, v_ssd_d=v_ssd_d, v_ssd_norm_w=v_ssd_norm_w, v_hg_lower=v_hg_lower, v_hg_norm_w=v_hg_norm_w, v_ab_w_out=v_ab_w_out, v_cd_w_in=v_cd_w_in, v_swa_sinks=v_swa_sinks, v_rg_conv_w=v_rg_conv_w, v_rg_conv_b=v_rg_conv_b, v_rg_wa=v_rg_wa, v_rg_ba=v_rg_ba, v_rg_wx=v_rg_wx, v_rg_bx=v_rg_bx, v_rg_lambda=v_rg_lambda, v_cd_w_out=v_cd_w_out, v_ffn_w_up=v_ffn_w_up, v_ffn_conv_w=v_ffn_conv_w, v_ffn_conv_b=v_ffn_conv_b, v_ffn_w_down=v_ffn_w_down, v_ln_g=v_ln_g, v_ln_b=v_ln_b)
    weights = {n: given[n] for n in TWIN_WEIGHTS}
    shared = {n: given[n] for n in SHARED_INPUTS}
    per_example = {n: given[n] for n in ['x']}
    grad_fn = _jax.value_and_grad(_loss, argnums=(0, 1))

    def one_microbatch(ex, loss_target):
        ex = dict(ex)
        diff = ex.pop(TWIN_DIFF_INPUT)
        return grad_fn(weights, diff, {**shared, **ex}, loss_target)

    if N_MICROBATCH == 1:
        loss, (grad_w, grad_x) = one_microbatch(per_example, given["loss_target"])
    else:
        def body(carry, xs):
            loss_sum, grad_sum = carry
            l_k, (gw_k, gx_k) = one_microbatch(xs[0], xs[1])
            with _jax.named_scope("update"):
                return (loss_sum + l_k, _jax.tree.map(_jnp.add, grad_sum, gw_k)), gx_k

        init = (_jnp.zeros((), _jnp.float32), _jax.tree.map(_jnp.zeros_like, weights))
        (loss, grad_w), grad_x = _jax.lax.scan(body, init, (per_example, given["loss_target"]))
    with _jax.named_scope("update"):
        delta_w, new_m, new_v = {}, {}, {}
        for n in TWIN_WEIGHTS:
            delta_w[n], new_m[n], new_v[n] = _adamw(weights[n], grad_w[n], given["m_" + n], given["v_" + n])
    return (loss, grad_x, *[grad_w[n] for n in TWIN_WEIGHTS], *[delta_w[n] for n in TWIN_WEIGHTS],
            *[new_m[n] for n in TWIN_WEIGHTS], *[new_v[n] for n in TWIN_WEIGHTS])
```

```python
import functools
import math

import numpy as np
import jax
import jax.numpy as jnp
from jax import lax
from jax.experimental import pallas as pl
from jax.experimental.pallas import tpu as pltpu

F32 = jnp.float32
BF16 = jnp.bfloat16
HI = lax.Precision.HIGHEST

D_MODEL = 1024
DEPTH = 4
SSD_HEADS = 8
SSD_D_INNER = 512
SSD_CONV_DIM = 768
SSD_CHUNK = 128
HG_CHUNK = 64
HG_HEADS = 4
SWA_BLOCK = 128
RG_WIDTH = 512
FFN_DIM = 2816
LN_EPS = 1e-5
RMS_EPS = 1e-6
MASK_VALUE = -1e9
ALPHA = (2 * DEPTH) ** 0.25
RG_C = 8.0
ADAM_LR, ADAM_B1, ADAM_B2, ADAM_EPS, ADAM_WD, ADAM_STEP = 0.001, 0.9, 0.999, 1e-08, 0.01, 10

LANES = 128
SUBLANES = 8
HALO = 16
CONV_CB = 256
VMEM_LIMIT = 56 * 1024 * 1024

AB_Z, AB_HQ, AB_HF, AB_HI, AB_HG, AB_XBC, AB_DT, AB_COLS = 0, 512, 1024, 1536, 2048, 2560, 3328, 3456
CD_Q, CD_GATE, CD_XR, CD_K, CD_V, CD_COLS = 0, 512, 1024, 1536, 1664, 1792


def _cp(sem=None):
    kw = dict(vmem_limit_bytes=VMEM_LIMIT)
    if sem is not None:
        kw["dimension_semantics"] = sem
    return pltpu.CompilerParams(**kw)


def _dot(a, b, dims=(((1,), (0,)), ((), ())), precision=None):
    return lax.dot_general(a, b, dims, precision=precision, preferred_element_type=F32)


NN = (((1,), (0,)), ((), ()))
NT = (((1,), (1,)), ((), ()))
TN = (((0,), (0,)), ((), ()))


@functools.partial(jax.custom_vjp, nondiff_argnums=(2,))
def _bdot(a, b, dims=NN):
    return _dot(a.astype(BF16), b.astype(BF16), dims)


def _bdot_fwd(a, b, dims):
    return _bdot(a, b, dims), (a, b)


def _bdot_bwd(dims, res, ct):
    a, b = res
    ab, bb, cb = a.astype(BF16), b.astype(BF16), ct.astype(BF16)
    if dims == NN:
        da, db = _dot(cb, bb, NT), _dot(ab, cb, TN)
    elif dims == NT:
        da, db = _dot(cb, bb, NN), _dot(cb, ab, TN)
    else:
        da, db = _dot(bb, cb, NT), _dot(ab, cb, NN)
    return da.astype(a.dtype), db.astype(b.dtype)


_bdot.defvjp(_bdot_fwd, _bdot_bwd)


def _hdot(a, b, dims=NN):
    return _dot(a, b, dims, precision=HI)


def _pick(n, cands):
    for c in cands:
        if n % c == 0:
            return c
    return n


def _mm(pairs, n_out, out_dtype, name, trans_b=False, tm=512):
    m = pairs[0][0].shape[0]
    tm = min(tm, m)
    tn = _pick(n_out, (512, 384, 256, 128))
    n = len(pairs)

    def kern(*refs):
        acc = None
        for i in range(n):
            p = _bdot(refs[i][...], refs[n + i][...], NT if trans_b else NN)
            acc = p if acc is None else acc + p
        refs[2 * n][...] = acc.astype(out_dtype)

    in_specs = []
    for a, acb, b, bcb, k in pairs:
        in_specs.append(pl.BlockSpec((tm, k), functools.partial(lambda i, j, c: (i, c), c=acb)))
    for a, acb, b, bcb, k in pairs:
        if trans_b:
            in_specs.append(pl.BlockSpec((tn, k), functools.partial(lambda i, j, c: (j, c), c=bcb)))
        else:
            in_specs.append(pl.BlockSpec((k, tn), functools.partial(lambda i, j, c: (c, j), c=bcb)))
    return pl.pallas_call(
        kern, name=name, grid=(m // tm, n_out // tn), in_specs=in_specs,
        out_specs=pl.BlockSpec((tm, tn), lambda i, j: (i, j)),
        out_shape=jax.ShapeDtypeStruct((m, n_out), out_dtype),
        compiler_params=_cp(("parallel", "arbitrary")),
    )(*[p[0] for p in pairs], *[p[2] for p in pairs])


def _mm_tn(a, a_cb, ka, g, g_cb, ng, name, tm=512):
    m = a.shape[0]
    tm = min(tm, m)
    tk = _pick(ka, (1024, 1408, 512, 256, 128))
    tn = _pick(ng, (512, 384, 256, 128))
    nm = m // tm

    def kern(a_ref, g_ref, o_ref):
        p = _bdot(a_ref[...], g_ref[...], TN)

        @pl.when(pl.program_id(2) == 0)
        def _():
            o_ref[...] = p

        @pl.when(pl.program_id(2) > 0)
        def _():
            o_ref[...] += p

    ak, gk = ka // tk, ng // tn
    return pl.pallas_call(
        kern, name=name, grid=(ak, gk, nm),
        in_specs=[pl.BlockSpec((tm, tk), lambda i, j, r: (r, a_cb * ak + i)),
                  pl.BlockSpec((tm, tn), lambda i, j, r: (r, g_cb * gk + j))],
        out_specs=pl.BlockSpec((tk, tn), lambda i, j, r: (i, j)),
        out_shape=jax.ShapeDtypeStruct((ka, ng), F32),
        compiler_params=_cp(("parallel", "parallel", "arbitrary")),
    )(a, g)


def _ln_fwd(x, m, g, b, name, tm=256):
    s, d = x.shape
    tm = min(tm, s)

    def kern(x_ref, m_ref, g_ref, b_ref, y_ref, yb_ref, r_ref):
        r = ALPHA * x_ref[...] + m_ref[...]
        mu = jnp.mean(r, -1, keepdims=True)
        xc = r - mu
        var = jnp.mean(xc * xc, -1, keepdims=True)
        y = xc * lax.rsqrt(var + LN_EPS) * g_ref[...] + b_ref[...]
        y_ref[...] = y
        yb_ref[...] = y.astype(BF16)
        r_ref[...] = r

    row = pl.BlockSpec((tm, d), lambda i: (i, 0))
    vec = pl.BlockSpec((1, d), lambda i: (0, 0))
    return pl.pallas_call(
        kern, name=name, grid=(s // tm,), in_specs=[row, row, vec, vec], out_specs=[row, row, row],
        out_shape=[jax.ShapeDtypeStruct((s, d), F32), jax.ShapeDtypeStruct((s, d), BF16), jax.ShapeDtypeStruct((s, d), F32)],
        compiler_params=_cp(("parallel",)),
    )(x, m, g.reshape(1, d), b.reshape(1, d))


def _ln_bwd(dya, dyb, ca, cb, r, g, name, tm=256):
    s, d = r.shape
    tm = min(tm, s)

    def kern(a_ref, b_ref, r_ref, g_ref, dr_ref, dg_ref, db_ref):
        dy = ca * a_ref[...] + cb * b_ref[...]
        rr = r_ref[...]
        mu = jnp.mean(rr, -1, keepdims=True)
        xc = rr - mu
        var = jnp.mean(xc * xc, -1, keepdims=True)
        rstd = lax.rsqrt(var + LN_EPS)
        xhat = xc * rstd
        dxh = dy * g_ref[...]
        dr_ref[...] = rstd * (dxh - jnp.mean(dxh, -1, keepdims=True) - xhat * jnp.mean(dxh * xhat, -1, keepdims=True))
        dg = jnp.sum(dy * xhat, 0, keepdims=True)
        db = jnp.sum(dy, 0, keepdims=True)

        @pl.when(pl.program_id(0) == 0)
        def _():
            dg_ref[...] = dg
            db_ref[...] = db

        @pl.when(pl.program_id(0) > 0)
        def _():
            dg_ref[...] += dg
            db_ref[...] += db

    row = pl.BlockSpec((tm, d), lambda i: (i, 0))
    vec = pl.BlockSpec((1, d), lambda i: (0, 0))
    dr, dg, db = pl.pallas_call(
        kern, name=name, grid=(s // tm,), in_specs=[row, row, row, vec], out_specs=[row, vec, vec],
        out_shape=[jax.ShapeDtypeStruct((s, d), F32), jax.ShapeDtypeStruct((1, d), F32), jax.ShapeDtypeStruct((1, d), F32)],
        compiler_params=_cp(("arbitrary",)),
    )(dya, dyb, r, g.reshape(1, d))
    return dr, dg[0], db[0]


def _sqerr(y, t, name, tm=256):
    s, d = y.shape
    tm = min(tm, s)

    def kern(y_ref, t_ref, o_ref):
        e = y_ref[...] - t_ref[...]
        p = jnp.sum(e * e, 0, keepdims=True)

        @pl.when(pl.program_id(0) == 0)
        def _():
            o_ref[...] = p

        @pl.when(pl.program_id(0) > 0)
        def _():
            o_ref[...] += p

    row = pl.BlockSpec((tm, d), lambda i: (i, 0))
    return pl.pallas_call(
        kern, name=name, grid=(s // tm,), in_specs=[row, row], out_specs=pl.BlockSpec((1, d), lambda i: (0, 0)),
        out_shape=jax.ShapeDtypeStruct((1, d), F32), compiler_params=_cp(("arbitrary",)),
    )(y, t)


def _axpby(a, b, ca, cb, name, tm=256):
    s, d = a.shape
    tm = min(tm, s)

    def kern(a_ref, b_ref, o_ref):
        o_ref[...] = ca * a_ref[...] + cb * b_ref[...]

    row = pl.BlockSpec((tm, d), lambda i: (i, 0))
    return pl.pallas_call(
        kern, name=name, grid=(s // tm,), in_specs=[row, row], out_specs=row,
        out_shape=jax.ShapeDtypeStruct((s, d), F32), compiler_params=_cp(("parallel",)),
    )(a, b)


def _shift_down(xx, s, n):
    if s == 0:
        return xx[HALO:HALO + n]
    return pltpu.roll(xx, s, axis=0)[HALO:HALO + n]


def _shift_up(yy, s, n):
    if s == 0:
        return yy[0:n]
    return pltpu.roll(yy, yy.shape[0] - s, axis=0)[0:n]


def _prev_halo(tm):
    return lambda i: jnp.maximum(i * (tm // HALO) - 1, 0)


def _dwconv_fwd(x, x_cb0, c, w8, b, k, name, tm=512):
    s = x.shape[0]
    tm = min(tm, s)
    ph = _prev_halo(tm)

    def kern(x_ref, h_ref, w_ref, b_ref, y_ref):
        halo = jnp.where(pl.program_id(0) == 0, 0.0, h_ref[...].astype(F32))
        xx = jnp.concatenate([halo, x_ref[...].astype(F32)], axis=0)
        w = w_ref[...]
        acc = b_ref[...] + w[k - 1:k] * xx[HALO:]
        for j in range(k - 1):
            acc = acc + w[j:j + 1] * _shift_down(xx, k - 1 - j, tm)
        y_ref[...] = acc

    return pl.pallas_call(
        kern, name=name, grid=(s // tm, c // CONV_CB),
        in_specs=[pl.BlockSpec((tm, CONV_CB), lambda i, j: (i, x_cb0 + j)),
                  pl.BlockSpec((HALO, CONV_CB), lambda i, j: (ph(i), x_cb0 + j)),
                  pl.BlockSpec((SUBLANES, CONV_CB), lambda i, j: (0, j)),
                  pl.BlockSpec((1, CONV_CB), lambda i, j: (0, j))],
        out_specs=pl.BlockSpec((tm, CONV_CB), lambda i, j: (i, j)),
        out_shape=jax.ShapeDtypeStruct((s, c), F32), compiler_params=_cp(("parallel", "parallel")),
    )(x, x, w8, b.reshape(1, c))


def _dwconv_bwd(dy, x, x_cb0, c, w8, k, name, tm=512):
    s = x.shape[0]
    tm = min(tm, s)
    nt = s // tm
    ph = _prev_halo(tm)
    nh = lambda i: jnp.minimum((i + 1) * (tm // HALO), s // HALO - 1)

    def kern(dy_ref, dyn_ref, x_ref, h_ref, w_ref, dx_ref, dw_ref, db_ref):
        i = pl.program_id(1)
        halo = jnp.where(i == 0, 0.0, h_ref[...].astype(F32))
        xx = jnp.concatenate([halo, x_ref[...].astype(F32)], axis=0)
        dyt = dy_ref[...]
        nxt = jnp.where(i == nt - 1, 0.0, dyn_ref[...])
        dyy = jnp.concatenate([dyt, nxt], axis=0)
        w = w_ref[...]
        dx = w[k - 1:k] * dyt
        rows = [jnp.sum(dyt * _shift_down(xx, k - 1 - j, tm), 0, keepdims=True) for j in range(k)]
        for j in range(k - 1):
            dx = dx + w[j:j + 1] * _shift_up(dyy, k - 1 - j, tm)
        dx_ref[...] = dx.astype(BF16)
        dw = jnp.concatenate(rows + [jnp.zeros((SUBLANES - k, CONV_CB), F32)], axis=0)
        db = jnp.sum(dyt, 0, keepdims=True)

        @pl.when(i == 0)
        def _():
            dw_ref[...] = dw
            db_ref[...] = db

        @pl.when(i > 0)
        def _():
            dw_ref[...] += dw
            db_ref[...] += db

    dx, dw, db = pl.pallas_call(
        kern, name=name, grid=(c // CONV_CB, nt),
        in_specs=[pl.BlockSpec((tm, CONV_CB), lambda j, i: (i, j)),
                  pl.BlockSpec((HALO, CONV_CB), lambda j, i: (nh(i), j)),
                  pl.BlockSpec((tm, CONV_CB), lambda j, i: (i, x_cb0 + j)),
                  pl.BlockSpec((HALO, CONV_CB), lambda j, i: (ph(i), x_cb0 + j)),
                  pl.BlockSpec((SUBLANES, CONV_CB), lambda j, i: (0, j))],
        out_specs=[pl.BlockSpec((tm, CONV_CB), lambda j, i: (i, j)),
                   pl.BlockSpec((SUBLANES, CONV_CB), lambda j, i: (0, j)),
                   pl.BlockSpec((1, CONV_CB), lambda j, i: (0, j))],
        out_shape=[jax.ShapeDtypeStruct((s, c), BF16), jax.ShapeDtypeStruct((SUBLANES, c), F32),
                   jax.ShapeDtypeStruct((1, c), F32)],
        compiler_params=_cp(("parallel", "arbitrary")),
    )(dy, dy, x, x, w8)
    return dx, dw, db[0]


def _silu(x):
    return x * jax.nn.sigmoid(x)


def _ffn_gate_fwd(hu, w8, b, name, tm=512):
    s = hu.shape[0]
    tm = min(tm, s)
    nb = FFN_DIM // CONV_CB
    ph = _prev_halo(tm)
    k = 3

    def conv(x_ref, h_ref, w_ref, b_ref):
        halo = jnp.where(pl.program_id(0) == 0, 0.0, h_ref[...].astype(F32))
        xx = jnp.concatenate([halo, x_ref[...].astype(F32)], axis=0)
        w = w_ref[...]
        acc = b_ref[...] + w[k - 1:k] * xx[HALO:]
        for j in range(k - 1):
            acc = acc + w[j:j + 1] * _shift_down(xx, k - 1 - j, tm)
        return acc

    def kern(g_ref, gh_ref, u_ref, uh_ref, wg_ref, wu_ref, bg_ref, bu_ref, a_ref):
        g = conv(g_ref, gh_ref, wg_ref, bg_ref)
        u = conv(u_ref, uh_ref, wu_ref, bu_ref)
        a_ref[...] = (_silu(g) * u).astype(BF16)

    main = lambda off: pl.BlockSpec((tm, CONV_CB), lambda i, j: (i, off + j))
    halo = lambda off: pl.BlockSpec((HALO, CONV_CB), lambda i, j: (ph(i), off + j))
    wsp = lambda off: pl.BlockSpec((SUBLANES, CONV_CB), lambda i, j: (0, off + j))
    bsp = lambda off: pl.BlockSpec((1, CONV_CB), lambda i, j: (0, off + j))
    b2 = b.reshape(1, 2 * FFN_DIM)
    return pl.pallas_call(
        kern, name=name, grid=(s // tm, nb),
        in_specs=[main(0), halo(0), main(nb), halo(nb), wsp(0), wsp(nb), bsp(0), bsp(nb)],
        out_specs=pl.BlockSpec((tm, CONV_CB), lambda i, j: (i, j)),
        out_shape=jax.ShapeDtypeStruct((s, FFN_DIM), BF16), compiler_params=_cp(("parallel", "parallel")),
    )(hu, hu, hu, hu, w8, w8, b2, b2)


def _ffn_gate_bwd(hu, da, w8, b, name, tm=512):
    s = hu.shape[0]
    tm = min(tm, s)
    nt = s // tm
    nb = FFN_DIM // CONV_CB
    ph = _prev_halo(tm)
    nh = lambda i: jnp.minimum((i + 1) * (tm // HALO), s // HALO - 1)
    k = 3
    ne = tm + HALO

    def kern(g_ref, gp_ref, gn_ref, u_ref, up_ref, un_ref, da_ref, dan_ref, wg_ref, wu_ref, bg_ref, bu_ref,
             dg_ref, du_ref, dwg_ref, dwu_ref, dbg_ref, dbu_ref):
        i = pl.program_id(1)

        def ext(x_ref, p_ref, n_ref):
            halo = jnp.where(i == 0, 0.0, p_ref[...].astype(F32))
            return jnp.concatenate([halo, x_ref[...].astype(F32), n_ref[...].astype(F32)], axis=0)

        def conv(xx, w, bb):
            acc = bb + w[k - 1:k] * xx[HALO:]
            for j in range(k - 1):
                acc = acc + w[j:j + 1] * _shift_down(xx, k - 1 - j, ne)
            return acc

        gx, ux = ext(g_ref, gp_ref, gn_ref), ext(u_ref, up_ref, un_ref)
        wg, wu = wg_ref[...], wu_ref[...]
        g = conv(gx, wg, bg_ref[...])
        u = conv(ux, wu, bu_ref[...])
        dae = jnp.concatenate([da_ref[...].astype(F32), jnp.where(i == nt - 1, 0.0, dan_ref[...].astype(F32))], axis=0)
        sg = jax.nn.sigmoid(g)
        dhg = dae * u * (sg * (1.0 + g * (1.0 - sg)))
        dhu = dae * (g * sg)

        def back(dh, xx, w):
            dx = w[k - 1:k] * dh[0:tm]
            for j in range(k - 1):
                dx = dx + w[j:j + 1] * _shift_up(dh, k - 1 - j, tm)
            rows = [jnp.sum(dh[0:tm] * _shift_down(xx, k - 1 - j, tm), 0, keepdims=True) for j in range(k)]
            dw = jnp.concatenate(rows + [jnp.zeros((SUBLANES - k, CONV_CB), F32)], axis=0)
            return dx, dw, jnp.sum(dh[0:tm], 0, keepdims=True)

        dxg, dwg, dbg = back(dhg, gx, wg)
        dxu, dwu, dbu = back(dhu, ux, wu)
        dg_ref[...] = dxg.astype(BF16)
        du_ref[...] = dxu.astype(BF16)

        @pl.when(i == 0)
        def _():
            dwg_ref[...] = dwg
            dwu_ref[...] = dwu
            dbg_ref[...] = dbg
            dbu_ref[...] = dbu

        @pl.when(i > 0)
        def _():
            dwg_ref[...] += dwg
            dwu_ref[...] += dwu
            dbg_ref[...] += dbg
            dbu_ref[...] += dbu

    main = lambda off: pl.BlockSpec((tm, CONV_CB), lambda j, i: (i, off + j))
    prev = lambda off: pl.BlockSpec((HALO, CONV_CB), lambda j, i: (ph(i), off + j))
    nxt = lambda off: pl.BlockSpec((HALO, CONV_CB), lambda j, i: (nh(i), off + j))
    wsp = lambda off: pl.BlockSpec((SUBLANES, CONV_CB), lambda j, i: (0, off + j))
    bsp = lambda off: pl.BlockSpec((1, CONV_CB), lambda j, i: (0, off + j))
    b2 = b.reshape(1, 2 * FFN_DIM)
    outs = pl.pallas_call(
        kern, name=name, grid=(nb, nt),
        in_specs=[main(0), prev(0), nxt(0), main(nb), prev(nb), nxt(nb), main(0), nxt(0), wsp(0), wsp(nb), bsp(0), bsp(nb)],
        out_specs=[main(0), main(0), wsp(0), wsp(0), bsp(0), bsp(0)],
        out_shape=[jax.ShapeDtypeStruct((s, FFN_DIM), BF16), jax.ShapeDtypeStruct((s, FFN_DIM), BF16),
                   jax.ShapeDtypeStruct((SUBLANES, FFN_DIM), F32), jax.ShapeDtypeStruct((SUBLANES, FFN_DIM), F32),
                   jax.ShapeDtypeStruct((1, FFN_DIM), F32), jax.ShapeDtypeStruct((1, FFN_DIM), F32)],
        compiler_params=_cp(("parallel", "arbitrary")),
    )(hu, hu, hu, hu, hu, hu, da, da, w8, w8, b2, b2)
    dg, du, dwg, dwu, dbg, dbu = outs
    return dg, du, jnp.concatenate([dwg, dwu], axis=1), jnp.concatenate([dbg[0], dbu[0]])


def _iota(shape, dim):
    return lax.broadcasted_iota(jnp.int32, shape, dim)


def _in_range(idx, start, size):
    return jnp.logical_and(idx >= start, idx < start + size)


def _expand8(v, e):
    return jnp.sum(_hdot(jnp.broadcast_to(v, (SUBLANES, v.shape[1])), e), 0, keepdims=True) * (1.0 / SUBLANES)


def _ssd_consts():
    tri = np.tril(np.ones((SSD_CHUNK, SSD_CHUNK), np.float32))
    e = np.zeros((LANES, SSD_D_INNER), np.float32)
    for h in range(SSD_HEADS):
        e[h, 64 * h:64 * h + 64] = 1.0
    bd = np.zeros((LANES, SSD_D_INNER), np.float32)
    bd[:64, :256] = 1.0
    bd[64:, 256:] = 1.0
    return jnp.asarray(tri), jnp.asarray(e), jnp.asarray(bd)


def _ssd_chunk(tri, e, bd, z, xs_p, bm_p, cm_p, dtr, st, dtb, alog, dsk, nw):
    n = z.shape[0]
    lane128, lane512 = _iota((1, LANES), 1), _iota((1, SSD_D_INNER), 1)
    sub128 = _iota((LANES, 1), 0)
    tril = _iota((n, 1), 0) >= _iota((1, n), 1)
    xs, bm, cm = _silu(xs_p), _silu(bm_p), _silu(cm_p)
    dt = jax.nn.softplus(dtr + dtb)
    da = dt * (-jnp.exp(alog))
    acs = _hdot(tri, da)
    acst = _hdot(da, tri, (((0,), (1,)), ((), ())))
    xc = xs * _hdot(dt, e)
    tot = jnp.sum(da, 0, keepdims=True)
    y = _bdot(cm, st) * _hdot(jnp.exp(acs), e)
    st_new = st * _expand8(jnp.exp(tot), e) + bd * _bdot(bm, xc * _hdot(jnp.exp(tot - acs), e), TN)
    for g in range(2):
        cb = _bdot(jnp.where(_in_range(lane128, 64 * g, 64), cm, 0.0), bm, NT)
        for h in range(4 * g, 4 * g + 4):
            col = jnp.sum(jnp.where(lane128 == h, acs, 0.0), 1, keepdims=True)
            row = jnp.sum(jnp.where(sub128 == h, acst, 0.0), 0, keepdims=True)
            dm = jnp.where(tril, jnp.exp(jnp.minimum(col - row, 0.0)), 0.0)
            y = y + _bdot(cb * dm, jnp.where(_in_range(lane512, 64 * h, 64), xc, 0.0))
    y = (y + _expand8(dsk, e) * xs) * _silu(z)
    ysq = y * y
    g0 = lane512 < 256
    ms0 = jnp.sum(jnp.where(g0, ysq, 0.0), 1, keepdims=True) * (1.0 / 256)
    ms1 = jnp.sum(jnp.where(g0, 0.0, ysq), 1, keepdims=True) * (1.0 / 256)
    rs = jnp.where(g0, lax.rsqrt(ms0 + RMS_EPS), lax.rsqrt(ms1 + RMS_EPS))
    return y * rs * nw, st_new


def _ssd_specs(rev, nc):
    ci = (lambda i: nc - 1 - i) if rev else (lambda i: i)
    n = SSD_CHUNK
    act = [pl.BlockSpec((n, 512), lambda i: (ci(i), AB_Z // 512)),
           pl.BlockSpec((n, 512), lambda i: (ci(i), 0)),
           pl.BlockSpec((n, LANES), lambda i: (ci(i), 4)),
           pl.BlockSpec((n, LANES), lambda i: (ci(i), 5)),
           pl.BlockSpec((n, LANES), lambda i: (ci(i), AB_DT // LANES))]
    const = [pl.BlockSpec((n, n), lambda i: (0, 0)), pl.BlockSpec((LANES, 512), lambda i: (0, 0)),
             pl.BlockSpec((LANES, 512), lambda i: (0, 0))]
    par = [pl.BlockSpec((1, LANES), lambda i: (0, 0))] * 3 + [pl.BlockSpec((1, 512), lambda i: (0, 0))]
    return ci, act, const, par


def _ssd_fwd(h, xbcc, dtb, alog, dsk, nw, name):
    s = h.shape[0]
    nc = s // SSD_CHUNK
    ci, act, const, par = _ssd_specs(False, nc)

    def kern(z_ref, xs_ref, bm_ref, cm_ref, dt_ref, tri_ref, e_ref, bd_ref, dtb_ref, al_ref, dsk_ref, nw_ref,
             y_ref, sts_ref, st):
        @pl.when(pl.program_id(0) == 0)
        def _():
            st[...] = jnp.zeros_like(st)

        sts_ref[0] = st[...]
        y, stn = _ssd_chunk(tri_ref[...], e_ref[...], bd_ref[...], z_ref[...], xs_ref[...], bm_ref[...], cm_ref[...],
                            dt_ref[...], st[...], dtb_ref[...], al_ref[...], dsk_ref[...], nw_ref[...])
        y_ref[...] = y.astype(BF16)
        st[...] = stn

    return pl.pallas_call(
        kern, name=name, grid=(nc,), in_specs=act + const + par,
        out_specs=[pl.BlockSpec((SSD_CHUNK, 512), lambda i: (i, 0)), pl.BlockSpec((1, LANES, 512), lambda i: (i, 0, 0))],
        out_shape=[jax.ShapeDtypeStruct((s, 512), BF16), jax.ShapeDtypeStruct((nc, LANES, 512), F32)],
        scratch_shapes=[pltpu.VMEM((LANES, 512), F32)], compiler_params=_cp(("arbitrary",)),
    )(h, xbcc, xbcc, xbcc, h, *_ssd_consts(), dtb, alog, dsk, nw)


def _acc(ref, val, first):
    @pl.when(first)
    def _():
        ref[...] = val

    @pl.when(jnp.logical_not(first))
    def _():
        ref[...] += val


def _ssd_bwd(dy, dy_cb, h, xbcc, sts, dtb, alog, dsk, nw, name):
    s = h.shape[0]
    nc = s // SSD_CHUNK
    n = SSD_CHUNK
    ci, act, const, par = _ssd_specs(True, nc)

    def kern(dy_ref, sts_ref, z_ref, xs_ref, bm_ref, cm_ref, dt_ref, tri_ref, e_ref, bd_ref, dtb_ref, al_ref, dsk_ref,
             nw_ref, dz_ref, dx_ref, ddt_ref, ddtb_ref, dal_ref, ddsk_ref, dnw_ref, dst):
        first = pl.program_id(0) == 0

        @pl.when(first)
        def _():
            dst[...] = jnp.zeros_like(dst)

        f = functools.partial(_ssd_chunk, tri_ref[...], e_ref[...], bd_ref[...])
        _, pull = jax.vjp(f, z_ref[...], xs_ref[...], bm_ref[...], cm_ref[...], dt_ref[...], sts_ref[0],
                          dtb_ref[...], al_ref[...], dsk_ref[...], nw_ref[...])
        dz, dxs, dbm, dcm, ddt, dsti, ddtb, dal, ddsk, dnw = pull((dy_ref[...], dst[...]))
        dz_ref[...] = dz.astype(BF16)
        dx_ref[:, 0:512] = dxs
        dx_ref[:, 512:640] = dbm
        dx_ref[:, 640:768] = dcm
        ddt_ref[...] = ddt.astype(BF16)
        dst[...] = dsti
        _acc(ddtb_ref, ddtb, first)
        _acc(dal_ref, dal, first)
        _acc(ddsk_ref, ddsk, first)
        _acc(dnw_ref, dnw, first)

    vec = pl.BlockSpec((1, LANES), lambda i: (0, 0))
    return pl.pallas_call(
        kern, name=name, grid=(nc,),
        in_specs=[pl.BlockSpec((n, 512), lambda i: (ci(i), dy_cb)), pl.BlockSpec((1, LANES, 512), lambda i: (ci(i), 0, 0))]
        + act + const + par,
        out_specs=[pl.BlockSpec((n, 512), lambda i: (ci(i), 0)), pl.BlockSpec((n, SSD_CONV_DIM), lambda i: (ci(i), 0)),
                   pl.BlockSpec((n, LANES), lambda i: (ci(i), 0)), vec, vec, vec, pl.BlockSpec((1, 512), lambda i: (0, 0))],
        out_shape=[jax.ShapeDtypeStruct((s, 512), BF16), jax.ShapeDtypeStruct((s, SSD_CONV_DIM), F32),
                   jax.ShapeDtypeStruct((s, LANES), BF16), jax.ShapeDtypeStruct((1, LANES), F32),
                   jax.ShapeDtypeStruct((1, LANES), F32), jax.ShapeDtypeStruct((1, LANES), F32),
                   jax.ShapeDtypeStruct((1, 512), F32)],
        scratch_shapes=[pltpu.VMEM((LANES, 512), F32)], compiler_params=_cp(("arbitrary",)),
    )(dy, sts, h, xbcc, xbcc, xbcc, h, *_ssd_consts(), dtb, alog, dsk, nw)


HG_LEVELS = 6


def _hg_consts():
    n = HG_CHUNK
    t = np.arange(n)
    mats = [np.tril(np.ones((n, n), np.float32))]
    dq, dk, mk = [], [], []
    for lv in range(HG_LEVELS):
        b = 1 << lv
        start = (t // b) * b
        end = start + b - 1
        dq.append(((t[None, :] >= start[:, None]) & (t[None, :] <= t[:, None])).astype(np.float32))
        dk.append(((t[None, :] > t[:, None]) & (t[None, :] <= end[:, None])).astype(np.float32))
        mk.append((((t // b) % 2 == 1)[:, None] & ((t // b)[None, :] == (t // b)[:, None] - 1)).astype(np.float32))
    return jnp.asarray(np.stack(mats + dq + dk + mk))


def _hg_head(c, hq, hf, hi, hg, st, lb, nw):
    q = _silu(hq)
    g = jnp.log(lb + (1.0 - lb) * jax.nn.sigmoid(hf))
    k = (1.0 - lb) * jax.nn.sigmoid(-hf)
    bc = _hdot(c[0], g)
    att = None
    for lv in range(HG_LEVELS):
        qt = q * jnp.exp(_hdot(c[1 + lv], g))
        kt = k * jnp.exp(_hdot(c[1 + HG_LEVELS + lv], g))
        t = c[1 + 2 * HG_LEVELS + lv] * _bdot(qt, kt, NT)
        att = t if att is None else att + t
    o = _bdot(att, hi) + jnp.sum(q * k, 1, keepdims=True) * hi + _bdot(q * jnp.exp(bc), st, NT)
    tot = jnp.sum(g, 0, keepdims=True)
    st_new = st * jnp.exp(tot) + _bdot(hi, k * jnp.exp(tot - bc), TN)
    on = o * lax.rsqrt(jnp.mean(o * o, 1, keepdims=True) + RMS_EPS) * nw * _silu(hg)
    return on, st_new


def _hg_specs(rev, nc):
    ci = (lambda i: nc - 1 - i) if rev else (lambda i: i)
    n = HG_CHUNK
    act = [pl.BlockSpec((n, 512), functools.partial(lambda i, c: (ci(i), c), c=c))
           for c in (AB_HQ // 512, AB_HF // 512, AB_HI // 512, AB_HG // 512)]
    other = [pl.BlockSpec((1 + 3 * HG_LEVELS, n, n), lambda i: (0, 0, 0)), pl.BlockSpec((1, 512), lambda i: (0, 0)),
             pl.BlockSpec((1, LANES), lambda i: (0, 0))]
    return ci, act, other


def _hg_fwd(h, lb, nw, name):
    s = h.shape[0]
    nc = s // HG_CHUNK
    ci, act, other = _hg_specs(False, nc)

    def kern(q_ref, f_ref, i_ref, g_ref, c_ref, lb_ref, nw_ref, y_ref, sts_ref, st):
        @pl.when(pl.program_id(0) == 0)
        def _():
            st[...] = jnp.zeros_like(st)

        sts_ref[0] = st[...]
        c = [c_ref[j] for j in range(1 + 3 * HG_LEVELS)]
        for hd in range(HG_HEADS):
            ln = slice(LANES * hd, LANES * (hd + 1))
            on, stn = _hg_head(c, q_ref[:, ln], f_ref[:, ln], i_ref[:, ln], g_ref[:, ln], st[ln, :], lb_ref[:, ln], nw_ref[...])
            y_ref[:, ln] = on.astype(BF16)
            st[ln, :] = stn

    return pl.pallas_call(
        kern, name=name, grid=(nc,), in_specs=act + other,
        out_specs=[pl.BlockSpec((HG_CHUNK, 512), lambda i: (i, 0)), pl.BlockSpec((1, 512, LANES), lambda i: (i, 0, 0))],
        out_shape=[jax.ShapeDtypeStruct((s, 512), BF16), jax.ShapeDtypeStruct((nc, 512, LANES), F32)],
        scratch_shapes=[pltpu.VMEM((512, LANES), F32)], compiler_params=_cp(("arbitrary",)),
    )(h, h, h, h, _hg_consts(), lb, nw)


def _hg_bwd(dy, dy_cb, h, sts, lb, nw, name):
    s = h.shape[0]
    nc = s // HG_CHUNK
    n = HG_CHUNK
    ci, act, other = _hg_specs(True, nc)

    def kern(dy_ref, sts_ref, q_ref, f_ref, i_ref, g_ref, c_ref, lb_ref, nw_ref,
             dq_ref, df_ref, di_ref, dg_ref, dlb_ref, dnw_ref, dst):
        first = pl.program_id(0) == 0

        @pl.when(first)
        def _():
            dst[...] = jnp.zeros_like(dst)
            dnw_ref[...] = jnp.zeros_like(dnw_ref)

        c = [c_ref[j] for j in range(1 + 3 * HG_LEVELS)]
        for hd in range(HG_HEADS):
            ln = slice(LANES * hd, LANES * (hd + 1))
            _, pull = jax.vjp(functools.partial(_hg_head, c), q_ref[:, ln], f_ref[:, ln], i_ref[:, ln], g_ref[:, ln],
                              sts_ref[0, ln, :], lb_ref[:, ln], nw_ref[...])
            dq, df, di, dg, dsti, dlb, dnw = pull((dy_ref[:, ln], dst[ln, :]))
            dq_ref[:, ln] = dq.astype(BF16)
            df_ref[:, ln] = df.astype(BF16)
            di_ref[:, ln] = di.astype(BF16)
            dg_ref[:, ln] = dg.astype(BF16)
            dst[ln, :] = dsti
            dnw_ref[...] += dnw

            @pl.when(first)
            def _():
                dlb_ref[:, ln] = dlb

            @pl.when(jnp.logical_not(first))
            def _():
                dlb_ref[:, ln] += dlb

    blk = pl.BlockSpec((n, 512), lambda i: (ci(i), 0))
    return pl.pallas_call(
        kern, name=name, grid=(nc,),
        in_specs=[pl.BlockSpec((n, 512), lambda i: (ci(i), dy_cb)), pl.BlockSpec((1, 512, LANES), lambda i: (ci(i), 0, 0))]
        + act + other,
        out_specs=[blk, blk, blk, blk, pl.BlockSpec((1, 512), lambda i: (0, 0)), pl.BlockSpec((1, LANES), lambda i: (0, 0))],
        out_shape=[jax.ShapeDtypeStruct((s, 512), BF16)] * 4 + [jax.ShapeDtypeStruct((1, 512), F32),
                                                                jax.ShapeDtypeStruct((1, LANES), F32)],
        scratch_shapes=[pltpu.VMEM((512, LANES), F32)], compiler_params=_cp(("arbitrary",)),
    )(dy, sts, h, h, h, h, _hg_consts(), lb, nw)


def _swa_pair(pair, notfirst, qp, k0, v0, k1, v1, srow):
    n = qp.shape[0]
    rows, cols = _iota((n, 1), 0), _iota((1, n), 1)
    own = rows >= cols
    prev = jnp.logical_and(cols > rows, notfirst)
    lane = _iota((1, LANES), 1)
    out = None
    for e in range(2):
        lm = _in_range(lane, 64 * e, 64)
        qm = jnp.where(lm, qp, 0.0)
        s0 = jnp.where(prev, _bdot(qm, k0, NT) * 0.125, MASK_VALUE)
        s1 = jnp.where(own, _bdot(qm, k1, NT) * 0.125, MASK_VALUE)
        sk = jnp.sum(jnp.where(lane == pair + 4 * e, srow, 0.0), 1, keepdims=True)
        m = lax.stop_gradient(jnp.maximum(jnp.maximum(jnp.max(s0, 1, keepdims=True), jnp.max(s1, 1, keepdims=True)), sk))
        p0, p1 = jnp.exp(s0 - m), jnp.exp(s1 - m)
        inv = 1.0 / (jnp.sum(p0, 1, keepdims=True) + jnp.sum(p1, 1, keepdims=True) + jnp.exp(sk - m))
        t = _bdot(p0 * inv, jnp.where(lm, v0, 0.0)) + _bdot(p1 * inv, jnp.where(lm, v1, 0.0))
        out = t if out is None else out + t
    return out


def _swa_specs():
    n = SWA_BLOCK
    pv = lambda i: jnp.maximum(i - 1, 0)
    return [pl.BlockSpec((n, 512), lambda i: (i, CD_Q // 512)),
            pl.BlockSpec((n, LANES), lambda i: (pv(i), CD_K // LANES)), pl.BlockSpec((n, LANES), lambda i: (pv(i), CD_V // LANES)),
            pl.BlockSpec((n, LANES), lambda i: (i, CD_K // LANES)), pl.BlockSpec((n, LANES), lambda i: (i, CD_V // LANES)),
            pl.BlockSpec((1, LANES), lambda i: (0, 0))]


def _swa_fwd(hc, srow, name):
    s = hc.shape[0]
    n = SWA_BLOCK

    def kern(q_ref, k0_ref, v0_ref, k1_ref, v1_ref, s_ref, y_ref):
        notfirst = pl.program_id(0) > 0
        for p in range(4):
            ln = slice(LANES * p, LANES * (p + 1))
            y_ref[:, ln] = _swa_pair(p, notfirst, q_ref[:, ln], k0_ref[...], v0_ref[...], k1_ref[...], v1_ref[...],
                                     s_ref[...]).astype(BF16)

    return pl.pallas_call(
        kern, name=name, grid=(s // n,), in_specs=_swa_specs(), out_specs=pl.BlockSpec((n, 512), lambda i: (i, 0)),
        out_shape=jax.ShapeDtypeStruct((s, 512), BF16), compiler_params=_cp(("parallel",)),
    )(hc, hc, hc, hc, hc, srow)


def _swa_bwd(dy, dy_cb, hc, srow, name):
    s = hc.shape[0]
    n = SWA_BLOCK

    def kern(dy_ref, q_ref, k0_ref, v0_ref, k1_ref, v1_ref, s_ref, dq_ref, dk_ref, dv_ref, ds_ref):
        i = pl.program_id(0)
        notfirst = i > 0

        @pl.when(i == 0)
        def _():
            dk_ref[...] = jnp.zeros_like(dk_ref)
            dv_ref[...] = jnp.zeros_like(dv_ref)
            ds_ref[...] = jnp.zeros_like(ds_ref)

        r1 = pl.ds(pl.multiple_of(i * n, n), n)
        r0 = pl.ds(pl.multiple_of(jnp.maximum(i - 1, 0) * n, n), n)
        for p in range(4):
            ln = slice(LANES * p, LANES * (p + 1))
            _, pull = jax.vjp(functools.partial(_swa_pair, p, notfirst), q_ref[:, ln], k0_ref[...], v0_ref[...],
                              k1_ref[...], v1_ref[...], s_ref[...])
            dq, dk0, dv0, dk1, dv1, dsr = pull(dy_ref[:, ln])
            dq_ref[:, ln] = dq.astype(BF16)
            dk_ref[r0, :] += dk0
            dv_ref[r0, :] += dv0
            dk_ref[r1, :] += dk1
            dv_ref[r1, :] += dv1
            ds_ref[...] += dsr

    full = pl.BlockSpec((s, LANES), lambda i: (0, 0))
    return pl.pallas_call(
        kern, name=name, grid=(s // n,), in_specs=[pl.BlockSpec((n, 512), lambda i: (i, dy_cb))] + _swa_specs(),
        out_specs=[pl.BlockSpec((n, 512), lambda i: (i, 0)), full, full, pl.BlockSpec((1, LANES), lambda i: (0, 0))],
        out_shape=[jax.ShapeDtypeStruct((s, 512), BF16), jax.ShapeDtypeStruct((s, LANES), F32),
                   jax.ShapeDtypeStruct((s, LANES), F32), jax.ShapeDtypeStruct((1, LANES), F32)],
        compiler_params=_cp(("arbitrary",)),
    )(dy, hc, hc, hc, hc, hc, srow)


RG_TILE = 256


def _neg_expm1(x):
    ser = x * (1 + x / 2 * (1 + x / 3 * (1 + x / 4 * (1 + x / 5 * (1 + x / 6 * (1 + x / 7 * (1 + x / 8)))))))
    return jnp.where(x > -0.25, -ser, 1.0 - jnp.exp(x))


def _rg_gates(xc, wa, wx, ba, bx, lam):
    r = jax.nn.sigmoid(_bdot(xc, wa) + ba)
    i = jax.nn.sigmoid(_bdot(xc, wx) + bx)
    log_a = -RG_C * r * jax.nn.softplus(-lam)
    u = jnp.sqrt(jnp.maximum(_neg_expm1(2.0 * log_a), 0.0)) * (i * xc)
    return jnp.exp(log_a), u


def _rg_out(hs, gate):
    return hs * jax.nn.gelu(gate)


def _rows_to_tile(rows):
    sub = _iota((SUBLANES, 1), 0)
    out = jnp.broadcast_to(rows[0], (SUBLANES, rows[0].shape[1]))
    for j in range(1, SUBLANES):
        out = jnp.where(sub == j, rows[j], out)
    return out


def _rg_fwd(xc, hc, wa, wx, ba, bx, lam, name):
    s = xc.shape[0]
    tm = min(RG_TILE, s)

    def kern(x_ref, g_ref, wa_ref, wx_ref, ba_ref, bx_ref, lam_ref, y_ref, h_ref, a_s, u_s, carry):
        @pl.when(pl.program_id(0) == 0)
        def _():
            carry[...] = jnp.zeros_like(carry)

        a, u = _rg_gates(x_ref[...], wa_ref[...], wx_ref[...], ba_ref[...], bx_ref[...], lam_ref[...])
        a_s[...] = a
        u_s[...] = u

        def body(gi, hp):
            r0 = pl.multiple_of(gi * SUBLANES, SUBLANES)
            ab, ub = a_s[pl.ds(r0, SUBLANES), :], u_s[pl.ds(r0, SUBLANES), :]
            rows = []
            for j in range(SUBLANES):
                hp = ab[j:j + 1] * hp + ub[j:j + 1]
                rows.append(hp)
            h_ref[pl.ds(r0, SUBLANES), :] = _rows_to_tile(rows)
            return hp

        carry[0:1, :] = lax.fori_loop(0, tm // SUBLANES, body, carry[0:1, :])
        y_ref[...] = _rg_out(h_ref[...], g_ref[...]).astype(BF16)

    row = pl.BlockSpec((tm, 512), lambda i: (i, 0))
    mat = pl.BlockSpec((512, 512), lambda i: (0, 0))
    vec = pl.BlockSpec((1, 512), lambda i: (0, 0))
    return pl.pallas_call(
        kern, name=name, grid=(s // tm,),
        in_specs=[row, pl.BlockSpec((tm, 512), lambda i: (i, CD_GATE // 512)), mat, mat, vec, vec, vec],
        out_specs=[row, row],
        out_shape=[jax.ShapeDtypeStruct((s, 512), BF16), jax.ShapeDtypeStruct((s, 512), F32)],
        scratch_shapes=[pltpu.VMEM((tm, 512), F32), pltpu.VMEM((tm, 512), F32), pltpu.VMEM((SUBLANES, 512), F32)],
        compiler_params=_cp(("arbitrary",)),
    )(xc, hc, wa, wx, ba, bx, lam)


def _rg_bwd(dy, dy_cb, xc, hc, hs, wa, wx, ba, bx, lam, name):
    s = xc.shape[0]
    tm = min(RG_TILE, s)
    nt = s // tm
    ti = lambda i: nt - 1 - i

    def kern(dy_ref, x_ref, g_ref, h_ref, hp_ref, wa_ref, wx_ref, ba_ref, bx_ref, lam_ref,
             dx_ref, dg_ref, dwa_ref, dwx_ref, dba_ref, dbx_ref, dlam_ref, a_s, d_s, g_s, cg, ca):
        first = pl.program_id(0) == 0

        @pl.when(first)
        def _():
            cg[...] = jnp.zeros_like(cg)
            ca[...] = jnp.zeros_like(ca)

        (a, _), pull = jax.vjp(_rg_gates, x_ref[...], wa_ref[...].astype(F32), wx_ref[...].astype(F32), ba_ref[...],
                               bx_ref[...], lam_ref[...])
        hs_t = h_ref[...]
        _, pull_out = jax.vjp(_rg_out, hs_t, g_ref[...])
        dho, dgate = pull_out(dy_ref[...])
        dg_ref[...] = dgate.astype(BF16)
        a_s[...] = a
        d_s[...] = dho

        def body(k, c):
            gn, an = c
            r0 = pl.multiple_of((tm // SUBLANES - 1 - k) * SUBLANES, SUBLANES)
            ab, db = a_s[pl.ds(r0, SUBLANES), :], d_s[pl.ds(r0, SUBLANES), :]
            rows = [None] * SUBLANES
            for j in range(SUBLANES - 1, -1, -1):
                gn = db[j:j + 1] + an * gn
                an = ab[j:j + 1]
                rows[j] = gn
            g_s[pl.ds(r0, SUBLANES), :] = _rows_to_tile(rows)
            return gn, an

        gn, an = lax.fori_loop(0, tm // SUBLANES, body, (cg[0:1, :], ca[0:1, :]))
        cg[0:1, :] = gn
        ca[0:1, :] = an
        g = g_s[...]
        halo = jnp.where(pl.program_id(0) == nt - 1, 0.0, hp_ref[...])
        hprev = pltpu.roll(jnp.concatenate([halo, hs_t], axis=0), 1, axis=0)[SUBLANES:]
        dxc, dwa, dwx, dba, dbx, dlam = pull((g * hprev, g))
        dx_ref[...] = dxc
        _acc(dwa_ref, dwa, first)
        _acc(dwx_ref, dwx, first)
        _acc(dba_ref, dba, first)
        _acc(dbx_ref, dbx, first)
        _acc(dlam_ref, dlam, first)

    row = pl.BlockSpec((tm, 512), lambda i: (ti(i), 0))
    mat = pl.BlockSpec((512, 512), lambda i: (0, 0))
    vec = pl.BlockSpec((1, 512), lambda i: (0, 0))
    hprev = pl.BlockSpec((SUBLANES, 512), lambda i: (jnp.maximum(ti(i) * (tm // SUBLANES) - 1, 0), 0))
    return pl.pallas_call(
        kern, name=name, grid=(nt,),
        in_specs=[pl.BlockSpec((tm, 512), lambda i: (ti(i), dy_cb)), row,
                  pl.BlockSpec((tm, 512), lambda i: (ti(i), CD_GATE // 512)), row, hprev, mat, mat, vec, vec, vec],
        out_specs=[row, row, mat, mat, vec, vec, vec],
        out_shape=[jax.ShapeDtypeStruct((s, 512), F32), jax.ShapeDtypeStruct((s, 512), BF16),
                   jax.ShapeDtypeStruct((512, 512), F32), jax.ShapeDtypeStruct((512, 512), F32),
                   jax.ShapeDtypeStruct((1, 512), F32), jax.ShapeDtypeStruct((1, 512), F32), jax.ShapeDtypeStruct((1, 512), F32)],
        scratch_shapes=[pltpu.VMEM((tm, 512), F32)] * 3 + [pltpu.VMEM((SUBLANES, 512), F32)] * 2,
        compiler_params=_cp(("arbitrary",)),
    )(dy, xc, hc, hs, hs, wa, wx, ba, bx, lam)


AB_REF = dict(z=(0, 512), xbc=(512, 768), dt=(1280, 8), hq=(1288, 512), hf=(1800, 512), hi=(2312, 512), hg=(2824, 512))
AB_INT = dict(z=AB_Z, hq=AB_HQ, hf=AB_HF, hi=AB_HI, hg=AB_HG, xbc=AB_XBC, dt=AB_DT)
CD_REF = dict(q=(0, 512), k=(512, 128), v=(640, 128), gate=(768, 512), xr=(1280, 512))


def _cols(w, start, size):
    return lax.slice_in_dim(w, start, start + size, axis=w.ndim - 1)


def _pad_lanes(v):
    return jnp.concatenate([v.astype(F32), jnp.zeros((LANES - v.shape[0],), F32)]).reshape(1, LANES)


def _ab_in_to_int(w):
    parts = [_cols(w, *AB_REF[k]) for k in ("z", "hq", "hf", "hi", "hg", "xbc", "dt")]
    return jnp.concatenate(parts + [jnp.zeros(w.shape[:-1] + (AB_COLS - AB_DT - 8,), w.dtype)], axis=-1)


def _ab_in_to_ref(w):
    return jnp.concatenate([_cols(w, AB_INT[k], AB_REF[k][1]) for k in ("z", "xbc", "dt", "hq", "hf", "hi", "hg")], axis=-1)


def _pairs_to_int(w, axis):
    sl = lambda h: lax.slice_in_dim(w, 64 * h, 64 * h + 64, axis=axis)
    return jnp.concatenate([sl(h) for p in range(4) for h in (p, 4 + p)], axis=axis)


def _pairs_to_ref(w, axis):
    sl = lambda i: lax.slice_in_dim(w, 64 * i, 64 * i + 64, axis=axis)
    return jnp.concatenate([sl(2 * h) for h in range(4)] + [sl(2 * h + 1) for h in range(4)], axis=axis)


def _cd_in_to_int(w):
    q = _pairs_to_int(_cols(w, *CD_REF["q"]), w.ndim - 1)
    return jnp.concatenate([q] + [_cols(w, *CD_REF[k]) for k in ("gate", "xr", "k", "v")], axis=-1)


def _cd_in_to_ref(w):
    q = _pairs_to_ref(_cols(w, CD_Q, 512), w.ndim - 1)
    return jnp.concatenate([q, _cols(w, CD_K, 128), _cols(w, CD_V, 128), _cols(w, CD_GATE, 512), _cols(w, CD_XR, 512)], axis=-1)


def _block_diag(w):
    out = jnp.zeros((RG_WIDTH, RG_WIDTH), w.dtype)
    for g in range(8):
        out = lax.dynamic_update_slice(out, w[g], (64 * g, 64 * g))
    return out


def _diag_blocks(m):
    return jnp.stack([m[64 * g:64 * g + 64, 64 * g:64 * g + 64] for g in range(8)])


def _conv8(w):
    return jnp.concatenate([w, jnp.zeros((SUBLANES - w.shape[0], w.shape[1]), w.dtype)], axis=0)


def _hg_lower_bounds(hg_lower):
    sm = jax.nn.softmax(hg_lower.astype(F32), axis=0)
    return jnp.clip(jnp.cumsum(sm, axis=0) - sm[0], 0.0, 1.0)


def _local_step(x, target, w):
    s = x.shape[0]
    lb_all, lb_pull = jax.vjp(_hg_lower_bounds, w["hg_lower"])
    grads = {k: [None] * v.shape[0] for k, v in w.items()}
    saved = []
    xb = x.astype(BF16)
    for l in range(DEPTH):
        j = l // 2
        t = f"l{l}"
        sv = dict(x=x, xb=xb)
        if l % 2 == 0:
            win = _ab_in_to_int(w["ab_w_in"][j])
            wout = w["ab_w_out"][j]
            h = _mm([(xb, 0, win, 0, D_MODEL)], AB_COLS, F32, t + "_in")
            w8 = _conv8(w["ssd_conv_w"][j])
            xbcc = _dwconv_fwd(h, AB_XBC // CONV_CB, SSD_CONV_DIM, w8, w["ssd_conv_b"][j], 4, t + "_conv")
            sp = (_pad_lanes(w["ssd_dt_bias"][j]), _pad_lanes(w["ssd_a_log"][j]), _pad_lanes(w["ssd_d"][j]),
                  w["ssd_norm_w"][j].reshape(1, 512))
            ya, ssts = _ssd_fwd(h, xbcc, *sp, t + "_ssd")
            hp = (lb_all[j].reshape(1, 512), w["hg_norm_w"][j].reshape(1, LANES))
            yb, hsts = _hg_fwd(h, *hp, t + "_hg")
            sv.update(win=win, wout=wout, h=h, w8=w8, xbcc=xbcc, sp=sp, ssts=ssts, hp=hp, hsts=hsts)
        else:
            win = _cd_in_to_int(w["cd_w_in"][j])
            wout = jnp.concatenate([_pairs_to_int(w["cd_w_out"][j][:512], 0), w["cd_w_out"][j][512:]], axis=0)
            h = _mm([(xb, 0, win, 0, D_MODEL)], CD_COLS, F32, t + "_in")
            srow = _pad_lanes(w["swa_sinks"][j])
            ya = _swa_fwd(h, srow, t + "_swa")
            w8 = _conv8(w["rg_conv_w"][j])
            xc = _dwconv_fwd(h, CD_XR // CONV_CB, RG_WIDTH, w8, w["rg_conv_b"][j], 4, t + "_conv")
            rp = (_block_diag(w["rg_wa"][j]).astype(BF16), _block_diag(w["rg_wx"][j]).astype(BF16),
                  w["rg_ba"][j].reshape(1, 512), w["rg_bx"][j].reshape(1, 512), w["rg_lambda"][j].reshape(1, 512))
            yb, hs = _rg_fwd(xc, h, *rp, t + "_rg")
            sv.update(win=win, wout=wout, h=h, w8=w8, srow=srow, xc=xc, rp=rp, hs=hs)
        m = _mm([(ya, 0, wout, 0, 512), (yb, 0, wout, 1, 512)], D_MODEL, F32, t + "_out")
        x1, x1b, r1 = _ln_fwd(x, m, w["ln_g"][l, 0], w["ln_b"][l, 0], t + "_ln1")
        wup, wdown = w["ffn_w_up"][l], w["ffn_w_down"][l]
        fw8 = _conv8(w["ffn_conv_w"][l])
        hu = _mm([(x1b, 0, wup, 0, D_MODEL)], 2 * FFN_DIM, BF16, t + "_up")
        a = _ffn_gate_fwd(hu, fw8, w["ffn_conv_b"][l], t + "_gate")
        f = _mm([(a, 0, wdown, 0, FFN_DIM)], D_MODEL, F32, t + "_down")
        x, xb, r2 = _ln_fwd(x1, f, w["ln_g"][l, 1], w["ln_b"][l, 1], t + "_ln2")
        sv.update(ya=ya, yb=yb, r1=r1, x1b=x1b, hu=hu, a=a, r2=r2, fw8=fw8)
        saved.append(sv)

    loss = 0.5 * jnp.sum(_sqerr(x, target, "sqerr")) / D_MODEL
    dya, dyb, ca, cb = x, target, 1.0 / D_MODEL, -1.0 / D_MODEL
    for l in range(DEPTH - 1, -1, -1):
        j = l // 2
        t = f"l{l}b"
        sv = saved[l]
        wup, wdown = w["ffn_w_up"][l], w["ffn_w_down"][l]
        dr2, dg2, db2 = _ln_bwd(dya, dyb, ca, cb, sv["r2"], w["ln_g"][l, 1], t + "_ln2")
        da = _mm([(dr2, 0, wdown, 0, D_MODEL)], FFN_DIM, BF16, t + "_down", trans_b=True)
        grads["ffn_w_down"][l] = _mm_tn(sv["a"], 0, FFN_DIM, dr2, 0, D_MODEL, t + "_wdown")
        dhg, dhu, dfw8, dfb = _ffn_gate_bwd(sv["hu"], da, sv["fw8"], w["ffn_conv_b"][l], t + "_gate")
        grads["ffn_conv_w"][l] = dfw8[:3]
        grads["ffn_conv_b"][l] = dfb
        dx1 = _mm([(dhg, 0, wup, 0, FFN_DIM), (dhu, 0, wup, 1, FFN_DIM)], D_MODEL, F32, t + "_up", trans_b=True)
        grads["ffn_w_up"][l] = jnp.concatenate([_mm_tn(sv["x1b"], 0, D_MODEL, dhg, 0, FFN_DIM, t + "_wup_g"),
                                                _mm_tn(sv["x1b"], 0, D_MODEL, dhu, 0, FFN_DIM, t + "_wup_u")], axis=1)
        dr1, dg1, db1 = _ln_bwd(dr2, dx1, ALPHA, 1.0, sv["r1"], w["ln_g"][l, 0], t + "_ln1")
        grads["ln_g"][l] = jnp.stack([dg1, dg2])
        grads["ln_b"][l] = jnp.stack([db1, db2])
        win, wout, h = sv["win"], sv["wout"], sv["h"]
        dycat = _mm([(dr1, 0, wout, 0, D_MODEL)], D_MODEL, F32, t + "_out", trans_b=True)
        dwout = jnp.concatenate([_mm_tn(sv["ya"], 0, 512, dr1, 0, D_MODEL, t + "_wout_a"),
                                 _mm_tn(sv["yb"], 0, 512, dr1, 0, D_MODEL, t + "_wout_b")], axis=0)
        if l % 2 == 0:
            dz, dxbcc, ddt, ddtb, dal, ddsk, dnw = _ssd_bwd(dycat, 0, h, sv["xbcc"], sv["ssts"], *sv["sp"], t + "_ssd")
            dxbc, dcw, dcb = _dwconv_bwd(dxbcc, h, AB_XBC // CONV_CB, SSD_CONV_DIM, sv["w8"], 4, t + "_conv")
            dq, df, di, dg, dlb, dhnw = _hg_bwd(dycat, 1, h, sv["hsts"], *sv["hp"], t + "_hg")
            grads["ab_w_out"][j] = dwout
            grads["ssd_conv_w"][j], grads["ssd_conv_b"][j] = dcw[:4], dcb
            grads["ssd_dt_bias"][j], grads["ssd_a_log"][j], grads["ssd_d"][j] = ddtb[0, :8], dal[0, :8], ddsk[0, :8]
            grads["ssd_norm_w"][j], grads["hg_norm_w"][j] = dnw[0], dhnw[0]
            grads["hg_lower"][j] = dlb[0]
            pieces = [(dz, 0, AB_Z, 512), (dq, 0, AB_HQ, 512), (df, 0, AB_HF, 512), (di, 0, AB_HI, 512), (dg, 0, AB_HG, 512),
                      (dxbc, 0, AB_XBC, 256), (dxbc, 1, AB_XBC + 256, 256), (dxbc, 2, AB_XBC + 512, 256), (ddt, 0, AB_DT, LANES)]
            dwin = [_mm_tn(sv["xb"], 0, D_MODEL, p, 0, p.shape[1], t + f"_win{i}")
                    for i, p in enumerate((dz, dq, df, di, dg, dxbc, ddt))]
            grads["ab_w_in"][j] = _ab_in_to_ref(jnp.concatenate(dwin, axis=1))
        else:
            dq, dk, dv, dsr = _swa_bwd(dycat, 0, h, sv["srow"], t + "_swa")
            dxc, dgate, dwa, dwx, dba, dbx, dlam = _rg_bwd(dycat, 1, sv["xc"], h, sv["hs"], *sv["rp"], t + "_rg")
            dxr, dcw, dcb = _dwconv_bwd(dxc, h, CD_XR // CONV_CB, RG_WIDTH, sv["w8"], 4, t + "_conv")
            grads["cd_w_out"][j] = jnp.concatenate([_pairs_to_ref(dwout[:512], 0), dwout[512:]], axis=0)
            grads["swa_sinks"][j] = dsr[0, :8]
            grads["rg_conv_w"][j], grads["rg_conv_b"][j] = dcw[:4], dcb
            grads["rg_wa"][j], grads["rg_wx"][j] = _diag_blocks(dwa), _diag_blocks(dwx)
            grads["rg_ba"][j], grads["rg_bx"][j], grads["rg_lambda"][j] = dba[0], dbx[0], dlam[0]
            pieces = [(dq, 0, CD_Q, 512), (dgate, 0, CD_GATE, 512), (dxr, 0, CD_XR, 512), (dk, 0, CD_K, LANES), (dv, 0, CD_V, LANES)]
            dwin = [_mm_tn(sv["xb"], 0, D_MODEL, p, 0, p.shape[1], t + f"_win{i}")
                    for i, p in enumerate((dq, dgate, dxr, dk, dv))]
            grads["cd_w_in"][j] = _cd_in_to_ref(jnp.concatenate(dwin, axis=1))
        dx0 = _mm([(p, pcb, win, off // k, k) for p, pcb, off, k in pieces], D_MODEL, F32, t + "_in", trans_b=True)
        dya, dyb, ca, cb = dr1, dx0, ALPHA, 1.0
    grad_x = _axpby(dya, dyb, ca, cb, "grad_x")
    out = {k: jnp.stack(v) for k, v in grads.items()}
    (out["hg_lower"],) = lb_pull(out["hg_lower"])
    return loss, grad_x, out


MESH = pl.DeviceIdType.MESH
ANY = pl.BlockSpec(memory_space=pl.ANY)
PACK_ROWS = 2048


def _place():
    x, y, c = lax.axis_index("x"), lax.axis_index("y"), lax.axis_index("c")
    return x, y, c, [(1 - x, y), (x, 1 - y), (1 - x, 1 - y)]


def _rcopy(src, dst, send_sems, recv_sems, k, to):
    return pltpu.make_async_remote_copy(src_ref=src, dst_ref=dst, send_sem=send_sems.at[k], recv_sem=recv_sems.at[k],
                                        device_id=to, device_id_type=MESH)


def _gather_chips(wp, name):
    def body(w_ref, out_ref, send_sems, recv_sems, local_sem):
        x, y, c, chips = _place()
        me, sib = 2 * x + y, (x, y, 1 - c)
        mine = pltpu.make_async_copy(w_ref, out_ref.at[me], local_sem)
        mine.start()
        first = [_rcopy(w_ref.at[c], out_ref.at[me, c], send_sems, recv_sems, k, (cx, cy, c)) for k, (cx, cy) in enumerate(chips)]
        for cp in first:
            cp.start()
        passed = []
        for k, (cx, cy) in enumerate(chips):
            blk = out_ref.at[2 * cx + cy, c]
            _rcopy(blk, blk, send_sems, recv_sems, k, (cx, cy, c)).wait_recv()
            passed.append(_rcopy(blk, blk, send_sems, recv_sems, 3 + k, sib))
            passed[-1].start()
        for k, (cx, cy) in enumerate(chips):
            blk = out_ref.at[2 * cx + cy, 1 - c]
            _rcopy(blk, blk, send_sems, recv_sems, 3 + k, sib).wait_recv()
        for cp in first + passed:
            cp.wait_send()
        mine.wait()

    return pl.pallas_call(
        body, name=name, in_specs=[ANY], out_specs=ANY, out_shape=jax.ShapeDtypeStruct((4,) + wp.shape, wp.dtype),
        scratch_shapes=[pltpu.SemaphoreType.DMA((6,)), pltpu.SemaphoreType.DMA((6,)), pltpu.SemaphoreType.DMA(())],
    )(wp)


def _swap_halves(g, name):
    def body(g_ref, x_ref, send_sems, recv_sems):
        x, y, c, _ = _place()
        cp = _rcopy(g_ref.at[:, 1 - c], x_ref, send_sems, recv_sems, 0, (x, y, 1 - c))
        cp.start()
        cp.wait()

    return pl.pallas_call(
        body, name=name, in_specs=[ANY], out_specs=ANY,
        out_shape=jax.ShapeDtypeStruct((g.shape[0],) + g.shape[2:], g.dtype),
        scratch_shapes=[pltpu.SemaphoreType.DMA((1,)), pltpu.SemaphoreType.DMA((1,))],
    )(g)


def _scatter_chips(hb, name):
    def body(h_ref, y_ref, send_sems, recv_sems, local_sem):
        x, y, c, chips = _place()
        me = 2 * x + y
        mine = pltpu.make_async_copy(h_ref.at[me], y_ref.at[me], local_sem)
        mine.start()
        sends = [_rcopy(h_ref.at[2 * cx + cy], y_ref.at[me], send_sems, recv_sems, k, (cx, cy, c)) for k, (cx, cy) in enumerate(chips)]
        for cp in sends:
            cp.start()
        for k, (cx, cy) in enumerate(chips):
            blk = y_ref.at[2 * cx + cy]
            _rcopy(blk, blk, send_sems, recv_sems, k, (cx, cy, c)).wait_recv()
        for cp in sends:
            cp.wait_send()
        mine.wait()

    return pl.pallas_call(
        body, name=name, in_specs=[ANY], out_specs=ANY, out_shape=jax.ShapeDtypeStruct(hb.shape, hb.dtype),
        scratch_shapes=[pltpu.SemaphoreType.DMA((3,)), pltpu.SemaphoreType.DMA((3,)), pltpu.SemaphoreType.DMA(())],
    )(hb)


def _share_half(f, name):
    def body(f_ref, out_ref, send_sems, recv_sems, local_sem):
        x, y, c, _ = _place()
        sib = (x, y, 1 - c)
        mine = pltpu.make_async_copy(f_ref, out_ref.at[c], local_sem)
        mine.start()
        cp = _rcopy(f_ref, out_ref.at[c], send_sems, recv_sems, 0, sib)
        cp.start()
        _rcopy(f_ref, out_ref.at[1 - c], send_sems, recv_sems, 0, sib).wait_recv()
        cp.wait_send()
        mine.wait()

    return pl.pallas_call(
        body, name=name, in_specs=[ANY], out_specs=ANY, out_shape=jax.ShapeDtypeStruct((2,) + f.shape, f.dtype),
        scratch_shapes=[pltpu.SemaphoreType.DMA((1,)), pltpu.SemaphoreType.DMA((1,)), pltpu.SemaphoreType.DMA(())],
    )(f)


def _gather_devices(v, name):
    m_per, n = v.shape

    def body(x_ref, out_ref, send_sems, recv_sems, local_sem):
        x, y, c, chips = _place()
        me, sib = (x, y, c), (x, y, 1 - c)

        def rows(px, py, pc):
            return out_ref.at[pl.ds((4 * px + 2 * py + pc) * m_per, m_per), :]

        def copy(k, block, to, src=None):
            return _rcopy(rows(*block) if src is None else src, rows(*block), send_sems, recv_sems, k, to)

        mine = pltpu.make_async_copy(x_ref, rows(*me), local_sem)
        mine.start()
        first = [copy(0, me, sib, src=x_ref)]
        first += [copy(1 + j, me, (*chip, c), src=x_ref) for j, chip in enumerate(chips)]
        for cp in first:
            cp.start()
        passed = [copy(4 + j, (*chip, c), sib) for j, chip in enumerate(chips)]
        for j, chip in enumerate(chips):
            copy(1 + j, (*chip, c), me).wait_recv()
            passed[j].start()
        copy(0, sib, me).wait_recv()
        for j, chip in enumerate(chips):
            copy(4 + j, (*chip, 1 - c), me).wait_recv()
        for cp in first + passed:
            cp.wait_send()
        mine.wait()

    return pl.pallas_call(
        body, name=name, out_shape=jax.ShapeDtypeStruct((8 * m_per, n), v.dtype),
        in_specs=[pl.BlockSpec(memory_space=pltpu.VMEM)], out_specs=pl.BlockSpec(memory_space=pltpu.VMEM),
        scratch_shapes=[pltpu.SemaphoreType.DMA((7,)), pltpu.SemaphoreType.DMA((7,)), pltpu.SemaphoreType.DMA(())],
        compiler_params=_cp(),
    )(v)


def _add_half(g, xr, c, name, tr=2048):
    rh = g.shape[2]

    def kern(c_ref, g_ref, x_ref, o_ref):
        o_ref[...] = (g_ref[0] + x_ref[...]).astype(BF16)

    return pl.pallas_call(
        kern, name=name,
        grid_spec=pltpu.PrefetchScalarGridSpec(
            num_scalar_prefetch=1, grid=(4, rh // tr),
            in_specs=[pl.BlockSpec((1, 1, tr, LANES), lambda k, i, c_ref: (k, c_ref[0], i, 0)),
                      pl.BlockSpec((1, tr, LANES), lambda k, i, c_ref: (k, i, 0))],
            out_specs=pl.BlockSpec((1, tr, LANES), lambda k, i, c_ref: (k, i, 0))),
        out_shape=jax.ShapeDtypeStruct(xr.shape, BF16), compiler_params=_cp(("parallel", "parallel")),
    )(c.reshape(1).astype(jnp.int32), g, xr)


def _sum_blocks(y, name, tr=2048):
    n, r, _ = y.shape
    tr = min(tr, r)

    def kern(y_ref, o_ref):
        acc = y_ref[0].astype(F32)
        for k in range(1, n):
            acc = acc + y_ref[k].astype(F32)
        o_ref[...] = acc

    return pl.pallas_call(
        kern, name=name, grid=(r // tr,), in_specs=[pl.BlockSpec((n, tr, LANES), lambda i: (0, i, 0))],
        out_specs=pl.BlockSpec((tr, LANES), lambda i: (i, 0)), out_shape=jax.ShapeDtypeStruct((r, LANES), F32),
        compiler_params=_cp(("parallel",)),
    )(y)


def _adamw(w, g, m, v, name, tr=256):
    r, n = w.shape
    tr = tr if r % tr == 0 else r

    def kern(w_ref, g_ref, m_ref, v_ref, d_ref, nm_ref, nv_ref):
        gg = g_ref[...]
        nm = ADAM_B1 * m_ref[...] + (1.0 - ADAM_B1) * gg
        nv = ADAM_B2 * v_ref[...] + (1.0 - ADAM_B2) * (gg * gg)
        m_hat = nm / (1.0 - ADAM_B1 ** ADAM_STEP)
        v_hat = nv / (1.0 - ADAM_B2 ** ADAM_STEP)
        d_ref[...] = -ADAM_LR * (m_hat / (jnp.sqrt(v_hat) + ADAM_EPS) + ADAM_WD * w_ref[...])
        nm_ref[...] = nm
        nv_ref[...] = nv

    blk = pl.BlockSpec((tr, n), lambda i: (i, 0))
    return pl.pallas_call(
        kern, name=name, grid=(r // tr,), in_specs=[blk] * 4, out_specs=[blk] * 3,
        out_shape=[jax.ShapeDtypeStruct((r, n), F32)] * 3, compiler_params=_cp(("parallel",)),
    )(w, g, m, v)


def _pack(arrs, lead, dtype, mult):
    ls = arrs[0].shape[:lead]
    flat = [a.astype(dtype).reshape(ls + (-1,)) for a in arrs]
    n = sum(f.shape[-1] for f in flat)
    rows = -(-n // (LANES * mult)) * mult
    flat.append(jnp.zeros(ls + (rows * LANES - n,), dtype))
    return jnp.concatenate(flat, axis=-1).reshape(ls + (rows, LANES))


def _unpack(buf, lead, shapes):
    ls = buf.shape[:lead]
    flat = buf.reshape(ls + (-1,))
    out, off = [], 0
    for sh in shapes:
        n = int(np.prod(sh))
        out.append(lax.slice_in_dim(flat, off, off + n, axis=lead).reshape(ls + tuple(sh)))
        off += n
    return out


BIG = ("ab_w_in", "ab_w_out", "cd_w_in", "cd_w_out", "ffn_w_up", "ffn_w_down")
BIG_COLS = ("ab_w_in", "cd_w_in", "ffn_w_up")
SMALL_SHARDED = ("ssd_conv_w", "rg_conv_w", "rg_conv_b", "rg_ba", "rg_bx", "rg_lambda", "ffn_conv_w", "ln_g", "ln_b")
WEIGHTS = ("ab_w_in", "ssd_conv_w", "ssd_conv_b", "ssd_dt_bias", "ssd_a_log", "ssd_d", "ssd_norm_w", "hg_lower", "hg_norm_w",
           "ab_w_out", "cd_w_in", "swa_sinks", "rg_conv_w", "rg_conv_b", "rg_wa", "rg_ba", "rg_wx", "rg_bx", "rg_lambda",
           "cd_w_out", "ffn_w_up", "ffn_conv_w", "ffn_conv_b", "ffn_w_down", "ln_g", "ln_b")
SMALL = tuple(n for n in WEIGHTS if n not in BIG)


def _full_from_shards(name, g):
    if name in BIG_COLS or name in SMALL_SHARDED:
        return jnp.concatenate([g[k] for k in range(4)], axis=-1)
    return jnp.concatenate([g[k] for k in range(4)], axis=1)


def _shards_from_full(name, f):
    axis = f.ndim - 1 if (name in BIG_COLS or name in SMALL_SHARDED) else 1
    n = f.shape[axis] // 4
    return jnp.stack([lax.slice_in_dim(f, k * n, (k + 1) * n, axis=axis) for k in range(4)])


def kernel(x, ab_w_in, ssd_conv_w, ssd_conv_b, ssd_dt_bias, ssd_a_log, ssd_d, ssd_norm_w, hg_lower, hg_norm_w, ab_w_out, cd_w_in, swa_sinks, rg_conv_w, rg_conv_b, rg_wa, rg_ba, rg_wx, rg_bx, rg_lambda, cd_w_out, ffn_w_up, ffn_conv_w, ffn_conv_b, ffn_w_down, ln_g, ln_b, loss_target, m_ab_w_in, m_ssd_conv_w, m_ssd_conv_b, m_ssd_dt_bias, m_ssd_a_log, m_ssd_d, m_ssd_norm_w, m_hg_lower, m_hg_norm_w, m_ab_w_out, m_cd_w_in, m_swa_sinks, m_rg_conv_w, m_rg_conv_b, m_rg_wa, m_rg_ba, m_rg_wx, m_rg_bx, m_rg_lambda, m_cd_w_out, m_ffn_w_up, m_ffn_conv_w, m_ffn_conv_b, m_ffn_w_down, m_ln_g, m_ln_b, v_ab_w_in, v_ssd_conv_w, v_ssd_conv_b, v_ssd_dt_bias, v_ssd_a_log, v_ssd_d, v_ssd_norm_w, v_hg_lower, v_hg_norm_w, v_ab_w_out, v_cd_w_in, v_swa_sinks, v_rg_conv_w, v_rg_conv_b, v_rg_wa, v_rg_ba, v_rg_wx, v_rg_bx, v_rg_lambda, v_cd_w_out, v_ffn_w_up, v_ffn_conv_w, v_ffn_conv_b, v_ffn_w_down, v_ln_g, v_ln_b):
    args = locals()
    w = {n: args[n] for n in WEIGHTS}
    mom = {n: args["m_" + n] for n in WEIGHTS}
    var = {n: args["v_" + n] for n in WEIGHTS}
    cx, cy, cc = lax.axis_index("x"), lax.axis_index("y"), lax.axis_index("c")
    chip = 2 * cx + cy

    big_shapes = [w[n].shape for n in BIG]
    wp = _pack([w[n] for n in BIG], 0, BF16, 2 * PACK_ROWS)
    rh = wp.shape[0] // 2
    gathered = _gather_chips(wp.reshape(2, rh, LANES), "gather_weights").reshape(4, 2 * rh, LANES)
    full = {n: _full_from_shards(n, g) for n, g in zip(BIG, _unpack(gathered, 1, big_shapes))}
    sp = _pack([w[n] for n in SMALL_SHARDED], 0, F32, SUBLANES)
    sg = _gather_devices(sp, "gather_small").reshape(4, 2, sp.shape[0], LANES)[:, 0]
    for n, g in zip(SMALL_SHARDED, _unpack(sg, 1, [w[n].shape for n in SMALL_SHARDED])):
        full[n] = _full_from_shards(n, g)
    for n in SMALL:
        full.setdefault(n, w[n])

    loss, grad_x, grads = _local_step(x[0], loss_target[0], full)
    loss = lax.psum(loss, ("x", "y", "c"))

    g4 = _pack([_shards_from_full(n, grads[n]) for n in BIG], 1, F32, 2 * PACK_ROWS).reshape(4, 2, rh, LANES)
    hb = _add_half(g4, _swap_halves(g4, "swap_halves"), cc, "add_halves")
    own = _sum_blocks(_scatter_chips(hb, "scatter_chips"), "sum_chips")
    gbig = _unpack(_share_half(own, "share_half").reshape(2 * rh, LANES), 0, big_shapes)
    gout = dict(zip(BIG, gbig))

    small_shapes = [grads[n].shape for n in SMALL]
    gs = _pack([grads[n] for n in SMALL], 0, F32, SUBLANES)
    gsum = _sum_blocks(_gather_devices(gs, "gather_small_grads").reshape(8, gs.shape[0], LANES), "sum_small")
    for n, g in zip(SMALL, _unpack(gsum, 0, small_shapes)):
        if n in SMALL_SHARDED:
            width = w[n].shape[-1]
            g = lax.dynamic_slice_in_dim(g, chip * width, width, axis=g.ndim - 1)
        gout[n] = g

    delta, new_m, new_v = {}, {}, {}
    for n in BIG:
        sh = w[n].shape
        two = lambda a: a.reshape(-1, sh[-1])
        d, nm, nv = _adamw(two(w[n]), two(gout[n]), two(mom[n]), two(var[n]), "adamw_" + n)
        delta[n], new_m[n], new_v[n] = d.reshape(sh), nm.reshape(sh), nv.reshape(sh)
    local_shapes = [w[n].shape for n in SMALL]
    packs = [_pack([d[n] for n in SMALL], 0, F32, SUBLANES) for d in (w, gout, mom, var)]
    for dst, buf in zip((delta, new_m, new_v), _adamw(*packs, "adamw_small")):
        dst.update(zip(SMALL, _unpack(buf, 0, local_shapes)))

    return (loss, grad_x[None], *[gout[n] for n in WEIGHTS], *[delta[n] for n in WEIGHTS],
            *[new_m[n] for n in WEIGHTS], *[new_v[n] for n in WEIGHTS])
```

```python
import functools
import math

import numpy as np
import jax
import jax.numpy as jnp
from jax import lax
from jax.experimental import pallas as pl
from jax.experimental.pallas import tpu as pltpu

F32 = jnp.float32
BF16 = jnp.bfloat16
HI = lax.Precision.HIGHEST

D_MODEL = 1024
DEPTH = 4
SSD_HEADS = 8
SSD_D_INNER = 512
SSD_CONV_DIM = 768
SSD_CHUNK = 128
HG_CHUNK = 64
HG_HEADS = 4
SWA_BLOCK = 128
RG_WIDTH = 512
FFN_DIM = 2816
LN_EPS = 1e-5
RMS_EPS = 1e-6
MASK_VALUE = -1e9
ALPHA = (2 * DEPTH) ** 0.25
RG_C = 8.0
ADAM_LR, ADAM_B1, ADAM_B2, ADAM_EPS, ADAM_WD, ADAM_STEP = 0.001, 0.9, 0.999, 1e-08, 0.01, 10

LANES = 128
SUBLANES = 8
HALO = 16
CONV_CB = 256
VMEM_LIMIT = 56 * 1024 * 1024

AB_Z, AB_HQ, AB_HF, AB_HI, AB_HG, AB_XBC, AB_DT, AB_COLS = 0, 512, 1024, 1536, 2048, 2560, 3328, 3456
CD_Q, CD_GATE, CD_XR, CD_K, CD_V, CD_COLS = 0, 512, 1024, 1536, 1664, 1792


def _cp(sem=None):
    kw = dict(vmem_limit_bytes=VMEM_LIMIT)
    if sem is not None:
        kw["dimension_semantics"] = sem
    return pltpu.CompilerParams(**kw)


def _dot(a, b, dims=(((1,), (0,)), ((), ())), precision=None):
    return lax.dot_general(a, b, dims, precision=precision, preferred_element_type=F32)


NN = (((1,), (0,)), ((), ()))
NT = (((1,), (1,)), ((), ()))
TN = (((0,), (0,)), ((), ()))


@functools.partial(jax.custom_vjp, nondiff_argnums=(2,))
def _bdot(a, b, dims=NN):
    return _dot(a.astype(BF16), b.astype(BF16), dims)


def _bdot_fwd(a, b, dims):
    return _bdot(a, b, dims), (a, b)


def _bdot_bwd(dims, res, ct):
    a, b = res
    ab, bb, cb = a.astype(BF16), b.astype(BF16), ct.astype(BF16)
    if dims == NN:
        da, db = _dot(cb, bb, NT), _dot(ab, cb, TN)
    elif dims == NT:
        da, db = _dot(cb, bb, NN), _dot(cb, ab, TN)
    else:
        da, db = _dot(bb, cb, NT), _dot(ab, cb, NN)
    return da.astype(a.dtype), db.astype(b.dtype)


_bdot.defvjp(_bdot_fwd, _bdot_bwd)


def _hdot(a, b, dims=NN):
    return _dot(a, b, dims, precision=HI)


WIDE_TILES = (1408, 1152, 1024, 896, 768, 512, 384, 256, 128)
MM_BLOCK_BYTES = 6 * 1024 * 1024


def _pick(n, cands):
    for c in cands:
        if n % c == 0:
            return c
    return n


def _mm(pairs, n_out, out_dtype, name, trans_b=False, tm=512):
    m = pairs[0][0].shape[0]
    a_row_bytes = sum(p[4] * p[0].dtype.itemsize for p in pairs)
    b_col_bytes = sum(p[4] * p[2].dtype.itemsize for p in pairs)
    tm = min(2 * tm if 2 * tm * a_row_bytes <= MM_BLOCK_BYTES else tm, m)
    tn = _pick(n_out, [c for c in WIDE_TILES if c * b_col_bytes <= MM_BLOCK_BYTES])
    n = len(pairs)

    def kern(*refs):
        acc = None
        for i in range(n):
            p = _bdot(refs[i][...], refs[n + i][...], NT if trans_b else NN)
            acc = p if acc is None else acc + p
        refs[2 * n][...] = acc.astype(out_dtype)

    in_specs = []
    for a, acb, b, bcb, k in pairs:
        in_specs.append(pl.BlockSpec((tm, k), functools.partial(lambda i, j, c: (i, c), c=acb)))
    for a, acb, b, bcb, k in pairs:
        if trans_b:
            in_specs.append(pl.BlockSpec((tn, k), functools.partial(lambda i, j, c: (j, c), c=bcb)))
        else:
            in_specs.append(pl.BlockSpec((k, tn), functools.partial(lambda i, j, c: (c, j), c=bcb)))
    return pl.pallas_call(
        kern, name=name, grid=(m // tm, n_out // tn), in_specs=in_specs,
        out_specs=pl.BlockSpec((tm, tn), lambda i, j: (i, j)),
        out_shape=jax.ShapeDtypeStruct((m, n_out), out_dtype),
        compiler_params=_cp(("parallel", "arbitrary")),
    )(*[p[0] for p in pairs], *[p[2] for p in pairs])


def _mm_tn(a, a_cb, ka, g, g_cb, ng, name, tm=512):
    m = a.shape[0]
    tm = min(tm, m)
    tk = _pick(ka, (1024, 1408, 512, 256, 128))
    tn = _pick(ng, WIDE_TILES)
    nm = m // tm

    def kern(a_ref, g_ref, o_ref):
        p = _bdot(a_ref[...], g_ref[...], TN)

        @pl.when(pl.program_id(2) == 0)
        def _():
            o_ref[...] = p

        @pl.when(pl.program_id(2) > 0)
        def _():
            o_ref[...] += p

    ak, gk = ka // tk, ng // tn
    return pl.pallas_call(
        kern, name=name, grid=(ak, gk, nm),
        in_specs=[pl.BlockSpec((tm, tk), lambda i, j, r: (r, a_cb * ak + i)),
                  pl.BlockSpec((tm, tn), lambda i, j, r: (r, g_cb * gk + j))],
        out_specs=pl.BlockSpec((tk, tn), lambda i, j, r: (i, j)),
        out_shape=jax.ShapeDtypeStruct((ka, ng), F32),
        compiler_params=_cp(("parallel", "parallel", "arbitrary")),
    )(a, g)


def _ln_fwd(x, m, g, b, name, tm=256):
    s, d = x.shape
    tm = min(tm, s)

    def kern(x_ref, m_ref, g_ref, b_ref, y_ref, yb_ref, r_ref):
        r = ALPHA * x_ref[...] + m_ref[...]
        mu = jnp.mean(r, -1, keepdims=True)
        xc = r - mu
        var = jnp.mean(xc * xc, -1, keepdims=True)
        y = xc * lax.rsqrt(var + LN_EPS) * g_ref[...] + b_ref[...]
        y_ref[...] = y
        yb_ref[...] = y.astype(BF16)
        r_ref[...] = r

    row = pl.BlockSpec((tm, d), lambda i: (i, 0))
    vec = pl.BlockSpec((1, d), lambda i: (0, 0))
    return pl.pallas_call(
        kern, name=name, grid=(s // tm,), in_specs=[row, row, vec, vec], out_specs=[row, row, row],
        out_shape=[jax.ShapeDtypeStruct((s, d), F32), jax.ShapeDtypeStruct((s, d), BF16), jax.ShapeDtypeStruct((s, d), F32)],
        compiler_params=_cp(("parallel",)),
    )(x, m, g.reshape(1, d), b.reshape(1, d))


def _ln_bwd(dya, dyb, ca, cb, r, g, name, tm=256):
    s, d = r.shape
    tm = min(tm, s)

    def kern(a_ref, b_ref, r_ref, g_ref, dr_ref, dg_ref, db_ref):
        dy = ca * a_ref[...] + cb * b_ref[...]
        rr = r_ref[...]
        mu = jnp.mean(rr, -1, keepdims=True)
        xc = rr - mu
        var = jnp.mean(xc * xc, -1, keepdims=True)
        rstd = lax.rsqrt(var + LN_EPS)
        xhat = xc * rstd
        dxh = dy * g_ref[...]
        dr_ref[...] = rstd * (dxh - jnp.mean(dxh, -1, keepdims=True) - xhat * jnp.mean(dxh * xhat, -1, keepdims=True))
        dg = jnp.sum(dy * xhat, 0, keepdims=True)
        db = jnp.sum(dy, 0, keepdims=True)

        @pl.when(pl.program_id(0) == 0)
        def _():
            dg_ref[...] = dg
            db_ref[...] = db

        @pl.when(pl.program_id(0) > 0)
        def _():
            dg_ref[...] += dg
            db_ref[...] += db

    row = pl.BlockSpec((tm, d), lambda i: (i, 0))
    vec = pl.BlockSpec((1, d), lambda i: (0, 0))
    dr, dg, db = pl.pallas_call(
        kern, name=name, grid=(s // tm,), in_specs=[row, row, row, vec], out_specs=[row, vec, vec],
        out_shape=[jax.ShapeDtypeStruct((s, d), F32), jax.ShapeDtypeStruct((1, d), F32), jax.ShapeDtypeStruct((1, d), F32)],
        compiler_params=_cp(("arbitrary",)),
    )(dya, dyb, r, g.reshape(1, d))
    return dr, dg[0], db[0]


def _sqerr(y, t, name, tm=256):
    s, d = y.shape
    tm = min(tm, s)

    def kern(y_ref, t_ref, o_ref):
        e = y_ref[...] - t_ref[...]
        p = jnp.sum(e * e, 0, keepdims=True)

        @pl.when(pl.program_id(0) == 0)
        def _():
            o_ref[...] = p

        @pl.when(pl.program_id(0) > 0)
        def _():
            o_ref[...] += p

    row = pl.BlockSpec((tm, d), lambda i: (i, 0))
    return pl.pallas_call(
        kern, name=name, grid=(s // tm,), in_specs=[row, row], out_specs=pl.BlockSpec((1, d), lambda i: (0, 0)),
        out_shape=jax.ShapeDtypeStruct((1, d), F32), compiler_params=_cp(("arbitrary",)),
    )(y, t)


def _axpby(a, b, ca, cb, name, tm=256):
    s, d = a.shape
    tm = min(tm, s)

    def kern(a_ref, b_ref, o_ref):
        o_ref[...] = ca * a_ref[...] + cb * b_ref[...]

    row = pl.BlockSpec((tm, d), lambda i: (i, 0))
    return pl.pallas_call(
        kern, name=name, grid=(s // tm,), in_specs=[row, row], out_specs=row,
        out_shape=jax.ShapeDtypeStruct((s, d), F32), compiler_params=_cp(("parallel",)),
    )(a, b)


def _shift_down(xx, s, n):
    if s == 0:
        return xx[HALO:HALO + n]
    return pltpu.roll(xx, s, axis=0)[HALO:HALO + n]


def _shift_up(yy, s, n):
    if s == 0:
        return yy[0:n]
    return pltpu.roll(yy, yy.shape[0] - s, axis=0)[0:n]


def _prev_halo(tm):
    return lambda i: jnp.maximum(i * (tm // HALO) - 1, 0)


def _dwconv_fwd(x, x_cb0, c, w8, b, k, name, tm=512):
    s = x.shape[0]
    tm = min(tm, s)
    ph = _prev_halo(tm)

    def kern(x_ref, h_ref, w_ref, b_ref, y_ref):
        halo = jnp.where(pl.program_id(0) == 0, 0.0, h_ref[...].astype(F32))
        xx = jnp.concatenate([halo, x_ref[...].astype(F32)], axis=0)
        w = w_ref[...]
        acc = b_ref[...] + w[k - 1:k] * xx[HALO:]
        for j in range(k - 1):
            acc = acc + w[j:j + 1] * _shift_down(xx, k - 1 - j, tm)
        y_ref[...] = acc

    return pl.pallas_call(
        kern, name=name, grid=(s // tm, c // CONV_CB),
        in_specs=[pl.BlockSpec((tm, CONV_CB), lambda i, j: (i, x_cb0 + j)),
                  pl.BlockSpec((HALO, CONV_CB), lambda i, j: (ph(i), x_cb0 + j)),
                  pl.BlockSpec((SUBLANES, CONV_CB), lambda i, j: (0, j)),
                  pl.BlockSpec((1, CONV_CB), lambda i, j: (0, j))],
        out_specs=pl.BlockSpec((tm, CONV_CB), lambda i, j: (i, j)),
        out_shape=jax.ShapeDtypeStruct((s, c), F32), compiler_params=_cp(("parallel", "parallel")),
    )(x, x, w8, b.reshape(1, c))


def _dwconv_bwd(dy, x, x_cb0, c, w8, k, name, tm=512):
    s = x.shape[0]
    tm = min(tm, s)
    nt = s // tm
    ph = _prev_halo(tm)
    nh = lambda i: jnp.minimum((i + 1) * (tm // HALO), s // HALO - 1)

    def kern(dy_ref, dyn_ref, x_ref, h_ref, w_ref, dx_ref, dw_ref, db_ref):
        i = pl.program_id(1)
        halo = jnp.where(i == 0, 0.0, h_ref[...].astype(F32))
        xx = jnp.concatenate([halo, x_ref[...].astype(F32)], axis=0)
        dyt = dy_ref[...]
        nxt = jnp.where(i == nt - 1, 0.0, dyn_ref[...])
        dyy = jnp.concatenate([dyt, nxt], axis=0)
        w = w_ref[...]
        dx = w[k - 1:k] * dyt
        rows = [jnp.sum(dyt * _shift_down(xx, k - 1 - j, tm), 0, keepdims=True) for j in range(k)]
        for j in range(k - 1):
            dx = dx + w[j:j + 1] * _shift_up(dyy, k - 1 - j, tm)
        dx_ref[...] = dx.astype(BF16)
        dw = jnp.concatenate(rows + [jnp.zeros((SUBLANES - k, CONV_CB), F32)], axis=0)
        db = jnp.sum(dyt, 0, keepdims=True)

        @pl.when(i == 0)
        def _():
            dw_ref[...] = dw
            db_ref[...] = db

        @pl.when(i > 0)
        def _():
            dw_ref[...] += dw
            db_ref[...] += db

    dx, dw, db = pl.pallas_call(
        kern, name=name, grid=(c // CONV_CB, nt),
        in_specs=[pl.BlockSpec((tm, CONV_CB), lambda j, i: (i, j)),
                  pl.BlockSpec((HALO, CONV_CB), lambda j, i: (nh(i), j)),
                  pl.BlockSpec((tm, CONV_CB), lambda j, i: (i, x_cb0 + j)),
                  pl.BlockSpec((HALO, CONV_CB), lambda j, i: (ph(i), x_cb0 + j)),
                  pl.BlockSpec((SUBLANES, CONV_CB), lambda j, i: (0, j))],
        out_specs=[pl.BlockSpec((tm, CONV_CB), lambda j, i: (i, j)),
                   pl.BlockSpec((SUBLANES, CONV_CB), lambda j, i: (0, j)),
                   pl.BlockSpec((1, CONV_CB), lambda j, i: (0, j))],
        out_shape=[jax.ShapeDtypeStruct((s, c), BF16), jax.ShapeDtypeStruct((SUBLANES, c), F32),
                   jax.ShapeDtypeStruct((1, c), F32)],
        compiler_params=_cp(("parallel", "arbitrary")),
    )(dy, dy, x, x, w8)
    return dx, dw, db[0]


def _silu(x):
    return x * jax.nn.sigmoid(x)


def _ffn_gate_fwd(hu, w8, b, name, tm=512):
    s = hu.shape[0]
    tm = min(tm, s)
    nb = FFN_DIM // CONV_CB
    ph = _prev_halo(tm)
    k = 3

    def conv(x_ref, h_ref, w_ref, b_ref):
        halo = jnp.where(pl.program_id(0) == 0, 0.0, h_ref[...].astype(F32))
        xx = jnp.concatenate([halo, x_ref[...].astype(F32)], axis=0)
        w = w_ref[...]
        acc = b_ref[...] + w[k - 1:k] * xx[HALO:]
        for j in range(k - 1):
            acc = acc + w[j:j + 1] * _shift_down(xx, k - 1 - j, tm)
        return acc

    def kern(g_ref, gh_ref, u_ref, uh_ref, wg_ref, wu_ref, bg_ref, bu_ref, a_ref):
        g = conv(g_ref, gh_ref, wg_ref, bg_ref)
        u = conv(u_ref, uh_ref, wu_ref, bu_ref)
        a_ref[...] = (_silu(g) * u).astype(BF16)

    main = lambda off: pl.BlockSpec((tm, CONV_CB), lambda i, j: (i, off + j))
    halo = lambda off: pl.BlockSpec((HALO, CONV_CB), lambda i, j: (ph(i), off + j))
    wsp = lambda off: pl.BlockSpec((SUBLANES, CONV_CB), lambda i, j: (0, off + j))
    bsp = lambda off: pl.BlockSpec((1, CONV_CB), lambda i, j: (0, off + j))
    b2 = b.reshape(1, 2 * FFN_DIM)
    return pl.pallas_call(
        kern, name=name, grid=(s // tm, nb),
        in_specs=[main(0), halo(0), main(nb), halo(nb), wsp(0), wsp(nb), bsp(0), bsp(nb)],
        out_specs=pl.BlockSpec((tm, CONV_CB), lambda i, j: (i, j)),
        out_shape=jax.ShapeDtypeStruct((s, FFN_DIM), BF16), compiler_params=_cp(("parallel", "parallel")),
    )(hu, hu, hu, hu, w8, w8, b2, b2)


def _ffn_gate_bwd(hu, da, w8, b, name, tm=512):
    s = hu.shape[0]
    tm = min(tm, s)
    nt = s // tm
    nb = FFN_DIM // CONV_CB
    ph = _prev_halo(tm)
    nh = lambda i: jnp.minimum((i + 1) * (tm // HALO), s // HALO - 1)
    k = 3
    ne = tm + HALO

    def kern(g_ref, gp_ref, gn_ref, u_ref, up_ref, un_ref, da_ref, dan_ref, wg_ref, wu_ref, bg_ref, bu_ref,
             dg_ref, du_ref, dwg_ref, dwu_ref, dbg_ref, dbu_ref):
        i = pl.program_id(1)

        def ext(x_ref, p_ref, n_ref):
            halo = jnp.where(i == 0, 0.0, p_ref[...].astype(F32))
            return jnp.concatenate([halo, x_ref[...].astype(F32), n_ref[...].astype(F32)], axis=0)

        def conv(xx, w, bb):
            acc = bb + w[k - 1:k] * xx[HALO:]
            for j in range(k - 1):
                acc = acc + w[j:j + 1] * _shift_down(xx, k - 1 - j, ne)
            return acc

        gx, ux = ext(g_ref, gp_ref, gn_ref), ext(u_ref, up_ref, un_ref)
        wg, wu = wg_ref[...], wu_ref[...]
        g = conv(gx, wg, bg_ref[...])
        u = conv(ux, wu, bu_ref[...])
        dae = jnp.concatenate([da_ref[...].astype(F32), jnp.where(i == nt - 1, 0.0, dan_ref[...].astype(F32))], axis=0)
        sg = jax.nn.sigmoid(g)
        dhg = dae * u * (sg * (1.0 + g * (1.0 - sg)))
        dhu = dae * (g * sg)

        def back(dh, xx, w):
            dx = w[k - 1:k] * dh[0:tm]
            for j in range(k - 1):
                dx = dx + w[j:j + 1] * _shift_up(dh, k - 1 - j, tm)
            rows = [jnp.sum(dh[0:tm] * _shift_down(xx, k - 1 - j, tm), 0, keepdims=True) for j in range(k)]
            dw = jnp.concatenate(rows + [jnp.zeros((SUBLANES - k, CONV_CB), F32)], axis=0)
            return dx, dw, jnp.sum(dh[0:tm], 0, keepdims=True)

        dxg, dwg, dbg = back(dhg, gx, wg)
        dxu, dwu, dbu = back(dhu, ux, wu)
        dg_ref[...] = dxg.astype(BF16)
        du_ref[...] = dxu.astype(BF16)

        @pl.when(i == 0)
        def _():
            dwg_ref[...] = dwg
            dwu_ref[...] = dwu
            dbg_ref[...] = dbg
            dbu_ref[...] = dbu

        @pl.when(i > 0)
        def _():
            dwg_ref[...] += dwg
            dwu_ref[...] += dwu
            dbg_ref[...] += dbg
            dbu_ref[...] += dbu

    main = lambda off: pl.BlockSpec((tm, CONV_CB), lambda j, i: (i, off + j))
    prev = lambda off: pl.BlockSpec((HALO, CONV_CB), lambda j, i: (ph(i), off + j))
    nxt = lambda off: pl.BlockSpec((HALO, CONV_CB), lambda j, i: (nh(i), off + j))
    wsp = lambda off: pl.BlockSpec((SUBLANES, CONV_CB), lambda j, i: (0, off + j))
    bsp = lambda off: pl.BlockSpec((1, CONV_CB), lambda j, i: (0, off + j))
    b2 = b.reshape(1, 2 * FFN_DIM)
    outs = pl.pallas_call(
        kern, name=name, grid=(nb, nt),
        in_specs=[main(0), prev(0), nxt(0), main(nb), prev(nb), nxt(nb), main(0), nxt(0), wsp(0), wsp(nb), bsp(0), bsp(nb)],
        out_specs=[main(0), main(0), wsp(0), wsp(0), bsp(0), bsp(0)],
        out_shape=[jax.ShapeDtypeStruct((s, FFN_DIM), BF16), jax.ShapeDtypeStruct((s, FFN_DIM), BF16),
                   jax.ShapeDtypeStruct((SUBLANES, FFN_DIM), F32), jax.ShapeDtypeStruct((SUBLANES, FFN_DIM), F32),
                   jax.ShapeDtypeStruct((1, FFN_DIM), F32), jax.ShapeDtypeStruct((1, FFN_DIM), F32)],
        compiler_params=_cp(("parallel", "arbitrary")),
    )(hu, hu, hu, hu, hu, hu, da, da, w8, w8, b2, b2)
    dg, du, dwg, dwu, dbg, dbu = outs
    return dg, du, jnp.concatenate([dwg, dwu], axis=1), jnp.concatenate([dbg[0], dbu[0]])


def _iota(shape, dim):
    return lax.broadcasted_iota(jnp.int32, shape, dim)


def _in_range(idx, start, size):
    return jnp.logical_and(idx >= start, idx < start + size)


def _expand8(v, e):
    return jnp.sum(_hdot(jnp.broadcast_to(v, (SUBLANES, v.shape[1])), e), 0, keepdims=True) * (1.0 / SUBLANES)


def _ssd_consts():
    tri = np.tril(np.ones((SSD_CHUNK, SSD_CHUNK), np.float32))
    e = np.zeros((LANES, SSD_D_INNER), np.float32)
    for h in range(SSD_HEADS):
        e[h, 64 * h:64 * h + 64] = 1.0
    bd = np.zeros((LANES, SSD_D_INNER), np.float32)
    bd[:64, :256] = 1.0
    bd[64:, 256:] = 1.0
    return jnp.asarray(tri), jnp.asarray(e), jnp.asarray(bd)


def _ssd_chunk(tri, e, bd, z, xs_p, bm_p, cm_p, dtr, st, dtb, alog, dsk, nw):
    n = z.shape[0]
    lane128, lane512 = _iota((1, LANES), 1), _iota((1, SSD_D_INNER), 1)
    sub128 = _iota((LANES, 1), 0)
    tril = _iota((n, 1), 0) >= _iota((1, n), 1)
    xs, bm, cm = _silu(xs_p), _silu(bm_p), _silu(cm_p)
    dt = jax.nn.softplus(dtr + dtb)
    da = dt * (-jnp.exp(alog))
    acs = _hdot(tri, da)
    acst = _hdot(da, tri, (((0,), (1,)), ((), ())))
    xc = xs * _hdot(dt, e)
    tot = jnp.sum(da, 0, keepdims=True)
    y = _bdot(cm, st) * _hdot(jnp.exp(acs), e)
    st_new = st * _expand8(jnp.exp(tot), e) + bd * _bdot(bm, xc * _hdot(jnp.exp(tot - acs), e), TN)
    for g in range(2):
        cb = _bdot(jnp.where(_in_range(lane128, 64 * g, 64), cm, 0.0), bm, NT)
        for h in range(4 * g, 4 * g + 4):
            col = jnp.sum(jnp.where(lane128 == h, acs, 0.0), 1, keepdims=True)
            row = jnp.sum(jnp.where(sub128 == h, acst, 0.0), 0, keepdims=True)
            dm = jnp.where(tril, jnp.exp(jnp.minimum(col - row, 0.0)), 0.0)
            y = y + _bdot(cb * dm, jnp.where(_in_range(lane512, 64 * h, 64), xc, 0.0))
    y = (y + _expand8(dsk, e) * xs) * _silu(z)
    ysq = y * y
    g0 = lane512 < 256
    ms0 = jnp.sum(jnp.where(g0, ysq, 0.0), 1, keepdims=True) * (1.0 / 256)
    ms1 = jnp.sum(jnp.where(g0, 0.0, ysq), 1, keepdims=True) * (1.0 / 256)
    rs = jnp.where(g0, lax.rsqrt(ms0 + RMS_EPS), lax.rsqrt(ms1 + RMS_EPS))
    return y * rs * nw, st_new


def _ssd_specs(rev, nc):
    ci = (lambda i: nc - 1 - i) if rev else (lambda i: i)
    n = SSD_CHUNK
    act = [pl.BlockSpec((n, 512), lambda i: (ci(i), AB_Z // 512)),
           pl.BlockSpec((n, 512), lambda i: (ci(i), 0)),
           pl.BlockSpec((n, LANES), lambda i: (ci(i), 4)),
           pl.BlockSpec((n, LANES), lambda i: (ci(i), 5)),
           pl.BlockSpec((n, LANES), lambda i: (ci(i), AB_DT // LANES))]
    const = [pl.BlockSpec((n, n), lambda i: (0, 0)), pl.BlockSpec((LANES, 512), lambda i: (0, 0)),
             pl.BlockSpec((LANES, 512), lambda i: (0, 0))]
    par = [pl.BlockSpec((1, LANES), lambda i: (0, 0))] * 3 + [pl.BlockSpec((1, 512), lambda i: (0, 0))]
    return ci, act, const, par


def _ssd_fwd(h, xbcc, dtb, alog, dsk, nw, name):
    s = h.shape[0]
    nc = s // SSD_CHUNK
    ci, act, const, par = _ssd_specs(False, nc)

    def kern(z_ref, xs_ref, bm_ref, cm_ref, dt_ref, tri_ref, e_ref, bd_ref, dtb_ref, al_ref, dsk_ref, nw_ref,
             y_ref, sts_ref, st):
        @pl.when(pl.program_id(0) == 0)
        def _():
            st[...] = jnp.zeros_like(st)

        sts_ref[0] = st[...]
        y, stn = _ssd_chunk(tri_ref[...], e_ref[...], bd_ref[...], z_ref[...], xs_ref[...], bm_ref[...], cm_ref[...],
                            dt_ref[...], st[...], dtb_ref[...], al_ref[...], dsk_ref[...], nw_ref[...])
        y_ref[...] = y.astype(BF16)
        st[...] = stn

    return pl.pallas_call(
        kern, name=name, grid=(nc,), in_specs=act + const + par,
        out_specs=[pl.BlockSpec((SSD_CHUNK, 512), lambda i: (i, 0)), pl.BlockSpec((1, LANES, 512), lambda i: (i, 0, 0))],
        out_shape=[jax.ShapeDtypeStruct((s, 512), BF16), jax.ShapeDtypeStruct((nc, LANES, 512), F32)],
        scratch_shapes=[pltpu.VMEM((LANES, 512), F32)], compiler_params=_cp(("arbitrary",)),
    )(h, xbcc, xbcc, xbcc, h, *_ssd_consts(), dtb, alog, dsk, nw)


def _acc(ref, val, first):
    @pl.when(first)
    def _():
        ref[...] = val

    @pl.when(jnp.logical_not(first))
    def _():
        ref[...] += val


def _ssd_bwd(dy, dy_cb, h, xbcc, sts, dtb, alog, dsk, nw, name):
    s = h.shape[0]
    nc = s // SSD_CHUNK
    n = SSD_CHUNK
    ci, act, const, par = _ssd_specs(True, nc)

    def kern(dy_ref, sts_ref, z_ref, xs_ref, bm_ref, cm_ref, dt_ref, tri_ref, e_ref, bd_ref, dtb_ref, al_ref, dsk_ref,
             nw_ref, dz_ref, dx_ref, ddt_ref, ddtb_ref, dal_ref, ddsk_ref, dnw_ref, dst):
        first = pl.program_id(0) == 0

        @pl.when(first)
        def _():
            dst[...] = jnp.zeros_like(dst)

        f = functools.partial(_ssd_chunk, tri_ref[...], e_ref[...], bd_ref[...])
        _, pull = jax.vjp(f, z_ref[...], xs_ref[...], bm_ref[...], cm_ref[...], dt_ref[...], sts_ref[0],
                          dtb_ref[...], al_ref[...], dsk_ref[...], nw_ref[...])
        dz, dxs, dbm, dcm, ddt, dsti, ddtb, dal, ddsk, dnw = pull((dy_ref[...], dst[...]))
        dz_ref[...] = dz.astype(BF16)
        dx_ref[:, 0:512] = dxs
        dx_ref[:, 512:640] = dbm
        dx_ref[:, 640:768] = dcm
        ddt_ref[...] = ddt.astype(BF16)
        dst[...] = dsti
        _acc(ddtb_ref, ddtb, first)
        _acc(dal_ref, dal, first)
        _acc(ddsk_ref, ddsk, first)
        _acc(dnw_ref, dnw, first)

    vec = pl.BlockSpec((1, LANES), lambda i: (0, 0))
    return pl.pallas_call(
        kern, name=name, grid=(nc,),
        in_specs=[pl.BlockSpec((n, 512), lambda i: (ci(i), dy_cb)), pl.BlockSpec((1, LANES, 512), lambda i: (ci(i), 0, 0))]
        + act + const + par,
        out_specs=[pl.BlockSpec((n, 512), lambda i: (ci(i), 0)), pl.BlockSpec((n, SSD_CONV_DIM), lambda i: (ci(i), 0)),
                   pl.BlockSpec((n, LANES), lambda i: (ci(i), 0)), vec, vec, vec, pl.BlockSpec((1, 512), lambda i: (0, 0))],
        out_shape=[jax.ShapeDtypeStruct((s, 512), BF16), jax.ShapeDtypeStruct((s, SSD_CONV_DIM), F32),
                   jax.ShapeDtypeStruct((s, LANES), BF16), jax.ShapeDtypeStruct((1, LANES), F32),
                   jax.ShapeDtypeStruct((1, LANES), F32), jax.ShapeDtypeStruct((1, LANES), F32),
                   jax.ShapeDtypeStruct((1, 512), F32)],
        scratch_shapes=[pltpu.VMEM((LANES, 512), F32)], compiler_params=_cp(("arbitrary",)),
    )(dy, sts, h, xbcc, xbcc, xbcc, h, *_ssd_consts(), dtb, alog, dsk, nw)


HG_LEVELS = 6


def _hg_consts():
    n = HG_CHUNK
    t = np.arange(n)
    mats = [np.tril(np.ones((n, n), np.float32))]
    dq, dk, mk = [], [], []
    for lv in range(HG_LEVELS):
        b = 1 << lv
        start = (t // b) * b
        end = start + b - 1
        dq.append(((t[None, :] >= start[:, None]) & (t[None, :] <= t[:, None])).astype(np.float32))
        dk.append(((t[None, :] > t[:, None]) & (t[None, :] <= end[:, None])).astype(np.float32))
        mk.append((((t // b) % 2 == 1)[:, None] & ((t // b)[None, :] == (t // b)[:, None] - 1)).astype(np.float32))
    return jnp.asarray(np.stack(mats + dq + dk + mk))


def _hg_head(c, hq, hf, hi, hg, st, lb, nw):
    q = _silu(hq)
    g = jnp.log(lb + (1.0 - lb) * jax.nn.sigmoid(hf))
    k = (1.0 - lb) * jax.nn.sigmoid(-hf)
    bc = _hdot(c[0], g)
    att = None
    for lv in range(HG_LEVELS):
        qt = q * jnp.exp(_hdot(c[1 + lv], g))
        kt = k * jnp.exp(_hdot(c[1 + HG_LEVELS + lv], g))
        t = c[1 + 2 * HG_LEVELS + lv] * _bdot(qt, kt, NT)
        att = t if att is None else att + t
    o = _bdot(att, hi) + jnp.sum(q * k, 1, keepdims=True) * hi + _bdot(q * jnp.exp(bc), st, NT)
    tot = jnp.sum(g, 0, keepdims=True)
    st_new = st * jnp.exp(tot) + _bdot(hi, k * jnp.exp(tot - bc), TN)
    on = o * lax.rsqrt(jnp.mean(o * o, 1, keepdims=True) + RMS_EPS) * nw * _silu(hg)
    return on, st_new


def _hg_specs(rev, nc):
    ci = (lambda i: nc - 1 - i) if rev else (lambda i: i)
    n = HG_CHUNK
    act = [pl.BlockSpec((n, 512), functools.partial(lambda i, c: (ci(i), c), c=c))
           for c in (AB_HQ // 512, AB_HF // 512, AB_HI // 512, AB_HG // 512)]
    other = [pl.BlockSpec((1 + 3 * HG_LEVELS, n, n), lambda i: (0, 0, 0)), pl.BlockSpec((1, 512), lambda i: (0, 0)),
             pl.BlockSpec((1, LANES), lambda i: (0, 0))]
    return ci, act, other


def _hg_fwd(h, lb, nw, name):
    s = h.shape[0]
    nc = s // HG_CHUNK
    ci, act, other = _hg_specs(False, nc)

    def kern(q_ref, f_ref, i_ref, g_ref, c_ref, lb_ref, nw_ref, y_ref, sts_ref, st):
        @pl.when(pl.program_id(0) == 0)
        def _():
            st[...] = jnp.zeros_like(st)

        sts_ref[0] = st[...]
        c = [c_ref[j] for j in range(1 + 3 * HG_LEVELS)]
        for hd in range(HG_HEADS):
            ln = slice(LANES * hd, LANES * (hd + 1))
            on, stn = _hg_head(c, q_ref[:, ln], f_ref[:, ln], i_ref[:, ln], g_ref[:, ln], st[ln, :], lb_ref[:, ln], nw_ref[...])
            y_ref[:, ln] = on.astype(BF16)
            st[ln, :] = stn

    return pl.pallas_call(
        kern, name=name, grid=(nc,), in_specs=act + other,
        out_specs=[pl.BlockSpec((HG_CHUNK, 512), lambda i: (i, 0)), pl.BlockSpec((1, 512, LANES), lambda i: (i, 0, 0))],
        out_shape=[jax.ShapeDtypeStruct((s, 512), BF16), jax.ShapeDtypeStruct((nc, 512, LANES), F32)],
        scratch_shapes=[pltpu.VMEM((512, LANES), F32)], compiler_params=_cp(("arbitrary",)),
    )(h, h, h, h, _hg_consts(), lb, nw)


def _hg_bwd(dy, dy_cb, h, sts, lb, nw, name):
    s = h.shape[0]
    nc = s // HG_CHUNK
    n = HG_CHUNK
    ci, act, other = _hg_specs(True, nc)

    def kern(dy_ref, sts_ref, q_ref, f_ref, i_ref, g_ref, c_ref, lb_ref, nw_ref,
             dq_ref, df_ref, di_ref, dg_ref, dlb_ref, dnw_ref, dst):
        first = pl.program_id(0) == 0

        @pl.when(first)
        def _():
            dst[...] = jnp.zeros_like(dst)
            dnw_ref[...] = jnp.zeros_like(dnw_ref)

        c = [c_ref[j] for j in range(1 + 3 * HG_LEVELS)]
        for hd in range(HG_HEADS):
            ln = slice(LANES * hd, LANES * (hd + 1))
            _, pull = jax.vjp(functools.partial(_hg_head, c), q_ref[:, ln], f_ref[:, ln], i_ref[:, ln], g_ref[:, ln],
                              sts_ref[0, ln, :], lb_ref[:, ln], nw_ref[...])
            dq, df, di, dg, dsti, dlb, dnw = pull((dy_ref[:, ln], dst[ln, :]))
            dq_ref[:, ln] = dq.astype(BF16)
            df_ref[:, ln] = df.astype(BF16)
            di_ref[:, ln] = di.astype(BF16)
            dg_ref[:, ln] = dg.astype(BF16)
            dst[ln, :] = dsti
            dnw_ref[...] += dnw

            @pl.when(first)
            def _():
                dlb_ref[:, ln] = dlb

            @pl.when(jnp.logical_not(first))
            def _():
                dlb_ref[:, ln] += dlb

    blk = pl.BlockSpec((n, 512), lambda i: (ci(i), 0))
    return pl.pallas_call(
        kern, name=name, grid=(nc,),
        in_specs=[pl.BlockSpec((n, 512), lambda i: (ci(i), dy_cb)), pl.BlockSpec((1, 512, LANES), lambda i: (ci(i), 0, 0))]
        + act + other,
        out_specs=[blk, blk, blk, blk, pl.BlockSpec((1, 512), lambda i: (0, 0)), pl.BlockSpec((1, LANES), lambda i: (0, 0))],
        out_shape=[jax.ShapeDtypeStruct((s, 512), BF16)] * 4 + [jax.ShapeDtypeStruct((1, 512), F32),
                                                                jax.ShapeDtypeStruct((1, LANES), F32)],
        scratch_shapes=[pltpu.VMEM((512, LANES), F32)], compiler_params=_cp(("arbitrary",)),
    )(dy, sts, h, h, h, h, _hg_consts(), lb, nw)


def _swa_pair(pair, notfirst, qp, k0, v0, k1, v1, srow):
    n = qp.shape[0]
    rows, cols = _iota((n, 1), 0), _iota((1, n), 1)
    own = rows >= cols
    prev = jnp.logical_and(cols > rows, notfirst)
    lane = _iota((1, LANES), 1)
    out = None
    for e in range(2):
        lm = _in_range(lane, 64 * e, 64)
        qm = jnp.where(lm, qp, 0.0)
        s0 = jnp.where(prev, _bdot(qm, k0, NT) * 0.125, MASK_VALUE)
        s1 = jnp.where(own, _bdot(qm, k1, NT) * 0.125, MASK_VALUE)
        sk = jnp.sum(jnp.where(lane == pair + 4 * e, srow, 0.0), 1, keepdims=True)
        m = lax.stop_gradient(jnp.maximum(jnp.maximum(jnp.max(s0, 1, keepdims=True), jnp.max(s1, 1, keepdims=True)), sk))
        p0, p1 = jnp.exp(s0 - m), jnp.exp(s1 - m)
        inv = 1.0 / (jnp.sum(p0, 1, keepdims=True) + jnp.sum(p1, 1, keepdims=True) + jnp.exp(sk - m))
        t = _bdot(p0 * inv, jnp.where(lm, v0, 0.0)) + _bdot(p1 * inv, jnp.where(lm, v1, 0.0))
        out = t if out is None else out + t
    return out


def _swa_specs():
    n = SWA_BLOCK
    pv = lambda i: jnp.maximum(i - 1, 0)
    return [pl.BlockSpec((n, 512), lambda i: (i, CD_Q // 512)),
            pl.BlockSpec((n, LANES), lambda i: (pv(i), CD_K // LANES)), pl.BlockSpec((n, LANES), lambda i: (pv(i), CD_V // LANES)),
            pl.BlockSpec((n, LANES), lambda i: (i, CD_K // LANES)), pl.BlockSpec((n, LANES), lambda i: (i, CD_V // LANES)),
            pl.BlockSpec((1, LANES), lambda i: (0, 0))]


def _swa_fwd(hc, srow, name):
    s = hc.shape[0]
    n = SWA_BLOCK

    def kern(q_ref, k0_ref, v0_ref, k1_ref, v1_ref, s_ref, y_ref):
        notfirst = pl.program_id(0) > 0
        for p in range(4):
            ln = slice(LANES * p, LANES * (p + 1))
            y_ref[:, ln] = _swa_pair(p, notfirst, q_ref[:, ln], k0_ref[...], v0_ref[...], k1_ref[...], v1_ref[...],
                                     s_ref[...]).astype(BF16)

    return pl.pallas_call(
        kern, name=name, grid=(s // n,), in_specs=_swa_specs(), out_specs=pl.BlockSpec((n, 512), lambda i: (i, 0)),
        out_shape=jax.ShapeDtypeStruct((s, 512), BF16), compiler_params=_cp(("parallel",)),
    )(hc, hc, hc, hc, hc, srow)


def _swa_bwd(dy, dy_cb, hc, srow, name):
    s = hc.shape[0]
    n = SWA_BLOCK

    def kern(dy_ref, q_ref, k0_ref, v0_ref, k1_ref, v1_ref, s_ref, dq_ref, dk_ref, dv_ref, ds_ref):
        i = pl.program_id(0)
        notfirst = i > 0

        @pl.when(i == 0)
        def _():
            dk_ref[...] = jnp.zeros_like(dk_ref)
            dv_ref[...] = jnp.zeros_like(dv_ref)
            ds_ref[...] = jnp.zeros_like(ds_ref)

        r1 = pl.ds(pl.multiple_of(i * n, n), n)
        r0 = pl.ds(pl.multiple_of(jnp.maximum(i - 1, 0) * n, n), n)
        for p in range(4):
            ln = slice(LANES * p, LANES * (p + 1))
            _, pull = jax.vjp(functools.partial(_swa_pair, p, notfirst), q_ref[:, ln], k0_ref[...], v0_ref[...],
                              k1_ref[...], v1_ref[...], s_ref[...])
            dq, dk0, dv0, dk1, dv1, dsr = pull(dy_ref[:, ln])
            dq_ref[:, ln] = dq.astype(BF16)
            dk_ref[r0, :] += dk0
            dv_ref[r0, :] += dv0
            dk_ref[r1, :] += dk1
            dv_ref[r1, :] += dv1
            ds_ref[...] += dsr

    full = pl.BlockSpec((s, LANES), lambda i: (0, 0))
    return pl.pallas_call(
        kern, name=name, grid=(s // n,), in_specs=[pl.BlockSpec((n, 512), lambda i: (i, dy_cb))] + _swa_specs(),
        out_specs=[pl.BlockSpec((n, 512), lambda i: (i, 0)), full, full, pl.BlockSpec((1, LANES), lambda i: (0, 0))],
        out_shape=[jax.ShapeDtypeStruct((s, 512), BF16), jax.ShapeDtypeStruct((s, LANES), F32),
                   jax.ShapeDtypeStruct((s, LANES), F32), jax.ShapeDtypeStruct((1, LANES), F32)],
        compiler_params=_cp(("arbitrary",)),
    )(dy, hc, hc, hc, hc, hc, srow)


RG_TILE = 256


def _neg_expm1(x):
    ser = x * (1 + x / 2 * (1 + x / 3 * (1 + x / 4 * (1 + x / 5 * (1 + x / 6 * (1 + x / 7 * (1 + x / 8)))))))
    return jnp.where(x > -0.25, -ser, 1.0 - jnp.exp(x))


def _rg_gates(xc, wa, wx, ba, bx, lam):
    r = jax.nn.sigmoid(_bdot(xc, wa) + ba)
    i = jax.nn.sigmoid(_bdot(xc, wx) + bx)
    log_a = -RG_C * r * jax.nn.softplus(-lam)
    u = jnp.sqrt(jnp.maximum(_neg_expm1(2.0 * log_a), 0.0)) * (i * xc)
    return jnp.exp(log_a), u


def _rg_out(hs, gate):
    return hs * jax.nn.gelu(gate)


def _rows_to_tile(rows):
    sub = _iota((SUBLANES, 1), 0)
    out = jnp.broadcast_to(rows[0], (SUBLANES, rows[0].shape[1]))
    for j in range(1, SUBLANES):
        out = jnp.where(sub == j, rows[j], out)
    return out


def _rg_fwd(xc, hc, wa, wx, ba, bx, lam, name):
    s = xc.shape[0]
    tm = min(RG_TILE, s)

    def kern(x_ref, g_ref, wa_ref, wx_ref, ba_ref, bx_ref, lam_ref, y_ref, h_ref, a_s, u_s, carry):
        @pl.when(pl.program_id(0) == 0)
        def _():
            carry[...] = jnp.zeros_like(carry)

        a, u = _rg_gates(x_ref[...], wa_ref[...], wx_ref[...], ba_ref[...], bx_ref[...], lam_ref[...])
        a_s[...] = a
        u_s[...] = u

        def body(gi, hp):
            r0 = pl.multiple_of(gi * SUBLANES, SUBLANES)
            ab, ub = a_s[pl.ds(r0, SUBLANES), :], u_s[pl.ds(r0, SUBLANES), :]
            rows = []
            for j in range(SUBLANES):
                hp = ab[j:j + 1] * hp + ub[j:j + 1]
                rows.append(hp)
            h_ref[pl.ds(r0, SUBLANES), :] = _rows_to_tile(rows)
            return hp

        carry[0:1, :] = lax.fori_loop(0, tm // SUBLANES, body, carry[0:1, :])
        y_ref[...] = _rg_out(h_ref[...], g_ref[...]).astype(BF16)

    row = pl.BlockSpec((tm, 512), lambda i: (i, 0))
    mat = pl.BlockSpec((512, 512), lambda i: (0, 0))
    vec = pl.BlockSpec((1, 512), lambda i: (0, 0))
    return pl.pallas_call(
        kern, name=name, grid=(s // tm,),
        in_specs=[row, pl.BlockSpec((tm, 512), lambda i: (i, CD_GATE // 512)), mat, mat, vec, vec, vec],
        out_specs=[row, row],
        out_shape=[jax.ShapeDtypeStruct((s, 512), BF16), jax.ShapeDtypeStruct((s, 512), F32)],
        scratch_shapes=[pltpu.VMEM((tm, 512), F32), pltpu.VMEM((tm, 512), F32), pltpu.VMEM((SUBLANES, 512), F32)],
        compiler_params=_cp(("arbitrary",)),
    )(xc, hc, wa, wx, ba, bx, lam)


def _rg_bwd(dy, dy_cb, xc, hc, hs, wa, wx, ba, bx, lam, name):
    s = xc.shape[0]
    tm = min(RG_TILE, s)
    nt = s // tm
    ti = lambda i: nt - 1 - i

    def kern(dy_ref, x_ref, g_ref, h_ref, hp_ref, wa_ref, wx_ref, ba_ref, bx_ref, lam_ref,
             dx_ref, dg_ref, dwa_ref, dwx_ref, dba_ref, dbx_ref, dlam_ref, a_s, d_s, g_s, cg, ca):
        first = pl.program_id(0) == 0

        @pl.when(first)
        def _():
            cg[...] = jnp.zeros_like(cg)
            ca[...] = jnp.zeros_like(ca)

        (a, _), pull = jax.vjp(_rg_gates, x_ref[...], wa_ref[...].astype(F32), wx_ref[...].astype(F32), ba_ref[...],
                               bx_ref[...], lam_ref[...])
        hs_t = h_ref[...]
        _, pull_out = jax.vjp(_rg_out, hs_t, g_ref[...])
        dho, dgate = pull_out(dy_ref[...])
        dg_ref[...] = dgate.astype(BF16)
        a_s[...] = a
        d_s[...] = dho

        def body(k, c):
            gn, an = c
            r0 = pl.multiple_of((tm // SUBLANES - 1 - k) * SUBLANES, SUBLANES)
            ab, db = a_s[pl.ds(r0, SUBLANES), :], d_s[pl.ds(r0, SUBLANES), :]
            rows = [None] * SUBLANES
            for j in range(SUBLANES - 1, -1, -1):
                gn = db[j:j + 1] + an * gn
                an = ab[j:j + 1]
                rows[j] = gn
            g_s[pl.ds(r0, SUBLANES), :] = _rows_to_tile(rows)
            return gn, an

        gn, an = lax.fori_loop(0, tm // SUBLANES, body, (cg[0:1, :], ca[0:1, :]))
        cg[0:1, :] = gn
        ca[0:1, :] = an
        g = g_s[...]
        halo = jnp.where(pl.program_id(0) == nt - 1, 0.0, hp_ref[...])
        hprev = pltpu.roll(jnp.concatenate([halo, hs_t], axis=0), 1, axis=0)[SUBLANES:]
        dxc, dwa, dwx, dba, dbx, dlam = pull((g * hprev, g))
        dx_ref[...] = dxc
        _acc(dwa_ref, dwa, first)
        _acc(dwx_ref, dwx, first)
        _acc(dba_ref, dba, first)
        _acc(dbx_ref, dbx, first)
        _acc(dlam_ref, dlam, first)

    row = pl.BlockSpec((tm, 512), lambda i: (ti(i), 0))
    mat = pl.BlockSpec((512, 512), lambda i: (0, 0))
    vec = pl.BlockSpec((1, 512), lambda i: (0, 0))
    hprev = pl.BlockSpec((SUBLANES, 512), lambda i: (jnp.maximum(ti(i) * (tm // SUBLANES) - 1, 0), 0))
    return pl.pallas_call(
        kern, name=name, grid=(nt,),
        in_specs=[pl.BlockSpec((tm, 512), lambda i: (ti(i), dy_cb)), row,
                  pl.BlockSpec((tm, 512), lambda i: (ti(i), CD_GATE // 512)), row, hprev, mat, mat, vec, vec, vec],
        out_specs=[row, row, mat, mat, vec, vec, vec],
        out_shape=[jax.ShapeDtypeStruct((s, 512), F32), jax.ShapeDtypeStruct((s, 512), BF16),
                   jax.ShapeDtypeStruct((512, 512), F32), jax.ShapeDtypeStruct((512, 512), F32),
                   jax.ShapeDtypeStruct((1, 512), F32), jax.ShapeDtypeStruct((1, 512), F32), jax.ShapeDtypeStruct((1, 512), F32)],
        scratch_shapes=[pltpu.VMEM((tm, 512), F32)] * 3 + [pltpu.VMEM((SUBLANES, 512), F32)] * 2,
        compiler_params=_cp(("arbitrary",)),
    )(dy, xc, hc, hs, hs, wa, wx, ba, bx, lam)


AB_REF = dict(z=(0, 512), xbc=(512, 768), dt=(1280, 8), hq=(1288, 512), hf=(1800, 512), hi=(2312, 512), hg=(2824, 512))
AB_INT = dict(z=AB_Z, hq=AB_HQ, hf=AB_HF, hi=AB_HI, hg=AB_HG, xbc=AB_XBC, dt=AB_DT)
CD_REF = dict(q=(0, 512), k=(512, 128), v=(640, 128), gate=(768, 512), xr=(1280, 512))


def _cols(w, start, size):
    return lax.slice_in_dim(w, start, start + size, axis=w.ndim - 1)


def _pad_lanes(v):
    return jnp.concatenate([v.astype(F32), jnp.zeros((LANES - v.shape[0],), F32)]).reshape(1, LANES)


def _ab_in_to_int(w):
    parts = [_cols(w, *AB_REF[k]) for k in ("z", "hq", "hf", "hi", "hg", "xbc", "dt")]
    return jnp.concatenate(parts + [jnp.zeros(w.shape[:-1] + (AB_COLS - AB_DT - 8,), w.dtype)], axis=-1)


def _ab_in_to_ref(w):
    return jnp.concatenate([_cols(w, AB_INT[k], AB_REF[k][1]) for k in ("z", "xbc", "dt", "hq", "hf", "hi", "hg")], axis=-1)


def _pairs_to_int(w, axis):
    sl = lambda h: lax.slice_in_dim(w, 64 * h, 64 * h + 64, axis=axis)
    return jnp.concatenate([sl(h) for p in range(4) for h in (p, 4 + p)], axis=axis)


def _pairs_to_ref(w, axis):
    sl = lambda i: lax.slice_in_dim(w, 64 * i, 64 * i + 64, axis=axis)
    return jnp.concatenate([sl(2 * h) for h in range(4)] + [sl(2 * h + 1) for h in range(4)], axis=axis)


def _cd_in_to_int(w):
    q = _pairs_to_int(_cols(w, *CD_REF["q"]), w.ndim - 1)
    return jnp.concatenate([q] + [_cols(w, *CD_REF[k]) for k in ("gate", "xr", "k", "v")], axis=-1)


def _cd_in_to_ref(w):
    q = _pairs_to_ref(_cols(w, CD_Q, 512), w.ndim - 1)
    return jnp.concatenate([q, _cols(w, CD_K, 128), _cols(w, CD_V, 128), _cols(w, CD_GATE, 512), _cols(w, CD_XR, 512)], axis=-1)


def _block_diag(w):
    out = jnp.zeros((RG_WIDTH, RG_WIDTH), w.dtype)
    for g in range(8):
        out = lax.dynamic_update_slice(out, w[g], (64 * g, 64 * g))
    return out


def _diag_blocks(m):
    return jnp.stack([m[64 * g:64 * g + 64, 64 * g:64 * g + 64] for g in range(8)])


def _conv8(w):
    return jnp.concatenate([w, jnp.zeros((SUBLANES - w.shape[0], w.shape[1]), w.dtype)], axis=0)


def _hg_lower_bounds(hg_lower):
    sm = jax.nn.softmax(hg_lower.astype(F32), axis=0)
    return jnp.clip(jnp.cumsum(sm, axis=0) - sm[0], 0.0, 1.0)


def _local_step(x, target, w):
    s = x.shape[0]
    lb_all, lb_pull = jax.vjp(_hg_lower_bounds, w["hg_lower"])
    grads = {k: [None] * v.shape[0] for k, v in w.items()}
    saved = []
    xb = x.astype(BF16)
    for l in range(DEPTH):
        j = l // 2
        t = f"l{l}"
        sv = dict(x=x, xb=xb)
        if l % 2 == 0:
            win = _ab_in_to_int(w["ab_w_in"][j])
            wout = w["ab_w_out"][j]
            h = _mm([(xb, 0, win, 0, D_MODEL)], AB_COLS, F32, t + "_in")
            w8 = _conv8(w["ssd_conv_w"][j])
            xbcc = _dwconv_fwd(h, AB_XBC // CONV_CB, SSD_CONV_DIM, w8, w["ssd_conv_b"][j], 4, t + "_conv")
            sp = (_pad_lanes(w["ssd_dt_bias"][j]), _pad_lanes(w["ssd_a_log"][j]), _pad_lanes(w["ssd_d"][j]),
                  w["ssd_norm_w"][j].reshape(1, 512))
            ya, ssts = _ssd_fwd(h, xbcc, *sp, t + "_ssd")
            hp = (lb_all[j].reshape(1, 512), w["hg_norm_w"][j].reshape(1, LANES))
            yb, hsts = _hg_fwd(h, *hp, t + "_hg")
            sv.update(win=win, wout=wout, h=h, w8=w8, xbcc=xbcc, sp=sp, ssts=ssts, hp=hp, hsts=hsts)
        else:
            win = _cd_in_to_int(w["cd_w_in"][j])
            wout = jnp.concatenate([_pairs_to_int(w["cd_w_out"][j][:512], 0), w["cd_w_out"][j][512:]], axis=0)
            h = _mm([(xb, 0, win, 0, D_MODEL)], CD_COLS, F32, t + "_in")
            srow = _pad_lanes(w["swa_sinks"][j])
            ya = _swa_fwd(h, srow, t + "_swa")
            w8 = _conv8(w["rg_conv_w"][j])
            xc = _dwconv_fwd(h, CD_XR // CONV_CB, RG_WIDTH, w8, w["rg_conv_b"][j], 4, t + "_conv")
            rp = (_block_diag(w["rg_wa"][j]).astype(BF16), _block_diag(w["rg_wx"][j]).astype(BF16),
                  w["rg_ba"][j].reshape(1, 512), w["rg_bx"][j].reshape(1, 512), w["rg_lambda"][j].reshape(1, 512))
            yb, hs = _rg_fwd(xc, h, *rp, t + "_rg")
            sv.update(win=win, wout=wout, h=h, w8=w8, srow=srow, xc=xc, rp=rp, hs=hs)
        m = _mm([(ya, 0, wout, 0, 512), (yb, 0, wout, 1, 512)], D_MODEL, F32, t + "_out")
        x1, x1b, r1 = _ln_fwd(x, m, w["ln_g"][l, 0], w["ln_b"][l, 0], t + "_ln1")
        wup, wdown = w["ffn_w_up"][l], w["ffn_w_down"][l]
        fw8 = _conv8(w["ffn_conv_w"][l])
        hu = _mm([(x1b, 0, wup, 0, D_MODEL)], 2 * FFN_DIM, BF16, t + "_up")
        a = _ffn_gate_fwd(hu, fw8, w["ffn_conv_b"][l], t + "_gate")
        f = _mm([(a, 0, wdown, 0, FFN_DIM)], D_MODEL, F32, t + "_down")
        x, xb, r2 = _ln_fwd(x1, f, w["ln_g"][l, 1], w["ln_b"][l, 1], t + "_ln2")
        sv.update(ya=ya, yb=yb, r1=r1, x1b=x1b, hu=hu, a=a, r2=r2, fw8=fw8)
        saved.append(sv)

    loss = 0.5 * jnp.sum(_sqerr(x, target, "sqerr")) / D_MODEL
    dya, dyb, ca, cb = x, target, 1.0 / D_MODEL, -1.0 / D_MODEL
    for l in range(DEPTH - 1, -1, -1):
        j = l // 2
        t = f"l{l}b"
        sv = saved[l]
        wup, wdown = w["ffn_w_up"][l], w["ffn_w_down"][l]
        dr2, dg2, db2 = _ln_bwd(dya, dyb, ca, cb, sv["r2"], w["ln_g"][l, 1], t + "_ln2")
        da = _mm([(dr2, 0, wdown, 0, D_MODEL)], FFN_DIM, BF16, t + "_down", trans_b=True)
        grads["ffn_w_down"][l] = _mm_tn(sv["a"], 0, FFN_DIM, dr2, 0, D_MODEL, t + "_wdown")
        dhg, dhu, dfw8, dfb = _ffn_gate_bwd(sv["hu"], da, sv["fw8"], w["ffn_conv_b"][l], t + "_gate")
        grads["ffn_conv_w"][l] = dfw8[:3]
        grads["ffn_conv_b"][l] = dfb
        dx1 = _mm([(dhg, 0, wup, 0, FFN_DIM), (dhu, 0, wup, 1, FFN_DIM)], D_MODEL, F32, t + "_up", trans_b=True)
        grads["ffn_w_up"][l] = jnp.concatenate([_mm_tn(sv["x1b"], 0, D_MODEL, dhg, 0, FFN_DIM, t + "_wup_g"),
                                                _mm_tn(sv["x1b"], 0, D_MODEL, dhu, 0, FFN_DIM, t + "_wup_u")], axis=1)
        dr1, dg1, db1 = _ln_bwd(dr2, dx1, ALPHA, 1.0, sv["r1"], w["ln_g"][l, 0], t + "_ln1")
        grads["ln_g"][l] = jnp.stack([dg1, dg2])
        grads["ln_b"][l] = jnp.stack([db1, db2])
        win, wout, h = sv["win"], sv["wout"], sv["h"]
        dycat = _mm([(dr1, 0, wout, 0, D_MODEL)], D_MODEL, F32, t + "_out", trans_b=True)
        dwout = jnp.concatenate([_mm_tn(sv["ya"], 0, 512, dr1, 0, D_MODEL, t + "_wout_a"),
                                 _mm_tn(sv["yb"], 0, 512, dr1, 0, D_MODEL, t + "_wout_b")], axis=0)
        if l % 2 == 0:
            dz, dxbcc, ddt, ddtb, dal, ddsk, dnw = _ssd_bwd(dycat, 0, h, sv["xbcc"], sv["ssts"], *sv["sp"], t + "_ssd")
            dxbc, dcw, dcb = _dwconv_bwd(dxbcc, h, AB_XBC // CONV_CB, SSD_CONV_DIM, sv["w8"], 4, t + "_conv")
            dq, df, di, dg, dlb, dhnw = _hg_bwd(dycat, 1, h, sv["hsts"], *sv["hp"], t + "_hg")
            grads["ab_w_out"][j] = dwout
            grads["ssd_conv_w"][j], grads["ssd_conv_b"][j] = dcw[:4], dcb
            grads["ssd_dt_bias"][j], grads["ssd_a_log"][j], grads["ssd_d"][j] = ddtb[0, :8], dal[0, :8], ddsk[0, :8]
            grads["ssd_norm_w"][j], grads["hg_norm_w"][j] = dnw[0], dhnw[0]
            grads["hg_lower"][j] = dlb[0]
            pieces = [(dz, 0, AB_Z, 512), (dq, 0, AB_HQ, 512), (df, 0, AB_HF, 512), (di, 0, AB_HI, 512), (dg, 0, AB_HG, 512),
                      (dxbc, 0, AB_XBC, 256), (dxbc, 1, AB_XBC + 256, 256), (dxbc, 2, AB_XBC + 512, 256), (ddt, 0, AB_DT, LANES)]
            dwin = [_mm_tn(sv["xb"], 0, D_MODEL, p, 0, p.shape[1], t + f"_win{i}")
                    for i, p in enumerate((dz, dq, df, di, dg, dxbc, ddt))]
            grads["ab_w_in"][j] = _ab_in_to_ref(jnp.concatenate(dwin, axis=1))
        else:
            dq, dk, dv, dsr = _swa_bwd(dycat, 0, h, sv["srow"], t + "_swa")
            dxc, dgate, dwa, dwx, dba, dbx, dlam = _rg_bwd(dycat, 1, sv["xc"], h, sv["hs"], *sv["rp"], t + "_rg")
            dxr, dcw, dcb = _dwconv_bwd(dxc, h, CD_XR // CONV_CB, RG_WIDTH, sv["w8"], 4, t + "_conv")
            grads["cd_w_out"][j] = jnp.concatenate([_pairs_to_ref(dwout[:512], 0), dwout[512:]], axis=0)
            grads["swa_sinks"][j] = dsr[0, :8]
            grads["rg_conv_w"][j], grads["rg_conv_b"][j] = dcw[:4], dcb
            grads["rg_wa"][j], grads["rg_wx"][j] = _diag_blocks(dwa), _diag_blocks(dwx)
            grads["rg_ba"][j], grads["rg_bx"][j], grads["rg_lambda"][j] = dba[0], dbx[0], dlam[0]
            pieces = [(dq, 0, CD_Q, 512), (dgate, 0, CD_GATE, 512), (dxr, 0, CD_XR, 512), (dk, 0, CD_K, LANES), (dv, 0, CD_V, LANES)]
            dwin = [_mm_tn(sv["xb"], 0, D_MODEL, p, 0, p.shape[1], t + f"_win{i}")
                    for i, p in enumerate((dq, dgate, dxr, dk, dv))]
            grads["cd_w_in"][j] = _cd_in_to_ref(jnp.concatenate(dwin, axis=1))
        dx0 = _mm([(p, pcb, win, off // k, k) for p, pcb, off, k in pieces], D_MODEL, F32, t + "_in", trans_b=True)
        dya, dyb, ca, cb = dr1, dx0, ALPHA, 1.0
    grad_x = _axpby(dya, dyb, ca, cb, "grad_x")
    out = {k: jnp.stack(v) for k, v in grads.items()}
    (out["hg_lower"],) = lb_pull(out["hg_lower"])
    return loss, grad_x, out


MESH = pl.DeviceIdType.MESH
ANY = pl.BlockSpec(memory_space=pl.ANY)
PACK_ROWS = 2048


def _place():
    x, y, c = lax.axis_index("x"), lax.axis_index("y"), lax.axis_index("c")
    return x, y, c, [(1 - x, y), (x, 1 - y), (1 - x, 1 - y)]


def _rcopy(src, dst, send_sems, recv_sems, k, to):
    return pltpu.make_async_remote_copy(src_ref=src, dst_ref=dst, send_sem=send_sems.at[k], recv_sem=recv_sems.at[k],
                                        device_id=to, device_id_type=MESH)


def _gather_chips(wp, name):
    def body(w_ref, out_ref, send_sems, recv_sems, local_sem):
        x, y, c, chips = _place()
        me, sib = 2 * x + y, (x, y, 1 - c)
        mine = pltpu.make_async_copy(w_ref, out_ref.at[me], local_sem)
        mine.start()
        first = [_rcopy(w_ref.at[c], out_ref.at[me, c], send_sems, recv_sems, k, (cx, cy, c)) for k, (cx, cy) in enumerate(chips)]
        for cp in first:
            cp.start()
        passed = []
        for k, (cx, cy) in enumerate(chips):
            blk = out_ref.at[2 * cx + cy, c]
            _rcopy(blk, blk, send_sems, recv_sems, k, (cx, cy, c)).wait_recv()
            passed.append(_rcopy(blk, blk, send_sems, recv_sems, 3 + k, sib))
            passed[-1].start()
        for k, (cx, cy) in enumerate(chips):
            blk = out_ref.at[2 * cx + cy, 1 - c]
            _rcopy(blk, blk, send_sems, recv_sems, 3 + k, sib).wait_recv()
        for cp in first + passed:
            cp.wait_send()
        mine.wait()

    return pl.pallas_call(
        body, name=name, in_specs=[ANY], out_specs=ANY, out_shape=jax.ShapeDtypeStruct((4,) + wp.shape, wp.dtype),
        scratch_shapes=[pltpu.SemaphoreType.DMA((6,)), pltpu.SemaphoreType.DMA((6,)), pltpu.SemaphoreType.DMA(())],
    )(wp)


def _swap_halves(g, name):
    def body(g_ref, x_ref, send_sems, recv_sems):
        x, y, c, _ = _place()
        cp = _rcopy(g_ref.at[:, 1 - c], x_ref, send_sems, recv_sems, 0, (x, y, 1 - c))
        cp.start()
        cp.wait()

    return pl.pallas_call(
        body, name=name, in_specs=[ANY], out_specs=ANY,
        out_shape=jax.ShapeDtypeStruct((g.shape[0],) + g.shape[2:], g.dtype),
        scratch_shapes=[pltpu.SemaphoreType.DMA((1,)), pltpu.SemaphoreType.DMA((1,))],
    )(g)


def _scatter_chips(hb, name):
    def body(h_ref, y_ref, send_sems, recv_sems, local_sem):
        x, y, c, chips = _place()
        me = 2 * x + y
        mine = pltpu.make_async_copy(h_ref.at[me], y_ref.at[me], local_sem)
        mine.start()
        sends = [_rcopy(h_ref.at[2 * cx + cy], y_ref.at[me], send_sems, recv_sems, k, (cx, cy, c)) for k, (cx, cy) in enumerate(chips)]
        for cp in sends:
            cp.start()
        for k, (cx, cy) in enumerate(chips):
            blk = y_ref.at[2 * cx + cy]
            _rcopy(blk, blk, send_sems, recv_sems, k, (cx, cy, c)).wait_recv()
        for cp in sends:
            cp.wait_send()
        mine.wait()

    return pl.pallas_call(
        body, name=name, in_specs=[ANY], out_specs=ANY, out_shape=jax.ShapeDtypeStruct(hb.shape, hb.dtype),
        scratch_shapes=[pltpu.SemaphoreType.DMA((3,)), pltpu.SemaphoreType.DMA((3,)), pltpu.SemaphoreType.DMA(())],
    )(hb)


def _share_half(f, name):
    def body(f_ref, out_ref, send_sems, recv_sems, local_sem):
        x, y, c, _ = _place()
        sib = (x, y, 1 - c)
        mine = pltpu.make_async_copy(f_ref, out_ref.at[c], local_sem)
        mine.start()
        cp = _rcopy(f_ref, out_ref.at[c], send_sems, recv_sems, 0, sib)
        cp.start()
        _rcopy(f_ref, out_ref.at[1 - c], send_sems, recv_sems, 0, sib).wait_recv()
        cp.wait_send()
        mine.wait()

    return pl.pallas_call(
        body, name=name, in_specs=[ANY], out_specs=ANY, out_shape=jax.ShapeDtypeStruct((2,) + f.shape, f.dtype),
        scratch_shapes=[pltpu.SemaphoreType.DMA((1,)), pltpu.SemaphoreType.DMA((1,)), pltpu.SemaphoreType.DMA(())],
    )(f)


def _gather_devices(v, name):
    m_per, n = v.shape

    def body(x_ref, out_ref, send_sems, recv_sems, local_sem):
        x, y, c, chips = _place()
        me, sib = (x, y, c), (x, y, 1 - c)

        def rows(px, py, pc):
            return out_ref.at[pl.ds((4 * px + 2 * py + pc) * m_per, m_per), :]

        def copy(k, block, to, src=None):
            return _rcopy(rows(*block) if src is None else src, rows(*block), send_sems, recv_sems, k, to)

        mine = pltpu.make_async_copy(x_ref, rows(*me), local_sem)
        mine.start()
        first = [copy(0, me, sib, src=x_ref)]
        first += [copy(1 + j, me, (*chip, c), src=x_ref) for j, chip in enumerate(chips)]
        for cp in first:
            cp.start()
        passed = [copy(4 + j, (*chip, c), sib) for j, chip in enumerate(chips)]
        for j, chip in enumerate(chips):
            copy(1 + j, (*chip, c), me).wait_recv()
            passed[j].start()
        copy(0, sib, me).wait_recv()
        for j, chip in enumerate(chips):
            copy(4 + j, (*chip, 1 - c), me).wait_recv()
        for cp in first + passed:
            cp.wait_send()
        mine.wait()

    return pl.pallas_call(
        body, name=name, out_shape=jax.ShapeDtypeStruct((8 * m_per, n), v.dtype),
        in_specs=[pl.BlockSpec(memory_space=pltpu.VMEM)], out_specs=pl.BlockSpec(memory_space=pltpu.VMEM),
        scratch_shapes=[pltpu.SemaphoreType.DMA((7,)), pltpu.SemaphoreType.DMA((7,)), pltpu.SemaphoreType.DMA(())],
        compiler_params=_cp(),
    )(v)


def _add_half(g, xr, c, name, tr=2048):
    rh = g.shape[2]

    def kern(c_ref, g_ref, x_ref, o_ref):
        o_ref[...] = (g_ref[0] + x_ref[...]).astype(BF16)

    return pl.pallas_call(
        kern, name=name,
        grid_spec=pltpu.PrefetchScalarGridSpec(
            num_scalar_prefetch=1, grid=(4, rh // tr),
            in_specs=[pl.BlockSpec((1, 1, tr, LANES), lambda k, i, c_ref: (k, c_ref[0], i, 0)),
                      pl.BlockSpec((1, tr, LANES), lambda k, i, c_ref: (k, i, 0))],
            out_specs=pl.BlockSpec((1, tr, LANES), lambda k, i, c_ref: (k, i, 0))),
        out_shape=jax.ShapeDtypeStruct(xr.shape, BF16), compiler_params=_cp(("parallel", "parallel")),
    )(c.reshape(1).astype(jnp.int32), g, xr)


def _sum_blocks(y, name, tr=2048):
    n, r, _ = y.shape
    tr = min(tr, r)

    def kern(y_ref, o_ref):
        acc = y_ref[0].astype(F32)
        for k in range(1, n):
            acc = acc + y_ref[k].astype(F32)
        o_ref[...] = acc

    return pl.pallas_call(
        kern, name=name, grid=(r // tr,), in_specs=[pl.BlockSpec((n, tr, LANES), lambda i: (0, i, 0))],
        out_specs=pl.BlockSpec((tr, LANES), lambda i: (i, 0)), out_shape=jax.ShapeDtypeStruct((r, LANES), F32),
        compiler_params=_cp(("parallel",)),
    )(y)


def _adamw(w, g, m, v, name, tr=256):
    r, n = w.shape
    tr = tr if r % tr == 0 else r

    def kern(w_ref, g_ref, m_ref, v_ref, d_ref, nm_ref, nv_ref):
        gg = g_ref[...]
        nm = ADAM_B1 * m_ref[...] + (1.0 - ADAM_B1) * gg
        nv = ADAM_B2 * v_ref[...] + (1.0 - ADAM_B2) * (gg * gg)
        m_hat = nm / (1.0 - ADAM_B1 ** ADAM_STEP)
        v_hat = nv / (1.0 - ADAM_B2 ** ADAM_STEP)
        d_ref[...] = -ADAM_LR * (m_hat / (jnp.sqrt(v_hat) + ADAM_EPS) + ADAM_WD * w_ref[...])
        nm_ref[...] = nm
        nv_ref[...] = nv

    blk = pl.BlockSpec((tr, n), lambda i: (i, 0))
    return pl.pallas_call(
        kern, name=name, grid=(r // tr,), in_specs=[blk] * 4, out_specs=[blk] * 3,
        out_shape=[jax.ShapeDtypeStruct((r, n), F32)] * 3, compiler_params=_cp(("parallel",)),
    )(w, g, m, v)


def _pack(arrs, lead, dtype, mult):
    ls = arrs[0].shape[:lead]
    flat = [a.astype(dtype).reshape(ls + (-1,)) for a in arrs]
    n = sum(f.shape[-1] for f in flat)
    rows = -(-n // (LANES * mult)) * mult
    flat.append(jnp.zeros(ls + (rows * LANES - n,), dtype))
    return jnp.concatenate(flat, axis=-1).reshape(ls + (rows, LANES))


def _unpack(buf, lead, shapes):
    ls = buf.shape[:lead]
    flat = buf.reshape(ls + (-1,))
    out, off = [], 0
    for sh in shapes:
        n = int(np.prod(sh))
        out.append(lax.slice_in_dim(flat, off, off + n, axis=lead).reshape(ls + tuple(sh)))
        off += n
    return out


BIG = ("ab_w_in", "ab_w_out", "cd_w_in", "cd_w_out", "ffn_w_up", "ffn_w_down")
BIG_COLS = ("ab_w_in", "cd_w_in", "ffn_w_up")
SMALL_SHARDED = ("ssd_conv_w", "rg_conv_w", "rg_conv_b", "rg_ba", "rg_bx", "rg_lambda", "ffn_conv_w", "ln_g", "ln_b")
WEIGHTS = ("ab_w_in", "ssd_conv_w", "ssd_conv_b", "ssd_dt_bias", "ssd_a_log", "ssd_d", "ssd_norm_w", "hg_lower", "hg_norm_w",
           "ab_w_out", "cd_w_in", "swa_sinks", "rg_conv_w", "rg_conv_b", "rg_wa", "rg_ba", "rg_wx", "rg_bx", "rg_lambda",
           "cd_w_out", "ffn_w_up", "ffn_conv_w", "ffn_conv_b", "ffn_w_down", "ln_g", "ln_b")
SMALL = tuple(n for n in WEIGHTS if n not in BIG)


def _full_from_shards(name, g):
    if name in BIG_COLS or name in SMALL_SHARDED:
        return jnp.concatenate([g[k] for k in range(4)], axis=-1)
    return jnp.concatenate([g[k] for k in range(4)], axis=1)


def _shards_from_full(name, f):
    axis = f.ndim - 1 if (name in BIG_COLS or name in SMALL_SHARDED) else 1
    n = f.shape[axis] // 4
    return jnp.stack([lax.slice_in_dim(f, k * n, (k + 1) * n, axis=axis) for k in range(4)])


def kernel(x, ab_w_in, ssd_conv_w, ssd_conv_b, ssd_dt_bias, ssd_a_log, ssd_d, ssd_norm_w, hg_lower, hg_norm_w, ab_w_out, cd_w_in, swa_sinks, rg_conv_w, rg_conv_b, rg_wa, rg_ba, rg_wx, rg_bx, rg_lambda, cd_w_out, ffn_w_up, ffn_conv_w, ffn_conv_b, ffn_w_down, ln_g, ln_b, loss_target, m_ab_w_in, m_ssd_conv_w, m_ssd_conv_b, m_ssd_dt_bias, m_ssd_a_log, m_ssd_d, m_ssd_norm_w, m_hg_lower, m_hg_norm_w, m_ab_w_out, m_cd_w_in, m_swa_sinks, m_rg_conv_w, m_rg_conv_b, m_rg_wa, m_rg_ba, m_rg_wx, m_rg_bx, m_rg_lambda, m_cd_w_out, m_ffn_w_up, m_ffn_conv_w, m_ffn_conv_b, m_ffn_w_down, m_ln_g, m_ln_b, v_ab_w_in, v_ssd_conv_w, v_ssd_conv_b, v_ssd_dt_bias, v_ssd_a_log, v_ssd_d, v_ssd_norm_w, v_hg_lower, v_hg_norm_w, v_ab_w_out, v_cd_w_in, v_swa_sinks, v_rg_conv_w, v_rg_conv_b, v_rg_wa, v_rg_ba, v_rg_wx, v_rg_bx, v_rg_lambda, v_cd_w_out, v_ffn_w_up, v_ffn_conv_w, v_ffn_conv_b, v_ffn_w_down, v_ln_g, v_ln_b):
    args = locals()
    w = {n: args[n] for n in WEIGHTS}
    mom = {n: args["m_" + n] for n in WEIGHTS}
    var = {n: args["v_" + n] for n in WEIGHTS}
    cx, cy, cc = lax.axis_index("x"), lax.axis_index("y"), lax.axis_index("c")
    chip = 2 * cx + cy

    big_shapes = [w[n].shape for n in BIG]
    wp = _pack([w[n] for n in BIG], 0, BF16, 2 * PACK_ROWS)
    rh = wp.shape[0] // 2
    gathered = _gather_chips(wp.reshape(2, rh, LANES), "gather_weights").reshape(4, 2 * rh, LANES)
    full = {n: _full_from_shards(n, g) for n, g in zip(BIG, _unpack(gathered, 1, big_shapes))}
    sp = _pack([w[n] for n in SMALL_SHARDED], 0, F32, SUBLANES)
    sg = _gather_devices(sp, "gather_small").reshape(4, 2, sp.shape[0], LANES)[:, 0]
    for n, g in zip(SMALL_SHARDED, _unpack(sg, 1, [w[n].shape for n in SMALL_SHARDED])):
        full[n] = _full_from_shards(n, g)
    for n in SMALL:
        full.setdefault(n, w[n])

    loss, grad_x, grads = _local_step(x[0], loss_target[0], full)
    loss = lax.psum(loss, ("x", "y", "c"))

    g4 = _pack([_shards_from_full(n, grads[n]) for n in BIG], 1, F32, 2 * PACK_ROWS).reshape(4, 2, rh, LANES)
    hb = _add_half(g4, _swap_halves(g4, "swap_halves"), cc, "add_halves")
    own = _sum_blocks(_scatter_chips(hb, "scatter_chips"), "sum_chips")
    gbig = _unpack(_share_half(own, "share_half").reshape(2 * rh, LANES), 0, big_shapes)
    gout = dict(zip(BIG, gbig))

    small_shapes = [grads[n].shape for n in SMALL]
    gs = _pack([grads[n] for n in SMALL], 0, F32, SUBLANES)
    gsum = _sum_blocks(_gather_devices(gs, "gather_small_grads").reshape(8, gs.shape[0], LANES), "sum_small")
    for n, g in zip(SMALL, _unpack(gsum, 0, small_shapes)):
        if n in SMALL_SHARDED:
            width = w[n].shape[-1]
            g = lax.dynamic_slice_in_dim(g, chip * width, width, axis=g.ndim - 1)
        gout[n] = g

    delta, new_m, new_v = {}, {}, {}
    for n in BIG:
        sh = w[n].shape
        two = lambda a: a.reshape(-1, sh[-1])
        d, nm, nv = _adamw(two(w[n]), two(gout[n]), two(mom[n]), two(var[n]), "adamw_" + n)
        delta[n], new_m[n], new_v[n] = d.reshape(sh), nm.reshape(sh), nv.reshape(sh)
    local_shapes = [w[n].shape for n in SMALL]
    packs = [_pack([d[n] for n in SMALL], 0, F32, SUBLANES) for d in (w, gout, mom, var)]
    for dst, buf in zip((delta, new_m, new_v), _adamw(*packs, "adamw_small")):
        dst.update(zip(SMALL, _unpack(buf, 0, local_shapes)))

    return (loss, grad_x[None], *[gout[n] for n in WEIGHTS], *[delta[n] for n in WEIGHTS],
            *[new_m[n] for n in WEIGHTS], *[new_v[n] for n in WEIGHTS])
```

```python
import functools
import math

import numpy as np
import jax
import jax.numpy as jnp
from jax import lax
from jax.experimental import pallas as pl
from jax.experimental.pallas import tpu as pltpu

F32 = jnp.float32
BF16 = jnp.bfloat16
HI = lax.Precision.HIGHEST

D_MODEL = 1024
DEPTH = 4
SSD_HEADS = 8
SSD_D_INNER = 512
SSD_CONV_DIM = 768
SSD_CHUNK = 128
HG_CHUNK = 64
HG_HEADS = 4
SWA_BLOCK = 128
RG_WIDTH = 512
FFN_DIM = 2816
LN_EPS = 1e-5
RMS_EPS = 1e-6
MASK_VALUE = -1e9
ALPHA = (2 * DEPTH) ** 0.25
RG_C = 8.0
ADAM_LR, ADAM_B1, ADAM_B2, ADAM_EPS, ADAM_WD, ADAM_STEP = 0.001, 0.9, 0.999, 1e-08, 0.01, 10

LANES = 128
SUBLANES = 8
HALO = 16
CONV_CB = 256
VMEM_LIMIT = 56 * 1024 * 1024

AB_Z, AB_HQ, AB_HF, AB_HI, AB_HG, AB_XBC, AB_DT, AB_COLS = 0, 512, 1024, 1536, 2048, 2560, 3328, 3456
CD_Q, CD_GATE, CD_XR, CD_K, CD_V, CD_COLS = 0, 512, 1024, 1536, 1664, 1792


def _cp(sem=None):
    kw = dict(vmem_limit_bytes=VMEM_LIMIT)
    if sem is not None:
        kw["dimension_semantics"] = sem
    return pltpu.CompilerParams(**kw)


def _dot(a, b, dims=(((1,), (0,)), ((), ())), precision=None):
    return lax.dot_general(a, b, dims, precision=precision, preferred_element_type=F32)


NN = (((1,), (0,)), ((), ()))
NT = (((1,), (1,)), ((), ()))
TN = (((0,), (0,)), ((), ()))


@functools.partial(jax.custom_vjp, nondiff_argnums=(2,))
def _bdot(a, b, dims=NN):
    return _dot(a.astype(BF16), b.astype(BF16), dims)


def _bdot_fwd(a, b, dims):
    return _bdot(a, b, dims), (a, b)


def _bdot_bwd(dims, res, ct):
    a, b = res
    ab, bb, cb = a.astype(BF16), b.astype(BF16), ct.astype(BF16)
    if dims == NN:
        da, db = _dot(cb, bb, NT), _dot(ab, cb, TN)
    elif dims == NT:
        da, db = _dot(cb, bb, NN), _dot(cb, ab, TN)
    else:
        da, db = _dot(bb, cb, NT), _dot(ab, cb, NN)
    return da.astype(a.dtype), db.astype(b.dtype)


_bdot.defvjp(_bdot_fwd, _bdot_bwd)


def _hdot(a, b, dims=NN):
    return _dot(a, b, dims, precision=HI)


WIDE_TILES = (1408, 1152, 1024, 896, 768, 512, 384, 256, 128)
MM_BLOCK_BYTES = 6 * 1024 * 1024


def _pick(n, cands):
    for c in cands:
        if n % c == 0:
            return c
    return n


def _mm(pairs, n_out, out_dtype, name, trans_b=False, tm=512):
    m = pairs[0][0].shape[0]
    a_row_bytes = sum(p[4] * p[0].dtype.itemsize for p in pairs)
    b_col_bytes = sum(p[4] * p[2].dtype.itemsize for p in pairs)
    tm = min(2 * tm if 2 * tm * a_row_bytes <= MM_BLOCK_BYTES else tm, m)
    tn = _pick(n_out, [c for c in WIDE_TILES if c * b_col_bytes <= MM_BLOCK_BYTES])
    n = len(pairs)

    def kern(*refs):
        acc = None
        for i in range(n):
            p = _bdot(refs[i][...], refs[n + i][...], NT if trans_b else NN)
            acc = p if acc is None else acc + p
        refs[2 * n][...] = acc.astype(out_dtype)

    in_specs = []
    for a, acb, b, bcb, k in pairs:
        in_specs.append(pl.BlockSpec((tm, k), functools.partial(lambda i, j, c: (i, c), c=acb)))
    for a, acb, b, bcb, k in pairs:
        if trans_b:
            in_specs.append(pl.BlockSpec((tn, k), functools.partial(lambda i, j, c: (j, c), c=bcb)))
        else:
            in_specs.append(pl.BlockSpec((k, tn), functools.partial(lambda i, j, c: (c, j), c=bcb)))
    return pl.pallas_call(
        kern, name=name, grid=(m // tm, n_out // tn), in_specs=in_specs,
        out_specs=pl.BlockSpec((tm, tn), lambda i, j: (i, j)),
        out_shape=jax.ShapeDtypeStruct((m, n_out), out_dtype),
        compiler_params=_cp(("parallel", "arbitrary")),
    )(*[p[0] for p in pairs], *[p[2] for p in pairs])


def _mm_tn(a, a_cb, ka, g, g_cb, ng, name, tm=1024):
    m = a.shape[0]
    tm = min(tm, m)
    tk = _pick(ka, (1024, 1408, 512, 256, 128))
    tn = _pick(ng, WIDE_TILES)
    nm = m // tm

    def kern(a_ref, g_ref, o_ref):
        p = _bdot(a_ref[...], g_ref[...], TN)

        @pl.when(pl.program_id(2) == 0)
        def _():
            o_ref[...] = p

        @pl.when(pl.program_id(2) > 0)
        def _():
            o_ref[...] += p

    ak, gk = ka // tk, ng // tn
    return pl.pallas_call(
        kern, name=name, grid=(ak, gk, nm),
        in_specs=[pl.BlockSpec((tm, tk), lambda i, j, r: (r, a_cb * ak + i)),
                  pl.BlockSpec((tm, tn), lambda i, j, r: (r, g_cb * gk + j))],
        out_specs=pl.BlockSpec((tk, tn), lambda i, j, r: (i, j)),
        out_shape=jax.ShapeDtypeStruct((ka, ng), F32),
        compiler_params=_cp(("parallel", "parallel", "arbitrary")),
    )(a, g)


def _ln_fwd(x, m, g, b, name, tm=256):
    s, d = x.shape
    tm = min(tm, s)

    def kern(x_ref, m_ref, g_ref, b_ref, y_ref, yb_ref, r_ref):
        r = ALPHA * x_ref[...] + m_ref[...]
        mu = jnp.mean(r, -1, keepdims=True)
        xc = r - mu
        var = jnp.mean(xc * xc, -1, keepdims=True)
        y = xc * lax.rsqrt(var + LN_EPS) * g_ref[...] + b_ref[...]
        y_ref[...] = y
        yb_ref[...] = y.astype(BF16)
        r_ref[...] = r

    row = pl.BlockSpec((tm, d), lambda i: (i, 0))
    vec = pl.BlockSpec((1, d), lambda i: (0, 0))
    return pl.pallas_call(
        kern, name=name, grid=(s // tm,), in_specs=[row, row, vec, vec], out_specs=[row, row, row],
        out_shape=[jax.ShapeDtypeStruct((s, d), F32), jax.ShapeDtypeStruct((s, d), BF16), jax.ShapeDtypeStruct((s, d), F32)],
        compiler_params=_cp(("parallel",)),
    )(x, m, g.reshape(1, d), b.reshape(1, d))


def _ln_bwd(dya, dyb, ca, cb, r, g, name, tm=256):
    s, d = r.shape
    tm = min(tm, s)

    def kern(a_ref, b_ref, r_ref, g_ref, dr_ref, dg_ref, db_ref):
        dy = ca * a_ref[...] + cb * b_ref[...]
        rr = r_ref[...]
        mu = jnp.mean(rr, -1, keepdims=True)
        xc = rr - mu
        var = jnp.mean(xc * xc, -1, keepdims=True)
        rstd = lax.rsqrt(var + LN_EPS)
        xhat = xc * rstd
        dxh = dy * g_ref[...]
        dr_ref[...] = rstd * (dxh - jnp.mean(dxh, -1, keepdims=True) - xhat * jnp.mean(dxh * xhat, -1, keepdims=True))
        dg = jnp.sum(dy * xhat, 0, keepdims=True)
        db = jnp.sum(dy, 0, keepdims=True)

        @pl.when(pl.program_id(0) == 0)
        def _():
            dg_ref[...] = dg
            db_ref[...] = db

        @pl.when(pl.program_id(0) > 0)
        def _():
            dg_ref[...] += dg
            db_ref[...] += db

    row = pl.BlockSpec((tm, d), lambda i: (i, 0))
    vec = pl.BlockSpec((1, d), lambda i: (0, 0))
    dr, dg, db = pl.pallas_call(
        kern, name=name, grid=(s // tm,), in_specs=[row, row, row, vec], out_specs=[row, vec, vec],
        out_shape=[jax.ShapeDtypeStruct((s, d), F32), jax.ShapeDtypeStruct((1, d), F32), jax.ShapeDtypeStruct((1, d), F32)],
        compiler_params=_cp(("arbitrary",)),
    )(dya, dyb, r, g.reshape(1, d))
    return dr, dg[0], db[0]


def _sqerr(y, t, name, tm=256):
    s, d = y.shape
    tm = min(tm, s)

    def kern(y_ref, t_ref, o_ref):
        e = y_ref[...] - t_ref[...]
        p = jnp.sum(e * e, 0, keepdims=True)

        @pl.when(pl.program_id(0) == 0)
        def _():
            o_ref[...] = p

        @pl.when(pl.program_id(0) > 0)
        def _():
            o_ref[...] += p

    row = pl.BlockSpec((tm, d), lambda i: (i, 0))
    return pl.pallas_call(
        kern, name=name, grid=(s // tm,), in_specs=[row, row], out_specs=pl.BlockSpec((1, d), lambda i: (0, 0)),
        out_shape=jax.ShapeDtypeStruct((1, d), F32), compiler_params=_cp(("arbitrary",)),
    )(y, t)


def _axpby(a, b, ca, cb, name, tm=256):
    s, d = a.shape
    tm = min(tm, s)

    def kern(a_ref, b_ref, o_ref):
        o_ref[...] = ca * a_ref[...] + cb * b_ref[...]

    row = pl.BlockSpec((tm, d), lambda i: (i, 0))
    return pl.pallas_call(
        kern, name=name, grid=(s // tm,), in_specs=[row, row], out_specs=row,
        out_shape=jax.ShapeDtypeStruct((s, d), F32), compiler_params=_cp(("parallel",)),
    )(a, b)


def _shift_down(xx, s, n):
    if s == 0:
        return xx[HALO:HALO + n]
    return pltpu.roll(xx, s, axis=0)[HALO:HALO + n]


def _shift_up(yy, s, n):
    if s == 0:
        return yy[0:n]
    return pltpu.roll(yy, yy.shape[0] - s, axis=0)[0:n]


def _prev_halo(tm):
    return lambda i: jnp.maximum(i * (tm // HALO) - 1, 0)


def _dwconv_fwd(x, x_cb0, c, w8, b, k, name, tm=512):
    s = x.shape[0]
    tm = min(tm, s)
    ph = _prev_halo(tm)

    def kern(x_ref, h_ref, w_ref, b_ref, y_ref):
        halo = jnp.where(pl.program_id(0) == 0, 0.0, h_ref[...].astype(F32))
        xx = jnp.concatenate([halo, x_ref[...].astype(F32)], axis=0)
        w = w_ref[...]
        acc = b_ref[...] + w[k - 1:k] * xx[HALO:]
        for j in range(k - 1):
            acc = acc + w[j:j + 1] * _shift_down(xx, k - 1 - j, tm)
        y_ref[...] = acc

    return pl.pallas_call(
        kern, name=name, grid=(s // tm, c // CONV_CB),
        in_specs=[pl.BlockSpec((tm, CONV_CB), lambda i, j: (i, x_cb0 + j)),
                  pl.BlockSpec((HALO, CONV_CB), lambda i, j: (ph(i), x_cb0 + j)),
                  pl.BlockSpec((SUBLANES, CONV_CB), lambda i, j: (0, j)),
                  pl.BlockSpec((1, CONV_CB), lambda i, j: (0, j))],
        out_specs=pl.BlockSpec((tm, CONV_CB), lambda i, j: (i, j)),
        out_shape=jax.ShapeDtypeStruct((s, c), F32), compiler_params=_cp(("parallel", "parallel")),
    )(x, x, w8, b.reshape(1, c))


def _dwconv_bwd(dy, x, x_cb0, c, w8, k, name, tm=512):
    s = x.shape[0]
    tm = min(tm, s)
    nt = s // tm
    ph = _prev_halo(tm)
    nh = lambda i: jnp.minimum((i + 1) * (tm // HALO), s // HALO - 1)

    def kern(dy_ref, dyn_ref, x_ref, h_ref, w_ref, dx_ref, dw_ref, db_ref):
        i = pl.program_id(1)
        halo = jnp.where(i == 0, 0.0, h_ref[...].astype(F32))
        xx = jnp.concatenate([halo, x_ref[...].astype(F32)], axis=0)
        dyt = dy_ref[...]
        nxt = jnp.where(i == nt - 1, 0.0, dyn_ref[...])
        dyy = jnp.concatenate([dyt, nxt], axis=0)
        w = w_ref[...]
        dx = w[k - 1:k] * dyt
        rows = [jnp.sum(dyt * _shift_down(xx, k - 1 - j, tm), 0, keepdims=True) for j in range(k)]
        for j in range(k - 1):
            dx = dx + w[j:j + 1] * _shift_up(dyy, k - 1 - j, tm)
        dx_ref[...] = dx.astype(BF16)
        dw = jnp.concatenate(rows + [jnp.zeros((SUBLANES - k, CONV_CB), F32)], axis=0)
        db = jnp.sum(dyt, 0, keepdims=True)

        @pl.when(i == 0)
        def _():
            dw_ref[...] = dw
            db_ref[...] = db

        @pl.when(i > 0)
        def _():
            dw_ref[...] += dw
            db_ref[...] += db

    dx, dw, db = pl.pallas_call(
        kern, name=name, grid=(c // CONV_CB, nt),
        in_specs=[pl.BlockSpec((tm, CONV_CB), lambda j, i: (i, j)),
                  pl.BlockSpec((HALO, CONV_CB), lambda j, i: (nh(i), j)),
                  pl.BlockSpec((tm, CONV_CB), lambda j, i: (i, x_cb0 + j)),
                  pl.BlockSpec((HALO, CONV_CB), lambda j, i: (ph(i), x_cb0 + j)),
                  pl.BlockSpec((SUBLANES, CONV_CB), lambda j, i: (0, j))],
        out_specs=[pl.BlockSpec((tm, CONV_CB), lambda j, i: (i, j)),
                   pl.BlockSpec((SUBLANES, CONV_CB), lambda j, i: (0, j)),
                   pl.BlockSpec((1, CONV_CB), lambda j, i: (0, j))],
        out_shape=[jax.ShapeDtypeStruct((s, c), BF16), jax.ShapeDtypeStruct((SUBLANES, c), F32),
                   jax.ShapeDtypeStruct((1, c), F32)],
        compiler_params=_cp(("parallel", "arbitrary")),
    )(dy, dy, x, x, w8)
    return dx, dw, db[0]


def _silu(x):
    return x * jax.nn.sigmoid(x)


def _ffn_gate_fwd(hu, w8, b, name, tm=512):
    s = hu.shape[0]
    tm = min(tm, s)
    nb = FFN_DIM // CONV_CB
    ph = _prev_halo(tm)
    k = 3

    def conv(x_ref, h_ref, w_ref, b_ref):
        halo = jnp.where(pl.program_id(0) == 0, 0.0, h_ref[...].astype(F32))
        xx = jnp.concatenate([halo, x_ref[...].astype(F32)], axis=0)
        w = w_ref[...]
        acc = b_ref[...] + w[k - 1:k] * xx[HALO:]
        for j in range(k - 1):
            acc = acc + w[j:j + 1] * _shift_down(xx, k - 1 - j, tm)
        return acc

    def kern(g_ref, gh_ref, u_ref, uh_ref, wg_ref, wu_ref, bg_ref, bu_ref, a_ref):
        g = conv(g_ref, gh_ref, wg_ref, bg_ref)
        u = conv(u_ref, uh_ref, wu_ref, bu_ref)
        a_ref[...] = (_silu(g) * u).astype(BF16)

    main = lambda off: pl.BlockSpec((tm, CONV_CB), lambda i, j: (i, off + j))
    halo = lambda off: pl.BlockSpec((HALO, CONV_CB), lambda i, j: (ph(i), off + j))
    wsp = lambda off: pl.BlockSpec((SUBLANES, CONV_CB), lambda i, j: (0, off + j))
    bsp = lambda off: pl.BlockSpec((1, CONV_CB), lambda i, j: (0, off + j))
    b2 = b.reshape(1, 2 * FFN_DIM)
    return pl.pallas_call(
        kern, name=name, grid=(s // tm, nb),
        in_specs=[main(0), halo(0), main(nb), halo(nb), wsp(0), wsp(nb), bsp(0), bsp(nb)],
        out_specs=pl.BlockSpec((tm, CONV_CB), lambda i, j: (i, j)),
        out_shape=jax.ShapeDtypeStruct((s, FFN_DIM), BF16), compiler_params=_cp(("parallel", "parallel")),
    )(hu, hu, hu, hu, w8, w8, b2, b2)


def _ffn_gate_bwd(hu, da, w8, b, name, tm=512):
    s = hu.shape[0]
    tm = min(tm, s)
    nt = s // tm
    nb = FFN_DIM // CONV_CB
    ph = _prev_halo(tm)
    nh = lambda i: jnp.minimum((i + 1) * (tm // HALO), s // HALO - 1)
    k = 3
    ne = tm + HALO

    def kern(g_ref, gp_ref, gn_ref, u_ref, up_ref, un_ref, da_ref, dan_ref, wg_ref, wu_ref, bg_ref, bu_ref,
             dg_ref, du_ref, dwg_ref, dwu_ref, dbg_ref, dbu_ref):
        i = pl.program_id(1)

        def ext(x_ref, p_ref, n_ref):
            halo = jnp.where(i == 0, 0.0, p_ref[...].astype(F32))
            return jnp.concatenate([halo, x_ref[...].astype(F32), n_ref[...].astype(F32)], axis=0)

        def conv(xx, w, bb):
            acc = bb + w[k - 1:k] * xx[HALO:]
            for j in range(k - 1):
                acc = acc + w[j:j + 1] * _shift_down(xx, k - 1 - j, ne)
            return acc

        gx, ux = ext(g_ref, gp_ref, gn_ref), ext(u_ref, up_ref, un_ref)
        wg, wu = wg_ref[...], wu_ref[...]
        g = conv(gx, wg, bg_ref[...])
        u = conv(ux, wu, bu_ref[...])
        dae = jnp.concatenate([da_ref[...].astype(F32), jnp.where(i == nt - 1, 0.0, dan_ref[...].astype(F32))], axis=0)
        sg = jax.nn.sigmoid(g)
        dhg = dae * u * (sg * (1.0 + g * (1.0 - sg)))
        dhu = dae * (g * sg)

        def back(dh, xx, w):
            dx = w[k - 1:k] * dh[0:tm]
            for j in range(k - 1):
                dx = dx + w[j:j + 1] * _shift_up(dh, k - 1 - j, tm)
            rows = [jnp.sum(dh[0:tm] * _shift_down(xx, k - 1 - j, tm), 0, keepdims=True) for j in range(k)]
            dw = jnp.concatenate(rows + [jnp.zeros((SUBLANES - k, CONV_CB), F32)], axis=0)
            return dx, dw, jnp.sum(dh[0:tm], 0, keepdims=True)

        dxg, dwg, dbg = back(dhg, gx, wg)
        dxu, dwu, dbu = back(dhu, ux, wu)
        dg_ref[...] = dxg.astype(BF16)
        du_ref[...] = dxu.astype(BF16)

        @pl.when(i == 0)
        def _():
            dwg_ref[...] = dwg
            dwu_ref[...] = dwu
            dbg_ref[...] = dbg
            dbu_ref[...] = dbu

        @pl.when(i > 0)
        def _():
            dwg_ref[...] += dwg
            dwu_ref[...] += dwu
            dbg_ref[...] += dbg
            dbu_ref[...] += dbu

    main = lambda off: pl.BlockSpec((tm, CONV_CB), lambda j, i: (i, off + j))
    prev = lambda off: pl.BlockSpec((HALO, CONV_CB), lambda j, i: (ph(i), off + j))
    nxt = lambda off: pl.BlockSpec((HALO, CONV_CB), lambda j, i: (nh(i), off + j))
    wsp = lambda off: pl.BlockSpec((SUBLANES, CONV_CB), lambda j, i: (0, off + j))
    bsp = lambda off: pl.BlockSpec((1, CONV_CB), lambda j, i: (0, off + j))
    b2 = b.reshape(1, 2 * FFN_DIM)
    outs = pl.pallas_call(
        kern, name=name, grid=(nb, nt),
        in_specs=[main(0), prev(0), nxt(0), main(nb), prev(nb), nxt(nb), main(0), nxt(0), wsp(0), wsp(nb), bsp(0), bsp(nb)],
        out_specs=[main(0), main(0), wsp(0), wsp(0), bsp(0), bsp(0)],
        out_shape=[jax.ShapeDtypeStruct((s, FFN_DIM), BF16), jax.ShapeDtypeStruct((s, FFN_DIM), BF16),
                   jax.ShapeDtypeStruct((SUBLANES, FFN_DIM), F32), jax.ShapeDtypeStruct((SUBLANES, FFN_DIM), F32),
                   jax.ShapeDtypeStruct((1, FFN_DIM), F32), jax.ShapeDtypeStruct((1, FFN_DIM), F32)],
        compiler_params=_cp(("parallel", "arbitrary")),
    )(hu, hu, hu, hu, hu, hu, da, da, w8, w8, b2, b2)
    dg, du, dwg, dwu, dbg, dbu = outs
    return dg, du, jnp.concatenate([dwg, dwu], axis=1), jnp.concatenate([dbg[0], dbu[0]])


def _iota(shape, dim):
    return lax.broadcasted_iota(jnp.int32, shape, dim)


def _in_range(idx, start, size):
    return jnp.logical_and(idx >= start, idx < start + size)


def _expand8(v, e):
    return jnp.sum(_hdot(jnp.broadcast_to(v, (SUBLANES, v.shape[1])), e), 0, keepdims=True) * (1.0 / SUBLANES)


def _ssd_consts():
    tri = np.tril(np.ones((SSD_CHUNK, SSD_CHUNK), np.float32))
    e = np.zeros((LANES, SSD_D_INNER), np.float32)
    for h in range(SSD_HEADS):
        e[h, 64 * h:64 * h + 64] = 1.0
    bd = np.zeros((LANES, SSD_D_INNER), np.float32)
    bd[:64, :256] = 1.0
    bd[64:, 256:] = 1.0
    return jnp.asarray(tri), jnp.asarray(e), jnp.asarray(bd)


def _ssd_chunk(tri, e, bd, z, xs_p, bm_p, cm_p, dtr, st, dtb, alog, dsk, nw):
    n = z.shape[0]
    lane128, lane512 = _iota((1, LANES), 1), _iota((1, SSD_D_INNER), 1)
    sub128 = _iota((LANES, 1), 0)
    tril = _iota((n, 1), 0) >= _iota((1, n), 1)
    xs, bm, cm = _silu(xs_p), _silu(bm_p), _silu(cm_p)
    dt = jax.nn.softplus(dtr + dtb)
    da = dt * (-jnp.exp(alog))
    acs = _hdot(tri, da)
    acst = _hdot(da, tri, (((0,), (1,)), ((), ())))
    xc = xs * _hdot(dt, e)
    tot = jnp.sum(da, 0, keepdims=True)
    y = _bdot(cm, st) * _hdot(jnp.exp(acs), e)
    st_new = st * _expand8(jnp.exp(tot), e) + bd * _bdot(bm, xc * _hdot(jnp.exp(tot - acs), e), TN)
    for g in range(2):
        cb = _bdot(jnp.where(_in_range(lane128, 64 * g, 64), cm, 0.0), bm, NT)
        for h in range(4 * g, 4 * g + 4):
            col = jnp.sum(jnp.where(lane128 == h, acs, 0.0), 1, keepdims=True)
            row = jnp.sum(jnp.where(sub128 == h, acst, 0.0), 0, keepdims=True)
            dm = jnp.where(tril, jnp.exp(jnp.minimum(col - row, 0.0)), 0.0)
            y = y + _bdot(cb * dm, jnp.where(_in_range(lane512, 64 * h, 64), xc, 0.0))
    y = (y + _expand8(dsk, e) * xs) * _silu(z)
    ysq = y * y
    g0 = lane512 < 256
    ms0 = jnp.sum(jnp.where(g0, ysq, 0.0), 1, keepdims=True) * (1.0 / 256)
    ms1 = jnp.sum(jnp.where(g0, 0.0, ysq), 1, keepdims=True) * (1.0 / 256)
    rs = jnp.where(g0, lax.rsqrt(ms0 + RMS_EPS), lax.rsqrt(ms1 + RMS_EPS))
    return y * rs * nw, st_new


def _ssd_specs(rev, nc):
    ci = (lambda i: nc - 1 - i) if rev else (lambda i: i)
    n = SSD_CHUNK
    act = [pl.BlockSpec((n, 512), lambda i: (ci(i), AB_Z // 512)),
           pl.BlockSpec((n, 512), lambda i: (ci(i), 0)),
           pl.BlockSpec((n, LANES), lambda i: (ci(i), 4)),
           pl.BlockSpec((n, LANES), lambda i: (ci(i), 5)),
           pl.BlockSpec((n, LANES), lambda i: (ci(i), AB_DT // LANES))]
    const = [pl.BlockSpec((n, n), lambda i: (0, 0)), pl.BlockSpec((LANES, 512), lambda i: (0, 0)),
             pl.BlockSpec((LANES, 512), lambda i: (0, 0))]
    par = [pl.BlockSpec((1, LANES), lambda i: (0, 0))] * 3 + [pl.BlockSpec((1, 512), lambda i: (0, 0))]
    return ci, act, const, par


def _ssd_fwd(h, xbcc, dtb, alog, dsk, nw, name):
    s = h.shape[0]
    nc = s // SSD_CHUNK
    ci, act, const, par = _ssd_specs(False, nc)

    def kern(z_ref, xs_ref, bm_ref, cm_ref, dt_ref, tri_ref, e_ref, bd_ref, dtb_ref, al_ref, dsk_ref, nw_ref,
             y_ref, sts_ref, st):
        @pl.when(pl.program_id(0) == 0)
        def _():
            st[...] = jnp.zeros_like(st)

        sts_ref[0] = st[...]
        y, stn = _ssd_chunk(tri_ref[...], e_ref[...], bd_ref[...], z_ref[...], xs_ref[...], bm_ref[...], cm_ref[...],
                            dt_ref[...], st[...], dtb_ref[...], al_ref[...], dsk_ref[...], nw_ref[...])
        y_ref[...] = y.astype(BF16)
        st[...] = stn

    return pl.pallas_call(
        kern, name=name, grid=(nc,), in_specs=act + const + par,
        out_specs=[pl.BlockSpec((SSD_CHUNK, 512), lambda i: (i, 0)), pl.BlockSpec((1, LANES, 512), lambda i: (i, 0, 0))],
        out_shape=[jax.ShapeDtypeStruct((s, 512), BF16), jax.ShapeDtypeStruct((nc, LANES, 512), F32)],
        scratch_shapes=[pltpu.VMEM((LANES, 512), F32)], compiler_params=_cp(("arbitrary",)),
    )(h, xbcc, xbcc, xbcc, h, *_ssd_consts(), dtb, alog, dsk, nw)


def _acc(ref, val, first):
    @pl.when(first)
    def _():
        ref[...] = val

    @pl.when(jnp.logical_not(first))
    def _():
        ref[...] += val


def _ssd_bwd(dy, dy_cb, h, xbcc, sts, dtb, alog, dsk, nw, name):
    s = h.shape[0]
    nc = s // SSD_CHUNK
    n = SSD_CHUNK
    ci, act, const, par = _ssd_specs(True, nc)

    def kern(dy_ref, sts_ref, z_ref, xs_ref, bm_ref, cm_ref, dt_ref, tri_ref, e_ref, bd_ref, dtb_ref, al_ref, dsk_ref,
             nw_ref, dz_ref, dx_ref, ddt_ref, ddtb_ref, dal_ref, ddsk_ref, dnw_ref, dst):
        first = pl.program_id(0) == 0

        @pl.when(first)
        def _():
            dst[...] = jnp.zeros_like(dst)

        f = functools.partial(_ssd_chunk, tri_ref[...], e_ref[...], bd_ref[...])
        _, pull = jax.vjp(f, z_ref[...], xs_ref[...], bm_ref[...], cm_ref[...], dt_ref[...], sts_ref[0],
                          dtb_ref[...], al_ref[...], dsk_ref[...], nw_ref[...])
        dz, dxs, dbm, dcm, ddt, dsti, ddtb, dal, ddsk, dnw = pull((dy_ref[...], dst[...]))
        dz_ref[...] = dz.astype(BF16)
        dx_ref[:, 0:512] = dxs
        dx_ref[:, 512:640] = dbm
        dx_ref[:, 640:768] = dcm
        ddt_ref[...] = ddt.astype(BF16)
        dst[...] = dsti
        _acc(ddtb_ref, ddtb, first)
        _acc(dal_ref, dal, first)
        _acc(ddsk_ref, ddsk, first)
        _acc(dnw_ref, dnw, first)

    vec = pl.BlockSpec((1, LANES), lambda i: (0, 0))
    return pl.pallas_call(
        kern, name=name, grid=(nc,),
        in_specs=[pl.BlockSpec((n, 512), lambda i: (ci(i), dy_cb)), pl.BlockSpec((1, LANES, 512), lambda i: (ci(i), 0, 0))]
        + act + const + par,
        out_specs=[pl.BlockSpec((n, 512), lambda i: (ci(i), 0)), pl.BlockSpec((n, SSD_CONV_DIM), lambda i: (ci(i), 0)),
                   pl.BlockSpec((n, LANES), lambda i: (ci(i), 0)), vec, vec, vec, pl.BlockSpec((1, 512), lambda i: (0, 0))],
        out_shape=[jax.ShapeDtypeStruct((s, 512), BF16), jax.ShapeDtypeStruct((s, SSD_CONV_DIM), F32),
                   jax.ShapeDtypeStruct((s, LANES), BF16), jax.ShapeDtypeStruct((1, LANES), F32),
                   jax.ShapeDtypeStruct((1, LANES), F32), jax.ShapeDtypeStruct((1, LANES), F32),
                   jax.ShapeDtypeStruct((1, 512), F32)],
        scratch_shapes=[pltpu.VMEM((LANES, 512), F32)], compiler_params=_cp(("arbitrary",)),
    )(dy, sts, h, xbcc, xbcc, xbcc, h, *_ssd_consts(), dtb, alog, dsk, nw)


HG_LEVELS = 6


HG_SUMS = 1 + 2 * HG_LEVELS


def _hg_consts():
    n = HG_CHUNK
    t = np.arange(n)
    mats = [np.tril(np.ones((n, n), np.float32))]
    dq, dk, mk = [], [], []
    for lv in range(HG_LEVELS):
        b = 1 << lv
        start = (t // b) * b
        end = start + b - 1
        dq.append(((t[None, :] >= start[:, None]) & (t[None, :] <= t[:, None])).astype(np.float32))
        dk.append(((t[None, :] > t[:, None]) & (t[None, :] <= end[:, None])).astype(np.float32))
        mk.append((((t // b) % 2 == 1)[:, None] & ((t // b)[None, :] == (t // b)[:, None] - 1)).astype(np.float32))
    c1 = np.concatenate(mats + dq + dk, axis=0)
    return (jnp.asarray(c1, BF16), jnp.asarray(np.concatenate([c1, c1, c1], axis=1), BF16), jnp.asarray(np.stack(mk)))


@functools.partial(jax.custom_vjp, nondiff_argnums=(1, 2))
def _split_grid(x, nr, nc):
    br, bc = x.shape[0] // nr, x.shape[1] // nc
    return tuple(x[i * br:(i + 1) * br, j * bc:(j + 1) * bc] for i in range(nr) for j in range(nc))


def _split_grid_fwd(x, nr, nc):
    return _split_grid(x, nr, nc), None


def _split_grid_bwd(nr, nc, _, cts):
    rows = [cts[i * nc] if nc == 1 else jnp.concatenate(cts[i * nc:(i + 1) * nc], axis=1) for i in range(nr)]
    return (rows[0] if nr == 1 else jnp.concatenate(rows, axis=0),)


_split_grid.defvjp(_split_grid_fwd, _split_grid_bwd)


@jax.custom_vjp
def _level_sums(c1, c3, g):
    gh = g.astype(BF16)
    r1 = g - gh.astype(F32)
    gm = r1.astype(BF16)
    gl = (r1 - gm.astype(F32)).astype(BF16)
    return _dot(c3, jnp.concatenate([gh, gm, gl], axis=0), NN)


def _level_sums_fwd(c1, c3, g):
    return _level_sums(c1, c3, g), (c1, c3)


def _level_sums_bwd(res, ct):
    c1, c3 = res
    ch = ct.astype(BF16)
    cm = (ct - ch.astype(F32)).astype(BF16)
    return jnp.zeros_like(c1), jnp.zeros_like(c3), _dot(c1, ch, TN) + _dot(c1, cm, TN)


_level_sums.defvjp(_level_sums_fwd, _level_sums_bwd)


def _hg_chunk(c1, c3, masks, hq, hf, hi, hg, sts, lb, nw):
    q = _silu(hq)
    g = jnp.log(lb + (1.0 - lb) * jax.nn.sigmoid(hf))
    k = (1.0 - lb) * jax.nn.sigmoid(-hf)
    parts = _split_grid(_level_sums(c1, c3, g), HG_SUMS, 1)
    bc = parts[0]
    tot = jnp.sum(g, 0, keepdims=True)
    qe = _split_grid(q * jnp.exp(bc), 1, HG_HEADS)
    kd = _split_grid(k * jnp.exp(tot - bc), 1, HG_HEADS)
    etot = _split_grid(jnp.exp(tot), 1, HG_HEADS)
    qh, kh, vh, gh = [_split_grid(a, 1, HG_HEADS) for a in (q, k, hi, hg)]
    att = [None] * HG_HEADS
    for lv in range(HG_LEVELS):
        qt = _split_grid(q * jnp.exp(parts[1 + lv]), 1, HG_HEADS)
        kt = _split_grid(k * jnp.exp(parts[1 + HG_LEVELS + lv]), 1, HG_HEADS)
        for hd in range(HG_HEADS):
            t = masks[lv] * _bdot(qt[hd], kt[hd], NT)
            att[hd] = t if att[hd] is None else att[hd] + t
    outs, stn = [], []
    for hd in range(HG_HEADS):
        o = (_bdot(att[hd], vh[hd]) + jnp.sum(qh[hd] * kh[hd], 1, keepdims=True) * vh[hd] + _bdot(qe[hd], sts[hd], NT))
        stn.append(sts[hd] * etot[hd] + _bdot(vh[hd], kd[hd], TN))
        outs.append(o * lax.rsqrt(jnp.mean(o * o, 1, keepdims=True) + RMS_EPS) * nw * _silu(gh[hd]))
    return tuple(outs), tuple(stn)


def _hg_specs(rev, nc):
    ci = (lambda i: nc - 1 - i) if rev else (lambda i: i)
    n = HG_CHUNK
    act = [pl.BlockSpec((n, 512), functools.partial(lambda i, c: (ci(i), c), c=c))
           for c in (AB_HQ // 512, AB_HF // 512, AB_HI // 512, AB_HG // 512)]
    other = [pl.BlockSpec((HG_SUMS * n, n), lambda i: (0, 0)), pl.BlockSpec((HG_SUMS * n, 3 * n), lambda i: (0, 0)),
             pl.BlockSpec((HG_LEVELS, n, n), lambda i: (0, 0, 0)), pl.BlockSpec((1, 512), lambda i: (0, 0)),
             pl.BlockSpec((1, LANES), lambda i: (0, 0))]
    return ci, act, other


def _head_rows(hd):
    return slice(LANES * hd, LANES * (hd + 1))


def _hg_fwd(h, lb, nw, name):
    s = h.shape[0]
    nc = s // HG_CHUNK
    ci, act, other = _hg_specs(False, nc)

    def kern(q_ref, f_ref, i_ref, g_ref, c1_ref, c3_ref, mk_ref, lb_ref, nw_ref, y_ref, sts_ref, st):
        @pl.when(pl.program_id(0) == 0)
        def _():
            st[...] = jnp.zeros_like(st)

        sts_ref[0] = st[...]
        masks = [mk_ref[lv] for lv in range(HG_LEVELS)]
        outs, stn = _hg_chunk(c1_ref[...], c3_ref[...], masks, q_ref[...], f_ref[...], i_ref[...], g_ref[...],
                              tuple(st[_head_rows(hd), :] for hd in range(HG_HEADS)), lb_ref[...], nw_ref[...])
        for hd in range(HG_HEADS):
            y_ref[:, _head_rows(hd)] = outs[hd].astype(BF16)
            st[_head_rows(hd), :] = stn[hd]

    return pl.pallas_call(
        kern, name=name, grid=(nc,), in_specs=act + other,
        out_specs=[pl.BlockSpec((HG_CHUNK, 512), lambda i: (i, 0)), pl.BlockSpec((1, 512, LANES), lambda i: (i, 0, 0))],
        out_shape=[jax.ShapeDtypeStruct((s, 512), BF16), jax.ShapeDtypeStruct((nc, 512, LANES), F32)],
        scratch_shapes=[pltpu.VMEM((512, LANES), F32)], compiler_params=_cp(("arbitrary",)),
    )(h, h, h, h, *_hg_consts(), lb, nw)


def _hg_bwd(dy, dy_cb, h, sts, lb, nw, name):
    s = h.shape[0]
    nc = s // HG_CHUNK
    n = HG_CHUNK
    ci, act, other = _hg_specs(True, nc)

    def kern(dy_ref, sts_ref, q_ref, f_ref, i_ref, g_ref, c1_ref, c3_ref, mk_ref, lb_ref, nw_ref,
             dq_ref, df_ref, di_ref, dg_ref, dlb_ref, dnw_ref, dst):
        first = pl.program_id(0) == 0

        @pl.when(first)
        def _():
            dst[...] = jnp.zeros_like(dst)

        masks = [mk_ref[lv] for lv in range(HG_LEVELS)]
        heads = range(HG_HEADS)
        _, pull = jax.vjp(functools.partial(_hg_chunk, c1_ref[...], c3_ref[...], masks), q_ref[...], f_ref[...], i_ref[...],
                          g_ref[...], tuple(sts_ref[0, _head_rows(hd), :] for hd in heads), lb_ref[...], nw_ref[...])
        dq, df, di, dg, dsti, dlb, dnw = pull((tuple(dy_ref[:, _head_rows(hd)] for hd in heads),
                                               tuple(dst[_head_rows(hd), :] for hd in heads)))
        dq_ref[...] = dq.astype(BF16)
        df_ref[...] = df.astype(BF16)
        di_ref[...] = di.astype(BF16)
        dg_ref[...] = dg.astype(BF16)
        for hd in heads:
            dst[_head_rows(hd), :] = dsti[hd]
        _acc(dlb_ref, dlb, first)
        _acc(dnw_ref, dnw, first)

    blk = pl.BlockSpec((n, 512), lambda i: (ci(i), 0))
    return pl.pallas_call(
        kern, name=name, grid=(nc,),
        in_specs=[pl.BlockSpec((n, 512), lambda i: (ci(i), dy_cb)), pl.BlockSpec((1, 512, LANES), lambda i: (ci(i), 0, 0))]
        + act + other,
        out_specs=[blk, blk, blk, blk, pl.BlockSpec((1, 512), lambda i: (0, 0)), pl.BlockSpec((1, LANES), lambda i: (0, 0))],
        out_shape=[jax.ShapeDtypeStruct((s, 512), BF16)] * 4 + [jax.ShapeDtypeStruct((1, 512), F32),
                                                                jax.ShapeDtypeStruct((1, LANES), F32)],
        scratch_shapes=[pltpu.VMEM((512, LANES), F32)], compiler_params=_cp(("arbitrary",)),
    )(dy, sts, h, h, h, h, *_hg_consts(), lb, nw)


def _swa_pair(pair, notfirst, qp, k0, v0, k1, v1, srow):
    n = qp.shape[0]
    rows, cols = _iota((n, 1), 0), _iota((1, n), 1)
    own = rows >= cols
    prev = jnp.logical_and(cols > rows, notfirst)
    lane = _iota((1, LANES), 1)
    out = None
    for e in range(2):
        lm = _in_range(lane, 64 * e, 64)
        qm = jnp.where(lm, qp, 0.0)
        s0 = jnp.where(prev, _bdot(qm, k0, NT) * 0.125, MASK_VALUE)
        s1 = jnp.where(own, _bdot(qm, k1, NT) * 0.125, MASK_VALUE)
        sk = jnp.sum(jnp.where(lane == pair + 4 * e, srow, 0.0), 1, keepdims=True)
        m = lax.stop_gradient(jnp.maximum(jnp.maximum(jnp.max(s0, 1, keepdims=True), jnp.max(s1, 1, keepdims=True)), sk))
        p0, p1 = jnp.exp(s0 - m), jnp.exp(s1 - m)
        inv = 1.0 / (jnp.sum(p0, 1, keepdims=True) + jnp.sum(p1, 1, keepdims=True) + jnp.exp(sk - m))
        t = _bdot(p0 * inv, jnp.where(lm, v0, 0.0)) + _bdot(p1 * inv, jnp.where(lm, v1, 0.0))
        out = t if out is None else out + t
    return out


def _swa_specs():
    n = SWA_BLOCK
    pv = lambda i: jnp.maximum(i - 1, 0)
    return [pl.BlockSpec((n, 512), lambda i: (i, CD_Q // 512)),
            pl.BlockSpec((n, LANES), lambda i: (pv(i), CD_K // LANES)), pl.BlockSpec((n, LANES), lambda i: (pv(i), CD_V // LANES)),
            pl.BlockSpec((n, LANES), lambda i: (i, CD_K // LANES)), pl.BlockSpec((n, LANES), lambda i: (i, CD_V // LANES)),
            pl.BlockSpec((1, LANES), lambda i: (0, 0))]


def _swa_fwd(hc, srow, name):
    s = hc.shape[0]
    n = SWA_BLOCK

    def kern(q_ref, k0_ref, v0_ref, k1_ref, v1_ref, s_ref, y_ref):
        notfirst = pl.program_id(0) > 0
        for p in range(4):
            ln = slice(LANES * p, LANES * (p + 1))
            y_ref[:, ln] = _swa_pair(p, notfirst, q_ref[:, ln], k0_ref[...], v0_ref[...], k1_ref[...], v1_ref[...],
                                     s_ref[...]).astype(BF16)

    return pl.pallas_call(
        kern, name=name, grid=(s // n,), in_specs=_swa_specs(), out_specs=pl.BlockSpec((n, 512), lambda i: (i, 0)),
        out_shape=jax.ShapeDtypeStruct((s, 512), BF16), compiler_params=_cp(("parallel",)),
    )(hc, hc, hc, hc, hc, srow)


def _swa_bwd(dy, dy_cb, hc, srow, name):
    s = hc.shape[0]
    n = SWA_BLOCK

    def kern(dy_ref, q_ref, k0_ref, v0_ref, k1_ref, v1_ref, s_ref, dq_ref, dk_ref, dv_ref, ds_ref):
        i = pl.program_id(0)
        notfirst = i > 0

        @pl.when(i == 0)
        def _():
            dk_ref[...] = jnp.zeros_like(dk_ref)
            dv_ref[...] = jnp.zeros_like(dv_ref)
            ds_ref[...] = jnp.zeros_like(ds_ref)

        r1 = pl.ds(pl.multiple_of(i * n, n), n)
        r0 = pl.ds(pl.multiple_of(jnp.maximum(i - 1, 0) * n, n), n)
        for p in range(4):
            ln = slice(LANES * p, LANES * (p + 1))
            _, pull = jax.vjp(functools.partial(_swa_pair, p, notfirst), q_ref[:, ln], k0_ref[...], v0_ref[...],
                              k1_ref[...], v1_ref[...], s_ref[...])
            dq, dk0, dv0, dk1, dv1, dsr = pull(dy_ref[:, ln])
            dq_ref[:, ln] = dq.astype(BF16)
            dk_ref[r0, :] += dk0
            dv_ref[r0, :] += dv0
            dk_ref[r1, :] += dk1
            dv_ref[r1, :] += dv1
            ds_ref[...] += dsr

    full = pl.BlockSpec((s, LANES), lambda i: (0, 0))
    return pl.pallas_call(
        kern, name=name, grid=(s // n,), in_specs=[pl.BlockSpec((n, 512), lambda i: (i, dy_cb))] + _swa_specs(),
        out_specs=[pl.BlockSpec((n, 512), lambda i: (i, 0)), full, full, pl.BlockSpec((1, LANES), lambda i: (0, 0))],
        out_shape=[jax.ShapeDtypeStruct((s, 512), BF16), jax.ShapeDtypeStruct((s, LANES), F32),
                   jax.ShapeDtypeStruct((s, LANES), F32), jax.ShapeDtypeStruct((1, LANES), F32)],
        compiler_params=_cp(("arbitrary",)),
    )(dy, hc, hc, hc, hc, hc, srow)


RG_TILE = 256


def _neg_expm1(x):
    ser = x * (1 + x / 2 * (1 + x / 3 * (1 + x / 4 * (1 + x / 5 * (1 + x / 6 * (1 + x / 7 * (1 + x / 8)))))))
    return jnp.where(x > -0.25, -ser, 1.0 - jnp.exp(x))


def _rg_gates(xc, wa, wx, ba, bx, lam):
    r = jax.nn.sigmoid(_bdot(xc, wa) + ba)
    i = jax.nn.sigmoid(_bdot(xc, wx) + bx)
    log_a = -RG_C * r * jax.nn.softplus(-lam)
    u = jnp.sqrt(jnp.maximum(_neg_expm1(2.0 * log_a), 0.0)) * (i * xc)
    return jnp.exp(log_a), u


def _rg_out(hs, gate):
    return hs * jax.nn.gelu(gate)


def _rows_to_tile(rows):
    sub = _iota((SUBLANES, 1), 0)
    out = jnp.broadcast_to(rows[0], (SUBLANES, rows[0].shape[1]))
    for j in range(1, SUBLANES):
        out = jnp.where(sub == j, rows[j], out)
    return out


def _rg_fwd(xc, hc, wa, wx, ba, bx, lam, name):
    s = xc.shape[0]
    tm = min(RG_TILE, s)

    def kern(x_ref, g_ref, wa_ref, wx_ref, ba_ref, bx_ref, lam_ref, y_ref, h_ref, a_s, u_s, carry):
        @pl.when(pl.program_id(0) == 0)
        def _():
            carry[...] = jnp.zeros_like(carry)

        a, u = _rg_gates(x_ref[...], wa_ref[...], wx_ref[...], ba_ref[...], bx_ref[...], lam_ref[...])
        a_s[...] = a
        u_s[...] = u

        def body(gi, hp):
            r0 = pl.multiple_of(gi * SUBLANES, SUBLANES)
            ab, ub = a_s[pl.ds(r0, SUBLANES), :], u_s[pl.ds(r0, SUBLANES), :]
            rows = []
            for j in range(SUBLANES):
                hp = ab[j:j + 1] * hp + ub[j:j + 1]
                rows.append(hp)
            h_ref[pl.ds(r0, SUBLANES), :] = _rows_to_tile(rows)
            return hp

        carry[0:1, :] = lax.fori_loop(0, tm // SUBLANES, body, carry[0:1, :])
        y_ref[...] = _rg_out(h_ref[...], g_ref[...]).astype(BF16)

    row = pl.BlockSpec((tm, 512), lambda i: (i, 0))
    mat = pl.BlockSpec((512, 512), lambda i: (0, 0))
    vec = pl.BlockSpec((1, 512), lambda i: (0, 0))
    return pl.pallas_call(
        kern, name=name, grid=(s // tm,),
        in_specs=[row, pl.BlockSpec((tm, 512), lambda i: (i, CD_GATE // 512)), mat, mat, vec, vec, vec],
        out_specs=[row, row],
        out_shape=[jax.ShapeDtypeStruct((s, 512), BF16), jax.ShapeDtypeStruct((s, 512), F32)],
        scratch_shapes=[pltpu.VMEM((tm, 512), F32), pltpu.VMEM((tm, 512), F32), pltpu.VMEM((SUBLANES, 512), F32)],
        compiler_params=_cp(("arbitrary",)),
    )(xc, hc, wa, wx, ba, bx, lam)


def _rg_bwd(dy, dy_cb, xc, hc, hs, wa, wx, ba, bx, lam, name):
    s = xc.shape[0]
    tm = min(RG_TILE, s)
    nt = s // tm
    ti = lambda i: nt - 1 - i

    def kern(dy_ref, x_ref, g_ref, h_ref, hp_ref, wa_ref, wx_ref, ba_ref, bx_ref, lam_ref,
             dx_ref, dg_ref, dwa_ref, dwx_ref, dba_ref, dbx_ref, dlam_ref, a_s, d_s, g_s, cg, ca):
        first = pl.program_id(0) == 0

        @pl.when(first)
        def _():
            cg[...] = jnp.zeros_like(cg)
            ca[...] = jnp.zeros_like(ca)

        (a, _), pull = jax.vjp(_rg_gates, x_ref[...], wa_ref[...].astype(F32), wx_ref[...].astype(F32), ba_ref[...],
                               bx_ref[...], lam_ref[...])
        hs_t = h_ref[...]
        _, pull_out = jax.vjp(_rg_out, hs_t, g_ref[...])
        dho, dgate = pull_out(dy_ref[...])
        dg_ref[...] = dgate.astype(BF16)
        a_s[...] = a
        d_s[...] = dho

        def body(k, c):
            gn, an = c
            r0 = pl.multiple_of((tm // SUBLANES - 1 - k) * SUBLANES, SUBLANES)
            ab, db = a_s[pl.ds(r0, SUBLANES), :], d_s[pl.ds(r0, SUBLANES), :]
            rows = [None] * SUBLANES
            for j in range(SUBLANES - 1, -1, -1):
                gn = db[j:j + 1] + an * gn
                an = ab[j:j + 1]
                rows[j] = gn
            g_s[pl.ds(r0, SUBLANES), :] = _rows_to_tile(rows)
            return gn, an

        gn, an = lax.fori_loop(0, tm // SUBLANES, body, (cg[0:1, :], ca[0:1, :]))
        cg[0:1, :] = gn
        ca[0:1, :] = an
        g = g_s[...]
        halo = jnp.where(pl.program_id(0) == nt - 1, 0.0, hp_ref[...])
        hprev = pltpu.roll(jnp.concatenate([halo, hs_t], axis=0), 1, axis=0)[SUBLANES:]
        dxc, dwa, dwx, dba, dbx, dlam = pull((g * hprev, g))
        dx_ref[...] = dxc
        _acc(dwa_ref, dwa, first)
        _acc(dwx_ref, dwx, first)
        _acc(dba_ref, dba, first)
        _acc(dbx_ref, dbx, first)
        _acc(dlam_ref, dlam, first)

    row = pl.BlockSpec((tm, 512), lambda i: (ti(i), 0))
    mat = pl.BlockSpec((512, 512), lambda i: (0, 0))
    vec = pl.BlockSpec((1, 512), lambda i: (0, 0))
    hprev = pl.BlockSpec((SUBLANES, 512), lambda i: (jnp.maximum(ti(i) * (tm // SUBLANES) - 1, 0), 0))
    return pl.pallas_call(
        kern, name=name, grid=(nt,),
        in_specs=[pl.BlockSpec((tm, 512), lambda i: (ti(i), dy_cb)), row,
                  pl.BlockSpec((tm, 512), lambda i: (ti(i), CD_GATE // 512)), row, hprev, mat, mat, vec, vec, vec],
        out_specs=[row, row, mat, mat, vec, vec, vec],
        out_shape=[jax.ShapeDtypeStruct((s, 512), F32), jax.ShapeDtypeStruct((s, 512), BF16),
                   jax.ShapeDtypeStruct((512, 512), F32), jax.ShapeDtypeStruct((512, 512), F32),
                   jax.ShapeDtypeStruct((1, 512), F32), jax.ShapeDtypeStruct((1, 512), F32), jax.ShapeDtypeStruct((1, 512), F32)],
        scratch_shapes=[pltpu.VMEM((tm, 512), F32)] * 3 + [pltpu.VMEM((SUBLANES, 512), F32)] * 2,
        compiler_params=_cp(("arbitrary",)),
    )(dy, xc, hc, hs, hs, wa, wx, ba, bx, lam)


AB_REF = dict(z=(0, 512), xbc=(512, 768), dt=(1280, 8), hq=(1288, 512), hf=(1800, 512), hi=(2312, 512), hg=(2824, 512))
AB_INT = dict(z=AB_Z, hq=AB_HQ, hf=AB_HF, hi=AB_HI, hg=AB_HG, xbc=AB_XBC, dt=AB_DT)
CD_REF = dict(q=(0, 512), k=(512, 128), v=(640, 128), gate=(768, 512), xr=(1280, 512))


def _cols(w, start, size):
    return lax.slice_in_dim(w, start, start + size, axis=w.ndim - 1)


def _pad_lanes(v):
    return jnp.concatenate([v.astype(F32), jnp.zeros((LANES - v.shape[0],), F32)]).reshape(1, LANES)


def _ab_in_to_int(w):
    parts = [_cols(w, *AB_REF[k]) for k in ("z", "hq", "hf", "hi", "hg", "xbc", "dt")]
    return jnp.concatenate(parts + [jnp.zeros(w.shape[:-1] + (AB_COLS - AB_DT - 8,), w.dtype)], axis=-1)


def _ab_in_to_ref(w):
    return jnp.concatenate([_cols(w, AB_INT[k], AB_REF[k][1]) for k in ("z", "xbc", "dt", "hq", "hf", "hi", "hg")], axis=-1)


def _pairs_to_int(w, axis):
    sl = lambda h: lax.slice_in_dim(w, 64 * h, 64 * h + 64, axis=axis)
    return jnp.concatenate([sl(h) for p in range(4) for h in (p, 4 + p)], axis=axis)


def _pairs_to_ref(w, axis):
    sl = lambda i: lax.slice_in_dim(w, 64 * i, 64 * i + 64, axis=axis)
    return jnp.concatenate([sl(2 * h) for h in range(4)] + [sl(2 * h + 1) for h in range(4)], axis=axis)


def _cd_in_to_int(w):
    q = _pairs_to_int(_cols(w, *CD_REF["q"]), w.ndim - 1)
    return jnp.concatenate([q] + [_cols(w, *CD_REF[k]) for k in ("gate", "xr", "k", "v")], axis=-1)


def _cd_in_to_ref(w):
    q = _pairs_to_ref(_cols(w, CD_Q, 512), w.ndim - 1)
    return jnp.concatenate([q, _cols(w, CD_K, 128), _cols(w, CD_V, 128), _cols(w, CD_GATE, 512), _cols(w, CD_XR, 512)], axis=-1)


def _block_diag(w):
    out = jnp.zeros((RG_WIDTH, RG_WIDTH), w.dtype)
    for g in range(8):
        out = lax.dynamic_update_slice(out, w[g], (64 * g, 64 * g))
    return out


def _diag_blocks(m):
    return jnp.stack([m[64 * g:64 * g + 64, 64 * g:64 * g + 64] for g in range(8)])


def _conv8(w):
    return jnp.concatenate([w, jnp.zeros((SUBLANES - w.shape[0], w.shape[1]), w.dtype)], axis=0)


def _hg_lower_bounds(hg_lower):
    sm = jax.nn.softmax(hg_lower.astype(F32), axis=0)
    return jnp.clip(jnp.cumsum(sm, axis=0) - sm[0], 0.0, 1.0)


def _local_step(x, target, w):
    s = x.shape[0]
    lb_all, lb_pull = jax.vjp(_hg_lower_bounds, w["hg_lower"])
    grads = {k: [None] * v.shape[0] for k, v in w.items()}
    saved = []
    xb = x.astype(BF16)
    for l in range(DEPTH):
        j = l // 2
        t = f"l{l}"
        sv = dict(x=x, xb=xb)
        if l % 2 == 0:
            win = _ab_in_to_int(w["ab_w_in"][j])
            wout = w["ab_w_out"][j]
            h = _mm([(xb, 0, win, 0, D_MODEL)], AB_COLS, F32, t + "_in")
            w8 = _conv8(w["ssd_conv_w"][j])
            xbcc = _dwconv_fwd(h, AB_XBC // CONV_CB, SSD_CONV_DIM, w8, w["ssd_conv_b"][j], 4, t + "_conv")
            sp = (_pad_lanes(w["ssd_dt_bias"][j]), _pad_lanes(w["ssd_a_log"][j]), _pad_lanes(w["ssd_d"][j]),
                  w["ssd_norm_w"][j].reshape(1, 512))
            ya, ssts = _ssd_fwd(h, xbcc, *sp, t + "_ssd")
            hp = (lb_all[j].reshape(1, 512), w["hg_norm_w"][j].reshape(1, LANES))
            yb, hsts = _hg_fwd(h, *hp, t + "_hg")
            sv.update(win=win, wout=wout, h=h, w8=w8, xbcc=xbcc, sp=sp, ssts=ssts, hp=hp, hsts=hsts)
        else:
            win = _cd_in_to_int(w["cd_w_in"][j])
            wout = jnp.concatenate([_pairs_to_int(w["cd_w_out"][j][:512], 0), w["cd_w_out"][j][512:]], axis=0)
            h = _mm([(xb, 0, win, 0, D_MODEL)], CD_COLS, F32, t + "_in")
            srow = _pad_lanes(w["swa_sinks"][j])
            ya = _swa_fwd(h, srow, t + "_swa")
            w8 = _conv8(w["rg_conv_w"][j])
            xc = _dwconv_fwd(h, CD_XR // CONV_CB, RG_WIDTH, w8, w["rg_conv_b"][j], 4, t + "_conv")
            rp = (_block_diag(w["rg_wa"][j]).astype(BF16), _block_diag(w["rg_wx"][j]).astype(BF16),
                  w["rg_ba"][j].reshape(1, 512), w["rg_bx"][j].reshape(1, 512), w["rg_lambda"][j].reshape(1, 512))
            yb, hs = _rg_fwd(xc, h, *rp, t + "_rg")
            sv.update(win=win, wout=wout, h=h, w8=w8, srow=srow, xc=xc, rp=rp, hs=hs)
        m = _mm([(ya, 0, wout, 0, 512), (yb, 0, wout, 1, 512)], D_MODEL, F32, t + "_out")
        x1, x1b, r1 = _ln_fwd(x, m, w["ln_g"][l, 0], w["ln_b"][l, 0], t + "_ln1")
        wup, wdown = w["ffn_w_up"][l], w["ffn_w_down"][l]
        fw8 = _conv8(w["ffn_conv_w"][l])
        hu = _mm([(x1b, 0, wup, 0, D_MODEL)], 2 * FFN_DIM, BF16, t + "_up")
        a = _ffn_gate_fwd(hu, fw8, w["ffn_conv_b"][l], t + "_gate")
        f = _mm([(a, 0, wdown, 0, FFN_DIM)], D_MODEL, F32, t + "_down")
        x, xb, r2 = _ln_fwd(x1, f, w["ln_g"][l, 1], w["ln_b"][l, 1], t + "_ln2")
        sv.update(ya=ya, yb=yb, r1=r1, x1b=x1b, hu=hu, a=a, r2=r2, fw8=fw8)
        saved.append(sv)

    loss = 0.5 * jnp.sum(_sqerr(x, target, "sqerr")) / D_MODEL
    dya, dyb, ca, cb = x, target, 1.0 / D_MODEL, -1.0 / D_MODEL
    for l in range(DEPTH - 1, -1, -1):
        j = l // 2
        t = f"l{l}b"
        sv = saved[l]
        wup, wdown = w["ffn_w_up"][l], w["ffn_w_down"][l]
        dr2, dg2, db2 = _ln_bwd(dya, dyb, ca, cb, sv["r2"], w["ln_g"][l, 1], t + "_ln2")
        da = _mm([(dr2, 0, wdown, 0, D_MODEL)], FFN_DIM, BF16, t + "_down", trans_b=True)
        grads["ffn_w_down"][l] = _mm_tn(sv["a"], 0, FFN_DIM, dr2, 0, D_MODEL, t + "_wdown")
        dhg, dhu, dfw8, dfb = _ffn_gate_bwd(sv["hu"], da, sv["fw8"], w["ffn_conv_b"][l], t + "_gate")
        grads["ffn_conv_w"][l] = dfw8[:3]
        grads["ffn_conv_b"][l] = dfb
        dx1 = _mm([(dhg, 0, wup, 0, FFN_DIM), (dhu, 0, wup, 1, FFN_DIM)], D_MODEL, F32, t + "_up", trans_b=True)
        grads["ffn_w_up"][l] = jnp.concatenate([_mm_tn(sv["x1b"], 0, D_MODEL, dhg, 0, FFN_DIM, t + "_wup_g"),
                                                _mm_tn(sv["x1b"], 0, D_MODEL, dhu, 0, FFN_DIM, t + "_wup_u")], axis=1)
        dr1, dg1, db1 = _ln_bwd(dr2, dx1, ALPHA, 1.0, sv["r1"], w["ln_g"][l, 0], t + "_ln1")
        grads["ln_g"][l] = jnp.stack([dg1, dg2])
        grads["ln_b"][l] = jnp.stack([db1, db2])
        win, wout, h = sv["win"], sv["wout"], sv["h"]
        dycat = _mm([(dr1, 0, wout, 0, D_MODEL)], D_MODEL, F32, t + "_out", trans_b=True)
        dwout = jnp.concatenate([_mm_tn(sv["ya"], 0, 512, dr1, 0, D_MODEL, t + "_wout_a"),
                                 _mm_tn(sv["yb"], 0, 512, dr1, 0, D_MODEL, t + "_wout_b")], axis=0)
        if l % 2 == 0:
            dz, dxbcc, ddt, ddtb, dal, ddsk, dnw = _ssd_bwd(dycat, 0, h, sv["xbcc"], sv["ssts"], *sv["sp"], t + "_ssd")
            dxbc, dcw, dcb = _dwconv_bwd(dxbcc, h, AB_XBC // CONV_CB, SSD_CONV_DIM, sv["w8"], 4, t + "_conv")
            dq, df, di, dg, dlb, dhnw = _hg_bwd(dycat, 1, h, sv["hsts"], *sv["hp"], t + "_hg")
            grads["ab_w_out"][j] = dwout
            grads["ssd_conv_w"][j], grads["ssd_conv_b"][j] = dcw[:4], dcb
            grads["ssd_dt_bias"][j], grads["ssd_a_log"][j], grads["ssd_d"][j] = ddtb[0, :8], dal[0, :8], ddsk[0, :8]
            grads["ssd_norm_w"][j], grads["hg_norm_w"][j] = dnw[0], dhnw[0]
            grads["hg_lower"][j] = dlb[0]
            pieces = [(dz, 0, AB_Z, 512), (dq, 0, AB_HQ, 512), (df, 0, AB_HF, 512), (di, 0, AB_HI, 512), (dg, 0, AB_HG, 512),
                      (dxbc, 0, AB_XBC, 256), (dxbc, 1, AB_XBC + 256, 256), (dxbc, 2, AB_XBC + 512, 256), (ddt, 0, AB_DT, LANES)]
            dwin = [_mm_tn(sv["xb"], 0, D_MODEL, p, 0, p.shape[1], t + f"_win{i}")
                    for i, p in enumerate((dz, dq, df, di, dg, dxbc, ddt))]
            grads["ab_w_in"][j] = _ab_in_to_ref(jnp.concatenate(dwin, axis=1))
        else:
            dq, dk, dv, dsr = _swa_bwd(dycat, 0, h, sv["srow"], t + "_swa")
            dxc, dgate, dwa, dwx, dba, dbx, dlam = _rg_bwd(dycat, 1, sv["xc"], h, sv["hs"], *sv["rp"], t + "_rg")
            dxr, dcw, dcb = _dwconv_bwd(dxc, h, CD_XR // CONV_CB, RG_WIDTH, sv["w8"], 4, t + "_conv")
            grads["cd_w_out"][j] = jnp.concatenate([_pairs_to_ref(dwout[:512], 0), dwout[512:]], axis=0)
            grads["swa_sinks"][j] = dsr[0, :8]
            grads["rg_conv_w"][j], grads["rg_conv_b"][j] = dcw[:4], dcb
            grads["rg_wa"][j], grads["rg_wx"][j] = _diag_blocks(dwa), _diag_blocks(dwx)
            grads["rg_ba"][j], grads["rg_bx"][j], grads["rg_lambda"][j] = dba[0], dbx[0], dlam[0]
            pieces = [(dq, 0, CD_Q, 512), (dgate, 0, CD_GATE, 512), (dxr, 0, CD_XR, 512), (dk, 0, CD_K, LANES), (dv, 0, CD_V, LANES)]
            dwin = [_mm_tn(sv["xb"], 0, D_MODEL, p, 0, p.shape[1], t + f"_win{i}")
                    for i, p in enumerate((dq, dgate, dxr, dk, dv))]
            grads["cd_w_in"][j] = _cd_in_to_ref(jnp.concatenate(dwin, axis=1))
        dx0 = _mm([(p, pcb, win, off // k, k) for p, pcb, off, k in pieces], D_MODEL, F32, t + "_in", trans_b=True)
        dya, dyb, ca, cb = dr1, dx0, ALPHA, 1.0
    grad_x = _axpby(dya, dyb, ca, cb, "grad_x")
    out = {k: (v if k in BIG else jnp.stack(v)) for k, v in grads.items()}
    (out["hg_lower"],) = lb_pull(out["hg_lower"])
    return loss, grad_x, out


MESH = pl.DeviceIdType.MESH
ANY = pl.BlockSpec(memory_space=pl.ANY)
PACK_ROWS = 2048


def _place():
    x, y, c = lax.axis_index("x"), lax.axis_index("y"), lax.axis_index("c")
    return x, y, c, [(1 - x, y), (x, 1 - y), (1 - x, 1 - y)]


def _rcopy(src, dst, send_sems, recv_sems, k, to):
    return pltpu.make_async_remote_copy(src_ref=src, dst_ref=dst, send_sem=send_sems.at[k], recv_sem=recv_sems.at[k],
                                        device_id=to, device_id_type=MESH)


def _gather_chips(wp, name):
    def body(w_ref, out_ref, send_sems, recv_sems, local_sem):
        x, y, c, chips = _place()
        me, sib = 2 * x + y, (x, y, 1 - c)
        mine = pltpu.make_async_copy(w_ref, out_ref.at[me], local_sem)
        mine.start()
        first = [_rcopy(w_ref.at[c], out_ref.at[me, c], send_sems, recv_sems, k, (cx, cy, c)) for k, (cx, cy) in enumerate(chips)]
        for cp in first:
            cp.start()
        passed = []
        for k, (cx, cy) in enumerate(chips):
            blk = out_ref.at[2 * cx + cy, c]
            _rcopy(blk, blk, send_sems, recv_sems, k, (cx, cy, c)).wait_recv()
            passed.append(_rcopy(blk, blk, send_sems, recv_sems, 3 + k, sib))
            passed[-1].start()
        for k, (cx, cy) in enumerate(chips):
            blk = out_ref.at[2 * cx + cy, 1 - c]
            _rcopy(blk, blk, send_sems, recv_sems, 3 + k, sib).wait_recv()
        for cp in first + passed:
            cp.wait_send()
        mine.wait()

    return pl.pallas_call(
        body, name=name, in_specs=[ANY], out_specs=ANY, out_shape=jax.ShapeDtypeStruct((4,) + wp.shape, wp.dtype),
        scratch_shapes=[pltpu.SemaphoreType.DMA((6,)), pltpu.SemaphoreType.DMA((6,)), pltpu.SemaphoreType.DMA(())],
    )(wp)


def _swap_halves(g, name):
    def body(g_ref, x_ref, send_sems, recv_sems):
        x, y, c, _ = _place()
        cp = _rcopy(g_ref.at[:, 1 - c], x_ref, send_sems, recv_sems, 0, (x, y, 1 - c))
        cp.start()
        cp.wait()

    return pl.pallas_call(
        body, name=name, in_specs=[ANY], out_specs=ANY,
        out_shape=jax.ShapeDtypeStruct((g.shape[0],) + g.shape[2:], g.dtype),
        scratch_shapes=[pltpu.SemaphoreType.DMA((1,)), pltpu.SemaphoreType.DMA((1,))],
    )(g)


def _scatter_chips(hb, name):
    def body(h_ref, y_ref, send_sems, recv_sems, local_sem):
        x, y, c, chips = _place()
        me = 2 * x + y
        mine = pltpu.make_async_copy(h_ref.at[me], y_ref.at[me], local_sem)
        mine.start()
        sends = [_rcopy(h_ref.at[2 * cx + cy], y_ref.at[me], send_sems, recv_sems, k, (cx, cy, c)) for k, (cx, cy) in enumerate(chips)]
        for cp in sends:
            cp.start()
        for k, (cx, cy) in enumerate(chips):
            blk = y_ref.at[2 * cx + cy]
            _rcopy(blk, blk, send_sems, recv_sems, k, (cx, cy, c)).wait_recv()
        for cp in sends:
            cp.wait_send()
        mine.wait()

    return pl.pallas_call(
        body, name=name, in_specs=[ANY], out_specs=ANY, out_shape=jax.ShapeDtypeStruct(hb.shape, hb.dtype),
        scratch_shapes=[pltpu.SemaphoreType.DMA((3,)), pltpu.SemaphoreType.DMA((3,)), pltpu.SemaphoreType.DMA(())],
    )(hb)


def _share_half(f, name):
    def body(f_ref, out_ref, send_sems, recv_sems, local_sem):
        x, y, c, _ = _place()
        sib = (x, y, 1 - c)
        mine = pltpu.make_async_copy(f_ref, out_ref.at[c], local_sem)
        mine.start()
        cp = _rcopy(f_ref, out_ref.at[c], send_sems, recv_sems, 0, sib)
        cp.start()
        _rcopy(f_ref, out_ref.at[1 - c], send_sems, recv_sems, 0, sib).wait_recv()
        cp.wait_send()
        mine.wait()

    return pl.pallas_call(
        body, name=name, in_specs=[ANY], out_specs=ANY, out_shape=jax.ShapeDtypeStruct((2,) + f.shape, f.dtype),
        scratch_shapes=[pltpu.SemaphoreType.DMA((1,)), pltpu.SemaphoreType.DMA((1,)), pltpu.SemaphoreType.DMA(())],
    )(f)


def _gather_devices(v, name):
    m_per, n = v.shape

    def body(x_ref, out_ref, send_sems, recv_sems, local_sem):
        x, y, c, chips = _place()
        me, sib = (x, y, c), (x, y, 1 - c)

        def rows(px, py, pc):
            return out_ref.at[pl.ds((4 * px + 2 * py + pc) * m_per, m_per), :]

        def copy(k, block, to, src=None):
            return _rcopy(rows(*block) if src is None else src, rows(*block), send_sems, recv_sems, k, to)

        mine = pltpu.make_async_copy(x_ref, rows(*me), local_sem)
        mine.start()
        first = [copy(0, me, sib, src=x_ref)]
        first += [copy(1 + j, me, (*chip, c), src=x_ref) for j, chip in enumerate(chips)]
        for cp in first:
            cp.start()
        passed = [copy(4 + j, (*chip, c), sib) for j, chip in enumerate(chips)]
        for j, chip in enumerate(chips):
            copy(1 + j, (*chip, c), me).wait_recv()
            passed[j].start()
        copy(0, sib, me).wait_recv()
        for j, chip in enumerate(chips):
            copy(4 + j, (*chip, 1 - c), me).wait_recv()
        for cp in first + passed:
            cp.wait_send()
        mine.wait()

    return pl.pallas_call(
        body, name=name, out_shape=jax.ShapeDtypeStruct((8 * m_per, n), v.dtype),
        in_specs=[pl.BlockSpec(memory_space=pltpu.VMEM)], out_specs=pl.BlockSpec(memory_space=pltpu.VMEM),
        scratch_shapes=[pltpu.SemaphoreType.DMA((7,)), pltpu.SemaphoreType.DMA((7,)), pltpu.SemaphoreType.DMA(())],
        compiler_params=_cp(),
    )(v)


def _add_half(g, xr, c, name, tr=2048):
    rh = g.shape[2]

    def kern(c_ref, g_ref, x_ref, o_ref):
        o_ref[...] = (g_ref[0] + x_ref[...]).astype(BF16)

    return pl.pallas_call(
        kern, name=name,
        grid_spec=pltpu.PrefetchScalarGridSpec(
            num_scalar_prefetch=1, grid=(4, rh // tr),
            in_specs=[pl.BlockSpec((1, 1, tr, LANES), lambda k, i, c_ref: (k, c_ref[0], i, 0)),
                      pl.BlockSpec((1, tr, LANES), lambda k, i, c_ref: (k, i, 0))],
            out_specs=pl.BlockSpec((1, tr, LANES), lambda k, i, c_ref: (k, i, 0))),
        out_shape=jax.ShapeDtypeStruct(xr.shape, BF16), compiler_params=_cp(("parallel", "parallel")),
    )(c.reshape(1).astype(jnp.int32), g, xr)


def _sum_blocks(y, name, tr=2048):
    n, r, _ = y.shape
    tr = tr if r % tr == 0 else r

    def kern(y_ref, o_ref):
        acc = y_ref[0].astype(F32)
        for k in range(1, n):
            acc = acc + y_ref[k].astype(F32)
        o_ref[...] = acc

    return pl.pallas_call(
        kern, name=name, grid=(r // tr,), in_specs=[pl.BlockSpec((n, tr, LANES), lambda i: (0, i, 0))],
        out_specs=pl.BlockSpec((tr, LANES), lambda i: (i, 0)), out_shape=jax.ShapeDtypeStruct((r, LANES), F32),
        compiler_params=_cp(("parallel",)),
    )(y)


def _adamw(w, g, m, v, name, tr=256):
    r, n = w.shape
    tr = tr if r % tr == 0 else r

    def kern(w_ref, g_ref, m_ref, v_ref, d_ref, nm_ref, nv_ref):
        gg = g_ref[...]
        nm = ADAM_B1 * m_ref[...] + (1.0 - ADAM_B1) * gg
        nv = ADAM_B2 * v_ref[...] + (1.0 - ADAM_B2) * (gg * gg)
        m_hat = nm / (1.0 - ADAM_B1 ** ADAM_STEP)
        v_hat = nv / (1.0 - ADAM_B2 ** ADAM_STEP)
        d_ref[...] = -ADAM_LR * (m_hat / (jnp.sqrt(v_hat) + ADAM_EPS) + ADAM_WD * w_ref[...])
        nm_ref[...] = nm
        nv_ref[...] = nv

    blk = pl.BlockSpec((tr, n), lambda i: (i, 0))
    return pl.pallas_call(
        kern, name=name, grid=(r // tr,), in_specs=[blk] * 4, out_specs=[blk] * 3,
        out_shape=[jax.ShapeDtypeStruct((r, n), F32)] * 3, compiler_params=_cp(("parallel",)),
    )(w, g, m, v)


def _rows_of(n):
    return -(-n // (LANES * SUBLANES)) * SUBLANES


def _pack(arrs, lead, dtype, mult):
    ls = arrs[0].shape[:lead]
    blocks, total = [], 0
    for a in arrs:
        f = a.astype(dtype).reshape(ls + (-1,))
        n = f.shape[-1]
        rows = _rows_of(n)
        if rows * LANES != n:
            f = jnp.concatenate([f, jnp.zeros(ls + (rows * LANES - n,), dtype)], axis=-1)
        blocks.append(f.reshape(ls + (rows, LANES)))
        total += rows
    if total % mult:
        blocks.append(jnp.zeros(ls + (mult - total % mult, LANES), dtype))
    return jnp.concatenate(blocks, axis=lead)


def _unpack(buf, lead, shapes):
    ls = buf.shape[:lead]
    out, off = [], 0
    for sh in shapes:
        n = int(np.prod(sh))
        rows = _rows_of(n)
        flat = lax.slice_in_dim(buf, off, off + rows, axis=lead).reshape(ls + (rows * LANES,))
        if rows * LANES != n:
            flat = lax.slice_in_dim(flat, 0, n, axis=lead)
        out.append(flat.reshape(ls + tuple(sh)))
        off += rows
    return out


BIG = ("ab_w_in", "ab_w_out", "cd_w_in", "cd_w_out", "ffn_w_up", "ffn_w_down")
BIG_COLS = ("ab_w_in", "cd_w_in", "ffn_w_up")
SMALL_SHARDED = ("ssd_conv_w", "rg_conv_w", "rg_conv_b", "rg_ba", "rg_bx", "rg_lambda", "ffn_conv_w", "ln_g", "ln_b")
WEIGHTS = ("ab_w_in", "ssd_conv_w", "ssd_conv_b", "ssd_dt_bias", "ssd_a_log", "ssd_d", "ssd_norm_w", "hg_lower", "hg_norm_w",
           "ab_w_out", "cd_w_in", "swa_sinks", "rg_conv_w", "rg_conv_b", "rg_wa", "rg_ba", "rg_wx", "rg_bx", "rg_lambda",
           "cd_w_out", "ffn_w_up", "ffn_conv_w", "ffn_conv_b", "ffn_w_down", "ln_g", "ln_b")
SMALL = tuple(n for n in WEIGHTS if n not in BIG)


def _full_from_shards(name, g):
    if name in BIG_COLS or name in SMALL_SHARDED:
        return jnp.concatenate([g[k] for k in range(4)], axis=-1)
    return jnp.concatenate([g[k] for k in range(4)], axis=1)


def _layer_shards(name, f):
    if name in BIG_COLS:
        k, n = f.shape
        return jnp.transpose(f.reshape(k, 4, n // 4), (1, 0, 2)).reshape(4, -1)
    return f.reshape(4, -1)


def kernel(x, ab_w_in, ssd_conv_w, ssd_conv_b, ssd_dt_bias, ssd_a_log, ssd_d, ssd_norm_w, hg_lower, hg_norm_w, ab_w_out, cd_w_in, swa_sinks, rg_conv_w, rg_conv_b, rg_wa, rg_ba, rg_wx, rg_bx, rg_lambda, cd_w_out, ffn_w_up, ffn_conv_w, ffn_conv_b, ffn_w_down, ln_g, ln_b, loss_target, m_ab_w_in, m_ssd_conv_w, m_ssd_conv_b, m_ssd_dt_bias, m_ssd_a_log, m_ssd_d, m_ssd_norm_w, m_hg_lower, m_hg_norm_w, m_ab_w_out, m_cd_w_in, m_swa_sinks, m_rg_conv_w, m_rg_conv_b, m_rg_wa, m_rg_ba, m_rg_wx, m_rg_bx, m_rg_lambda, m_cd_w_out, m_ffn_w_up, m_ffn_conv_w, m_ffn_conv_b, m_ffn_w_down, m_ln_g, m_ln_b, v_ab_w_in, v_ssd_conv_w, v_ssd_conv_b, v_ssd_dt_bias, v_ssd_a_log, v_ssd_d, v_ssd_norm_w, v_hg_lower, v_hg_norm_w, v_ab_w_out, v_cd_w_in, v_swa_sinks, v_rg_conv_w, v_rg_conv_b, v_rg_wa, v_rg_ba, v_rg_wx, v_rg_bx, v_rg_lambda, v_cd_w_out, v_ffn_w_up, v_ffn_conv_w, v_ffn_conv_b, v_ffn_w_down, v_ln_g, v_ln_b):
    args = locals()
    w = {n: args[n] for n in WEIGHTS}
    mom = {n: args["m_" + n] for n in WEIGHTS}
    var = {n: args["v_" + n] for n in WEIGHTS}
    cx, cy, cc = lax.axis_index("x"), lax.axis_index("y"), lax.axis_index("c")
    chip = 2 * cx + cy

    big_shapes = [w[n].shape for n in BIG]
    wp = _pack([w[n] for n in BIG], 0, BF16, 2 * PACK_ROWS)
    rh = wp.shape[0] // 2
    gathered = _gather_chips(wp.reshape(2, rh, LANES), "gather_weights").reshape(4, 2 * rh, LANES)
    full = {n: _full_from_shards(n, g) for n, g in zip(BIG, _unpack(gathered, 1, big_shapes))}
    sp = _pack([w[n] for n in SMALL_SHARDED], 0, F32, SUBLANES)
    sg = _gather_devices(sp, "gather_small").reshape(4, 2, sp.shape[0], LANES)[:, 0]
    for n, g in zip(SMALL_SHARDED, _unpack(sg, 1, [w[n].shape for n in SMALL_SHARDED])):
        full[n] = _full_from_shards(n, g)
    for n in SMALL:
        full.setdefault(n, w[n])

    loss, grad_x, grads = _local_step(x[0], loss_target[0], full)
    loss = lax.psum(loss, ("x", "y", "c"))

    g4 = _pack([jnp.concatenate([_layer_shards(n, f) for f in grads[n]], axis=1) for n in BIG], 1, F32,
               2 * PACK_ROWS).reshape(4, 2, rh, LANES)
    hb = _add_half(g4, _swap_halves(g4, "swap_halves"), cc, "add_halves")
    own = _sum_blocks(_scatter_chips(hb, "scatter_chips"), "sum_chips")
    gbig = _unpack(_share_half(own, "share_half").reshape(2 * rh, LANES), 0, big_shapes)
    gout = dict(zip(BIG, gbig))

    small_shapes = [grads[n].shape for n in SMALL]
    gs = _pack([grads[n] for n in SMALL], 0, F32, SUBLANES)
    gsum = _sum_blocks(_gather_devices(gs, "gather_small_grads").reshape(8, gs.shape[0], LANES), "sum_small")
    for n, g in zip(SMALL, _unpack(gsum, 0, small_shapes)):
        if n in SMALL_SHARDED:
            width = w[n].shape[-1]
            g = lax.dynamic_slice_in_dim(g, chip * width, width, axis=g.ndim - 1)
        gout[n] = g

    delta, new_m, new_v = {}, {}, {}
    for n in BIG:
        sh = w[n].shape
        two = lambda a: a.reshape(-1, sh[-1])
        d, nm, nv = _adamw(two(w[n]), two(gout[n]), two(mom[n]), two(var[n]), "adamw_" + n)
        delta[n], new_m[n], new_v[n] = d.reshape(sh), nm.reshape(sh), nv.reshape(sh)
    local_shapes = [w[n].shape for n in SMALL]
    packs = [_pack([d[n] for n in SMALL], 0, F32, SUBLANES) for d in (w, gout, mom, var)]
    for dst, buf in zip((delta, new_m, new_v), _adamw(*packs, "adamw_small")):
        dst.update(zip(SMALL, _unpack(buf, 0, local_shapes)))

    return (loss, grad_x[None], *[gout[n] for n in WEIGHTS], *[delta[n] for n in WEIGHTS],
            *[new_m[n] for n in WEIGHTS], *[new_v[n] for n in WEIGHTS])
```

```python
import functools
import math

import numpy as np
import jax
import jax.numpy as jnp
from jax import lax
from jax.experimental import pallas as pl
from jax.experimental.pallas import tpu as pltpu

F32 = jnp.float32
BF16 = jnp.bfloat16
HI = lax.Precision.HIGHEST

D_MODEL = 1024
DEPTH = 4
SSD_HEADS = 8
SSD_D_INNER = 512
SSD_CONV_DIM = 768
SSD_CHUNK = 128
HG_CHUNK = 64
HG_HEADS = 4
SWA_BLOCK = 128
RG_WIDTH = 512
FFN_DIM = 2816
LN_EPS = 1e-5
RMS_EPS = 1e-6
MASK_VALUE = -1e9
ALPHA = (2 * DEPTH) ** 0.25
RG_C = 8.0
ADAM_LR, ADAM_B1, ADAM_B2, ADAM_EPS, ADAM_WD, ADAM_STEP = 0.001, 0.9, 0.999, 1e-08, 0.01, 10

LANES = 128
SUBLANES = 8
HALO = 16
CONV_CB = 256
VMEM_LIMIT = 56 * 1024 * 1024

AB_Z, AB_HQ, AB_HF, AB_HI, AB_HG, AB_XBC, AB_DT, AB_COLS = 0, 512, 1024, 1536, 2048, 2560, 3328, 3456
CD_Q, CD_GATE, CD_XR, CD_K, CD_V, CD_COLS = 0, 512, 1024, 1536, 1664, 1792


def _cp(sem=None):
    kw = dict(vmem_limit_bytes=VMEM_LIMIT)
    if sem is not None:
        kw["dimension_semantics"] = sem
    return pltpu.CompilerParams(**kw)


def _dot(a, b, dims=(((1,), (0,)), ((), ())), precision=None):
    return lax.dot_general(a, b, dims, precision=precision, preferred_element_type=F32)


NN = (((1,), (0,)), ((), ()))
NT = (((1,), (1,)), ((), ()))
TN = (((0,), (0,)), ((), ()))


@functools.partial(jax.custom_vjp, nondiff_argnums=(2,))
def _bdot(a, b, dims=NN):
    return _dot(a.astype(BF16), b.astype(BF16), dims)


def _bdot_fwd(a, b, dims):
    return _bdot(a, b, dims), (a, b)


def _bdot_bwd(dims, res, ct):
    a, b = res
    ab, bb, cb = a.astype(BF16), b.astype(BF16), ct.astype(BF16)
    if dims == NN:
        da, db = _dot(cb, bb, NT), _dot(ab, cb, TN)
    elif dims == NT:
        da, db = _dot(cb, bb, NN), _dot(cb, ab, TN)
    else:
        da, db = _dot(bb, cb, NT), _dot(ab, cb, NN)
    return da.astype(a.dtype), db.astype(b.dtype)


_bdot.defvjp(_bdot_fwd, _bdot_bwd)


def _hdot(a, b, dims=NN):
    return _dot(a, b, dims, precision=HI)


WIDE_TILES = (1408, 1152, 1024, 896, 768, 512, 384, 256, 128)
MM_BLOCK_BYTES = 6 * 1024 * 1024


def _pick(n, cands):
    for c in cands:
        if n % c == 0:
            return c
    return n


def _mm(pairs, n_out, out_dtype, name, trans_b=False, tm=512):
    m = pairs[0][0].shape[0]
    a_row_bytes = sum(p[4] * p[0].dtype.itemsize for p in pairs)
    b_col_bytes = sum(p[4] * p[2].dtype.itemsize for p in pairs)
    tm = min(2 * tm if 2 * tm * a_row_bytes <= MM_BLOCK_BYTES else tm, m)
    tn = _pick(n_out, [c for c in WIDE_TILES if c * b_col_bytes <= MM_BLOCK_BYTES])
    n = len(pairs)

    def kern(*refs):
        acc = None
        for i in range(n):
            p = _bdot(refs[i][...], refs[n + i][...], NT if trans_b else NN)
            acc = p if acc is None else acc + p
        refs[2 * n][...] = acc.astype(out_dtype)

    in_specs = []
    for a, acb, b, bcb, k in pairs:
        in_specs.append(pl.BlockSpec((tm, k), functools.partial(lambda i, j, c: (i, c), c=acb)))
    for a, acb, b, bcb, k in pairs:
        if trans_b:
            in_specs.append(pl.BlockSpec((tn, k), functools.partial(lambda i, j, c: (j, c), c=bcb)))
        else:
            in_specs.append(pl.BlockSpec((k, tn), functools.partial(lambda i, j, c: (c, j), c=bcb)))
    return pl.pallas_call(
        kern, name=name, grid=(m // tm, n_out // tn), in_specs=in_specs,
        out_specs=pl.BlockSpec((tm, tn), lambda i, j: (i, j)),
        out_shape=jax.ShapeDtypeStruct((m, n_out), out_dtype),
        compiler_params=_cp(("parallel", "arbitrary")),
    )(*[p[0] for p in pairs], *[p[2] for p in pairs])


def _mm_tn(a, a_cb, ka, g, g_cb, ng, name, tm=1024):
    m = a.shape[0]
    tm = min(tm, m)
    tk = _pick(ka, (1024, 1408, 512, 256, 128))
    tn = _pick(ng, WIDE_TILES)
    nm = m // tm

    def kern(a_ref, g_ref, o_ref):
        p = _bdot(a_ref[...], g_ref[...], TN)

        @pl.when(pl.program_id(2) == 0)
        def _():
            o_ref[...] = p

        @pl.when(pl.program_id(2) > 0)
        def _():
            o_ref[...] += p

    ak, gk = ka // tk, ng // tn
    return pl.pallas_call(
        kern, name=name, grid=(ak, gk, nm),
        in_specs=[pl.BlockSpec((tm, tk), lambda i, j, r: (r, a_cb * ak + i)),
                  pl.BlockSpec((tm, tn), lambda i, j, r: (r, g_cb * gk + j))],
        out_specs=pl.BlockSpec((tk, tn), lambda i, j, r: (i, j)),
        out_shape=jax.ShapeDtypeStruct((ka, ng), F32),
        compiler_params=_cp(("parallel", "parallel", "arbitrary")),
    )(a, g)


def _ln_fwd(x, m, g, b, name, tm=256):
    s, d = x.shape
    tm = min(tm, s)

    def kern(x_ref, m_ref, g_ref, b_ref, y_ref, yb_ref, r_ref):
        r = ALPHA * x_ref[...] + m_ref[...]
        mu = jnp.mean(r, -1, keepdims=True)
        xc = r - mu
        var = jnp.mean(xc * xc, -1, keepdims=True)
        y = xc * lax.rsqrt(var + LN_EPS) * g_ref[...] + b_ref[...]
        y_ref[...] = y
        yb_ref[...] = y.astype(BF16)
        r_ref[...] = r

    row = pl.BlockSpec((tm, d), lambda i: (i, 0))
    vec = pl.BlockSpec((1, d), lambda i: (0, 0))
    return pl.pallas_call(
        kern, name=name, grid=(s // tm,), in_specs=[row, row, vec, vec], out_specs=[row, row, row],
        out_shape=[jax.ShapeDtypeStruct((s, d), F32), jax.ShapeDtypeStruct((s, d), BF16), jax.ShapeDtypeStruct((s, d), F32)],
        compiler_params=_cp(("parallel",)),
    )(x, m, g.reshape(1, d), b.reshape(1, d))


def _ln_bwd(dya, dyb, ca, cb, r, g, name, tm=256):
    s, d = r.shape
    tm = min(tm, s)

    def kern(a_ref, b_ref, r_ref, g_ref, dr_ref, dg_ref, db_ref):
        dy = ca * a_ref[...] + cb * b_ref[...]
        rr = r_ref[...]
        mu = jnp.mean(rr, -1, keepdims=True)
        xc = rr - mu
        var = jnp.mean(xc * xc, -1, keepdims=True)
        rstd = lax.rsqrt(var + LN_EPS)
        xhat = xc * rstd
        dxh = dy * g_ref[...]
        dr_ref[...] = rstd * (dxh - jnp.mean(dxh, -1, keepdims=True) - xhat * jnp.mean(dxh * xhat, -1, keepdims=True))
        dg = jnp.sum(dy * xhat, 0, keepdims=True)
        db = jnp.sum(dy, 0, keepdims=True)

        @pl.when(pl.program_id(0) == 0)
        def _():
            dg_ref[...] = dg
            db_ref[...] = db

        @pl.when(pl.program_id(0) > 0)
        def _():
            dg_ref[...] += dg
            db_ref[...] += db

    row = pl.BlockSpec((tm, d), lambda i: (i, 0))
    vec = pl.BlockSpec((1, d), lambda i: (0, 0))
    dr, dg, db = pl.pallas_call(
        kern, name=name, grid=(s // tm,), in_specs=[row, row, row, vec], out_specs=[row, vec, vec],
        out_shape=[jax.ShapeDtypeStruct((s, d), F32), jax.ShapeDtypeStruct((1, d), F32), jax.ShapeDtypeStruct((1, d), F32)],
        compiler_params=_cp(("arbitrary",)),
    )(dya, dyb, r, g.reshape(1, d))
    return dr, dg[0], db[0]


def _sqerr(y, t, name, tm=256):
    s, d = y.shape
    tm = min(tm, s)

    def kern(y_ref, t_ref, o_ref):
        e = y_ref[...] - t_ref[...]
        p = jnp.sum(e * e, 0, keepdims=True)

        @pl.when(pl.program_id(0) == 0)
        def _():
            o_ref[...] = p

        @pl.when(pl.program_id(0) > 0)
        def _():
            o_ref[...] += p

    row = pl.BlockSpec((tm, d), lambda i: (i, 0))
    return pl.pallas_call(
        kern, name=name, grid=(s // tm,), in_specs=[row, row], out_specs=pl.BlockSpec((1, d), lambda i: (0, 0)),
        out_shape=jax.ShapeDtypeStruct((1, d), F32), compiler_params=_cp(("arbitrary",)),
    )(y, t)


def _axpby(a, b, ca, cb, name, tm=256):
    s, d = a.shape
    tm = min(tm, s)

    def kern(a_ref, b_ref, o_ref):
        o_ref[...] = ca * a_ref[...] + cb * b_ref[...]

    row = pl.BlockSpec((tm, d), lambda i: (i, 0))
    return pl.pallas_call(
        kern, name=name, grid=(s // tm,), in_specs=[row, row], out_specs=row,
        out_shape=jax.ShapeDtypeStruct((s, d), F32), compiler_params=_cp(("parallel",)),
    )(a, b)


def _shift_down(xx, s, n):
    if s == 0:
        return xx[HALO:HALO + n]
    return pltpu.roll(xx, s, axis=0)[HALO:HALO + n]


def _shift_up(yy, s, n):
    if s == 0:
        return yy[0:n]
    return pltpu.roll(yy, yy.shape[0] - s, axis=0)[0:n]


def _prev_halo(tm):
    return lambda i: jnp.maximum(i * (tm // HALO) - 1, 0)


def _dwconv_fwd(x, x_cb0, c, w8, b, k, name, tm=512):
    s = x.shape[0]
    tm = min(tm, s)
    ph = _prev_halo(tm)

    def kern(x_ref, h_ref, w_ref, b_ref, y_ref):
        halo = jnp.where(pl.program_id(0) == 0, 0.0, h_ref[...].astype(F32))
        xx = jnp.concatenate([halo, x_ref[...].astype(F32)], axis=0)
        w = w_ref[...]
        acc = b_ref[...] + w[k - 1:k] * xx[HALO:]
        for j in range(k - 1):
            acc = acc + w[j:j + 1] * _shift_down(xx, k - 1 - j, tm)
        y_ref[...] = acc

    return pl.pallas_call(
        kern, name=name, grid=(s // tm, c // CONV_CB),
        in_specs=[pl.BlockSpec((tm, CONV_CB), lambda i, j: (i, x_cb0 + j)),
                  pl.BlockSpec((HALO, CONV_CB), lambda i, j: (ph(i), x_cb0 + j)),
                  pl.BlockSpec((SUBLANES, CONV_CB), lambda i, j: (0, j)),
                  pl.BlockSpec((1, CONV_CB), lambda i, j: (0, j))],
        out_specs=pl.BlockSpec((tm, CONV_CB), lambda i, j: (i, j)),
        out_shape=jax.ShapeDtypeStruct((s, c), F32), compiler_params=_cp(("parallel", "parallel")),
    )(x, x, w8, b.reshape(1, c))


def _dwconv_bwd(dy, x, x_cb0, c, w8, k, name, tm=512):
    s = x.shape[0]
    tm = min(tm, s)
    nt = s // tm
    ph = _prev_halo(tm)
    nh = lambda i: jnp.minimum((i + 1) * (tm // HALO), s // HALO - 1)

    def kern(dy_ref, dyn_ref, x_ref, h_ref, w_ref, dx_ref, dw_ref, db_ref):
        i = pl.program_id(1)
        halo = jnp.where(i == 0, 0.0, h_ref[...].astype(F32))
        xx = jnp.concatenate([halo, x_ref[...].astype(F32)], axis=0)
        dyt = dy_ref[...]
        nxt = jnp.where(i == nt - 1, 0.0, dyn_ref[...])
        dyy = jnp.concatenate([dyt, nxt], axis=0)
        w = w_ref[...]
        dx = w[k - 1:k] * dyt
        rows = [jnp.sum(dyt * _shift_down(xx, k - 1 - j, tm), 0, keepdims=True) for j in range(k)]
        for j in range(k - 1):
            dx = dx + w[j:j + 1] * _shift_up(dyy, k - 1 - j, tm)
        dx_ref[...] = dx.astype(BF16)
        dw = jnp.concatenate(rows + [jnp.zeros((SUBLANES - k, CONV_CB), F32)], axis=0)
        db = jnp.sum(dyt, 0, keepdims=True)

        @pl.when(i == 0)
        def _():
            dw_ref[...] = dw
            db_ref[...] = db

        @pl.when(i > 0)
        def _():
            dw_ref[...] += dw
            db_ref[...] += db

    dx, dw, db = pl.pallas_call(
        kern, name=name, grid=(c // CONV_CB, nt),
        in_specs=[pl.BlockSpec((tm, CONV_CB), lambda j, i: (i, j)),
                  pl.BlockSpec((HALO, CONV_CB), lambda j, i: (nh(i), j)),
                  pl.BlockSpec((tm, CONV_CB), lambda j, i: (i, x_cb0 + j)),
                  pl.BlockSpec((HALO, CONV_CB), lambda j, i: (ph(i), x_cb0 + j)),
                  pl.BlockSpec((SUBLANES, CONV_CB), lambda j, i: (0, j))],
        out_specs=[pl.BlockSpec((tm, CONV_CB), lambda j, i: (i, j)),
                   pl.BlockSpec((SUBLANES, CONV_CB), lambda j, i: (0, j)),
                   pl.BlockSpec((1, CONV_CB), lambda j, i: (0, j))],
        out_shape=[jax.ShapeDtypeStruct((s, c), BF16), jax.ShapeDtypeStruct((SUBLANES, c), F32),
                   jax.ShapeDtypeStruct((1, c), F32)],
        compiler_params=_cp(("parallel", "arbitrary")),
    )(dy, dy, x, x, w8)
    return dx, dw, db[0]


def _silu(x):
    return x * jax.nn.sigmoid(x)


def _ffn_gate_fwd(hu, w8, b, name, tm=512):
    s = hu.shape[0]
    tm = min(tm, s)
    nb = FFN_DIM // CONV_CB
    ph = _prev_halo(tm)
    k = 3

    def conv(x_ref, h_ref, w_ref, b_ref):
        halo = jnp.where(pl.program_id(0) == 0, 0.0, h_ref[...].astype(F32))
        xx = jnp.concatenate([halo, x_ref[...].astype(F32)], axis=0)
        w = w_ref[...]
        acc = b_ref[...] + w[k - 1:k] * xx[HALO:]
        for j in range(k - 1):
            acc = acc + w[j:j + 1] * _shift_down(xx, k - 1 - j, tm)
        return acc

    def kern(g_ref, gh_ref, u_ref, uh_ref, wg_ref, wu_ref, bg_ref, bu_ref, a_ref):
        g = conv(g_ref, gh_ref, wg_ref, bg_ref)
        u = conv(u_ref, uh_ref, wu_ref, bu_ref)
        a_ref[...] = (_silu(g) * u).astype(BF16)

    main = lambda off: pl.BlockSpec((tm, CONV_CB), lambda i, j: (i, off + j))
    halo = lambda off: pl.BlockSpec((HALO, CONV_CB), lambda i, j: (ph(i), off + j))
    wsp = lambda off: pl.BlockSpec((SUBLANES, CONV_CB), lambda i, j: (0, off + j))
    bsp = lambda off: pl.BlockSpec((1, CONV_CB), lambda i, j: (0, off + j))
    b2 = b.reshape(1, 2 * FFN_DIM)
    return pl.pallas_call(
        kern, name=name, grid=(s // tm, nb),
        in_specs=[main(0), halo(0), main(nb), halo(nb), wsp(0), wsp(nb), bsp(0), bsp(nb)],
        out_specs=pl.BlockSpec((tm, CONV_CB), lambda i, j: (i, j)),
        out_shape=jax.ShapeDtypeStruct((s, FFN_DIM), BF16), compiler_params=_cp(("parallel", "parallel")),
    )(hu, hu, hu, hu, w8, w8, b2, b2)


def _ffn_gate_bwd(hu, da, w8, b, name, tm=512):
    s = hu.shape[0]
    tm = min(tm, s)
    nt = s // tm
    nb = FFN_DIM // CONV_CB
    ph = _prev_halo(tm)
    nh = lambda i: jnp.minimum((i + 1) * (tm // HALO), s // HALO - 1)
    k = 3
    ne = tm + HALO

    def kern(g_ref, gp_ref, gn_ref, u_ref, up_ref, un_ref, da_ref, dan_ref, wg_ref, wu_ref, bg_ref, bu_ref,
             dg_ref, du_ref, dwg_ref, dwu_ref, dbg_ref, dbu_ref):
        i = pl.program_id(1)

        def ext(x_ref, p_ref, n_ref):
            halo = jnp.where(i == 0, 0.0, p_ref[...].astype(F32))
            return jnp.concatenate([halo, x_ref[...].astype(F32), n_ref[...].astype(F32)], axis=0)

        def conv(xx, w, bb):
            acc = bb + w[k - 1:k] * xx[HALO:]
            for j in range(k - 1):
                acc = acc + w[j:j + 1] * _shift_down(xx, k - 1 - j, ne)
            return acc

        gx, ux = ext(g_ref, gp_ref, gn_ref), ext(u_ref, up_ref, un_ref)
        wg, wu = wg_ref[...], wu_ref[...]
        g = conv(gx, wg, bg_ref[...])
        u = conv(ux, wu, bu_ref[...])
        dae = jnp.concatenate([da_ref[...].astype(F32), jnp.where(i == nt - 1, 0.0, dan_ref[...].astype(F32))], axis=0)
        sg = jax.nn.sigmoid(g)
        dhg = dae * u * (sg * (1.0 + g * (1.0 - sg)))
        dhu = dae * (g * sg)

        def back(dh, xx, w):
            dx = w[k - 1:k] * dh[0:tm]
            for j in range(k - 1):
                dx = dx + w[j:j + 1] * _shift_up(dh, k - 1 - j, tm)
            rows = [jnp.sum(dh[0:tm] * _shift_down(xx, k - 1 - j, tm), 0, keepdims=True) for j in range(k)]
            dw = jnp.concatenate(rows + [jnp.zeros((SUBLANES - k, CONV_CB), F32)], axis=0)
            return dx, dw, jnp.sum(dh[0:tm], 0, keepdims=True)

        dxg, dwg, dbg = back(dhg, gx, wg)
        dxu, dwu, dbu = back(dhu, ux, wu)
        dg_ref[...] = dxg.astype(BF16)
        du_ref[...] = dxu.astype(BF16)

        @pl.when(i == 0)
        def _():
            dwg_ref[...] = dwg
            dwu_ref[...] = dwu
            dbg_ref[...] = dbg
            dbu_ref[...] = dbu

        @pl.when(i > 0)
        def _():
            dwg_ref[...] += dwg
            dwu_ref[...] += dwu
            dbg_ref[...] += dbg
            dbu_ref[...] += dbu

    main = lambda off: pl.BlockSpec((tm, CONV_CB), lambda j, i: (i, off + j))
    prev = lambda off: pl.BlockSpec((HALO, CONV_CB), lambda j, i: (ph(i), off + j))
    nxt = lambda off: pl.BlockSpec((HALO, CONV_CB), lambda j, i: (nh(i), off + j))
    wsp = lambda off: pl.BlockSpec((SUBLANES, CONV_CB), lambda j, i: (0, off + j))
    bsp = lambda off: pl.BlockSpec((1, CONV_CB), lambda j, i: (0, off + j))
    b2 = b.reshape(1, 2 * FFN_DIM)
    outs = pl.pallas_call(
        kern, name=name, grid=(nb, nt),
        in_specs=[main(0), prev(0), nxt(0), main(nb), prev(nb), nxt(nb), main(0), nxt(0), wsp(0), wsp(nb), bsp(0), bsp(nb)],
        out_specs=[main(0), main(0), wsp(0), wsp(0), bsp(0), bsp(0)],
        out_shape=[jax.ShapeDtypeStruct((s, FFN_DIM), BF16), jax.ShapeDtypeStruct((s, FFN_DIM), BF16),
                   jax.ShapeDtypeStruct((SUBLANES, FFN_DIM), F32), jax.ShapeDtypeStruct((SUBLANES, FFN_DIM), F32),
                   jax.ShapeDtypeStruct((1, FFN_DIM), F32), jax.ShapeDtypeStruct((1, FFN_DIM), F32)],
        compiler_params=_cp(("parallel", "arbitrary")),
    )(hu, hu, hu, hu, hu, hu, da, da, w8, w8, b2, b2)
    dg, du, dwg, dwu, dbg, dbu = outs
    return dg, du, jnp.concatenate([dwg, dwu], axis=1), jnp.concatenate([dbg[0], dbu[0]])


def _iota(shape, dim):
    return lax.broadcasted_iota(jnp.int32, shape, dim)


def _in_range(idx, start, size):
    return jnp.logical_and(idx >= start, idx < start + size)


@functools.partial(jax.custom_vjp, nondiff_argnums=(2, 3))
def _zdot(a, b, dims, z):
    zz, x = ((a, b) if z == 0 else (b, a))
    zz = zz.astype(BF16)
    xh = x.astype(BF16)
    r1 = x - xh.astype(F32)
    xm = r1.astype(BF16)
    xl = (r1 - xm.astype(F32)).astype(BF16)
    f = (lambda p: _dot(zz, p, dims)) if z == 0 else (lambda p: _dot(p, zz, dims))
    return f(xh) + f(xm) + f(xl)


def _zdot_fwd(a, b, dims, z):
    return _zdot(a, b, dims, z), (a if z == 0 else b)


def _zdot_bwd(dims, z, zz, ct):
    zb = zz.astype(BF16)
    ch = ct.astype(BF16)
    cm = (ct - ch.astype(F32)).astype(BF16)
    if dims == NN and z == 0:
        f = lambda c: _dot(zb, c, TN)
    elif dims == NN and z == 1:
        f = lambda c: _dot(c, zb, NT)
    else:
        f = lambda c: _dot(c, zb, TN)
    dx = f(ch) + f(cm)
    return (jnp.zeros_like(zz), dx) if z == 0 else (dx, jnp.zeros_like(zz))


_zdot.defvjp(_zdot_fwd, _zdot_bwd)


def _expand8(v, e):
    return jnp.sum(_zdot(jnp.broadcast_to(v, (SUBLANES, v.shape[1])), e, NN, 1), 0, keepdims=True) * (1.0 / SUBLANES)


def _ssd_consts():
    tri = np.tril(np.ones((SSD_CHUNK, SSD_CHUNK), np.float32))
    e = np.zeros((LANES, SSD_D_INNER), np.float32)
    for h in range(SSD_HEADS):
        e[h, 64 * h:64 * h + 64] = 1.0
    bd = np.zeros((LANES, SSD_D_INNER), np.float32)
    bd[:64, :256] = 1.0
    bd[64:, 256:] = 1.0
    return jnp.asarray(tri), jnp.asarray(e), jnp.asarray(bd)


def _ssd_chunk(tri, e, bd, z, xs_p, bm_p, cm_p, dtr, st, dtb, alog, dsk, nw):
    n = z.shape[0]
    lane128, lane512 = _iota((1, LANES), 1), _iota((1, SSD_D_INNER), 1)
    sub8 = _iota((SUBLANES, 1), 0)
    tril = _iota((n, 1), 0) >= _iota((1, n), 1)
    head_rows = jnp.where(sub8 == lane128, 1.0, 0.0)
    xs, bm, cm = _silu(xs_p), _silu(bm_p), _silu(cm_p)
    dt = jax.nn.softplus(dtr + dtb)
    da = dt * (-jnp.exp(alog))
    acs = _zdot(tri, da, NN, 0)
    acst = _zdot(head_rows, acs, NT, 0)
    xc = xs * _zdot(dt, e, NN, 1)
    tot = jnp.sum(da, 0, keepdims=True)
    y = _bdot(cm, st) * _zdot(jnp.exp(acs), e, NN, 1)
    st_new = st * _expand8(jnp.exp(tot), e) + bd * _bdot(bm, xc * _zdot(jnp.exp(tot - acs), e, NN, 1), TN)
    for g in range(2):
        cb = _bdot(jnp.where(_in_range(lane128, 64 * g, 64), cm, 0.0), bm, NT)
        for h in range(4 * g, 4 * g + 4):
            col = jnp.sum(jnp.where(lane128 == h, acs, 0.0), 1, keepdims=True)
            row = jnp.sum(jnp.where(sub8 == h, acst, 0.0), 0, keepdims=True)
            dm = jnp.where(tril, jnp.exp(jnp.minimum(col - row, 0.0)), 0.0)
            y = y + _bdot(cb * dm, jnp.where(_in_range(lane512, 64 * h, 64), xc, 0.0))
    y = (y + _expand8(dsk, e) * xs) * _silu(z)
    ysq = y * y
    g0 = lane512 < 256
    ms0 = jnp.sum(jnp.where(g0, ysq, 0.0), 1, keepdims=True) * (1.0 / 256)
    ms1 = jnp.sum(jnp.where(g0, 0.0, ysq), 1, keepdims=True) * (1.0 / 256)
    rs = jnp.where(g0, lax.rsqrt(ms0 + RMS_EPS), lax.rsqrt(ms1 + RMS_EPS))
    return y * rs * nw, st_new


def _ssd_specs(rev, nc):
    ci = (lambda i: nc - 1 - i) if rev else (lambda i: i)
    n = SSD_CHUNK
    act = [pl.BlockSpec((n, 512), lambda i: (ci(i), AB_Z // 512)),
           pl.BlockSpec((n, 512), lambda i: (ci(i), 0)),
           pl.BlockSpec((n, LANES), lambda i: (ci(i), 4)),
           pl.BlockSpec((n, LANES), lambda i: (ci(i), 5)),
           pl.BlockSpec((n, LANES), lambda i: (ci(i), AB_DT // LANES))]
    const = [pl.BlockSpec((n, n), lambda i: (0, 0)), pl.BlockSpec((LANES, 512), lambda i: (0, 0)),
             pl.BlockSpec((LANES, 512), lambda i: (0, 0))]
    par = [pl.BlockSpec((1, LANES), lambda i: (0, 0))] * 3 + [pl.BlockSpec((1, 512), lambda i: (0, 0))]
    return ci, act, const, par


def _ssd_fwd(h, xbcc, dtb, alog, dsk, nw, name):
    s = h.shape[0]
    nc = s // SSD_CHUNK
    ci, act, const, par = _ssd_specs(False, nc)

    def kern(z_ref, xs_ref, bm_ref, cm_ref, dt_ref, tri_ref, e_ref, bd_ref, dtb_ref, al_ref, dsk_ref, nw_ref,
             y_ref, sts_ref, st):
        @pl.when(pl.program_id(0) == 0)
        def _():
            st[...] = jnp.zeros_like(st)

        sts_ref[0] = st[...]
        y, stn = _ssd_chunk(tri_ref[...], e_ref[...], bd_ref[...], z_ref[...], xs_ref[...], bm_ref[...], cm_ref[...],
                            dt_ref[...], st[...], dtb_ref[...], al_ref[...], dsk_ref[...], nw_ref[...])
        y_ref[...] = y.astype(BF16)
        st[...] = stn

    return pl.pallas_call(
        kern, name=name, grid=(nc,), in_specs=act + const + par,
        out_specs=[pl.BlockSpec((SSD_CHUNK, 512), lambda i: (i, 0)), pl.BlockSpec((1, LANES, 512), lambda i: (i, 0, 0))],
        out_shape=[jax.ShapeDtypeStruct((s, 512), BF16), jax.ShapeDtypeStruct((nc, LANES, 512), F32)],
        scratch_shapes=[pltpu.VMEM((LANES, 512), F32)], compiler_params=_cp(("arbitrary",)),
    )(h, xbcc, xbcc, xbcc, h, *_ssd_consts(), dtb, alog, dsk, nw)


def _acc(ref, val, first):
    @pl.when(first)
    def _():
        ref[...] = val

    @pl.when(jnp.logical_not(first))
    def _():
        ref[...] += val


def _ssd_bwd(dy, dy_cb, h, xbcc, sts, dtb, alog, dsk, nw, name):
    s = h.shape[0]
    nc = s // SSD_CHUNK
    n = SSD_CHUNK
    ci, act, const, par = _ssd_specs(True, nc)

    def kern(dy_ref, sts_ref, z_ref, xs_ref, bm_ref, cm_ref, dt_ref, tri_ref, e_ref, bd_ref, dtb_ref, al_ref, dsk_ref,
             nw_ref, dz_ref, dx_ref, ddt_ref, ddtb_ref, dal_ref, ddsk_ref, dnw_ref, dst):
        first = pl.program_id(0) == 0

        @pl.when(first)
        def _():
            dst[...] = jnp.zeros_like(dst)

        f = functools.partial(_ssd_chunk, tri_ref[...], e_ref[...], bd_ref[...])
        _, pull = jax.vjp(f, z_ref[...], xs_ref[...], bm_ref[...], cm_ref[...], dt_ref[...], sts_ref[0],
                          dtb_ref[...], al_ref[...], dsk_ref[...], nw_ref[...])
        dz, dxs, dbm, dcm, ddt, dsti, ddtb, dal, ddsk, dnw = pull((dy_ref[...], dst[...]))
        dz_ref[...] = dz.astype(BF16)
        dx_ref[:, 0:512] = dxs
        dx_ref[:, 512:640] = dbm
        dx_ref[:, 640:768] = dcm
        ddt_ref[...] = ddt.astype(BF16)
        dst[...] = dsti
        _acc(ddtb_ref, ddtb, first)
        _acc(dal_ref, dal, first)
        _acc(ddsk_ref, ddsk, first)
        _acc(dnw_ref, dnw, first)

    vec = pl.BlockSpec((1, LANES), lambda i: (0, 0))
    return pl.pallas_call(
        kern, name=name, grid=(nc,),
        in_specs=[pl.BlockSpec((n, 512), lambda i: (ci(i), dy_cb)), pl.BlockSpec((1, LANES, 512), lambda i: (ci(i), 0, 0))]
        + act + const + par,
        out_specs=[pl.BlockSpec((n, 512), lambda i: (ci(i), 0)), pl.BlockSpec((n, SSD_CONV_DIM), lambda i: (ci(i), 0)),
                   pl.BlockSpec((n, LANES), lambda i: (ci(i), 0)), vec, vec, vec, pl.BlockSpec((1, 512), lambda i: (0, 0))],
        out_shape=[jax.ShapeDtypeStruct((s, 512), BF16), jax.ShapeDtypeStruct((s, SSD_CONV_DIM), F32),
                   jax.ShapeDtypeStruct((s, LANES), BF16), jax.ShapeDtypeStruct((1, LANES), F32),
                   jax.ShapeDtypeStruct((1, LANES), F32), jax.ShapeDtypeStruct((1, LANES), F32),
                   jax.ShapeDtypeStruct((1, 512), F32)],
        scratch_shapes=[pltpu.VMEM((LANES, 512), F32)], compiler_params=_cp(("arbitrary",)),
    )(dy, sts, h, xbcc, xbcc, xbcc, h, *_ssd_consts(), dtb, alog, dsk, nw)


HG_LEVELS = 6


HG_SUMS = 1 + 2 * HG_LEVELS


def _hg_consts():
    n = HG_CHUNK
    t = np.arange(n)
    mats = [np.tril(np.ones((n, n), np.float32))]
    dq, dk, mk = [], [], []
    for lv in range(HG_LEVELS):
        b = 1 << lv
        start = (t // b) * b
        end = start + b - 1
        dq.append(((t[None, :] >= start[:, None]) & (t[None, :] <= t[:, None])).astype(np.float32))
        dk.append(((t[None, :] > t[:, None]) & (t[None, :] <= end[:, None])).astype(np.float32))
        mk.append((((t // b) % 2 == 1)[:, None] & ((t // b)[None, :] == (t // b)[:, None] - 1)).astype(np.float32))
    c1 = np.concatenate(mats + dq + dk, axis=0)
    return (jnp.asarray(c1, BF16), jnp.asarray(np.concatenate([c1, c1, c1], axis=1), BF16), jnp.asarray(np.stack(mk)))


@functools.partial(jax.custom_vjp, nondiff_argnums=(1, 2))
def _split_grid(x, nr, nc):
    br, bc = x.shape[0] // nr, x.shape[1] // nc
    return tuple(x[i * br:(i + 1) * br, j * bc:(j + 1) * bc] for i in range(nr) for j in range(nc))


def _split_grid_fwd(x, nr, nc):
    return _split_grid(x, nr, nc), None


def _split_grid_bwd(nr, nc, _, cts):
    rows = [cts[i * nc] if nc == 1 else jnp.concatenate(cts[i * nc:(i + 1) * nc], axis=1) for i in range(nr)]
    return (rows[0] if nr == 1 else jnp.concatenate(rows, axis=0),)


_split_grid.defvjp(_split_grid_fwd, _split_grid_bwd)


@jax.custom_vjp
def _level_sums(c1, c3, g):
    gh = g.astype(BF16)
    r1 = g - gh.astype(F32)
    gm = r1.astype(BF16)
    gl = (r1 - gm.astype(F32)).astype(BF16)
    return _dot(c3, jnp.concatenate([gh, gm, gl], axis=0), NN)


def _level_sums_fwd(c1, c3, g):
    return _level_sums(c1, c3, g), (c1, c3)


def _level_sums_bwd(res, ct):
    c1, c3 = res
    ch = ct.astype(BF16)
    cm = (ct - ch.astype(F32)).astype(BF16)
    return jnp.zeros_like(c1), jnp.zeros_like(c3), _dot(c1, ch, TN) + _dot(c1, cm, TN)


_level_sums.defvjp(_level_sums_fwd, _level_sums_bwd)


def _hg_chunk(c1, c3, masks, hq, hf, hi, hg, sts, lb, nw):
    q = _silu(hq)
    g = jnp.log(lb + (1.0 - lb) * jax.nn.sigmoid(hf))
    k = (1.0 - lb) * jax.nn.sigmoid(-hf)
    parts = _split_grid(_level_sums(c1, c3, g), HG_SUMS, 1)
    bc = parts[0]
    tot = jnp.sum(g, 0, keepdims=True)
    qe = _split_grid(q * jnp.exp(bc), 1, HG_HEADS)
    kd = _split_grid(k * jnp.exp(tot - bc), 1, HG_HEADS)
    etot = _split_grid(jnp.exp(tot), 1, HG_HEADS)
    qh, kh, vh, gh = [_split_grid(a, 1, HG_HEADS) for a in (q, k, hi, hg)]
    att = [None] * HG_HEADS
    for lv in range(HG_LEVELS):
        qt = _split_grid(q * jnp.exp(parts[1 + lv]), 1, HG_HEADS)
        kt = _split_grid(k * jnp.exp(parts[1 + HG_LEVELS + lv]), 1, HG_HEADS)
        for hd in range(HG_HEADS):
            t = masks[lv] * _bdot(qt[hd], kt[hd], NT)
            att[hd] = t if att[hd] is None else att[hd] + t
    outs, stn = [], []
    for hd in range(HG_HEADS):
        o = (_bdot(att[hd], vh[hd]) + jnp.sum(qh[hd] * kh[hd], 1, keepdims=True) * vh[hd] + _bdot(qe[hd], sts[hd], NT))
        stn.append(sts[hd] * etot[hd] + _bdot(vh[hd], kd[hd], TN))
        outs.append(o * lax.rsqrt(jnp.mean(o * o, 1, keepdims=True) + RMS_EPS) * nw * _silu(gh[hd]))
    return tuple(outs), tuple(stn)


def _hg_specs(rev, nc):
    ci = (lambda i: nc - 1 - i) if rev else (lambda i: i)
    n = HG_CHUNK
    act = [pl.BlockSpec((n, 512), functools.partial(lambda i, c: (ci(i), c), c=c))
           for c in (AB_HQ // 512, AB_HF // 512, AB_HI // 512, AB_HG // 512)]
    other = [pl.BlockSpec((HG_SUMS * n, n), lambda i: (0, 0)), pl.BlockSpec((HG_SUMS * n, 3 * n), lambda i: (0, 0)),
             pl.BlockSpec((HG_LEVELS, n, n), lambda i: (0, 0, 0)), pl.BlockSpec((1, 512), lambda i: (0, 0)),
             pl.BlockSpec((1, LANES), lambda i: (0, 0))]
    return ci, act, other


def _head_rows(hd):
    return slice(LANES * hd, LANES * (hd + 1))


def _hg_fwd(h, lb, nw, name):
    s = h.shape[0]
    nc = s // HG_CHUNK
    ci, act, other = _hg_specs(False, nc)

    def kern(q_ref, f_ref, i_ref, g_ref, c1_ref, c3_ref, mk_ref, lb_ref, nw_ref, y_ref, sts_ref, st):
        @pl.when(pl.program_id(0) == 0)
        def _():
            st[...] = jnp.zeros_like(st)

        sts_ref[0] = st[...]
        masks = [mk_ref[lv] for lv in range(HG_LEVELS)]
        outs, stn = _hg_chunk(c1_ref[...], c3_ref[...], masks, q_ref[...], f_ref[...], i_ref[...], g_ref[...],
                              tuple(st[_head_rows(hd), :] for hd in range(HG_HEADS)), lb_ref[...], nw_ref[...])
        for hd in range(HG_HEADS):
            y_ref[:, _head_rows(hd)] = outs[hd].astype(BF16)
            st[_head_rows(hd), :] = stn[hd]

    return pl.pallas_call(
        kern, name=name, grid=(nc,), in_specs=act + other,
        out_specs=[pl.BlockSpec((HG_CHUNK, 512), lambda i: (i, 0)), pl.BlockSpec((1, 512, LANES), lambda i: (i, 0, 0))],
        out_shape=[jax.ShapeDtypeStruct((s, 512), BF16), jax.ShapeDtypeStruct((nc, 512, LANES), F32)],
        scratch_shapes=[pltpu.VMEM((512, LANES), F32)], compiler_params=_cp(("arbitrary",)),
    )(h, h, h, h, *_hg_consts(), lb, nw)


def _hg_bwd(dy, dy_cb, h, sts, lb, nw, name):
    s = h.shape[0]
    nc = s // HG_CHUNK
    n = HG_CHUNK
    ci, act, other = _hg_specs(True, nc)

    def kern(dy_ref, sts_ref, q_ref, f_ref, i_ref, g_ref, c1_ref, c3_ref, mk_ref, lb_ref, nw_ref,
             dq_ref, df_ref, di_ref, dg_ref, dlb_ref, dnw_ref, dst):
        first = pl.program_id(0) == 0

        @pl.when(first)
        def _():
            dst[...] = jnp.zeros_like(dst)

        masks = [mk_ref[lv] for lv in range(HG_LEVELS)]
        heads = range(HG_HEADS)
        _, pull = jax.vjp(functools.partial(_hg_chunk, c1_ref[...], c3_ref[...], masks), q_ref[...], f_ref[...], i_ref[...],
                          g_ref[...], tuple(sts_ref[0, _head_rows(hd), :] for hd in heads), lb_ref[...], nw_ref[...])
        dq, df, di, dg, dsti, dlb, dnw = pull((tuple(dy_ref[:, _head_rows(hd)] for hd in heads),
                                               tuple(dst[_head_rows(hd), :] for hd in heads)))
        dq_ref[...] = dq.astype(BF16)
        df_ref[...] = df.astype(BF16)
        di_ref[...] = di.astype(BF16)
        dg_ref[...] = dg.astype(BF16)
        for hd in heads:
            dst[_head_rows(hd), :] = dsti[hd]
        _acc(dlb_ref, dlb, first)
        _acc(dnw_ref, dnw, first)

    blk = pl.BlockSpec((n, 512), lambda i: (ci(i), 0))
    return pl.pallas_call(
        kern, name=name, grid=(nc,),
        in_specs=[pl.BlockSpec((n, 512), lambda i: (ci(i), dy_cb)), pl.BlockSpec((1, 512, LANES), lambda i: (ci(i), 0, 0))]
        + act + other,
        out_specs=[blk, blk, blk, blk, pl.BlockSpec((1, 512), lambda i: (0, 0)), pl.BlockSpec((1, LANES), lambda i: (0, 0))],
        out_shape=[jax.ShapeDtypeStruct((s, 512), BF16)] * 4 + [jax.ShapeDtypeStruct((1, 512), F32),
                                                                jax.ShapeDtypeStruct((1, LANES), F32)],
        scratch_shapes=[pltpu.VMEM((512, LANES), F32)], compiler_params=_cp(("arbitrary",)),
    )(dy, sts, h, h, h, h, *_hg_consts(), lb, nw)


def _swa_pair(pair, notfirst, qp, k0, v0, k1, v1, srow):
    n = qp.shape[0]
    rows, cols = _iota((n, 1), 0), _iota((1, n), 1)
    own = rows >= cols
    prev = jnp.logical_and(cols > rows, notfirst)
    lane = _iota((1, LANES), 1)
    out = None
    for e in range(2):
        lm = _in_range(lane, 64 * e, 64)
        qm = jnp.where(lm, qp, 0.0)
        s0 = jnp.where(prev, _bdot(qm, k0, NT) * 0.125, MASK_VALUE)
        s1 = jnp.where(own, _bdot(qm, k1, NT) * 0.125, MASK_VALUE)
        sk = jnp.sum(jnp.where(lane == pair + 4 * e, srow, 0.0), 1, keepdims=True)
        m = lax.stop_gradient(jnp.maximum(jnp.maximum(jnp.max(s0, 1, keepdims=True), jnp.max(s1, 1, keepdims=True)), sk))
        p0, p1 = jnp.exp(s0 - m), jnp.exp(s1 - m)
        inv = 1.0 / (jnp.sum(p0, 1, keepdims=True) + jnp.sum(p1, 1, keepdims=True) + jnp.exp(sk - m))
        t = _bdot(p0 * inv, jnp.where(lm, v0, 0.0)) + _bdot(p1 * inv, jnp.where(lm, v1, 0.0))
        out = t if out is None else out + t
    return out


def _swa_specs():
    n = SWA_BLOCK
    pv = lambda i: jnp.maximum(i - 1, 0)
    return [pl.BlockSpec((n, 512), lambda i: (i, CD_Q // 512)),
            pl.BlockSpec((n, LANES), lambda i: (pv(i), CD_K // LANES)), pl.BlockSpec((n, LANES), lambda i: (pv(i), CD_V // LANES)),
            pl.BlockSpec((n, LANES), lambda i: (i, CD_K // LANES)), pl.BlockSpec((n, LANES), lambda i: (i, CD_V // LANES)),
            pl.BlockSpec((1, LANES), lambda i: (0, 0))]


def _swa_fwd(hc, srow, name):
    s = hc.shape[0]
    n = SWA_BLOCK

    def kern(q_ref, k0_ref, v0_ref, k1_ref, v1_ref, s_ref, y_ref):
        notfirst = pl.program_id(0) > 0
        for p in range(4):
            ln = slice(LANES * p, LANES * (p + 1))
            y_ref[:, ln] = _swa_pair(p, notfirst, q_ref[:, ln], k0_ref[...], v0_ref[...], k1_ref[...], v1_ref[...],
                                     s_ref[...]).astype(BF16)

    return pl.pallas_call(
        kern, name=name, grid=(s // n,), in_specs=_swa_specs(), out_specs=pl.BlockSpec((n, 512), lambda i: (i, 0)),
        out_shape=jax.ShapeDtypeStruct((s, 512), BF16), compiler_params=_cp(("parallel",)),
    )(hc, hc, hc, hc, hc, srow)


def _swa_bwd(dy, dy_cb, hc, srow, name):
    s = hc.shape[0]
    n = SWA_BLOCK

    def kern(dy_ref, q_ref, k0_ref, v0_ref, k1_ref, v1_ref, s_ref, dq_ref, dk_ref, dv_ref, ds_ref):
        i = pl.program_id(0)
        notfirst = i > 0

        @pl.when(i == 0)
        def _():
            dk_ref[...] = jnp.zeros_like(dk_ref)
            dv_ref[...] = jnp.zeros_like(dv_ref)
            ds_ref[...] = jnp.zeros_like(ds_ref)

        r1 = pl.ds(pl.multiple_of(i * n, n), n)
        r0 = pl.ds(pl.multiple_of(jnp.maximum(i - 1, 0) * n, n), n)
        for p in range(4):
            ln = slice(LANES * p, LANES * (p + 1))
            _, pull = jax.vjp(functools.partial(_swa_pair, p, notfirst), q_ref[:, ln], k0_ref[...], v0_ref[...],
                              k1_ref[...], v1_ref[...], s_ref[...])
            dq, dk0, dv0, dk1, dv1, dsr = pull(dy_ref[:, ln])
            dq_ref[:, ln] = dq.astype(BF16)
            dk_ref[r0, :] += dk0
            dv_ref[r0, :] += dv0
            dk_ref[r1, :] += dk1
            dv_ref[r1, :] += dv1
            ds_ref[...] += dsr

    full = pl.BlockSpec((s, LANES), lambda i: (0, 0))
    return pl.pallas_call(
        kern, name=name, grid=(s // n,), in_specs=[pl.BlockSpec((n, 512), lambda i: (i, dy_cb))] + _swa_specs(),
        out_specs=[pl.BlockSpec((n, 512), lambda i: (i, 0)), full, full, pl.BlockSpec((1, LANES), lambda i: (0, 0))],
        out_shape=[jax.ShapeDtypeStruct((s, 512), BF16), jax.ShapeDtypeStruct((s, LANES), F32),
                   jax.ShapeDtypeStruct((s, LANES), F32), jax.ShapeDtypeStruct((1, LANES), F32)],
        compiler_params=_cp(("arbitrary",)),
    )(dy, hc, hc, hc, hc, hc, srow)


RG_TILE = 256


def _neg_expm1(x):
    ser = x * (1 + x / 2 * (1 + x / 3 * (1 + x / 4 * (1 + x / 5 * (1 + x / 6 * (1 + x / 7 * (1 + x / 8)))))))
    return jnp.where(x > -0.25, -ser, 1.0 - jnp.exp(x))


def _rg_gates(xc, wa, wx, ba, bx, lam):
    r = jax.nn.sigmoid(_bdot(xc, wa) + ba)
    i = jax.nn.sigmoid(_bdot(xc, wx) + bx)
    log_a = -RG_C * r * jax.nn.softplus(-lam)
    u = jnp.sqrt(jnp.maximum(_neg_expm1(2.0 * log_a), 0.0)) * (i * xc)
    return jnp.exp(log_a), u


def _rg_out(hs, gate):
    return hs * jax.nn.gelu(gate)


def _rows_to_tile(rows):
    sub = _iota((SUBLANES, 1), 0)
    out = jnp.broadcast_to(rows[0], (SUBLANES, rows[0].shape[1]))
    for j in range(1, SUBLANES):
        out = jnp.where(sub == j, rows[j], out)
    return out


def _rg_fwd(xc, hc, wa, wx, ba, bx, lam, name):
    s = xc.shape[0]
    tm = min(RG_TILE, s)

    def kern(x_ref, g_ref, wa_ref, wx_ref, ba_ref, bx_ref, lam_ref, y_ref, h_ref, a_s, u_s, carry):
        @pl.when(pl.program_id(0) == 0)
        def _():
            carry[...] = jnp.zeros_like(carry)

        a, u = _rg_gates(x_ref[...], wa_ref[...], wx_ref[...], ba_ref[...], bx_ref[...], lam_ref[...])
        a_s[...] = a
        u_s[...] = u

        def body(gi, hp):
            r0 = pl.multiple_of(gi * SUBLANES, SUBLANES)
            ab, ub = a_s[pl.ds(r0, SUBLANES), :], u_s[pl.ds(r0, SUBLANES), :]
            rows = []
            for j in range(SUBLANES):
                hp = ab[j:j + 1] * hp + ub[j:j + 1]
                rows.append(hp)
            h_ref[pl.ds(r0, SUBLANES), :] = _rows_to_tile(rows)
            return hp

        carry[0:1, :] = lax.fori_loop(0, tm // SUBLANES, body, carry[0:1, :])
        y_ref[...] = _rg_out(h_ref[...], g_ref[...]).astype(BF16)

    row = pl.BlockSpec((tm, 512), lambda i: (i, 0))
    mat = pl.BlockSpec((512, 512), lambda i: (0, 0))
    vec = pl.BlockSpec((1, 512), lambda i: (0, 0))
    return pl.pallas_call(
        kern, name=name, grid=(s // tm,),
        in_specs=[row, pl.BlockSpec((tm, 512), lambda i: (i, CD_GATE // 512)), mat, mat, vec, vec, vec],
        out_specs=[row, row],
        out_shape=[jax.ShapeDtypeStruct((s, 512), BF16), jax.ShapeDtypeStruct((s, 512), F32)],
        scratch_shapes=[pltpu.VMEM((tm, 512), F32), pltpu.VMEM((tm, 512), F32), pltpu.VMEM((SUBLANES, 512), F32)],
        compiler_params=_cp(("arbitrary",)),
    )(xc, hc, wa, wx, ba, bx, lam)


def _rg_bwd(dy, dy_cb, xc, hc, hs, wa, wx, ba, bx, lam, name):
    s = xc.shape[0]
    tm = min(RG_TILE, s)
    nt = s // tm
    ti = lambda i: nt - 1 - i

    def kern(dy_ref, x_ref, g_ref, h_ref, hp_ref, wa_ref, wx_ref, ba_ref, bx_ref, lam_ref,
             dx_ref, dg_ref, dwa_ref, dwx_ref, dba_ref, dbx_ref, dlam_ref, a_s, d_s, g_s, cg, ca):
        first = pl.program_id(0) == 0

        @pl.when(first)
        def _():
            cg[...] = jnp.zeros_like(cg)
            ca[...] = jnp.zeros_like(ca)

        (a, _), pull = jax.vjp(_rg_gates, x_ref[...], wa_ref[...].astype(F32), wx_ref[...].astype(F32), ba_ref[...],
                               bx_ref[...], lam_ref[...])
        hs_t = h_ref[...]
        _, pull_out = jax.vjp(_rg_out, hs_t, g_ref[...])
        dho, dgate = pull_out(dy_ref[...])
        dg_ref[...] = dgate.astype(BF16)
        a_s[...] = a
        d_s[...] = dho

        def body(k, c):
            gn, an = c
            r0 = pl.multiple_of((tm // SUBLANES - 1 - k) * SUBLANES, SUBLANES)
            ab, db = a_s[pl.ds(r0, SUBLANES), :], d_s[pl.ds(r0, SUBLANES), :]
            rows = [None] * SUBLANES
            for j in range(SUBLANES - 1, -1, -1):
                gn = db[j:j + 1] + an * gn
                an = ab[j:j + 1]
                rows[j] = gn
            g_s[pl.ds(r0, SUBLANES), :] = _rows_to_tile(rows)
            return gn, an

        gn, an = lax.fori_loop(0, tm // SUBLANES, body, (cg[0:1, :], ca[0:1, :]))
        cg[0:1, :] = gn
        ca[0:1, :] = an
        g = g_s[...]
        halo = jnp.where(pl.program_id(0) == nt - 1, 0.0, hp_ref[...])
        hprev = pltpu.roll(jnp.concatenate([halo, hs_t], axis=0), 1, axis=0)[SUBLANES:]
        dxc, dwa, dwx, dba, dbx, dlam = pull((g * hprev, g))
        dx_ref[...] = dxc
        _acc(dwa_ref, dwa, first)
        _acc(dwx_ref, dwx, first)
        _acc(dba_ref, dba, first)
        _acc(dbx_ref, dbx, first)
        _acc(dlam_ref, dlam, first)

    row = pl.BlockSpec((tm, 512), lambda i: (ti(i), 0))
    mat = pl.BlockSpec((512, 512), lambda i: (0, 0))
    vec = pl.BlockSpec((1, 512), lambda i: (0, 0))
    hprev = pl.BlockSpec((SUBLANES, 512), lambda i: (jnp.maximum(ti(i) * (tm // SUBLANES) - 1, 0), 0))
    return pl.pallas_call(
        kern, name=name, grid=(nt,),
        in_specs=[pl.BlockSpec((tm, 512), lambda i: (ti(i), dy_cb)), row,
                  pl.BlockSpec((tm, 512), lambda i: (ti(i), CD_GATE // 512)), row, hprev, mat, mat, vec, vec, vec],
        out_specs=[row, row, mat, mat, vec, vec, vec],
        out_shape=[jax.ShapeDtypeStruct((s, 512), F32), jax.ShapeDtypeStruct((s, 512), BF16),
                   jax.ShapeDtypeStruct((512, 512), F32), jax.ShapeDtypeStruct((512, 512), F32),
                   jax.ShapeDtypeStruct((1, 512), F32), jax.ShapeDtypeStruct((1, 512), F32), jax.ShapeDtypeStruct((1, 512), F32)],
        scratch_shapes=[pltpu.VMEM((tm, 512), F32)] * 3 + [pltpu.VMEM((SUBLANES, 512), F32)] * 2,
        compiler_params=_cp(("arbitrary",)),
    )(dy, xc, hc, hs, hs, wa, wx, ba, bx, lam)


AB_REF = dict(z=(0, 512), xbc=(512, 768), dt=(1280, 8), hq=(1288, 512), hf=(1800, 512), hi=(2312, 512), hg=(2824, 512))
AB_INT = dict(z=AB_Z, hq=AB_HQ, hf=AB_HF, hi=AB_HI, hg=AB_HG, xbc=AB_XBC, dt=AB_DT)
CD_REF = dict(q=(0, 512), k=(512, 128), v=(640, 128), gate=(768, 512), xr=(1280, 512))


def _cols(w, start, size):
    return lax.slice_in_dim(w, start, start + size, axis=w.ndim - 1)


def _pad_lanes(v):
    return jnp.concatenate([v.astype(F32), jnp.zeros((LANES - v.shape[0],), F32)]).reshape(1, LANES)


def _ab_in_to_int(w):
    parts = [_cols(w, *AB_REF[k]) for k in ("z", "hq", "hf", "hi", "hg", "xbc", "dt")]
    return jnp.concatenate(parts + [jnp.zeros(w.shape[:-1] + (AB_COLS - AB_DT - 8,), w.dtype)], axis=-1)


def _ab_in_to_ref(w):
    return jnp.concatenate([_cols(w, AB_INT[k], AB_REF[k][1]) for k in ("z", "xbc", "dt", "hq", "hf", "hi", "hg")], axis=-1)


def _pairs_to_int(w, axis):
    sl = lambda h: lax.slice_in_dim(w, 64 * h, 64 * h + 64, axis=axis)
    return jnp.concatenate([sl(h) for p in range(4) for h in (p, 4 + p)], axis=axis)


def _pairs_to_ref(w, axis):
    sl = lambda i: lax.slice_in_dim(w, 64 * i, 64 * i + 64, axis=axis)
    return jnp.concatenate([sl(2 * h) for h in range(4)] + [sl(2 * h + 1) for h in range(4)], axis=axis)


def _cd_in_to_int(w):
    q = _pairs_to_int(_cols(w, *CD_REF["q"]), w.ndim - 1)
    return jnp.concatenate([q] + [_cols(w, *CD_REF[k]) for k in ("gate", "xr", "k", "v")], axis=-1)


def _cd_in_to_ref(w):
    q = _pairs_to_ref(_cols(w, CD_Q, 512), w.ndim - 1)
    return jnp.concatenate([q, _cols(w, CD_K, 128), _cols(w, CD_V, 128), _cols(w, CD_GATE, 512), _cols(w, CD_XR, 512)], axis=-1)


def _block_diag(w):
    tiled = jnp.concatenate([w.reshape(RG_WIDTH, 64)] * 8, axis=1)
    return tiled * _BLOCK_DIAG_MASK.astype(w.dtype)


_BLOCK_DIAG_MASK = np.kron(np.eye(8, dtype=np.float32), np.ones((64, 64), np.float32))


def _diag_blocks(m):
    return jnp.sum((m * _BLOCK_DIAG_MASK).reshape(8, 64, 8, 64), axis=2)


def _conv8(w):
    return jnp.concatenate([w, jnp.zeros((SUBLANES - w.shape[0], w.shape[1]), w.dtype)], axis=0)


def _hg_lower_bounds(hg_lower):
    sm = jax.nn.softmax(hg_lower.astype(F32), axis=0)
    return jnp.clip(jnp.cumsum(sm, axis=0) - sm[0], 0.0, 1.0)


def _local_step(x, target, w):
    s = x.shape[0]
    lb_all, lb_pull = jax.vjp(_hg_lower_bounds, w["hg_lower"])
    grads = {k: [None] * v.shape[0] for k, v in w.items()}
    saved = []
    xb = x.astype(BF16)
    for l in range(DEPTH):
        j = l // 2
        t = f"l{l}"
        sv = dict(x=x, xb=xb)
        if l % 2 == 0:
            win = _ab_in_to_int(w["ab_w_in"][j])
            wout = w["ab_w_out"][j]
            h = _mm([(xb, 0, win, 0, D_MODEL)], AB_COLS, F32, t + "_in")
            w8 = _conv8(w["ssd_conv_w"][j])
            xbcc = _dwconv_fwd(h, AB_XBC // CONV_CB, SSD_CONV_DIM, w8, w["ssd_conv_b"][j], 4, t + "_conv")
            sp = (_pad_lanes(w["ssd_dt_bias"][j]), _pad_lanes(w["ssd_a_log"][j]), _pad_lanes(w["ssd_d"][j]),
                  w["ssd_norm_w"][j].reshape(1, 512))
            ya, ssts = _ssd_fwd(h, xbcc, *sp, t + "_ssd")
            hp = (lb_all[j].reshape(1, 512), w["hg_norm_w"][j].reshape(1, LANES))
            yb, hsts = _hg_fwd(h, *hp, t + "_hg")
            sv.update(win=win, wout=wout, h=h, w8=w8, xbcc=xbcc, sp=sp, ssts=ssts, hp=hp, hsts=hsts)
        else:
            win = _cd_in_to_int(w["cd_w_in"][j])
            wout = jnp.concatenate([_pairs_to_int(w["cd_w_out"][j][:512], 0), w["cd_w_out"][j][512:]], axis=0)
            h = _mm([(xb, 0, win, 0, D_MODEL)], CD_COLS, F32, t + "_in")
            srow = _pad_lanes(w["swa_sinks"][j])
            ya = _swa_fwd(h, srow, t + "_swa")
            w8 = _conv8(w["rg_conv_w"][j])
            xc = _dwconv_fwd(h, CD_XR // CONV_CB, RG_WIDTH, w8, w["rg_conv_b"][j], 4, t + "_conv")
            rp = (_block_diag(w["rg_wa"][j]).astype(BF16), _block_diag(w["rg_wx"][j]).astype(BF16),
                  w["rg_ba"][j].reshape(1, 512), w["rg_bx"][j].reshape(1, 512), w["rg_lambda"][j].reshape(1, 512))
            yb, hs = _rg_fwd(xc, h, *rp, t + "_rg")
            sv.update(win=win, wout=wout, h=h, w8=w8, srow=srow, xc=xc, rp=rp, hs=hs)
        m = _mm([(ya, 0, wout, 0, 512), (yb, 0, wout, 1, 512)], D_MODEL, F32, t + "_out")
        x1, x1b, r1 = _ln_fwd(x, m, w["ln_g"][l, 0], w["ln_b"][l, 0], t + "_ln1")
        wup, wdown = w["ffn_w_up"][l], w["ffn_w_down"][l]
        fw8 = _conv8(w["ffn_conv_w"][l])
        hu = _mm([(x1b, 0, wup, 0, D_MODEL)], 2 * FFN_DIM, BF16, t + "_up")
        a = _ffn_gate_fwd(hu, fw8, w["ffn_conv_b"][l], t + "_gate")
        f = _mm([(a, 0, wdown, 0, FFN_DIM)], D_MODEL, F32, t + "_down")
        x, xb, r2 = _ln_fwd(x1, f, w["ln_g"][l, 1], w["ln_b"][l, 1], t + "_ln2")
        sv.update(ya=ya, yb=yb, r1=r1, x1b=x1b, hu=hu, a=a, r2=r2, fw8=fw8)
        saved.append(sv)

    loss = 0.5 * jnp.sum(_sqerr(x, target, "sqerr")) / D_MODEL
    dya, dyb, ca, cb = x, target, 1.0 / D_MODEL, -1.0 / D_MODEL
    for l in range(DEPTH - 1, -1, -1):
        j = l // 2
        t = f"l{l}b"
        sv = saved[l]
        wup, wdown = w["ffn_w_up"][l], w["ffn_w_down"][l]
        dr2, dg2, db2 = _ln_bwd(dya, dyb, ca, cb, sv["r2"], w["ln_g"][l, 1], t + "_ln2")
        da = _mm([(dr2, 0, wdown, 0, D_MODEL)], FFN_DIM, BF16, t + "_down", trans_b=True)
        grads["ffn_w_down"][l] = _mm_tn(sv["a"], 0, FFN_DIM, dr2, 0, D_MODEL, t + "_wdown")
        dhg, dhu, dfw8, dfb = _ffn_gate_bwd(sv["hu"], da, sv["fw8"], w["ffn_conv_b"][l], t + "_gate")
        grads["ffn_conv_w"][l] = dfw8[:3]
        grads["ffn_conv_b"][l] = dfb
        dx1 = _mm([(dhg, 0, wup, 0, FFN_DIM), (dhu, 0, wup, 1, FFN_DIM)], D_MODEL, F32, t + "_up", trans_b=True)
        grads["ffn_w_up"][l] = jnp.concatenate([_mm_tn(sv["x1b"], 0, D_MODEL, dhg, 0, FFN_DIM, t + "_wup_g"),
                                                _mm_tn(sv["x1b"], 0, D_MODEL, dhu, 0, FFN_DIM, t + "_wup_u")], axis=1)
        dr1, dg1, db1 = _ln_bwd(dr2, dx1, ALPHA, 1.0, sv["r1"], w["ln_g"][l, 0], t + "_ln1")
        grads["ln_g"][l] = jnp.stack([dg1, dg2])
        grads["ln_b"][l] = jnp.stack([db1, db2])
        win, wout, h = sv["win"], sv["wout"], sv["h"]
        dycat = _mm([(dr1, 0, wout, 0, D_MODEL)], D_MODEL, F32, t + "_out", trans_b=True)
        dwout = jnp.concatenate([_mm_tn(sv["ya"], 0, 512, dr1, 0, D_MODEL, t + "_wout_a"),
                                 _mm_tn(sv["yb"], 0, 512, dr1, 0, D_MODEL, t + "_wout_b")], axis=0)
        if l % 2 == 0:
            dz, dxbcc, ddt, ddtb, dal, ddsk, dnw = _ssd_bwd(dycat, 0, h, sv["xbcc"], sv["ssts"], *sv["sp"], t + "_ssd")
            dxbc, dcw, dcb = _dwconv_bwd(dxbcc, h, AB_XBC // CONV_CB, SSD_CONV_DIM, sv["w8"], 4, t + "_conv")
            dq, df, di, dg, dlb, dhnw = _hg_bwd(dycat, 1, h, sv["hsts"], *sv["hp"], t + "_hg")
            grads["ab_w_out"][j] = dwout
            grads["ssd_conv_w"][j], grads["ssd_conv_b"][j] = dcw[:4], dcb
            grads["ssd_dt_bias"][j], grads["ssd_a_log"][j], grads["ssd_d"][j] = ddtb[0, :8], dal[0, :8], ddsk[0, :8]
            grads["ssd_norm_w"][j], grads["hg_norm_w"][j] = dnw[0], dhnw[0]
            grads["hg_lower"][j] = dlb[0]
            pieces = [(dz, 0, AB_Z, 512), (dq, 0, AB_HQ, 512), (df, 0, AB_HF, 512), (di, 0, AB_HI, 512), (dg, 0, AB_HG, 512),
                      (dxbc, 0, AB_XBC, 256), (dxbc, 1, AB_XBC + 256, 256), (dxbc, 2, AB_XBC + 512, 256), (ddt, 0, AB_DT, LANES)]
            dwin = [_mm_tn(sv["xb"], 0, D_MODEL, p, 0, p.shape[1], t + f"_win{i}")
                    for i, p in enumerate((dz, dq, df, di, dg, dxbc, ddt))]
            grads["ab_w_in"][j] = _ab_in_to_ref(jnp.concatenate(dwin, axis=1))
        else:
            dq, dk, dv, dsr = _swa_bwd(dycat, 0, h, sv["srow"], t + "_swa")
            dxc, dgate, dwa, dwx, dba, dbx, dlam = _rg_bwd(dycat, 1, sv["xc"], h, sv["hs"], *sv["rp"], t + "_rg")
            dxr, dcw, dcb = _dwconv_bwd(dxc, h, CD_XR // CONV_CB, RG_WIDTH, sv["w8"], 4, t + "_conv")
            grads["cd_w_out"][j] = jnp.concatenate([_pairs_to_ref(dwout[:512], 0), dwout[512:]], axis=0)
            grads["swa_sinks"][j] = dsr[0, :8]
            grads["rg_conv_w"][j], grads["rg_conv_b"][j] = dcw[:4], dcb
            grads["rg_wa"][j], grads["rg_wx"][j] = _diag_blocks(dwa), _diag_blocks(dwx)
            grads["rg_ba"][j], grads["rg_bx"][j], grads["rg_lambda"][j] = dba[0], dbx[0], dlam[0]
            pieces = [(dq, 0, CD_Q, 512), (dgate, 0, CD_GATE, 512), (dxr, 0, CD_XR, 512), (dk, 0, CD_K, LANES), (dv, 0, CD_V, LANES)]
            dwin = [_mm_tn(sv["xb"], 0, D_MODEL, p, 0, p.shape[1], t + f"_win{i}")
                    for i, p in enumerate((dq, dgate, dxr, dk, dv))]
            grads["cd_w_in"][j] = _cd_in_to_ref(jnp.concatenate(dwin, axis=1))
        dx0 = _mm([(p, pcb, win, off // k, k) for p, pcb, off, k in pieces], D_MODEL, F32, t + "_in", trans_b=True)
        dya, dyb, ca, cb = dr1, dx0, ALPHA, 1.0
    grad_x = _axpby(dya, dyb, ca, cb, "grad_x")
    out = {k: (v if k in BIG else jnp.stack(v)) for k, v in grads.items()}
    (out["hg_lower"],) = lb_pull(out["hg_lower"])
    return loss, grad_x, out


MESH = pl.DeviceIdType.MESH
ANY = pl.BlockSpec(memory_space=pl.ANY)
PACK_ROWS = 2048


def _place():
    x, y, c = lax.axis_index("x"), lax.axis_index("y"), lax.axis_index("c")
    return x, y, c, [(1 - x, y), (x, 1 - y), (1 - x, 1 - y)]


def _rcopy(src, dst, send_sems, recv_sems, k, to):
    return pltpu.make_async_remote_copy(src_ref=src, dst_ref=dst, send_sem=send_sems.at[k], recv_sem=recv_sems.at[k],
                                        device_id=to, device_id_type=MESH)


def _gather_chips(wp, name):
    def body(w_ref, out_ref, send_sems, recv_sems):
        x, y, c, chips = _place()
        me, sib = 2 * x + y, (x, y, 1 - c)
        first = [_rcopy(w_ref.at[c], out_ref.at[me, c], send_sems, recv_sems, k, (cx, cy, c)) for k, (cx, cy) in enumerate(chips)]
        for cp in first:
            cp.start()
        passed = []
        for k, (cx, cy) in enumerate(chips):
            blk = out_ref.at[2 * cx + cy, c]
            _rcopy(blk, blk, send_sems, recv_sems, k, (cx, cy, c)).wait_recv()
            passed.append(_rcopy(blk, blk, send_sems, recv_sems, 3 + k, sib))
            passed[-1].start()
        for k, (cx, cy) in enumerate(chips):
            blk = out_ref.at[2 * cx + cy, 1 - c]
            _rcopy(blk, blk, send_sems, recv_sems, 3 + k, sib).wait_recv()
        for cp in first + passed:
            cp.wait_send()

    out = pl.pallas_call(
        body, name=name, in_specs=[ANY], out_specs=ANY, out_shape=jax.ShapeDtypeStruct((4,) + wp.shape, wp.dtype),
        scratch_shapes=[pltpu.SemaphoreType.DMA((6,)), pltpu.SemaphoreType.DMA((6,))],
    )(wp)
    return lax.dynamic_update_slice(out, wp[None], (2 * lax.axis_index("x") + lax.axis_index("y"), 0, 0, 0))


def _swap_halves(g, name):
    def body(g_ref, x_ref, send_sems, recv_sems):
        x, y, c, _ = _place()
        cp = _rcopy(g_ref.at[:, 1 - c], x_ref, send_sems, recv_sems, 0, (x, y, 1 - c))
        cp.start()
        cp.wait()

    return pl.pallas_call(
        body, name=name, in_specs=[ANY], out_specs=ANY,
        out_shape=jax.ShapeDtypeStruct((g.shape[0],) + g.shape[2:], g.dtype),
        scratch_shapes=[pltpu.SemaphoreType.DMA((1,)), pltpu.SemaphoreType.DMA((1,))],
    )(g)


def _scatter_chips(hb, name):
    def body(h_ref, y_ref, send_sems, recv_sems):
        x, y, c, chips = _place()
        me = 2 * x + y
        sends = [_rcopy(h_ref.at[2 * cx + cy], y_ref.at[me], send_sems, recv_sems, k, (cx, cy, c)) for k, (cx, cy) in enumerate(chips)]
        for cp in sends:
            cp.start()
        for k, (cx, cy) in enumerate(chips):
            blk = y_ref.at[2 * cx + cy]
            _rcopy(blk, blk, send_sems, recv_sems, k, (cx, cy, c)).wait_recv()
        for cp in sends:
            cp.wait_send()

    out = pl.pallas_call(
        body, name=name, in_specs=[ANY], out_specs=ANY, out_shape=jax.ShapeDtypeStruct(hb.shape, hb.dtype),
        scratch_shapes=[pltpu.SemaphoreType.DMA((3,)), pltpu.SemaphoreType.DMA((3,))],
    )(hb)
    me = 2 * lax.axis_index("x") + lax.axis_index("y")
    return lax.dynamic_update_slice(out, lax.dynamic_slice_in_dim(hb, me, 1, axis=0), (me, 0, 0))


def _share_half(f, name):
    def body(f_ref, out_ref, send_sems, recv_sems):
        x, y, c, _ = _place()
        sib = (x, y, 1 - c)
        cp = _rcopy(f_ref, out_ref.at[c], send_sems, recv_sems, 0, sib)
        cp.start()
        _rcopy(f_ref, out_ref.at[1 - c], send_sems, recv_sems, 0, sib).wait_recv()
        cp.wait_send()

    out = pl.pallas_call(
        body, name=name, in_specs=[ANY], out_specs=ANY, out_shape=jax.ShapeDtypeStruct((2,) + f.shape, f.dtype),
        scratch_shapes=[pltpu.SemaphoreType.DMA((1,)), pltpu.SemaphoreType.DMA((1,))],
    )(f)
    return lax.dynamic_update_slice(out, f[None], (lax.axis_index("c"), 0, 0))


def _gather_devices(v, name):
    m_per, n = v.shape

    def body(x_ref, out_ref, send_sems, recv_sems, local_sem):
        x, y, c, chips = _place()
        me, sib = (x, y, c), (x, y, 1 - c)

        def rows(px, py, pc):
            return out_ref.at[pl.ds((4 * px + 2 * py + pc) * m_per, m_per), :]

        def copy(k, block, to, src=None):
            return _rcopy(rows(*block) if src is None else src, rows(*block), send_sems, recv_sems, k, to)

        mine = pltpu.make_async_copy(x_ref, rows(*me), local_sem)
        mine.start()
        first = [copy(0, me, sib, src=x_ref)]
        first += [copy(1 + j, me, (*chip, c), src=x_ref) for j, chip in enumerate(chips)]
        for cp in first:
            cp.start()
        passed = [copy(4 + j, (*chip, c), sib) for j, chip in enumerate(chips)]
        for j, chip in enumerate(chips):
            copy(1 + j, (*chip, c), me).wait_recv()
            passed[j].start()
        copy(0, sib, me).wait_recv()
        for j, chip in enumerate(chips):
            copy(4 + j, (*chip, 1 - c), me).wait_recv()
        for cp in first + passed:
            cp.wait_send()
        mine.wait()

    return pl.pallas_call(
        body, name=name, out_shape=jax.ShapeDtypeStruct((8 * m_per, n), v.dtype),
        in_specs=[pl.BlockSpec(memory_space=pltpu.VMEM)], out_specs=pl.BlockSpec(memory_space=pltpu.VMEM),
        scratch_shapes=[pltpu.SemaphoreType.DMA((7,)), pltpu.SemaphoreType.DMA((7,)), pltpu.SemaphoreType.DMA(())],
        compiler_params=_cp(),
    )(v)


def _add_half(g, xr, c, name, tr=2048):
    rh = g.shape[2]

    def kern(c_ref, g_ref, x_ref, o_ref):
        o_ref[...] = (g_ref[0] + x_ref[...]).astype(BF16)

    return pl.pallas_call(
        kern, name=name,
        grid_spec=pltpu.PrefetchScalarGridSpec(
            num_scalar_prefetch=1, grid=(4, rh // tr),
            in_specs=[pl.BlockSpec((1, 1, tr, LANES), lambda k, i, c_ref: (k, c_ref[0], i, 0)),
                      pl.BlockSpec((1, tr, LANES), lambda k, i, c_ref: (k, i, 0))],
            out_specs=pl.BlockSpec((1, tr, LANES), lambda k, i, c_ref: (k, i, 0))),
        out_shape=jax.ShapeDtypeStruct(xr.shape, BF16), compiler_params=_cp(("parallel", "parallel")),
    )(c.reshape(1).astype(jnp.int32), g, xr)


def _sum_blocks(y, name, tr=2048):
    n, r, _ = y.shape
    tr = tr if r % tr == 0 else r

    def kern(y_ref, o_ref):
        acc = y_ref[0].astype(F32)
        for k in range(1, n):
            acc = acc + y_ref[k].astype(F32)
        o_ref[...] = acc

    return pl.pallas_call(
        kern, name=name, grid=(r // tr,), in_specs=[pl.BlockSpec((n, tr, LANES), lambda i: (0, i, 0))],
        out_specs=pl.BlockSpec((tr, LANES), lambda i: (i, 0)), out_shape=jax.ShapeDtypeStruct((r, LANES), F32),
        compiler_params=_cp(("parallel",)),
    )(y)


def _adamw(w, g, m, v, name, tr=256):
    r, n = w.shape
    tr = tr if r % tr == 0 else r

    def kern(w_ref, g_ref, m_ref, v_ref, d_ref, nm_ref, nv_ref):
        gg = g_ref[...]
        nm = ADAM_B1 * m_ref[...] + (1.0 - ADAM_B1) * gg
        nv = ADAM_B2 * v_ref[...] + (1.0 - ADAM_B2) * (gg * gg)
        m_hat = nm / (1.0 - ADAM_B1 ** ADAM_STEP)
        v_hat = nv / (1.0 - ADAM_B2 ** ADAM_STEP)
        d_ref[...] = -ADAM_LR * (m_hat / (jnp.sqrt(v_hat) + ADAM_EPS) + ADAM_WD * w_ref[...])
        nm_ref[...] = nm
        nv_ref[...] = nv

    blk = pl.BlockSpec((tr, n), lambda i: (i, 0))
    return pl.pallas_call(
        kern, name=name, grid=(r // tr,), in_specs=[blk] * 4, out_specs=[blk] * 3,
        out_shape=[jax.ShapeDtypeStruct((r, n), F32)] * 3, compiler_params=_cp(("parallel",)),
    )(w, g, m, v)


def _rows_of(n):
    return -(-n // (LANES * SUBLANES)) * SUBLANES


def _pack(arrs, lead, dtype, mult):
    ls = arrs[0].shape[:lead]
    blocks, total = [], 0
    for a in arrs:
        f = a.astype(dtype).reshape(ls + (-1,))
        n = f.shape[-1]
        rows = _rows_of(n)
        if rows * LANES != n:
            f = jnp.concatenate([f, jnp.zeros(ls + (rows * LANES - n,), dtype)], axis=-1)
        blocks.append(f.reshape(ls + (rows, LANES)))
        total += rows
    if total % mult:
        blocks.append(jnp.zeros(ls + (mult - total % mult, LANES), dtype))
    return jnp.concatenate(blocks, axis=lead)


def _unpack(buf, lead, shapes):
    ls = buf.shape[:lead]
    out, off = [], 0
    for sh in shapes:
        n = int(np.prod(sh))
        rows = _rows_of(n)
        flat = lax.slice_in_dim(buf, off, off + rows, axis=lead).reshape(ls + (rows * LANES,))
        if rows * LANES != n:
            flat = lax.slice_in_dim(flat, 0, n, axis=lead)
        out.append(flat.reshape(ls + tuple(sh)))
        off += rows
    return out


BIG = ("ab_w_in", "ab_w_out", "cd_w_in", "cd_w_out", "ffn_w_up", "ffn_w_down")
BIG_COLS = ("ab_w_in", "cd_w_in", "ffn_w_up")
SMALL_SHARDED = ("ssd_conv_w", "rg_conv_w", "rg_conv_b", "rg_ba", "rg_bx", "rg_lambda", "ffn_conv_w", "ln_g", "ln_b")
WEIGHTS = ("ab_w_in", "ssd_conv_w", "ssd_conv_b", "ssd_dt_bias", "ssd_a_log", "ssd_d", "ssd_norm_w", "hg_lower", "hg_norm_w",
           "ab_w_out", "cd_w_in", "swa_sinks", "rg_conv_w", "rg_conv_b", "rg_wa", "rg_ba", "rg_wx", "rg_bx", "rg_lambda",
           "cd_w_out", "ffn_w_up", "ffn_conv_w", "ffn_conv_b", "ffn_w_down", "ln_g", "ln_b")
SMALL = tuple(n for n in WEIGHTS if n not in BIG)


def _full_from_shards(name, g):
    if name in BIG_COLS or name in SMALL_SHARDED:
        return jnp.concatenate([g[k] for k in range(4)], axis=-1)
    return jnp.concatenate([g[k] for k in range(4)], axis=1)


def _layer_shards(name, f):
    if name in BIG_COLS:
        k, n = f.shape
        return jnp.transpose(f.reshape(k, 4, n // 4), (1, 0, 2)).reshape(4, -1)
    return f.reshape(4, -1)


def kernel(x, ab_w_in, ssd_conv_w, ssd_conv_b, ssd_dt_bias, ssd_a_log, ssd_d, ssd_norm_w, hg_lower, hg_norm_w, ab_w_out, cd_w_in, swa_sinks, rg_conv_w, rg_conv_b, rg_wa, rg_ba, rg_wx, rg_bx, rg_lambda, cd_w_out, ffn_w_up, ffn_conv_w, ffn_conv_b, ffn_w_down, ln_g, ln_b, loss_target, m_ab_w_in, m_ssd_conv_w, m_ssd_conv_b, m_ssd_dt_bias, m_ssd_a_log, m_ssd_d, m_ssd_norm_w, m_hg_lower, m_hg_norm_w, m_ab_w_out, m_cd_w_in, m_swa_sinks, m_rg_conv_w, m_rg_conv_b, m_rg_wa, m_rg_ba, m_rg_wx, m_rg_bx, m_rg_lambda, m_cd_w_out, m_ffn_w_up, m_ffn_conv_w, m_ffn_conv_b, m_ffn_w_down, m_ln_g, m_ln_b, v_ab_w_in, v_ssd_conv_w, v_ssd_conv_b, v_ssd_dt_bias, v_ssd_a_log, v_ssd_d, v_ssd_norm_w, v_hg_lower, v_hg_norm_w, v_ab_w_out, v_cd_w_in, v_swa_sinks, v_rg_conv_w, v_rg_conv_b, v_rg_wa, v_rg_ba, v_rg_wx, v_rg_bx, v_rg_lambda, v_cd_w_out, v_ffn_w_up, v_ffn_conv_w, v_ffn_conv_b, v_ffn_w_down, v_ln_g, v_ln_b):
    args = locals()
    w = {n: args[n] for n in WEIGHTS}
    mom = {n: args["m_" + n] for n in WEIGHTS}
    var = {n: args["v_" + n] for n in WEIGHTS}
    cx, cy, cc = lax.axis_index("x"), lax.axis_index("y"), lax.axis_index("c")
    chip = 2 * cx + cy

    big_shapes = [w[n].shape for n in BIG]
    wp = _pack([w[n] for n in BIG], 0, BF16, 2 * PACK_ROWS)
    rh = wp.shape[0] // 2
    gathered = _gather_chips(wp.reshape(2, rh, LANES), "gather_weights").reshape(4, 2 * rh, LANES)
    full = {n: _full_from_shards(n, g) for n, g in zip(BIG, _unpack(gathered, 1, big_shapes))}
    sp = _pack([w[n] for n in SMALL_SHARDED], 0, F32, SUBLANES)
    sg = _gather_devices(sp, "gather_small").reshape(4, 2, sp.shape[0], LANES)[:, 0]
    for n, g in zip(SMALL_SHARDED, _unpack(sg, 1, [w[n].shape for n in SMALL_SHARDED])):
        full[n] = _full_from_shards(n, g)
    for n in SMALL:
        full.setdefault(n, w[n])

    loss, grad_x, grads = _local_step(x[0], loss_target[0], full)
    loss = lax.psum(loss, ("x", "y", "c"))

    g4 = _pack([jnp.concatenate([_layer_shards(n, f) for f in grads[n]], axis=1) for n in BIG], 1, F32,
               2 * PACK_ROWS).reshape(4, 2, rh, LANES)
    hb = _add_half(g4, _swap_halves(g4, "swap_halves"), cc, "add_halves")
    own = _sum_blocks(_scatter_chips(hb, "scatter_chips"), "sum_chips")
    gbig = _unpack(_share_half(own, "share_half").reshape(2 * rh, LANES), 0, big_shapes)
    gout = dict(zip(BIG, gbig))

    small_shapes = [grads[n].shape for n in SMALL]
    gs = _pack([grads[n] for n in SMALL], 0, F32, SUBLANES)
    gsum = _sum_blocks(_gather_devices(gs, "gather_small_grads").reshape(8, gs.shape[0], LANES), "sum_small")
    for n, g in zip(SMALL, _unpack(gsum, 0, small_shapes)):
        if n in SMALL_SHARDED:
            width = w[n].shape[-1]
            g = lax.dynamic_slice_in_dim(g, chip * width, width, axis=g.ndim - 1)
        gout[n] = g

    delta, new_m, new_v = {}, {}, {}
    for n in BIG:
        sh = w[n].shape
        two = lambda a: a.reshape(-1, sh[-1])
        d, nm, nv = _adamw(two(w[n]), two(gout[n]), two(mom[n]), two(var[n]), "adamw_" + n)
        delta[n], new_m[n], new_v[n] = d.reshape(sh), nm.reshape(sh), nv.reshape(sh)
    local_shapes = [w[n].shape for n in SMALL]
    packs = [_pack([d[n] for n in SMALL], 0, F32, SUBLANES) for d in (w, gout, mom, var)]
    for dst, buf in zip((delta, new_m, new_v), _adamw(*packs, "adamw_small")):
        dst.update(zip(SMALL, _unpack(buf, 0, local_shapes)))

    return (loss, grad_x[None], *[gout[n] for n in WEIGHTS], *[delta[n] for n in WEIGHTS],
            *[new_m[n] for n in WEIGHTS], *[new_v[n] for n in WEIGHTS])
```

```python
import functools
import math

import numpy as np
import jax
import jax.numpy as jnp
from jax import lax
from jax.experimental import pallas as pl
from jax.experimental.pallas import tpu as pltpu

F32 = jnp.float32
BF16 = jnp.bfloat16
HI = lax.Precision.HIGHEST

D_MODEL = 1024
DEPTH = 4
SSD_HEADS = 8
SSD_D_INNER = 512
SSD_CONV_DIM = 768
SSD_CHUNK = 128
HG_CHUNK = 64
HG_HEADS = 4
SWA_BLOCK = 128
RG_WIDTH = 512
FFN_DIM = 2816
LN_EPS = 1e-5
RMS_EPS = 1e-6
MASK_VALUE = -1e9
ALPHA = (2 * DEPTH) ** 0.25
RG_C = 8.0
ADAM_LR, ADAM_B1, ADAM_B2, ADAM_EPS, ADAM_WD, ADAM_STEP = 0.001, 0.9, 0.999, 1e-08, 0.01, 10

LANES = 128
SUBLANES = 8
HALO = 16
CONV_CB = 256
VMEM_LIMIT = 56 * 1024 * 1024

AB_Z, AB_HQ, AB_HF, AB_HI, AB_HG, AB_XBC, AB_DT, AB_COLS = 0, 512, 1024, 1536, 2048, 2560, 3328, 3456
CD_Q, CD_GATE, CD_XR, CD_K, CD_V, CD_COLS = 0, 512, 1024, 1536, 1664, 1792


def _cp(sem=None):
    kw = dict(vmem_limit_bytes=VMEM_LIMIT)
    if sem is not None:
        kw["dimension_semantics"] = sem
    return pltpu.CompilerParams(**kw)


def _dot(a, b, dims=(((1,), (0,)), ((), ())), precision=None):
    return lax.dot_general(a, b, dims, precision=precision, preferred_element_type=F32)


NN = (((1,), (0,)), ((), ()))
NT = (((1,), (1,)), ((), ()))
TN = (((0,), (0,)), ((), ()))


@functools.partial(jax.custom_vjp, nondiff_argnums=(2,))
def _bdot(a, b, dims=NN):
    return _dot(a.astype(BF16), b.astype(BF16), dims)


def _bdot_fwd(a, b, dims):
    return _bdot(a, b, dims), (a, b)


def _bdot_bwd(dims, res, ct):
    a, b = res
    ab, bb, cb = a.astype(BF16), b.astype(BF16), ct.astype(BF16)
    if dims == NN:
        da, db = _dot(cb, bb, NT), _dot(ab, cb, TN)
    elif dims == NT:
        da, db = _dot(cb, bb, NN), _dot(cb, ab, TN)
    else:
        da, db = _dot(bb, cb, NT), _dot(ab, cb, NN)
    return da.astype(a.dtype), db.astype(b.dtype)


_bdot.defvjp(_bdot_fwd, _bdot_bwd)


def _hdot(a, b, dims=NN):
    return _dot(a, b, dims, precision=HI)


WIDE_TILES = (1408, 1152, 1024, 896, 768, 512, 384, 256, 128)
MM_BLOCK_BYTES = 6 * 1024 * 1024


def _pick(n, cands):
    for c in cands:
        if n % c == 0:
            return c
    return n


def _mm(pairs, n_out, out_dtype, name, trans_b=False, tm=512):
    m = pairs[0][0].shape[0]
    a_row_bytes = sum(p[4] * p[0].dtype.itemsize for p in pairs)
    b_col_bytes = sum(p[4] * p[2].dtype.itemsize for p in pairs)
    tm = min(2 * tm if 2 * tm * a_row_bytes <= 2 * MM_BLOCK_BYTES else tm, m)
    tn = _pick(n_out, [c for c in WIDE_TILES if c * b_col_bytes <= MM_BLOCK_BYTES])
    n = len(pairs)

    def kern(*refs):
        acc = None
        for i in range(n):
            p = _bdot(refs[i][...], refs[n + i][...], NT if trans_b else NN)
            acc = p if acc is None else acc + p
        refs[2 * n][...] = acc.astype(out_dtype)

    in_specs = []
    for a, acb, b, bcb, k in pairs:
        in_specs.append(pl.BlockSpec((tm, k), functools.partial(lambda i, j, c: (i, c), c=acb)))
    for a, acb, b, bcb, k in pairs:
        if trans_b:
            in_specs.append(pl.BlockSpec((tn, k), functools.partial(lambda i, j, c: (j, c), c=bcb)))
        else:
            in_specs.append(pl.BlockSpec((k, tn), functools.partial(lambda i, j, c: (c, j), c=bcb)))
    return pl.pallas_call(
        kern, name=name, grid=(m // tm, n_out // tn), in_specs=in_specs,
        out_specs=pl.BlockSpec((tm, tn), lambda i, j: (i, j)),
        out_shape=jax.ShapeDtypeStruct((m, n_out), out_dtype),
        compiler_params=_cp(("parallel", "arbitrary")),
    )(*[p[0] for p in pairs], *[p[2] for p in pairs])


def _mm_tn(a, a_cb, ka, g, g_cb, ng, name, tm=2048):
    m = a.shape[0]
    tm = min(tm, m)
    tk = _pick(ka, (1024, 1408, 512, 256, 128))
    tn = _pick(ng, WIDE_TILES)
    nm = m // tm

    def kern(a_ref, g_ref, o_ref):
        p = _bdot(a_ref[...], g_ref[...], TN)

        @pl.when(pl.program_id(2) == 0)
        def _():
            o_ref[...] = p

        @pl.when(pl.program_id(2) > 0)
        def _():
            o_ref[...] += p

    ak, gk = ka // tk, ng // tn
    return pl.pallas_call(
        kern, name=name, grid=(ak, gk, nm),
        in_specs=[pl.BlockSpec((tm, tk), lambda i, j, r: (r, a_cb * ak + i)),
                  pl.BlockSpec((tm, tn), lambda i, j, r: (r, g_cb * gk + j))],
        out_specs=pl.BlockSpec((tk, tn), lambda i, j, r: (i, j)),
        out_shape=jax.ShapeDtypeStruct((ka, ng), F32),
        compiler_params=_cp(("parallel", "parallel", "arbitrary")),
    )(a, g)


def _ln_fwd(x, m, g, b, name, tm=256):
    s, d = x.shape
    tm = min(tm, s)

    def kern(x_ref, m_ref, g_ref, b_ref, y_ref, yb_ref, r_ref):
        r = ALPHA * x_ref[...] + m_ref[...]
        mu = jnp.mean(r, -1, keepdims=True)
        xc = r - mu
        var = jnp.mean(xc * xc, -1, keepdims=True)
        y = xc * lax.rsqrt(var + LN_EPS) * g_ref[...] + b_ref[...]
        y_ref[...] = y
        yb_ref[...] = y.astype(BF16)
        r_ref[...] = r

    row = pl.BlockSpec((tm, d), lambda i: (i, 0))
    vec = pl.BlockSpec((1, d), lambda i: (0, 0))
    return pl.pallas_call(
        kern, name=name, grid=(s // tm,), in_specs=[row, row, vec, vec], out_specs=[row, row, row],
        out_shape=[jax.ShapeDtypeStruct((s, d), F32), jax.ShapeDtypeStruct((s, d), BF16), jax.ShapeDtypeStruct((s, d), F32)],
        compiler_params=_cp(("parallel",)),
    )(x, m, g.reshape(1, d), b.reshape(1, d))


def _ln_bwd(dya, dyb, ca, cb, r, g, name, tm=256):
    s, d = r.shape
    tm = min(tm, s)

    def kern(a_ref, b_ref, r_ref, g_ref, dr_ref, dg_ref, db_ref):
        dy = ca * a_ref[...] + cb * b_ref[...]
        rr = r_ref[...]
        mu = jnp.mean(rr, -1, keepdims=True)
        xc = rr - mu
        var = jnp.mean(xc * xc, -1, keepdims=True)
        rstd = lax.rsqrt(var + LN_EPS)
        xhat = xc * rstd
        dxh = dy * g_ref[...]
        dr_ref[...] = rstd * (dxh - jnp.mean(dxh, -1, keepdims=True) - xhat * jnp.mean(dxh * xhat, -1, keepdims=True))
        dg = jnp.sum(dy * xhat, 0, keepdims=True)
        db = jnp.sum(dy, 0, keepdims=True)

        @pl.when(pl.program_id(0) == 0)
        def _():
            dg_ref[...] = dg
            db_ref[...] = db

        @pl.when(pl.program_id(0) > 0)
        def _():
            dg_ref[...] += dg
            db_ref[...] += db

    row = pl.BlockSpec((tm, d), lambda i: (i, 0))
    vec = pl.BlockSpec((1, d), lambda i: (0, 0))
    dr, dg, db = pl.pallas_call(
        kern, name=name, grid=(s // tm,), in_specs=[row, row, row, vec], out_specs=[row, vec, vec],
        out_shape=[jax.ShapeDtypeStruct((s, d), F32), jax.ShapeDtypeStruct((1, d), F32), jax.ShapeDtypeStruct((1, d), F32)],
        compiler_params=_cp(("arbitrary",)),
    )(dya, dyb, r, g.reshape(1, d))
    return dr, dg[0], db[0]


def _sqerr(y, t, name, tm=256):
    s, d = y.shape
    tm = min(tm, s)

    def kern(y_ref, t_ref, o_ref):
        e = y_ref[...] - t_ref[...]
        p = jnp.sum(e * e, 0, keepdims=True)

        @pl.when(pl.program_id(0) == 0)
        def _():
            o_ref[...] = p

        @pl.when(pl.program_id(0) > 0)
        def _():
            o_ref[...] += p

    row = pl.BlockSpec((tm, d), lambda i: (i, 0))
    return pl.pallas_call(
        kern, name=name, grid=(s // tm,), in_specs=[row, row], out_specs=pl.BlockSpec((1, d), lambda i: (0, 0)),
        out_shape=jax.ShapeDtypeStruct((1, d), F32), compiler_params=_cp(("arbitrary",)),
    )(y, t)


def _axpby(a, b, ca, cb, name, tm=256):
    s, d = a.shape
    tm = min(tm, s)

    def kern(a_ref, b_ref, o_ref):
        o_ref[...] = ca * a_ref[...] + cb * b_ref[...]

    row = pl.BlockSpec((tm, d), lambda i: (i, 0))
    return pl.pallas_call(
        kern, name=name, grid=(s // tm,), in_specs=[row, row], out_specs=row,
        out_shape=jax.ShapeDtypeStruct((s, d), F32), compiler_params=_cp(("parallel",)),
    )(a, b)


def _shift_down(xx, s, n):
    if s == 0:
        return xx[HALO:HALO + n]
    return pltpu.roll(xx, s, axis=0)[HALO:HALO + n]


def _shift_up(yy, s, n):
    if s == 0:
        return yy[0:n]
    return pltpu.roll(yy, yy.shape[0] - s, axis=0)[0:n]


def _prev_halo(tm):
    return lambda i: jnp.maximum(i * (tm // HALO) - 1, 0)


def _dwconv_fwd(x, x_cb0, c, w8, b, k, name, tm=512):
    s = x.shape[0]
    tm = min(tm, s)
    ph = _prev_halo(tm)

    def kern(x_ref, h_ref, w_ref, b_ref, y_ref):
        halo = jnp.where(pl.program_id(0) == 0, 0.0, h_ref[...].astype(F32))
        xx = jnp.concatenate([halo, x_ref[...].astype(F32)], axis=0)
        w = w_ref[...]
        acc = b_ref[...] + w[k - 1:k] * xx[HALO:]
        for j in range(k - 1):
            acc = acc + w[j:j + 1] * _shift_down(xx, k - 1 - j, tm)
        y_ref[...] = acc

    return pl.pallas_call(
        kern, name=name, grid=(s // tm, c // CONV_CB),
        in_specs=[pl.BlockSpec((tm, CONV_CB), lambda i, j: (i, x_cb0 + j)),
                  pl.BlockSpec((HALO, CONV_CB), lambda i, j: (ph(i), x_cb0 + j)),
                  pl.BlockSpec((SUBLANES, CONV_CB), lambda i, j: (0, j)),
                  pl.BlockSpec((1, CONV_CB), lambda i, j: (0, j))],
        out_specs=pl.BlockSpec((tm, CONV_CB), lambda i, j: (i, j)),
        out_shape=jax.ShapeDtypeStruct((s, c), F32), compiler_params=_cp(("parallel", "parallel")),
    )(x, x, w8, b.reshape(1, c))


def _dwconv_bwd(dy, x, x_cb0, c, w8, k, name, tm=512):
    s = x.shape[0]
    tm = min(tm, s)
    nt = s // tm
    ph = _prev_halo(tm)
    nh = lambda i: jnp.minimum((i + 1) * (tm // HALO), s // HALO - 1)

    def kern(dy_ref, dyn_ref, x_ref, h_ref, w_ref, dx_ref, dw_ref, db_ref):
        i = pl.program_id(1)
        halo = jnp.where(i == 0, 0.0, h_ref[...].astype(F32))
        xx = jnp.concatenate([halo, x_ref[...].astype(F32)], axis=0)
        dyt = dy_ref[...]
        nxt = jnp.where(i == nt - 1, 0.0, dyn_ref[...])
        dyy = jnp.concatenate([dyt, nxt], axis=0)
        w = w_ref[...]
        dx = w[k - 1:k] * dyt
        rows = [jnp.sum(dyt * _shift_down(xx, k - 1 - j, tm), 0, keepdims=True) for j in range(k)]
        for j in range(k - 1):
            dx = dx + w[j:j + 1] * _shift_up(dyy, k - 1 - j, tm)
        dx_ref[...] = dx.astype(BF16)
        dw = jnp.concatenate(rows + [jnp.zeros((SUBLANES - k, CONV_CB), F32)], axis=0)
        db = jnp.sum(dyt, 0, keepdims=True)

        @pl.when(i == 0)
        def _():
            dw_ref[...] = dw
            db_ref[...] = db

        @pl.when(i > 0)
        def _():
            dw_ref[...] += dw
            db_ref[...] += db

    dx, dw, db = pl.pallas_call(
        kern, name=name, grid=(c // CONV_CB, nt),
        in_specs=[pl.BlockSpec((tm, CONV_CB), lambda j, i: (i, j)),
                  pl.BlockSpec((HALO, CONV_CB), lambda j, i: (nh(i), j)),
                  pl.BlockSpec((tm, CONV_CB), lambda j, i: (i, x_cb0 + j)),
                  pl.BlockSpec((HALO, CONV_CB), lambda j, i: (ph(i), x_cb0 + j)),
                  pl.BlockSpec((SUBLANES, CONV_CB), lambda j, i: (0, j))],
        out_specs=[pl.BlockSpec((tm, CONV_CB), lambda j, i: (i, j)),
                   pl.BlockSpec((SUBLANES, CONV_CB), lambda j, i: (0, j)),
                   pl.BlockSpec((1, CONV_CB), lambda j, i: (0, j))],
        out_shape=[jax.ShapeDtypeStruct((s, c), BF16), jax.ShapeDtypeStruct((SUBLANES, c), F32),
                   jax.ShapeDtypeStruct((1, c), F32)],
        compiler_params=_cp(("parallel", "arbitrary")),
    )(dy, dy, x, x, w8)
    return dx, dw, db[0]


def _silu(x):
    return x * jax.nn.sigmoid(x)


def _ffn_gate_fwd(hu, w8, b, name, tm=512):
    s = hu.shape[0]
    tm = min(tm, s)
    nb = FFN_DIM // CONV_CB
    ph = _prev_halo(tm)
    k = 3

    def conv(x_ref, h_ref, w_ref, b_ref):
        halo = jnp.where(pl.program_id(0) == 0, 0.0, h_ref[...].astype(F32))
        xx = jnp.concatenate([halo, x_ref[...].astype(F32)], axis=0)
        w = w_ref[...]
        acc = b_ref[...] + w[k - 1:k] * xx[HALO:]
        for j in range(k - 1):
            acc = acc + w[j:j + 1] * _shift_down(xx, k - 1 - j, tm)
        return acc

    def kern(g_ref, gh_ref, u_ref, uh_ref, wg_ref, wu_ref, bg_ref, bu_ref, a_ref):
        g = conv(g_ref, gh_ref, wg_ref, bg_ref)
        u = conv(u_ref, uh_ref, wu_ref, bu_ref)
        a_ref[...] = (_silu(g) * u).astype(BF16)

    main = lambda off: pl.BlockSpec((tm, CONV_CB), lambda i, j: (i, off + j))
    halo = lambda off: pl.BlockSpec((HALO, CONV_CB), lambda i, j: (ph(i), off + j))
    wsp = lambda off: pl.BlockSpec((SUBLANES, CONV_CB), lambda i, j: (0, off + j))
    bsp = lambda off: pl.BlockSpec((1, CONV_CB), lambda i, j: (0, off + j))
    b2 = b.reshape(1, 2 * FFN_DIM)
    return pl.pallas_call(
        kern, name=name, grid=(s // tm, nb),
        in_specs=[main(0), halo(0), main(nb), halo(nb), wsp(0), wsp(nb), bsp(0), bsp(nb)],
        out_specs=pl.BlockSpec((tm, CONV_CB), lambda i, j: (i, j)),
        out_shape=jax.ShapeDtypeStruct((s, FFN_DIM), BF16), compiler_params=_cp(("parallel", "parallel")),
    )(hu, hu, hu, hu, w8, w8, b2, b2)


def _ffn_gate_bwd(hu, da, w8, b, name, tm=512):
    s = hu.shape[0]
    tm = min(tm, s)
    nt = s // tm
    nb = FFN_DIM // CONV_CB
    ph = _prev_halo(tm)
    nh = lambda i: jnp.minimum((i + 1) * (tm // HALO), s // HALO - 1)
    k = 3
    ne = tm + HALO

    def kern(g_ref, gp_ref, gn_ref, u_ref, up_ref, un_ref, da_ref, dan_ref, wg_ref, wu_ref, bg_ref, bu_ref,
             dg_ref, du_ref, dwg_ref, dwu_ref, dbg_ref, dbu_ref):
        i = pl.program_id(1)

        def ext(x_ref, p_ref, n_ref):
            halo = jnp.where(i == 0, 0.0, p_ref[...].astype(F32))
            return jnp.concatenate([halo, x_ref[...].astype(F32), n_ref[...].astype(F32)], axis=0)

        def conv(xx, w, bb):
            acc = bb + w[k - 1:k] * xx[HALO:]
            for j in range(k - 1):
                acc = acc + w[j:j + 1] * _shift_down(xx, k - 1 - j, ne)
            return acc

        gx, ux = ext(g_ref, gp_ref, gn_ref), ext(u_ref, up_ref, un_ref)
        wg, wu = wg_ref[...], wu_ref[...]
        g = conv(gx, wg, bg_ref[...])
        u = conv(ux, wu, bu_ref[...])
        dae = jnp.concatenate([da_ref[...].astype(F32), jnp.where(i == nt - 1, 0.0, dan_ref[...].astype(F32))], axis=0)
        sg = jax.nn.sigmoid(g)
        dhg = dae * u * (sg * (1.0 + g * (1.0 - sg)))
        dhu = dae * (g * sg)

        def back(dh, xx, w):
            dx = w[k - 1:k] * dh[0:tm]
            for j in range(k - 1):
                dx = dx + w[j:j + 1] * _shift_up(dh, k - 1 - j, tm)
            rows = [jnp.sum(dh[0:tm] * _shift_down(xx, k - 1 - j, tm), 0, keepdims=True) for j in range(k)]
            dw = jnp.concatenate(rows + [jnp.zeros((SUBLANES - k, CONV_CB), F32)], axis=0)
            return dx, dw, jnp.sum(dh[0:tm], 0, keepdims=True)

        dxg, dwg, dbg = back(dhg, gx, wg)
        dxu, dwu, dbu = back(dhu, ux, wu)
        dg_ref[...] = dxg.astype(BF16)
        du_ref[...] = dxu.astype(BF16)

        @pl.when(i == 0)
        def _():
            dwg_ref[...] = dwg
            dwu_ref[...] = dwu
            dbg_ref[...] = dbg
            dbu_ref[...] = dbu

        @pl.when(i > 0)
        def _():
            dwg_ref[...] += dwg
            dwu_ref[...] += dwu
            dbg_ref[...] += dbg
            dbu_ref[...] += dbu

    main = lambda off: pl.BlockSpec((tm, CONV_CB), lambda j, i: (i, off + j))
    prev = lambda off: pl.BlockSpec((HALO, CONV_CB), lambda j, i: (ph(i), off + j))
    nxt = lambda off: pl.BlockSpec((HALO, CONV_CB), lambda j, i: (nh(i), off + j))
    wsp = lambda off: pl.BlockSpec((SUBLANES, CONV_CB), lambda j, i: (0, off + j))
    bsp = lambda off: pl.BlockSpec((1, CONV_CB), lambda j, i: (0, off + j))
    b2 = b.reshape(1, 2 * FFN_DIM)
    outs = pl.pallas_call(
        kern, name=name, grid=(nb, nt),
        in_specs=[main(0), prev(0), nxt(0), main(nb), prev(nb), nxt(nb), main(0), nxt(0), wsp(0), wsp(nb), bsp(0), bsp(nb)],
        out_specs=[main(0), main(0), wsp(0), wsp(0), bsp(0), bsp(0)],
        out_shape=[jax.ShapeDtypeStruct((s, FFN_DIM), BF16), jax.ShapeDtypeStruct((s, FFN_DIM), BF16),
                   jax.ShapeDtypeStruct((SUBLANES, FFN_DIM), F32), jax.ShapeDtypeStruct((SUBLANES, FFN_DIM), F32),
                   jax.ShapeDtypeStruct((1, FFN_DIM), F32), jax.ShapeDtypeStruct((1, FFN_DIM), F32)],
        compiler_params=_cp(("parallel", "arbitrary")),
    )(hu, hu, hu, hu, hu, hu, da, da, w8, w8, b2, b2)
    dg, du, dwg, dwu, dbg, dbu = outs
    return dg, du, jnp.concatenate([dwg, dwu], axis=1), jnp.concatenate([dbg[0], dbu[0]])


def _iota(shape, dim):
    return lax.broadcasted_iota(jnp.int32, shape, dim)


def _in_range(idx, start, size):
    return jnp.logical_and(idx >= start, idx < start + size)


@functools.partial(jax.custom_vjp, nondiff_argnums=(2, 3))
def _zdot(a, b, dims, z):
    zz, x = ((a, b) if z == 0 else (b, a))
    zz = zz.astype(BF16)
    xh = x.astype(BF16)
    r1 = x - xh.astype(F32)
    xm = r1.astype(BF16)
    xl = (r1 - xm.astype(F32)).astype(BF16)
    f = (lambda p: _dot(zz, p, dims)) if z == 0 else (lambda p: _dot(p, zz, dims))
    return f(xh) + f(xm) + f(xl)


def _zdot_fwd(a, b, dims, z):
    return _zdot(a, b, dims, z), (a if z == 0 else b)


def _zdot_bwd(dims, z, zz, ct):
    zb = zz.astype(BF16)
    ch = ct.astype(BF16)
    cm = (ct - ch.astype(F32)).astype(BF16)
    if dims == NN and z == 0:
        f = lambda c: _dot(zb, c, TN)
    elif dims == NN and z == 1:
        f = lambda c: _dot(c, zb, NT)
    else:
        f = lambda c: _dot(c, zb, TN)
    dx = f(ch) + f(cm)
    return (jnp.zeros_like(zz), dx) if z == 0 else (dx, jnp.zeros_like(zz))


_zdot.defvjp(_zdot_fwd, _zdot_bwd)


def _expand8(v, e):
    return jnp.sum(_zdot(jnp.broadcast_to(v, (SUBLANES, v.shape[1])), e, NN, 1), 0, keepdims=True) * (1.0 / SUBLANES)


def _ssd_consts():
    tri = np.tril(np.ones((SSD_CHUNK, SSD_CHUNK), np.float32))
    e = np.zeros((LANES, SSD_D_INNER), np.float32)
    for h in range(SSD_HEADS):
        e[h, 64 * h:64 * h + 64] = 1.0
    bd = np.zeros((LANES, SSD_D_INNER), np.float32)
    bd[:64, :256] = 1.0
    bd[64:, 256:] = 1.0
    return jnp.asarray(tri), jnp.asarray(e), jnp.asarray(bd)


def _ssd_chunk(tri, e, bd, z, xs_p, bm_p, cm_p, dtr, st, dtb, alog, dsk, nw):
    n = z.shape[0]
    lane128, lane512 = _iota((1, LANES), 1), _iota((1, SSD_D_INNER), 1)
    sub8 = _iota((SUBLANES, 1), 0)
    tril = _iota((n, 1), 0) >= _iota((1, n), 1)
    head_rows = jnp.where(sub8 == lane128, 1.0, 0.0)
    xs, bm, cm = _silu(xs_p), _silu(bm_p), _silu(cm_p)
    dt = jax.nn.softplus(dtr + dtb)
    da = dt * (-jnp.exp(alog))
    acs = _zdot(tri, da, NN, 0)
    acst = _zdot(head_rows, acs, NT, 0)
    xc = xs * _zdot(dt, e, NN, 1)
    tot = jnp.sum(da, 0, keepdims=True)
    y = _bdot(cm, st) * _zdot(jnp.exp(acs), e, NN, 1)
    st_new = st * _expand8(jnp.exp(tot), e) + bd * _bdot(bm, xc * _zdot(jnp.exp(tot - acs), e, NN, 1), TN)
    for g in range(2):
        cb = _bdot(jnp.where(_in_range(lane128, 64 * g, 64), cm, 0.0), bm, NT)
        for h in range(4 * g, 4 * g + 4):
            col = jnp.sum(jnp.where(lane128 == h, acs, 0.0), 1, keepdims=True)
            row = jnp.sum(jnp.where(sub8 == h, acst, 0.0), 0, keepdims=True)
            dm = jnp.where(tril, jnp.exp(jnp.minimum(col - row, 0.0)), 0.0)
            y = y + _bdot(cb * dm, jnp.where(_in_range(lane512, 64 * h, 64), xc, 0.0))
    y = (y + _expand8(dsk, e) * xs) * _silu(z)
    ysq = y * y
    g0 = lane512 < 256
    ms0 = jnp.sum(jnp.where(g0, ysq, 0.0), 1, keepdims=True) * (1.0 / 256)
    ms1 = jnp.sum(jnp.where(g0, 0.0, ysq), 1, keepdims=True) * (1.0 / 256)
    rs = jnp.where(g0, lax.rsqrt(ms0 + RMS_EPS), lax.rsqrt(ms1 + RMS_EPS))
    return y * rs * nw, st_new


def _ssd_specs(rev, nc):
    ci = (lambda i: nc - 1 - i) if rev else (lambda i: i)
    n = SSD_CHUNK
    act = [pl.BlockSpec((n, 512), lambda i: (ci(i), AB_Z // 512)),
           pl.BlockSpec((n, 512), lambda i: (ci(i), 0)),
           pl.BlockSpec((n, LANES), lambda i: (ci(i), 4)),
           pl.BlockSpec((n, LANES), lambda i: (ci(i), 5)),
           pl.BlockSpec((n, LANES), lambda i: (ci(i), AB_DT // LANES))]
    const = [pl.BlockSpec((n, n), lambda i: (0, 0)), pl.BlockSpec((LANES, 512), lambda i: (0, 0)),
             pl.BlockSpec((LANES, 512), lambda i: (0, 0))]
    par = [pl.BlockSpec((1, LANES), lambda i: (0, 0))] * 3 + [pl.BlockSpec((1, 512), lambda i: (0, 0))]
    return ci, act, const, par


def _ssd_fwd(h, xbcc, dtb, alog, dsk, nw, name):
    s = h.shape[0]
    nc = s // SSD_CHUNK
    ci, act, const, par = _ssd_specs(False, nc)

    def kern(z_ref, xs_ref, bm_ref, cm_ref, dt_ref, tri_ref, e_ref, bd_ref, dtb_ref, al_ref, dsk_ref, nw_ref,
             y_ref, sts_ref, st):
        @pl.when(pl.program_id(0) == 0)
        def _():
            st[...] = jnp.zeros_like(st)

        sts_ref[0] = st[...]
        y, stn = _ssd_chunk(tri_ref[...], e_ref[...], bd_ref[...], z_ref[...], xs_ref[...], bm_ref[...], cm_ref[...],
                            dt_ref[...], st[...], dtb_ref[...], al_ref[...], dsk_ref[...], nw_ref[...])
        y_ref[...] = y.astype(BF16)
        st[...] = stn

    return pl.pallas_call(
        kern, name=name, grid=(nc,), in_specs=act + const + par,
        out_specs=[pl.BlockSpec((SSD_CHUNK, 512), lambda i: (i, 0)), pl.BlockSpec((1, LANES, 512), lambda i: (i, 0, 0))],
        out_shape=[jax.ShapeDtypeStruct((s, 512), BF16), jax.ShapeDtypeStruct((nc, LANES, 512), F32)],
        scratch_shapes=[pltpu.VMEM((LANES, 512), F32)], compiler_params=_cp(("arbitrary",)),
    )(h, xbcc, xbcc, xbcc, h, *_ssd_consts(), dtb, alog, dsk, nw)


def _acc(ref, val, first):
    @pl.when(first)
    def _():
        ref[...] = val

    @pl.when(jnp.logical_not(first))
    def _():
        ref[...] += val


def _ssd_bwd(dy, dy_cb, h, xbcc, sts, dtb, alog, dsk, nw, name):
    s = h.shape[0]
    nc = s // SSD_CHUNK
    n = SSD_CHUNK
    ci, act, const, par = _ssd_specs(True, nc)

    def kern(dy_ref, sts_ref, z_ref, xs_ref, bm_ref, cm_ref, dt_ref, tri_ref, e_ref, bd_ref, dtb_ref, al_ref, dsk_ref,
             nw_ref, dz_ref, dx_ref, ddt_ref, ddtb_ref, dal_ref, ddsk_ref, dnw_ref, dst):
        first = pl.program_id(0) == 0

        @pl.when(first)
        def _():
            dst[...] = jnp.zeros_like(dst)

        f = functools.partial(_ssd_chunk, tri_ref[...], e_ref[...], bd_ref[...])
        _, pull = jax.vjp(f, z_ref[...], xs_ref[...], bm_ref[...], cm_ref[...], dt_ref[...], sts_ref[0],
                          dtb_ref[...], al_ref[...], dsk_ref[...], nw_ref[...])
        dz, dxs, dbm, dcm, ddt, dsti, ddtb, dal, ddsk, dnw = pull((dy_ref[...], dst[...]))
        dz_ref[...] = dz.astype(BF16)
        dx_ref[:, 0:512] = dxs
        dx_ref[:, 512:640] = dbm
        dx_ref[:, 640:768] = dcm
        ddt_ref[...] = ddt.astype(BF16)
        dst[...] = dsti
        _acc(ddtb_ref, ddtb, first)
        _acc(dal_ref, dal, first)
        _acc(ddsk_ref, ddsk, first)
        _acc(dnw_ref, dnw, first)

    vec = pl.BlockSpec((1, LANES), lambda i: (0, 0))
    return pl.pallas_call(
        kern, name=name, grid=(nc,),
        in_specs=[pl.BlockSpec((n, 512), lambda i: (ci(i), dy_cb)), pl.BlockSpec((1, LANES, 512), lambda i: (ci(i), 0, 0))]
        + act + const + par,
        out_specs=[pl.BlockSpec((n, 512), lambda i: (ci(i), 0)), pl.BlockSpec((n, SSD_CONV_DIM), lambda i: (ci(i), 0)),
                   pl.BlockSpec((n, LANES), lambda i: (ci(i), 0)), vec, vec, vec, pl.BlockSpec((1, 512), lambda i: (0, 0))],
        out_shape=[jax.ShapeDtypeStruct((s, 512), BF16), jax.ShapeDtypeStruct((s, SSD_CONV_DIM), F32),
                   jax.ShapeDtypeStruct((s, LANES), BF16), jax.ShapeDtypeStruct((1, LANES), F32),
                   jax.ShapeDtypeStruct((1, LANES), F32), jax.ShapeDtypeStruct((1, LANES), F32),
                   jax.ShapeDtypeStruct((1, 512), F32)],
        scratch_shapes=[pltpu.VMEM((LANES, 512), F32)], compiler_params=_cp(("arbitrary",)),
    )(dy, sts, h, xbcc, xbcc, xbcc, h, *_ssd_consts(), dtb, alog, dsk, nw)


HG_LEVELS = 6


HG_SUMS = 1 + 2 * HG_LEVELS


def _hg_consts():
    n = HG_CHUNK
    t = np.arange(n)
    mats = [np.tril(np.ones((n, n), np.float32))]
    dq, dk, mk = [], [], []
    for lv in range(HG_LEVELS):
        b = 1 << lv
        start = (t // b) * b
        end = start + b - 1
        dq.append(((t[None, :] >= start[:, None]) & (t[None, :] <= t[:, None])).astype(np.float32))
        dk.append(((t[None, :] > t[:, None]) & (t[None, :] <= end[:, None])).astype(np.float32))
        mk.append((((t // b) % 2 == 1)[:, None] & ((t // b)[None, :] == (t // b)[:, None] - 1)).astype(np.float32))
    c1 = np.concatenate(mats + dq + dk, axis=0)
    return (jnp.asarray(c1, BF16), jnp.asarray(np.concatenate([c1, c1, c1], axis=1), BF16), jnp.asarray(np.stack(mk)))


@functools.partial(jax.custom_vjp, nondiff_argnums=(1, 2))
def _split_grid(x, nr, nc):
    br, bc = x.shape[0] // nr, x.shape[1] // nc
    return tuple(x[i * br:(i + 1) * br, j * bc:(j + 1) * bc] for i in range(nr) for j in range(nc))


def _split_grid_fwd(x, nr, nc):
    return _split_grid(x, nr, nc), None


def _split_grid_bwd(nr, nc, _, cts):
    rows = [cts[i * nc] if nc == 1 else jnp.concatenate(cts[i * nc:(i + 1) * nc], axis=1) for i in range(nr)]
    return (rows[0] if nr == 1 else jnp.concatenate(rows, axis=0),)


_split_grid.defvjp(_split_grid_fwd, _split_grid_bwd)


@jax.custom_vjp
def _level_sums(c1, c3, g):
    gh = g.astype(BF16)
    r1 = g - gh.astype(F32)
    gm = r1.astype(BF16)
    gl = (r1 - gm.astype(F32)).astype(BF16)
    return _dot(c3, jnp.concatenate([gh, gm, gl], axis=0), NN)


def _level_sums_fwd(c1, c3, g):
    return _level_sums(c1, c3, g), (c1, c3)


def _level_sums_bwd(res, ct):
    c1, c3 = res
    ch = ct.astype(BF16)
    cm = (ct - ch.astype(F32)).astype(BF16)
    return jnp.zeros_like(c1), jnp.zeros_like(c3), _dot(c1, ch, TN) + _dot(c1, cm, TN)


_level_sums.defvjp(_level_sums_fwd, _level_sums_bwd)


def _hg_chunk(c1, c3, masks, hq, hf, hi, hg, sts, lb, nw):
    q = _silu(hq)
    g = jnp.log(lb + (1.0 - lb) * jax.nn.sigmoid(hf))
    k = (1.0 - lb) * jax.nn.sigmoid(-hf)
    parts = _split_grid(_level_sums(c1, c3, g), HG_SUMS, 1)
    bc = parts[0]
    tot = jnp.sum(g, 0, keepdims=True)
    qe = _split_grid(q * jnp.exp(bc), 1, HG_HEADS)
    kd = _split_grid(k * jnp.exp(tot - bc), 1, HG_HEADS)
    etot = _split_grid(jnp.exp(tot), 1, HG_HEADS)
    qh, kh, vh, gh = [_split_grid(a, 1, HG_HEADS) for a in (q, k, hi, hg)]
    att = [None] * HG_HEADS
    for lv in range(HG_LEVELS):
        qt = _split_grid(q * jnp.exp(parts[1 + lv]), 1, HG_HEADS)
        kt = _split_grid(k * jnp.exp(parts[1 + HG_LEVELS + lv]), 1, HG_HEADS)
        for hd in range(HG_HEADS):
            t = masks[lv] * _bdot(qt[hd], kt[hd], NT)
            att[hd] = t if att[hd] is None else att[hd] + t
    outs, stn = [], []
    for hd in range(HG_HEADS):
        o = (_bdot(att[hd], vh[hd]) + jnp.sum(qh[hd] * kh[hd], 1, keepdims=True) * vh[hd] + _bdot(qe[hd], sts[hd], NT))
        stn.append(sts[hd] * etot[hd] + _bdot(vh[hd], kd[hd], TN))
        outs.append(o * lax.rsqrt(jnp.mean(o * o, 1, keepdims=True) + RMS_EPS) * nw * _silu(gh[hd]))
    return tuple(outs), tuple(stn)


def _hg_specs(rev, nc):
    ci = (lambda i: nc - 1 - i) if rev else (lambda i: i)
    n = HG_CHUNK
    act = [pl.BlockSpec((n, 512), functools.partial(lambda i, c: (ci(i), c), c=c))
           for c in (AB_HQ // 512, AB_HF // 512, AB_HI // 512, AB_HG // 512)]
    other = [pl.BlockSpec((HG_SUMS * n, n), lambda i: (0, 0)), pl.BlockSpec((HG_SUMS * n, 3 * n), lambda i: (0, 0)),
             pl.BlockSpec((HG_LEVELS, n, n), lambda i: (0, 0, 0)), pl.BlockSpec((1, 512), lambda i: (0, 0)),
             pl.BlockSpec((1, LANES), lambda i: (0, 0))]
    return ci, act, other


def _head_rows(hd):
    return slice(LANES * hd, LANES * (hd + 1))


def _hg_fwd(h, lb, nw, name):
    s = h.shape[0]
    nc = s // HG_CHUNK
    ci, act, other = _hg_specs(False, nc)

    def kern(q_ref, f_ref, i_ref, g_ref, c1_ref, c3_ref, mk_ref, lb_ref, nw_ref, y_ref, sts_ref, st):
        @pl.when(pl.program_id(0) == 0)
        def _():
            st[...] = jnp.zeros_like(st)

        sts_ref[0] = st[...]
        masks = [mk_ref[lv] for lv in range(HG_LEVELS)]
        outs, stn = _hg_chunk(c1_ref[...], c3_ref[...], masks, q_ref[...], f_ref[...], i_ref[...], g_ref[...],
                              tuple(st[_head_rows(hd), :] for hd in range(HG_HEADS)), lb_ref[...], nw_ref[...])
        for hd in range(HG_HEADS):
            y_ref[:, _head_rows(hd)] = outs[hd].astype(BF16)
            st[_head_rows(hd), :] = stn[hd]

    return pl.pallas_call(
        kern, name=name, grid=(nc,), in_specs=act + other,
        out_specs=[pl.BlockSpec((HG_CHUNK, 512), lambda i: (i, 0)), pl.BlockSpec((1, 512, LANES), lambda i: (i, 0, 0))],
        out_shape=[jax.ShapeDtypeStruct((s, 512), BF16), jax.ShapeDtypeStruct((nc, 512, LANES), F32)],
        scratch_shapes=[pltpu.VMEM((512, LANES), F32)], compiler_params=_cp(("arbitrary",)),
    )(h, h, h, h, *_hg_consts(), lb, nw)


def _hg_bwd(dy, dy_cb, h, sts, lb, nw, name):
    s = h.shape[0]
    nc = s // HG_CHUNK
    n = HG_CHUNK
    ci, act, other = _hg_specs(True, nc)

    def kern(dy_ref, sts_ref, q_ref, f_ref, i_ref, g_ref, c1_ref, c3_ref, mk_ref, lb_ref, nw_ref,
             dq_ref, df_ref, di_ref, dg_ref, dlb_ref, dnw_ref, dst):
        first = pl.program_id(0) == 0

        @pl.when(first)
        def _():
            dst[...] = jnp.zeros_like(dst)

        masks = [mk_ref[lv] for lv in range(HG_LEVELS)]
        heads = range(HG_HEADS)
        _, pull = jax.vjp(functools.partial(_hg_chunk, c1_ref[...], c3_ref[...], masks), q_ref[...], f_ref[...], i_ref[...],
                          g_ref[...], tuple(sts_ref[0, _head_rows(hd), :] for hd in heads), lb_ref[...], nw_ref[...])
        dq, df, di, dg, dsti, dlb, dnw = pull((tuple(dy_ref[:, _head_rows(hd)] for hd in heads),
                                               tuple(dst[_head_rows(hd), :] for hd in heads)))
        dq_ref[...] = dq.astype(BF16)
        df_ref[...] = df.astype(BF16)
        di_ref[...] = di.astype(BF16)
        dg_ref[...] = dg.astype(BF16)
        for hd in heads:
            dst[_head_rows(hd), :] = dsti[hd]
        _acc(dlb_ref, dlb, first)
        _acc(dnw_ref, dnw, first)

    blk = pl.BlockSpec((n, 512), lambda i: (ci(i), 0))
    return pl.pallas_call(
        kern, name=name, grid=(nc,),
        in_specs=[pl.BlockSpec((n, 512), lambda i: (ci(i), dy_cb)), pl.BlockSpec((1, 512, LANES), lambda i: (ci(i), 0, 0))]
        + act + other,
        out_specs=[blk, blk, blk, blk, pl.BlockSpec((1, 512), lambda i: (0, 0)), pl.BlockSpec((1, LANES), lambda i: (0, 0))],
        out_shape=[jax.ShapeDtypeStruct((s, 512), BF16)] * 4 + [jax.ShapeDtypeStruct((1, 512), F32),
                                                                jax.ShapeDtypeStruct((1, LANES), F32)],
        scratch_shapes=[pltpu.VMEM((512, LANES), F32)], compiler_params=_cp(("arbitrary",)),
    )(dy, sts, h, h, h, h, *_hg_consts(), lb, nw)


def _swa_consts():
    n = SWA_BLOCK
    r = np.arange(8 * n)
    t, blk = r % n, r // n
    lanes = np.arange(LANES)
    own = (t[:, None] >= lanes[None, :]).astype(np.float32)
    prev = (lanes[None, :] > t[:, None]).astype(np.float32)
    sel = (lanes[None, :] == (blk // 2 + 4 * (blk % 2))[:, None]).astype(np.float32)
    grp = ((lanes[None, :] // 64) == (blk % 2)[:, None]).astype(np.float32)
    return tuple(jnp.asarray(a) for a in (own, prev, sel, grp))


def _swa_block(own, prev, sel, grp, q, k0, v0, k1, v1, srow):
    qs = _split_grid(q, 1, 4)
    qa = jnp.concatenate([qs[p] for p in range(4) for _ in range(2)], axis=0) * grp
    s0 = jnp.where(prev > 0.0, _bdot(qa, k0, NT) * 0.125, MASK_VALUE)
    s1 = jnp.where(own > 0.0, _bdot(qa, k1, NT) * 0.125, MASK_VALUE)
    sk = jnp.sum(sel * srow, 1, keepdims=True)
    m = lax.stop_gradient(jnp.maximum(jnp.maximum(jnp.max(s0, 1, keepdims=True), jnp.max(s1, 1, keepdims=True)), sk))
    p0, p1 = jnp.exp(s0 - m), jnp.exp(s1 - m)
    inv = 1.0 / (jnp.sum(p0, 1, keepdims=True) + jnp.sum(p1, 1, keepdims=True) + jnp.exp(sk - m))
    o = _split_grid((_bdot(p0 * inv, v0) + _bdot(p1 * inv, v1)) * grp, 8, 1)
    return tuple(o[2 * p] + o[2 * p + 1] for p in range(4))


def _swa_specs():
    n = SWA_BLOCK
    pv = lambda i: jnp.maximum(i - 1, 0)
    mask = pl.BlockSpec((8 * n, LANES), lambda i: (0, 0))
    return [mask, mask, mask, mask, pl.BlockSpec((n, 512), lambda i: (i, CD_Q // 512)),
            pl.BlockSpec((n, LANES), lambda i: (pv(i), CD_K // LANES)), pl.BlockSpec((n, LANES), lambda i: (pv(i), CD_V // LANES)),
            pl.BlockSpec((n, LANES), lambda i: (i, CD_K // LANES)), pl.BlockSpec((n, LANES), lambda i: (i, CD_V // LANES)),
            pl.BlockSpec((1, LANES), lambda i: (0, 0))]


def _swa_args(refs, notfirst):
    own, prev, sel, grp = (r[...] for r in refs[:4])
    return (own, prev * notfirst.astype(F32), sel, grp), tuple(r[...] for r in refs[4:])


def _swa_fwd(hc, srow, name):
    s = hc.shape[0]
    n = SWA_BLOCK

    def kern(*refs):
        y_ref = refs[10]
        consts, acts = _swa_args(refs[:10], pl.program_id(0) > 0)
        outs = _swa_block(*consts, *acts)
        for p in range(4):
            y_ref[:, LANES * p:LANES * (p + 1)] = outs[p].astype(BF16)

    return pl.pallas_call(
        kern, name=name, grid=(s // n,), in_specs=_swa_specs(), out_specs=pl.BlockSpec((n, 512), lambda i: (i, 0)),
        out_shape=jax.ShapeDtypeStruct((s, 512), BF16), compiler_params=_cp(("parallel",)),
    )(*_swa_consts(), hc, hc, hc, hc, hc, srow)


def _swa_bwd(dy, dy_cb, hc, srow, name):
    s = hc.shape[0]
    n = SWA_BLOCK

    def kern(dy_ref, *refs):
        dq_ref, dk_ref, dv_ref, ds_ref = refs[10:]
        i = pl.program_id(0)

        @pl.when(i == 0)
        def _():
            dk_ref[...] = jnp.zeros_like(dk_ref)
            dv_ref[...] = jnp.zeros_like(dv_ref)
            ds_ref[...] = jnp.zeros_like(ds_ref)

        r1 = pl.ds(pl.multiple_of(i * n, n), n)
        r0 = pl.ds(pl.multiple_of(jnp.maximum(i - 1, 0) * n, n), n)
        consts, acts = _swa_args(refs[:10], i > 0)
        _, pull = jax.vjp(functools.partial(_swa_block, *consts), *acts)
        dq, dk0, dv0, dk1, dv1, dsr = pull(tuple(dy_ref[:, LANES * p:LANES * (p + 1)] for p in range(4)))
        dq_ref[...] = dq.astype(BF16)
        dk_ref[r0, :] += dk0
        dv_ref[r0, :] += dv0
        dk_ref[r1, :] += dk1
        dv_ref[r1, :] += dv1
        ds_ref[...] += dsr

    full = pl.BlockSpec((s, LANES), lambda i: (0, 0))
    return pl.pallas_call(
        kern, name=name, grid=(s // n,), in_specs=[pl.BlockSpec((n, 512), lambda i: (i, dy_cb))] + _swa_specs(),
        out_specs=[pl.BlockSpec((n, 512), lambda i: (i, 0)), full, full, pl.BlockSpec((1, LANES), lambda i: (0, 0))],
        out_shape=[jax.ShapeDtypeStruct((s, 512), BF16), jax.ShapeDtypeStruct((s, LANES), F32),
                   jax.ShapeDtypeStruct((s, LANES), F32), jax.ShapeDtypeStruct((1, LANES), F32)],
        compiler_params=_cp(("arbitrary",)),
    )(dy, *_swa_consts(), hc, hc, hc, hc, hc, srow)


RG_TILE = 256


def _neg_expm1(x):
    ser = x * (1 + x / 2 * (1 + x / 3 * (1 + x / 4 * (1 + x / 5 * (1 + x / 6 * (1 + x / 7 * (1 + x / 8)))))))
    return jnp.where(x > -0.25, -ser, 1.0 - jnp.exp(x))


def _rg_gates(xc, wa, wx, ba, bx, lam):
    r = jax.nn.sigmoid(_bdot(xc, wa) + ba)
    i = jax.nn.sigmoid(_bdot(xc, wx) + bx)
    log_a = -RG_C * r * jax.nn.softplus(-lam)
    u = jnp.sqrt(jnp.maximum(_neg_expm1(2.0 * log_a), 0.0)) * (i * xc)
    return jnp.exp(log_a), u


def _rg_out(hs, gate):
    return hs * jax.nn.gelu(gate)


def _rows_to_tile(rows):
    sub = _iota((SUBLANES, 1), 0)
    out = jnp.broadcast_to(rows[0], (SUBLANES, rows[0].shape[1]))
    for j in range(1, SUBLANES):
        out = jnp.where(sub == j, rows[j], out)
    return out


def _rg_fwd(xc, hc, wa, wx, ba, bx, lam, name):
    s = xc.shape[0]
    tm = min(RG_TILE, s)

    def kern(x_ref, g_ref, wa_ref, wx_ref, ba_ref, bx_ref, lam_ref, y_ref, h_ref, a_s, u_s, carry):
        @pl.when(pl.program_id(0) == 0)
        def _():
            carry[...] = jnp.zeros_like(carry)

        a, u = _rg_gates(x_ref[...], wa_ref[...], wx_ref[...], ba_ref[...], bx_ref[...], lam_ref[...])
        a_s[...] = a
        u_s[...] = u

        def body(gi, hp):
            r0 = pl.multiple_of(gi * SUBLANES, SUBLANES)
            ab, ub = a_s[pl.ds(r0, SUBLANES), :], u_s[pl.ds(r0, SUBLANES), :]
            rows = []
            for j in range(SUBLANES):
                hp = ab[j:j + 1] * hp + ub[j:j + 1]
                rows.append(hp)
            h_ref[pl.ds(r0, SUBLANES), :] = _rows_to_tile(rows)
            return hp

        carry[0:1, :] = lax.fori_loop(0, tm // SUBLANES, body, carry[0:1, :])
        y_ref[...] = _rg_out(h_ref[...], g_ref[...]).astype(BF16)

    row = pl.BlockSpec((tm, 512), lambda i: (i, 0))
    mat = pl.BlockSpec((512, 512), lambda i: (0, 0))
    vec = pl.BlockSpec((1, 512), lambda i: (0, 0))
    return pl.pallas_call(
        kern, name=name, grid=(s // tm,),
        in_specs=[row, pl.BlockSpec((tm, 512), lambda i: (i, CD_GATE // 512)), mat, mat, vec, vec, vec],
        out_specs=[row, row],
        out_shape=[jax.ShapeDtypeStruct((s, 512), BF16), jax.ShapeDtypeStruct((s, 512), F32)],
        scratch_shapes=[pltpu.VMEM((tm, 512), F32), pltpu.VMEM((tm, 512), F32), pltpu.VMEM((SUBLANES, 512), F32)],
        compiler_params=_cp(("arbitrary",)),
    )(xc, hc, wa, wx, ba, bx, lam)


def _rg_bwd(dy, dy_cb, xc, hc, hs, wa, wx, ba, bx, lam, name):
    s = xc.shape[0]
    tm = min(RG_TILE, s)
    nt = s // tm
    ti = lambda i: nt - 1 - i

    def kern(dy_ref, x_ref, g_ref, h_ref, hp_ref, wa_ref, wx_ref, ba_ref, bx_ref, lam_ref,
             dx_ref, dg_ref, dwa_ref, dwx_ref, dba_ref, dbx_ref, dlam_ref, a_s, d_s, g_s, cg, ca):
        first = pl.program_id(0) == 0

        @pl.when(first)
        def _():
            cg[...] = jnp.zeros_like(cg)
            ca[...] = jnp.zeros_like(ca)

        (a, _), pull = jax.vjp(_rg_gates, x_ref[...], wa_ref[...].astype(F32), wx_ref[...].astype(F32), ba_ref[...],
                               bx_ref[...], lam_ref[...])
        hs_t = h_ref[...]
        _, pull_out = jax.vjp(_rg_out, hs_t, g_ref[...])
        dho, dgate = pull_out(dy_ref[...])
        dg_ref[...] = dgate.astype(BF16)
        a_s[...] = a
        d_s[...] = dho

        def body(k, c):
            gn, an = c
            r0 = pl.multiple_of((tm // SUBLANES - 1 - k) * SUBLANES, SUBLANES)
            ab, db = a_s[pl.ds(r0, SUBLANES), :], d_s[pl.ds(r0, SUBLANES), :]
            rows = [None] * SUBLANES
            for j in range(SUBLANES - 1, -1, -1):
                gn = db[j:j + 1] + an * gn
                an = ab[j:j + 1]
                rows[j] = gn
            g_s[pl.ds(r0, SUBLANES), :] = _rows_to_tile(rows)
            return gn, an

        gn, an = lax.fori_loop(0, tm // SUBLANES, body, (cg[0:1, :], ca[0:1, :]))
        cg[0:1, :] = gn
        ca[0:1, :] = an
        g = g_s[...]
        halo = jnp.where(pl.program_id(0) == nt - 1, 0.0, hp_ref[...])
        hprev = pltpu.roll(jnp.concatenate([halo, hs_t], axis=0), 1, axis=0)[SUBLANES:]
        dxc, dwa, dwx, dba, dbx, dlam = pull((g * hprev, g))
        dx_ref[...] = dxc
        _acc(dwa_ref, dwa, first)
        _acc(dwx_ref, dwx, first)
        _acc(dba_ref, dba, first)
        _acc(dbx_ref, dbx, first)
        _acc(dlam_ref, dlam, first)

    row = pl.BlockSpec((tm, 512), lambda i: (ti(i), 0))
    mat = pl.BlockSpec((512, 512), lambda i: (0, 0))
    vec = pl.BlockSpec((1, 512), lambda i: (0, 0))
    hprev = pl.BlockSpec((SUBLANES, 512), lambda i: (jnp.maximum(ti(i) * (tm // SUBLANES) - 1, 0), 0))
    return pl.pallas_call(
        kern, name=name, grid=(nt,),
        in_specs=[pl.BlockSpec((tm, 512), lambda i: (ti(i), dy_cb)), row,
                  pl.BlockSpec((tm, 512), lambda i: (ti(i), CD_GATE // 512)), row, hprev, mat, mat, vec, vec, vec],
        out_specs=[row, row, mat, mat, vec, vec, vec],
        out_shape=[jax.ShapeDtypeStruct((s, 512), F32), jax.ShapeDtypeStruct((s, 512), BF16),
                   jax.ShapeDtypeStruct((512, 512), F32), jax.ShapeDtypeStruct((512, 512), F32),
                   jax.ShapeDtypeStruct((1, 512), F32), jax.ShapeDtypeStruct((1, 512), F32), jax.ShapeDtypeStruct((1, 512), F32)],
        scratch_shapes=[pltpu.VMEM((tm, 512), F32)] * 3 + [pltpu.VMEM((SUBLANES, 512), F32)] * 2,
        compiler_params=_cp(("arbitrary",)),
    )(dy, xc, hc, hs, hs, wa, wx, ba, bx, lam)


AB_REF = dict(z=(0, 512), xbc=(512, 768), dt=(1280, 8), hq=(1288, 512), hf=(1800, 512), hi=(2312, 512), hg=(2824, 512))
AB_INT = dict(z=AB_Z, hq=AB_HQ, hf=AB_HF, hi=AB_HI, hg=AB_HG, xbc=AB_XBC, dt=AB_DT)
CD_REF = dict(q=(0, 512), k=(512, 128), v=(640, 128), gate=(768, 512), xr=(1280, 512))


def _cols(w, start, size):
    return lax.slice_in_dim(w, start, start + size, axis=w.ndim - 1)


def _pad_lanes(v):
    return jnp.concatenate([v.astype(F32), jnp.zeros((LANES - v.shape[0],), F32)]).reshape(1, LANES)


def _ab_in_to_int(w):
    parts = [_cols(w, *AB_REF[k]) for k in ("z", "hq", "hf", "hi", "hg", "xbc", "dt")]
    return jnp.concatenate(parts + [jnp.zeros(w.shape[:-1] + (AB_COLS - AB_DT - 8,), w.dtype)], axis=-1)


def _ab_in_to_ref(w):
    return jnp.concatenate([_cols(w, AB_INT[k], AB_REF[k][1]) for k in ("z", "xbc", "dt", "hq", "hf", "hi", "hg")], axis=-1)


def _pairs_to_int(w, axis):
    sl = lambda h: lax.slice_in_dim(w, 64 * h, 64 * h + 64, axis=axis)
    return jnp.concatenate([sl(h) for p in range(4) for h in (p, 4 + p)], axis=axis)


def _pairs_to_ref(w, axis):
    sl = lambda i: lax.slice_in_dim(w, 64 * i, 64 * i + 64, axis=axis)
    return jnp.concatenate([sl(2 * h) for h in range(4)] + [sl(2 * h + 1) for h in range(4)], axis=axis)


def _cd_in_to_int(w):
    q = _pairs_to_int(_cols(w, *CD_REF["q"]), w.ndim - 1)
    return jnp.concatenate([q] + [_cols(w, *CD_REF[k]) for k in ("gate", "xr", "k", "v")], axis=-1)


def _cd_in_to_ref(w):
    q = _pairs_to_ref(_cols(w, CD_Q, 512), w.ndim - 1)
    return jnp.concatenate([q, _cols(w, CD_K, 128), _cols(w, CD_V, 128), _cols(w, CD_GATE, 512), _cols(w, CD_XR, 512)], axis=-1)


def _block_diag(w):
    tiled = jnp.concatenate([w.reshape(RG_WIDTH, 64)] * 8, axis=1)
    return tiled * _BLOCK_DIAG_MASK.astype(w.dtype)


_BLOCK_DIAG_MASK = np.kron(np.eye(8, dtype=np.float32), np.ones((64, 64), np.float32))


def _diag_blocks(m):
    return jnp.sum((m * _BLOCK_DIAG_MASK).reshape(8, 64, 8, 64), axis=2)


def _conv8(w):
    return jnp.concatenate([w, jnp.zeros((SUBLANES - w.shape[0], w.shape[1]), w.dtype)], axis=0)


def _hg_lower_bounds(hg_lower):
    sm = jax.nn.softmax(hg_lower.astype(F32), axis=0)
    return jnp.clip(jnp.cumsum(sm, axis=0) - sm[0], 0.0, 1.0)


def _local_step(x, target, w):
    s = x.shape[0]
    lb_all, lb_pull = jax.vjp(_hg_lower_bounds, w["hg_lower"])
    grads = {k: [None] * v.shape[0] for k, v in w.items()}
    saved = []
    xb = x.astype(BF16)
    for l in range(DEPTH):
        j = l // 2
        t = f"l{l}"
        sv = dict(x=x, xb=xb)
        if l % 2 == 0:
            win = _ab_in_to_int(w["ab_w_in"][j])
            wout = w["ab_w_out"][j]
            h = _mm([(xb, 0, win, 0, D_MODEL)], AB_COLS, F32, t + "_in")
            w8 = _conv8(w["ssd_conv_w"][j])
            xbcc = _dwconv_fwd(h, AB_XBC // CONV_CB, SSD_CONV_DIM, w8, w["ssd_conv_b"][j], 4, t + "_conv")
            sp = (_pad_lanes(w["ssd_dt_bias"][j]), _pad_lanes(w["ssd_a_log"][j]), _pad_lanes(w["ssd_d"][j]),
                  w["ssd_norm_w"][j].reshape(1, 512))
            ya, ssts = _ssd_fwd(h, xbcc, *sp, t + "_ssd")
            hp = (lb_all[j].reshape(1, 512), w["hg_norm_w"][j].reshape(1, LANES))
            yb, hsts = _hg_fwd(h, *hp, t + "_hg")
            sv.update(win=win, wout=wout, h=h, w8=w8, xbcc=xbcc, sp=sp, ssts=ssts, hp=hp, hsts=hsts)
        else:
            win = _cd_in_to_int(w["cd_w_in"][j])
            wout = jnp.concatenate([_pairs_to_int(w["cd_w_out"][j][:512], 0), w["cd_w_out"][j][512:]], axis=0)
            h = _mm([(xb, 0, win, 0, D_MODEL)], CD_COLS, F32, t + "_in")
            srow = _pad_lanes(w["swa_sinks"][j])
            ya = _swa_fwd(h, srow, t + "_swa")
            w8 = _conv8(w["rg_conv_w"][j])
            xc = _dwconv_fwd(h, CD_XR // CONV_CB, RG_WIDTH, w8, w["rg_conv_b"][j], 4, t + "_conv")
            rp = (_block_diag(w["rg_wa"][j]).astype(BF16), _block_diag(w["rg_wx"][j]).astype(BF16),
                  w["rg_ba"][j].reshape(1, 512), w["rg_bx"][j].reshape(1, 512), w["rg_lambda"][j].reshape(1, 512))
            yb, hs = _rg_fwd(xc, h, *rp, t + "_rg")
            sv.update(win=win, wout=wout, h=h, w8=w8, srow=srow, xc=xc, rp=rp, hs=hs)
        m = _mm([(ya, 0, wout, 0, 512), (yb, 0, wout, 1, 512)], D_MODEL, F32, t + "_out")
        x1, x1b, r1 = _ln_fwd(x, m, w["ln_g"][l, 0], w["ln_b"][l, 0], t + "_ln1")
        wup, wdown = w["ffn_w_up"][l], w["ffn_w_down"][l]
        fw8 = _conv8(w["ffn_conv_w"][l])
        hu = _mm([(x1b, 0, wup, 0, D_MODEL)], 2 * FFN_DIM, BF16, t + "_up")
        a = _ffn_gate_fwd(hu, fw8, w["ffn_conv_b"][l], t + "_gate")
        f = _mm([(a, 0, wdown, 0, FFN_DIM)], D_MODEL, F32, t + "_down")
        x, xb, r2 = _ln_fwd(x1, f, w["ln_g"][l, 1], w["ln_b"][l, 1], t + "_ln2")
        sv.update(ya=ya, yb=yb, r1=r1, x1b=x1b, hu=hu, a=a, r2=r2, fw8=fw8)
        saved.append(sv)

    loss = 0.5 * jnp.sum(_sqerr(x, target, "sqerr")) / D_MODEL
    dya, dyb, ca, cb = x, target, 1.0 / D_MODEL, -1.0 / D_MODEL
    for l in range(DEPTH - 1, -1, -1):
        j = l // 2
        t = f"l{l}b"
        sv = saved[l]
        wup, wdown = w["ffn_w_up"][l], w["ffn_w_down"][l]
        dr2, dg2, db2 = _ln_bwd(dya, dyb, ca, cb, sv["r2"], w["ln_g"][l, 1], t + "_ln2")
        da = _mm([(dr2, 0, wdown, 0, D_MODEL)], FFN_DIM, BF16, t + "_down", trans_b=True)
        grads["ffn_w_down"][l] = _mm_tn(sv["a"], 0, FFN_DIM, dr2, 0, D_MODEL, t + "_wdown")
        dhg, dhu, dfw8, dfb = _ffn_gate_bwd(sv["hu"], da, sv["fw8"], w["ffn_conv_b"][l], t + "_gate")
        grads["ffn_conv_w"][l] = dfw8[:3]
        grads["ffn_conv_b"][l] = dfb
        dx1 = _mm([(dhg, 0, wup, 0, FFN_DIM), (dhu, 0, wup, 1, FFN_DIM)], D_MODEL, F32, t + "_up", trans_b=True)
        grads["ffn_w_up"][l] = jnp.concatenate([_mm_tn(sv["x1b"], 0, D_MODEL, dhg, 0, FFN_DIM, t + "_wup_g"),
                                                _mm_tn(sv["x1b"], 0, D_MODEL, dhu, 0, FFN_DIM, t + "_wup_u")], axis=1)
        dr1, dg1, db1 = _ln_bwd(dr2, dx1, ALPHA, 1.0, sv["r1"], w["ln_g"][l, 0], t + "_ln1")
        grads["ln_g"][l] = jnp.stack([dg1, dg2])
        grads["ln_b"][l] = jnp.stack([db1, db2])
        win, wout, h = sv["win"], sv["wout"], sv["h"]
        dycat = _mm([(dr1, 0, wout, 0, D_MODEL)], D_MODEL, F32, t + "_out", trans_b=True)
        dwout = jnp.concatenate([_mm_tn(sv["ya"], 0, 512, dr1, 0, D_MODEL, t + "_wout_a"),
                                 _mm_tn(sv["yb"], 0, 512, dr1, 0, D_MODEL, t + "_wout_b")], axis=0)
        if l % 2 == 0:
            dz, dxbcc, ddt, ddtb, dal, ddsk, dnw = _ssd_bwd(dycat, 0, h, sv["xbcc"], sv["ssts"], *sv["sp"], t + "_ssd")
            dxbc, dcw, dcb = _dwconv_bwd(dxbcc, h, AB_XBC // CONV_CB, SSD_CONV_DIM, sv["w8"], 4, t + "_conv")
            dq, df, di, dg, dlb, dhnw = _hg_bwd(dycat, 1, h, sv["hsts"], *sv["hp"], t + "_hg")
            grads["ab_w_out"][j] = dwout
            grads["ssd_conv_w"][j], grads["ssd_conv_b"][j] = dcw[:4], dcb
            grads["ssd_dt_bias"][j], grads["ssd_a_log"][j], grads["ssd_d"][j] = ddtb[0, :8], dal[0, :8], ddsk[0, :8]
            grads["ssd_norm_w"][j], grads["hg_norm_w"][j] = dnw[0], dhnw[0]
            grads["hg_lower"][j] = dlb[0]
            pieces = [(dz, 0, AB_Z, 512), (dq, 0, AB_HQ, 512), (df, 0, AB_HF, 512), (di, 0, AB_HI, 512), (dg, 0, AB_HG, 512),
                      (dxbc, 0, AB_XBC, 256), (dxbc, 1, AB_XBC + 256, 256), (dxbc, 2, AB_XBC + 512, 256), (ddt, 0, AB_DT, LANES)]
            dwin = [_mm_tn(sv["xb"], 0, D_MODEL, p, 0, p.shape[1], t + f"_win{i}")
                    for i, p in enumerate((dz, dq, df, di, dg, dxbc, ddt))]
            grads["ab_w_in"][j] = _ab_in_to_ref(jnp.concatenate(dwin, axis=1))
        else:
            dq, dk, dv, dsr = _swa_bwd(dycat, 0, h, sv["srow"], t + "_swa")
            dxc, dgate, dwa, dwx, dba, dbx, dlam = _rg_bwd(dycat, 1, sv["xc"], h, sv["hs"], *sv["rp"], t + "_rg")
            dxr, dcw, dcb = _dwconv_bwd(dxc, h, CD_XR // CONV_CB, RG_WIDTH, sv["w8"], 4, t + "_conv")
            grads["cd_w_out"][j] = jnp.concatenate([_pairs_to_ref(dwout[:512], 0), dwout[512:]], axis=0)
            grads["swa_sinks"][j] = dsr[0, :8]
            grads["rg_conv_w"][j], grads["rg_conv_b"][j] = dcw[:4], dcb
            grads["rg_wa"][j], grads["rg_wx"][j] = _diag_blocks(dwa), _diag_blocks(dwx)
            grads["rg_ba"][j], grads["rg_bx"][j], grads["rg_lambda"][j] = dba[0], dbx[0], dlam[0]
            pieces = [(dq, 0, CD_Q, 512), (dgate, 0, CD_GATE, 512), (dxr, 0, CD_XR, 512), (dk, 0, CD_K, LANES), (dv, 0, CD_V, LANES)]
            dwin = [_mm_tn(sv["xb"], 0, D_MODEL, p, 0, p.shape[1], t + f"_win{i}")
                    for i, p in enumerate((dq, dgate, dxr, dk, dv))]
            grads["cd_w_in"][j] = _cd_in_to_ref(jnp.concatenate(dwin, axis=1))
        dx0 = _mm([(p, pcb, win, off // k, k) for p, pcb, off, k in pieces], D_MODEL, F32, t + "_in", trans_b=True)
        dya, dyb, ca, cb = dr1, dx0, ALPHA, 1.0
    grad_x = _axpby(dya, dyb, ca, cb, "grad_x")
    out = {k: (v if k in BIG else jnp.stack(v)) for k, v in grads.items()}
    (out["hg_lower"],) = lb_pull(out["hg_lower"])
    return loss, grad_x, out


MESH = pl.DeviceIdType.MESH
ANY = pl.BlockSpec(memory_space=pl.ANY)
PACK_ROWS = 2048


def _place():
    x, y, c = lax.axis_index("x"), lax.axis_index("y"), lax.axis_index("c")
    return x, y, c, [(1 - x, y), (x, 1 - y), (1 - x, 1 - y)]


def _rcopy(src, dst, send_sems, recv_sems, k, to):
    return pltpu.make_async_remote_copy(src_ref=src, dst_ref=dst, send_sem=send_sems.at[k], recv_sem=recv_sems.at[k],
                                        device_id=to, device_id_type=MESH)


def _gather_chips(wp, name):
    def body(w_ref, out_ref, send_sems, recv_sems):
        x, y, c, chips = _place()
        me, sib = 2 * x + y, (x, y, 1 - c)
        first = [_rcopy(w_ref.at[c], out_ref.at[me, c], send_sems, recv_sems, k, (cx, cy, c)) for k, (cx, cy) in enumerate(chips)]
        for cp in first:
            cp.start()
        passed = []
        for k, (cx, cy) in enumerate(chips):
            blk = out_ref.at[2 * cx + cy, c]
            _rcopy(blk, blk, send_sems, recv_sems, k, (cx, cy, c)).wait_recv()
            passed.append(_rcopy(blk, blk, send_sems, recv_sems, 3 + k, sib))
            passed[-1].start()
        for k, (cx, cy) in enumerate(chips):
            blk = out_ref.at[2 * cx + cy, 1 - c]
            _rcopy(blk, blk, send_sems, recv_sems, 3 + k, sib).wait_recv()
        for cp in first + passed:
            cp.wait_send()

    out = pl.pallas_call(
        body, name=name, in_specs=[ANY], out_specs=ANY, out_shape=jax.ShapeDtypeStruct((4,) + wp.shape, wp.dtype),
        scratch_shapes=[pltpu.SemaphoreType.DMA((6,)), pltpu.SemaphoreType.DMA((6,))],
    )(wp)
    return lax.dynamic_update_slice(out, wp[None], (2 * lax.axis_index("x") + lax.axis_index("y"), 0, 0, 0))


def _swap_halves(g, name):
    def body(g_ref, x_ref, send_sems, recv_sems):
        x, y, c, _ = _place()
        cp = _rcopy(g_ref.at[:, 1 - c], x_ref, send_sems, recv_sems, 0, (x, y, 1 - c))
        cp.start()
        cp.wait()

    return pl.pallas_call(
        body, name=name, in_specs=[ANY], out_specs=ANY,
        out_shape=jax.ShapeDtypeStruct((g.shape[0],) + g.shape[2:], g.dtype),
        scratch_shapes=[pltpu.SemaphoreType.DMA((1,)), pltpu.SemaphoreType.DMA((1,))],
    )(g)


def _scatter_chips(hb, name):
    def body(h_ref, y_ref, send_sems, recv_sems):
        x, y, c, chips = _place()
        me = 2 * x + y
        sends = [_rcopy(h_ref.at[2 * cx + cy], y_ref.at[me], send_sems, recv_sems, k, (cx, cy, c)) for k, (cx, cy) in enumerate(chips)]
        for cp in sends:
            cp.start()
        for k, (cx, cy) in enumerate(chips):
            blk = y_ref.at[2 * cx + cy]
            _rcopy(blk, blk, send_sems, recv_sems, k, (cx, cy, c)).wait_recv()
        for cp in sends:
            cp.wait_send()

    out = pl.pallas_call(
        body, name=name, in_specs=[ANY], out_specs=ANY, out_shape=jax.ShapeDtypeStruct(hb.shape, hb.dtype),
        scratch_shapes=[pltpu.SemaphoreType.DMA((3,)), pltpu.SemaphoreType.DMA((3,))],
    )(hb)
    me = 2 * lax.axis_index("x") + lax.axis_index("y")
    return lax.dynamic_update_slice(out, lax.dynamic_slice_in_dim(hb, me, 1, axis=0), (me, 0, 0))


def _share_half(f, name):
    def body(f_ref, out_ref, send_sems, recv_sems):
        x, y, c, _ = _place()
        sib = (x, y, 1 - c)
        cp = _rcopy(f_ref, out_ref.at[c], send_sems, recv_sems, 0, sib)
        cp.start()
        _rcopy(f_ref, out_ref.at[1 - c], send_sems, recv_sems, 0, sib).wait_recv()
        cp.wait_send()

    out = pl.pallas_call(
        body, name=name, in_specs=[ANY], out_specs=ANY, out_shape=jax.ShapeDtypeStruct((2,) + f.shape, f.dtype),
        scratch_shapes=[pltpu.SemaphoreType.DMA((1,)), pltpu.SemaphoreType.DMA((1,))],
    )(f)
    return lax.dynamic_update_slice(out, f[None], (lax.axis_index("c"), 0, 0))


def _gather_devices(v, name):
    m_per, n = v.shape

    def body(x_ref, out_ref, send_sems, recv_sems, local_sem):
        x, y, c, chips = _place()
        me, sib = (x, y, c), (x, y, 1 - c)

        def rows(px, py, pc):
            return out_ref.at[pl.ds((4 * px + 2 * py + pc) * m_per, m_per), :]

        def copy(k, block, to, src=None):
            return _rcopy(rows(*block) if src is None else src, rows(*block), send_sems, recv_sems, k, to)

        mine = pltpu.make_async_copy(x_ref, rows(*me), local_sem)
        mine.start()
        first = [copy(0, me, sib, src=x_ref)]
        first += [copy(1 + j, me, (*chip, c), src=x_ref) for j, chip in enumerate(chips)]
        for cp in first:
            cp.start()
        passed = [copy(4 + j, (*chip, c), sib) for j, chip in enumerate(chips)]
        for j, chip in enumerate(chips):
            copy(1 + j, (*chip, c), me).wait_recv()
            passed[j].start()
        copy(0, sib, me).wait_recv()
        for j, chip in enumerate(chips):
            copy(4 + j, (*chip, 1 - c), me).wait_recv()
        for cp in first + passed:
            cp.wait_send()
        mine.wait()

    return pl.pallas_call(
        body, name=name, out_shape=jax.ShapeDtypeStruct((8 * m_per, n), v.dtype),
        in_specs=[pl.BlockSpec(memory_space=pltpu.VMEM)], out_specs=pl.BlockSpec(memory_space=pltpu.VMEM),
        scratch_shapes=[pltpu.SemaphoreType.DMA((7,)), pltpu.SemaphoreType.DMA((7,)), pltpu.SemaphoreType.DMA(())],
        compiler_params=_cp(),
    )(v)


def _add_half(g, xr, c, name, tr=2048):
    rh = g.shape[2]

    def kern(c_ref, g_ref, x_ref, o_ref):
        o_ref[...] = (g_ref[0] + x_ref[...]).astype(BF16)

    return pl.pallas_call(
        kern, name=name,
        grid_spec=pltpu.PrefetchScalarGridSpec(
            num_scalar_prefetch=1, grid=(4, rh // tr),
            in_specs=[pl.BlockSpec((1, 1, tr, LANES), lambda k, i, c_ref: (k, c_ref[0], i, 0)),
                      pl.BlockSpec((1, tr, LANES), lambda k, i, c_ref: (k, i, 0))],
            out_specs=pl.BlockSpec((1, tr, LANES), lambda k, i, c_ref: (k, i, 0))),
        out_shape=jax.ShapeDtypeStruct(xr.shape, BF16), compiler_params=_cp(("parallel", "parallel")),
    )(c.reshape(1).astype(jnp.int32), g, xr)


def _sum_blocks(y, name, tr=2048):
    n, r, _ = y.shape
    tr = tr if r % tr == 0 else r

    def kern(y_ref, o_ref):
        acc = y_ref[0].astype(F32)
        for k in range(1, n):
            acc = acc + y_ref[k].astype(F32)
        o_ref[...] = acc

    return pl.pallas_call(
        kern, name=name, grid=(r // tr,), in_specs=[pl.BlockSpec((n, tr, LANES), lambda i: (0, i, 0))],
        out_specs=pl.BlockSpec((tr, LANES), lambda i: (i, 0)), out_shape=jax.ShapeDtypeStruct((r, LANES), F32),
        compiler_params=_cp(("parallel",)),
    )(y)


def _adamw(w, g, m, v, name, tr=256):
    r, n = w.shape
    tr = tr if r % tr == 0 else r

    def kern(w_ref, g_ref, m_ref, v_ref, d_ref, nm_ref, nv_ref):
        gg = g_ref[...]
        nm = ADAM_B1 * m_ref[...] + (1.0 - ADAM_B1) * gg
        nv = ADAM_B2 * v_ref[...] + (1.0 - ADAM_B2) * (gg * gg)
        m_hat = nm / (1.0 - ADAM_B1 ** ADAM_STEP)
        v_hat = nv / (1.0 - ADAM_B2 ** ADAM_STEP)
        d_ref[...] = -ADAM_LR * (m_hat / (jnp.sqrt(v_hat) + ADAM_EPS) + ADAM_WD * w_ref[...])
        nm_ref[...] = nm
        nv_ref[...] = nv

    blk = pl.BlockSpec((tr, n), lambda i: (i, 0))
    return pl.pallas_call(
        kern, name=name, grid=(r // tr,), in_specs=[blk] * 4, out_specs=[blk] * 3,
        out_shape=[jax.ShapeDtypeStruct((r, n), F32)] * 3, compiler_params=_cp(("parallel",)),
    )(w, g, m, v)


def _rows_of(n):
    return -(-n // (LANES * SUBLANES)) * SUBLANES


def _pack(arrs, lead, dtype, mult):
    ls = arrs[0].shape[:lead]
    blocks, total = [], 0
    for a in arrs:
        f = a.astype(dtype).reshape(ls + (-1,))
        n = f.shape[-1]
        rows = _rows_of(n)
        if rows * LANES != n:
            f = jnp.concatenate([f, jnp.zeros(ls + (rows * LANES - n,), dtype)], axis=-1)
        blocks.append(f.reshape(ls + (rows, LANES)))
        total += rows
    if total % mult:
        blocks.append(jnp.zeros(ls + (mult - total % mult, LANES), dtype))
    return jnp.concatenate(blocks, axis=lead)


def _unpack(buf, lead, shapes):
    ls = buf.shape[:lead]
    out, off = [], 0
    for sh in shapes:
        n = int(np.prod(sh))
        rows = _rows_of(n)
        flat = lax.slice_in_dim(buf, off, off + rows, axis=lead).reshape(ls + (rows * LANES,))
        if rows * LANES != n:
            flat = lax.slice_in_dim(flat, 0, n, axis=lead)
        out.append(flat.reshape(ls + tuple(sh)))
        off += rows
    return out


BIG = ("ab_w_in", "ab_w_out", "cd_w_in", "cd_w_out", "ffn_w_up", "ffn_w_down")
BIG_COLS = ("ab_w_in", "cd_w_in", "ffn_w_up")
SMALL_SHARDED = ("ssd_conv_w", "rg_conv_w", "rg_conv_b", "rg_ba", "rg_bx", "rg_lambda", "ffn_conv_w", "ln_g", "ln_b")
WEIGHTS = ("ab_w_in", "ssd_conv_w", "ssd_conv_b", "ssd_dt_bias", "ssd_a_log", "ssd_d", "ssd_norm_w", "hg_lower", "hg_norm_w",
           "ab_w_out", "cd_w_in", "swa_sinks", "rg_conv_w", "rg_conv_b", "rg_wa", "rg_ba", "rg_wx", "rg_bx", "rg_lambda",
           "cd_w_out", "ffn_w_up", "ffn_conv_w", "ffn_conv_b", "ffn_w_down", "ln_g", "ln_b")
SMALL = tuple(n for n in WEIGHTS if n not in BIG)


def _full_from_shards(name, g):
    if name in BIG_COLS or name in SMALL_SHARDED:
        return jnp.concatenate([g[k] for k in range(4)], axis=-1)
    return jnp.concatenate([g[k] for k in range(4)], axis=1)


def _layer_shards(name, f):
    if name in BIG_COLS:
        k, n = f.shape
        return jnp.transpose(f.reshape(k, 4, n // 4), (1, 0, 2)).reshape(4, -1)
    return f.reshape(4, -1)


def kernel(x, ab_w_in, ssd_conv_w, ssd_conv_b, ssd_dt_bias, ssd_a_log, ssd_d, ssd_norm_w, hg_lower, hg_norm_w, ab_w_out, cd_w_in, swa_sinks, rg_conv_w, rg_conv_b, rg_wa, rg_ba, rg_wx, rg_bx, rg_lambda, cd_w_out, ffn_w_up, ffn_conv_w, ffn_conv_b, ffn_w_down, ln_g, ln_b, loss_target, m_ab_w_in, m_ssd_conv_w, m_ssd_conv_b, m_ssd_dt_bias, m_ssd_a_log, m_ssd_d, m_ssd_norm_w, m_hg_lower, m_hg_norm_w, m_ab_w_out, m_cd_w_in, m_swa_sinks, m_rg_conv_w, m_rg_conv_b, m_rg_wa, m_rg_ba, m_rg_wx, m_rg_bx, m_rg_lambda, m_cd_w_out, m_ffn_w_up, m_ffn_conv_w, m_ffn_conv_b, m_ffn_w_down, m_ln_g, m_ln_b, v_ab_w_in, v_ssd_conv_w, v_ssd_conv_b, v_ssd_dt_bias, v_ssd_a_log, v_ssd_d, v_ssd_norm_w, v_hg_lower, v_hg_norm_w, v_ab_w_out, v_cd_w_in, v_swa_sinks, v_rg_conv_w, v_rg_conv_b, v_rg_wa, v_rg_ba, v_rg_wx, v_rg_bx, v_rg_lambda, v_cd_w_out, v_ffn_w_up, v_ffn_conv_w, v_ffn_conv_b, v_ffn_w_down, v_ln_g, v_ln_b):
    args = locals()
    w = {n: args[n] for n in WEIGHTS}
    mom = {n: args["m_" + n] for n in WEIGHTS}
    var = {n: args["v_" + n] for n in WEIGHTS}
    cx, cy, cc = lax.axis_index("x"), lax.axis_index("y"), lax.axis_index("c")
    chip = 2 * cx + cy

    big_shapes = [w[n].shape for n in BIG]
    wp = _pack([w[n] for n in BIG], 0, BF16, 2 * PACK_ROWS)
    rh = wp.shape[0] // 2
    gathered = _gather_chips(wp.reshape(2, rh, LANES), "gather_weights").reshape(4, 2 * rh, LANES)
    full = {n: _full_from_shards(n, g) for n, g in zip(BIG, _unpack(gathered, 1, big_shapes))}
    sp = _pack([w[n] for n in SMALL_SHARDED], 0, F32, SUBLANES)
    sg = _gather_devices(sp, "gather_small").reshape(4, 2, sp.shape[0], LANES)[:, 0]
    for n, g in zip(SMALL_SHARDED, _unpack(sg, 1, [w[n].shape for n in SMALL_SHARDED])):
        full[n] = _full_from_shards(n, g)
    for n in SMALL:
        full.setdefault(n, w[n])

    loss, grad_x, grads = _local_step(x[0], loss_target[0], full)
    loss = lax.psum(loss, ("x", "y", "c"))

    g4 = _pack([jnp.concatenate([_layer_shards(n, f) for f in grads[n]], axis=1) for n in BIG], 1, F32,
               2 * PACK_ROWS).reshape(4, 2, rh, LANES)
    hb = _add_half(g4, _swap_halves(g4, "swap_halves"), cc, "add_halves")
    own = _sum_blocks(_scatter_chips(hb, "scatter_chips"), "sum_chips")
    gbig = _unpack(_share_half(own, "share_half").reshape(2 * rh, LANES), 0, big_shapes)
    gout = dict(zip(BIG, gbig))

    small_shapes = [grads[n].shape for n in SMALL]
    gs = _pack([grads[n] for n in SMALL], 0, F32, SUBLANES)
    gsum = _sum_blocks(_gather_devices(gs, "gather_small_grads").reshape(8, gs.shape[0], LANES), "sum_small")
    for n, g in zip(SMALL, _unpack(gsum, 0, small_shapes)):
        if n in SMALL_SHARDED:
            width = w[n].shape[-1]
            g = lax.dynamic_slice_in_dim(g, chip * width, width, axis=g.ndim - 1)
        gout[n] = g

    delta, new_m, new_v = {}, {}, {}
    for n in BIG:
        sh = w[n].shape
        two = lambda a: a.reshape(-1, sh[-1])
        d, nm, nv = _adamw(two(w[n]), two(gout[n]), two(mom[n]), two(var[n]), "adamw_" + n)
        delta[n], new_m[n], new_v[n] = d.reshape(sh), nm.reshape(sh), nv.reshape(sh)
    local_shapes = [w[n].shape for n in SMALL]
    packs = [_pack([d[n] for n in SMALL], 0, F32, SUBLANES) for d in (w, gout, mom, var)]
    for dst, buf in zip((delta, new_m, new_v), _adamw(*packs, "adamw_small")):
        dst.update(zip(SMALL, _unpack(buf, 0, local_shapes)))

    return (loss, grad_x[None], *[gout[n] for n in WEIGHTS], *[delta[n] for n in WEIGHTS],
            *[new_m[n] for n in WEIGHTS], *[new_v[n] for n in WEIGHTS])
```

```python
import functools
import math

import numpy as np
import jax
import jax.numpy as jnp
from jax import lax
from jax.experimental import pallas as pl
from jax.experimental.pallas import tpu as pltpu

F32 = jnp.float32
BF16 = jnp.bfloat16
HI = lax.Precision.HIGHEST

D_MODEL = 1024
DEPTH = 4
SSD_HEADS = 8
SSD_D_INNER = 512
SSD_CONV_DIM = 768
SSD_CHUNK = 128
HG_CHUNK = 64
HG_HEADS = 4
SWA_BLOCK = 128
RG_WIDTH = 512
FFN_DIM = 2816
LN_EPS = 1e-5
RMS_EPS = 1e-6
MASK_VALUE = -1e9
ALPHA = (2 * DEPTH) ** 0.25
RG_C = 8.0
ADAM_LR, ADAM_B1, ADAM_B2, ADAM_EPS, ADAM_WD, ADAM_STEP = 0.001, 0.9, 0.999, 1e-08, 0.01, 10

LANES = 128
SUBLANES = 8
HALO = 16
CONV_CB = 256
VMEM_LIMIT = 56 * 1024 * 1024

AB_Z, AB_HQ, AB_HF, AB_HI, AB_HG, AB_XBC, AB_DT, AB_COLS = 0, 512, 1024, 1536, 2048, 2560, 3328, 3456
CD_Q, CD_GATE, CD_XR, CD_K, CD_V, CD_COLS = 0, 512, 1024, 1536, 1664, 1792


def _cp(sem=None):
    kw = dict(vmem_limit_bytes=VMEM_LIMIT)
    if sem is not None:
        kw["dimension_semantics"] = sem
    return pltpu.CompilerParams(**kw)


def _dot(a, b, dims=(((1,), (0,)), ((), ())), precision=None):
    return lax.dot_general(a, b, dims, precision=precision, preferred_element_type=F32)


NN = (((1,), (0,)), ((), ()))
NT = (((1,), (1,)), ((), ()))
TN = (((0,), (0,)), ((), ()))


@functools.partial(jax.custom_vjp, nondiff_argnums=(2,))
def _bdot(a, b, dims=NN):
    return _dot(a.astype(BF16), b.astype(BF16), dims)


def _bdot_fwd(a, b, dims):
    return _bdot(a, b, dims), (a, b)


def _bdot_bwd(dims, res, ct):
    a, b = res
    ab, bb, cb = a.astype(BF16), b.astype(BF16), ct.astype(BF16)
    if dims == NN:
        da, db = _dot(cb, bb, NT), _dot(ab, cb, TN)
    elif dims == NT:
        da, db = _dot(cb, bb, NN), _dot(cb, ab, TN)
    else:
        da, db = _dot(bb, cb, NT), _dot(ab, cb, NN)
    return da.astype(a.dtype), db.astype(b.dtype)


_bdot.defvjp(_bdot_fwd, _bdot_bwd)


def _hdot(a, b, dims=NN):
    return _dot(a, b, dims, precision=HI)


WIDE_TILES = (1408, 1152, 1024, 896, 768, 512, 384, 256, 128)
MM_BLOCK_BYTES = 6 * 1024 * 1024


def _pick(n, cands):
    for c in cands:
        if n % c == 0:
            return c
    return n


def _mm(pairs, n_out, out_dtype, name, trans_b=False, tm=512):
    m = pairs[0][0].shape[0]
    a_row_bytes = sum(p[4] * p[0].dtype.itemsize for p in pairs)
    b_col_bytes = sum(p[4] * p[2].dtype.itemsize for p in pairs)
    tm = min(2 * tm if 2 * tm * a_row_bytes <= 2 * MM_BLOCK_BYTES else tm, m)
    tn = _pick(n_out, [c for c in WIDE_TILES if c * b_col_bytes <= MM_BLOCK_BYTES])
    n = len(pairs)

    def kern(*refs):
        acc = None
        for i in range(n):
            p = _bdot(refs[i][...], refs[n + i][...], NT if trans_b else NN)
            acc = p if acc is None else acc + p
        refs[2 * n][...] = acc.astype(out_dtype)

    in_specs = []
    for a, acb, b, bcb, k in pairs:
        in_specs.append(pl.BlockSpec((tm, k), functools.partial(lambda i, j, c: (i, c), c=acb)))
    for a, acb, b, bcb, k in pairs:
        if trans_b:
            in_specs.append(pl.BlockSpec((tn, k), functools.partial(lambda i, j, c: (j, c), c=bcb)))
        else:
            in_specs.append(pl.BlockSpec((k, tn), functools.partial(lambda i, j, c: (c, j), c=bcb)))
    return pl.pallas_call(
        kern, name=name, grid=(m // tm, n_out // tn), in_specs=in_specs,
        out_specs=pl.BlockSpec((tm, tn), lambda i, j: (i, j)),
        out_shape=jax.ShapeDtypeStruct((m, n_out), out_dtype),
        compiler_params=_cp(("parallel", "arbitrary")),
    )(*[p[0] for p in pairs], *[p[2] for p in pairs])


def _mm_tn(a, a_cb, ka, g, g_cb, ng, name, tm=2048):
    m = a.shape[0]
    tm = min(tm, m)
    tk = _pick(ka, (1024, 1408, 512, 256, 128))
    tn = _pick(ng, WIDE_TILES)
    nm = m // tm

    def kern(a_ref, g_ref, o_ref):
        p = _bdot(a_ref[...], g_ref[...], TN)

        @pl.when(pl.program_id(2) == 0)
        def _():
            o_ref[...] = p

        @pl.when(pl.program_id(2) > 0)
        def _():
            o_ref[...] += p

    ak, gk = ka // tk, ng // tn
    return pl.pallas_call(
        kern, name=name, grid=(ak, gk, nm),
        in_specs=[pl.BlockSpec((tm, tk), lambda i, j, r: (r, a_cb * ak + i)),
                  pl.BlockSpec((tm, tn), lambda i, j, r: (r, g_cb * gk + j))],
        out_specs=pl.BlockSpec((tk, tn), lambda i, j, r: (i, j)),
        out_shape=jax.ShapeDtypeStruct((ka, ng), F32),
        compiler_params=_cp(("parallel", "parallel", "arbitrary")),
    )(a, g)


def _mm_ln(pairs, x, g, b, name, tm=512):
    m, d = x.shape
    tm = min(tm, m)
    n = len(pairs)

    def kern(*refs):
        x_ref, g_ref, b_ref, y_ref, yb_ref, r_ref = refs[2 * n:]
        r = ALPHA * x_ref[...]
        for i in range(n):
            r = r + _bdot(refs[i][...], refs[n + i][...])
        mu = jnp.mean(r, -1, keepdims=True)
        xc = r - mu
        var = jnp.mean(xc * xc, -1, keepdims=True)
        y = xc * lax.rsqrt(var + LN_EPS) * g_ref[...] + b_ref[...]
        y_ref[...] = y
        yb_ref[...] = y.astype(BF16)
        r_ref[...] = r

    row = pl.BlockSpec((tm, d), lambda i: (i, 0))
    vec = pl.BlockSpec((1, d), lambda i: (0, 0))
    in_specs = [pl.BlockSpec((tm, k), functools.partial(lambda i, c: (i, c), c=acb)) for _, acb, _, _, k in pairs]
    in_specs += [pl.BlockSpec((k, d), functools.partial(lambda i, c: (c, 0), c=bcb)) for _, _, _, bcb, k in pairs]
    return pl.pallas_call(
        kern, name=name, grid=(m // tm,), in_specs=in_specs + [row, vec, vec], out_specs=[row, row, row],
        out_shape=[jax.ShapeDtypeStruct((m, d), F32), jax.ShapeDtypeStruct((m, d), BF16), jax.ShapeDtypeStruct((m, d), F32)],
        compiler_params=_cp(("parallel",)),
    )(*[p[0] for p in pairs], *[p[2] for p in pairs], x, g.reshape(1, d), b.reshape(1, d))


def _ln_bwd(dya, dyb, ca, cb, r, g, name, tm=256):
    s, d = r.shape
    tm = min(tm, s)

    def kern(a_ref, b_ref, r_ref, g_ref, dr_ref, dg_ref, db_ref):
        dy = ca * a_ref[...] + cb * b_ref[...]
        rr = r_ref[...]
        mu = jnp.mean(rr, -1, keepdims=True)
        xc = rr - mu
        var = jnp.mean(xc * xc, -1, keepdims=True)
        rstd = lax.rsqrt(var + LN_EPS)
        xhat = xc * rstd
        dxh = dy * g_ref[...]
        dr_ref[...] = rstd * (dxh - jnp.mean(dxh, -1, keepdims=True) - xhat * jnp.mean(dxh * xhat, -1, keepdims=True))
        dg = jnp.sum(dy * xhat, 0, keepdims=True)
        db = jnp.sum(dy, 0, keepdims=True)

        @pl.when(pl.program_id(0) == 0)
        def _():
            dg_ref[...] = dg
            db_ref[...] = db

        @pl.when(pl.program_id(0) > 0)
        def _():
            dg_ref[...] += dg
            db_ref[...] += db

    row = pl.BlockSpec((tm, d), lambda i: (i, 0))
    vec = pl.BlockSpec((1, d), lambda i: (0, 0))
    dr, dg, db = pl.pallas_call(
        kern, name=name, grid=(s // tm,), in_specs=[row, row, row, vec], out_specs=[row, vec, vec],
        out_shape=[jax.ShapeDtypeStruct((s, d), F32), jax.ShapeDtypeStruct((1, d), F32), jax.ShapeDtypeStruct((1, d), F32)],
        compiler_params=_cp(("arbitrary",)),
    )(dya, dyb, r, g.reshape(1, d))
    return dr, dg[0], db[0]


def _sqerr(y, t, name, tm=256):
    s, d = y.shape
    tm = min(tm, s)

    def kern(y_ref, t_ref, o_ref):
        e = y_ref[...] - t_ref[...]
        p = jnp.sum(e * e, 0, keepdims=True)

        @pl.when(pl.program_id(0) == 0)
        def _():
            o_ref[...] = p

        @pl.when(pl.program_id(0) > 0)
        def _():
            o_ref[...] += p

    row = pl.BlockSpec((tm, d), lambda i: (i, 0))
    return pl.pallas_call(
        kern, name=name, grid=(s // tm,), in_specs=[row, row], out_specs=pl.BlockSpec((1, d), lambda i: (0, 0)),
        out_shape=jax.ShapeDtypeStruct((1, d), F32), compiler_params=_cp(("arbitrary",)),
    )(y, t)


def _axpby(a, b, ca, cb, name, tm=256):
    s, d = a.shape
    tm = min(tm, s)

    def kern(a_ref, b_ref, o_ref):
        o_ref[...] = ca * a_ref[...] + cb * b_ref[...]

    row = pl.BlockSpec((tm, d), lambda i: (i, 0))
    return pl.pallas_call(
        kern, name=name, grid=(s // tm,), in_specs=[row, row], out_specs=row,
        out_shape=jax.ShapeDtypeStruct((s, d), F32), compiler_params=_cp(("parallel",)),
    )(a, b)


def _shift_down(xx, s, n):
    if s == 0:
        return xx[HALO:HALO + n]
    return pltpu.roll(xx, s, axis=0)[HALO:HALO + n]


def _shift_up(yy, s, n):
    if s == 0:
        return yy[0:n]
    return pltpu.roll(yy, yy.shape[0] - s, axis=0)[0:n]


def _prev_halo(tm):
    return lambda i: jnp.maximum(i * (tm // HALO) - 1, 0)


def _dwconv_fwd(x, x_cb0, c, w8, b, k, name, tm=512):
    s = x.shape[0]
    tm = min(tm, s)
    ph = _prev_halo(tm)

    def kern(x_ref, h_ref, w_ref, b_ref, y_ref):
        halo = jnp.where(pl.program_id(0) == 0, 0.0, h_ref[...].astype(F32))
        xx = jnp.concatenate([halo, x_ref[...].astype(F32)], axis=0)
        w = w_ref[...]
        acc = b_ref[...] + w[k - 1:k] * xx[HALO:]
        for j in range(k - 1):
            acc = acc + w[j:j + 1] * _shift_down(xx, k - 1 - j, tm)
        y_ref[...] = acc

    return pl.pallas_call(
        kern, name=name, grid=(s // tm, c // CONV_CB),
        in_specs=[pl.BlockSpec((tm, CONV_CB), lambda i, j: (i, x_cb0 + j)),
                  pl.BlockSpec((HALO, CONV_CB), lambda i, j: (ph(i), x_cb0 + j)),
                  pl.BlockSpec((SUBLANES, CONV_CB), lambda i, j: (0, j)),
                  pl.BlockSpec((1, CONV_CB), lambda i, j: (0, j))],
        out_specs=pl.BlockSpec((tm, CONV_CB), lambda i, j: (i, j)),
        out_shape=jax.ShapeDtypeStruct((s, c), F32), compiler_params=_cp(("parallel", "parallel")),
    )(x, x, w8, b.reshape(1, c))


def _dwconv_bwd(dy, x, x_cb0, c, w8, k, name, tm=512):
    s = x.shape[0]
    tm = min(tm, s)
    nt = s // tm
    ph = _prev_halo(tm)
    nh = lambda i: jnp.minimum((i + 1) * (tm // HALO), s // HALO - 1)

    def kern(dy_ref, dyn_ref, x_ref, h_ref, w_ref, dx_ref, dw_ref, db_ref):
        i = pl.program_id(1)
        halo = jnp.where(i == 0, 0.0, h_ref[...].astype(F32))
        xx = jnp.concatenate([halo, x_ref[...].astype(F32)], axis=0)
        dyt = dy_ref[...]
        nxt = jnp.where(i == nt - 1, 0.0, dyn_ref[...])
        dyy = jnp.concatenate([dyt, nxt], axis=0)
        w = w_ref[...]
        dx = w[k - 1:k] * dyt
        rows = [jnp.sum(dyt * _shift_down(xx, k - 1 - j, tm), 0, keepdims=True) for j in range(k)]
        for j in range(k - 1):
            dx = dx + w[j:j + 1] * _shift_up(dyy, k - 1 - j, tm)
        dx_ref[...] = dx.astype(BF16)
        dw = jnp.concatenate(rows + [jnp.zeros((SUBLANES - k, CONV_CB), F32)], axis=0)
        db = jnp.sum(dyt, 0, keepdims=True)

        @pl.when(i == 0)
        def _():
            dw_ref[...] = dw
            db_ref[...] = db

        @pl.when(i > 0)
        def _():
            dw_ref[...] += dw
            db_ref[...] += db

    dx, dw, db = pl.pallas_call(
        kern, name=name, grid=(c // CONV_CB, nt),
        in_specs=[pl.BlockSpec((tm, CONV_CB), lambda j, i: (i, j)),
                  pl.BlockSpec((HALO, CONV_CB), lambda j, i: (nh(i), j)),
                  pl.BlockSpec((tm, CONV_CB), lambda j, i: (i, x_cb0 + j)),
                  pl.BlockSpec((HALO, CONV_CB), lambda j, i: (ph(i), x_cb0 + j)),
                  pl.BlockSpec((SUBLANES, CONV_CB), lambda j, i: (0, j))],
        out_specs=[pl.BlockSpec((tm, CONV_CB), lambda j, i: (i, j)),
                   pl.BlockSpec((SUBLANES, CONV_CB), lambda j, i: (0, j)),
                   pl.BlockSpec((1, CONV_CB), lambda j, i: (0, j))],
        out_shape=[jax.ShapeDtypeStruct((s, c), BF16), jax.ShapeDtypeStruct((SUBLANES, c), F32),
                   jax.ShapeDtypeStruct((1, c), F32)],
        compiler_params=_cp(("parallel", "arbitrary")),
    )(dy, dy, x, x, w8)
    return dx, dw, db[0]


def _silu(x):
    return x * jax.nn.sigmoid(x)


def _ffn_gate_fwd(hu, w8, b, name, tm=512):
    s = hu.shape[0]
    tm = min(tm, s)
    nb = FFN_DIM // CONV_CB
    ph = _prev_halo(tm)
    k = 3

    def conv(x_ref, h_ref, w_ref, b_ref):
        halo = jnp.where(pl.program_id(0) == 0, 0.0, h_ref[...].astype(F32))
        xx = jnp.concatenate([halo, x_ref[...].astype(F32)], axis=0)
        w = w_ref[...]
        acc = b_ref[...] + w[k - 1:k] * xx[HALO:]
        for j in range(k - 1):
            acc = acc + w[j:j + 1] * _shift_down(xx, k - 1 - j, tm)
        return acc

    def kern(g_ref, gh_ref, u_ref, uh_ref, wg_ref, wu_ref, bg_ref, bu_ref, a_ref, hg_ref, hu_ref):
        g = conv(g_ref, gh_ref, wg_ref, bg_ref)
        u = conv(u_ref, uh_ref, wu_ref, bu_ref)
        a_ref[...] = (_silu(g) * u).astype(BF16)
        hg_ref[...] = g.astype(BF16)
        hu_ref[...] = u.astype(BF16)

    main = lambda off: pl.BlockSpec((tm, CONV_CB), lambda i, j: (i, off + j))
    halo = lambda off: pl.BlockSpec((HALO, CONV_CB), lambda i, j: (ph(i), off + j))
    wsp = lambda off: pl.BlockSpec((SUBLANES, CONV_CB), lambda i, j: (0, off + j))
    bsp = lambda off: pl.BlockSpec((1, CONV_CB), lambda i, j: (0, off + j))
    b2 = b.reshape(1, 2 * FFN_DIM)
    half = jax.ShapeDtypeStruct((s, FFN_DIM), BF16)
    return pl.pallas_call(
        kern, name=name, grid=(s // tm, nb),
        in_specs=[main(0), halo(0), main(nb), halo(nb), wsp(0), wsp(nb), bsp(0), bsp(nb)],
        out_specs=[main(0), main(0), main(0)], out_shape=[half, half, half],
        compiler_params=_cp(("parallel", "parallel")),
    )(hu, hu, hu, hu, w8, w8, b2, b2)


def _ffn_gate_bwd(hu, hg, hv, da, w8, name, tm=512):
    s = hu.shape[0]
    tm = min(tm, s)
    nt = s // tm
    nb = FFN_DIM // CONV_CB
    nh = lambda i: jnp.minimum((i + 1) * (tm // HALO), s // HALO - 1)
    k = 3

    def kern(xg_ref, xu_ref, g_ref, gn_ref, u_ref, un_ref, da_ref, dan_ref, wg_ref, wu_ref,
             dg_ref, du_ref, dwg_ref, dwu_ref, dbg_ref, dbu_ref):
        i = pl.program_id(1)

        def ext(x_ref, n_ref):
            return jnp.concatenate([x_ref[...].astype(F32), n_ref[...].astype(F32)], axis=0)

        g, u = ext(g_ref, gn_ref), ext(u_ref, un_ref)
        dae = jnp.concatenate([da_ref[...].astype(F32), jnp.where(i == nt - 1, 0.0, dan_ref[...].astype(F32))], axis=0)
        sg = jax.nn.sigmoid(g)
        dhg = dae * u * (sg * (1.0 + g * (1.0 - sg)))
        dhu = dae * (g * sg)

        def back(dh, x, w):
            taps = [dh[0:tm] if j == k - 1 else _shift_up(dh, k - 1 - j, tm) for j in range(k)]
            dx = w[0:1] * taps[0]
            for j in range(1, k):
                dx = dx + w[j:j + 1] * taps[j]
            rows = [jnp.sum(taps[j] * x, 0, keepdims=True) for j in range(k)]
            dw = jnp.concatenate(rows + [jnp.zeros((SUBLANES - k, CONV_CB), F32)], axis=0)
            return dx, dw, jnp.sum(dh[0:tm], 0, keepdims=True)

        dxg, dwg, dbg = back(dhg, xg_ref[...].astype(F32), wg_ref[...])
        dxu, dwu, dbu = back(dhu, xu_ref[...].astype(F32), wu_ref[...])
        dg_ref[...] = dxg.astype(BF16)
        du_ref[...] = dxu.astype(BF16)

        @pl.when(i == 0)
        def _():
            dwg_ref[...] = dwg
            dwu_ref[...] = dwu
            dbg_ref[...] = dbg
            dbu_ref[...] = dbu

        @pl.when(i > 0)
        def _():
            dwg_ref[...] += dwg
            dwu_ref[...] += dwu
            dbg_ref[...] += dbg
            dbu_ref[...] += dbu

    main = lambda off: pl.BlockSpec((tm, CONV_CB), lambda j, i: (i, off + j))
    nxt = lambda off: pl.BlockSpec((HALO, CONV_CB), lambda j, i: (nh(i), off + j))
    wsp = lambda off: pl.BlockSpec((SUBLANES, CONV_CB), lambda j, i: (0, off + j))
    bsp = lambda off: pl.BlockSpec((1, CONV_CB), lambda j, i: (0, off + j))
    outs = pl.pallas_call(
        kern, name=name, grid=(nb, nt),
        in_specs=[main(0), main(nb), main(0), nxt(0), main(0), nxt(0), main(0), nxt(0), wsp(0), wsp(nb)],
        out_specs=[main(0), main(0), wsp(0), wsp(0), bsp(0), bsp(0)],
        out_shape=[jax.ShapeDtypeStruct((s, FFN_DIM), BF16), jax.ShapeDtypeStruct((s, FFN_DIM), BF16),
                   jax.ShapeDtypeStruct((SUBLANES, FFN_DIM), F32), jax.ShapeDtypeStruct((SUBLANES, FFN_DIM), F32),
                   jax.ShapeDtypeStruct((1, FFN_DIM), F32), jax.ShapeDtypeStruct((1, FFN_DIM), F32)],
        compiler_params=_cp(("parallel", "arbitrary")),
    )(hu, hu, hg, hg, hv, hv, da, da, w8, w8)
    dg, du, dwg, dwu, dbg, dbu = outs
    return dg, du, jnp.concatenate([dwg, dwu], axis=1), jnp.concatenate([dbg[0], dbu[0]])


def _iota(shape, dim):
    return lax.broadcasted_iota(jnp.int32, shape, dim)


def _in_range(idx, start, size):
    return jnp.logical_and(idx >= start, idx < start + size)


@functools.partial(jax.custom_vjp, nondiff_argnums=(2, 3))
def _zdot(a, b, dims, z):
    zz, x = ((a, b) if z == 0 else (b, a))
    zz = zz.astype(BF16)
    xh = x.astype(BF16)
    r1 = x - xh.astype(F32)
    xm = r1.astype(BF16)
    xl = (r1 - xm.astype(F32)).astype(BF16)
    f = (lambda p: _dot(zz, p, dims)) if z == 0 else (lambda p: _dot(p, zz, dims))
    return f(xh) + f(xm) + f(xl)


def _zdot_fwd(a, b, dims, z):
    return _zdot(a, b, dims, z), (a if z == 0 else b)


def _zdot_bwd(dims, z, zz, ct):
    zb = zz.astype(BF16)
    ch = ct.astype(BF16)
    cm = (ct - ch.astype(F32)).astype(BF16)
    if dims == NN and z == 0:
        f = lambda c: _dot(zb, c, TN)
    elif dims == NN and z == 1:
        f = lambda c: _dot(c, zb, NT)
    else:
        f = lambda c: _dot(c, zb, TN)
    dx = f(ch) + f(cm)
    return (jnp.zeros_like(zz), dx) if z == 0 else (dx, jnp.zeros_like(zz))


_zdot.defvjp(_zdot_fwd, _zdot_bwd)


def _expand8(v, e):
    return jnp.sum(_zdot(jnp.broadcast_to(v, (SUBLANES, v.shape[1])), e, NN, 1), 0, keepdims=True) * (1.0 / SUBLANES)


def _ssd_consts():
    tri = np.tril(np.ones((SSD_CHUNK, SSD_CHUNK), np.float32))
    e = np.zeros((LANES, SSD_D_INNER), np.float32)
    for h in range(SSD_HEADS):
        e[h, 64 * h:64 * h + 64] = 1.0
    bd = np.zeros((LANES, SSD_D_INNER), np.float32)
    bd[:64, :256] = 1.0
    bd[64:, 256:] = 1.0
    return jnp.asarray(tri), jnp.asarray(e), jnp.asarray(bd)


def _ssd_chunk(tri, e, bd, z, xs_p, bm_p, cm_p, dtr, st, dtb, alog, dsk, nw):
    n = z.shape[0]
    lane128, lane512 = _iota((1, LANES), 1), _iota((1, SSD_D_INNER), 1)
    sub8 = _iota((SUBLANES, 1), 0)
    tril = _iota((n, 1), 0) >= _iota((1, n), 1)
    head_rows = jnp.where(sub8 == lane128, 1.0, 0.0)
    xs, bm, cm = _silu(xs_p), _silu(bm_p), _silu(cm_p)
    dt = jax.nn.softplus(dtr + dtb)
    da = dt * (-jnp.exp(alog))
    acs = _zdot(tri, da, NN, 0)
    acst = _zdot(head_rows, acs, NT, 0)
    xc = xs * _zdot(dt, e, NN, 1)
    tot = jnp.sum(da, 0, keepdims=True)
    y = _bdot(cm, st) * _zdot(jnp.exp(acs), e, NN, 1)
    st_new = st * _expand8(jnp.exp(tot), e) + bd * _bdot(bm, xc * _zdot(jnp.exp(tot - acs), e, NN, 1), TN)
    for g in range(2):
        cb = _bdot(jnp.where(_in_range(lane128, 64 * g, 64), cm, 0.0), bm, NT)
        for h in range(4 * g, 4 * g + 4):
            col = jnp.sum(jnp.where(lane128 == h, acs, 0.0), 1, keepdims=True)
            row = jnp.sum(jnp.where(sub8 == h, acst, 0.0), 0, keepdims=True)
            dm = jnp.where(tril, jnp.exp(jnp.minimum(col - row, 0.0)), 0.0)
            y = y + _bdot(cb * dm, jnp.where(_in_range(lane512, 64 * h, 64), xc, 0.0))
    y = (y + _expand8(dsk, e) * xs) * _silu(z)
    ysq = y * y
    g0 = lane512 < 256
    ms0 = jnp.sum(jnp.where(g0, ysq, 0.0), 1, keepdims=True) * (1.0 / 256)
    ms1 = jnp.sum(jnp.where(g0, 0.0, ysq), 1, keepdims=True) * (1.0 / 256)
    rs = jnp.where(g0, lax.rsqrt(ms0 + RMS_EPS), lax.rsqrt(ms1 + RMS_EPS))
    return y * rs * nw, st_new


def _ssd_specs(rev, nc):
    ci = (lambda i: nc - 1 - i) if rev else (lambda i: i)
    n = SSD_CHUNK
    act = [pl.BlockSpec((n, 512), lambda i: (ci(i), AB_Z // 512)),
           pl.BlockSpec((n, 512), lambda i: (ci(i), 0)),
           pl.BlockSpec((n, LANES), lambda i: (ci(i), 4)),
           pl.BlockSpec((n, LANES), lambda i: (ci(i), 5)),
           pl.BlockSpec((n, LANES), lambda i: (ci(i), AB_DT // LANES))]
    const = [pl.BlockSpec((n, n), lambda i: (0, 0)), pl.BlockSpec((LANES, 512), lambda i: (0, 0)),
             pl.BlockSpec((LANES, 512), lambda i: (0, 0))]
    par = [pl.BlockSpec((1, LANES), lambda i: (0, 0))] * 3 + [pl.BlockSpec((1, 512), lambda i: (0, 0))]
    return ci, act, const, par


def _ssd_fwd(h, xbcc, dtb, alog, dsk, nw, name):
    s = h.shape[0]
    nc = s // SSD_CHUNK
    ci, act, const, par = _ssd_specs(False, nc)

    def kern(z_ref, xs_ref, bm_ref, cm_ref, dt_ref, tri_ref, e_ref, bd_ref, dtb_ref, al_ref, dsk_ref, nw_ref,
             y_ref, sts_ref, st):
        @pl.when(pl.program_id(0) == 0)
        def _():
            st[...] = jnp.zeros_like(st)

        sts_ref[0] = st[...]
        y, stn = _ssd_chunk(tri_ref[...], e_ref[...], bd_ref[...], z_ref[...], xs_ref[...], bm_ref[...], cm_ref[...],
                            dt_ref[...], st[...], dtb_ref[...], al_ref[...], dsk_ref[...], nw_ref[...])
        y_ref[...] = y.astype(BF16)
        st[...] = stn

    return pl.pallas_call(
        kern, name=name, grid=(nc,), in_specs=act + const + par,
        out_specs=[pl.BlockSpec((SSD_CHUNK, 512), lambda i: (i, 0)), pl.BlockSpec((1, LANES, 512), lambda i: (i, 0, 0))],
        out_shape=[jax.ShapeDtypeStruct((s, 512), BF16), jax.ShapeDtypeStruct((nc, LANES, 512), F32)],
        scratch_shapes=[pltpu.VMEM((LANES, 512), F32)], compiler_params=_cp(("arbitrary",)),
    )(h, xbcc, xbcc, xbcc, h, *_ssd_consts(), dtb, alog, dsk, nw)


def _acc(ref, val, first):
    @pl.when(first)
    def _():
        ref[...] = val

    @pl.when(jnp.logical_not(first))
    def _():
        ref[...] += val


def _ssd_bwd(dy, dy_cb, h, xbcc, sts, dtb, alog, dsk, nw, name):
    s = h.shape[0]
    nc = s // SSD_CHUNK
    n = SSD_CHUNK
    ci, act, const, par = _ssd_specs(True, nc)

    def kern(dy_ref, sts_ref, z_ref, xs_ref, bm_ref, cm_ref, dt_ref, tri_ref, e_ref, bd_ref, dtb_ref, al_ref, dsk_ref,
             nw_ref, dz_ref, dx_ref, ddt_ref, ddtb_ref, dal_ref, ddsk_ref, dnw_ref, dst):
        first = pl.program_id(0) == 0

        @pl.when(first)
        def _():
            dst[...] = jnp.zeros_like(dst)

        f = functools.partial(_ssd_chunk, tri_ref[...], e_ref[...], bd_ref[...])
        _, pull = jax.vjp(f, z_ref[...], xs_ref[...], bm_ref[...], cm_ref[...], dt_ref[...], sts_ref[0],
                          dtb_ref[...], al_ref[...], dsk_ref[...], nw_ref[...])
        dz, dxs, dbm, dcm, ddt, dsti, ddtb, dal, ddsk, dnw = pull((dy_ref[...], dst[...]))
        dz_ref[...] = dz.astype(BF16)
        dx_ref[:, 0:512] = dxs
        dx_ref[:, 512:640] = dbm
        dx_ref[:, 640:768] = dcm
        ddt_ref[...] = ddt.astype(BF16)
        dst[...] = dsti
        _acc(ddtb_ref, ddtb, first)
        _acc(dal_ref, dal, first)
        _acc(ddsk_ref, ddsk, first)
        _acc(dnw_ref, dnw, first)

    vec = pl.BlockSpec((1, LANES), lambda i: (0, 0))
    return pl.pallas_call(
        kern, name=name, grid=(nc,),
        in_specs=[pl.BlockSpec((n, 512), lambda i: (ci(i), dy_cb)), pl.BlockSpec((1, LANES, 512), lambda i: (ci(i), 0, 0))]
        + act + const + par,
        out_specs=[pl.BlockSpec((n, 512), lambda i: (ci(i), 0)), pl.BlockSpec((n, SSD_CONV_DIM), lambda i: (ci(i), 0)),
                   pl.BlockSpec((n, LANES), lambda i: (ci(i), 0)), vec, vec, vec, pl.BlockSpec((1, 512), lambda i: (0, 0))],
        out_shape=[jax.ShapeDtypeStruct((s, 512), BF16), jax.ShapeDtypeStruct((s, SSD_CONV_DIM), F32),
                   jax.ShapeDtypeStruct((s, LANES), BF16), jax.ShapeDtypeStruct((1, LANES), F32),
                   jax.ShapeDtypeStruct((1, LANES), F32), jax.ShapeDtypeStruct((1, LANES), F32),
                   jax.ShapeDtypeStruct((1, 512), F32)],
        scratch_shapes=[pltpu.VMEM((LANES, 512), F32)], compiler_params=_cp(("arbitrary",)),
    )(dy, sts, h, xbcc, xbcc, xbcc, h, *_ssd_consts(), dtb, alog, dsk, nw)


HG_LEVELS = 6


HG_SUMS = 1 + 2 * HG_LEVELS


def _hg_consts():
    n = HG_CHUNK
    t = np.arange(n)
    mats = [np.tril(np.ones((n, n), np.float32))]
    dq, dk, mk = [], [], []
    for lv in range(HG_LEVELS):
        b = 1 << lv
        start = (t // b) * b
        end = start + b - 1
        dq.append(((t[None, :] >= start[:, None]) & (t[None, :] <= t[:, None])).astype(np.float32))
        dk.append(((t[None, :] > t[:, None]) & (t[None, :] <= end[:, None])).astype(np.float32))
        mk.append((((t // b) % 2 == 1)[:, None] & ((t // b)[None, :] == (t // b)[:, None] - 1)).astype(np.float32))
    c1 = np.concatenate(mats + dq + dk, axis=0)
    return (jnp.asarray(c1, BF16), jnp.asarray(np.concatenate([c1, c1, c1], axis=1), BF16), jnp.asarray(np.stack(mk)))


@functools.partial(jax.custom_vjp, nondiff_argnums=(1, 2))
def _split_grid(x, nr, nc):
    br, bc = x.shape[0] // nr, x.shape[1] // nc
    return tuple(x[i * br:(i + 1) * br, j * bc:(j + 1) * bc] for i in range(nr) for j in range(nc))


def _split_grid_fwd(x, nr, nc):
    return _split_grid(x, nr, nc), None


def _split_grid_bwd(nr, nc, _, cts):
    rows = [cts[i * nc] if nc == 1 else jnp.concatenate(cts[i * nc:(i + 1) * nc], axis=1) for i in range(nr)]
    return (rows[0] if nr == 1 else jnp.concatenate(rows, axis=0),)


_split_grid.defvjp(_split_grid_fwd, _split_grid_bwd)


@jax.custom_vjp
def _level_sums(c1, c3, g):
    gh = g.astype(BF16)
    r1 = g - gh.astype(F32)
    gm = r1.astype(BF16)
    gl = (r1 - gm.astype(F32)).astype(BF16)
    return _dot(c3, jnp.concatenate([gh, gm, gl], axis=0), NN)


def _level_sums_fwd(c1, c3, g):
    return _level_sums(c1, c3, g), (c1, c3)


def _level_sums_bwd(res, ct):
    c1, c3 = res
    ch = ct.astype(BF16)
    cm = (ct - ch.astype(F32)).astype(BF16)
    return jnp.zeros_like(c1), jnp.zeros_like(c3), _dot(c1, ch, TN) + _dot(c1, cm, TN)


_level_sums.defvjp(_level_sums_fwd, _level_sums_bwd)


def _hg_chunk(c1, c3, masks, hq, hf, hi, hg, sts, lb, nw):
    q = _silu(hq)
    g = jnp.log(lb + (1.0 - lb) * jax.nn.sigmoid(hf))
    k = (1.0 - lb) * jax.nn.sigmoid(-hf)
    parts = _split_grid(_level_sums(c1, c3, g), HG_SUMS, 1)
    bc = parts[0]
    tot = jnp.sum(g, 0, keepdims=True)
    qe = _split_grid(q * jnp.exp(bc), 1, HG_HEADS)
    kd = _split_grid(k * jnp.exp(tot - bc), 1, HG_HEADS)
    etot = _split_grid(jnp.exp(tot), 1, HG_HEADS)
    qh, kh, vh, gh = [_split_grid(a, 1, HG_HEADS) for a in (q, k, hi, hg)]
    att = [None] * HG_HEADS
    for lv in range(HG_LEVELS):
        qt = _split_grid(q * jnp.exp(parts[1 + lv]), 1, HG_HEADS)
        kt = _split_grid(k * jnp.exp(parts[1 + HG_LEVELS + lv]), 1, HG_HEADS)
        for hd in range(HG_HEADS):
            t = masks[lv] * _bdot(qt[hd], kt[hd], NT)
            att[hd] = t if att[hd] is None else att[hd] + t
    outs, stn = [], []
    for hd in range(HG_HEADS):
        o = (_bdot(att[hd], vh[hd]) + jnp.sum(qh[hd] * kh[hd], 1, keepdims=True) * vh[hd] + _bdot(qe[hd], sts[hd], NT))
        stn.append(sts[hd] * etot[hd] + _bdot(vh[hd], kd[hd], TN))
        outs.append(o * lax.rsqrt(jnp.mean(o * o, 1, keepdims=True) + RMS_EPS) * nw * _silu(gh[hd]))
    return tuple(outs), tuple(stn)


def _hg_specs(rev, nc):
    ci = (lambda i: nc - 1 - i) if rev else (lambda i: i)
    n = HG_CHUNK
    act = [pl.BlockSpec((n, 512), functools.partial(lambda i, c: (ci(i), c), c=c))
           for c in (AB_HQ // 512, AB_HF // 512, AB_HI // 512, AB_HG // 512)]
    other = [pl.BlockSpec((HG_SUMS * n, n), lambda i: (0, 0)), pl.BlockSpec((HG_SUMS * n, 3 * n), lambda i: (0, 0)),
             pl.BlockSpec((HG_LEVELS, n, n), lambda i: (0, 0, 0)), pl.BlockSpec((1, 512), lambda i: (0, 0)),
             pl.BlockSpec((1, LANES), lambda i: (0, 0))]
    return ci, act, other


def _head_rows(hd):
    return slice(LANES * hd, LANES * (hd + 1))


def _hg_fwd(h, lb, nw, name):
    s = h.shape[0]
    nc = s // HG_CHUNK
    ci, act, other = _hg_specs(False, nc)

    def kern(q_ref, f_ref, i_ref, g_ref, c1_ref, c3_ref, mk_ref, lb_ref, nw_ref, y_ref, sts_ref, st):
        @pl.when(pl.program_id(0) == 0)
        def _():
            st[...] = jnp.zeros_like(st)

        sts_ref[0] = st[...]
        masks = [mk_ref[lv] for lv in range(HG_LEVELS)]
        outs, stn = _hg_chunk(c1_ref[...], c3_ref[...], masks, q_ref[...], f_ref[...], i_ref[...], g_ref[...],
                              tuple(st[_head_rows(hd), :] for hd in range(HG_HEADS)), lb_ref[...], nw_ref[...])
        for hd in range(HG_HEADS):
            y_ref[:, _head_rows(hd)] = outs[hd].astype(BF16)
            st[_head_rows(hd), :] = stn[hd]

    return pl.pallas_call(
        kern, name=name, grid=(nc,), in_specs=act + other,
        out_specs=[pl.BlockSpec((HG_CHUNK, 512), lambda i: (i, 0)), pl.BlockSpec((1, 512, LANES), lambda i: (i, 0, 0))],
        out_shape=[jax.ShapeDtypeStruct((s, 512), BF16), jax.ShapeDtypeStruct((nc, 512, LANES), F32)],
        scratch_shapes=[pltpu.VMEM((512, LANES), F32)], compiler_params=_cp(("arbitrary",)),
    )(h, h, h, h, *_hg_consts(), lb, nw)


def _hg_bwd(dy, dy_cb, h, sts, lb, nw, name):
    s = h.shape[0]
    nc = s // HG_CHUNK
    n = HG_CHUNK
    ci, act, other = _hg_specs(True, nc)

    def kern(dy_ref, sts_ref, q_ref, f_ref, i_ref, g_ref, c1_ref, c3_ref, mk_ref, lb_ref, nw_ref,
             dq_ref, df_ref, di_ref, dg_ref, dlb_ref, dnw_ref, dst):
        first = pl.program_id(0) == 0

        @pl.when(first)
        def _():
            dst[...] = jnp.zeros_like(dst)

        masks = [mk_ref[lv] for lv in range(HG_LEVELS)]
        heads = range(HG_HEADS)
        _, pull = jax.vjp(functools.partial(_hg_chunk, c1_ref[...], c3_ref[...], masks), q_ref[...], f_ref[...], i_ref[...],
                          g_ref[...], tuple(sts_ref[0, _head_rows(hd), :] for hd in heads), lb_ref[...], nw_ref[...])
        dq, df, di, dg, dsti, dlb, dnw = pull((tuple(dy_ref[:, _head_rows(hd)] for hd in heads),
                                               tuple(dst[_head_rows(hd), :] for hd in heads)))
        dq_ref[...] = dq.astype(BF16)
        df_ref[...] = df.astype(BF16)
        di_ref[...] = di.astype(BF16)
        dg_ref[...] = dg.astype(BF16)
        for hd in heads:
            dst[_head_rows(hd), :] = dsti[hd]
        _acc(dlb_ref, dlb, first)
        _acc(dnw_ref, dnw, first)

    blk = pl.BlockSpec((n, 512), lambda i: (ci(i), 0))
    return pl.pallas_call(
        kern, name=name, grid=(nc,),
        in_specs=[pl.BlockSpec((n, 512), lambda i: (ci(i), dy_cb)), pl.BlockSpec((1, 512, LANES), lambda i: (ci(i), 0, 0))]
        + act + other,
        out_specs=[blk, blk, blk, blk, pl.BlockSpec((1, 512), lambda i: (0, 0)), pl.BlockSpec((1, LANES), lambda i: (0, 0))],
        out_shape=[jax.ShapeDtypeStruct((s, 512), BF16)] * 4 + [jax.ShapeDtypeStruct((1, 512), F32),
                                                                jax.ShapeDtypeStruct((1, LANES), F32)],
        scratch_shapes=[pltpu.VMEM((512, LANES), F32)], compiler_params=_cp(("arbitrary",)),
    )(dy, sts, h, h, h, h, *_hg_consts(), lb, nw)


def _swa_consts():
    n = SWA_BLOCK
    r = np.arange(8 * n)
    t, blk = r % n, r // n
    lanes = np.arange(LANES)
    own = (t[:, None] >= lanes[None, :]).astype(np.float32)
    prev = (lanes[None, :] > t[:, None]).astype(np.float32)
    sel = (lanes[None, :] == (blk // 2 + 4 * (blk % 2))[:, None]).astype(np.float32)
    grp = ((lanes[None, :] // 64) == (blk % 2)[:, None]).astype(np.float32)
    return tuple(jnp.asarray(a) for a in (own, prev, sel, grp))


def _swa_block(own, prev, sel, grp, q, k0, v0, k1, v1, srow):
    qs = _split_grid(q, 1, 4)
    qa = jnp.concatenate([qs[p] for p in range(4) for _ in range(2)], axis=0) * grp
    s0 = jnp.where(prev > 0.0, _bdot(qa, k0, NT) * 0.125, MASK_VALUE)
    s1 = jnp.where(own > 0.0, _bdot(qa, k1, NT) * 0.125, MASK_VALUE)
    sk = jnp.sum(sel * srow, 1, keepdims=True)
    m = lax.stop_gradient(jnp.maximum(jnp.maximum(jnp.max(s0, 1, keepdims=True), jnp.max(s1, 1, keepdims=True)), sk))
    p0, p1 = jnp.exp(s0 - m), jnp.exp(s1 - m)
    inv = 1.0 / (jnp.sum(p0, 1, keepdims=True) + jnp.sum(p1, 1, keepdims=True) + jnp.exp(sk - m))
    o = _split_grid((_bdot(p0 * inv, v0) + _bdot(p1 * inv, v1)) * grp, 8, 1)
    return tuple(o[2 * p] + o[2 * p + 1] for p in range(4))


def _swa_specs():
    n = SWA_BLOCK
    pv = lambda i: jnp.maximum(i - 1, 0)
    mask = pl.BlockSpec((8 * n, LANES), lambda i: (0, 0))
    return [mask, mask, mask, mask, pl.BlockSpec((n, 512), lambda i: (i, CD_Q // 512)),
            pl.BlockSpec((n, LANES), lambda i: (pv(i), CD_K // LANES)), pl.BlockSpec((n, LANES), lambda i: (pv(i), CD_V // LANES)),
            pl.BlockSpec((n, LANES), lambda i: (i, CD_K // LANES)), pl.BlockSpec((n, LANES), lambda i: (i, CD_V // LANES)),
            pl.BlockSpec((1, LANES), lambda i: (0, 0))]


def _swa_args(refs, notfirst):
    own, prev, sel, grp = (r[...] for r in refs[:4])
    return (own, prev * notfirst.astype(F32), sel, grp), tuple(r[...] for r in refs[4:])


def _swa_fwd(hc, srow, name):
    s = hc.shape[0]
    n = SWA_BLOCK

    def kern(*refs):
        y_ref = refs[10]
        consts, acts = _swa_args(refs[:10], pl.program_id(0) > 0)
        outs = _swa_block(*consts, *acts)
        for p in range(4):
            y_ref[:, LANES * p:LANES * (p + 1)] = outs[p].astype(BF16)

    return pl.pallas_call(
        kern, name=name, grid=(s // n,), in_specs=_swa_specs(), out_specs=pl.BlockSpec((n, 512), lambda i: (i, 0)),
        out_shape=jax.ShapeDtypeStruct((s, 512), BF16), compiler_params=_cp(("parallel",)),
    )(*_swa_consts(), hc, hc, hc, hc, hc, srow)


def _swa_bwd(dy, dy_cb, hc, srow, name):
    s = hc.shape[0]
    n = SWA_BLOCK

    def kern(dy_ref, *refs):
        dq_ref, dk_ref, dv_ref, ds_ref = refs[10:]
        i = pl.program_id(0)

        @pl.when(i == 0)
        def _():
            dk_ref[...] = jnp.zeros_like(dk_ref)
            dv_ref[...] = jnp.zeros_like(dv_ref)
            ds_ref[...] = jnp.zeros_like(ds_ref)

        r1 = pl.ds(pl.multiple_of(i * n, n), n)
        r0 = pl.ds(pl.multiple_of(jnp.maximum(i - 1, 0) * n, n), n)
        consts, acts = _swa_args(refs[:10], i > 0)
        _, pull = jax.vjp(functools.partial(_swa_block, *consts), *acts)
        dq, dk0, dv0, dk1, dv1, dsr = pull(tuple(dy_ref[:, LANES * p:LANES * (p + 1)] for p in range(4)))
        dq_ref[...] = dq.astype(BF16)
        dk_ref[r0, :] += dk0
        dv_ref[r0, :] += dv0
        dk_ref[r1, :] += dk1
        dv_ref[r1, :] += dv1
        ds_ref[...] += dsr

    full = pl.BlockSpec((s, LANES), lambda i: (0, 0))
    return pl.pallas_call(
        kern, name=name, grid=(s // n,), in_specs=[pl.BlockSpec((n, 512), lambda i: (i, dy_cb))] + _swa_specs(),
        out_specs=[pl.BlockSpec((n, 512), lambda i: (i, 0)), full, full, pl.BlockSpec((1, LANES), lambda i: (0, 0))],
        out_shape=[jax.ShapeDtypeStruct((s, 512), BF16), jax.ShapeDtypeStruct((s, LANES), F32),
                   jax.ShapeDtypeStruct((s, LANES), F32), jax.ShapeDtypeStruct((1, LANES), F32)],
        compiler_params=_cp(("arbitrary",)),
    )(dy, *_swa_consts(), hc, hc, hc, hc, hc, srow)


RG_TILE = 256


def _neg_expm1(x):
    ser = x * (1 + x / 2 * (1 + x / 3 * (1 + x / 4 * (1 + x / 5 * (1 + x / 6 * (1 + x / 7 * (1 + x / 8)))))))
    return jnp.where(x > -0.25, -ser, 1.0 - jnp.exp(x))


def _rg_gates(xc, wa, wx, ba, bx, lam):
    r = jax.nn.sigmoid(_bdot(xc, wa) + ba)
    i = jax.nn.sigmoid(_bdot(xc, wx) + bx)
    log_a = -RG_C * r * jax.nn.softplus(-lam)
    u = jnp.sqrt(jnp.maximum(_neg_expm1(2.0 * log_a), 0.0)) * (i * xc)
    return jnp.exp(log_a), u


def _rg_out(hs, gate):
    return hs * jax.nn.gelu(gate)


def _rows_to_tile(rows):
    sub = _iota((SUBLANES, 1), 0)
    out = jnp.broadcast_to(rows[0], (SUBLANES, rows[0].shape[1]))
    for j in range(1, SUBLANES):
        out = jnp.where(sub == j, rows[j], out)
    return out


def _rg_fwd(xc, hc, wa, wx, ba, bx, lam, name):
    s = xc.shape[0]
    tm = min(RG_TILE, s)

    def kern(x_ref, g_ref, wa_ref, wx_ref, ba_ref, bx_ref, lam_ref, y_ref, h_ref, a_s, u_s, carry):
        @pl.when(pl.program_id(0) == 0)
        def _():
            carry[...] = jnp.zeros_like(carry)

        a, u = _rg_gates(x_ref[...], wa_ref[...], wx_ref[...], ba_ref[...], bx_ref[...], lam_ref[...])
        a_s[...] = a
        u_s[...] = u

        def body(gi, hp):
            r0 = pl.multiple_of(gi * SUBLANES, SUBLANES)
            ab, ub = a_s[pl.ds(r0, SUBLANES), :], u_s[pl.ds(r0, SUBLANES), :]
            rows = []
            for j in range(SUBLANES):
                hp = ab[j:j + 1] * hp + ub[j:j + 1]
                rows.append(hp)
            h_ref[pl.ds(r0, SUBLANES), :] = _rows_to_tile(rows)
            return hp

        carry[0:1, :] = lax.fori_loop(0, tm // SUBLANES, body, carry[0:1, :])
        y_ref[...] = _rg_out(h_ref[...], g_ref[...]).astype(BF16)

    row = pl.BlockSpec((tm, 512), lambda i: (i, 0))
    mat = pl.BlockSpec((512, 512), lambda i: (0, 0))
    vec = pl.BlockSpec((1, 512), lambda i: (0, 0))
    return pl.pallas_call(
        kern, name=name, grid=(s // tm,),
        in_specs=[row, pl.BlockSpec((tm, 512), lambda i: (i, CD_GATE // 512)), mat, mat, vec, vec, vec],
        out_specs=[row, row],
        out_shape=[jax.ShapeDtypeStruct((s, 512), BF16), jax.ShapeDtypeStruct((s, 512), F32)],
        scratch_shapes=[pltpu.VMEM((tm, 512), F32), pltpu.VMEM((tm, 512), F32), pltpu.VMEM((SUBLANES, 512), F32)],
        compiler_params=_cp(("arbitrary",)),
    )(xc, hc, wa, wx, ba, bx, lam)


def _rg_bwd(dy, dy_cb, xc, hc, hs, wa, wx, ba, bx, lam, name):
    s = xc.shape[0]
    tm = min(RG_TILE, s)
    nt = s // tm
    ti = lambda i: nt - 1 - i

    def kern(dy_ref, x_ref, g_ref, h_ref, hp_ref, wa_ref, wx_ref, ba_ref, bx_ref, lam_ref,
             dx_ref, dg_ref, dwa_ref, dwx_ref, dba_ref, dbx_ref, dlam_ref, a_s, d_s, g_s, cg, ca):
        first = pl.program_id(0) == 0

        @pl.when(first)
        def _():
            cg[...] = jnp.zeros_like(cg)
            ca[...] = jnp.zeros_like(ca)

        (a, _), pull = jax.vjp(_rg_gates, x_ref[...], wa_ref[...].astype(F32), wx_ref[...].astype(F32), ba_ref[...],
                               bx_ref[...], lam_ref[...])
        hs_t = h_ref[...]
        _, pull_out = jax.vjp(_rg_out, hs_t, g_ref[...])
        dho, dgate = pull_out(dy_ref[...])
        dg_ref[...] = dgate.astype(BF16)
        a_s[...] = a
        d_s[...] = dho

        def body(k, c):
            gn, an = c
            r0 = pl.multiple_of((tm // SUBLANES - 1 - k) * SUBLANES, SUBLANES)
            ab, db = a_s[pl.ds(r0, SUBLANES), :], d_s[pl.ds(r0, SUBLANES), :]
            rows = [None] * SUBLANES
            for j in range(SUBLANES - 1, -1, -1):
                gn = db[j:j + 1] + an * gn
                an = ab[j:j + 1]
                rows[j] = gn
            g_s[pl.ds(r0, SUBLANES), :] = _rows_to_tile(rows)
            return gn, an

        gn, an = lax.fori_loop(0, tm // SUBLANES, body, (cg[0:1, :], ca[0:1, :]))
        cg[0:1, :] = gn
        ca[0:1, :] = an
        g = g_s[...]
        halo = jnp.where(pl.program_id(0) == nt - 1, 0.0, hp_ref[...])
        hprev = pltpu.roll(jnp.concatenate([halo, hs_t], axis=0), 1, axis=0)[SUBLANES:]
        dxc, dwa, dwx, dba, dbx, dlam = pull((g * hprev, g))
        dx_ref[...] = dxc
        _acc(dwa_ref, dwa, first)
        _acc(dwx_ref, dwx, first)
        _acc(dba_ref, dba, first)
        _acc(dbx_ref, dbx, first)
        _acc(dlam_ref, dlam, first)

    row = pl.BlockSpec((tm, 512), lambda i: (ti(i), 0))
    mat = pl.BlockSpec((512, 512), lambda i: (0, 0))
    vec = pl.BlockSpec((1, 512), lambda i: (0, 0))
    hprev = pl.BlockSpec((SUBLANES, 512), lambda i: (jnp.maximum(ti(i) * (tm // SUBLANES) - 1, 0), 0))
    return pl.pallas_call(
        kern, name=name, grid=(nt,),
        in_specs=[pl.BlockSpec((tm, 512), lambda i: (ti(i), dy_cb)), row,
                  pl.BlockSpec((tm, 512), lambda i: (ti(i), CD_GATE // 512)), row, hprev, mat, mat, vec, vec, vec],
        out_specs=[row, row, mat, mat, vec, vec, vec],
        out_shape=[jax.ShapeDtypeStruct((s, 512), F32), jax.ShapeDtypeStruct((s, 512), BF16),
                   jax.ShapeDtypeStruct((512, 512), F32), jax.ShapeDtypeStruct((512, 512), F32),
                   jax.ShapeDtypeStruct((1, 512), F32), jax.ShapeDtypeStruct((1, 512), F32), jax.ShapeDtypeStruct((1, 512), F32)],
        scratch_shapes=[pltpu.VMEM((tm, 512), F32)] * 3 + [pltpu.VMEM((SUBLANES, 512), F32)] * 2,
        compiler_params=_cp(("arbitrary",)),
    )(dy, xc, hc, hs, hs, wa, wx, ba, bx, lam)


AB_REF = dict(z=(0, 512), xbc=(512, 768), dt=(1280, 8), hq=(1288, 512), hf=(1800, 512), hi=(2312, 512), hg=(2824, 512))
AB_INT = dict(z=AB_Z, hq=AB_HQ, hf=AB_HF, hi=AB_HI, hg=AB_HG, xbc=AB_XBC, dt=AB_DT)
CD_REF = dict(q=(0, 512), k=(512, 128), v=(640, 128), gate=(768, 512), xr=(1280, 512))


def _cols(w, start, size):
    return lax.slice_in_dim(w, start, start + size, axis=w.ndim - 1)


def _pad_lanes(v):
    return jnp.concatenate([v.astype(F32), jnp.zeros((LANES - v.shape[0],), F32)]).reshape(1, LANES)


def _ab_in_to_int(w):
    parts = [_cols(w, *AB_REF[k]) for k in ("z", "hq", "hf", "hi", "hg", "xbc", "dt")]
    return jnp.concatenate(parts + [jnp.zeros(w.shape[:-1] + (AB_COLS - AB_DT - 8,), w.dtype)], axis=-1)


def _ab_in_to_ref(w):
    return jnp.concatenate([_cols(w, AB_INT[k], AB_REF[k][1]) for k in ("z", "xbc", "dt", "hq", "hf", "hi", "hg")], axis=-1)


def _pairs_to_int(w, axis):
    sl = lambda h: lax.slice_in_dim(w, 64 * h, 64 * h + 64, axis=axis)
    return jnp.concatenate([sl(h) for p in range(4) for h in (p, 4 + p)], axis=axis)


def _pairs_to_ref(w, axis):
    sl = lambda i: lax.slice_in_dim(w, 64 * i, 64 * i + 64, axis=axis)
    return jnp.concatenate([sl(2 * h) for h in range(4)] + [sl(2 * h + 1) for h in range(4)], axis=axis)


def _cd_in_to_int(w):
    q = _pairs_to_int(_cols(w, *CD_REF["q"]), w.ndim - 1)
    return jnp.concatenate([q] + [_cols(w, *CD_REF[k]) for k in ("gate", "xr", "k", "v")], axis=-1)


def _cd_in_to_ref(w):
    q = _pairs_to_ref(_cols(w, CD_Q, 512), w.ndim - 1)
    return jnp.concatenate([q, _cols(w, CD_K, 128), _cols(w, CD_V, 128), _cols(w, CD_GATE, 512), _cols(w, CD_XR, 512)], axis=-1)


def _block_diag(w):
    tiled = jnp.concatenate([w.reshape(RG_WIDTH, 64)] * 8, axis=1)
    return tiled * _BLOCK_DIAG_MASK.astype(w.dtype)


_BLOCK_DIAG_MASK = np.kron(np.eye(8, dtype=np.float32), np.ones((64, 64), np.float32))


def _diag_blocks(m):
    return jnp.sum((m * _BLOCK_DIAG_MASK).reshape(8, 64, 8, 64), axis=2)


def _conv8(w):
    return jnp.concatenate([w, jnp.zeros((SUBLANES - w.shape[0], w.shape[1]), w.dtype)], axis=0)


def _hg_lower_bounds(hg_lower):
    sm = jax.nn.softmax(hg_lower.astype(F32), axis=0)
    return jnp.clip(jnp.cumsum(sm, axis=0) - sm[0], 0.0, 1.0)


def _local_step(x, target, w):
    s = x.shape[0]
    lb_all, lb_pull = jax.vjp(_hg_lower_bounds, w["hg_lower"])
    grads = {k: [None] * v.shape[0] for k, v in w.items()}
    saved = []
    xb = x.astype(BF16)
    for l in range(DEPTH):
        j = l // 2
        t = f"l{l}"
        sv = dict(x=x, xb=xb)
        if l % 2 == 0:
            win = _ab_in_to_int(w["ab_w_in"][j])
            wout = w["ab_w_out"][j]
            h = _mm([(xb, 0, win, 0, D_MODEL)], AB_COLS, F32, t + "_in")
            w8 = _conv8(w["ssd_conv_w"][j])
            xbcc = _dwconv_fwd(h, AB_XBC // CONV_CB, SSD_CONV_DIM, w8, w["ssd_conv_b"][j], 4, t + "_conv")
            sp = (_pad_lanes(w["ssd_dt_bias"][j]), _pad_lanes(w["ssd_a_log"][j]), _pad_lanes(w["ssd_d"][j]),
                  w["ssd_norm_w"][j].reshape(1, 512))
            ya, ssts = _ssd_fwd(h, xbcc, *sp, t + "_ssd")
            hp = (lb_all[j].reshape(1, 512), w["hg_norm_w"][j].reshape(1, LANES))
            yb, hsts = _hg_fwd(h, *hp, t + "_hg")
            sv.update(win=win, wout=wout, h=h, w8=w8, xbcc=xbcc, sp=sp, ssts=ssts, hp=hp, hsts=hsts)
        else:
            win = _cd_in_to_int(w["cd_w_in"][j])
            wout = jnp.concatenate([_pairs_to_int(w["cd_w_out"][j][:512], 0), w["cd_w_out"][j][512:]], axis=0)
            h = _mm([(xb, 0, win, 0, D_MODEL)], CD_COLS, F32, t + "_in")
            srow = _pad_lanes(w["swa_sinks"][j])
            ya = _swa_fwd(h, srow, t + "_swa")
            w8 = _conv8(w["rg_conv_w"][j])
            xc = _dwconv_fwd(h, CD_XR // CONV_CB, RG_WIDTH, w8, w["rg_conv_b"][j], 4, t + "_conv")
            rp = (_block_diag(w["rg_wa"][j]).astype(BF16), _block_diag(w["rg_wx"][j]).astype(BF16),
                  w["rg_ba"][j].reshape(1, 512), w["rg_bx"][j].reshape(1, 512), w["rg_lambda"][j].reshape(1, 512))
            yb, hs = _rg_fwd(xc, h, *rp, t + "_rg")
            sv.update(win=win, wout=wout, h=h, w8=w8, srow=srow, xc=xc, rp=rp, hs=hs)
        x1, x1b, r1 = _mm_ln([(ya, 0, wout, 0, 512), (yb, 0, wout, 1, 512)], x, w["ln_g"][l, 0], w["ln_b"][l, 0], t + "_out_ln")
        wup, wdown = w["ffn_w_up"][l], w["ffn_w_down"][l]
        fw8 = _conv8(w["ffn_conv_w"][l])
        hu = _mm([(x1b, 0, wup, 0, D_MODEL)], 2 * FFN_DIM, BF16, t + "_up")
        a, hg, hv = _ffn_gate_fwd(hu, fw8, w["ffn_conv_b"][l], t + "_gate")
        x, xb, r2 = _mm_ln([(a, 0, wdown, 0, FFN_DIM)], x1, w["ln_g"][l, 1], w["ln_b"][l, 1], t + "_down_ln")
        sv.update(ya=ya, yb=yb, r1=r1, x1b=x1b, hu=hu, hg=hg, hv=hv, a=a, r2=r2, fw8=fw8)
        saved.append(sv)

    loss = 0.5 * jnp.sum(_sqerr(x, target, "sqerr")) / D_MODEL
    dya, dyb, ca, cb = x, target, 1.0 / D_MODEL, -1.0 / D_MODEL
    for l in range(DEPTH - 1, -1, -1):
        j = l // 2
        t = f"l{l}b"
        sv = saved[l]
        wup, wdown = w["ffn_w_up"][l], w["ffn_w_down"][l]
        dr2, dg2, db2 = _ln_bwd(dya, dyb, ca, cb, sv["r2"], w["ln_g"][l, 1], t + "_ln2")
        da = _mm([(dr2, 0, wdown, 0, D_MODEL)], FFN_DIM, BF16, t + "_down", trans_b=True)
        grads["ffn_w_down"][l] = _mm_tn(sv["a"], 0, FFN_DIM, dr2, 0, D_MODEL, t + "_wdown")
        dhg, dhu, dfw8, dfb = _ffn_gate_bwd(sv["hu"], sv["hg"], sv["hv"], da, sv["fw8"], t + "_gate")
        grads["ffn_conv_w"][l] = dfw8[:3]
        grads["ffn_conv_b"][l] = dfb
        dx1 = _mm([(dhg, 0, wup, 0, FFN_DIM), (dhu, 0, wup, 1, FFN_DIM)], D_MODEL, F32, t + "_up", trans_b=True)
        grads["ffn_w_up"][l] = jnp.concatenate([_mm_tn(sv["x1b"], 0, D_MODEL, dhg, 0, FFN_DIM, t + "_wup_g"),
                                                _mm_tn(sv["x1b"], 0, D_MODEL, dhu, 0, FFN_DIM, t + "_wup_u")], axis=1)
        dr1, dg1, db1 = _ln_bwd(dr2, dx1, ALPHA, 1.0, sv["r1"], w["ln_g"][l, 0], t + "_ln1")
        grads["ln_g"][l] = jnp.stack([dg1, dg2])
        grads["ln_b"][l] = jnp.stack([db1, db2])
        win, wout, h = sv["win"], sv["wout"], sv["h"]
        dycat = _mm([(dr1, 0, wout, 0, D_MODEL)], D_MODEL, F32, t + "_out", trans_b=True)
        dwout = jnp.concatenate([_mm_tn(sv["ya"], 0, 512, dr1, 0, D_MODEL, t + "_wout_a"),
                                 _mm_tn(sv["yb"], 0, 512, dr1, 0, D_MODEL, t + "_wout_b")], axis=0)
        if l % 2 == 0:
            dz, dxbcc, ddt, ddtb, dal, ddsk, dnw = _ssd_bwd(dycat, 0, h, sv["xbcc"], sv["ssts"], *sv["sp"], t + "_ssd")
            dxbc, dcw, dcb = _dwconv_bwd(dxbcc, h, AB_XBC // CONV_CB, SSD_CONV_DIM, sv["w8"], 4, t + "_conv")
            dq, df, di, dg, dlb, dhnw = _hg_bwd(dycat, 1, h, sv["hsts"], *sv["hp"], t + "_hg")
            grads["ab_w_out"][j] = dwout
            grads["ssd_conv_w"][j], grads["ssd_conv_b"][j] = dcw[:4], dcb
            grads["ssd_dt_bias"][j], grads["ssd_a_log"][j], grads["ssd_d"][j] = ddtb[0, :8], dal[0, :8], ddsk[0, :8]
            grads["ssd_norm_w"][j], grads["hg_norm_w"][j] = dnw[0], dhnw[0]
            grads["hg_lower"][j] = dlb[0]
            pieces = [(dz, 0, AB_Z, 512), (dq, 0, AB_HQ, 512), (df, 0, AB_HF, 512), (di, 0, AB_HI, 512), (dg, 0, AB_HG, 512),
                      (dxbc, 0, AB_XBC, 256), (dxbc, 1, AB_XBC + 256, 256), (dxbc, 2, AB_XBC + 512, 256), (ddt, 0, AB_DT, LANES)]
            dwin = [_mm_tn(sv["xb"], 0, D_MODEL, p, 0, p.shape[1], t + f"_win{i}")
                    for i, p in enumerate((dz, dq, df, di, dg, dxbc, ddt))]
            grads["ab_w_in"][j] = _ab_in_to_ref(jnp.concatenate(dwin, axis=1))
        else:
            dq, dk, dv, dsr = _swa_bwd(dycat, 0, h, sv["srow"], t + "_swa")
            dxc, dgate, dwa, dwx, dba, dbx, dlam = _rg_bwd(dycat, 1, sv["xc"], h, sv["hs"], *sv["rp"], t + "_rg")
            dxr, dcw, dcb = _dwconv_bwd(dxc, h, CD_XR // CONV_CB, RG_WIDTH, sv["w8"], 4, t + "_conv")
            grads["cd_w_out"][j] = jnp.concatenate([_pairs_to_ref(dwout[:512], 0), dwout[512:]], axis=0)
            grads["swa_sinks"][j] = dsr[0, :8]
            grads["rg_conv_w"][j], grads["rg_conv_b"][j] = dcw[:4], dcb
            grads["rg_wa"][j], grads["rg_wx"][j] = _diag_blocks(dwa), _diag_blocks(dwx)
            grads["rg_ba"][j], grads["rg_bx"][j], grads["rg_lambda"][j] = dba[0], dbx[0], dlam[0]
            pieces = [(dq, 0, CD_Q, 512), (dgate, 0, CD_GATE, 512), (dxr, 0, CD_XR, 512), (dk, 0, CD_K, LANES), (dv, 0, CD_V, LANES)]
            dwin = [_mm_tn(sv["xb"], 0, D_MODEL, p, 0, p.shape[1], t + f"_win{i}")
                    for i, p in enumerate((dq, dgate, dxr, dk, dv))]
            grads["cd_w_in"][j] = _cd_in_to_ref(jnp.concatenate(dwin, axis=1))
        dx0 = _mm([(p, pcb, win, off // k, k) for p, pcb, off, k in pieces], D_MODEL, F32, t + "_in", trans_b=True)
        dya, dyb, ca, cb = dr1, dx0, ALPHA, 1.0
    grad_x = _axpby(dya, dyb, ca, cb, "grad_x")
    out = {k: (v if k in BIG else jnp.stack(v)) for k, v in grads.items()}
    (out["hg_lower"],) = lb_pull(out["hg_lower"])
    return loss, grad_x, out


MESH = pl.DeviceIdType.MESH
ANY = pl.BlockSpec(memory_space=pl.ANY)
PACK_ROWS = 2048


def _place():
    x, y, c = lax.axis_index("x"), lax.axis_index("y"), lax.axis_index("c")
    return x, y, c, [(1 - x, y), (x, 1 - y), (1 - x, 1 - y)]


def _rcopy(src, dst, send_sems, recv_sems, k, to):
    return pltpu.make_async_remote_copy(src_ref=src, dst_ref=dst, send_sem=send_sems.at[k], recv_sem=recv_sems.at[k],
                                        device_id=to, device_id_type=MESH)


def _gather_chips(wp, name):
    def body(w_ref, out_ref, send_sems, recv_sems):
        x, y, c, chips = _place()
        me, sib = 2 * x + y, (x, y, 1 - c)
        first = [_rcopy(w_ref.at[c], out_ref.at[me, c], send_sems, recv_sems, k, (cx, cy, c)) for k, (cx, cy) in enumerate(chips)]
        for cp in first:
            cp.start()
        passed = []
        for k, (cx, cy) in enumerate(chips):
            blk = out_ref.at[2 * cx + cy, c]
            _rcopy(blk, blk, send_sems, recv_sems, k, (cx, cy, c)).wait_recv()
            passed.append(_rcopy(blk, blk, send_sems, recv_sems, 3 + k, sib))
            passed[-1].start()
        for k, (cx, cy) in enumerate(chips):
            blk = out_ref.at[2 * cx + cy, 1 - c]
            _rcopy(blk, blk, send_sems, recv_sems, 3 + k, sib).wait_recv()
        for cp in first + passed:
            cp.wait_send()

    out = pl.pallas_call(
        body, name=name, in_specs=[ANY], out_specs=ANY, out_shape=jax.ShapeDtypeStruct((4,) + wp.shape, wp.dtype),
        scratch_shapes=[pltpu.SemaphoreType.DMA((6,)), pltpu.SemaphoreType.DMA((6,))],
    )(wp)
    return lax.dynamic_update_slice(out, wp[None], (2 * lax.axis_index("x") + lax.axis_index("y"), 0, 0, 0))


def _swap_halves(g, name):
    def body(g_ref, x_ref, send_sems, recv_sems):
        x, y, c, _ = _place()
        cp = _rcopy(g_ref.at[:, 1 - c], x_ref, send_sems, recv_sems, 0, (x, y, 1 - c))
        cp.start()
        cp.wait()

    return pl.pallas_call(
        body, name=name, in_specs=[ANY], out_specs=ANY,
        out_shape=jax.ShapeDtypeStruct((g.shape[0],) + g.shape[2:], g.dtype),
        scratch_shapes=[pltpu.SemaphoreType.DMA((1,)), pltpu.SemaphoreType.DMA((1,))],
    )(g)


def _scatter_chips(hb, name):
    def body(h_ref, y_ref, send_sems, recv_sems):
        x, y, c, chips = _place()
        me = 2 * x + y
        sends = [_rcopy(h_ref.at[2 * cx + cy], y_ref.at[me], send_sems, recv_sems, k, (cx, cy, c)) for k, (cx, cy) in enumerate(chips)]
        for cp in sends:
            cp.start()
        for k, (cx, cy) in enumerate(chips):
            blk = y_ref.at[2 * cx + cy]
            _rcopy(blk, blk, send_sems, recv_sems, k, (cx, cy, c)).wait_recv()
        for cp in sends:
            cp.wait_send()

    out = pl.pallas_call(
        body, name=name, in_specs=[ANY], out_specs=ANY, out_shape=jax.ShapeDtypeStruct(hb.shape, hb.dtype),
        scratch_shapes=[pltpu.SemaphoreType.DMA((3,)), pltpu.SemaphoreType.DMA((3,))],
    )(hb)
    me = 2 * lax.axis_index("x") + lax.axis_index("y")
    return lax.dynamic_update_slice(out, lax.dynamic_slice_in_dim(hb, me, 1, axis=0), (me, 0, 0))


def _share_half(f, name):
    def body(f_ref, out_ref, send_sems, recv_sems):
        x, y, c, _ = _place()
        sib = (x, y, 1 - c)
        cp = _rcopy(f_ref, out_ref.at[c], send_sems, recv_sems, 0, sib)
        cp.start()
        _rcopy(f_ref, out_ref.at[1 - c], send_sems, recv_sems, 0, sib).wait_recv()
        cp.wait_send()

    out = pl.pallas_call(
        body, name=name, in_specs=[ANY], out_specs=ANY, out_shape=jax.ShapeDtypeStruct((2,) + f.shape, f.dtype),
        scratch_shapes=[pltpu.SemaphoreType.DMA((1,)), pltpu.SemaphoreType.DMA((1,))],
    )(f)
    return lax.dynamic_update_slice(out, f[None], (lax.axis_index("c"), 0, 0))


def _gather_devices(v, name):
    m_per, n = v.shape

    def body(x_ref, out_ref, send_sems, recv_sems, local_sem):
        x, y, c, chips = _place()
        me, sib = (x, y, c), (x, y, 1 - c)

        def rows(px, py, pc):
            return out_ref.at[pl.ds((4 * px + 2 * py + pc) * m_per, m_per), :]

        def copy(k, block, to, src=None):
            return _rcopy(rows(*block) if src is None else src, rows(*block), send_sems, recv_sems, k, to)

        mine = pltpu.make_async_copy(x_ref, rows(*me), local_sem)
        mine.start()
        first = [copy(0, me, sib, src=x_ref)]
        first += [copy(1 + j, me, (*chip, c), src=x_ref) for j, chip in enumerate(chips)]
        for cp in first:
            cp.start()
        passed = [copy(4 + j, (*chip, c), sib) for j, chip in enumerate(chips)]
        for j, chip in enumerate(chips):
            copy(1 + j, (*chip, c), me).wait_recv()
            passed[j].start()
        copy(0, sib, me).wait_recv()
        for j, chip in enumerate(chips):
            copy(4 + j, (*chip, 1 - c), me).wait_recv()
        for cp in first + passed:
            cp.wait_send()
        mine.wait()

    return pl.pallas_call(
        body, name=name, out_shape=jax.ShapeDtypeStruct((8 * m_per, n), v.dtype),
        in_specs=[pl.BlockSpec(memory_space=pltpu.VMEM)], out_specs=pl.BlockSpec(memory_space=pltpu.VMEM),
        scratch_shapes=[pltpu.SemaphoreType.DMA((7,)), pltpu.SemaphoreType.DMA((7,)), pltpu.SemaphoreType.DMA(())],
        compiler_params=_cp(),
    )(v)


def _add_half(g, xr, c, name, tr=2048):
    rh = g.shape[2]

    def kern(c_ref, g_ref, x_ref, o_ref):
        o_ref[...] = (g_ref[0] + x_ref[...]).astype(BF16)

    return pl.pallas_call(
        kern, name=name,
        grid_spec=pltpu.PrefetchScalarGridSpec(
            num_scalar_prefetch=1, grid=(4, rh // tr),
            in_specs=[pl.BlockSpec((1, 1, tr, LANES), lambda k, i, c_ref: (k, c_ref[0], i, 0)),
                      pl.BlockSpec((1, tr, LANES), lambda k, i, c_ref: (k, i, 0))],
            out_specs=pl.BlockSpec((1, tr, LANES), lambda k, i, c_ref: (k, i, 0))),
        out_shape=jax.ShapeDtypeStruct(xr.shape, BF16), compiler_params=_cp(("parallel", "parallel")),
    )(c.reshape(1).astype(jnp.int32), g, xr)


def _sum_blocks(y, name, tr=2048):
    n, r, _ = y.shape
    tr = tr if r % tr == 0 else r

    def kern(y_ref, o_ref):
        acc = y_ref[0].astype(F32)
        for k in range(1, n):
            acc = acc + y_ref[k].astype(F32)
        o_ref[...] = acc

    return pl.pallas_call(
        kern, name=name, grid=(r // tr,), in_specs=[pl.BlockSpec((n, tr, LANES), lambda i: (0, i, 0))],
        out_specs=pl.BlockSpec((tr, LANES), lambda i: (i, 0)), out_shape=jax.ShapeDtypeStruct((r, LANES), F32),
        compiler_params=_cp(("parallel",)),
    )(y)


def _adamw(w, g, m, v, name, tr=256):
    r, n = w.shape
    tr = tr if r % tr == 0 else r

    def kern(w_ref, g_ref, m_ref, v_ref, d_ref, nm_ref, nv_ref):
        gg = g_ref[...]
        nm = ADAM_B1 * m_ref[...] + (1.0 - ADAM_B1) * gg
        nv = ADAM_B2 * v_ref[...] + (1.0 - ADAM_B2) * (gg * gg)
        m_hat = nm / (1.0 - ADAM_B1 ** ADAM_STEP)
        v_hat = nv / (1.0 - ADAM_B2 ** ADAM_STEP)
        d_ref[...] = -ADAM_LR * (m_hat / (jnp.sqrt(v_hat) + ADAM_EPS) + ADAM_WD * w_ref[...])
        nm_ref[...] = nm
        nv_ref[...] = nv

    blk = pl.BlockSpec((tr, n), lambda i: (i, 0))
    return pl.pallas_call(
        kern, name=name, grid=(r // tr,), in_specs=[blk] * 4, out_specs=[blk] * 3,
        out_shape=[jax.ShapeDtypeStruct((r, n), F32)] * 3, compiler_params=_cp(("parallel",)),
    )(w, g, m, v)


def _rows_of(n):
    return -(-n // (LANES * SUBLANES)) * SUBLANES


def _pack(arrs, lead, dtype, mult):
    ls = arrs[0].shape[:lead]
    blocks, total = [], 0
    for a in arrs:
        f = a.astype(dtype).reshape(ls + (-1,))
        n = f.shape[-1]
        rows = _rows_of(n)
        if rows * LANES != n:
            f = jnp.concatenate([f, jnp.zeros(ls + (rows * LANES - n,), dtype)], axis=-1)
        blocks.append(f.reshape(ls + (rows, LANES)))
        total += rows
    if total % mult:
        blocks.append(jnp.zeros(ls + (mult - total % mult, LANES), dtype))
    return jnp.concatenate(blocks, axis=lead)


def _unpack(buf, lead, shapes):
    ls = buf.shape[:lead]
    out, off = [], 0
    for sh in shapes:
        n = int(np.prod(sh))
        rows = _rows_of(n)
        flat = lax.slice_in_dim(buf, off, off + rows, axis=lead).reshape(ls + (rows * LANES,))
        if rows * LANES != n:
            flat = lax.slice_in_dim(flat, 0, n, axis=lead)
        out.append(flat.reshape(ls + tuple(sh)))
        off += rows
    return out


BIG = ("ab_w_in", "ab_w_out", "cd_w_in", "cd_w_out", "ffn_w_up", "ffn_w_down")
BIG_COLS = ("ab_w_in", "cd_w_in", "ffn_w_up")
SMALL_SHARDED = ("ssd_conv_w", "rg_conv_w", "rg_conv_b", "rg_ba", "rg_bx", "rg_lambda", "ffn_conv_w", "ln_g", "ln_b")
WEIGHTS = ("ab_w_in", "ssd_conv_w", "ssd_conv_b", "ssd_dt_bias", "ssd_a_log", "ssd_d", "ssd_norm_w", "hg_lower", "hg_norm_w",
           "ab_w_out", "cd_w_in", "swa_sinks", "rg_conv_w", "rg_conv_b", "rg_wa", "rg_ba", "rg_wx", "rg_bx", "rg_lambda",
           "cd_w_out", "ffn_w_up", "ffn_conv_w", "ffn_conv_b", "ffn_w_down", "ln_g", "ln_b")
SMALL = tuple(n for n in WEIGHTS if n not in BIG)


def _full_from_shards(name, g):
    if name in BIG_COLS or name in SMALL_SHARDED:
        return jnp.concatenate([g[k] for k in range(4)], axis=-1)
    return jnp.concatenate([g[k] for k in range(4)], axis=1)


def _layer_shards(name, f):
    if name in BIG_COLS:
        k, n = f.shape
        return jnp.transpose(f.reshape(k, 4, n // 4), (1, 0, 2)).reshape(4, -1)
    return f.reshape(4, -1)


def kernel(x, ab_w_in, ssd_conv_w, ssd_conv_b, ssd_dt_bias, ssd_a_log, ssd_d, ssd_norm_w, hg_lower, hg_norm_w, ab_w_out, cd_w_in, swa_sinks, rg_conv_w, rg_conv_b, rg_wa, rg_ba, rg_wx, rg_bx, rg_lambda, cd_w_out, ffn_w_up, ffn_conv_w, ffn_conv_b, ffn_w_down, ln_g, ln_b, loss_target, m_ab_w_in, m_ssd_conv_w, m_ssd_conv_b, m_ssd_dt_bias, m_ssd_a_log, m_ssd_d, m_ssd_norm_w, m_hg_lower, m_hg_norm_w, m_ab_w_out, m_cd_w_in, m_swa_sinks, m_rg_conv_w, m_rg_conv_b, m_rg_wa, m_rg_ba, m_rg_wx, m_rg_bx, m_rg_lambda, m_cd_w_out, m_ffn_w_up, m_ffn_conv_w, m_ffn_conv_b, m_ffn_w_down, m_ln_g, m_ln_b, v_ab_w_in, v_ssd_conv_w, v_ssd_conv_b, v_ssd_dt_bias, v_ssd_a_log, v_ssd_d, v_ssd_norm_w, v_hg_lower, v_hg_norm_w, v_ab_w_out, v_cd_w_in, v_swa_sinks, v_rg_conv_w, v_rg_conv_b, v_rg_wa, v_rg_ba, v_rg_wx, v_rg_bx, v_rg_lambda, v_cd_w_out, v_ffn_w_up, v_ffn_conv_w, v_ffn_conv_b, v_ffn_w_down, v_ln_g, v_ln_b):
    args = locals()
    w = {n: args[n] for n in WEIGHTS}
    mom = {n: args["m_" + n] for n in WEIGHTS}
    var = {n: args["v_" + n] for n in WEIGHTS}
    cx, cy, cc = lax.axis_index("x"), lax.axis_index("y"), lax.axis_index("c")
    chip = 2 * cx + cy

    big_shapes = [w[n].shape for n in BIG]
    wp = _pack([w[n] for n in BIG], 0, BF16, 2 * PACK_ROWS)
    rh = wp.shape[0] // 2
    gathered = _gather_chips(wp.reshape(2, rh, LANES), "gather_weights").reshape(4, 2 * rh, LANES)
    full = {n: _full_from_shards(n, g) for n, g in zip(BIG, _unpack(gathered, 1, big_shapes))}
    sp = _pack([w[n] for n in SMALL_SHARDED], 0, F32, SUBLANES)
    sg = _gather_devices(sp, "gather_small").reshape(4, 2, sp.shape[0], LANES)[:, 0]
    for n, g in zip(SMALL_SHARDED, _unpack(sg, 1, [w[n].shape for n in SMALL_SHARDED])):
        full[n] = _full_from_shards(n, g)
    for n in SMALL:
        full.setdefault(n, w[n])

    loss, grad_x, grads = _local_step(x[0], loss_target[0], full)
    loss = lax.psum(loss, ("x", "y", "c"))

    g4 = _pack([jnp.concatenate([_layer_shards(n, f) for f in grads[n]], axis=1) for n in BIG], 1, F32,
               2 * PACK_ROWS).reshape(4, 2, rh, LANES)
    hb = _add_half(g4, _swap_halves(g4, "swap_halves"), cc, "add_halves")
    own = _sum_blocks(_scatter_chips(hb, "scatter_chips"), "sum_chips")
    gbig = _unpack(_share_half(own, "share_half").reshape(2 * rh, LANES), 0, big_shapes)
    gout = dict(zip(BIG, gbig))

    small_shapes = [grads[n].shape for n in SMALL]
    gs = _pack([grads[n] for n in SMALL], 0, F32, SUBLANES)
    gsum = _sum_blocks(_gather_devices(gs, "gather_small_grads").reshape(8, gs.shape[0], LANES), "sum_small")
    for n, g in zip(SMALL, _unpack(gsum, 0, small_shapes)):
        if n in SMALL_SHARDED:
            width = w[n].shape[-1]
            g = lax.dynamic_slice_in_dim(g, chip * width, width, axis=g.ndim - 1)
        gout[n] = g

    delta, new_m, new_v = {}, {}, {}
    for n in BIG:
        sh = w[n].shape
        two = lambda a: a.reshape(-1, sh[-1])
        d, nm, nv = _adamw(two(w[n]), two(gout[n]), two(mom[n]), two(var[n]), "adamw_" + n)
        delta[n], new_m[n], new_v[n] = d.reshape(sh), nm.reshape(sh), nv.reshape(sh)
    local_shapes = [w[n].shape for n in SMALL]
    packs = [_pack([d[n] for n in SMALL], 0, F32, SUBLANES) for d in (w, gout, mom, var)]
    for dst, buf in zip((delta, new_m, new_v), _adamw(*packs, "adamw_small")):
        dst.update(zip(SMALL, _unpack(buf, 0, local_shapes)))

    return (loss, grad_x[None], *[gout[n] for n in WEIGHTS], *[delta[n] for n in WEIGHTS],
            *[new_m[n] for n in WEIGHTS], *[new_v[n] for n in WEIGHTS])
```

```python
import functools

import numpy as np
import jax
import jax.numpy as jnp
from jax import lax
from jax.experimental import pallas as pl
from jax.experimental.pallas import tpu as pltpu

F32 = jnp.float32
BF16 = jnp.bfloat16

D_MODEL = 1024
DEPTH = 4
SSD_HEADS = 8
SSD_D_INNER = 512
SSD_CONV_DIM = 768
SSD_CHUNK = 128
HG_CHUNK = 64
HG_HEADS = 4
SWA_BLOCK = 128
RG_WIDTH = 512
FFN_DIM = 2816
LN_EPS = 1e-5
RMS_EPS = 1e-6
MASK_VALUE = -1e9
ALPHA = (2 * DEPTH) ** 0.25
RG_C = 8.0
ADAM_LR, ADAM_B1, ADAM_B2, ADAM_EPS, ADAM_WD, ADAM_STEP = 0.001, 0.9, 0.999, 1e-08, 0.01, 10

LANES = 128
SUBLANES = 8
HALO = 16
CONV_CB = 256
VMEM_LIMIT = 56 * 1024 * 1024

AB_Z, AB_HQ, AB_HF, AB_HI, AB_HG, AB_XBC, AB_DT, AB_COLS = 0, 512, 1024, 1536, 2048, 2560, 3328, 3456
CD_Q, CD_GATE, CD_XR, CD_K, CD_V, CD_COLS = 0, 512, 1024, 1536, 1664, 1792


def _cp(sem=None):
    kw = dict(vmem_limit_bytes=VMEM_LIMIT)
    if sem is not None:
        kw["dimension_semantics"] = sem
    return pltpu.CompilerParams(**kw)


def _dot(a, b, dims=(((1,), (0,)), ((), ())), precision=None):
    return lax.dot_general(a, b, dims, precision=precision, preferred_element_type=F32)


NN = (((1,), (0,)), ((), ()))
NT = (((1,), (1,)), ((), ()))
TN = (((0,), (0,)), ((), ()))


@functools.partial(jax.custom_vjp, nondiff_argnums=(2,))
def _bdot(a, b, dims=NN):
    return _dot(a.astype(BF16), b.astype(BF16), dims)


def _bdot_fwd(a, b, dims):
    return _bdot(a, b, dims), (a, b)


def _bdot_bwd(dims, res, ct):
    a, b = res
    ab, bb, cb = a.astype(BF16), b.astype(BF16), ct.astype(BF16)
    if dims == NN:
        da, db = _dot(cb, bb, NT), _dot(ab, cb, TN)
    elif dims == NT:
        da, db = _dot(cb, bb, NN), _dot(cb, ab, TN)
    else:
        da, db = _dot(bb, cb, NT), _dot(ab, cb, NN)
    return da.astype(a.dtype), db.astype(b.dtype)


_bdot.defvjp(_bdot_fwd, _bdot_bwd)


WIDE_TILES = (1408, 1152, 1024, 896, 768, 512, 384, 256, 128)
MM_BLOCK_BYTES = 6 * 1024 * 1024


def _pick(n, cands):
    for c in cands:
        if n % c == 0:
            return c
    return n


def _mm(pairs, n_out, out_dtype, name, trans_b=False, tm=512):
    m = pairs[0][0].shape[0]
    a_row_bytes = sum(p[4] * p[0].dtype.itemsize for p in pairs)
    b_col_bytes = sum(p[4] * p[2].dtype.itemsize for p in pairs)
    tm = min(2 * tm if 2 * tm * a_row_bytes <= 2 * MM_BLOCK_BYTES else tm, m)
    tn = _pick(n_out, [c for c in WIDE_TILES if c * b_col_bytes <= MM_BLOCK_BYTES])
    n = len(pairs)

    def kern(*refs):
        acc = None
        for i in range(n):
            p = _bdot(refs[i][...], refs[n + i][...], NT if trans_b else NN)
            acc = p if acc is None else acc + p
        refs[2 * n][...] = acc.astype(out_dtype)

    in_specs = []
    for a, acb, b, bcb, k in pairs:
        in_specs.append(pl.BlockSpec((tm, k), functools.partial(lambda i, j, c: (i, c), c=acb)))
    for a, acb, b, bcb, k in pairs:
        if trans_b:
            in_specs.append(pl.BlockSpec((tn, k), functools.partial(lambda i, j, c: (j, c), c=bcb)))
        else:
            in_specs.append(pl.BlockSpec((k, tn), functools.partial(lambda i, j, c: (c, j), c=bcb)))
    return pl.pallas_call(
        kern, name=name, grid=(m // tm, n_out // tn), in_specs=in_specs,
        out_specs=pl.BlockSpec((tm, tn), lambda i, j: (i, j)),
        out_shape=jax.ShapeDtypeStruct((m, n_out), out_dtype),
        compiler_params=_cp(("parallel", "arbitrary")),
    )(*[p[0] for p in pairs], *[p[2] for p in pairs])


def _mm_tn(a, a_cb, ka, g, g_cb, ng, name, tm=2048):
    m = a.shape[0]
    tm = min(tm, m)
    tk = _pick(ka, (1024, 1408, 512, 256, 128))
    tn = _pick(ng, WIDE_TILES)
    nm = m // tm

    def kern(a_ref, g_ref, o_ref):
        p = _bdot(a_ref[...], g_ref[...], TN)

        @pl.when(pl.program_id(2) == 0)
        def _():
            o_ref[...] = p

        @pl.when(pl.program_id(2) > 0)
        def _():
            o_ref[...] += p

    ak, gk = ka // tk, ng // tn
    return pl.pallas_call(
        kern, name=name, grid=(ak, gk, nm),
        in_specs=[pl.BlockSpec((tm, tk), lambda i, j, r: (r, a_cb * ak + i)),
                  pl.BlockSpec((tm, tn), lambda i, j, r: (r, g_cb * gk + j))],
        out_specs=pl.BlockSpec((tk, tn), lambda i, j, r: (i, j)),
        out_shape=jax.ShapeDtypeStruct((ka, ng), F32),
        compiler_params=_cp(("parallel", "parallel", "arbitrary")),
    )(a, g)


def _mm_ln(pairs, x, g, b, name, tm=512):
    m, d = x.shape
    tm = min(tm, m)
    n = len(pairs)

    def kern(*refs):
        x_ref, g_ref, b_ref, y_ref, yb_ref, r_ref = refs[2 * n:]
        r = ALPHA * x_ref[...]
        for i in range(n):
            r = r + _bdot(refs[i][...], refs[n + i][...])
        mu = jnp.mean(r, -1, keepdims=True)
        xc = r - mu
        var = jnp.mean(xc * xc, -1, keepdims=True)
        y = xc * lax.rsqrt(var + LN_EPS) * g_ref[...] + b_ref[...]
        y_ref[...] = y
        yb_ref[...] = y.astype(BF16)
        r_ref[...] = r

    row = pl.BlockSpec((tm, d), lambda i: (i, 0))
    vec = pl.BlockSpec((1, d), lambda i: (0, 0))
    in_specs = [pl.BlockSpec((tm, k), functools.partial(lambda i, c: (i, c), c=acb)) for _, acb, _, _, k in pairs]
    in_specs += [pl.BlockSpec((k, d), functools.partial(lambda i, c: (c, 0), c=bcb)) for _, _, _, bcb, k in pairs]
    return pl.pallas_call(
        kern, name=name, grid=(m // tm,), in_specs=in_specs + [row, vec, vec], out_specs=[row, row, row],
        out_shape=[jax.ShapeDtypeStruct((m, d), F32), jax.ShapeDtypeStruct((m, d), BF16), jax.ShapeDtypeStruct((m, d), F32)],
        compiler_params=_cp(("parallel",)),
    )(*[p[0] for p in pairs], *[p[2] for p in pairs], x, g.reshape(1, d), b.reshape(1, d))


def _ln_bwd(dya, dyb, ca, cb, r, g, name, tm=256):
    s, d = r.shape
    tm = min(tm, s)

    def kern(a_ref, b_ref, r_ref, g_ref, dr_ref, dg_ref, db_ref):
        dy = ca * a_ref[...] + cb * b_ref[...]
        rr = r_ref[...]
        mu = jnp.mean(rr, -1, keepdims=True)
        xc = rr - mu
        var = jnp.mean(xc * xc, -1, keepdims=True)
        rstd = lax.rsqrt(var + LN_EPS)
        xhat = xc * rstd
        dxh = dy * g_ref[...]
        dr_ref[...] = rstd * (dxh - jnp.mean(dxh, -1, keepdims=True) - xhat * jnp.mean(dxh * xhat, -1, keepdims=True))
        dg = jnp.sum(dy * xhat, 0, keepdims=True)
        db = jnp.sum(dy, 0, keepdims=True)

        @pl.when(pl.program_id(0) == 0)
        def _():
            dg_ref[...] = dg
            db_ref[...] = db

        @pl.when(pl.program_id(0) > 0)
        def _():
            dg_ref[...] += dg
            db_ref[...] += db

    row = pl.BlockSpec((tm, d), lambda i: (i, 0))
    vec = pl.BlockSpec((1, d), lambda i: (0, 0))
    dr, dg, db = pl.pallas_call(
        kern, name=name, grid=(s // tm,), in_specs=[row, row, row, vec], out_specs=[row, vec, vec],
        out_shape=[jax.ShapeDtypeStruct((s, d), F32), jax.ShapeDtypeStruct((1, d), F32), jax.ShapeDtypeStruct((1, d), F32)],
        compiler_params=_cp(("arbitrary",)),
    )(dya, dyb, r, g.reshape(1, d))
    return dr, dg[0], db[0]


def _sqerr(y, t, name, tm=256):
    s, d = y.shape
    tm = min(tm, s)

    def kern(y_ref, t_ref, o_ref):
        e = y_ref[...] - t_ref[...]
        p = jnp.sum(e * e, 0, keepdims=True)

        @pl.when(pl.program_id(0) == 0)
        def _():
            o_ref[...] = p

        @pl.when(pl.program_id(0) > 0)
        def _():
            o_ref[...] += p

    row = pl.BlockSpec((tm, d), lambda i: (i, 0))
    return pl.pallas_call(
        kern, name=name, grid=(s // tm,), in_specs=[row, row], out_specs=pl.BlockSpec((1, d), lambda i: (0, 0)),
        out_shape=jax.ShapeDtypeStruct((1, d), F32), compiler_params=_cp(("arbitrary",)),
    )(y, t)


def _axpby(a, b, ca, cb, name, tm=256):
    s, d = a.shape
    tm = min(tm, s)

    def kern(a_ref, b_ref, o_ref):
        o_ref[...] = ca * a_ref[...] + cb * b_ref[...]

    row = pl.BlockSpec((tm, d), lambda i: (i, 0))
    return pl.pallas_call(
        kern, name=name, grid=(s // tm,), in_specs=[row, row], out_specs=row,
        out_shape=jax.ShapeDtypeStruct((s, d), F32), compiler_params=_cp(("parallel",)),
    )(a, b)


def _shift_down(xx, s, n):
    if s == 0:
        return xx[HALO:HALO + n]
    return pltpu.roll(xx, s, axis=0)[HALO:HALO + n]


def _shift_up(yy, s, n):
    if s == 0:
        return yy[0:n]
    return pltpu.roll(yy, yy.shape[0] - s, axis=0)[0:n]


def _prev_halo(tm):
    return lambda i: jnp.maximum(i * (tm // HALO) - 1, 0)


def _dwconv_fwd(x, x_cb0, c, w8, b, k, name, tm=512):
    s = x.shape[0]
    tm = min(tm, s)
    ph = _prev_halo(tm)

    def kern(x_ref, h_ref, w_ref, b_ref, y_ref):
        halo = jnp.where(pl.program_id(0) == 0, 0.0, h_ref[...].astype(F32))
        xx = jnp.concatenate([halo, x_ref[...].astype(F32)], axis=0)
        w = w_ref[...]
        acc = b_ref[...] + w[k - 1:k] * xx[HALO:]
        for j in range(k - 1):
            acc = acc + w[j:j + 1] * _shift_down(xx, k - 1 - j, tm)
        y_ref[...] = acc

    return pl.pallas_call(
        kern, name=name, grid=(s // tm, c // CONV_CB),
        in_specs=[pl.BlockSpec((tm, CONV_CB), lambda i, j: (i, x_cb0 + j)),
                  pl.BlockSpec((HALO, CONV_CB), lambda i, j: (ph(i), x_cb0 + j)),
                  pl.BlockSpec((SUBLANES, CONV_CB), lambda i, j: (0, j)),
                  pl.BlockSpec((1, CONV_CB), lambda i, j: (0, j))],
        out_specs=pl.BlockSpec((tm, CONV_CB), lambda i, j: (i, j)),
        out_shape=jax.ShapeDtypeStruct((s, c), F32), compiler_params=_cp(("parallel", "parallel")),
    )(x, x, w8, b.reshape(1, c))


def _dwconv_bwd(dy, x, x_cb0, c, w8, k, name, tm=512):
    s = x.shape[0]
    tm = min(tm, s)
    nt = s // tm
    ph = _prev_halo(tm)
    nh = lambda i: jnp.minimum((i + 1) * (tm // HALO), s // HALO - 1)

    def kern(dy_ref, dyn_ref, x_ref, h_ref, w_ref, dx_ref, dw_ref, db_ref):
        i = pl.program_id(1)
        halo = jnp.where(i == 0, 0.0, h_ref[...].astype(F32))
        xx = jnp.concatenate([halo, x_ref[...].astype(F32)], axis=0)
        dyt = dy_ref[...]
        nxt = jnp.where(i == nt - 1, 0.0, dyn_ref[...])
        dyy = jnp.concatenate([dyt, nxt], axis=0)
        w = w_ref[...]
        dx = w[k - 1:k] * dyt
        rows = [jnp.sum(dyt * _shift_down(xx, k - 1 - j, tm), 0, keepdims=True) for j in range(k)]
        for j in range(k - 1):
            dx = dx + w[j:j + 1] * _shift_up(dyy, k - 1 - j, tm)
        dx_ref[...] = dx.astype(BF16)
        dw = jnp.concatenate(rows + [jnp.zeros((SUBLANES - k, CONV_CB), F32)], axis=0)
        db = jnp.sum(dyt, 0, keepdims=True)

        @pl.when(i == 0)
        def _():
            dw_ref[...] = dw
            db_ref[...] = db

        @pl.when(i > 0)
        def _():
            dw_ref[...] += dw
            db_ref[...] += db

    dx, dw, db = pl.pallas_call(
        kern, name=name, grid=(c // CONV_CB, nt),
        in_specs=[pl.BlockSpec((tm, CONV_CB), lambda j, i: (i, j)),
                  pl.BlockSpec((HALO, CONV_CB), lambda j, i: (nh(i), j)),
                  pl.BlockSpec((tm, CONV_CB), lambda j, i: (i, x_cb0 + j)),
                  pl.BlockSpec((HALO, CONV_CB), lambda j, i: (ph(i), x_cb0 + j)),
                  pl.BlockSpec((SUBLANES, CONV_CB), lambda j, i: (0, j))],
        out_specs=[pl.BlockSpec((tm, CONV_CB), lambda j, i: (i, j)),
                   pl.BlockSpec((SUBLANES, CONV_CB), lambda j, i: (0, j)),
                   pl.BlockSpec((1, CONV_CB), lambda j, i: (0, j))],
        out_shape=[jax.ShapeDtypeStruct((s, c), BF16), jax.ShapeDtypeStruct((SUBLANES, c), F32),
                   jax.ShapeDtypeStruct((1, c), F32)],
        compiler_params=_cp(("parallel", "arbitrary")),
    )(dy, dy, x, x, w8)
    return dx, dw, db[0]


def _silu(x):
    return x * jax.nn.sigmoid(x)


def _ffn_gate_fwd(hu, w8, b, name, tm=512):
    s = hu.shape[0]
    tm = min(tm, s)
    nb = FFN_DIM // CONV_CB
    ph = _prev_halo(tm)
    k = 3

    def conv(x_ref, h_ref, w_ref, b_ref):
        halo = jnp.where(pl.program_id(0) == 0, 0.0, h_ref[...].astype(F32))
        xx = jnp.concatenate([halo, x_ref[...].astype(F32)], axis=0)
        w = w_ref[...]
        acc = b_ref[...] + w[k - 1:k] * xx[HALO:]
        for j in range(k - 1):
            acc = acc + w[j:j + 1] * _shift_down(xx, k - 1 - j, tm)
        return acc

    def kern(g_ref, gh_ref, u_ref, uh_ref, wg_ref, wu_ref, bg_ref, bu_ref, a_ref, hg_ref, hu_ref):
        g = conv(g_ref, gh_ref, wg_ref, bg_ref)
        u = conv(u_ref, uh_ref, wu_ref, bu_ref)
        a_ref[...] = (_silu(g) * u).astype(BF16)
        hg_ref[...] = g.astype(BF16)
        hu_ref[...] = u.astype(BF16)

    main = lambda off: pl.BlockSpec((tm, CONV_CB), lambda i, j: (i, off + j))
    halo = lambda off: pl.BlockSpec((HALO, CONV_CB), lambda i, j: (ph(i), off + j))
    wsp = lambda off: pl.BlockSpec((SUBLANES, CONV_CB), lambda i, j: (0, off + j))
    bsp = lambda off: pl.BlockSpec((1, CONV_CB), lambda i, j: (0, off + j))
    b2 = b.reshape(1, 2 * FFN_DIM)
    half = jax.ShapeDtypeStruct((s, FFN_DIM), BF16)
    return pl.pallas_call(
        kern, name=name, grid=(s // tm, nb),
        in_specs=[main(0), halo(0), main(nb), halo(nb), wsp(0), wsp(nb), bsp(0), bsp(nb)],
        out_specs=[main(0), main(0), main(0)], out_shape=[half, half, half],
        compiler_params=_cp(("parallel", "parallel")),
    )(hu, hu, hu, hu, w8, w8, b2, b2)


def _ffn_gate_bwd(hu, hg, hv, da, w8, name, tm=512):
    s = hu.shape[0]
    tm = min(tm, s)
    nt = s // tm
    nb = FFN_DIM // CONV_CB
    nh = lambda i: jnp.minimum((i + 1) * (tm // HALO), s // HALO - 1)
    k = 3

    def kern(xg_ref, xu_ref, g_ref, gn_ref, u_ref, un_ref, da_ref, dan_ref, wg_ref, wu_ref,
             dg_ref, du_ref, dwg_ref, dwu_ref, dbg_ref, dbu_ref):
        i = pl.program_id(1)

        def ext(x_ref, n_ref):
            return jnp.concatenate([x_ref[...].astype(F32), n_ref[...].astype(F32)], axis=0)

        g, u = ext(g_ref, gn_ref), ext(u_ref, un_ref)
        dae = jnp.concatenate([da_ref[...].astype(F32), jnp.where(i == nt - 1, 0.0, dan_ref[...].astype(F32))], axis=0)
        sg = jax.nn.sigmoid(g)
        dhg = dae * u * (sg * (1.0 + g * (1.0 - sg)))
        dhu = dae * (g * sg)

        def back(dh, x, w):
            taps = [dh[0:tm] if j == k - 1 else _shift_up(dh, k - 1 - j, tm) for j in range(k)]
            dx = w[0:1] * taps[0]
            for j in range(1, k):
                dx = dx + w[j:j + 1] * taps[j]
            rows = [jnp.sum(taps[j] * x, 0, keepdims=True) for j in range(k)]
            dw = jnp.concatenate(rows + [jnp.zeros((SUBLANES - k, CONV_CB), F32)], axis=0)
            return dx, dw, jnp.sum(dh[0:tm], 0, keepdims=True)

        dxg, dwg, dbg = back(dhg, xg_ref[...].astype(F32), wg_ref[...])
        dxu, dwu, dbu = back(dhu, xu_ref[...].astype(F32), wu_ref[...])
        dg_ref[...] = dxg.astype(BF16)
        du_ref[...] = dxu.astype(BF16)

        @pl.when(i == 0)
        def _():
            dwg_ref[...] = dwg
            dwu_ref[...] = dwu
            dbg_ref[...] = dbg
            dbu_ref[...] = dbu

        @pl.when(i > 0)
        def _():
            dwg_ref[...] += dwg
            dwu_ref[...] += dwu
            dbg_ref[...] += dbg
            dbu_ref[...] += dbu

    main = lambda off: pl.BlockSpec((tm, CONV_CB), lambda j, i: (i, off + j))
    nxt = lambda off: pl.BlockSpec((HALO, CONV_CB), lambda j, i: (nh(i), off + j))
    wsp = lambda off: pl.BlockSpec((SUBLANES, CONV_CB), lambda j, i: (0, off + j))
    bsp = lambda off: pl.BlockSpec((1, CONV_CB), lambda j, i: (0, off + j))
    outs = pl.pallas_call(
        kern, name=name, grid=(nb, nt),
        in_specs=[main(0), main(nb), main(0), nxt(0), main(0), nxt(0), main(0), nxt(0), wsp(0), wsp(nb)],
        out_specs=[main(0), main(0), wsp(0), wsp(0), bsp(0), bsp(0)],
        out_shape=[jax.ShapeDtypeStruct((s, FFN_DIM), BF16), jax.ShapeDtypeStruct((s, FFN_DIM), BF16),
                   jax.ShapeDtypeStruct((SUBLANES, FFN_DIM), F32), jax.ShapeDtypeStruct((SUBLANES, FFN_DIM), F32),
                   jax.ShapeDtypeStruct((1, FFN_DIM), F32), jax.ShapeDtypeStruct((1, FFN_DIM), F32)],
        compiler_params=_cp(("parallel", "arbitrary")),
    )(hu, hu, hg, hg, hv, hv, da, da, w8, w8)
    dg, du, dwg, dwu, dbg, dbu = outs
    return dg, du, jnp.concatenate([dwg, dwu], axis=1), jnp.concatenate([dbg[0], dbu[0]])


def _iota(shape, dim):
    return lax.broadcasted_iota(jnp.int32, shape, dim)


def _in_range(idx, start, size):
    return jnp.logical_and(idx >= start, idx < start + size)


@functools.partial(jax.custom_vjp, nondiff_argnums=(2, 3))
def _zdot(a, b, dims, z):
    zz, x = ((a, b) if z == 0 else (b, a))
    zz = zz.astype(BF16)
    xh = x.astype(BF16)
    r1 = x - xh.astype(F32)
    xm = r1.astype(BF16)
    xl = (r1 - xm.astype(F32)).astype(BF16)
    f = (lambda p: _dot(zz, p, dims)) if z == 0 else (lambda p: _dot(p, zz, dims))
    return f(xh) + f(xm) + f(xl)


def _zdot_fwd(a, b, dims, z):
    return _zdot(a, b, dims, z), (a if z == 0 else b)


def _zdot_bwd(dims, z, zz, ct):
    zb = zz.astype(BF16)
    ch = ct.astype(BF16)
    cm = (ct - ch.astype(F32)).astype(BF16)
    if dims == NN and z == 0:
        f = lambda c: _dot(zb, c, TN)
    elif dims == NN and z == 1:
        f = lambda c: _dot(c, zb, NT)
    else:
        f = lambda c: _dot(c, zb, TN)
    dx = f(ch) + f(cm)
    return (jnp.zeros_like(zz), dx) if z == 0 else (dx, jnp.zeros_like(zz))


_zdot.defvjp(_zdot_fwd, _zdot_bwd)


def _expand8(v, e):
    return jnp.sum(_zdot(jnp.broadcast_to(v, (SUBLANES, v.shape[1])), e, NN, 1), 0, keepdims=True) * (1.0 / SUBLANES)


def _ssd_consts():
    tri = np.tril(np.ones((SSD_CHUNK, SSD_CHUNK), np.float32))
    e = np.zeros((LANES, SSD_D_INNER), np.float32)
    for h in range(SSD_HEADS):
        e[h, 64 * h:64 * h + 64] = 1.0
    bd = np.zeros((LANES, SSD_D_INNER), np.float32)
    bd[:64, :256] = 1.0
    bd[64:, 256:] = 1.0
    return jnp.asarray(tri), jnp.asarray(e), jnp.asarray(bd)


def _ssd_chunk(tri, e, bd, z, xs_p, bm_p, cm_p, dtr, st, dtb, alog, dsk, nw):
    n = z.shape[0]
    lane128, lane512 = _iota((1, LANES), 1), _iota((1, SSD_D_INNER), 1)
    sub8 = _iota((SUBLANES, 1), 0)
    tril = _iota((n, 1), 0) >= _iota((1, n), 1)
    head_rows = jnp.where(sub8 == lane128, 1.0, 0.0)
    xs, bm, cm = _silu(xs_p), _silu(bm_p), _silu(cm_p)
    dt = jax.nn.softplus(dtr + dtb)
    da = dt * (-jnp.exp(alog))
    acs = _zdot(tri, da, NN, 0)
    acst = _zdot(head_rows, acs, NT, 0)
    xc = xs * _zdot(dt, e, NN, 1)
    tot = jnp.sum(da, 0, keepdims=True)
    y = _bdot(cm, st) * _zdot(jnp.exp(acs), e, NN, 1)
    st_new = st * _expand8(jnp.exp(tot), e) + bd * _bdot(bm, xc * _zdot(jnp.exp(tot - acs), e, NN, 1), TN)
    for g in range(2):
        cb = _bdot(jnp.where(_in_range(lane128, 64 * g, 64), cm, 0.0), bm, NT)
        for h in range(4 * g, 4 * g + 4):
            col = jnp.sum(jnp.where(lane128 == h, acs, 0.0), 1, keepdims=True)
            row = jnp.sum(jnp.where(sub8 == h, acst, 0.0), 0, keepdims=True)
            dm = jnp.where(tril, jnp.exp(jnp.minimum(col - row, 0.0)), 0.0)
            y = y + _bdot(cb * dm, jnp.where(_in_range(lane512, 64 * h, 64), xc, 0.0))
    y = (y + _expand8(dsk, e) * xs) * _silu(z)
    ysq = y * y
    g0 = lane512 < 256
    ms0 = jnp.sum(jnp.where(g0, ysq, 0.0), 1, keepdims=True) * (1.0 / 256)
    ms1 = jnp.sum(jnp.where(g0, 0.0, ysq), 1, keepdims=True) * (1.0 / 256)
    rs = jnp.where(g0, lax.rsqrt(ms0 + RMS_EPS), lax.rsqrt(ms1 + RMS_EPS))
    return y * rs * nw, st_new


def _ssd_specs(rev, nc):
    ci = (lambda i: nc - 1 - i) if rev else (lambda i: i)
    n = SSD_CHUNK
    act = [pl.BlockSpec((n, 512), lambda i: (ci(i), AB_Z // 512)),
           pl.BlockSpec((n, 512), lambda i: (ci(i), 0)),
           pl.BlockSpec((n, LANES), lambda i: (ci(i), 4)),
           pl.BlockSpec((n, LANES), lambda i: (ci(i), 5)),
           pl.BlockSpec((n, LANES), lambda i: (ci(i), AB_DT // LANES))]
    const = [pl.BlockSpec((n, n), lambda i: (0, 0)), pl.BlockSpec((LANES, 512), lambda i: (0, 0)),
             pl.BlockSpec((LANES, 512), lambda i: (0, 0))]
    par = [pl.BlockSpec((1, LANES), lambda i: (0, 0))] * 3 + [pl.BlockSpec((1, 512), lambda i: (0, 0))]
    return ci, act, const, par


def _ssd_fwd(h, xbcc, dtb, alog, dsk, nw, name):
    s = h.shape[0]
    nc = s // SSD_CHUNK
    ci, act, const, par = _ssd_specs(False, nc)

    def kern(z_ref, xs_ref, bm_ref, cm_ref, dt_ref, tri_ref, e_ref, bd_ref, dtb_ref, al_ref, dsk_ref, nw_ref,
             y_ref, sts_ref, st):
        @pl.when(pl.program_id(0) == 0)
        def _():
            st[...] = jnp.zeros_like(st)

        sts_ref[0] = st[...]
        y, stn = _ssd_chunk(tri_ref[...], e_ref[...], bd_ref[...], z_ref[...], xs_ref[...], bm_ref[...], cm_ref[...],
                            dt_ref[...], st[...], dtb_ref[...], al_ref[...], dsk_ref[...], nw_ref[...])
        y_ref[...] = y.astype(BF16)
        st[...] = stn

    return pl.pallas_call(
        kern, name=name, grid=(nc,), in_specs=act + const + par,
        out_specs=[pl.BlockSpec((SSD_CHUNK, 512), lambda i: (i, 0)), pl.BlockSpec((1, LANES, 512), lambda i: (i, 0, 0))],
        out_shape=[jax.ShapeDtypeStruct((s, 512), BF16), jax.ShapeDtypeStruct((nc, LANES, 512), F32)],
        scratch_shapes=[pltpu.VMEM((LANES, 512), F32)], compiler_params=_cp(("arbitrary",)),
    )(h, xbcc, xbcc, xbcc, h, *_ssd_consts(), dtb, alog, dsk, nw)


def _acc(ref, val, first):
    @pl.when(first)
    def _():
        ref[...] = val

    @pl.when(jnp.logical_not(first))
    def _():
        ref[...] += val


def _ssd_bwd(dy, dy_cb, h, xbcc, sts, dtb, alog, dsk, nw, name):
    s = h.shape[0]
    nc = s // SSD_CHUNK
    n = SSD_CHUNK
    ci, act, const, par = _ssd_specs(True, nc)

    def kern(dy_ref, sts_ref, z_ref, xs_ref, bm_ref, cm_ref, dt_ref, tri_ref, e_ref, bd_ref, dtb_ref, al_ref, dsk_ref,
             nw_ref, dz_ref, dx_ref, ddt_ref, ddtb_ref, dal_ref, ddsk_ref, dnw_ref, dst):
        first = pl.program_id(0) == 0

        @pl.when(first)
        def _():
            dst[...] = jnp.zeros_like(dst)

        f = functools.partial(_ssd_chunk, tri_ref[...], e_ref[...], bd_ref[...])
        _, pull = jax.vjp(f, z_ref[...], xs_ref[...], bm_ref[...], cm_ref[...], dt_ref[...], sts_ref[0],
                          dtb_ref[...], al_ref[...], dsk_ref[...], nw_ref[...])
        dz, dxs, dbm, dcm, ddt, dsti, ddtb, dal, ddsk, dnw = pull((dy_ref[...], dst[...]))
        dz_ref[...] = dz.astype(BF16)
        dx_ref[:, 0:512] = dxs
        dx_ref[:, 512:640] = dbm
        dx_ref[:, 640:768] = dcm
        ddt_ref[...] = ddt.astype(BF16)
        dst[...] = dsti
        _acc(ddtb_ref, ddtb, first)
        _acc(dal_ref, dal, first)
        _acc(ddsk_ref, ddsk, first)
        _acc(dnw_ref, dnw, first)

    vec = pl.BlockSpec((1, LANES), lambda i: (0, 0))
    return pl.pallas_call(
        kern, name=name, grid=(nc,),
        in_specs=[pl.BlockSpec((n, 512), lambda i: (ci(i), dy_cb)), pl.BlockSpec((1, LANES, 512), lambda i: (ci(i), 0, 0))]
        + act + const + par,
        out_specs=[pl.BlockSpec((n, 512), lambda i: (ci(i), 0)), pl.BlockSpec((n, SSD_CONV_DIM), lambda i: (ci(i), 0)),
                   pl.BlockSpec((n, LANES), lambda i: (ci(i), 0)), vec, vec, vec, pl.BlockSpec((1, 512), lambda i: (0, 0))],
        out_shape=[jax.ShapeDtypeStruct((s, 512), BF16), jax.ShapeDtypeStruct((s, SSD_CONV_DIM), F32),
                   jax.ShapeDtypeStruct((s, LANES), BF16), jax.ShapeDtypeStruct((1, LANES), F32),
                   jax.ShapeDtypeStruct((1, LANES), F32), jax.ShapeDtypeStruct((1, LANES), F32),
                   jax.ShapeDtypeStruct((1, 512), F32)],
        scratch_shapes=[pltpu.VMEM((LANES, 512), F32)], compiler_params=_cp(("arbitrary",)),
    )(dy, sts, h, xbcc, xbcc, xbcc, h, *_ssd_consts(), dtb, alog, dsk, nw)


HG_LEVELS = 6


HG_SUMS = 1 + 2 * HG_LEVELS


def _hg_consts():
    n = HG_CHUNK
    t = np.arange(n)
    mats = [np.tril(np.ones((n, n), np.float32))]
    dq, dk, mk = [], [], []
    for lv in range(HG_LEVELS):
        b = 1 << lv
        start = (t // b) * b
        end = start + b - 1
        dq.append(((t[None, :] >= start[:, None]) & (t[None, :] <= t[:, None])).astype(np.float32))
        dk.append(((t[None, :] > t[:, None]) & (t[None, :] <= end[:, None])).astype(np.float32))
        mk.append((((t // b) % 2 == 1)[:, None] & ((t // b)[None, :] == (t // b)[:, None] - 1)).astype(np.float32))
    c1 = np.concatenate(mats + dq + dk, axis=0)
    return (jnp.asarray(c1, BF16), jnp.asarray(np.concatenate([c1, c1, c1], axis=1), BF16), jnp.asarray(np.stack(mk)))


@functools.partial(jax.custom_vjp, nondiff_argnums=(1, 2))
def _split_grid(x, nr, nc):
    br, bc = x.shape[0] // nr, x.shape[1] // nc
    return tuple(x[i * br:(i + 1) * br, j * bc:(j + 1) * bc] for i in range(nr) for j in range(nc))


def _split_grid_fwd(x, nr, nc):
    return _split_grid(x, nr, nc), None


def _split_grid_bwd(nr, nc, _, cts):
    rows = [cts[i * nc] if nc == 1 else jnp.concatenate(cts[i * nc:(i + 1) * nc], axis=1) for i in range(nr)]
    return (rows[0] if nr == 1 else jnp.concatenate(rows, axis=0),)


_split_grid.defvjp(_split_grid_fwd, _split_grid_bwd)


@jax.custom_vjp
def _level_sums(c1, c3, g):
    gh = g.astype(BF16)
    r1 = g - gh.astype(F32)
    gm = r1.astype(BF16)
    gl = (r1 - gm.astype(F32)).astype(BF16)
    return _dot(c3, jnp.concatenate([gh, gm, gl], axis=0), NN)


def _level_sums_fwd(c1, c3, g):
    return _level_sums(c1, c3, g), (c1, c3)


def _level_sums_bwd(res, ct):
    c1, c3 = res
    ch = ct.astype(BF16)
    cm = (ct - ch.astype(F32)).astype(BF16)
    return jnp.zeros_like(c1), jnp.zeros_like(c3), _dot(c1, ch, TN) + _dot(c1, cm, TN)


_level_sums.defvjp(_level_sums_fwd, _level_sums_bwd)


def _hg_chunk(c1, c3, masks, hq, hf, hi, hg, sts, lb, nw):
    q = _silu(hq)
    g = jnp.log(lb + (1.0 - lb) * jax.nn.sigmoid(hf))
    k = (1.0 - lb) * jax.nn.sigmoid(-hf)
    parts = _split_grid(_level_sums(c1, c3, g), HG_SUMS, 1)
    bc = parts[0]
    tot = jnp.sum(g, 0, keepdims=True)
    qe = _split_grid(q * jnp.exp(bc), 1, HG_HEADS)
    kd = _split_grid(k * jnp.exp(tot - bc), 1, HG_HEADS)
    etot = _split_grid(jnp.exp(tot), 1, HG_HEADS)
    qh, kh, vh, gh = [_split_grid(a, 1, HG_HEADS) for a in (q, k, hi, hg)]
    att = [None] * HG_HEADS
    for lv in range(HG_LEVELS):
        qt = _split_grid(q * jnp.exp(parts[1 + lv]), 1, HG_HEADS)
        kt = _split_grid(k * jnp.exp(parts[1 + HG_LEVELS + lv]), 1, HG_HEADS)
        for hd in range(HG_HEADS):
            t = masks[lv] * _bdot(qt[hd], kt[hd], NT)
            att[hd] = t if att[hd] is None else att[hd] + t
    outs, stn = [], []
    for hd in range(HG_HEADS):
        o = (_bdot(att[hd], vh[hd]) + jnp.sum(qh[hd] * kh[hd], 1, keepdims=True) * vh[hd] + _bdot(qe[hd], sts[hd], NT))
        stn.append(sts[hd] * etot[hd] + _bdot(vh[hd], kd[hd], TN))
        outs.append(o * lax.rsqrt(jnp.mean(o * o, 1, keepdims=True) + RMS_EPS) * nw * _silu(gh[hd]))
    return tuple(outs), tuple(stn)


def _hg_specs(rev, nc):
    ci = (lambda i: nc - 1 - i) if rev else (lambda i: i)
    n = HG_CHUNK
    act = [pl.BlockSpec((n, 512), functools.partial(lambda i, c: (ci(i), c), c=c))
           for c in (AB_HQ // 512, AB_HF // 512, AB_HI // 512, AB_HG // 512)]
    other = [pl.BlockSpec((HG_SUMS * n, n), lambda i: (0, 0)), pl.BlockSpec((HG_SUMS * n, 3 * n), lambda i: (0, 0)),
             pl.BlockSpec((HG_LEVELS, n, n), lambda i: (0, 0, 0)), pl.BlockSpec((1, 512), lambda i: (0, 0)),
             pl.BlockSpec((1, LANES), lambda i: (0, 0))]
    return ci, act, other


def _head_rows(hd):
    return slice(LANES * hd, LANES * (hd + 1))


def _hg_fwd(h, lb, nw, name):
    s = h.shape[0]
    nc = s // HG_CHUNK
    ci, act, other = _hg_specs(False, nc)

    def kern(q_ref, f_ref, i_ref, g_ref, c1_ref, c3_ref, mk_ref, lb_ref, nw_ref, y_ref, sts_ref, st):
        @pl.when(pl.program_id(0) == 0)
        def _():
            st[...] = jnp.zeros_like(st)

        sts_ref[0] = st[...]
        masks = [mk_ref[lv] for lv in range(HG_LEVELS)]
        outs, stn = _hg_chunk(c1_ref[...], c3_ref[...], masks, q_ref[...], f_ref[...], i_ref[...], g_ref[...],
                              tuple(st[_head_rows(hd), :] for hd in range(HG_HEADS)), lb_ref[...], nw_ref[...])
        for hd in range(HG_HEADS):
            y_ref[:, _head_rows(hd)] = outs[hd].astype(BF16)
            st[_head_rows(hd), :] = stn[hd]

    return pl.pallas_call(
        kern, name=name, grid=(nc,), in_specs=act + other,
        out_specs=[pl.BlockSpec((HG_CHUNK, 512), lambda i: (i, 0)), pl.BlockSpec((1, 512, LANES), lambda i: (i, 0, 0))],
        out_shape=[jax.ShapeDtypeStruct((s, 512), BF16), jax.ShapeDtypeStruct((nc, 512, LANES), F32)],
        scratch_shapes=[pltpu.VMEM((512, LANES), F32)], compiler_params=_cp(("arbitrary",)),
    )(h, h, h, h, *_hg_consts(), lb, nw)


def _hg_bwd(dy, dy_cb, h, sts, lb, nw, name):
    s = h.shape[0]
    nc = s // HG_CHUNK
    n = HG_CHUNK
    ci, act, other = _hg_specs(True, nc)

    def kern(dy_ref, sts_ref, q_ref, f_ref, i_ref, g_ref, c1_ref, c3_ref, mk_ref, lb_ref, nw_ref,
             dq_ref, df_ref, di_ref, dg_ref, dlb_ref, dnw_ref, dst):
        first = pl.program_id(0) == 0

        @pl.when(first)
        def _():
            dst[...] = jnp.zeros_like(dst)

        masks = [mk_ref[lv] for lv in range(HG_LEVELS)]
        heads = range(HG_HEADS)
        _, pull = jax.vjp(functools.partial(_hg_chunk, c1_ref[...], c3_ref[...], masks), q_ref[...], f_ref[...], i_ref[...],
                          g_ref[...], tuple(sts_ref[0, _head_rows(hd), :] for hd in heads), lb_ref[...], nw_ref[...])
        dq, df, di, dg, dsti, dlb, dnw = pull((tuple(dy_ref[:, _head_rows(hd)] for hd in heads),
                                               tuple(dst[_head_rows(hd), :] for hd in heads)))
        dq_ref[...] = dq.astype(BF16)
        df_ref[...] = df.astype(BF16)
        di_ref[...] = di.astype(BF16)
        dg_ref[...] = dg.astype(BF16)
        for hd in heads:
            dst[_head_rows(hd), :] = dsti[hd]
        _acc(dlb_ref, dlb, first)
        _acc(dnw_ref, dnw, first)

    blk = pl.BlockSpec((n, 512), lambda i: (ci(i), 0))
    return pl.pallas_call(
        kern, name=name, grid=(nc,),
        in_specs=[pl.BlockSpec((n, 512), lambda i: (ci(i), dy_cb)), pl.BlockSpec((1, 512, LANES), lambda i: (ci(i), 0, 0))]
        + act + other,
        out_specs=[blk, blk, blk, blk, pl.BlockSpec((1, 512), lambda i: (0, 0)), pl.BlockSpec((1, LANES), lambda i: (0, 0))],
        out_shape=[jax.ShapeDtypeStruct((s, 512), BF16)] * 4 + [jax.ShapeDtypeStruct((1, 512), F32),
                                                                jax.ShapeDtypeStruct((1, LANES), F32)],
        scratch_shapes=[pltpu.VMEM((512, LANES), F32)], compiler_params=_cp(("arbitrary",)),
    )(dy, sts, h, h, h, h, *_hg_consts(), lb, nw)


def _swa_consts():
    n = SWA_BLOCK
    r = np.arange(8 * n)
    t, blk = r % n, r // n
    lanes = np.arange(LANES)
    own = (t[:, None] >= lanes[None, :]).astype(np.float32)
    prev = (lanes[None, :] > t[:, None]).astype(np.float32)
    sel = (lanes[None, :] == (blk // 2 + 4 * (blk % 2))[:, None]).astype(np.float32)
    grp = ((lanes[None, :] // 64) == (blk % 2)[:, None]).astype(np.float32)
    return tuple(jnp.asarray(a) for a in (own, prev, sel, grp))


def _swa_block(own, prev, sel, grp, q, k0, v0, k1, v1, srow):
    qs = _split_grid(q, 1, 4)
    qa = jnp.concatenate([qs[p] for p in range(4) for _ in range(2)], axis=0) * grp
    s0 = jnp.where(prev > 0.0, _bdot(qa, k0, NT) * 0.125, MASK_VALUE)
    s1 = jnp.where(own > 0.0, _bdot(qa, k1, NT) * 0.125, MASK_VALUE)
    sk = jnp.sum(sel * srow, 1, keepdims=True)
    m = lax.stop_gradient(jnp.maximum(jnp.maximum(jnp.max(s0, 1, keepdims=True), jnp.max(s1, 1, keepdims=True)), sk))
    p0, p1 = jnp.exp(s0 - m), jnp.exp(s1 - m)
    inv = 1.0 / (jnp.sum(p0, 1, keepdims=True) + jnp.sum(p1, 1, keepdims=True) + jnp.exp(sk - m))
    o = _split_grid((_bdot(p0 * inv, v0) + _bdot(p1 * inv, v1)) * grp, 8, 1)
    return tuple(o[2 * p] + o[2 * p + 1] for p in range(4))


def _swa_specs():
    n = SWA_BLOCK
    pv = lambda i: jnp.maximum(i - 1, 0)
    mask = pl.BlockSpec((8 * n, LANES), lambda i: (0, 0))
    return [mask, mask, mask, mask, pl.BlockSpec((n, 512), lambda i: (i, CD_Q // 512)),
            pl.BlockSpec((n, LANES), lambda i: (pv(i), CD_K // LANES)), pl.BlockSpec((n, LANES), lambda i: (pv(i), CD_V // LANES)),
            pl.BlockSpec((n, LANES), lambda i: (i, CD_K // LANES)), pl.BlockSpec((n, LANES), lambda i: (i, CD_V // LANES)),
            pl.BlockSpec((1, LANES), lambda i: (0, 0))]


def _swa_args(refs, notfirst):
    own, prev, sel, grp = (r[...] for r in refs[:4])
    return (own, prev * notfirst.astype(F32), sel, grp), tuple(r[...] for r in refs[4:])


def _swa_fwd(hc, srow, name):
    s = hc.shape[0]
    n = SWA_BLOCK

    def kern(*refs):
        y_ref = refs[10]
        consts, acts = _swa_args(refs[:10], pl.program_id(0) > 0)
        outs = _swa_block(*consts, *acts)
        for p in range(4):
            y_ref[:, LANES * p:LANES * (p + 1)] = outs[p].astype(BF16)

    return pl.pallas_call(
        kern, name=name, grid=(s // n,), in_specs=_swa_specs(), out_specs=pl.BlockSpec((n, 512), lambda i: (i, 0)),
        out_shape=jax.ShapeDtypeStruct((s, 512), BF16), compiler_params=_cp(("parallel",)),
    )(*_swa_consts(), hc, hc, hc, hc, hc, srow)


def _swa_bwd(dy, dy_cb, hc, srow, name):
    s = hc.shape[0]
    n = SWA_BLOCK

    def kern(dy_ref, *refs):
        dq_ref, dk_ref, dv_ref, ds_ref = refs[10:]
        i = pl.program_id(0)

        @pl.when(i == 0)
        def _():
            dk_ref[...] = jnp.zeros_like(dk_ref)
            dv_ref[...] = jnp.zeros_like(dv_ref)
            ds_ref[...] = jnp.zeros_like(ds_ref)

        r1 = pl.ds(pl.multiple_of(i * n, n), n)
        r0 = pl.ds(pl.multiple_of(jnp.maximum(i - 1, 0) * n, n), n)
        consts, acts = _swa_args(refs[:10], i > 0)
        _, pull = jax.vjp(functools.partial(_swa_block, *consts), *acts)
        dq, dk0, dv0, dk1, dv1, dsr = pull(tuple(dy_ref[:, LANES * p:LANES * (p + 1)] for p in range(4)))
        dq_ref[...] = dq.astype(BF16)
        dk_ref[r0, :] += dk0
        dv_ref[r0, :] += dv0
        dk_ref[r1, :] += dk1
        dv_ref[r1, :] += dv1
        ds_ref[...] += dsr

    full = pl.BlockSpec((s, LANES), lambda i: (0, 0))
    return pl.pallas_call(
        kern, name=name, grid=(s // n,), in_specs=[pl.BlockSpec((n, 512), lambda i: (i, dy_cb))] + _swa_specs(),
        out_specs=[pl.BlockSpec((n, 512), lambda i: (i, 0)), full, full, pl.BlockSpec((1, LANES), lambda i: (0, 0))],
        out_shape=[jax.ShapeDtypeStruct((s, 512), BF16), jax.ShapeDtypeStruct((s, LANES), F32),
                   jax.ShapeDtypeStruct((s, LANES), F32), jax.ShapeDtypeStruct((1, LANES), F32)],
        compiler_params=_cp(("arbitrary",)),
    )(dy, *_swa_consts(), hc, hc, hc, hc, hc, srow)


RG_TILE = 256


def _neg_expm1(x):
    ser = x * (1 + x / 2 * (1 + x / 3 * (1 + x / 4 * (1 + x / 5 * (1 + x / 6 * (1 + x / 7 * (1 + x / 8)))))))
    return jnp.where(x > -0.25, -ser, 1.0 - jnp.exp(x))


def _rg_gates(xc, wa, wx, ba, bx, lam):
    r = jax.nn.sigmoid(_bdot(xc, wa) + ba)
    i = jax.nn.sigmoid(_bdot(xc, wx) + bx)
    log_a = -RG_C * r * jax.nn.softplus(-lam)
    u = jnp.sqrt(jnp.maximum(_neg_expm1(2.0 * log_a), 0.0)) * (i * xc)
    return jnp.exp(log_a), u


def _rg_out(hs, gate):
    return hs * jax.nn.gelu(gate)


def _rows_to_tile(rows):
    sub = _iota((SUBLANES, 1), 0)
    out = jnp.broadcast_to(rows[0], (SUBLANES, rows[0].shape[1]))
    for j in range(1, SUBLANES):
        out = jnp.where(sub == j, rows[j], out)
    return out


def _rg_fwd(xc, hc, wa, wx, ba, bx, lam, name):
    s = xc.shape[0]
    tm = min(RG_TILE, s)

    def kern(x_ref, g_ref, wa_ref, wx_ref, ba_ref, bx_ref, lam_ref, y_ref, h_ref, a_s, u_s, carry):
        @pl.when(pl.program_id(0) == 0)
        def _():
            carry[...] = jnp.zeros_like(carry)

        a, u = _rg_gates(x_ref[...], wa_ref[...], wx_ref[...], ba_ref[...], bx_ref[...], lam_ref[...])
        a_s[...] = a
        u_s[...] = u

        def body(gi, hp):
            r0 = pl.multiple_of(gi * SUBLANES, SUBLANES)
            ab, ub = a_s[pl.ds(r0, SUBLANES), :], u_s[pl.ds(r0, SUBLANES), :]
            rows = []
            for j in range(SUBLANES):
                hp = ab[j:j + 1] * hp + ub[j:j + 1]
                rows.append(hp)
            h_ref[pl.ds(r0, SUBLANES), :] = _rows_to_tile(rows)
            return hp

        carry[0:1, :] = lax.fori_loop(0, tm // SUBLANES, body, carry[0:1, :])
        y_ref[...] = _rg_out(h_ref[...], g_ref[...]).astype(BF16)

    row = pl.BlockSpec((tm, 512), lambda i: (i, 0))
    mat = pl.BlockSpec((512, 512), lambda i: (0, 0))
    vec = pl.BlockSpec((1, 512), lambda i: (0, 0))
    return pl.pallas_call(
        kern, name=name, grid=(s // tm,),
        in_specs=[row, pl.BlockSpec((tm, 512), lambda i: (i, CD_GATE // 512)), mat, mat, vec, vec, vec],
        out_specs=[row, row],
        out_shape=[jax.ShapeDtypeStruct((s, 512), BF16), jax.ShapeDtypeStruct((s, 512), F32)],
        scratch_shapes=[pltpu.VMEM((tm, 512), F32), pltpu.VMEM((tm, 512), F32), pltpu.VMEM((SUBLANES, 512), F32)],
        compiler_params=_cp(("arbitrary",)),
    )(xc, hc, wa, wx, ba, bx, lam)


def _rg_bwd(dy, dy_cb, xc, hc, hs, wa, wx, ba, bx, lam, name):
    s = xc.shape[0]
    tm = min(RG_TILE, s)
    nt = s // tm
    ti = lambda i: nt - 1 - i

    def kern(dy_ref, x_ref, g_ref, h_ref, hp_ref, wa_ref, wx_ref, ba_ref, bx_ref, lam_ref,
             dx_ref, dg_ref, dwa_ref, dwx_ref, dba_ref, dbx_ref, dlam_ref, a_s, d_s, g_s, cg, ca):
        first = pl.program_id(0) == 0

        @pl.when(first)
        def _():
            cg[...] = jnp.zeros_like(cg)
            ca[...] = jnp.zeros_like(ca)

        (a, _), pull = jax.vjp(_rg_gates, x_ref[...], wa_ref[...].astype(F32), wx_ref[...].astype(F32), ba_ref[...],
                               bx_ref[...], lam_ref[...])
        hs_t = h_ref[...]
        _, pull_out = jax.vjp(_rg_out, hs_t, g_ref[...])
        dho, dgate = pull_out(dy_ref[...])
        dg_ref[...] = dgate.astype(BF16)
        a_s[...] = a
        d_s[...] = dho

        def body(k, c):
            gn, an = c
            r0 = pl.multiple_of((tm // SUBLANES - 1 - k) * SUBLANES, SUBLANES)
            ab, db = a_s[pl.ds(r0, SUBLANES), :], d_s[pl.ds(r0, SUBLANES), :]
            rows = [None] * SUBLANES
            for j in range(SUBLANES - 1, -1, -1):
                gn = db[j:j + 1] + an * gn
                an = ab[j:j + 1]
                rows[j] = gn
            g_s[pl.ds(r0, SUBLANES), :] = _rows_to_tile(rows)
            return gn, an

        gn, an = lax.fori_loop(0, tm // SUBLANES, body, (cg[0:1, :], ca[0:1, :]))
        cg[0:1, :] = gn
        ca[0:1, :] = an
        g = g_s[...]
        halo = jnp.where(pl.program_id(0) == nt - 1, 0.0, hp_ref[...])
        hprev = pltpu.roll(jnp.concatenate([halo, hs_t], axis=0), 1, axis=0)[SUBLANES:]
        dxc, dwa, dwx, dba, dbx, dlam = pull((g * hprev, g))
        dx_ref[...] = dxc
        _acc(dwa_ref, dwa, first)
        _acc(dwx_ref, dwx, first)
        _acc(dba_ref, dba, first)
        _acc(dbx_ref, dbx, first)
        _acc(dlam_ref, dlam, first)

    row = pl.BlockSpec((tm, 512), lambda i: (ti(i), 0))
    mat = pl.BlockSpec((512, 512), lambda i: (0, 0))
    vec = pl.BlockSpec((1, 512), lambda i: (0, 0))
    hprev = pl.BlockSpec((SUBLANES, 512), lambda i: (jnp.maximum(ti(i) * (tm // SUBLANES) - 1, 0), 0))
    return pl.pallas_call(
        kern, name=name, grid=(nt,),
        in_specs=[pl.BlockSpec((tm, 512), lambda i: (ti(i), dy_cb)), row,
                  pl.BlockSpec((tm, 512), lambda i: (ti(i), CD_GATE // 512)), row, hprev, mat, mat, vec, vec, vec],
        out_specs=[row, row, mat, mat, vec, vec, vec],
        out_shape=[jax.ShapeDtypeStruct((s, 512), F32), jax.ShapeDtypeStruct((s, 512), BF16),
                   jax.ShapeDtypeStruct((512, 512), F32), jax.ShapeDtypeStruct((512, 512), F32),
                   jax.ShapeDtypeStruct((1, 512), F32), jax.ShapeDtypeStruct((1, 512), F32), jax.ShapeDtypeStruct((1, 512), F32)],
        scratch_shapes=[pltpu.VMEM((tm, 512), F32)] * 3 + [pltpu.VMEM((SUBLANES, 512), F32)] * 2,
        compiler_params=_cp(("arbitrary",)),
    )(dy, xc, hc, hs, hs, wa, wx, ba, bx, lam)


AB_REF = dict(z=(0, 512), xbc=(512, 768), dt=(1280, 8), hq=(1288, 512), hf=(1800, 512), hi=(2312, 512), hg=(2824, 512))
AB_INT = dict(z=AB_Z, hq=AB_HQ, hf=AB_HF, hi=AB_HI, hg=AB_HG, xbc=AB_XBC, dt=AB_DT)
CD_REF = dict(q=(0, 512), k=(512, 128), v=(640, 128), gate=(768, 512), xr=(1280, 512))


def _cols(w, start, size):
    return lax.slice_in_dim(w, start, start + size, axis=w.ndim - 1)


def _pad_lanes(v):
    return jnp.concatenate([v.astype(F32), jnp.zeros((LANES - v.shape[0],), F32)]).reshape(1, LANES)


def _taker(g, layer, axis):
    width = g.shape[2 + axis]

    def take(start, size):
        out = []
        while size > 0:
            k, off = divmod(start, width)
            n = min(size, width - off)
            out.append(lax.slice_in_dim(g[k, layer], off, off + n, axis=axis))
            start, size = start + n, size - n
        return out

    return take


def _ab_in_to_int(g, layer):
    take = _taker(g, layer, 1)
    parts = sum([take(*AB_REF[k]) for k in ("z", "hq", "hf", "hi", "hg", "xbc", "dt")], [])
    return jnp.concatenate(parts + [jnp.zeros((g.shape[2], AB_COLS - AB_DT - 8), g.dtype)], axis=1)


def _ab_in_to_ref(w):
    return jnp.concatenate([_cols(w, AB_INT[k], AB_REF[k][1]) for k in ("z", "xbc", "dt", "hq", "hf", "hi", "hg")], axis=-1)


def _pairs_to_ref(w, axis):
    sl = lambda i: lax.slice_in_dim(w, 64 * i, 64 * i + 64, axis=axis)
    return jnp.concatenate([sl(2 * h) for h in range(4)] + [sl(2 * h + 1) for h in range(4)], axis=axis)


PAIR_ORDER = tuple(h for p in range(4) for h in (p, 4 + p))


def _cd_in_to_int(g, layer):
    take = _taker(g, layer, 1)
    q = sum([take(64 * h, 64) for h in PAIR_ORDER], [])
    return jnp.concatenate(q + sum([take(*CD_REF[k]) for k in ("gate", "xr", "k", "v")], []), axis=1)


def _cd_out_to_int(g, layer):
    take = _taker(g, layer, 0)
    return jnp.concatenate(sum([take(64 * h, 64) for h in PAIR_ORDER], []) + take(512, 512), axis=0)


def _whole(g, layer, axis):
    return jnp.concatenate([g[k, layer] for k in range(4)], axis=axis)


def _cd_in_to_ref(w):
    q = _pairs_to_ref(_cols(w, CD_Q, 512), w.ndim - 1)
    return jnp.concatenate([q, _cols(w, CD_K, 128), _cols(w, CD_V, 128), _cols(w, CD_GATE, 512), _cols(w, CD_XR, 512)], axis=-1)


def _block_diag(w):
    tiled = jnp.concatenate([w.reshape(RG_WIDTH, 64)] * 8, axis=1)
    return tiled * _BLOCK_DIAG_MASK.astype(w.dtype)


_BLOCK_DIAG_MASK = np.kron(np.eye(8, dtype=np.float32), np.ones((64, 64), np.float32))


def _diag_blocks(m):
    return jnp.sum((m * _BLOCK_DIAG_MASK).reshape(8, 64, 8, 64), axis=2)


def _conv8(w):
    return jnp.concatenate([w, jnp.zeros((SUBLANES - w.shape[0], w.shape[1]), w.dtype)], axis=0)


def _hg_lower_bounds(hg_lower):
    sm = jax.nn.softmax(hg_lower.astype(F32), axis=0)
    return jnp.clip(jnp.cumsum(sm, axis=0) - sm[0], 0.0, 1.0)


def _local_step(x, target, w):
    s = x.shape[0]
    sv_w = [(_whole(w["ffn_w_up"], l, 1), _whole(w["ffn_w_down"], l, 0)) for l in range(DEPTH)]
    lb_all, lb_pull = jax.vjp(_hg_lower_bounds, w["hg_lower"])
    grads = {k: [None] * (v.shape[1] if k in BIG else v.shape[0]) for k, v in w.items()}
    saved = []
    xb = x.astype(BF16)
    for l in range(DEPTH):
        j = l // 2
        t = f"l{l}"
        sv = dict(x=x, xb=xb)
        if l % 2 == 0:
            win = _ab_in_to_int(w["ab_w_in"], j)
            wout = _whole(w["ab_w_out"], j, 0)
            h = _mm([(xb, 0, win, 0, D_MODEL)], AB_COLS, F32, t + "_in")
            w8 = _conv8(w["ssd_conv_w"][j])
            xbcc = _dwconv_fwd(h, AB_XBC // CONV_CB, SSD_CONV_DIM, w8, w["ssd_conv_b"][j], 4, t + "_conv")
            sp = (_pad_lanes(w["ssd_dt_bias"][j]), _pad_lanes(w["ssd_a_log"][j]), _pad_lanes(w["ssd_d"][j]),
                  w["ssd_norm_w"][j].reshape(1, 512))
            ya, ssts = _ssd_fwd(h, xbcc, *sp, t + "_ssd")
            hp = (lb_all[j].reshape(1, 512), w["hg_norm_w"][j].reshape(1, LANES))
            yb, hsts = _hg_fwd(h, *hp, t + "_hg")
            sv.update(win=win, wout=wout, h=h, w8=w8, xbcc=xbcc, sp=sp, ssts=ssts, hp=hp, hsts=hsts)
        else:
            win = _cd_in_to_int(w["cd_w_in"], j)
            wout = _cd_out_to_int(w["cd_w_out"], j)
            h = _mm([(xb, 0, win, 0, D_MODEL)], CD_COLS, F32, t + "_in")
            srow = _pad_lanes(w["swa_sinks"][j])
            ya = _swa_fwd(h, srow, t + "_swa")
            w8 = _conv8(w["rg_conv_w"][j])
            xc = _dwconv_fwd(h, CD_XR // CONV_CB, RG_WIDTH, w8, w["rg_conv_b"][j], 4, t + "_conv")
            rp = (_block_diag(w["rg_wa"][j]).astype(BF16), _block_diag(w["rg_wx"][j]).astype(BF16),
                  w["rg_ba"][j].reshape(1, 512), w["rg_bx"][j].reshape(1, 512), w["rg_lambda"][j].reshape(1, 512))
            yb, hs = _rg_fwd(xc, h, *rp, t + "_rg")
            sv.update(win=win, wout=wout, h=h, w8=w8, srow=srow, xc=xc, rp=rp, hs=hs)
        x1, x1b, r1 = _mm_ln([(ya, 0, wout, 0, 512), (yb, 0, wout, 1, 512)], x, w["ln_g"][l, 0], w["ln_b"][l, 0], t + "_out_ln")
        wup, wdown = sv_w[l]
        fw8 = _conv8(w["ffn_conv_w"][l])
        hu = _mm([(x1b, 0, wup, 0, D_MODEL)], 2 * FFN_DIM, BF16, t + "_up")
        a, hg, hv = _ffn_gate_fwd(hu, fw8, w["ffn_conv_b"][l], t + "_gate")
        x, xb, r2 = _mm_ln([(a, 0, wdown, 0, FFN_DIM)], x1, w["ln_g"][l, 1], w["ln_b"][l, 1], t + "_down_ln")
        sv.update(ya=ya, yb=yb, r1=r1, x1b=x1b, hu=hu, hg=hg, hv=hv, a=a, r2=r2, fw8=fw8)
        saved.append(sv)

    loss = 0.5 * jnp.sum(_sqerr(x, target, "sqerr")) / D_MODEL
    dya, dyb, ca, cb = x, target, 1.0 / D_MODEL, -1.0 / D_MODEL
    for l in range(DEPTH - 1, -1, -1):
        j = l // 2
        t = f"l{l}b"
        sv = saved[l]
        wup, wdown = sv_w[l]
        dr2, dg2, db2 = _ln_bwd(dya, dyb, ca, cb, sv["r2"], w["ln_g"][l, 1], t + "_ln2")
        da = _mm([(dr2, 0, wdown, 0, D_MODEL)], FFN_DIM, BF16, t + "_down", trans_b=True)
        grads["ffn_w_down"][l] = _mm_tn(sv["a"], 0, FFN_DIM, dr2, 0, D_MODEL, t + "_wdown")
        dhg, dhu, dfw8, dfb = _ffn_gate_bwd(sv["hu"], sv["hg"], sv["hv"], da, sv["fw8"], t + "_gate")
        grads["ffn_conv_w"][l] = dfw8[:3]
        grads["ffn_conv_b"][l] = dfb
        dx1 = _mm([(dhg, 0, wup, 0, FFN_DIM), (dhu, 0, wup, 1, FFN_DIM)], D_MODEL, F32, t + "_up", trans_b=True)
        grads["ffn_w_up"][l] = jnp.concatenate([_mm_tn(sv["x1b"], 0, D_MODEL, dhg, 0, FFN_DIM, t + "_wup_g"),
                                                _mm_tn(sv["x1b"], 0, D_MODEL, dhu, 0, FFN_DIM, t + "_wup_u")], axis=1)
        dr1, dg1, db1 = _ln_bwd(dr2, dx1, ALPHA, 1.0, sv["r1"], w["ln_g"][l, 0], t + "_ln1")
        grads["ln_g"][l] = jnp.stack([dg1, dg2])
        grads["ln_b"][l] = jnp.stack([db1, db2])
        win, wout, h = sv["win"], sv["wout"], sv["h"]
        dycat = _mm([(dr1, 0, wout, 0, D_MODEL)], D_MODEL, F32, t + "_out", trans_b=True)
        dwout = jnp.concatenate([_mm_tn(sv["ya"], 0, 512, dr1, 0, D_MODEL, t + "_wout_a"),
                                 _mm_tn(sv["yb"], 0, 512, dr1, 0, D_MODEL, t + "_wout_b")], axis=0)
        if l % 2 == 0:
            dz, dxbcc, ddt, ddtb, dal, ddsk, dnw = _ssd_bwd(dycat, 0, h, sv["xbcc"], sv["ssts"], *sv["sp"], t + "_ssd")
            dxbc, dcw, dcb = _dwconv_bwd(dxbcc, h, AB_XBC // CONV_CB, SSD_CONV_DIM, sv["w8"], 4, t + "_conv")
            dq, df, di, dg, dlb, dhnw = _hg_bwd(dycat, 1, h, sv["hsts"], *sv["hp"], t + "_hg")
            grads["ab_w_out"][j] = dwout
            grads["ssd_conv_w"][j], grads["ssd_conv_b"][j] = dcw[:4], dcb
            grads["ssd_dt_bias"][j], grads["ssd_a_log"][j], grads["ssd_d"][j] = ddtb[0, :8], dal[0, :8], ddsk[0, :8]
            grads["ssd_norm_w"][j], grads["hg_norm_w"][j] = dnw[0], dhnw[0]
            grads["hg_lower"][j] = dlb[0]
            pieces = [(dz, 0, AB_Z, 512), (dq, 0, AB_HQ, 512), (df, 0, AB_HF, 512), (di, 0, AB_HI, 512), (dg, 0, AB_HG, 512),
                      (dxbc, 0, AB_XBC, 256), (dxbc, 1, AB_XBC + 256, 256), (dxbc, 2, AB_XBC + 512, 256), (ddt, 0, AB_DT, LANES)]
            dwin = [_mm_tn(sv["xb"], 0, D_MODEL, p, 0, p.shape[1], t + f"_win{i}")
                    for i, p in enumerate((dz, dq, df, di, dg, dxbc, ddt))]
            grads["ab_w_in"][j] = _ab_in_to_ref(jnp.concatenate(dwin, axis=1))
        else:
            dq, dk, dv, dsr = _swa_bwd(dycat, 0, h, sv["srow"], t + "_swa")
            dxc, dgate, dwa, dwx, dba, dbx, dlam = _rg_bwd(dycat, 1, sv["xc"], h, sv["hs"], *sv["rp"], t + "_rg")
            dxr, dcw, dcb = _dwconv_bwd(dxc, h, CD_XR // CONV_CB, RG_WIDTH, sv["w8"], 4, t + "_conv")
            grads["cd_w_out"][j] = jnp.concatenate([_pairs_to_ref(dwout[:512], 0), dwout[512:]], axis=0)
            grads["swa_sinks"][j] = dsr[0, :8]
            grads["rg_conv_w"][j], grads["rg_conv_b"][j] = dcw[:4], dcb
            grads["rg_wa"][j], grads["rg_wx"][j] = _diag_blocks(dwa), _diag_blocks(dwx)
            grads["rg_ba"][j], grads["rg_bx"][j], grads["rg_lambda"][j] = dba[0], dbx[0], dlam[0]
            pieces = [(dq, 0, CD_Q, 512), (dgate, 0, CD_GATE, 512), (dxr, 0, CD_XR, 512), (dk, 0, CD_K, LANES), (dv, 0, CD_V, LANES)]
            dwin = [_mm_tn(sv["xb"], 0, D_MODEL, p, 0, p.shape[1], t + f"_win{i}")
                    for i, p in enumerate((dq, dgate, dxr, dk, dv))]
            grads["cd_w_in"][j] = _cd_in_to_ref(jnp.concatenate(dwin, axis=1))
        dx0 = _mm([(p, pcb, win, off // k, k) for p, pcb, off, k in pieces], D_MODEL, F32, t + "_in", trans_b=True)
        dya, dyb, ca, cb = dr1, dx0, ALPHA, 1.0
    grad_x = _axpby(dya, dyb, ca, cb, "grad_x")
    out = {k: (v if k in BIG else jnp.stack(v)) for k, v in grads.items()}
    (out["hg_lower"],) = lb_pull(out["hg_lower"])
    return loss, grad_x, out


MESH = pl.DeviceIdType.MESH
ANY = pl.BlockSpec(memory_space=pl.ANY)


def _place():
    x, y, c = lax.axis_index("x"), lax.axis_index("y"), lax.axis_index("c")
    return x, y, c, [(1 - x, y), (x, 1 - y), (1 - x, 1 - y)]


def _rcopy(src, dst, send_sems, recv_sems, k, to):
    return pltpu.make_async_remote_copy(src_ref=src, dst_ref=dst, send_sem=send_sems.at[k], recv_sem=recv_sems.at[k],
                                        device_id=to, device_id_type=MESH)


def _zeros_index(ndim):
    return (0,) * ndim


def _gather_chips(parts, name):
    n = len(parts)

    def body(*refs):
        w_refs, out_refs, (send_sems, recv_sems) = refs[:n], refs[n:2 * n], refs[2 * n:]
        x, y, c, chips = _place()
        me, sib = 2 * x + y, (x, y, 1 - c)
        first = [_rcopy(w_refs[a].at[c], out_refs[a].at[me, c], send_sems, recv_sems, 6 * a + k, (cx, cy, c))
                 for a in range(n) for k, (cx, cy) in enumerate(chips)]
        for cp in first:
            cp.start()
        passed = []
        for a in range(n):
            for k, (cx, cy) in enumerate(chips):
                blk = out_refs[a].at[2 * cx + cy, c]
                _rcopy(blk, blk, send_sems, recv_sems, 6 * a + k, (cx, cy, c)).wait_recv()
                passed.append(_rcopy(blk, blk, send_sems, recv_sems, 6 * a + 3 + k, sib))
                passed[-1].start()
        for a in range(n):
            for k, (cx, cy) in enumerate(chips):
                blk = out_refs[a].at[2 * cx + cy, 1 - c]
                _rcopy(blk, blk, send_sems, recv_sems, 6 * a + 3 + k, sib).wait_recv()
        for cp in first + passed:
            cp.wait_send()

    outs = pl.pallas_call(
        body, name=name, in_specs=[ANY] * n, out_specs=[ANY] * n,
        out_shape=[jax.ShapeDtypeStruct((4,) + p.shape, p.dtype) for p in parts],
        scratch_shapes=[pltpu.SemaphoreType.DMA((6 * n,)), pltpu.SemaphoreType.DMA((6 * n,))],
    )(*parts)
    me = 2 * lax.axis_index("x") + lax.axis_index("y")
    return [lax.dynamic_update_slice(o, p[None], (me,) + _zeros_index(p.ndim)) for o, p in zip(outs, parts)]


def _swap_halves(gs, name):
    n = len(gs)

    def body(*refs):
        g_refs, x_refs, (send_sems, recv_sems) = refs[:n], refs[n:2 * n], refs[2 * n:]
        x, y, c, _ = _place()
        cps = [_rcopy(g_refs[a].at[:, 1 - c], x_refs[a], send_sems, recv_sems, a, (x, y, 1 - c)) for a in range(n)]
        for cp in cps:
            cp.start()
        for cp in cps:
            cp.wait()

    return pl.pallas_call(
        body, name=name, in_specs=[ANY] * n, out_specs=[ANY] * n,
        out_shape=[jax.ShapeDtypeStruct((g.shape[0],) + g.shape[2:], g.dtype) for g in gs],
        scratch_shapes=[pltpu.SemaphoreType.DMA((n,)), pltpu.SemaphoreType.DMA((n,))],
    )(*gs)


def _scatter_chips(hbs, name):
    n = len(hbs)

    def body(*refs):
        h_refs, y_refs, (send_sems, recv_sems) = refs[:n], refs[n:2 * n], refs[2 * n:]
        x, y, c, chips = _place()
        me = 2 * x + y
        sends = [_rcopy(h_refs[a].at[2 * cx + cy], y_refs[a].at[me], send_sems, recv_sems, 3 * a + k, (cx, cy, c))
                 for a in range(n) for k, (cx, cy) in enumerate(chips)]
        for cp in sends:
            cp.start()
        for a in range(n):
            for k, (cx, cy) in enumerate(chips):
                blk = y_refs[a].at[2 * cx + cy]
                _rcopy(blk, blk, send_sems, recv_sems, 3 * a + k, (cx, cy, c)).wait_recv()
        for cp in sends:
            cp.wait_send()

    outs = pl.pallas_call(
        body, name=name, in_specs=[ANY] * n, out_specs=[ANY] * n,
        out_shape=[jax.ShapeDtypeStruct(h.shape, h.dtype) for h in hbs],
        scratch_shapes=[pltpu.SemaphoreType.DMA((3 * n,)), pltpu.SemaphoreType.DMA((3 * n,))],
    )(*hbs)
    me = 2 * lax.axis_index("x") + lax.axis_index("y")
    return [lax.dynamic_update_slice(o, lax.dynamic_slice_in_dim(h, me, 1, axis=0), (me,) + _zeros_index(h.ndim - 1))
            for o, h in zip(outs, hbs)]


def _share_half(fs, name):
    n = len(fs)

    def body(*refs):
        f_refs, out_refs, (send_sems, recv_sems) = refs[:n], refs[n:2 * n], refs[2 * n:]
        x, y, c, _ = _place()
        sib = (x, y, 1 - c)
        cps = [_rcopy(f_refs[a], out_refs[a].at[c], send_sems, recv_sems, a, sib) for a in range(n)]
        for cp in cps:
            cp.start()
        for a in range(n):
            _rcopy(f_refs[a], out_refs[a].at[1 - c], send_sems, recv_sems, a, sib).wait_recv()
        for cp in cps:
            cp.wait_send()

    outs = pl.pallas_call(
        body, name=name, in_specs=[ANY] * n, out_specs=[ANY] * n,
        out_shape=[jax.ShapeDtypeStruct((2,) + f.shape, f.dtype) for f in fs],
        scratch_shapes=[pltpu.SemaphoreType.DMA((n,)), pltpu.SemaphoreType.DMA((n,))],
    )(*fs)
    c = lax.axis_index("c")
    return [lax.dynamic_update_slice(o, f[None], (c,) + _zeros_index(f.ndim)) for o, f in zip(outs, fs)]


def _gather_devices(v, name):
    m_per, n = v.shape

    def body(x_ref, out_ref, send_sems, recv_sems, local_sem):
        x, y, c, chips = _place()
        me, sib = (x, y, c), (x, y, 1 - c)

        def rows(px, py, pc):
            return out_ref.at[pl.ds((4 * px + 2 * py + pc) * m_per, m_per), :]

        def copy(k, block, to, src=None):
            return _rcopy(rows(*block) if src is None else src, rows(*block), send_sems, recv_sems, k, to)

        mine = pltpu.make_async_copy(x_ref, rows(*me), local_sem)
        mine.start()
        first = [copy(0, me, sib, src=x_ref)]
        first += [copy(1 + j, me, (*chip, c), src=x_ref) for j, chip in enumerate(chips)]
        for cp in first:
            cp.start()
        passed = [copy(4 + j, (*chip, c), sib) for j, chip in enumerate(chips)]
        for j, chip in enumerate(chips):
            copy(1 + j, (*chip, c), me).wait_recv()
            passed[j].start()
        copy(0, sib, me).wait_recv()
        for j, chip in enumerate(chips):
            copy(4 + j, (*chip, 1 - c), me).wait_recv()
        for cp in first + passed:
            cp.wait_send()
        mine.wait()

    return pl.pallas_call(
        body, name=name, out_shape=jax.ShapeDtypeStruct((8 * m_per, n), v.dtype),
        in_specs=[pl.BlockSpec(memory_space=pltpu.VMEM)], out_specs=pl.BlockSpec(memory_space=pltpu.VMEM),
        scratch_shapes=[pltpu.SemaphoreType.DMA((7,)), pltpu.SemaphoreType.DMA((7,)), pltpu.SemaphoreType.DMA(())],
        compiler_params=_cp(),
    )(v)


ROW_TILES = (512, 256, 352, 128)


def _add_half(g, xr, c, name):
    _, _, r, n = g.shape
    tr = _pick(r, ROW_TILES)

    def kern(c_ref, g_ref, x_ref, o_ref):
        o_ref[...] = (g_ref[0] + x_ref[...]).astype(BF16)

    return pl.pallas_call(
        kern, name=name,
        grid_spec=pltpu.PrefetchScalarGridSpec(
            num_scalar_prefetch=1, grid=(4, r // tr),
            in_specs=[pl.BlockSpec((1, 1, tr, n), lambda k, i, c_ref: (k, c_ref[0], i, 0)),
                      pl.BlockSpec((1, tr, n), lambda k, i, c_ref: (k, i, 0))],
            out_specs=pl.BlockSpec((1, tr, n), lambda k, i, c_ref: (k, i, 0))),
        out_shape=jax.ShapeDtypeStruct(xr.shape, BF16), compiler_params=_cp(("parallel", "parallel")),
    )(c.reshape(1).astype(jnp.int32), g, xr)


def _sum_blocks(y, name, tr=None):
    n, r, cols = y.shape
    tr = tr or _pick(r, ROW_TILES)
    tr = tr if r % tr == 0 else r

    def kern(y_ref, o_ref):
        acc = y_ref[0].astype(F32)
        for k in range(1, n):
            acc = acc + y_ref[k].astype(F32)
        o_ref[...] = acc

    return pl.pallas_call(
        kern, name=name, grid=(r // tr,), in_specs=[pl.BlockSpec((n, tr, cols), lambda i: (0, i, 0))],
        out_specs=pl.BlockSpec((tr, cols), lambda i: (i, 0)), out_shape=jax.ShapeDtypeStruct((r, cols), F32),
        compiler_params=_cp(("parallel",)),
    )(y)


def _adamw(w, g, m, v, name, tr=256):
    r, n = w.shape
    tr = tr if r % tr == 0 else r

    def kern(w_ref, g_ref, m_ref, v_ref, d_ref, nm_ref, nv_ref):
        gg = g_ref[...]
        nm = ADAM_B1 * m_ref[...] + (1.0 - ADAM_B1) * gg
        nv = ADAM_B2 * v_ref[...] + (1.0 - ADAM_B2) * (gg * gg)
        m_hat = nm / (1.0 - ADAM_B1 ** ADAM_STEP)
        v_hat = nv / (1.0 - ADAM_B2 ** ADAM_STEP)
        d_ref[...] = -ADAM_LR * (m_hat / (jnp.sqrt(v_hat) + ADAM_EPS) + ADAM_WD * w_ref[...])
        nm_ref[...] = nm
        nv_ref[...] = nv

    blk = pl.BlockSpec((tr, n), lambda i: (i, 0))
    return pl.pallas_call(
        kern, name=name, grid=(r // tr,), in_specs=[blk] * 4, out_specs=[blk] * 3,
        out_shape=[jax.ShapeDtypeStruct((r, n), F32)] * 3, compiler_params=_cp(("parallel",)),
    )(w, g, m, v)


def _rows_of(n):
    return -(-n // (LANES * SUBLANES)) * SUBLANES


def _pack(arrs, lead, dtype, mult):
    ls = arrs[0].shape[:lead]
    blocks, total = [], 0
    for a in arrs:
        f = a.astype(dtype).reshape(ls + (-1,))
        n = f.shape[-1]
        rows = _rows_of(n)
        if rows * LANES != n:
            f = jnp.concatenate([f, jnp.zeros(ls + (rows * LANES - n,), dtype)], axis=-1)
        blocks.append(f.reshape(ls + (rows, LANES)))
        total += rows
    if total % mult:
        blocks.append(jnp.zeros(ls + (mult - total % mult, LANES), dtype))
    return jnp.concatenate(blocks, axis=lead)


def _unpack(buf, lead, shapes):
    ls = buf.shape[:lead]
    out, off = [], 0
    for sh in shapes:
        n = int(np.prod(sh))
        rows = _rows_of(n)
        flat = lax.slice_in_dim(buf, off, off + rows, axis=lead).reshape(ls + (rows * LANES,))
        if rows * LANES != n:
            flat = lax.slice_in_dim(flat, 0, n, axis=lead)
        out.append(flat.reshape(ls + tuple(sh)))
        off += rows
    return out


BIG = ("ab_w_in", "ab_w_out", "cd_w_in", "cd_w_out", "ffn_w_up", "ffn_w_down")
BIG_COLS = ("ab_w_in", "cd_w_in", "ffn_w_up")
SMALL_SHARDED = ("ssd_conv_w", "rg_conv_w", "rg_conv_b", "rg_ba", "rg_bx", "rg_lambda", "ffn_conv_w", "ln_g", "ln_b")
WEIGHTS = ("ab_w_in", "ssd_conv_w", "ssd_conv_b", "ssd_dt_bias", "ssd_a_log", "ssd_d", "ssd_norm_w", "hg_lower", "hg_norm_w",
           "ab_w_out", "cd_w_in", "swa_sinks", "rg_conv_w", "rg_conv_b", "rg_wa", "rg_ba", "rg_wx", "rg_bx", "rg_lambda",
           "cd_w_out", "ffn_w_up", "ffn_conv_w", "ffn_conv_b", "ffn_w_down", "ln_g", "ln_b")
SMALL = tuple(n for n in WEIGHTS if n not in BIG)


def _full_from_shards(name, g):
    if name in BIG_COLS or name in SMALL_SHARDED:
        return jnp.concatenate([g[k] for k in range(4)], axis=-1)
    return jnp.concatenate([g[k] for k in range(4)], axis=1)


def _layer_shards(name, f):
    k, n = f.shape
    if name in BIG_COLS:
        return jnp.transpose(f.reshape(k, 4, n // 4), (1, 0, 2))
    return f.reshape(4, k // 4, n)


def kernel(x, ab_w_in, ssd_conv_w, ssd_conv_b, ssd_dt_bias, ssd_a_log, ssd_d, ssd_norm_w, hg_lower, hg_norm_w, ab_w_out, cd_w_in, swa_sinks, rg_conv_w, rg_conv_b, rg_wa, rg_ba, rg_wx, rg_bx, rg_lambda, cd_w_out, ffn_w_up, ffn_conv_w, ffn_conv_b, ffn_w_down, ln_g, ln_b, loss_target, m_ab_w_in, m_ssd_conv_w, m_ssd_conv_b, m_ssd_dt_bias, m_ssd_a_log, m_ssd_d, m_ssd_norm_w, m_hg_lower, m_hg_norm_w, m_ab_w_out, m_cd_w_in, m_swa_sinks, m_rg_conv_w, m_rg_conv_b, m_rg_wa, m_rg_ba, m_rg_wx, m_rg_bx, m_rg_lambda, m_cd_w_out, m_ffn_w_up, m_ffn_conv_w, m_ffn_conv_b, m_ffn_w_down, m_ln_g, m_ln_b, v_ab_w_in, v_ssd_conv_w, v_ssd_conv_b, v_ssd_dt_bias, v_ssd_a_log, v_ssd_d, v_ssd_norm_w, v_hg_lower, v_hg_norm_w, v_ab_w_out, v_cd_w_in, v_swa_sinks, v_rg_conv_w, v_rg_conv_b, v_rg_wa, v_rg_ba, v_rg_wx, v_rg_bx, v_rg_lambda, v_cd_w_out, v_ffn_w_up, v_ffn_conv_w, v_ffn_conv_b, v_ffn_w_down, v_ln_g, v_ln_b):
    args = locals()
    w = {n: args[n] for n in WEIGHTS}
    mom = {n: args["m_" + n] for n in WEIGHTS}
    var = {n: args["v_" + n] for n in WEIGHTS}
    cx, cy, cc = lax.axis_index("x"), lax.axis_index("y"), lax.axis_index("c")
    chip = 2 * cx + cy

    halves = [w[n].astype(BF16).reshape((2, w[n].shape[0] // 2) + w[n].shape[1:]) for n in BIG]
    gathered = _gather_chips(halves, "gather_weights")
    full = {n: g.reshape((4,) + w[n].shape) for n, g in zip(BIG, gathered)}
    sp = _pack([w[n] for n in SMALL_SHARDED], 0, F32, SUBLANES)
    sg = _gather_devices(sp, "gather_small").reshape(4, 2, sp.shape[0], LANES)[:, 0]
    for n, g in zip(SMALL_SHARDED, _unpack(sg, 1, [w[n].shape for n in SMALL_SHARDED])):
        full[n] = _full_from_shards(n, g)
    for n in SMALL:
        full.setdefault(n, w[n])

    loss, grad_x, grads = _local_step(x[0], loss_target[0], full)
    loss = lax.psum(loss, ("x", "y", "c"))

    g4 = []
    for n in BIG:
        st = jnp.stack([_layer_shards(n, f) for f in grads[n]], axis=1)
        g4.append(st.reshape(4, 2, st.shape[1] // 2 * st.shape[2], st.shape[3]))
    hbs = [_add_half(g, xr, cc, "add_halves_" + n) for n, g, xr in zip(BIG, g4, _swap_halves(g4, "swap_halves"))]
    own = [_sum_blocks(y, "sum_chips_" + n) for n, y in zip(BIG, _scatter_chips(hbs, "scatter_chips"))]
    gout = {n: o.reshape(w[n].shape) for n, o in zip(BIG, _share_half(own, "share_half"))}

    small_shapes = [grads[n].shape for n in SMALL]
    gs = _pack([grads[n] for n in SMALL], 0, F32, SUBLANES)
    gsum = _sum_blocks(_gather_devices(gs, "gather_small_grads").reshape(8, gs.shape[0], LANES), "sum_small")
    for n, g in zip(SMALL, _unpack(gsum, 0, small_shapes)):
        if n in SMALL_SHARDED:
            width = w[n].shape[-1]
            g = lax.dynamic_slice_in_dim(g, chip * width, width, axis=g.ndim - 1)
        gout[n] = g

    delta, new_m, new_v = {}, {}, {}
    for n in BIG:
        sh = w[n].shape
        two = lambda a: a.reshape(-1, sh[-1])
        d, nm, nv = _adamw(two(w[n]), two(gout[n]), two(mom[n]), two(var[n]), "adamw_" + n)
        delta[n], new_m[n], new_v[n] = d.reshape(sh), nm.reshape(sh), nv.reshape(sh)
    local_shapes = [w[n].shape for n in SMALL]
    packs = [_pack([d[n] for n in SMALL], 0, F32, SUBLANES) for d in (w, gout, mom, var)]
    for dst, buf in zip((delta, new_m, new_v), _adamw(*packs, "adamw_small")):
        dst.update(zip(SMALL, _unpack(buf, 0, local_shapes)))

    return (loss, grad_x[None], *[gout[n] for n in WEIGHTS], *[delta[n] for n in WEIGHTS],
            *[new_m[n] for n in WEIGHTS], *[new_v[n] for n in WEIGHTS])
```

```python
import functools

import numpy as np
import jax
import jax.numpy as jnp
from jax import lax
from jax.experimental import pallas as pl
from jax.experimental.pallas import tpu as pltpu

F32 = jnp.float32
BF16 = jnp.bfloat16

D_MODEL = 1024
DEPTH = 4
SSD_HEADS = 8
SSD_D_INNER = 512
SSD_CONV_DIM = 768
SSD_CHUNK = 128
HG_CHUNK = 64
HG_HEADS = 4
SWA_BLOCK = 128
RG_WIDTH = 512
FFN_DIM = 2816
LN_EPS = 1e-5
RMS_EPS = 1e-6
MASK_VALUE = -1e9
ALPHA = (2 * DEPTH) ** 0.25
RG_C = 8.0
ADAM_LR, ADAM_B1, ADAM_B2, ADAM_EPS, ADAM_WD, ADAM_STEP = 0.001, 0.9, 0.999, 1e-08, 0.01, 10

LANES = 128
SUBLANES = 8
HALO = 16
CONV_CB = 256
VMEM_LIMIT = 56 * 1024 * 1024

AB_Z, AB_HQ, AB_HF, AB_HI, AB_HG, AB_XBC, AB_DT, AB_COLS = 0, 512, 1024, 1536, 2048, 2560, 3328, 3456
CD_Q, CD_GATE, CD_XR, CD_K, CD_V, CD_COLS = 0, 512, 1024, 1536, 1664, 1792


def _cp(sem=None):
    kw = dict(vmem_limit_bytes=VMEM_LIMIT)
    if sem is not None:
        kw["dimension_semantics"] = sem
    return pltpu.CompilerParams(**kw)


def _dot(a, b, dims=(((1,), (0,)), ((), ())), precision=None):
    return lax.dot_general(a, b, dims, precision=precision, preferred_element_type=F32)


NN = (((1,), (0,)), ((), ()))
NT = (((1,), (1,)), ((), ()))
TN = (((0,), (0,)), ((), ()))


@functools.partial(jax.custom_vjp, nondiff_argnums=(2,))
def _bdot(a, b, dims=NN):
    return _dot(a.astype(BF16), b.astype(BF16), dims)


def _bdot_fwd(a, b, dims):
    return _bdot(a, b, dims), (a, b)


def _bdot_bwd(dims, res, ct):
    a, b = res
    ab, bb, cb = a.astype(BF16), b.astype(BF16), ct.astype(BF16)
    if dims == NN:
        da, db = _dot(cb, bb, NT), _dot(ab, cb, TN)
    elif dims == NT:
        da, db = _dot(cb, bb, NN), _dot(cb, ab, TN)
    else:
        da, db = _dot(bb, cb, NT), _dot(ab, cb, NN)
    return da.astype(a.dtype), db.astype(b.dtype)


_bdot.defvjp(_bdot_fwd, _bdot_bwd)


WIDE_TILES = (1408, 1152, 1024, 896, 768, 512, 384, 256, 128)
MM_BLOCK_BYTES = 6 * 1024 * 1024


def _pick(n, cands):
    for c in cands:
        if n % c == 0:
            return c
    return n


def _mm(pairs, n_out, out_dtype, name, trans_b=False, tm=512):
    m = pairs[0][0].shape[0]
    a_row_bytes = sum(p[4] * p[0].dtype.itemsize for p in pairs)
    b_col_bytes = sum(p[4] * p[2].dtype.itemsize for p in pairs)
    tm = min(2 * tm if 2 * tm * a_row_bytes <= 2 * MM_BLOCK_BYTES else tm, m)
    tn = _pick(n_out, [c for c in WIDE_TILES if c * b_col_bytes <= MM_BLOCK_BYTES])
    n = len(pairs)

    def kern(*refs):
        acc = None
        for i in range(n):
            p = _bdot(refs[i][...], refs[n + i][...], NT if trans_b else NN)
            acc = p if acc is None else acc + p
        refs[2 * n][...] = acc.astype(out_dtype)

    in_specs = []
    for a, acb, b, bcb, k in pairs:
        in_specs.append(pl.BlockSpec((tm, k), functools.partial(lambda i, j, c: (i, c), c=acb)))
    for a, acb, b, bcb, k in pairs:
        if trans_b:
            in_specs.append(pl.BlockSpec((tn, k), functools.partial(lambda i, j, c: (j, c), c=bcb)))
        else:
            in_specs.append(pl.BlockSpec((k, tn), functools.partial(lambda i, j, c: (c, j), c=bcb)))
    return pl.pallas_call(
        kern, name=name, grid=(m // tm, n_out // tn), in_specs=in_specs,
        out_specs=pl.BlockSpec((tm, tn), lambda i, j: (i, j)),
        out_shape=jax.ShapeDtypeStruct((m, n_out), out_dtype),
        compiler_params=_cp(("parallel", "arbitrary")),
    )(*[p[0] for p in pairs], *[p[2] for p in pairs])


def _mm_tn(a, a_cb, ka, g, g_cb, ng, name, tm=2048):
    m = a.shape[0]
    tm = min(tm, m)
    tk = _pick(ka, (1024, 1408, 512, 256, 128))
    tn = _pick(ng, WIDE_TILES)
    nm = m // tm

    def kern(a_ref, g_ref, o_ref):
        p = _bdot(a_ref[...], g_ref[...], TN)

        @pl.when(pl.program_id(2) == 0)
        def _():
            o_ref[...] = p

        @pl.when(pl.program_id(2) > 0)
        def _():
            o_ref[...] += p

    ak, gk = ka // tk, ng // tn
    return pl.pallas_call(
        kern, name=name, grid=(ak, gk, nm),
        in_specs=[pl.BlockSpec((tm, tk), lambda i, j, r: (r, a_cb * ak + i)),
                  pl.BlockSpec((tm, tn), lambda i, j, r: (r, g_cb * gk + j))],
        out_specs=pl.BlockSpec((tk, tn), lambda i, j, r: (i, j)),
        out_shape=jax.ShapeDtypeStruct((ka, ng), F32),
        compiler_params=_cp(("parallel", "parallel", "arbitrary")),
    )(a, g)


def _mm_ln(pairs, x, g, b, name, tm=512):
    m, d = x.shape
    tm = min(tm, m)
    n = len(pairs)

    def kern(*refs):
        x_ref, g_ref, b_ref, y_ref, yb_ref, r_ref = refs[2 * n:]
        r = ALPHA * x_ref[...]
        for i in range(n):
            r = r + _bdot(refs[i][...], refs[n + i][...])
        mu = jnp.mean(r, -1, keepdims=True)
        xc = r - mu
        var = jnp.mean(xc * xc, -1, keepdims=True)
        y = xc * lax.rsqrt(var + LN_EPS) * g_ref[...] + b_ref[...]
        y_ref[...] = y
        yb_ref[...] = y.astype(BF16)
        r_ref[...] = r

    row = pl.BlockSpec((tm, d), lambda i: (i, 0))
    vec = pl.BlockSpec((1, d), lambda i: (0, 0))
    in_specs = [pl.BlockSpec((tm, k), functools.partial(lambda i, c: (i, c), c=acb)) for _, acb, _, _, k in pairs]
    in_specs += [pl.BlockSpec((k, d), functools.partial(lambda i, c: (c, 0), c=bcb)) for _, _, _, bcb, k in pairs]
    return pl.pallas_call(
        kern, name=name, grid=(m // tm,), in_specs=in_specs + [row, vec, vec], out_specs=[row, row, row],
        out_shape=[jax.ShapeDtypeStruct((m, d), F32), jax.ShapeDtypeStruct((m, d), BF16), jax.ShapeDtypeStruct((m, d), F32)],
        compiler_params=_cp(("parallel",)),
    )(*[p[0] for p in pairs], *[p[2] for p in pairs], x, g.reshape(1, d), b.reshape(1, d))


def _ln_bwd(dya, dyb, ca, cb, r, g, name, tm=256):
    s, d = r.shape
    tm = min(tm, s)

    def kern(a_ref, b_ref, r_ref, g_ref, dr_ref, dg_ref, db_ref):
        dy = ca * a_ref[...] + cb * b_ref[...]
        rr = r_ref[...]
        mu = jnp.mean(rr, -1, keepdims=True)
        xc = rr - mu
        var = jnp.mean(xc * xc, -1, keepdims=True)
        rstd = lax.rsqrt(var + LN_EPS)
        xhat = xc * rstd
        dxh = dy * g_ref[...]
        dr_ref[...] = rstd * (dxh - jnp.mean(dxh, -1, keepdims=True) - xhat * jnp.mean(dxh * xhat, -1, keepdims=True))
        dg = jnp.sum(dy * xhat, 0, keepdims=True)
        db = jnp.sum(dy, 0, keepdims=True)

        @pl.when(pl.program_id(0) == 0)
        def _():
            dg_ref[...] = dg
            db_ref[...] = db

        @pl.when(pl.program_id(0) > 0)
        def _():
            dg_ref[...] += dg
            db_ref[...] += db

    row = pl.BlockSpec((tm, d), lambda i: (i, 0))
    vec = pl.BlockSpec((1, d), lambda i: (0, 0))
    dr, dg, db = pl.pallas_call(
        kern, name=name, grid=(s // tm,), in_specs=[row, row, row, vec], out_specs=[row, vec, vec],
        out_shape=[jax.ShapeDtypeStruct((s, d), F32), jax.ShapeDtypeStruct((1, d), F32), jax.ShapeDtypeStruct((1, d), F32)],
        compiler_params=_cp(("arbitrary",)),
    )(dya, dyb, r, g.reshape(1, d))
    return dr, dg[0], db[0]


def _sqerr(y, t, name, tm=256):
    s, d = y.shape
    tm = min(tm, s)

    def kern(y_ref, t_ref, o_ref):
        e = y_ref[...] - t_ref[...]
        p = jnp.sum(e * e, 0, keepdims=True)

        @pl.when(pl.program_id(0) == 0)
        def _():
            o_ref[...] = p

        @pl.when(pl.program_id(0) > 0)
        def _():
            o_ref[...] += p

    row = pl.BlockSpec((tm, d), lambda i: (i, 0))
    return pl.pallas_call(
        kern, name=name, grid=(s // tm,), in_specs=[row, row], out_specs=pl.BlockSpec((1, d), lambda i: (0, 0)),
        out_shape=jax.ShapeDtypeStruct((1, d), F32), compiler_params=_cp(("arbitrary",)),
    )(y, t)


def _axpby(a, b, ca, cb, name, tm=256):
    s, d = a.shape
    tm = min(tm, s)

    def kern(a_ref, b_ref, o_ref):
        o_ref[...] = ca * a_ref[...] + cb * b_ref[...]

    row = pl.BlockSpec((tm, d), lambda i: (i, 0))
    return pl.pallas_call(
        kern, name=name, grid=(s // tm,), in_specs=[row, row], out_specs=row,
        out_shape=jax.ShapeDtypeStruct((s, d), F32), compiler_params=_cp(("parallel",)),
    )(a, b)


def _shift_down(xx, s, n):
    if s == 0:
        return xx[HALO:HALO + n]
    return pltpu.roll(xx, s, axis=0)[HALO:HALO + n]


def _shift_up(yy, s, n):
    if s == 0:
        return yy[0:n]
    return pltpu.roll(yy, yy.shape[0] - s, axis=0)[0:n]


def _prev_halo(tm):
    return lambda i: jnp.maximum(i * (tm // HALO) - 1, 0)


def _dwconv_fwd(x, x_cb0, c, w8, b, k, name, tm=1024):
    s = x.shape[0]
    tm = min(tm, s)
    ph = _prev_halo(tm)

    def kern(x_ref, h_ref, w_ref, b_ref, y_ref):
        halo = jnp.where(pl.program_id(0) == 0, 0.0, h_ref[...].astype(F32))
        xx = jnp.concatenate([halo, x_ref[...].astype(F32)], axis=0)
        w = w_ref[...]
        acc = b_ref[...] + w[k - 1:k] * xx[HALO:]
        for j in range(k - 1):
            acc = acc + w[j:j + 1] * _shift_down(xx, k - 1 - j, tm)
        y_ref[...] = acc

    return pl.pallas_call(
        kern, name=name, grid=(s // tm, c // CONV_CB),
        in_specs=[pl.BlockSpec((tm, CONV_CB), lambda i, j: (i, x_cb0 + j)),
                  pl.BlockSpec((HALO, CONV_CB), lambda i, j: (ph(i), x_cb0 + j)),
                  pl.BlockSpec((SUBLANES, CONV_CB), lambda i, j: (0, j)),
                  pl.BlockSpec((1, CONV_CB), lambda i, j: (0, j))],
        out_specs=pl.BlockSpec((tm, CONV_CB), lambda i, j: (i, j)),
        out_shape=jax.ShapeDtypeStruct((s, c), F32), compiler_params=_cp(("parallel", "parallel")),
    )(x, x, w8, b.reshape(1, c))


def _dwconv_bwd(dy, x, x_cb0, c, w8, k, name, tm=1024):
    s = x.shape[0]
    tm = min(tm, s)
    nt = s // tm
    ph = _prev_halo(tm)
    nh = lambda i: jnp.minimum((i + 1) * (tm // HALO), s // HALO - 1)

    def kern(dy_ref, dyn_ref, x_ref, h_ref, w_ref, dx_ref, dw_ref, db_ref):
        i = pl.program_id(1)
        halo = jnp.where(i == 0, 0.0, h_ref[...].astype(F32))
        xx = jnp.concatenate([halo, x_ref[...].astype(F32)], axis=0)
        dyt = dy_ref[...]
        nxt = jnp.where(i == nt - 1, 0.0, dyn_ref[...])
        dyy = jnp.concatenate([dyt, nxt], axis=0)
        w = w_ref[...]
        dx = w[k - 1:k] * dyt
        rows = [jnp.sum(dyt * _shift_down(xx, k - 1 - j, tm), 0, keepdims=True) for j in range(k)]
        for j in range(k - 1):
            dx = dx + w[j:j + 1] * _shift_up(dyy, k - 1 - j, tm)
        dx_ref[...] = dx.astype(BF16)
        dw = jnp.concatenate(rows + [jnp.zeros((SUBLANES - k, CONV_CB), F32)], axis=0)
        db = jnp.sum(dyt, 0, keepdims=True)

        @pl.when(i == 0)
        def _():
            dw_ref[...] = dw
            db_ref[...] = db

        @pl.when(i > 0)
        def _():
            dw_ref[...] += dw
            db_ref[...] += db

    dx, dw, db = pl.pallas_call(
        kern, name=name, grid=(c // CONV_CB, nt),
        in_specs=[pl.BlockSpec((tm, CONV_CB), lambda j, i: (i, j)),
                  pl.BlockSpec((HALO, CONV_CB), lambda j, i: (nh(i), j)),
                  pl.BlockSpec((tm, CONV_CB), lambda j, i: (i, x_cb0 + j)),
                  pl.BlockSpec((HALO, CONV_CB), lambda j, i: (ph(i), x_cb0 + j)),
                  pl.BlockSpec((SUBLANES, CONV_CB), lambda j, i: (0, j))],
        out_specs=[pl.BlockSpec((tm, CONV_CB), lambda j, i: (i, j)),
                   pl.BlockSpec((SUBLANES, CONV_CB), lambda j, i: (0, j)),
                   pl.BlockSpec((1, CONV_CB), lambda j, i: (0, j))],
        out_shape=[jax.ShapeDtypeStruct((s, c), BF16), jax.ShapeDtypeStruct((SUBLANES, c), F32),
                   jax.ShapeDtypeStruct((1, c), F32)],
        compiler_params=_cp(("parallel", "arbitrary")),
    )(dy, dy, x, x, w8)
    return dx, dw, db[0]


def _silu(x):
    return x * jax.nn.sigmoid(x)


def _ffn_gate_fwd(hu, w8, b, name, tm=1024):
    s = hu.shape[0]
    tm = min(tm, s)
    nb = FFN_DIM // CONV_CB
    ph = _prev_halo(tm)
    k = 3

    def conv(x_ref, h_ref, w_ref, b_ref):
        halo = jnp.where(pl.program_id(0) == 0, 0.0, h_ref[...].astype(F32))
        xx = jnp.concatenate([halo, x_ref[...].astype(F32)], axis=0)
        w = w_ref[...]
        acc = b_ref[...] + w[k - 1:k] * xx[HALO:]
        for j in range(k - 1):
            acc = acc + w[j:j + 1] * _shift_down(xx, k - 1 - j, tm)
        return acc

    def kern(g_ref, gh_ref, u_ref, uh_ref, wg_ref, wu_ref, bg_ref, bu_ref, a_ref, hg_ref, hu_ref):
        g = conv(g_ref, gh_ref, wg_ref, bg_ref)
        u = conv(u_ref, uh_ref, wu_ref, bu_ref)
        a_ref[...] = (_silu(g) * u).astype(BF16)
        hg_ref[...] = g.astype(BF16)
        hu_ref[...] = u.astype(BF16)

    main = lambda off: pl.BlockSpec((tm, CONV_CB), lambda i, j: (i, off + j))
    halo = lambda off: pl.BlockSpec((HALO, CONV_CB), lambda i, j: (ph(i), off + j))
    wsp = lambda off: pl.BlockSpec((SUBLANES, CONV_CB), lambda i, j: (0, off + j))
    bsp = lambda off: pl.BlockSpec((1, CONV_CB), lambda i, j: (0, off + j))
    b2 = b.reshape(1, 2 * FFN_DIM)
    half = jax.ShapeDtypeStruct((s, FFN_DIM), BF16)
    return pl.pallas_call(
        kern, name=name, grid=(s // tm, nb),
        in_specs=[main(0), halo(0), main(nb), halo(nb), wsp(0), wsp(nb), bsp(0), bsp(nb)],
        out_specs=[main(0), main(0), main(0)], out_shape=[half, half, half],
        compiler_params=_cp(("parallel", "parallel")),
    )(hu, hu, hu, hu, w8, w8, b2, b2)


def _ffn_gate_bwd(hu, hg, hv, da, w8, name, tm=1024):
    s = hu.shape[0]
    tm = min(tm, s)
    nt = s // tm
    nb = FFN_DIM // CONV_CB
    nh = lambda i: jnp.minimum((i + 1) * (tm // HALO), s // HALO - 1)
    k = 3

    def kern(xg_ref, xu_ref, g_ref, gn_ref, u_ref, un_ref, da_ref, dan_ref, wg_ref, wu_ref,
             dg_ref, du_ref, dwg_ref, dwu_ref, dbg_ref, dbu_ref):
        i = pl.program_id(1)

        def ext(x_ref, n_ref):
            return jnp.concatenate([x_ref[...].astype(F32), n_ref[...].astype(F32)], axis=0)

        g, u = ext(g_ref, gn_ref), ext(u_ref, un_ref)
        dae = jnp.concatenate([da_ref[...].astype(F32), jnp.where(i == nt - 1, 0.0, dan_ref[...].astype(F32))], axis=0)
        sg = jax.nn.sigmoid(g)
        dhg = dae * u * (sg * (1.0 + g * (1.0 - sg)))
        dhu = dae * (g * sg)

        def back(dh, x, w):
            taps = [dh[0:tm] if j == k - 1 else _shift_up(dh, k - 1 - j, tm) for j in range(k)]
            dx = w[0:1] * taps[0]
            for j in range(1, k):
                dx = dx + w[j:j + 1] * taps[j]
            rows = [jnp.sum(taps[j] * x, 0, keepdims=True) for j in range(k)]
            dw = jnp.concatenate(rows + [jnp.zeros((SUBLANES - k, CONV_CB), F32)], axis=0)
            return dx, dw, jnp.sum(dh[0:tm], 0, keepdims=True)

        dxg, dwg, dbg = back(dhg, xg_ref[...].astype(F32), wg_ref[...])
        dxu, dwu, dbu = back(dhu, xu_ref[...].astype(F32), wu_ref[...])
        dg_ref[...] = dxg.astype(BF16)
        du_ref[...] = dxu.astype(BF16)

        @pl.when(i == 0)
        def _():
            dwg_ref[...] = dwg
            dwu_ref[...] = dwu
            dbg_ref[...] = dbg
            dbu_ref[...] = dbu

        @pl.when(i > 0)
        def _():
            dwg_ref[...] += dwg
            dwu_ref[...] += dwu
            dbg_ref[...] += dbg
            dbu_ref[...] += dbu

    main = lambda off: pl.BlockSpec((tm, CONV_CB), lambda j, i: (i, off + j))
    nxt = lambda off: pl.BlockSpec((HALO, CONV_CB), lambda j, i: (nh(i), off + j))
    wsp = lambda off: pl.BlockSpec((SUBLANES, CONV_CB), lambda j, i: (0, off + j))
    bsp = lambda off: pl.BlockSpec((1, CONV_CB), lambda j, i: (0, off + j))
    outs = pl.pallas_call(
        kern, name=name, grid=(nb, nt),
        in_specs=[main(0), main(nb), main(0), nxt(0), main(0), nxt(0), main(0), nxt(0), wsp(0), wsp(nb)],
        out_specs=[main(0), main(0), wsp(0), wsp(0), bsp(0), bsp(0)],
        out_shape=[jax.ShapeDtypeStruct((s, FFN_DIM), BF16), jax.ShapeDtypeStruct((s, FFN_DIM), BF16),
                   jax.ShapeDtypeStruct((SUBLANES, FFN_DIM), F32), jax.ShapeDtypeStruct((SUBLANES, FFN_DIM), F32),
                   jax.ShapeDtypeStruct((1, FFN_DIM), F32), jax.ShapeDtypeStruct((1, FFN_DIM), F32)],
        compiler_params=_cp(("parallel", "arbitrary")),
    )(hu, hu, hg, hg, hv, hv, da, da, w8, w8)
    dg, du, dwg, dwu, dbg, dbu = outs
    return dg, du, jnp.concatenate([dwg, dwu], axis=1), jnp.concatenate([dbg[0], dbu[0]])


def _iota(shape, dim):
    return lax.broadcasted_iota(jnp.int32, shape, dim)


def _in_range(idx, start, size):
    return jnp.logical_and(idx >= start, idx < start + size)


@functools.partial(jax.custom_vjp, nondiff_argnums=(2, 3))
def _zdot(a, b, dims, z):
    zz, x = ((a, b) if z == 0 else (b, a))
    zz = zz.astype(BF16)
    xh = x.astype(BF16)
    r1 = x - xh.astype(F32)
    xm = r1.astype(BF16)
    xl = (r1 - xm.astype(F32)).astype(BF16)
    f = (lambda p: _dot(zz, p, dims)) if z == 0 else (lambda p: _dot(p, zz, dims))
    return f(xh) + f(xm) + f(xl)


def _zdot_fwd(a, b, dims, z):
    return _zdot(a, b, dims, z), (a if z == 0 else b)


def _zdot_bwd(dims, z, zz, ct):
    zb = zz.astype(BF16)
    ch = ct.astype(BF16)
    cm = (ct - ch.astype(F32)).astype(BF16)
    if dims == NN and z == 0:
        f = lambda c: _dot(zb, c, TN)
    elif dims == NN and z == 1:
        f = lambda c: _dot(c, zb, NT)
    else:
        f = lambda c: _dot(c, zb, TN)
    dx = f(ch) + f(cm)
    return (jnp.zeros_like(zz), dx) if z == 0 else (dx, jnp.zeros_like(zz))


_zdot.defvjp(_zdot_fwd, _zdot_bwd)


def _expand8(v, e):
    return jnp.sum(_zdot(jnp.broadcast_to(v, (SUBLANES, v.shape[1])), e, NN, 1), 0, keepdims=True) * (1.0 / SUBLANES)


def _ssd_consts():
    tri = np.tril(np.ones((SSD_CHUNK, SSD_CHUNK), np.float32))
    e = np.zeros((LANES, SSD_D_INNER), np.float32)
    for h in range(SSD_HEADS):
        e[h, 64 * h:64 * h + 64] = 1.0
    bd = np.zeros((LANES, SSD_D_INNER), np.float32)
    bd[:64, :256] = 1.0
    bd[64:, 256:] = 1.0
    return jnp.asarray(tri), jnp.asarray(e), jnp.asarray(bd)


def _ssd_chunk(tri, e, bd, z, xs_p, bm_p, cm_p, dtr, st, dtb, alog, dsk, nw):
    n = z.shape[0]
    lane128, lane512 = _iota((1, LANES), 1), _iota((1, SSD_D_INNER), 1)
    sub8 = _iota((SUBLANES, 1), 0)
    tril = _iota((n, 1), 0) >= _iota((1, n), 1)
    head_rows = jnp.where(sub8 == lane128, 1.0, 0.0)
    xs, bm, cm = _silu(xs_p), _silu(bm_p), _silu(cm_p)
    dt = jax.nn.softplus(dtr + dtb)
    da = dt * (-jnp.exp(alog))
    acs = _zdot(tri, da, NN, 0)
    acst = _zdot(head_rows, acs, NT, 0)
    xc = xs * _zdot(dt, e, NN, 1)
    tot = jnp.sum(da, 0, keepdims=True)
    y = _bdot(cm, st) * _zdot(jnp.exp(acs), e, NN, 1)
    st_new = st * _expand8(jnp.exp(tot), e) + bd * _bdot(bm, xc * _zdot(jnp.exp(tot - acs), e, NN, 1), TN)
    for g in range(2):
        cb = _bdot(jnp.where(_in_range(lane128, 64 * g, 64), cm, 0.0), bm, NT)
        for h in range(4 * g, 4 * g + 4):
            col = jnp.sum(jnp.where(lane128 == h, acs, 0.0), 1, keepdims=True)
            row = jnp.sum(jnp.where(sub8 == h, acst, 0.0), 0, keepdims=True)
            dm = jnp.where(tril, jnp.exp(jnp.minimum(col - row, 0.0)), 0.0)
            y = y + _bdot(cb * dm, jnp.where(_in_range(lane512, 64 * h, 64), xc, 0.0))
    y = (y + _expand8(dsk, e) * xs) * _silu(z)
    ysq = y * y
    g0 = lane512 < 256
    ms0 = jnp.sum(jnp.where(g0, ysq, 0.0), 1, keepdims=True) * (1.0 / 256)
    ms1 = jnp.sum(jnp.where(g0, 0.0, ysq), 1, keepdims=True) * (1.0 / 256)
    rs = jnp.where(g0, lax.rsqrt(ms0 + RMS_EPS), lax.rsqrt(ms1 + RMS_EPS))
    return y * rs * nw, st_new


def _ssd_specs(rev, nc):
    ci = (lambda i: nc - 1 - i) if rev else (lambda i: i)
    n = SSD_CHUNK
    act = [pl.BlockSpec((n, 512), lambda i: (ci(i), AB_Z // 512)),
           pl.BlockSpec((n, 512), lambda i: (ci(i), 0)),
           pl.BlockSpec((n, LANES), lambda i: (ci(i), 4)),
           pl.BlockSpec((n, LANES), lambda i: (ci(i), 5)),
           pl.BlockSpec((n, LANES), lambda i: (ci(i), AB_DT // LANES))]
    const = [pl.BlockSpec((n, n), lambda i: (0, 0)), pl.BlockSpec((LANES, 512), lambda i: (0, 0)),
             pl.BlockSpec((LANES, 512), lambda i: (0, 0))]
    par = [pl.BlockSpec((1, LANES), lambda i: (0, 0))] * 3 + [pl.BlockSpec((1, 512), lambda i: (0, 0))]
    return ci, act, const, par


def _ssd_fwd(h, xbcc, dtb, alog, dsk, nw, name):
    s = h.shape[0]
    nc = s // SSD_CHUNK
    ci, act, const, par = _ssd_specs(False, nc)

    def kern(z_ref, xs_ref, bm_ref, cm_ref, dt_ref, tri_ref, e_ref, bd_ref, dtb_ref, al_ref, dsk_ref, nw_ref,
             y_ref, sts_ref, st):
        @pl.when(pl.program_id(0) == 0)
        def _():
            st[...] = jnp.zeros_like(st)

        sts_ref[0] = st[...]
        y, stn = _ssd_chunk(tri_ref[...], e_ref[...], bd_ref[...], z_ref[...], xs_ref[...], bm_ref[...], cm_ref[...],
                            dt_ref[...], st[...], dtb_ref[...], al_ref[...], dsk_ref[...], nw_ref[...])
        y_ref[...] = y.astype(BF16)
        st[...] = stn

    return pl.pallas_call(
        kern, name=name, grid=(nc,), in_specs=act + const + par,
        out_specs=[pl.BlockSpec((SSD_CHUNK, 512), lambda i: (i, 0)), pl.BlockSpec((1, LANES, 512), lambda i: (i, 0, 0))],
        out_shape=[jax.ShapeDtypeStruct((s, 512), BF16), jax.ShapeDtypeStruct((nc, LANES, 512), F32)],
        scratch_shapes=[pltpu.VMEM((LANES, 512), F32)], compiler_params=_cp(("arbitrary",)),
    )(h, xbcc, xbcc, xbcc, h, *_ssd_consts(), dtb, alog, dsk, nw)


def _acc(ref, val, first):
    @pl.when(first)
    def _():
        ref[...] = val

    @pl.when(jnp.logical_not(first))
    def _():
        ref[...] += val


def _ssd_bwd(dy, dy_cb, h, xbcc, sts, dtb, alog, dsk, nw, name):
    s = h.shape[0]
    nc = s // SSD_CHUNK
    n = SSD_CHUNK
    ci, act, const, par = _ssd_specs(True, nc)

    def kern(dy_ref, sts_ref, z_ref, xs_ref, bm_ref, cm_ref, dt_ref, tri_ref, e_ref, bd_ref, dtb_ref, al_ref, dsk_ref,
             nw_ref, dz_ref, dx_ref, ddt_ref, ddtb_ref, dal_ref, ddsk_ref, dnw_ref, dst):
        first = pl.program_id(0) == 0

        @pl.when(first)
        def _():
            dst[...] = jnp.zeros_like(dst)

        f = functools.partial(_ssd_chunk, tri_ref[...], e_ref[...], bd_ref[...])
        _, pull = jax.vjp(f, z_ref[...], xs_ref[...], bm_ref[...], cm_ref[...], dt_ref[...], sts_ref[0],
                          dtb_ref[...], al_ref[...], dsk_ref[...], nw_ref[...])
        dz, dxs, dbm, dcm, ddt, dsti, ddtb, dal, ddsk, dnw = pull((dy_ref[...], dst[...]))
        dz_ref[...] = dz.astype(BF16)
        dx_ref[:, 0:512] = dxs
        dx_ref[:, 512:640] = dbm
        dx_ref[:, 640:768] = dcm
        ddt_ref[...] = ddt.astype(BF16)
        dst[...] = dsti
        _acc(ddtb_ref, ddtb, first)
        _acc(dal_ref, dal, first)
        _acc(ddsk_ref, ddsk, first)
        _acc(dnw_ref, dnw, first)

    vec = pl.BlockSpec((1, LANES), lambda i: (0, 0))
    return pl.pallas_call(
        kern, name=name, grid=(nc,),
        in_specs=[pl.BlockSpec((n, 512), lambda i: (ci(i), dy_cb)), pl.BlockSpec((1, LANES, 512), lambda i: (ci(i), 0, 0))]
        + act + const + par,
        out_specs=[pl.BlockSpec((n, 512), lambda i: (ci(i), 0)), pl.BlockSpec((n, SSD_CONV_DIM), lambda i: (ci(i), 0)),
                   pl.BlockSpec((n, LANES), lambda i: (ci(i), 0)), vec, vec, vec, pl.BlockSpec((1, 512), lambda i: (0, 0))],
        out_shape=[jax.ShapeDtypeStruct((s, 512), BF16), jax.ShapeDtypeStruct((s, SSD_CONV_DIM), F32),
                   jax.ShapeDtypeStruct((s, LANES), BF16), jax.ShapeDtypeStruct((1, LANES), F32),
                   jax.ShapeDtypeStruct((1, LANES), F32), jax.ShapeDtypeStruct((1, LANES), F32),
                   jax.ShapeDtypeStruct((1, 512), F32)],
        scratch_shapes=[pltpu.VMEM((LANES, 512), F32)], compiler_params=_cp(("arbitrary",)),
    )(dy, sts, h, xbcc, xbcc, xbcc, h, *_ssd_consts(), dtb, alog, dsk, nw)


HG_LEVELS = 6


HG_SUMS = 1 + 2 * HG_LEVELS


def _hg_consts():
    n = HG_CHUNK
    t = np.arange(n)
    mats = [np.tril(np.ones((n, n), np.float32))]
    dq, dk, mk = [], [], []
    for lv in range(HG_LEVELS):
        b = 1 << lv
        start = (t // b) * b
        end = start + b - 1
        dq.append(((t[None, :] >= start[:, None]) & (t[None, :] <= t[:, None])).astype(np.float32))
        dk.append(((t[None, :] > t[:, None]) & (t[None, :] <= end[:, None])).astype(np.float32))
        mk.append((((t // b) % 2 == 1)[:, None] & ((t // b)[None, :] == (t // b)[:, None] - 1)).astype(np.float32))
    c1 = np.concatenate(mats + dq + dk, axis=0)
    return (jnp.asarray(c1, BF16), jnp.asarray(np.concatenate([c1, c1, c1], axis=1), BF16), jnp.asarray(np.stack(mk)))


@functools.partial(jax.custom_vjp, nondiff_argnums=(1, 2))
def _split_grid(x, nr, nc):
    br, bc = x.shape[0] // nr, x.shape[1] // nc
    return tuple(x[i * br:(i + 1) * br, j * bc:(j + 1) * bc] for i in range(nr) for j in range(nc))


def _split_grid_fwd(x, nr, nc):
    return _split_grid(x, nr, nc), None


def _split_grid_bwd(nr, nc, _, cts):
    rows = [cts[i * nc] if nc == 1 else jnp.concatenate(cts[i * nc:(i + 1) * nc], axis=1) for i in range(nr)]
    return (rows[0] if nr == 1 else jnp.concatenate(rows, axis=0),)


_split_grid.defvjp(_split_grid_fwd, _split_grid_bwd)


@jax.custom_vjp
def _level_sums(c1, c3, g):
    gh = g.astype(BF16)
    r1 = g - gh.astype(F32)
    gm = r1.astype(BF16)
    gl = (r1 - gm.astype(F32)).astype(BF16)
    return _dot(c3, jnp.concatenate([gh, gm, gl], axis=0), NN)


def _level_sums_fwd(c1, c3, g):
    return _level_sums(c1, c3, g), (c1, c3)


def _level_sums_bwd(res, ct):
    c1, c3 = res
    ch = ct.astype(BF16)
    cm = (ct - ch.astype(F32)).astype(BF16)
    return jnp.zeros_like(c1), jnp.zeros_like(c3), _dot(c1, ch, TN) + _dot(c1, cm, TN)


_level_sums.defvjp(_level_sums_fwd, _level_sums_bwd)


def _hg_chunk(c1, c3, masks, hq, hf, hi, hg, sts, lb, nw):
    q = _silu(hq)
    g = jnp.log(lb + (1.0 - lb) * jax.nn.sigmoid(hf))
    k = (1.0 - lb) * jax.nn.sigmoid(-hf)
    parts = _split_grid(_level_sums(c1, c3, g), HG_SUMS, 1)
    bc = parts[0]
    tot = jnp.sum(g, 0, keepdims=True)
    qe = _split_grid(q * jnp.exp(bc), 1, HG_HEADS)
    kd = _split_grid(k * jnp.exp(tot - bc), 1, HG_HEADS)
    etot = _split_grid(jnp.exp(tot), 1, HG_HEADS)
    qh, kh, vh, gh = [_split_grid(a, 1, HG_HEADS) for a in (q, k, hi, hg)]
    att = [None] * HG_HEADS
    for lv in range(HG_LEVELS):
        qt = _split_grid(q * jnp.exp(parts[1 + lv]), 1, HG_HEADS)
        kt = _split_grid(k * jnp.exp(parts[1 + HG_LEVELS + lv]), 1, HG_HEADS)
        for hd in range(HG_HEADS):
            t = masks[lv] * _bdot(qt[hd], kt[hd], NT)
            att[hd] = t if att[hd] is None else att[hd] + t
    outs, stn = [], []
    for hd in range(HG_HEADS):
        o = (_bdot(att[hd], vh[hd]) + jnp.sum(qh[hd] * kh[hd], 1, keepdims=True) * vh[hd] + _bdot(qe[hd], sts[hd], NT))
        stn.append(sts[hd] * etot[hd] + _bdot(vh[hd], kd[hd], TN))
        outs.append(o * lax.rsqrt(jnp.mean(o * o, 1, keepdims=True) + RMS_EPS) * nw * _silu(gh[hd]))
    return tuple(outs), tuple(stn)


HG_STEP = 2


def _hg_specs(rev, ng):
    ci = (lambda i: ng - 1 - i) if rev else (lambda i: i)
    n = HG_CHUNK
    act = [pl.BlockSpec((HG_STEP * n, 512), functools.partial(lambda i, c: (ci(i), c), c=c))
           for c in (AB_HQ // 512, AB_HF // 512, AB_HI // 512, AB_HG // 512)]
    other = [pl.BlockSpec((HG_SUMS * n, n), lambda i: (0, 0)), pl.BlockSpec((HG_SUMS * n, 3 * n), lambda i: (0, 0)),
             pl.BlockSpec((HG_LEVELS, n, n), lambda i: (0, 0, 0)), pl.BlockSpec((1, 512), lambda i: (0, 0)),
             pl.BlockSpec((1, LANES), lambda i: (0, 0))]
    return ci, act, other


def _head_rows(hd):
    return slice(LANES * hd, LANES * (hd + 1))


def _chunk_rows(u):
    return slice(HG_CHUNK * u, HG_CHUNK * (u + 1))


def _hg_fwd(h, lb, nw, name):
    s = h.shape[0]
    nc = s // HG_CHUNK
    ng = nc // HG_STEP
    ci, act, other = _hg_specs(False, ng)

    def kern(q_ref, f_ref, i_ref, g_ref, c1_ref, c3_ref, mk_ref, lb_ref, nw_ref, y_ref, sts_ref, st):
        @pl.when(pl.program_id(0) == 0)
        def _():
            st[...] = jnp.zeros_like(st)

        masks = [mk_ref[lv] for lv in range(HG_LEVELS)]
        for u in range(HG_STEP):
            rows = _chunk_rows(u)
            sts_ref[u] = st[...]
            outs, stn = _hg_chunk(c1_ref[...], c3_ref[...], masks, q_ref[rows, :], f_ref[rows, :], i_ref[rows, :],
                                  g_ref[rows, :], tuple(st[_head_rows(hd), :] for hd in range(HG_HEADS)), lb_ref[...],
                                  nw_ref[...])
            for hd in range(HG_HEADS):
                y_ref[rows, _head_rows(hd)] = outs[hd].astype(BF16)
                st[_head_rows(hd), :] = stn[hd]

    return pl.pallas_call(
        kern, name=name, grid=(ng,), in_specs=act + other,
        out_specs=[pl.BlockSpec((HG_STEP * HG_CHUNK, 512), lambda i: (i, 0)),
                   pl.BlockSpec((HG_STEP, 512, LANES), lambda i: (i, 0, 0))],
        out_shape=[jax.ShapeDtypeStruct((s, 512), BF16), jax.ShapeDtypeStruct((nc, 512, LANES), F32)],
        scratch_shapes=[pltpu.VMEM((512, LANES), F32)], compiler_params=_cp(("arbitrary",)),
    )(h, h, h, h, *_hg_consts(), lb, nw)


def _hg_bwd(dy, dy_cb, h, sts, lb, nw, name):
    s = h.shape[0]
    ng = s // HG_CHUNK // HG_STEP
    n = HG_STEP * HG_CHUNK
    ci, act, other = _hg_specs(True, ng)

    def kern(dy_ref, sts_ref, q_ref, f_ref, i_ref, g_ref, c1_ref, c3_ref, mk_ref, lb_ref, nw_ref,
             dq_ref, df_ref, di_ref, dg_ref, dlb_ref, dnw_ref, dst):
        @pl.when(pl.program_id(0) == 0)
        def _():
            dst[...] = jnp.zeros_like(dst)
            dlb_ref[...] = jnp.zeros_like(dlb_ref)
            dnw_ref[...] = jnp.zeros_like(dnw_ref)

        masks = [mk_ref[lv] for lv in range(HG_LEVELS)]
        heads = range(HG_HEADS)
        for u in range(HG_STEP - 1, -1, -1):
            rows = _chunk_rows(u)
            _, pull = jax.vjp(functools.partial(_hg_chunk, c1_ref[...], c3_ref[...], masks), q_ref[rows, :], f_ref[rows, :],
                              i_ref[rows, :], g_ref[rows, :], tuple(sts_ref[u, _head_rows(hd), :] for hd in heads),
                              lb_ref[...], nw_ref[...])
            dq, df, di, dg, dsti, dlb, dnw = pull((tuple(dy_ref[rows, _head_rows(hd)] for hd in heads),
                                                   tuple(dst[_head_rows(hd), :] for hd in heads)))
            dq_ref[rows, :] = dq.astype(BF16)
            df_ref[rows, :] = df.astype(BF16)
            di_ref[rows, :] = di.astype(BF16)
            dg_ref[rows, :] = dg.astype(BF16)
            for hd in heads:
                dst[_head_rows(hd), :] = dsti[hd]
            dlb_ref[...] += dlb
            dnw_ref[...] += dnw

    blk = pl.BlockSpec((n, 512), lambda i: (ci(i), 0))
    return pl.pallas_call(
        kern, name=name, grid=(ng,),
        in_specs=[pl.BlockSpec((n, 512), lambda i: (ci(i), dy_cb)), pl.BlockSpec((HG_STEP, 512, LANES), lambda i: (ci(i), 0, 0))]
        + act + other,
        out_specs=[blk, blk, blk, blk, pl.BlockSpec((1, 512), lambda i: (0, 0)), pl.BlockSpec((1, LANES), lambda i: (0, 0))],
        out_shape=[jax.ShapeDtypeStruct((s, 512), BF16)] * 4 + [jax.ShapeDtypeStruct((1, 512), F32),
                                                                jax.ShapeDtypeStruct((1, LANES), F32)],
        scratch_shapes=[pltpu.VMEM((512, LANES), F32)], compiler_params=_cp(("arbitrary",)),
    )(dy, sts, h, h, h, h, *_hg_consts(), lb, nw)


def _swa_consts():
    n = SWA_BLOCK
    r = np.arange(8 * n)
    t, blk = r % n, r // n
    lanes = np.arange(LANES)
    own = (t[:, None] >= lanes[None, :]).astype(np.float32)
    prev = (lanes[None, :] > t[:, None]).astype(np.float32)
    sel = (lanes[None, :] == (blk // 2 + 4 * (blk % 2))[:, None]).astype(np.float32)
    grp = ((lanes[None, :] // 64) == (blk % 2)[:, None]).astype(np.float32)
    return tuple(jnp.asarray(a) for a in (own, prev, sel, grp))


def _swa_block(own, prev, sel, grp, q, k0, v0, k1, v1, srow):
    qs = _split_grid(q, 1, 4)
    qa = jnp.concatenate([qs[p] for p in range(4) for _ in range(2)], axis=0) * grp
    s0 = jnp.where(prev > 0.0, _bdot(qa, k0, NT) * 0.125, MASK_VALUE)
    s1 = jnp.where(own > 0.0, _bdot(qa, k1, NT) * 0.125, MASK_VALUE)
    sk = jnp.sum(sel * srow, 1, keepdims=True)
    m = lax.stop_gradient(jnp.maximum(jnp.maximum(jnp.max(s0, 1, keepdims=True), jnp.max(s1, 1, keepdims=True)), sk))
    p0, p1 = jnp.exp(s0 - m), jnp.exp(s1 - m)
    inv = 1.0 / (jnp.sum(p0, 1, keepdims=True) + jnp.sum(p1, 1, keepdims=True) + jnp.exp(sk - m))
    o = _split_grid((_bdot(p0 * inv, v0) + _bdot(p1 * inv, v1)) * grp, 8, 1)
    return tuple(o[2 * p] + o[2 * p + 1] for p in range(4))


def _swa_specs():
    n = SWA_BLOCK
    pv = lambda i: jnp.maximum(i - 1, 0)
    mask = pl.BlockSpec((8 * n, LANES), lambda i: (0, 0))
    return [mask, mask, mask, mask, pl.BlockSpec((n, 512), lambda i: (i, CD_Q // 512)),
            pl.BlockSpec((n, LANES), lambda i: (pv(i), CD_K // LANES)), pl.BlockSpec((n, LANES), lambda i: (pv(i), CD_V // LANES)),
            pl.BlockSpec((n, LANES), lambda i: (i, CD_K // LANES)), pl.BlockSpec((n, LANES), lambda i: (i, CD_V // LANES)),
            pl.BlockSpec((1, LANES), lambda i: (0, 0))]


def _swa_args(refs, notfirst):
    own, prev, sel, grp = (r[...] for r in refs[:4])
    return (own, prev * notfirst.astype(F32), sel, grp), tuple(r[...] for r in refs[4:])


def _swa_fwd(hc, srow, name):
    s = hc.shape[0]
    n = SWA_BLOCK

    def kern(*refs):
        y_ref = refs[10]
        consts, acts = _swa_args(refs[:10], pl.program_id(0) > 0)
        outs = _swa_block(*consts, *acts)
        for p in range(4):
            y_ref[:, LANES * p:LANES * (p + 1)] = outs[p].astype(BF16)

    return pl.pallas_call(
        kern, name=name, grid=(s // n,), in_specs=_swa_specs(), out_specs=pl.BlockSpec((n, 512), lambda i: (i, 0)),
        out_shape=jax.ShapeDtypeStruct((s, 512), BF16), compiler_params=_cp(("parallel",)),
    )(*_swa_consts(), hc, hc, hc, hc, hc, srow)


def _swa_bwd(dy, dy_cb, hc, srow, name):
    s = hc.shape[0]
    n = SWA_BLOCK

    def kern(dy_ref, *refs):
        dq_ref, dk_ref, dv_ref, ds_ref = refs[10:]
        i = pl.program_id(0)

        @pl.when(i == 0)
        def _():
            dk_ref[...] = jnp.zeros_like(dk_ref)
            dv_ref[...] = jnp.zeros_like(dv_ref)
            ds_ref[...] = jnp.zeros_like(ds_ref)

        r1 = pl.ds(pl.multiple_of(i * n, n), n)
        r0 = pl.ds(pl.multiple_of(jnp.maximum(i - 1, 0) * n, n), n)
        consts, acts = _swa_args(refs[:10], i > 0)
        _, pull = jax.vjp(functools.partial(_swa_block, *consts), *acts)
        dq, dk0, dv0, dk1, dv1, dsr = pull(tuple(dy_ref[:, LANES * p:LANES * (p + 1)] for p in range(4)))
        dq_ref[...] = dq.astype(BF16)
        dk_ref[r0, :] += dk0
        dv_ref[r0, :] += dv0
        dk_ref[r1, :] += dk1
        dv_ref[r1, :] += dv1
        ds_ref[...] += dsr

    full = pl.BlockSpec((s, LANES), lambda i: (0, 0))
    return pl.pallas_call(
        kern, name=name, grid=(s // n,), in_specs=[pl.BlockSpec((n, 512), lambda i: (i, dy_cb))] + _swa_specs(),
        out_specs=[pl.BlockSpec((n, 512), lambda i: (i, 0)), full, full, pl.BlockSpec((1, LANES), lambda i: (0, 0))],
        out_shape=[jax.ShapeDtypeStruct((s, 512), BF16), jax.ShapeDtypeStruct((s, LANES), F32),
                   jax.ShapeDtypeStruct((s, LANES), F32), jax.ShapeDtypeStruct((1, LANES), F32)],
        compiler_params=_cp(("arbitrary",)),
    )(dy, *_swa_consts(), hc, hc, hc, hc, hc, srow)


RG_TILE = 256


def _neg_expm1(x):
    ser = x * (1 + x / 2 * (1 + x / 3 * (1 + x / 4 * (1 + x / 5 * (1 + x / 6 * (1 + x / 7 * (1 + x / 8)))))))
    return jnp.where(x > -0.25, -ser, 1.0 - jnp.exp(x))


def _rg_gates(xc, wa, wx, ba, bx, lam):
    r = jax.nn.sigmoid(_bdot(xc, wa) + ba)
    i = jax.nn.sigmoid(_bdot(xc, wx) + bx)
    log_a = -RG_C * r * jax.nn.softplus(-lam)
    u = jnp.sqrt(jnp.maximum(_neg_expm1(2.0 * log_a), 0.0)) * (i * xc)
    return jnp.exp(log_a), u


def _rg_out(hs, gate):
    return hs * jax.nn.gelu(gate)


def _rows_to_tile(rows):
    sub = _iota((SUBLANES, 1), 0)
    out = jnp.broadcast_to(rows[0], (SUBLANES, rows[0].shape[1]))
    for j in range(1, SUBLANES):
        out = jnp.where(sub == j, rows[j], out)
    return out


def _rg_fwd(xc, hc, wa, wx, ba, bx, lam, name):
    s = xc.shape[0]
    tm = min(RG_TILE, s)

    def kern(x_ref, g_ref, wa_ref, wx_ref, ba_ref, bx_ref, lam_ref, y_ref, h_ref, a_s, u_s, carry):
        @pl.when(pl.program_id(0) == 0)
        def _():
            carry[...] = jnp.zeros_like(carry)

        a, u = _rg_gates(x_ref[...], wa_ref[...], wx_ref[...], ba_ref[...], bx_ref[...], lam_ref[...])
        a_s[...] = a
        u_s[...] = u

        def body(gi, hp):
            r0 = pl.multiple_of(gi * SUBLANES, SUBLANES)
            ab, ub = a_s[pl.ds(r0, SUBLANES), :], u_s[pl.ds(r0, SUBLANES), :]
            rows = []
            for j in range(SUBLANES):
                hp = ab[j:j + 1] * hp + ub[j:j + 1]
                rows.append(hp)
            h_ref[pl.ds(r0, SUBLANES), :] = _rows_to_tile(rows)
            return hp

        carry[0:1, :] = lax.fori_loop(0, tm // SUBLANES, body, carry[0:1, :])
        y_ref[...] = _rg_out(h_ref[...], g_ref[...]).astype(BF16)

    row = pl.BlockSpec((tm, 512), lambda i: (i, 0))
    mat = pl.BlockSpec((512, 512), lambda i: (0, 0))
    vec = pl.BlockSpec((1, 512), lambda i: (0, 0))
    return pl.pallas_call(
        kern, name=name, grid=(s // tm,),
        in_specs=[row, pl.BlockSpec((tm, 512), lambda i: (i, CD_GATE // 512)), mat, mat, vec, vec, vec],
        out_specs=[row, row],
        out_shape=[jax.ShapeDtypeStruct((s, 512), BF16), jax.ShapeDtypeStruct((s, 512), F32)],
        scratch_shapes=[pltpu.VMEM((tm, 512), F32), pltpu.VMEM((tm, 512), F32), pltpu.VMEM((SUBLANES, 512), F32)],
        compiler_params=_cp(("arbitrary",)),
    )(xc, hc, wa, wx, ba, bx, lam)


def _rg_bwd(dy, dy_cb, xc, hc, hs, wa, wx, ba, bx, lam, name):
    s = xc.shape[0]
    tm = min(RG_TILE, s)
    nt = s // tm
    ti = lambda i: nt - 1 - i

    def kern(dy_ref, x_ref, g_ref, h_ref, hp_ref, wa_ref, wx_ref, ba_ref, bx_ref, lam_ref,
             dx_ref, dg_ref, dwa_ref, dwx_ref, dba_ref, dbx_ref, dlam_ref, a_s, d_s, g_s, cg, ca):
        first = pl.program_id(0) == 0

        @pl.when(first)
        def _():
            cg[...] = jnp.zeros_like(cg)
            ca[...] = jnp.zeros_like(ca)

        (a, _), pull = jax.vjp(_rg_gates, x_ref[...], wa_ref[...].astype(F32), wx_ref[...].astype(F32), ba_ref[...],
                               bx_ref[...], lam_ref[...])
        hs_t = h_ref[...]
        _, pull_out = jax.vjp(_rg_out, hs_t, g_ref[...])
        dho, dgate = pull_out(dy_ref[...])
        dg_ref[...] = dgate.astype(BF16)
        a_s[...] = a
        d_s[...] = dho

        def body(k, c):
            gn, an = c
            r0 = pl.multiple_of((tm // SUBLANES - 1 - k) * SUBLANES, SUBLANES)
            ab, db = a_s[pl.ds(r0, SUBLANES), :], d_s[pl.ds(r0, SUBLANES), :]
            rows = [None] * SUBLANES
            for j in range(SUBLANES - 1, -1, -1):
                gn = db[j:j + 1] + an * gn
                an = ab[j:j + 1]
                rows[j] = gn
            g_s[pl.ds(r0, SUBLANES), :] = _rows_to_tile(rows)
            return gn, an

        gn, an = lax.fori_loop(0, tm // SUBLANES, body, (cg[0:1, :], ca[0:1, :]))
        cg[0:1, :] = gn
        ca[0:1, :] = an
        g = g_s[...]
        halo = jnp.where(pl.program_id(0) == nt - 1, 0.0, hp_ref[...])
        hprev = pltpu.roll(jnp.concatenate([halo, hs_t], axis=0), 1, axis=0)[SUBLANES:]
        dxc, dwa, dwx, dba, dbx, dlam = pull((g * hprev, g))
        dx_ref[...] = dxc
        _acc(dwa_ref, dwa, first)
        _acc(dwx_ref, dwx, first)
        _acc(dba_ref, dba, first)
        _acc(dbx_ref, dbx, first)
        _acc(dlam_ref, dlam, first)

    row = pl.BlockSpec((tm, 512), lambda i: (ti(i), 0))
    mat = pl.BlockSpec((512, 512), lambda i: (0, 0))
    vec = pl.BlockSpec((1, 512), lambda i: (0, 0))
    hprev = pl.BlockSpec((SUBLANES, 512), lambda i: (jnp.maximum(ti(i) * (tm // SUBLANES) - 1, 0), 0))
    return pl.pallas_call(
        kern, name=name, grid=(nt,),
        in_specs=[pl.BlockSpec((tm, 512), lambda i: (ti(i), dy_cb)), row,
                  pl.BlockSpec((tm, 512), lambda i: (ti(i), CD_GATE // 512)), row, hprev, mat, mat, vec, vec, vec],
        out_specs=[row, row, mat, mat, vec, vec, vec],
        out_shape=[jax.ShapeDtypeStruct((s, 512), F32), jax.ShapeDtypeStruct((s, 512), BF16),
                   jax.ShapeDtypeStruct((512, 512), F32), jax.ShapeDtypeStruct((512, 512), F32),
                   jax.ShapeDtypeStruct((1, 512), F32), jax.ShapeDtypeStruct((1, 512), F32), jax.ShapeDtypeStruct((1, 512), F32)],
        scratch_shapes=[pltpu.VMEM((tm, 512), F32)] * 3 + [pltpu.VMEM((SUBLANES, 512), F32)] * 2,
        compiler_params=_cp(("arbitrary",)),
    )(dy, xc, hc, hs, hs, wa, wx, ba, bx, lam)


AB_REF = dict(z=(0, 512), xbc=(512, 768), dt=(1280, 8), hq=(1288, 512), hf=(1800, 512), hi=(2312, 512), hg=(2824, 512))
AB_INT = dict(z=AB_Z, hq=AB_HQ, hf=AB_HF, hi=AB_HI, hg=AB_HG, xbc=AB_XBC, dt=AB_DT)
CD_REF = dict(q=(0, 512), k=(512, 128), v=(640, 128), gate=(768, 512), xr=(1280, 512))


def _cols(w, start, size):
    return lax.slice_in_dim(w, start, start + size, axis=w.ndim - 1)


def _pad_lanes(v):
    return jnp.concatenate([v.astype(F32), jnp.zeros((LANES - v.shape[0],), F32)]).reshape(1, LANES)


def _taker(g, layer, axis):
    width = g.shape[2 + axis]

    def take(start, size):
        out = []
        while size > 0:
            k, off = divmod(start, width)
            n = min(size, width - off)
            out.append(lax.slice_in_dim(g[k, layer], off, off + n, axis=axis))
            start, size = start + n, size - n
        return out

    return take


def _ab_in_to_int(g, layer):
    take = _taker(g, layer, 1)
    parts = sum([take(*AB_REF[k]) for k in ("z", "hq", "hf", "hi", "hg", "xbc", "dt")], [])
    return jnp.concatenate(parts + [jnp.zeros((g.shape[2], AB_COLS - AB_DT - 8), g.dtype)], axis=1)


def _ab_in_to_ref(w):
    return jnp.concatenate([_cols(w, AB_INT[k], AB_REF[k][1]) for k in ("z", "xbc", "dt", "hq", "hf", "hi", "hg")], axis=-1)


def _pairs_to_ref(w, axis):
    sl = lambda i: lax.slice_in_dim(w, 64 * i, 64 * i + 64, axis=axis)
    return jnp.concatenate([sl(2 * h) for h in range(4)] + [sl(2 * h + 1) for h in range(4)], axis=axis)


PAIR_ORDER = tuple(h for p in range(4) for h in (p, 4 + p))


def _cd_in_to_int(g, layer):
    take = _taker(g, layer, 1)
    q = sum([take(64 * h, 64) for h in PAIR_ORDER], [])
    return jnp.concatenate(q + sum([take(*CD_REF[k]) for k in ("gate", "xr", "k", "v")], []), axis=1)


def _cd_out_to_int(g, layer):
    take = _taker(g, layer, 0)
    return jnp.concatenate(sum([take(64 * h, 64) for h in PAIR_ORDER], []) + take(512, 512), axis=0)


def _whole(g, layer, axis):
    return jnp.concatenate([g[k, layer] for k in range(4)], axis=axis)


def _cd_in_to_ref(w):
    q = _pairs_to_ref(_cols(w, CD_Q, 512), w.ndim - 1)
    return jnp.concatenate([q, _cols(w, CD_K, 128), _cols(w, CD_V, 128), _cols(w, CD_GATE, 512), _cols(w, CD_XR, 512)], axis=-1)


def _block_diag(w):
    tiled = jnp.concatenate([w.reshape(RG_WIDTH, 64)] * 8, axis=1)
    return tiled * _BLOCK_DIAG_MASK.astype(w.dtype)


_BLOCK_DIAG_MASK = np.kron(np.eye(8, dtype=np.float32), np.ones((64, 64), np.float32))


def _diag_blocks(m):
    return jnp.sum((m * _BLOCK_DIAG_MASK).reshape(8, 64, 8, 64), axis=2)


def _conv8(w):
    return jnp.concatenate([w, jnp.zeros((SUBLANES - w.shape[0], w.shape[1]), w.dtype)], axis=0)


def _hg_lower_bounds(hg_lower):
    sm = jax.nn.softmax(hg_lower.astype(F32), axis=0)
    return jnp.clip(jnp.cumsum(sm, axis=0) - sm[0], 0.0, 1.0)


def _local_step(x, target, w):
    s = x.shape[0]
    sv_w = [(_whole(w["ffn_w_up"], l, 1), _whole(w["ffn_w_down"], l, 0)) for l in range(DEPTH)]
    lb_all, lb_pull = jax.vjp(_hg_lower_bounds, w["hg_lower"])
    grads = {k: [None] * (v.shape[1] if k in BIG else v.shape[0]) for k, v in w.items()}
    saved = []
    xb = x.astype(BF16)
    for l in range(DEPTH):
        j = l // 2
        t = f"l{l}"
        sv = dict(x=x, xb=xb)
        if l % 2 == 0:
            win = _ab_in_to_int(w["ab_w_in"], j)
            wout = _whole(w["ab_w_out"], j, 0)
            h = _mm([(xb, 0, win, 0, D_MODEL)], AB_COLS, F32, t + "_in")
            w8 = _conv8(w["ssd_conv_w"][j])
            xbcc = _dwconv_fwd(h, AB_XBC // CONV_CB, SSD_CONV_DIM, w8, w["ssd_conv_b"][j], 4, t + "_conv")
            sp = (_pad_lanes(w["ssd_dt_bias"][j]), _pad_lanes(w["ssd_a_log"][j]), _pad_lanes(w["ssd_d"][j]),
                  w["ssd_norm_w"][j].reshape(1, 512))
            ya, ssts = _ssd_fwd(h, xbcc, *sp, t + "_ssd")
            hp = (lb_all[j].reshape(1, 512), w["hg_norm_w"][j].reshape(1, LANES))
            yb, hsts = _hg_fwd(h, *hp, t + "_hg")
            sv.update(win=win, wout=wout, h=h, w8=w8, xbcc=xbcc, sp=sp, ssts=ssts, hp=hp, hsts=hsts)
        else:
            win = _cd_in_to_int(w["cd_w_in"], j)
            wout = _cd_out_to_int(w["cd_w_out"], j)
            h = _mm([(xb, 0, win, 0, D_MODEL)], CD_COLS, F32, t + "_in")
            srow = _pad_lanes(w["swa_sinks"][j])
            ya = _swa_fwd(h, srow, t + "_swa")
            w8 = _conv8(w["rg_conv_w"][j])
            xc = _dwconv_fwd(h, CD_XR // CONV_CB, RG_WIDTH, w8, w["rg_conv_b"][j], 4, t + "_conv")
            rp = (_block_diag(w["rg_wa"][j]).astype(BF16), _block_diag(w["rg_wx"][j]).astype(BF16),
                  w["rg_ba"][j].reshape(1, 512), w["rg_bx"][j].reshape(1, 512), w["rg_lambda"][j].reshape(1, 512))
            yb, hs = _rg_fwd(xc, h, *rp, t + "_rg")
            sv.update(win=win, wout=wout, h=h, w8=w8, srow=srow, xc=xc, rp=rp, hs=hs)
        x1, x1b, r1 = _mm_ln([(ya, 0, wout, 0, 512), (yb, 0, wout, 1, 512)], x, w["ln_g"][l, 0], w["ln_b"][l, 0], t + "_out_ln")
        wup, wdown = sv_w[l]
        fw8 = _conv8(w["ffn_conv_w"][l])
        hu = _mm([(x1b, 0, wup, 0, D_MODEL)], 2 * FFN_DIM, BF16, t + "_up")
        a, hg, hv = _ffn_gate_fwd(hu, fw8, w["ffn_conv_b"][l], t + "_gate")
        x, xb, r2 = _mm_ln([(a, 0, wdown, 0, FFN_DIM)], x1, w["ln_g"][l, 1], w["ln_b"][l, 1], t + "_down_ln")
        sv.update(ya=ya, yb=yb, r1=r1, x1b=x1b, hu=hu, hg=hg, hv=hv, a=a, r2=r2, fw8=fw8)
        saved.append(sv)

    loss = 0.5 * jnp.sum(_sqerr(x, target, "sqerr")) / D_MODEL
    dya, dyb, ca, cb = x, target, 1.0 / D_MODEL, -1.0 / D_MODEL
    for l in range(DEPTH - 1, -1, -1):
        j = l // 2
        t = f"l{l}b"
        sv = saved[l]
        wup, wdown = sv_w[l]
        dr2, dg2, db2 = _ln_bwd(dya, dyb, ca, cb, sv["r2"], w["ln_g"][l, 1], t + "_ln2")
        da = _mm([(dr2, 0, wdown, 0, D_MODEL)], FFN_DIM, BF16, t + "_down", trans_b=True)
        grads["ffn_w_down"][l] = _mm_tn(sv["a"], 0, FFN_DIM, dr2, 0, D_MODEL, t + "_wdown")
        dhg, dhu, dfw8, dfb = _ffn_gate_bwd(sv["hu"], sv["hg"], sv["hv"], da, sv["fw8"], t + "_gate")
        grads["ffn_conv_w"][l] = dfw8[:3]
        grads["ffn_conv_b"][l] = dfb
        dx1 = _mm([(dhg, 0, wup, 0, FFN_DIM), (dhu, 0, wup, 1, FFN_DIM)], D_MODEL, F32, t + "_up", trans_b=True)
        grads["ffn_w_up"][l] = jnp.concatenate([_mm_tn(sv["x1b"], 0, D_MODEL, dhg, 0, FFN_DIM, t + "_wup_g"),
                                                _mm_tn(sv["x1b"], 0, D_MODEL, dhu, 0, FFN_DIM, t + "_wup_u")], axis=1)
        dr1, dg1, db1 = _ln_bwd(dr2, dx1, ALPHA, 1.0, sv["r1"], w["ln_g"][l, 0], t + "_ln1")
        grads["ln_g"][l] = jnp.stack([dg1, dg2])
        grads["ln_b"][l] = jnp.stack([db1, db2])
        win, wout, h = sv["win"], sv["wout"], sv["h"]
        dycat = _mm([(dr1, 0, wout, 0, D_MODEL)], D_MODEL, F32, t + "_out", trans_b=True)
        dwout = jnp.concatenate([_mm_tn(sv["ya"], 0, 512, dr1, 0, D_MODEL, t + "_wout_a"),
                                 _mm_tn(sv["yb"], 0, 512, dr1, 0, D_MODEL, t + "_wout_b")], axis=0)
        if l % 2 == 0:
            dz, dxbcc, ddt, ddtb, dal, ddsk, dnw = _ssd_bwd(dycat, 0, h, sv["xbcc"], sv["ssts"], *sv["sp"], t + "_ssd")
            dxbc, dcw, dcb = _dwconv_bwd(dxbcc, h, AB_XBC // CONV_CB, SSD_CONV_DIM, sv["w8"], 4, t + "_conv")
            dq, df, di, dg, dlb, dhnw = _hg_bwd(dycat, 1, h, sv["hsts"], *sv["hp"], t + "_hg")
            grads["ab_w_out"][j] = dwout
            grads["ssd_conv_w"][j], grads["ssd_conv_b"][j] = dcw[:4], dcb
            grads["ssd_dt_bias"][j], grads["ssd_a_log"][j], grads["ssd_d"][j] = ddtb[0, :8], dal[0, :8], ddsk[0, :8]
            grads["ssd_norm_w"][j], grads["hg_norm_w"][j] = dnw[0], dhnw[0]
            grads["hg_lower"][j] = dlb[0]
            pieces = [(dz, 0, AB_Z, 512), (dq, 0, AB_HQ, 512), (df, 0, AB_HF, 512), (di, 0, AB_HI, 512), (dg, 0, AB_HG, 512),
                      (dxbc, 0, AB_XBC, 256), (dxbc, 1, AB_XBC + 256, 256), (dxbc, 2, AB_XBC + 512, 256), (ddt, 0, AB_DT, LANES)]
            dwin = [_mm_tn(sv["xb"], 0, D_MODEL, p, 0, p.shape[1], t + f"_win{i}")
                    for i, p in enumerate((dz, dq, df, di, dg, dxbc, ddt))]
            grads["ab_w_in"][j] = _ab_in_to_ref(jnp.concatenate(dwin, axis=1))
        else:
            dq, dk, dv, dsr = _swa_bwd(dycat, 0, h, sv["srow"], t + "_swa")
            dxc, dgate, dwa, dwx, dba, dbx, dlam = _rg_bwd(dycat, 1, sv["xc"], h, sv["hs"], *sv["rp"], t + "_rg")
            dxr, dcw, dcb = _dwconv_bwd(dxc, h, CD_XR // CONV_CB, RG_WIDTH, sv["w8"], 4, t + "_conv")
            grads["cd_w_out"][j] = jnp.concatenate([_pairs_to_ref(dwout[:512], 0), dwout[512:]], axis=0)
            grads["swa_sinks"][j] = dsr[0, :8]
            grads["rg_conv_w"][j], grads["rg_conv_b"][j] = dcw[:4], dcb
            grads["rg_wa"][j], grads["rg_wx"][j] = _diag_blocks(dwa), _diag_blocks(dwx)
            grads["rg_ba"][j], grads["rg_bx"][j], grads["rg_lambda"][j] = dba[0], dbx[0], dlam[0]
            pieces = [(dq, 0, CD_Q, 512), (dgate, 0, CD_GATE, 512), (dxr, 0, CD_XR, 512), (dk, 0, CD_K, LANES), (dv, 0, CD_V, LANES)]
            dwin = [_mm_tn(sv["xb"], 0, D_MODEL, p, 0, p.shape[1], t + f"_win{i}")
                    for i, p in enumerate((dq, dgate, dxr, dk, dv))]
            grads["cd_w_in"][j] = _cd_in_to_ref(jnp.concatenate(dwin, axis=1))
        dx0 = _mm([(p, pcb, win, off // k, k) for p, pcb, off, k in pieces], D_MODEL, F32, t + "_in", trans_b=True)
        dya, dyb, ca, cb = dr1, dx0, ALPHA, 1.0
    grad_x = _axpby(dya, dyb, ca, cb, "grad_x")
    out = {k: (v if k in BIG else jnp.stack(v)) for k, v in grads.items()}
    (out["hg_lower"],) = lb_pull(out["hg_lower"])
    return loss, grad_x, out


MESH = pl.DeviceIdType.MESH
ANY = pl.BlockSpec(memory_space=pl.ANY)


def _place():
    x, y, c = lax.axis_index("x"), lax.axis_index("y"), lax.axis_index("c")
    return x, y, c, [(1 - x, y), (x, 1 - y), (1 - x, 1 - y)]


def _rcopy(src, dst, send_sems, recv_sems, k, to):
    return pltpu.make_async_remote_copy(src_ref=src, dst_ref=dst, send_sem=send_sems.at[k], recv_sem=recv_sems.at[k],
                                        device_id=to, device_id_type=MESH)


def _zeros_index(ndim):
    return (0,) * ndim


def _gather_chips(parts, name):
    n = len(parts)

    def body(*refs):
        w_refs, out_refs, (send_sems, recv_sems) = refs[:n], refs[n:2 * n], refs[2 * n:]
        x, y, c, chips = _place()
        me, sib = 2 * x + y, (x, y, 1 - c)
        first = [_rcopy(w_refs[a].at[c], out_refs[a].at[me, c], send_sems, recv_sems, 6 * a + k, (cx, cy, c))
                 for a in range(n) for k, (cx, cy) in enumerate(chips)]
        for cp in first:
            cp.start()
        passed = []
        for a in range(n):
            for k, (cx, cy) in enumerate(chips):
                blk = out_refs[a].at[2 * cx + cy, c]
                _rcopy(blk, blk, send_sems, recv_sems, 6 * a + k, (cx, cy, c)).wait_recv()
                passed.append(_rcopy(blk, blk, send_sems, recv_sems, 6 * a + 3 + k, sib))
                passed[-1].start()
        for a in range(n):
            for k, (cx, cy) in enumerate(chips):
                blk = out_refs[a].at[2 * cx + cy, 1 - c]
                _rcopy(blk, blk, send_sems, recv_sems, 6 * a + 3 + k, sib).wait_recv()
        for cp in first + passed:
            cp.wait_send()

    outs = pl.pallas_call(
        body, name=name, in_specs=[ANY] * n, out_specs=[ANY] * n,
        out_shape=[jax.ShapeDtypeStruct((4,) + p.shape, p.dtype) for p in parts],
        scratch_shapes=[pltpu.SemaphoreType.DMA((6 * n,)), pltpu.SemaphoreType.DMA((6 * n,))],
    )(*parts)
    me = 2 * lax.axis_index("x") + lax.axis_index("y")
    return [lax.dynamic_update_slice(o, p[None], (me,) + _zeros_index(p.ndim)) for o, p in zip(outs, parts)]


def _swap_halves(gs, name):
    n = len(gs)

    def body(*refs):
        g_refs, x_refs, (send_sems, recv_sems) = refs[:n], refs[n:2 * n], refs[2 * n:]
        x, y, c, _ = _place()
        cps = [_rcopy(g_refs[a].at[:, 1 - c], x_refs[a], send_sems, recv_sems, a, (x, y, 1 - c)) for a in range(n)]
        for cp in cps:
            cp.start()
        for cp in cps:
            cp.wait()

    return pl.pallas_call(
        body, name=name, in_specs=[ANY] * n, out_specs=[ANY] * n,
        out_shape=[jax.ShapeDtypeStruct((g.shape[0],) + g.shape[2:], g.dtype) for g in gs],
        scratch_shapes=[pltpu.SemaphoreType.DMA((n,)), pltpu.SemaphoreType.DMA((n,))],
    )(*gs)


def _scatter_chips(hbs, name):
    n = len(hbs)

    def body(*refs):
        h_refs, y_refs, (send_sems, recv_sems) = refs[:n], refs[n:2 * n], refs[2 * n:]
        x, y, c, chips = _place()
        me = 2 * x + y
        sends = [_rcopy(h_refs[a].at[2 * cx + cy], y_refs[a].at[me], send_sems, recv_sems, 3 * a + k, (cx, cy, c))
                 for a in range(n) for k, (cx, cy) in enumerate(chips)]
        for cp in sends:
            cp.start()
        for a in range(n):
            for k, (cx, cy) in enumerate(chips):
                blk = y_refs[a].at[2 * cx + cy]
                _rcopy(blk, blk, send_sems, recv_sems, 3 * a + k, (cx, cy, c)).wait_recv()
        for cp in sends:
            cp.wait_send()

    outs = pl.pallas_call(
        body, name=name, in_specs=[ANY] * n, out_specs=[ANY] * n,
        out_shape=[jax.ShapeDtypeStruct(h.shape, h.dtype) for h in hbs],
        scratch_shapes=[pltpu.SemaphoreType.DMA((3 * n,)), pltpu.SemaphoreType.DMA((3 * n,))],
    )(*hbs)
    me = 2 * lax.axis_index("x") + lax.axis_index("y")
    return [lax.dynamic_update_slice(o, lax.dynamic_slice_in_dim(h, me, 1, axis=0), (me,) + _zeros_index(h.ndim - 1))
            for o, h in zip(outs, hbs)]


def _share_half(fs, name):
    n = len(fs)

    def body(*refs):
        f_refs, out_refs, (send_sems, recv_sems) = refs[:n], refs[n:2 * n], refs[2 * n:]
        x, y, c, _ = _place()
        sib = (x, y, 1 - c)
        cps = [_rcopy(f_refs[a], out_refs[a].at[c], send_sems, recv_sems, a, sib) for a in range(n)]
        for cp in cps:
            cp.start()
        for a in range(n):
            _rcopy(f_refs[a], out_refs[a].at[1 - c], send_sems, recv_sems, a, sib).wait_recv()
        for cp in cps:
            cp.wait_send()

    outs = pl.pallas_call(
        body, name=name, in_specs=[ANY] * n, out_specs=[ANY] * n,
        out_shape=[jax.ShapeDtypeStruct((2,) + f.shape, f.dtype) for f in fs],
        scratch_shapes=[pltpu.SemaphoreType.DMA((n,)), pltpu.SemaphoreType.DMA((n,))],
    )(*fs)
    c = lax.axis_index("c")
    return [lax.dynamic_update_slice(o, f[None], (c,) + _zeros_index(f.ndim)) for o, f in zip(outs, fs)]


def _gather_devices(v, name):
    m_per, n = v.shape

    def body(x_ref, out_ref, send_sems, recv_sems, local_sem):
        x, y, c, chips = _place()
        me, sib = (x, y, c), (x, y, 1 - c)

        def rows(px, py, pc):
            return out_ref.at[pl.ds((4 * px + 2 * py + pc) * m_per, m_per), :]

        def copy(k, block, to, src=None):
            return _rcopy(rows(*block) if src is None else src, rows(*block), send_sems, recv_sems, k, to)

        mine = pltpu.make_async_copy(x_ref, rows(*me), local_sem)
        mine.start()
        first = [copy(0, me, sib, src=x_ref)]
        first += [copy(1 + j, me, (*chip, c), src=x_ref) for j, chip in enumerate(chips)]
        for cp in first:
            cp.start()
        passed = [copy(4 + j, (*chip, c), sib) for j, chip in enumerate(chips)]
        for j, chip in enumerate(chips):
            copy(1 + j, (*chip, c), me).wait_recv()
            passed[j].start()
        copy(0, sib, me).wait_recv()
        for j, chip in enumerate(chips):
            copy(4 + j, (*chip, 1 - c), me).wait_recv()
        for cp in first + passed:
            cp.wait_send()
        mine.wait()

    return pl.pallas_call(
        body, name=name, out_shape=jax.ShapeDtypeStruct((8 * m_per, n), v.dtype),
        in_specs=[pl.BlockSpec(memory_space=pltpu.VMEM)], out_specs=pl.BlockSpec(memory_space=pltpu.VMEM),
        scratch_shapes=[pltpu.SemaphoreType.DMA((7,)), pltpu.SemaphoreType.DMA((7,)), pltpu.SemaphoreType.DMA(())],
        compiler_params=_cp(),
    )(v)


ROW_TILES = (512, 256, 352, 128)


def _add_half(g, xr, c, name):
    _, _, r, n = g.shape
    tr = _pick(r, ROW_TILES)

    def kern(c_ref, g_ref, x_ref, o_ref):
        o_ref[...] = (g_ref[0] + x_ref[...]).astype(BF16)

    return pl.pallas_call(
        kern, name=name,
        grid_spec=pltpu.PrefetchScalarGridSpec(
            num_scalar_prefetch=1, grid=(4, r // tr),
            in_specs=[pl.BlockSpec((1, 1, tr, n), lambda k, i, c_ref: (k, c_ref[0], i, 0)),
                      pl.BlockSpec((1, tr, n), lambda k, i, c_ref: (k, i, 0))],
            out_specs=pl.BlockSpec((1, tr, n), lambda k, i, c_ref: (k, i, 0))),
        out_shape=jax.ShapeDtypeStruct(xr.shape, BF16), compiler_params=_cp(("parallel", "parallel")),
    )(c.reshape(1).astype(jnp.int32), g, xr)


def _sum_blocks(y, name, tr=None):
    n, r, cols = y.shape
    tr = tr or _pick(r, ROW_TILES)
    tr = tr if r % tr == 0 else r

    def kern(y_ref, o_ref):
        acc = y_ref[0].astype(F32)
        for k in range(1, n):
            acc = acc + y_ref[k].astype(F32)
        o_ref[...] = acc

    return pl.pallas_call(
        kern, name=name, grid=(r // tr,), in_specs=[pl.BlockSpec((n, tr, cols), lambda i: (0, i, 0))],
        out_specs=pl.BlockSpec((tr, cols), lambda i: (i, 0)), out_shape=jax.ShapeDtypeStruct((r, cols), F32),
        compiler_params=_cp(("parallel",)),
    )(y)


def _adamw(w, g, m, v, name, tr=256):
    r, n = w.shape
    tr = tr if r % tr == 0 else r

    def kern(w_ref, g_ref, m_ref, v_ref, d_ref, nm_ref, nv_ref):
        gg = g_ref[...]
        nm = ADAM_B1 * m_ref[...] + (1.0 - ADAM_B1) * gg
        nv = ADAM_B2 * v_ref[...] + (1.0 - ADAM_B2) * (gg * gg)
        m_hat = nm / (1.0 - ADAM_B1 ** ADAM_STEP)
        v_hat = nv / (1.0 - ADAM_B2 ** ADAM_STEP)
        d_ref[...] = -ADAM_LR * (m_hat / (jnp.sqrt(v_hat) + ADAM_EPS) + ADAM_WD * w_ref[...])
        nm_ref[...] = nm
        nv_ref[...] = nv

    blk = pl.BlockSpec((tr, n), lambda i: (i, 0))
    return pl.pallas_call(
        kern, name=name, grid=(r // tr,), in_specs=[blk] * 4, out_specs=[blk] * 3,
        out_shape=[jax.ShapeDtypeStruct((r, n), F32)] * 3, compiler_params=_cp(("parallel",)),
    )(w, g, m, v)


def _rows_of(n):
    return -(-n // (LANES * SUBLANES)) * SUBLANES


def _pack(arrs, lead, dtype, mult):
    ls = arrs[0].shape[:lead]
    blocks, total = [], 0
    for a in arrs:
        f = a.astype(dtype).reshape(ls + (-1,))
        n = f.shape[-1]
        rows = _rows_of(n)
        if rows * LANES != n:
            f = jnp.concatenate([f, jnp.zeros(ls + (rows * LANES - n,), dtype)], axis=-1)
        blocks.append(f.reshape(ls + (rows, LANES)))
        total += rows
    if total % mult:
        blocks.append(jnp.zeros(ls + (mult - total % mult, LANES), dtype))
    return jnp.concatenate(blocks, axis=lead)


def _unpack(buf, lead, shapes):
    ls = buf.shape[:lead]
    out, off = [], 0
    for sh in shapes:
        n = int(np.prod(sh))
        rows = _rows_of(n)
        flat = lax.slice_in_dim(buf, off, off + rows, axis=lead).reshape(ls + (rows * LANES,))
        if rows * LANES != n:
            flat = lax.slice_in_dim(flat, 0, n, axis=lead)
        out.append(flat.reshape(ls + tuple(sh)))
        off += rows
    return out


BIG = ("ab_w_in", "ab_w_out", "cd_w_in", "cd_w_out", "ffn_w_up", "ffn_w_down")
BIG_COLS = ("ab_w_in", "cd_w_in", "ffn_w_up")
SMALL_SHARDED = ("ssd_conv_w", "rg_conv_w", "rg_conv_b", "rg_ba", "rg_bx", "rg_lambda", "ffn_conv_w", "ln_g", "ln_b")
WEIGHTS = ("ab_w_in", "ssd_conv_w", "ssd_conv_b", "ssd_dt_bias", "ssd_a_log", "ssd_d", "ssd_norm_w", "hg_lower", "hg_norm_w",
           "ab_w_out", "cd_w_in", "swa_sinks", "rg_conv_w", "rg_conv_b", "rg_wa", "rg_ba", "rg_wx", "rg_bx", "rg_lambda",
           "cd_w_out", "ffn_w_up", "ffn_conv_w", "ffn_conv_b", "ffn_w_down", "ln_g", "ln_b")
SMALL = tuple(n for n in WEIGHTS if n not in BIG)


def _full_from_shards(name, g):
    if name in BIG_COLS or name in SMALL_SHARDED:
        return jnp.concatenate([g[k] for k in range(4)], axis=-1)
    return jnp.concatenate([g[k] for k in range(4)], axis=1)


def _layer_shards(name, f):
    k, n = f.shape
    if name in BIG_COLS:
        return jnp.transpose(f.reshape(k, 4, n // 4), (1, 0, 2))
    return f.reshape(4, k // 4, n)


def kernel(x, ab_w_in, ssd_conv_w, ssd_conv_b, ssd_dt_bias, ssd_a_log, ssd_d, ssd_norm_w, hg_lower, hg_norm_w, ab_w_out, cd_w_in, swa_sinks, rg_conv_w, rg_conv_b, rg_wa, rg_ba, rg_wx, rg_bx, rg_lambda, cd_w_out, ffn_w_up, ffn_conv_w, ffn_conv_b, ffn_w_down, ln_g, ln_b, loss_target, m_ab_w_in, m_ssd_conv_w, m_ssd_conv_b, m_ssd_dt_bias, m_ssd_a_log, m_ssd_d, m_ssd_norm_w, m_hg_lower, m_hg_norm_w, m_ab_w_out, m_cd_w_in, m_swa_sinks, m_rg_conv_w, m_rg_conv_b, m_rg_wa, m_rg_ba, m_rg_wx, m_rg_bx, m_rg_lambda, m_cd_w_out, m_ffn_w_up, m_ffn_conv_w, m_ffn_conv_b, m_ffn_w_down, m_ln_g, m_ln_b, v_ab_w_in, v_ssd_conv_w, v_ssd_conv_b, v_ssd_dt_bias, v_ssd_a_log, v_ssd_d, v_ssd_norm_w, v_hg_lower, v_hg_norm_w, v_ab_w_out, v_cd_w_in, v_swa_sinks, v_rg_conv_w, v_rg_conv_b, v_rg_wa, v_rg_ba, v_rg_wx, v_rg_bx, v_rg_lambda, v_cd_w_out, v_ffn_w_up, v_ffn_conv_w, v_ffn_conv_b, v_ffn_w_down, v_ln_g, v_ln_b):
    args = locals()
    w = {n: args[n] for n in WEIGHTS}
    mom = {n: args["m_" + n] for n in WEIGHTS}
    var = {n: args["v_" + n] for n in WEIGHTS}
    cx, cy, cc = lax.axis_index("x"), lax.axis_index("y"), lax.axis_index("c")
    chip = 2 * cx + cy

    halves = [w[n].astype(BF16).reshape((2, w[n].shape[0] // 2) + w[n].shape[1:]) for n in BIG]
    gathered = _gather_chips(halves, "gather_weights")
    full = {n: g.reshape((4,) + w[n].shape) for n, g in zip(BIG, gathered)}
    sp = _pack([w[n] for n in SMALL_SHARDED], 0, F32, SUBLANES)
    sg = _gather_devices(sp, "gather_small").reshape(4, 2, sp.shape[0], LANES)[:, 0]
    for n, g in zip(SMALL_SHARDED, _unpack(sg, 1, [w[n].shape for n in SMALL_SHARDED])):
        full[n] = _full_from_shards(n, g)
    for n in SMALL:
        full.setdefault(n, w[n])

    loss, grad_x, grads = _local_step(x[0], loss_target[0], full)
    loss = lax.psum(loss, ("x", "y", "c"))

    g4 = []
    for n in BIG:
        st = jnp.stack([_layer_shards(n, f) for f in grads[n]], axis=1)
        g4.append(st.reshape(4, 2, st.shape[1] // 2 * st.shape[2], st.shape[3]))
    hbs = [_add_half(g, xr, cc, "add_halves_" + n) for n, g, xr in zip(BIG, g4, _swap_halves(g4, "swap_halves"))]
    own = [_sum_blocks(y, "sum_chips_" + n) for n, y in zip(BIG, _scatter_chips(hbs, "scatter_chips"))]
    gout = {n: o.reshape(w[n].shape) for n, o in zip(BIG, _share_half(own, "share_half"))}

    small_shapes = [grads[n].shape for n in SMALL]
    gs = _pack([grads[n] for n in SMALL], 0, F32, SUBLANES)
    gsum = _sum_blocks(_gather_devices(gs, "gather_small_grads").reshape(8, gs.shape[0], LANES), "sum_small")
    for n, g in zip(SMALL, _unpack(gsum, 0, small_shapes)):
        if n in SMALL_SHARDED:
            width = w[n].shape[-1]
            g = lax.dynamic_slice_in_dim(g, chip * width, width, axis=g.ndim - 1)
        gout[n] = g

    delta, new_m, new_v = {}, {}, {}
    for n in BIG:
        sh = w[n].shape
        two = lambda a: a.reshape(-1, sh[-1])
        d, nm, nv = _adamw(two(w[n]), two(gout[n]), two(mom[n]), two(var[n]), "adamw_" + n)
        delta[n], new_m[n], new_v[n] = d.reshape(sh), nm.reshape(sh), nv.reshape(sh)
    local_shapes = [w[n].shape for n in SMALL]
    packs = [_pack([d[n] for n in SMALL], 0, F32, SUBLANES) for d in (w, gout, mom, var)]
    for dst, buf in zip((delta, new_m, new_v), _adamw(*packs, "adamw_small")):
        dst.update(zip(SMALL, _unpack(buf, 0, local_shapes)))

    return (loss, grad_x[None], *[gout[n] for n in WEIGHTS], *[delta[n] for n in WEIGHTS],
            *[new_m[n] for n in WEIGHTS], *[new_v[n] for n in WEIGHTS])
```

```python
import functools

import numpy as np
import jax
import jax.numpy as jnp
from jax import lax
from jax.experimental import pallas as pl
from jax.experimental.pallas import tpu as pltpu

F32 = jnp.float32
BF16 = jnp.bfloat16

D_MODEL = 1024
DEPTH = 4
SSD_HEADS = 8
SSD_D_INNER = 512
SSD_CONV_DIM = 768
SSD_CHUNK = 128
HG_CHUNK = 64
HG_HEADS = 4
SWA_BLOCK = 128
RG_WIDTH = 512
FFN_DIM = 2816
LN_EPS = 1e-5
RMS_EPS = 1e-6
MASK_VALUE = -1e9
ALPHA = (2 * DEPTH) ** 0.25
RG_C = 8.0
ADAM_LR, ADAM_B1, ADAM_B2, ADAM_EPS, ADAM_WD, ADAM_STEP = 0.001, 0.9, 0.999, 1e-08, 0.01, 10

LANES = 128
SUBLANES = 8
HALO = 16
CONV_CB = 256
VMEM_LIMIT = 56 * 1024 * 1024

AB_Z, AB_HQ, AB_HF, AB_HI, AB_HG, AB_XBC, AB_DT, AB_COLS = 0, 512, 1024, 1536, 2048, 2560, 3328, 3456
CD_Q, CD_GATE, CD_XR, CD_K, CD_V, CD_COLS = 0, 512, 1024, 1536, 1664, 1792


def _cp(sem=None):
    kw = dict(vmem_limit_bytes=VMEM_LIMIT)
    if sem is not None:
        kw["dimension_semantics"] = sem
    return pltpu.CompilerParams(**kw)


def _dot(a, b, dims=(((1,), (0,)), ((), ())), precision=None):
    return lax.dot_general(a, b, dims, precision=precision, preferred_element_type=F32)


NN = (((1,), (0,)), ((), ()))
NT = (((1,), (1,)), ((), ()))
TN = (((0,), (0,)), ((), ()))


@functools.partial(jax.custom_vjp, nondiff_argnums=(2,))
def _bdot(a, b, dims=NN):
    return _dot(a.astype(BF16), b.astype(BF16), dims)


def _bdot_fwd(a, b, dims):
    return _bdot(a, b, dims), (a, b)


def _bdot_bwd(dims, res, ct):
    a, b = res
    ab, bb, cb = a.astype(BF16), b.astype(BF16), ct.astype(BF16)
    if dims == NN:
        da, db = _dot(cb, bb, NT), _dot(ab, cb, TN)
    elif dims == NT:
        da, db = _dot(cb, bb, NN), _dot(cb, ab, TN)
    else:
        da, db = _dot(bb, cb, NT), _dot(ab, cb, NN)
    return da.astype(a.dtype), db.astype(b.dtype)


_bdot.defvjp(_bdot_fwd, _bdot_bwd)


WIDE_TILES = (1408, 1152, 1024, 896, 768, 512, 384, 256, 128)
MM_BLOCK_BYTES = 6 * 1024 * 1024


def _pick(n, cands):
    for c in cands:
        if n % c == 0:
            return c
    return n


def _mm(pairs, n_out, out_dtype, name, trans_b=False, tm=512):
    m = pairs[0][0].shape[0]
    a_row_bytes = sum(p[4] * p[0].dtype.itemsize for p in pairs)
    b_col_bytes = sum(p[4] * p[2].dtype.itemsize for p in pairs)
    tm = min(2 * tm if 2 * tm * a_row_bytes <= 2 * MM_BLOCK_BYTES else tm, m)
    tn = _pick(n_out, [c for c in WIDE_TILES if c * b_col_bytes <= MM_BLOCK_BYTES])
    n = len(pairs)

    def kern(*refs):
        acc = None
        for i in range(n):
            p = _bdot(refs[i][...], refs[n + i][...], NT if trans_b else NN)
            acc = p if acc is None else acc + p
        refs[2 * n][...] = acc.astype(out_dtype)

    in_specs = []
    for a, acb, b, bcb, k in pairs:
        in_specs.append(pl.BlockSpec((tm, k), functools.partial(lambda i, j, c: (i, c), c=acb)))
    for a, acb, b, bcb, k in pairs:
        if trans_b:
            in_specs.append(pl.BlockSpec((tn, k), functools.partial(lambda i, j, c: (j, c), c=bcb)))
        else:
            in_specs.append(pl.BlockSpec((k, tn), functools.partial(lambda i, j, c: (c, j), c=bcb)))
    return pl.pallas_call(
        kern, name=name, grid=(m // tm, n_out // tn), in_specs=in_specs,
        out_specs=pl.BlockSpec((tm, tn), lambda i, j: (i, j)),
        out_shape=jax.ShapeDtypeStruct((m, n_out), out_dtype),
        compiler_params=_cp(("parallel", "arbitrary")),
    )(*[p[0] for p in pairs], *[p[2] for p in pairs])


def _mm_tn(a, a_cb, ka, g, g_cb, ng, name, tm=2048):
    m = a.shape[0]
    tm = min(tm, m)
    tk = _pick(ka, (1024, 1408, 512, 256, 128))
    tn = _pick(ng, WIDE_TILES)
    nm = m // tm

    def kern(a_ref, g_ref, o_ref):
        p = _bdot(a_ref[...], g_ref[...], TN)

        @pl.when(pl.program_id(2) == 0)
        def _():
            o_ref[...] = p

        @pl.when(pl.program_id(2) > 0)
        def _():
            o_ref[...] += p

    ak, gk = ka // tk, ng // tn
    return pl.pallas_call(
        kern, name=name, grid=(ak, gk, nm),
        in_specs=[pl.BlockSpec((tm, tk), lambda i, j, r: (r, a_cb * ak + i)),
                  pl.BlockSpec((tm, tn), lambda i, j, r: (r, g_cb * gk + j))],
        out_specs=pl.BlockSpec((tk, tn), lambda i, j, r: (i, j)),
        out_shape=jax.ShapeDtypeStruct((ka, ng), F32),
        compiler_params=_cp(("parallel", "parallel", "arbitrary")),
    )(a, g)


def _mm_ln(pairs, x, g, b, name, tm=512):
    m, d = x.shape
    tm = min(tm, m)
    n = len(pairs)

    def kern(*refs):
        x_ref, g_ref, b_ref, y_ref, yb_ref, r_ref = refs[2 * n:]
        r = ALPHA * x_ref[...]
        for i in range(n):
            r = r + _bdot(refs[i][...], refs[n + i][...])
        mu = jnp.mean(r, -1, keepdims=True)
        xc = r - mu
        var = jnp.mean(xc * xc, -1, keepdims=True)
        y = xc * lax.rsqrt(var + LN_EPS) * g_ref[...] + b_ref[...]
        y_ref[...] = y
        yb_ref[...] = y.astype(BF16)
        r_ref[...] = r

    row = pl.BlockSpec((tm, d), lambda i: (i, 0))
    vec = pl.BlockSpec((1, d), lambda i: (0, 0))
    in_specs = [pl.BlockSpec((tm, k), functools.partial(lambda i, c: (i, c), c=acb)) for _, acb, _, _, k in pairs]
    in_specs += [pl.BlockSpec((k, d), functools.partial(lambda i, c: (c, 0), c=bcb)) for _, _, _, bcb, k in pairs]
    return pl.pallas_call(
        kern, name=name, grid=(m // tm,), in_specs=in_specs + [row, vec, vec], out_specs=[row, row, row],
        out_shape=[jax.ShapeDtypeStruct((m, d), F32), jax.ShapeDtypeStruct((m, d), BF16), jax.ShapeDtypeStruct((m, d), F32)],
        compiler_params=_cp(("parallel",)),
    )(*[p[0] for p in pairs], *[p[2] for p in pairs], x, g.reshape(1, d), b.reshape(1, d))


def _ln_bwd(dya, dyb, ca, cb, r, g, name, tm=256):
    s, d = r.shape
    tm = min(tm, s)

    def kern(a_ref, b_ref, r_ref, g_ref, dr_ref, dg_ref, db_ref):
        dy = ca * a_ref[...] + cb * b_ref[...]
        rr = r_ref[...]
        mu = jnp.mean(rr, -1, keepdims=True)
        xc = rr - mu
        var = jnp.mean(xc * xc, -1, keepdims=True)
        rstd = lax.rsqrt(var + LN_EPS)
        xhat = xc * rstd
        dxh = dy * g_ref[...]
        dr_ref[...] = rstd * (dxh - jnp.mean(dxh, -1, keepdims=True) - xhat * jnp.mean(dxh * xhat, -1, keepdims=True))
        dg = jnp.sum(dy * xhat, 0, keepdims=True)
        db = jnp.sum(dy, 0, keepdims=True)

        @pl.when(pl.program_id(0) == 0)
        def _():
            dg_ref[...] = dg
            db_ref[...] = db

        @pl.when(pl.program_id(0) > 0)
        def _():
            dg_ref[...] += dg
            db_ref[...] += db

    row = pl.BlockSpec((tm, d), lambda i: (i, 0))
    vec = pl.BlockSpec((1, d), lambda i: (0, 0))
    dr, dg, db = pl.pallas_call(
        kern, name=name, grid=(s // tm,), in_specs=[row, row, row, vec], out_specs=[row, vec, vec],
        out_shape=[jax.ShapeDtypeStruct((s, d), F32), jax.ShapeDtypeStruct((1, d), F32), jax.ShapeDtypeStruct((1, d), F32)],
        compiler_params=_cp(("arbitrary",)),
    )(dya, dyb, r, g.reshape(1, d))
    return dr, dg[0], db[0]


def _sqerr(y, t, name, tm=256):
    s, d = y.shape
    tm = min(tm, s)

    def kern(y_ref, t_ref, o_ref):
        e = y_ref[...] - t_ref[...]
        p = jnp.sum(e * e, 0, keepdims=True)

        @pl.when(pl.program_id(0) == 0)
        def _():
            o_ref[...] = p

        @pl.when(pl.program_id(0) > 0)
        def _():
            o_ref[...] += p

    row = pl.BlockSpec((tm, d), lambda i: (i, 0))
    return pl.pallas_call(
        kern, name=name, grid=(s // tm,), in_specs=[row, row], out_specs=pl.BlockSpec((1, d), lambda i: (0, 0)),
        out_shape=jax.ShapeDtypeStruct((1, d), F32), compiler_params=_cp(("arbitrary",)),
    )(y, t)


def _axpby(a, b, ca, cb, name, tm=256):
    s, d = a.shape
    tm = min(tm, s)

    def kern(a_ref, b_ref, o_ref):
        o_ref[...] = ca * a_ref[...] + cb * b_ref[...]

    row = pl.BlockSpec((tm, d), lambda i: (i, 0))
    return pl.pallas_call(
        kern, name=name, grid=(s // tm,), in_specs=[row, row], out_specs=row,
        out_shape=jax.ShapeDtypeStruct((s, d), F32), compiler_params=_cp(("parallel",)),
    )(a, b)


def _shift_down(xx, s, n):
    if s == 0:
        return xx[HALO:HALO + n]
    return pltpu.roll(xx, s, axis=0)[HALO:HALO + n]


def _shift_up(yy, s, n):
    if s == 0:
        return yy[0:n]
    return pltpu.roll(yy, yy.shape[0] - s, axis=0)[0:n]


def _prev_halo(tm):
    return lambda i: jnp.maximum(i * (tm // HALO) - 1, 0)


def _dwconv_fwd(x, x_cb0, c, w8, b, k, name, tm=1024):
    s = x.shape[0]
    tm = min(tm, s)
    ph = _prev_halo(tm)

    def kern(x_ref, h_ref, w_ref, b_ref, y_ref):
        halo = jnp.where(pl.program_id(0) == 0, 0.0, h_ref[...].astype(F32))
        xx = jnp.concatenate([halo, x_ref[...].astype(F32)], axis=0)
        w = w_ref[...]
        acc = b_ref[...] + w[k - 1:k] * xx[HALO:]
        for j in range(k - 1):
            acc = acc + w[j:j + 1] * _shift_down(xx, k - 1 - j, tm)
        y_ref[...] = acc

    return pl.pallas_call(
        kern, name=name, grid=(s // tm, c // CONV_CB),
        in_specs=[pl.BlockSpec((tm, CONV_CB), lambda i, j: (i, x_cb0 + j)),
                  pl.BlockSpec((HALO, CONV_CB), lambda i, j: (ph(i), x_cb0 + j)),
                  pl.BlockSpec((SUBLANES, CONV_CB), lambda i, j: (0, j)),
                  pl.BlockSpec((1, CONV_CB), lambda i, j: (0, j))],
        out_specs=pl.BlockSpec((tm, CONV_CB), lambda i, j: (i, j)),
        out_shape=jax.ShapeDtypeStruct((s, c), F32), compiler_params=_cp(("parallel", "parallel")),
    )(x, x, w8, b.reshape(1, c))


def _dwconv_bwd(dy, x, x_cb0, c, w8, k, name, tm=1024):
    s = x.shape[0]
    tm = min(tm, s)
    nt = s // tm
    ph = _prev_halo(tm)
    nh = lambda i: jnp.minimum((i + 1) * (tm // HALO), s // HALO - 1)

    def kern(dy_ref, dyn_ref, x_ref, h_ref, w_ref, dx_ref, dw_ref, db_ref):
        i = pl.program_id(1)
        halo = jnp.where(i == 0, 0.0, h_ref[...].astype(F32))
        xx = jnp.concatenate([halo, x_ref[...].astype(F32)], axis=0)
        dyt = dy_ref[...]
        nxt = jnp.where(i == nt - 1, 0.0, dyn_ref[...])
        dyy = jnp.concatenate([dyt, nxt], axis=0)
        w = w_ref[...]
        dx = w[k - 1:k] * dyt
        rows = [jnp.sum(dyt * _shift_down(xx, k - 1 - j, tm), 0, keepdims=True) for j in range(k)]
        for j in range(k - 1):
            dx = dx + w[j:j + 1] * _shift_up(dyy, k - 1 - j, tm)
        dx_ref[...] = dx.astype(BF16)
        dw = jnp.concatenate(rows + [jnp.zeros((SUBLANES - k, CONV_CB), F32)], axis=0)
        db = jnp.sum(dyt, 0, keepdims=True)

        @pl.when(i == 0)
        def _():
            dw_ref[...] = dw
            db_ref[...] = db

        @pl.when(i > 0)
        def _():
            dw_ref[...] += dw
            db_ref[...] += db

    dx, dw, db = pl.pallas_call(
        kern, name=name, grid=(c // CONV_CB, nt),
        in_specs=[pl.BlockSpec((tm, CONV_CB), lambda j, i: (i, j)),
                  pl.BlockSpec((HALO, CONV_CB), lambda j, i: (nh(i), j)),
                  pl.BlockSpec((tm, CONV_CB), lambda j, i: (i, x_cb0 + j)),
                  pl.BlockSpec((HALO, CONV_CB), lambda j, i: (ph(i), x_cb0 + j)),
                  pl.BlockSpec((SUBLANES, CONV_CB), lambda j, i: (0, j))],
        out_specs=[pl.BlockSpec((tm, CONV_CB), lambda j, i: (i, j)),
                   pl.BlockSpec((SUBLANES, CONV_CB), lambda j, i: (0, j)),
                   pl.BlockSpec((1, CONV_CB), lambda j, i: (0, j))],
        out_shape=[jax.ShapeDtypeStruct((s, c), BF16), jax.ShapeDtypeStruct((SUBLANES, c), F32),
                   jax.ShapeDtypeStruct((1, c), F32)],
        compiler_params=_cp(("parallel", "arbitrary")),
    )(dy, dy, x, x, w8)
    return dx, dw, db[0]


def _silu(x):
    return x * jax.nn.sigmoid(x)


def _ffn_gate_fwd(hu, w8, b, name, tm=2048):
    s = hu.shape[0]
    tm = min(tm, s)
    nb = FFN_DIM // CONV_CB
    ph = _prev_halo(tm)
    k = 3

    def conv(x_ref, h_ref, w_ref, b_ref):
        halo = jnp.where(pl.program_id(0) == 0, 0.0, h_ref[...].astype(F32))
        xx = jnp.concatenate([halo, x_ref[...].astype(F32)], axis=0)
        w = w_ref[...]
        acc = b_ref[...] + w[k - 1:k] * xx[HALO:]
        for j in range(k - 1):
            acc = acc + w[j:j + 1] * _shift_down(xx, k - 1 - j, tm)
        return acc

    def kern(g_ref, gh_ref, u_ref, uh_ref, wg_ref, wu_ref, bg_ref, bu_ref, a_ref, hg_ref, hu_ref):
        g = conv(g_ref, gh_ref, wg_ref, bg_ref)
        u = conv(u_ref, uh_ref, wu_ref, bu_ref)
        a_ref[...] = (_silu(g) * u).astype(BF16)
        hg_ref[...] = g.astype(BF16)
        hu_ref[...] = u.astype(BF16)

    main = lambda off: pl.BlockSpec((tm, CONV_CB), lambda i, j: (i, off + j))
    halo = lambda off: pl.BlockSpec((HALO, CONV_CB), lambda i, j: (ph(i), off + j))
    wsp = lambda off: pl.BlockSpec((SUBLANES, CONV_CB), lambda i, j: (0, off + j))
    bsp = lambda off: pl.BlockSpec((1, CONV_CB), lambda i, j: (0, off + j))
    b2 = b.reshape(1, 2 * FFN_DIM)
    half = jax.ShapeDtypeStruct((s, FFN_DIM), BF16)
    return pl.pallas_call(
        kern, name=name, grid=(s // tm, nb),
        in_specs=[main(0), halo(0), main(nb), halo(nb), wsp(0), wsp(nb), bsp(0), bsp(nb)],
        out_specs=[main(0), main(0), main(0)], out_shape=[half, half, half],
        compiler_params=_cp(("parallel", "parallel")),
    )(hu, hu, hu, hu, w8, w8, b2, b2)


def _ffn_gate_bwd(hu, hg, hv, da, w8, name, tm=1024):
    s = hu.shape[0]
    tm = min(tm, s)
    nt = s // tm
    nb = FFN_DIM // CONV_CB
    nh = lambda i: jnp.minimum((i + 1) * (tm // HALO), s // HALO - 1)
    k = 3

    def kern(xg_ref, xu_ref, g_ref, gn_ref, u_ref, un_ref, da_ref, dan_ref, wg_ref, wu_ref,
             dg_ref, du_ref, dwg_ref, dwu_ref, dbg_ref, dbu_ref):
        i = pl.program_id(1)

        def ext(x_ref, n_ref):
            return jnp.concatenate([x_ref[...].astype(F32), n_ref[...].astype(F32)], axis=0)

        g, u = ext(g_ref, gn_ref), ext(u_ref, un_ref)
        dae = jnp.concatenate([da_ref[...].astype(F32), jnp.where(i == nt - 1, 0.0, dan_ref[...].astype(F32))], axis=0)
        sg = jax.nn.sigmoid(g)
        dhg = dae * u * (sg * (1.0 + g * (1.0 - sg)))
        dhu = dae * (g * sg)

        def back(dh, x, w):
            taps = [dh[0:tm] if j == k - 1 else _shift_up(dh, k - 1 - j, tm) for j in range(k)]
            dx = w[0:1] * taps[0]
            for j in range(1, k):
                dx = dx + w[j:j + 1] * taps[j]
            rows = [jnp.sum(taps[j] * x, 0, keepdims=True) for j in range(k)]
            dw = jnp.concatenate(rows + [jnp.zeros((SUBLANES - k, CONV_CB), F32)], axis=0)
            return dx, dw, jnp.sum(dh[0:tm], 0, keepdims=True)

        dxg, dwg, dbg = back(dhg, xg_ref[...].astype(F32), wg_ref[...])
        dxu, dwu, dbu = back(dhu, xu_ref[...].astype(F32), wu_ref[...])
        dg_ref[...] = dxg.astype(BF16)
        du_ref[...] = dxu.astype(BF16)

        @pl.when(i == 0)
        def _():
            dwg_ref[...] = dwg
            dwu_ref[...] = dwu
            dbg_ref[...] = dbg
            dbu_ref[...] = dbu

        @pl.when(i > 0)
        def _():
            dwg_ref[...] += dwg
            dwu_ref[...] += dwu
            dbg_ref[...] += dbg
            dbu_ref[...] += dbu

    main = lambda off: pl.BlockSpec((tm, CONV_CB), lambda j, i: (i, off + j))
    nxt = lambda off: pl.BlockSpec((HALO, CONV_CB), lambda j, i: (nh(i), off + j))
    wsp = lambda off: pl.BlockSpec((SUBLANES, CONV_CB), lambda j, i: (0, off + j))
    bsp = lambda off: pl.BlockSpec((1, CONV_CB), lambda j, i: (0, off + j))
    outs = pl.pallas_call(
        kern, name=name, grid=(nb, nt),
        in_specs=[main(0), main(nb), main(0), nxt(0), main(0), nxt(0), main(0), nxt(0), wsp(0), wsp(nb)],
        out_specs=[main(0), main(0), wsp(0), wsp(0), bsp(0), bsp(0)],
        out_shape=[jax.ShapeDtypeStruct((s, FFN_DIM), BF16), jax.ShapeDtypeStruct((s, FFN_DIM), BF16),
                   jax.ShapeDtypeStruct((SUBLANES, FFN_DIM), F32), jax.ShapeDtypeStruct((SUBLANES, FFN_DIM), F32),
                   jax.ShapeDtypeStruct((1, FFN_DIM), F32), jax.ShapeDtypeStruct((1, FFN_DIM), F32)],
        compiler_params=_cp(("parallel", "arbitrary")),
    )(hu, hu, hg, hg, hv, hv, da, da, w8, w8)
    dg, du, dwg, dwu, dbg, dbu = outs
    return dg, du, jnp.concatenate([dwg, dwu], axis=1), jnp.concatenate([dbg[0], dbu[0]])


def _iota(shape, dim):
    return lax.broadcasted_iota(jnp.int32, shape, dim)


def _in_range(idx, start, size):
    return jnp.logical_and(idx >= start, idx < start + size)


@functools.partial(jax.custom_vjp, nondiff_argnums=(2, 3))
def _zdot(a, b, dims, z):
    zz, x = ((a, b) if z == 0 else (b, a))
    zz = zz.astype(BF16)
    xh = x.astype(BF16)
    r1 = x - xh.astype(F32)
    xm = r1.astype(BF16)
    xl = (r1 - xm.astype(F32)).astype(BF16)
    f = (lambda p: _dot(zz, p, dims)) if z == 0 else (lambda p: _dot(p, zz, dims))
    return f(xh) + f(xm) + f(xl)


def _zdot_fwd(a, b, dims, z):
    return _zdot(a, b, dims, z), (a if z == 0 else b)


def _zdot_bwd(dims, z, zz, ct):
    zb = zz.astype(BF16)
    ch = ct.astype(BF16)
    cm = (ct - ch.astype(F32)).astype(BF16)
    if dims == NN and z == 0:
        f = lambda c: _dot(zb, c, TN)
    elif dims == NN and z == 1:
        f = lambda c: _dot(c, zb, NT)
    else:
        f = lambda c: _dot(c, zb, TN)
    dx = f(ch) + f(cm)
    return (jnp.zeros_like(zz), dx) if z == 0 else (dx, jnp.zeros_like(zz))


_zdot.defvjp(_zdot_fwd, _zdot_bwd)


def _expand8(v, e):
    return jnp.sum(_zdot(jnp.broadcast_to(v, (SUBLANES, v.shape[1])), e, NN, 1), 0, keepdims=True) * (1.0 / SUBLANES)


def _ssd_consts():
    tri = np.tril(np.ones((SSD_CHUNK, SSD_CHUNK), np.float32))
    e = np.zeros((LANES, SSD_D_INNER), np.float32)
    for h in range(SSD_HEADS):
        e[h, 64 * h:64 * h + 64] = 1.0
    bd = np.zeros((LANES, SSD_D_INNER), np.float32)
    bd[:64, :256] = 1.0
    bd[64:, 256:] = 1.0
    return jnp.asarray(tri), jnp.asarray(e), jnp.asarray(bd)


def _ssd_chunk(tri, e, bd, z, xs_p, bm_p, cm_p, dtr, st, dtb, alog, dsk, nw):
    n = z.shape[0]
    lane128, lane512 = _iota((1, LANES), 1), _iota((1, SSD_D_INNER), 1)
    sub8 = _iota((SUBLANES, 1), 0)
    tril = _iota((n, 1), 0) >= _iota((1, n), 1)
    head_rows = jnp.where(sub8 == lane128, 1.0, 0.0)
    xs, bm, cm = _silu(xs_p), _silu(bm_p), _silu(cm_p)
    dt = jax.nn.softplus(dtr + dtb)
    da = dt * (-jnp.exp(alog))
    acs = _zdot(tri, da, NN, 0)
    acst = _zdot(head_rows, acs, NT, 0)
    xc = xs * _zdot(dt, e, NN, 1)
    tot = jnp.sum(da, 0, keepdims=True)
    y = _bdot(cm, st) * _zdot(jnp.exp(acs), e, NN, 1)
    st_new = st * _expand8(jnp.exp(tot), e) + bd * _bdot(bm, xc * _zdot(jnp.exp(tot - acs), e, NN, 1), TN)
    for g in range(2):
        cb = _bdot(jnp.where(_in_range(lane128, 64 * g, 64), cm, 0.0), bm, NT)
        for h in range(4 * g, 4 * g + 4):
            col = jnp.sum(jnp.where(lane128 == h, acs, 0.0), 1, keepdims=True)
            row = jnp.sum(jnp.where(sub8 == h, acst, 0.0), 0, keepdims=True)
            dm = jnp.where(tril, jnp.exp(jnp.minimum(col - row, 0.0)), 0.0)
            y = y + _bdot(cb * dm, jnp.where(_in_range(lane512, 64 * h, 64), xc, 0.0))
    y = (y + _expand8(dsk, e) * xs) * _silu(z)
    ysq = y * y
    g0 = lane512 < 256
    ms0 = jnp.sum(jnp.where(g0, ysq, 0.0), 1, keepdims=True) * (1.0 / 256)
    ms1 = jnp.sum(jnp.where(g0, 0.0, ysq), 1, keepdims=True) * (1.0 / 256)
    rs = jnp.where(g0, lax.rsqrt(ms0 + RMS_EPS), lax.rsqrt(ms1 + RMS_EPS))
    return y * rs * nw, st_new


SSD_STEP = 2


def _ssd_rows(u):
    return slice(SSD_CHUNK * u, SSD_CHUNK * (u + 1))


def _ssd_specs(rev, ng):
    ci = (lambda i: ng - 1 - i) if rev else (lambda i: i)
    n = SSD_STEP * SSD_CHUNK
    act = [pl.BlockSpec((n, 512), lambda i: (ci(i), AB_Z // 512)),
           pl.BlockSpec((n, 512), lambda i: (ci(i), 0)),
           pl.BlockSpec((n, LANES), lambda i: (ci(i), 4)),
           pl.BlockSpec((n, LANES), lambda i: (ci(i), 5)),
           pl.BlockSpec((n, LANES), lambda i: (ci(i), AB_DT // LANES))]
    const = [pl.BlockSpec((SSD_CHUNK, SSD_CHUNK), lambda i: (0, 0)), pl.BlockSpec((LANES, 512), lambda i: (0, 0)),
             pl.BlockSpec((LANES, 512), lambda i: (0, 0))]
    par = [pl.BlockSpec((1, LANES), lambda i: (0, 0))] * 3 + [pl.BlockSpec((1, 512), lambda i: (0, 0))]
    return ci, act, const, par


def _ssd_fwd(h, xbcc, dtb, alog, dsk, nw, name):
    s = h.shape[0]
    nc = s // SSD_CHUNK
    ng = nc // SSD_STEP
    ci, act, const, par = _ssd_specs(False, ng)

    def kern(z_ref, xs_ref, bm_ref, cm_ref, dt_ref, tri_ref, e_ref, bd_ref, dtb_ref, al_ref, dsk_ref, nw_ref,
             y_ref, sts_ref, st):
        @pl.when(pl.program_id(0) == 0)
        def _():
            st[...] = jnp.zeros_like(st)

        for u in range(SSD_STEP):
            rows = _ssd_rows(u)
            sts_ref[u] = st[...]
            y, stn = _ssd_chunk(tri_ref[...], e_ref[...], bd_ref[...], z_ref[rows, :], xs_ref[rows, :], bm_ref[rows, :],
                                cm_ref[rows, :], dt_ref[rows, :], st[...], dtb_ref[...], al_ref[...], dsk_ref[...], nw_ref[...])
            y_ref[rows, :] = y.astype(BF16)
            st[...] = stn

    return pl.pallas_call(
        kern, name=name, grid=(ng,), in_specs=act + const + par,
        out_specs=[pl.BlockSpec((SSD_STEP * SSD_CHUNK, 512), lambda i: (i, 0)),
                   pl.BlockSpec((SSD_STEP, LANES, 512), lambda i: (i, 0, 0))],
        out_shape=[jax.ShapeDtypeStruct((s, 512), BF16), jax.ShapeDtypeStruct((nc, LANES, 512), F32)],
        scratch_shapes=[pltpu.VMEM((LANES, 512), F32)], compiler_params=_cp(("arbitrary",)),
    )(h, xbcc, xbcc, xbcc, h, *_ssd_consts(), dtb, alog, dsk, nw)


def _acc(ref, val, first):
    @pl.when(first)
    def _():
        ref[...] = val

    @pl.when(jnp.logical_not(first))
    def _():
        ref[...] += val


def _ssd_bwd(dy, dy_cb, h, xbcc, sts, dtb, alog, dsk, nw, name):
    s = h.shape[0]
    ng = s // SSD_CHUNK // SSD_STEP
    n = SSD_STEP * SSD_CHUNK
    ci, act, const, par = _ssd_specs(True, ng)

    def kern(dy_ref, sts_ref, z_ref, xs_ref, bm_ref, cm_ref, dt_ref, tri_ref, e_ref, bd_ref, dtb_ref, al_ref, dsk_ref,
             nw_ref, dz_ref, dx_ref, ddt_ref, ddtb_ref, dal_ref, ddsk_ref, dnw_ref, dst):
        @pl.when(pl.program_id(0) == 0)
        def _():
            dst[...] = jnp.zeros_like(dst)
            for ref in (ddtb_ref, dal_ref, ddsk_ref, dnw_ref):
                ref[...] = jnp.zeros_like(ref)

        f = functools.partial(_ssd_chunk, tri_ref[...], e_ref[...], bd_ref[...])
        for u in range(SSD_STEP - 1, -1, -1):
            rows = _ssd_rows(u)
            _, pull = jax.vjp(f, z_ref[rows, :], xs_ref[rows, :], bm_ref[rows, :], cm_ref[rows, :], dt_ref[rows, :],
                              sts_ref[u], dtb_ref[...], al_ref[...], dsk_ref[...], nw_ref[...])
            dz, dxs, dbm, dcm, ddt, dsti, ddtb, dal, ddsk, dnw = pull((dy_ref[rows, :], dst[...]))
            dz_ref[rows, :] = dz.astype(BF16)
            dx_ref[rows, 0:512] = dxs
            dx_ref[rows, 512:640] = dbm
            dx_ref[rows, 640:768] = dcm
            ddt_ref[rows, :] = ddt.astype(BF16)
            dst[...] = dsti
            ddtb_ref[...] += ddtb
            dal_ref[...] += dal
            ddsk_ref[...] += ddsk
            dnw_ref[...] += dnw

    vec = pl.BlockSpec((1, LANES), lambda i: (0, 0))
    return pl.pallas_call(
        kern, name=name, grid=(ng,),
        in_specs=[pl.BlockSpec((n, 512), lambda i: (ci(i), dy_cb)), pl.BlockSpec((SSD_STEP, LANES, 512), lambda i: (ci(i), 0, 0))]
        + act + const + par,
        out_specs=[pl.BlockSpec((n, 512), lambda i: (ci(i), 0)), pl.BlockSpec((n, SSD_CONV_DIM), lambda i: (ci(i), 0)),
                   pl.BlockSpec((n, LANES), lambda i: (ci(i), 0)), vec, vec, vec, pl.BlockSpec((1, 512), lambda i: (0, 0))],
        out_shape=[jax.ShapeDtypeStruct((s, 512), BF16), jax.ShapeDtypeStruct((s, SSD_CONV_DIM), F32),
                   jax.ShapeDtypeStruct((s, LANES), BF16), jax.ShapeDtypeStruct((1, LANES), F32),
                   jax.ShapeDtypeStruct((1, LANES), F32), jax.ShapeDtypeStruct((1, LANES), F32),
                   jax.ShapeDtypeStruct((1, 512), F32)],
        scratch_shapes=[pltpu.VMEM((LANES, 512), F32)], compiler_params=_cp(("arbitrary",)),
    )(dy, sts, h, xbcc, xbcc, xbcc, h, *_ssd_consts(), dtb, alog, dsk, nw)


HG_LEVELS = 6


HG_SUMS = 1 + 2 * HG_LEVELS


def _hg_consts():
    n = HG_CHUNK
    t = np.arange(n)
    mats = [np.tril(np.ones((n, n), np.float32))]
    dq, dk, mk = [], [], []
    for lv in range(HG_LEVELS):
        b = 1 << lv
        start = (t // b) * b
        end = start + b - 1
        dq.append(((t[None, :] >= start[:, None]) & (t[None, :] <= t[:, None])).astype(np.float32))
        dk.append(((t[None, :] > t[:, None]) & (t[None, :] <= end[:, None])).astype(np.float32))
        mk.append((((t // b) % 2 == 1)[:, None] & ((t // b)[None, :] == (t // b)[:, None] - 1)).astype(np.float32))
    c1 = np.concatenate(mats + dq + dk, axis=0)
    return (jnp.asarray(c1, BF16), jnp.asarray(np.concatenate([c1, c1, c1], axis=1), BF16), jnp.asarray(np.stack(mk)))


@functools.partial(jax.custom_vjp, nondiff_argnums=(1, 2))
def _split_grid(x, nr, nc):
    br, bc = x.shape[0] // nr, x.shape[1] // nc
    return tuple(x[i * br:(i + 1) * br, j * bc:(j + 1) * bc] for i in range(nr) for j in range(nc))


def _split_grid_fwd(x, nr, nc):
    return _split_grid(x, nr, nc), None


def _split_grid_bwd(nr, nc, _, cts):
    rows = [cts[i * nc] if nc == 1 else jnp.concatenate(cts[i * nc:(i + 1) * nc], axis=1) for i in range(nr)]
    return (rows[0] if nr == 1 else jnp.concatenate(rows, axis=0),)


_split_grid.defvjp(_split_grid_fwd, _split_grid_bwd)


@jax.custom_vjp
def _level_sums(c1, c3, g):
    gh = g.astype(BF16)
    r1 = g - gh.astype(F32)
    gm = r1.astype(BF16)
    gl = (r1 - gm.astype(F32)).astype(BF16)
    return _dot(c3, jnp.concatenate([gh, gm, gl], axis=0), NN)


def _level_sums_fwd(c1, c3, g):
    return _level_sums(c1, c3, g), (c1, c3)


def _level_sums_bwd(res, ct):
    c1, c3 = res
    ch = ct.astype(BF16)
    cm = (ct - ch.astype(F32)).astype(BF16)
    return jnp.zeros_like(c1), jnp.zeros_like(c3), _dot(c1, ch, TN) + _dot(c1, cm, TN)


_level_sums.defvjp(_level_sums_fwd, _level_sums_bwd)


def _hg_chunk(c1, c3, masks, hq, hf, hi, hg, sts, lb, nw):
    q = _silu(hq)
    g = jnp.log(lb + (1.0 - lb) * jax.nn.sigmoid(hf))
    k = (1.0 - lb) * jax.nn.sigmoid(-hf)
    parts = _split_grid(_level_sums(c1, c3, g), HG_SUMS, 1)
    bc = parts[0]
    tot = jnp.sum(g, 0, keepdims=True)
    qe = _split_grid(q * jnp.exp(bc), 1, HG_HEADS)
    kd = _split_grid(k * jnp.exp(tot - bc), 1, HG_HEADS)
    etot = _split_grid(jnp.exp(tot), 1, HG_HEADS)
    qh, kh, vh, gh = [_split_grid(a, 1, HG_HEADS) for a in (q, k, hi, hg)]
    att = [None] * HG_HEADS
    for lv in range(HG_LEVELS):
        qt = _split_grid(q * jnp.exp(parts[1 + lv]), 1, HG_HEADS)
        kt = _split_grid(k * jnp.exp(parts[1 + HG_LEVELS + lv]), 1, HG_HEADS)
        for hd in range(HG_HEADS):
            t = masks[lv] * _bdot(qt[hd], kt[hd], NT)
            att[hd] = t if att[hd] is None else att[hd] + t
    outs, stn = [], []
    for hd in range(HG_HEADS):
        o = (_bdot(att[hd], vh[hd]) + jnp.sum(qh[hd] * kh[hd], 1, keepdims=True) * vh[hd] + _bdot(qe[hd], sts[hd], NT))
        stn.append(sts[hd] * etot[hd] + _bdot(vh[hd], kd[hd], TN))
        outs.append(o * lax.rsqrt(jnp.mean(o * o, 1, keepdims=True) + RMS_EPS) * nw * _silu(gh[hd]))
    return tuple(outs), tuple(stn)


HG_STEP = 4


def _hg_specs(rev, ng):
    ci = (lambda i: ng - 1 - i) if rev else (lambda i: i)
    n = HG_CHUNK
    act = [pl.BlockSpec((HG_STEP * n, 512), functools.partial(lambda i, c: (ci(i), c), c=c))
           for c in (AB_HQ // 512, AB_HF // 512, AB_HI // 512, AB_HG // 512)]
    other = [pl.BlockSpec((HG_SUMS * n, n), lambda i: (0, 0)), pl.BlockSpec((HG_SUMS * n, 3 * n), lambda i: (0, 0)),
             pl.BlockSpec((HG_LEVELS, n, n), lambda i: (0, 0, 0)), pl.BlockSpec((1, 512), lambda i: (0, 0)),
             pl.BlockSpec((1, LANES), lambda i: (0, 0))]
    return ci, act, other


def _head_rows(hd):
    return slice(LANES * hd, LANES * (hd + 1))


def _chunk_rows(u):
    return slice(HG_CHUNK * u, HG_CHUNK * (u + 1))


def _hg_fwd(h, lb, nw, name):
    s = h.shape[0]
    nc = s // HG_CHUNK
    ng = nc // HG_STEP
    ci, act, other = _hg_specs(False, ng)

    def kern(q_ref, f_ref, i_ref, g_ref, c1_ref, c3_ref, mk_ref, lb_ref, nw_ref, y_ref, sts_ref, st):
        @pl.when(pl.program_id(0) == 0)
        def _():
            st[...] = jnp.zeros_like(st)

        masks = [mk_ref[lv] for lv in range(HG_LEVELS)]
        for u in range(HG_STEP):
            rows = _chunk_rows(u)
            sts_ref[u] = st[...]
            outs, stn = _hg_chunk(c1_ref[...], c3_ref[...], masks, q_ref[rows, :], f_ref[rows, :], i_ref[rows, :],
                                  g_ref[rows, :], tuple(st[_head_rows(hd), :] for hd in range(HG_HEADS)), lb_ref[...],
                                  nw_ref[...])
            for hd in range(HG_HEADS):
                y_ref[rows, _head_rows(hd)] = outs[hd].astype(BF16)
                st[_head_rows(hd), :] = stn[hd]

    return pl.pallas_call(
        kern, name=name, grid=(ng,), in_specs=act + other,
        out_specs=[pl.BlockSpec((HG_STEP * HG_CHUNK, 512), lambda i: (i, 0)),
                   pl.BlockSpec((HG_STEP, 512, LANES), lambda i: (i, 0, 0))],
        out_shape=[jax.ShapeDtypeStruct((s, 512), BF16), jax.ShapeDtypeStruct((nc, 512, LANES), F32)],
        scratch_shapes=[pltpu.VMEM((512, LANES), F32)], compiler_params=_cp(("arbitrary",)),
    )(h, h, h, h, *_hg_consts(), lb, nw)


def _hg_bwd(dy, dy_cb, h, sts, lb, nw, name):
    s = h.shape[0]
    ng = s // HG_CHUNK // HG_STEP
    n = HG_STEP * HG_CHUNK
    ci, act, other = _hg_specs(True, ng)

    def kern(dy_ref, sts_ref, q_ref, f_ref, i_ref, g_ref, c1_ref, c3_ref, mk_ref, lb_ref, nw_ref,
             dq_ref, df_ref, di_ref, dg_ref, dlb_ref, dnw_ref, dst):
        @pl.when(pl.program_id(0) == 0)
        def _():
            dst[...] = jnp.zeros_like(dst)
            dlb_ref[...] = jnp.zeros_like(dlb_ref)
            dnw_ref[...] = jnp.zeros_like(dnw_ref)

        masks = [mk_ref[lv] for lv in range(HG_LEVELS)]
        heads = range(HG_HEADS)
        for u in range(HG_STEP - 1, -1, -1):
            rows = _chunk_rows(u)
            _, pull = jax.vjp(functools.partial(_hg_chunk, c1_ref[...], c3_ref[...], masks), q_ref[rows, :], f_ref[rows, :],
                              i_ref[rows, :], g_ref[rows, :], tuple(sts_ref[u, _head_rows(hd), :] for hd in heads),
                              lb_ref[...], nw_ref[...])
            dq, df, di, dg, dsti, dlb, dnw = pull((tuple(dy_ref[rows, _head_rows(hd)] for hd in heads),
                                                   tuple(dst[_head_rows(hd), :] for hd in heads)))
            dq_ref[rows, :] = dq.astype(BF16)
            df_ref[rows, :] = df.astype(BF16)
            di_ref[rows, :] = di.astype(BF16)
            dg_ref[rows, :] = dg.astype(BF16)
            for hd in heads:
                dst[_head_rows(hd), :] = dsti[hd]
            dlb_ref[...] += dlb
            dnw_ref[...] += dnw

    blk = pl.BlockSpec((n, 512), lambda i: (ci(i), 0))
    return pl.pallas_call(
        kern, name=name, grid=(ng,),
        in_specs=[pl.BlockSpec((n, 512), lambda i: (ci(i), dy_cb)), pl.BlockSpec((HG_STEP, 512, LANES), lambda i: (ci(i), 0, 0))]
        + act + other,
        out_specs=[blk, blk, blk, blk, pl.BlockSpec((1, 512), lambda i: (0, 0)), pl.BlockSpec((1, LANES), lambda i: (0, 0))],
        out_shape=[jax.ShapeDtypeStruct((s, 512), BF16)] * 4 + [jax.ShapeDtypeStruct((1, 512), F32),
                                                                jax.ShapeDtypeStruct((1, LANES), F32)],
        scratch_shapes=[pltpu.VMEM((512, LANES), F32)], compiler_params=_cp(("arbitrary",)),
    )(dy, sts, h, h, h, h, *_hg_consts(), lb, nw)


def _swa_consts():
    n = SWA_BLOCK
    r = np.arange(8 * n)
    t, blk = r % n, r // n
    lanes = np.arange(LANES)
    own = (t[:, None] >= lanes[None, :]).astype(np.float32)
    prev = (lanes[None, :] > t[:, None]).astype(np.float32)
    sel = (lanes[None, :] == (blk // 2 + 4 * (blk % 2))[:, None]).astype(np.float32)
    grp = ((lanes[None, :] // 64) == (blk % 2)[:, None]).astype(np.float32)
    return tuple(jnp.asarray(a) for a in (own, prev, sel, grp))


def _swa_block(own, prev, sel, grp, q, k0, v0, k1, v1, srow):
    qs = _split_grid(q, 1, 4)
    qa = jnp.concatenate([qs[p] for p in range(4) for _ in range(2)], axis=0) * grp
    s0 = jnp.where(prev > 0.0, _bdot(qa, k0, NT) * 0.125, MASK_VALUE)
    s1 = jnp.where(own > 0.0, _bdot(qa, k1, NT) * 0.125, MASK_VALUE)
    sk = jnp.sum(sel * srow, 1, keepdims=True)
    m = lax.stop_gradient(jnp.maximum(jnp.maximum(jnp.max(s0, 1, keepdims=True), jnp.max(s1, 1, keepdims=True)), sk))
    p0, p1 = jnp.exp(s0 - m), jnp.exp(s1 - m)
    inv = 1.0 / (jnp.sum(p0, 1, keepdims=True) + jnp.sum(p1, 1, keepdims=True) + jnp.exp(sk - m))
    o = _split_grid((_bdot(p0 * inv, v0) + _bdot(p1 * inv, v1)) * grp, 8, 1)
    return tuple(o[2 * p] + o[2 * p + 1] for p in range(4))


SWA_STEP = 2


def _swa_specs():
    n = SWA_BLOCK
    pv = lambda i: jnp.maximum(SWA_STEP * i - 1, 0)
    mask = pl.BlockSpec((8 * n, LANES), lambda i: (0, 0))
    return [mask, mask, mask, mask, pl.BlockSpec((SWA_STEP * n, 512), lambda i: (i, CD_Q // 512)),
            pl.BlockSpec((n, LANES), lambda i: (pv(i), CD_K // LANES)), pl.BlockSpec((n, LANES), lambda i: (pv(i), CD_V // LANES)),
            pl.BlockSpec((SWA_STEP * n, LANES), lambda i: (i, CD_K // LANES)),
            pl.BlockSpec((SWA_STEP * n, LANES), lambda i: (i, CD_V // LANES)), pl.BlockSpec((1, LANES), lambda i: (0, 0))]


def _swa_args(refs, u, notfirst):
    own, prev, sel, grp = (r[...] for r in refs[:4])
    q_ref, k0_ref, v0_ref, k_ref, v_ref, s_ref = refs[4:]
    rows = slice(SWA_BLOCK * u, SWA_BLOCK * (u + 1))
    if u == 0:
        k0, v0, prev = k0_ref[...], v0_ref[...], prev * notfirst.astype(F32)
    else:
        before = slice(SWA_BLOCK * (u - 1), SWA_BLOCK * u)
        k0, v0 = k_ref[before, :], v_ref[before, :]
    return (own, prev, sel, grp), (q_ref[rows, :], k0, v0, k_ref[rows, :], v_ref[rows, :], s_ref[...])


def _swa_fwd(hc, srow, name):
    s = hc.shape[0]
    n = SWA_STEP * SWA_BLOCK

    def kern(*refs):
        y_ref = refs[10]
        for u in range(SWA_STEP):
            consts, acts = _swa_args(refs[:10], u, pl.program_id(0) > 0)
            outs = _swa_block(*consts, *acts)
            for p in range(4):
                y_ref[SWA_BLOCK * u:SWA_BLOCK * (u + 1), LANES * p:LANES * (p + 1)] = outs[p].astype(BF16)

    return pl.pallas_call(
        kern, name=name, grid=(s // n,), in_specs=_swa_specs(), out_specs=pl.BlockSpec((n, 512), lambda i: (i, 0)),
        out_shape=jax.ShapeDtypeStruct((s, 512), BF16), compiler_params=_cp(("parallel",)),
    )(*_swa_consts(), hc, hc, hc, hc, hc, srow)


def _swa_bwd(dy, dy_cb, hc, srow, name):
    s = hc.shape[0]
    n = SWA_STEP * SWA_BLOCK
    nb = SWA_BLOCK

    def kern(dy_ref, *refs):
        dq_ref, dk_ref, dv_ref, ds_ref = refs[10:]
        i = pl.program_id(0)

        @pl.when(i == 0)
        def _():
            dk_ref[...] = jnp.zeros_like(dk_ref)
            dv_ref[...] = jnp.zeros_like(dv_ref)
            ds_ref[...] = jnp.zeros_like(ds_ref)

        for u in range(SWA_STEP):
            blk = SWA_STEP * i + u
            rows = slice(nb * u, nb * (u + 1))
            r1 = pl.ds(pl.multiple_of(blk * nb, nb), nb)
            r0 = pl.ds(pl.multiple_of(jnp.maximum(blk - 1, 0) * nb, nb), nb)
            consts, acts = _swa_args(refs[:10], u, i > 0)
            _, pull = jax.vjp(functools.partial(_swa_block, *consts), *acts)
            dq, dk0, dv0, dk1, dv1, dsr = pull(tuple(dy_ref[rows, LANES * p:LANES * (p + 1)] for p in range(4)))
            dq_ref[rows, :] = dq.astype(BF16)
            dk_ref[r0, :] += dk0
            dv_ref[r0, :] += dv0
            dk_ref[r1, :] += dk1
            dv_ref[r1, :] += dv1
            ds_ref[...] += dsr

    full = pl.BlockSpec((s, LANES), lambda i: (0, 0))
    return pl.pallas_call(
        kern, name=name, grid=(s // n,), in_specs=[pl.BlockSpec((n, 512), lambda i: (i, dy_cb))] + _swa_specs(),
        out_specs=[pl.BlockSpec((n, 512), lambda i: (i, 0)), full, full, pl.BlockSpec((1, LANES), lambda i: (0, 0))],
        out_shape=[jax.ShapeDtypeStruct((s, 512), BF16), jax.ShapeDtypeStruct((s, LANES), F32),
                   jax.ShapeDtypeStruct((s, LANES), F32), jax.ShapeDtypeStruct((1, LANES), F32)],
        compiler_params=_cp(("arbitrary",)),
    )(dy, *_swa_consts(), hc, hc, hc, hc, hc, srow)


RG_TILE = 512


def _neg_expm1(x):
    ser = x * (1 + x / 2 * (1 + x / 3 * (1 + x / 4 * (1 + x / 5 * (1 + x / 6 * (1 + x / 7 * (1 + x / 8)))))))
    return jnp.where(x > -0.25, -ser, 1.0 - jnp.exp(x))


def _rg_gates(xc, wa, wx, ba, bx, lam):
    r = jax.nn.sigmoid(_bdot(xc, wa) + ba)
    i = jax.nn.sigmoid(_bdot(xc, wx) + bx)
    log_a = -RG_C * r * jax.nn.softplus(-lam)
    u = jnp.sqrt(jnp.maximum(_neg_expm1(2.0 * log_a), 0.0)) * (i * xc)
    return jnp.exp(log_a), u


def _rg_out(hs, gate):
    return hs * jax.nn.gelu(gate)


def _rows_to_tile(rows):
    sub = _iota((SUBLANES, 1), 0)
    out = jnp.broadcast_to(rows[0], (SUBLANES, rows[0].shape[1]))
    for j in range(1, SUBLANES):
        out = jnp.where(sub == j, rows[j], out)
    return out


def _rg_fwd(xc, hc, wa, wx, ba, bx, lam, name):
    s = xc.shape[0]
    tm = min(RG_TILE, s)

    def kern(x_ref, g_ref, wa_ref, wx_ref, ba_ref, bx_ref, lam_ref, y_ref, h_ref, a_s, u_s, carry):
        @pl.when(pl.program_id(0) == 0)
        def _():
            carry[...] = jnp.zeros_like(carry)

        a, u = _rg_gates(x_ref[...], wa_ref[...], wx_ref[...], ba_ref[...], bx_ref[...], lam_ref[...])
        a_s[...] = a
        u_s[...] = u

        def body(gi, hp):
            r0 = pl.multiple_of(gi * SUBLANES, SUBLANES)
            ab, ub = a_s[pl.ds(r0, SUBLANES), :], u_s[pl.ds(r0, SUBLANES), :]
            rows = []
            for j in range(SUBLANES):
                hp = ab[j:j + 1] * hp + ub[j:j + 1]
                rows.append(hp)
            h_ref[pl.ds(r0, SUBLANES), :] = _rows_to_tile(rows)
            return hp

        carry[0:1, :] = lax.fori_loop(0, tm // SUBLANES, body, carry[0:1, :])
        y_ref[...] = _rg_out(h_ref[...], g_ref[...]).astype(BF16)

    row = pl.BlockSpec((tm, 512), lambda i: (i, 0))
    mat = pl.BlockSpec((512, 512), lambda i: (0, 0))
    vec = pl.BlockSpec((1, 512), lambda i: (0, 0))
    return pl.pallas_call(
        kern, name=name, grid=(s // tm,),
        in_specs=[row, pl.BlockSpec((tm, 512), lambda i: (i, CD_GATE // 512)), mat, mat, vec, vec, vec],
        out_specs=[row, row],
        out_shape=[jax.ShapeDtypeStruct((s, 512), BF16), jax.ShapeDtypeStruct((s, 512), F32)],
        scratch_shapes=[pltpu.VMEM((tm, 512), F32), pltpu.VMEM((tm, 512), F32), pltpu.VMEM((SUBLANES, 512), F32)],
        compiler_params=_cp(("arbitrary",)),
    )(xc, hc, wa, wx, ba, bx, lam)


def _rg_bwd(dy, dy_cb, xc, hc, hs, wa, wx, ba, bx, lam, name):
    s = xc.shape[0]
    tm = min(RG_TILE, s)
    nt = s // tm
    ti = lambda i: nt - 1 - i

    def kern(dy_ref, x_ref, g_ref, h_ref, hp_ref, wa_ref, wx_ref, ba_ref, bx_ref, lam_ref,
             dx_ref, dg_ref, dwa_ref, dwx_ref, dba_ref, dbx_ref, dlam_ref, a_s, d_s, g_s, cg, ca):
        first = pl.program_id(0) == 0

        @pl.when(first)
        def _():
            cg[...] = jnp.zeros_like(cg)
            ca[...] = jnp.zeros_like(ca)

        (a, _), pull = jax.vjp(_rg_gates, x_ref[...], wa_ref[...].astype(F32), wx_ref[...].astype(F32), ba_ref[...],
                               bx_ref[...], lam_ref[...])
        hs_t = h_ref[...]
        _, pull_out = jax.vjp(_rg_out, hs_t, g_ref[...])
        dho, dgate = pull_out(dy_ref[...])
        dg_ref[...] = dgate.astype(BF16)
        a_s[...] = a
        d_s[...] = dho

        def body(k, c):
            gn, an = c
            r0 = pl.multiple_of((tm // SUBLANES - 1 - k) * SUBLANES, SUBLANES)
            ab, db = a_s[pl.ds(r0, SUBLANES), :], d_s[pl.ds(r0, SUBLANES), :]
            rows = [None] * SUBLANES
            for j in range(SUBLANES - 1, -1, -1):
                gn = db[j:j + 1] + an * gn
                an = ab[j:j + 1]
                rows[j] = gn
            g_s[pl.ds(r0, SUBLANES), :] = _rows_to_tile(rows)
            return gn, an

        gn, an = lax.fori_loop(0, tm // SUBLANES, body, (cg[0:1, :], ca[0:1, :]))
        cg[0:1, :] = gn
        ca[0:1, :] = an
        g = g_s[...]
        halo = jnp.where(pl.program_id(0) == nt - 1, 0.0, hp_ref[...])
        hprev = pltpu.roll(jnp.concatenate([halo, hs_t], axis=0), 1, axis=0)[SUBLANES:]
        dxc, dwa, dwx, dba, dbx, dlam = pull((g * hprev, g))
        dx_ref[...] = dxc
        _acc(dwa_ref, dwa, first)
        _acc(dwx_ref, dwx, first)
        _acc(dba_ref, dba, first)
        _acc(dbx_ref, dbx, first)
        _acc(dlam_ref, dlam, first)

    row = pl.BlockSpec((tm, 512), lambda i: (ti(i), 0))
    mat = pl.BlockSpec((512, 512), lambda i: (0, 0))
    vec = pl.BlockSpec((1, 512), lambda i: (0, 0))
    hprev = pl.BlockSpec((SUBLANES, 512), lambda i: (jnp.maximum(ti(i) * (tm // SUBLANES) - 1, 0), 0))
    return pl.pallas_call(
        kern, name=name, grid=(nt,),
        in_specs=[pl.BlockSpec((tm, 512), lambda i: (ti(i), dy_cb)), row,
                  pl.BlockSpec((tm, 512), lambda i: (ti(i), CD_GATE // 512)), row, hprev, mat, mat, vec, vec, vec],
        out_specs=[row, row, mat, mat, vec, vec, vec],
        out_shape=[jax.ShapeDtypeStruct((s, 512), F32), jax.ShapeDtypeStruct((s, 512), BF16),
                   jax.ShapeDtypeStruct((512, 512), F32), jax.ShapeDtypeStruct((512, 512), F32),
                   jax.ShapeDtypeStruct((1, 512), F32), jax.ShapeDtypeStruct((1, 512), F32), jax.ShapeDtypeStruct((1, 512), F32)],
        scratch_shapes=[pltpu.VMEM((tm, 512), F32)] * 3 + [pltpu.VMEM((SUBLANES, 512), F32)] * 2,
        compiler_params=_cp(("arbitrary",)),
    )(dy, xc, hc, hs, hs, wa, wx, ba, bx, lam)


AB_REF = dict(z=(0, 512), xbc=(512, 768), dt=(1280, 8), hq=(1288, 512), hf=(1800, 512), hi=(2312, 512), hg=(2824, 512))
AB_INT = dict(z=AB_Z, hq=AB_HQ, hf=AB_HF, hi=AB_HI, hg=AB_HG, xbc=AB_XBC, dt=AB_DT)
CD_REF = dict(q=(0, 512), k=(512, 128), v=(640, 128), gate=(768, 512), xr=(1280, 512))


def _cols(w, start, size):
    return lax.slice_in_dim(w, start, start + size, axis=w.ndim - 1)


def _pad_lanes(v):
    return jnp.concatenate([v.astype(F32), jnp.zeros((LANES - v.shape[0],), F32)]).reshape(1, LANES)


def _taker(g, layer, axis):
    width = g.shape[2 + axis]

    def take(start, size):
        out = []
        while size > 0:
            k, off = divmod(start, width)
            n = min(size, width - off)
            out.append(lax.slice_in_dim(g[k, layer], off, off + n, axis=axis))
            start, size = start + n, size - n
        return out

    return take


def _ab_in_to_int(g, layer):
    take = _taker(g, layer, 1)
    parts = sum([take(*AB_REF[k]) for k in ("z", "hq", "hf", "hi", "hg", "xbc", "dt")], [])
    return jnp.concatenate(parts + [jnp.zeros((g.shape[2], AB_COLS - AB_DT - 8), g.dtype)], axis=1)


def _ab_in_to_ref(w):
    return jnp.concatenate([_cols(w, AB_INT[k], AB_REF[k][1]) for k in ("z", "xbc", "dt", "hq", "hf", "hi", "hg")], axis=-1)


def _pairs_to_ref(w, axis):
    sl = lambda i: lax.slice_in_dim(w, 64 * i, 64 * i + 64, axis=axis)
    return jnp.concatenate([sl(2 * h) for h in range(4)] + [sl(2 * h + 1) for h in range(4)], axis=axis)


PAIR_ORDER = tuple(h for p in range(4) for h in (p, 4 + p))


def _cd_in_to_int(g, layer):
    take = _taker(g, layer, 1)
    q = sum([take(64 * h, 64) for h in PAIR_ORDER], [])
    return jnp.concatenate(q + sum([take(*CD_REF[k]) for k in ("gate", "xr", "k", "v")], []), axis=1)


def _cd_out_to_int(g, layer):
    take = _taker(g, layer, 0)
    return jnp.concatenate(sum([take(64 * h, 64) for h in PAIR_ORDER], []) + take(512, 512), axis=0)


def _whole(g, layer, axis):
    return jnp.concatenate([g[k, layer] for k in range(4)], axis=axis)


def _cd_in_to_ref(w):
    q = _pairs_to_ref(_cols(w, CD_Q, 512), w.ndim - 1)
    return jnp.concatenate([q, _cols(w, CD_K, 128), _cols(w, CD_V, 128), _cols(w, CD_GATE, 512), _cols(w, CD_XR, 512)], axis=-1)


def _block_diag(w):
    tiled = jnp.concatenate([w.reshape(RG_WIDTH, 64)] * 8, axis=1)
    return tiled * _BLOCK_DIAG_MASK.astype(w.dtype)


_BLOCK_DIAG_MASK = np.kron(np.eye(8, dtype=np.float32), np.ones((64, 64), np.float32))


def _diag_blocks(m):
    return jnp.sum((m * _BLOCK_DIAG_MASK).reshape(8, 64, 8, 64), axis=2)


def _conv8(w):
    return jnp.concatenate([w, jnp.zeros((SUBLANES - w.shape[0], w.shape[1]), w.dtype)], axis=0)


def _hg_lower_bounds(hg_lower):
    sm = jax.nn.softmax(hg_lower.astype(F32), axis=0)
    return jnp.clip(jnp.cumsum(sm, axis=0) - sm[0], 0.0, 1.0)


def _local_step(x, target, w):
    s = x.shape[0]
    sv_w = [(_whole(w["ffn_w_up"], l, 1), _whole(w["ffn_w_down"], l, 0)) for l in range(DEPTH)]
    lb_all, lb_pull = jax.vjp(_hg_lower_bounds, w["hg_lower"])
    grads = {k: [None] * (v.shape[1] if k in BIG else v.shape[0]) for k, v in w.items()}
    saved = []
    xb = x.astype(BF16)
    for l in range(DEPTH):
        j = l // 2
        t = f"l{l}"
        sv = dict(x=x, xb=xb)
        if l % 2 == 0:
            win = _ab_in_to_int(w["ab_w_in"], j)
            wout = _whole(w["ab_w_out"], j, 0)
            h = _mm([(xb, 0, win, 0, D_MODEL)], AB_COLS, F32, t + "_in")
            w8 = _conv8(w["ssd_conv_w"][j])
            xbcc = _dwconv_fwd(h, AB_XBC // CONV_CB, SSD_CONV_DIM, w8, w["ssd_conv_b"][j], 4, t + "_conv")
            sp = (_pad_lanes(w["ssd_dt_bias"][j]), _pad_lanes(w["ssd_a_log"][j]), _pad_lanes(w["ssd_d"][j]),
                  w["ssd_norm_w"][j].reshape(1, 512))
            ya, ssts = _ssd_fwd(h, xbcc, *sp, t + "_ssd")
            hp = (lb_all[j].reshape(1, 512), w["hg_norm_w"][j].reshape(1, LANES))
            yb, hsts = _hg_fwd(h, *hp, t + "_hg")
            sv.update(win=win, wout=wout, h=h, w8=w8, xbcc=xbcc, sp=sp, ssts=ssts, hp=hp, hsts=hsts)
        else:
            win = _cd_in_to_int(w["cd_w_in"], j)
            wout = _cd_out_to_int(w["cd_w_out"], j)
            h = _mm([(xb, 0, win, 0, D_MODEL)], CD_COLS, F32, t + "_in")
            srow = _pad_lanes(w["swa_sinks"][j])
            ya = _swa_fwd(h, srow, t + "_swa")
            w8 = _conv8(w["rg_conv_w"][j])
            xc = _dwconv_fwd(h, CD_XR // CONV_CB, RG_WIDTH, w8, w["rg_conv_b"][j], 4, t + "_conv")
            rp = (_block_diag(w["rg_wa"][j]).astype(BF16), _block_diag(w["rg_wx"][j]).astype(BF16),
                  w["rg_ba"][j].reshape(1, 512), w["rg_bx"][j].reshape(1, 512), w["rg_lambda"][j].reshape(1, 512))
            yb, hs = _rg_fwd(xc, h, *rp, t + "_rg")
            sv.update(win=win, wout=wout, h=h, w8=w8, srow=srow, xc=xc, rp=rp, hs=hs)
        x1, x1b, r1 = _mm_ln([(ya, 0, wout, 0, 512), (yb, 0, wout, 1, 512)], x, w["ln_g"][l, 0], w["ln_b"][l, 0], t + "_out_ln")
        wup, wdown = sv_w[l]
        fw8 = _conv8(w["ffn_conv_w"][l])
        hu = _mm([(x1b, 0, wup, 0, D_MODEL)], 2 * FFN_DIM, BF16, t + "_up")
        a, hg, hv = _ffn_gate_fwd(hu, fw8, w["ffn_conv_b"][l], t + "_gate")
        x, xb, r2 = _mm_ln([(a, 0, wdown, 0, FFN_DIM)], x1, w["ln_g"][l, 1], w["ln_b"][l, 1], t + "_down_ln")
        sv.update(ya=ya, yb=yb, r1=r1, x1b=x1b, hu=hu, hg=hg, hv=hv, a=a, r2=r2, fw8=fw8)
        saved.append(sv)

    loss = 0.5 * jnp.sum(_sqerr(x, target, "sqerr")) / D_MODEL
    dya, dyb, ca, cb = x, target, 1.0 / D_MODEL, -1.0 / D_MODEL
    for l in range(DEPTH - 1, -1, -1):
        j = l // 2
        t = f"l{l}b"
        sv = saved[l]
        wup, wdown = sv_w[l]
        dr2, dg2, db2 = _ln_bwd(dya, dyb, ca, cb, sv["r2"], w["ln_g"][l, 1], t + "_ln2")
        da = _mm([(dr2, 0, wdown, 0, D_MODEL)], FFN_DIM, BF16, t + "_down", trans_b=True)
        grads["ffn_w_down"][l] = _mm_tn(sv["a"], 0, FFN_DIM, dr2, 0, D_MODEL, t + "_wdown")
        dhg, dhu, dfw8, dfb = _ffn_gate_bwd(sv["hu"], sv["hg"], sv["hv"], da, sv["fw8"], t + "_gate")
        grads["ffn_conv_w"][l] = dfw8[:3]
        grads["ffn_conv_b"][l] = dfb
        dx1 = _mm([(dhg, 0, wup, 0, FFN_DIM), (dhu, 0, wup, 1, FFN_DIM)], D_MODEL, F32, t + "_up", trans_b=True)
        grads["ffn_w_up"][l] = jnp.concatenate([_mm_tn(sv["x1b"], 0, D_MODEL, dhg, 0, FFN_DIM, t + "_wup_g"),
                                                _mm_tn(sv["x1b"], 0, D_MODEL, dhu, 0, FFN_DIM, t + "_wup_u")], axis=1)
        dr1, dg1, db1 = _ln_bwd(dr2, dx1, ALPHA, 1.0, sv["r1"], w["ln_g"][l, 0], t + "_ln1")
        grads["ln_g"][l] = jnp.stack([dg1, dg2])
        grads["ln_b"][l] = jnp.stack([db1, db2])
        win, wout, h = sv["win"], sv["wout"], sv["h"]
        dycat = _mm([(dr1, 0, wout, 0, D_MODEL)], D_MODEL, F32, t + "_out", trans_b=True)
        dwout = jnp.concatenate([_mm_tn(sv["ya"], 0, 512, dr1, 0, D_MODEL, t + "_wout_a"),
                                 _mm_tn(sv["yb"], 0, 512, dr1, 0, D_MODEL, t + "_wout_b")], axis=0)
        if l % 2 == 0:
            dz, dxbcc, ddt, ddtb, dal, ddsk, dnw = _ssd_bwd(dycat, 0, h, sv["xbcc"], sv["ssts"], *sv["sp"], t + "_ssd")
            dxbc, dcw, dcb = _dwconv_bwd(dxbcc, h, AB_XBC // CONV_CB, SSD_CONV_DIM, sv["w8"], 4, t + "_conv")
            dq, df, di, dg, dlb, dhnw = _hg_bwd(dycat, 1, h, sv["hsts"], *sv["hp"], t + "_hg")
            grads["ab_w_out"][j] = dwout
            grads["ssd_conv_w"][j], grads["ssd_conv_b"][j] = dcw[:4], dcb
            grads["ssd_dt_bias"][j], grads["ssd_a_log"][j], grads["ssd_d"][j] = ddtb[0, :8], dal[0, :8], ddsk[0, :8]
            grads["ssd_norm_w"][j], grads["hg_norm_w"][j] = dnw[0], dhnw[0]
            grads["hg_lower"][j] = dlb[0]
            pieces = [(dz, 0, AB_Z, 512), (dq, 0, AB_HQ, 512), (df, 0, AB_HF, 512), (di, 0, AB_HI, 512), (dg, 0, AB_HG, 512),
                      (dxbc, 0, AB_XBC, 256), (dxbc, 1, AB_XBC + 256, 256), (dxbc, 2, AB_XBC + 512, 256), (ddt, 0, AB_DT, LANES)]
            dwin = [_mm_tn(sv["xb"], 0, D_MODEL, p, 0, p.shape[1], t + f"_win{i}")
                    for i, p in enumerate((dz, dq, df, di, dg, dxbc, ddt))]
            grads["ab_w_in"][j] = _ab_in_to_ref(jnp.concatenate(dwin, axis=1))
        else:
            dq, dk, dv, dsr = _swa_bwd(dycat, 0, h, sv["srow"], t + "_swa")
            dxc, dgate, dwa, dwx, dba, dbx, dlam = _rg_bwd(dycat, 1, sv["xc"], h, sv["hs"], *sv["rp"], t + "_rg")
            dxr, dcw, dcb = _dwconv_bwd(dxc, h, CD_XR // CONV_CB, RG_WIDTH, sv["w8"], 4, t + "_conv")
            grads["cd_w_out"][j] = jnp.concatenate([_pairs_to_ref(dwout[:512], 0), dwout[512:]], axis=0)
            grads["swa_sinks"][j] = dsr[0, :8]
            grads["rg_conv_w"][j], grads["rg_conv_b"][j] = dcw[:4], dcb
            grads["rg_wa"][j], grads["rg_wx"][j] = _diag_blocks(dwa), _diag_blocks(dwx)
            grads["rg_ba"][j], grads["rg_bx"][j], grads["rg_lambda"][j] = dba[0], dbx[0], dlam[0]
            pieces = [(dq, 0, CD_Q, 512), (dgate, 0, CD_GATE, 512), (dxr, 0, CD_XR, 512), (dk, 0, CD_K, LANES), (dv, 0, CD_V, LANES)]
            dwin = [_mm_tn(sv["xb"], 0, D_MODEL, p, 0, p.shape[1], t + f"_win{i}")
                    for i, p in enumerate((dq, dgate, dxr, dk, dv))]
            grads["cd_w_in"][j] = _cd_in_to_ref(jnp.concatenate(dwin, axis=1))
        dx0 = _mm([(p, pcb, win, off // k, k) for p, pcb, off, k in pieces], D_MODEL, F32, t + "_in", trans_b=True)
        dya, dyb, ca, cb = dr1, dx0, ALPHA, 1.0
    grad_x = _axpby(dya, dyb, ca, cb, "grad_x")
    out = {k: (v if k in BIG else jnp.stack(v)) for k, v in grads.items()}
    (out["hg_lower"],) = lb_pull(out["hg_lower"])
    return loss, grad_x, out


MESH = pl.DeviceIdType.MESH
ANY = pl.BlockSpec(memory_space=pl.ANY)


def _place():
    x, y, c = lax.axis_index("x"), lax.axis_index("y"), lax.axis_index("c")
    return x, y, c, [(1 - x, y), (x, 1 - y), (1 - x, 1 - y)]


def _rcopy(src, dst, send_sems, recv_sems, k, to):
    return pltpu.make_async_remote_copy(src_ref=src, dst_ref=dst, send_sem=send_sems.at[k], recv_sem=recv_sems.at[k],
                                        device_id=to, device_id_type=MESH)


def _zeros_index(ndim):
    return (0,) * ndim


def _gather_chips(parts, name):
    n = len(parts)

    def body(*refs):
        w_refs, out_refs, (send_sems, recv_sems) = refs[:n], refs[n:2 * n], refs[2 * n:]
        x, y, c, chips = _place()
        me, sib = 2 * x + y, (x, y, 1 - c)
        first = [_rcopy(w_refs[a].at[c], out_refs[a].at[me, c], send_sems, recv_sems, 6 * a + k, (cx, cy, c))
                 for a in range(n) for k, (cx, cy) in enumerate(chips)]
        for cp in first:
            cp.start()
        passed = []
        for a in range(n):
            for k, (cx, cy) in enumerate(chips):
                blk = out_refs[a].at[2 * cx + cy, c]
                _rcopy(blk, blk, send_sems, recv_sems, 6 * a + k, (cx, cy, c)).wait_recv()
                passed.append(_rcopy(blk, blk, send_sems, recv_sems, 6 * a + 3 + k, sib))
                passed[-1].start()
        for a in range(n):
            for k, (cx, cy) in enumerate(chips):
                blk = out_refs[a].at[2 * cx + cy, 1 - c]
                _rcopy(blk, blk, send_sems, recv_sems, 6 * a + 3 + k, sib).wait_recv()
        for cp in first + passed:
            cp.wait_send()

    outs = pl.pallas_call(
        body, name=name, in_specs=[ANY] * n, out_specs=[ANY] * n,
        out_shape=[jax.ShapeDtypeStruct((4,) + p.shape, p.dtype) for p in parts],
        scratch_shapes=[pltpu.SemaphoreType.DMA((6 * n,)), pltpu.SemaphoreType.DMA((6 * n,))],
    )(*parts)
    me = 2 * lax.axis_index("x") + lax.axis_index("y")
    return [lax.dynamic_update_slice(o, p[None], (me,) + _zeros_index(p.ndim)) for o, p in zip(outs, parts)]


def _swap_halves(gs, name):
    n = len(gs)

    def body(*refs):
        g_refs, x_refs, (send_sems, recv_sems) = refs[:n], refs[n:2 * n], refs[2 * n:]
        x, y, c, _ = _place()
        cps = [_rcopy(g_refs[a].at[:, 1 - c], x_refs[a], send_sems, recv_sems, a, (x, y, 1 - c)) for a in range(n)]
        for cp in cps:
            cp.start()
        for cp in cps:
            cp.wait()

    return pl.pallas_call(
        body, name=name, in_specs=[ANY] * n, out_specs=[ANY] * n,
        out_shape=[jax.ShapeDtypeStruct((g.shape[0],) + g.shape[2:], g.dtype) for g in gs],
        scratch_shapes=[pltpu.SemaphoreType.DMA((n,)), pltpu.SemaphoreType.DMA((n,))],
    )(*gs)


def _scatter_chips(hbs, name):
    n = len(hbs)

    def body(*refs):
        h_refs, y_refs, (send_sems, recv_sems) = refs[:n], refs[n:2 * n], refs[2 * n:]
        x, y, c, chips = _place()
        me = 2 * x + y
        sends = [_rcopy(h_refs[a].at[2 * cx + cy], y_refs[a].at[me], send_sems, recv_sems, 3 * a + k, (cx, cy, c))
                 for a in range(n) for k, (cx, cy) in enumerate(chips)]
        for cp in sends:
            cp.start()
        for a in range(n):
            for k, (cx, cy) in enumerate(chips):
                blk = y_refs[a].at[2 * cx + cy]
                _rcopy(blk, blk, send_sems, recv_sems, 3 * a + k, (cx, cy, c)).wait_recv()
        for cp in sends:
            cp.wait_send()

    outs = pl.pallas_call(
        body, name=name, in_specs=[ANY] * n, out_specs=[ANY] * n,
        out_shape=[jax.ShapeDtypeStruct(h.shape, h.dtype) for h in hbs],
        scratch_shapes=[pltpu.SemaphoreType.DMA((3 * n,)), pltpu.SemaphoreType.DMA((3 * n,))],
    )(*hbs)
    me = 2 * lax.axis_index("x") + lax.axis_index("y")
    return [lax.dynamic_update_slice(o, lax.dynamic_slice_in_dim(h, me, 1, axis=0), (me,) + _zeros_index(h.ndim - 1))
            for o, h in zip(outs, hbs)]


def _share_half(fs, name):
    n = len(fs)

    def body(*refs):
        f_refs, out_refs, (send_sems, recv_sems) = refs[:n], refs[n:2 * n], refs[2 * n:]
        x, y, c, _ = _place()
        sib = (x, y, 1 - c)
        cps = [_rcopy(f_refs[a], out_refs[a].at[c], send_sems, recv_sems, a, sib) for a in range(n)]
        for cp in cps:
            cp.start()
        for a in range(n):
            _rcopy(f_refs[a], out_refs[a].at[1 - c], send_sems, recv_sems, a, sib).wait_recv()
        for cp in cps:
            cp.wait_send()

    outs = pl.pallas_call(
        body, name=name, in_specs=[ANY] * n, out_specs=[ANY] * n,
        out_shape=[jax.ShapeDtypeStruct((2,) + f.shape, f.dtype) for f in fs],
        scratch_shapes=[pltpu.SemaphoreType.DMA((n,)), pltpu.SemaphoreType.DMA((n,))],
    )(*fs)
    c = lax.axis_index("c")
    return [lax.dynamic_update_slice(o, f[None], (c,) + _zeros_index(f.ndim)) for o, f in zip(outs, fs)]


def _gather_devices(v, name):
    m_per, n = v.shape

    def body(x_ref, out_ref, send_sems, recv_sems, local_sem):
        x, y, c, chips = _place()
        me, sib = (x, y, c), (x, y, 1 - c)

        def rows(px, py, pc):
            return out_ref.at[pl.ds((4 * px + 2 * py + pc) * m_per, m_per), :]

        def copy(k, block, to, src=None):
            return _rcopy(rows(*block) if src is None else src, rows(*block), send_sems, recv_sems, k, to)

        mine = pltpu.make_async_copy(x_ref, rows(*me), local_sem)
        mine.start()
        first = [copy(0, me, sib, src=x_ref)]
        first += [copy(1 + j, me, (*chip, c), src=x_ref) for j, chip in enumerate(chips)]
        for cp in first:
            cp.start()
        passed = [copy(4 + j, (*chip, c), sib) for j, chip in enumerate(chips)]
        for j, chip in enumerate(chips):
            copy(1 + j, (*chip, c), me).wait_recv()
            passed[j].start()
        copy(0, sib, me).wait_recv()
        for j, chip in enumerate(chips):
            copy(4 + j, (*chip, 1 - c), me).wait_recv()
        for cp in first + passed:
            cp.wait_send()
        mine.wait()

    return pl.pallas_call(
        body, name=name, out_shape=jax.ShapeDtypeStruct((8 * m_per, n), v.dtype),
        in_specs=[pl.BlockSpec(memory_space=pltpu.VMEM)], out_specs=pl.BlockSpec(memory_space=pltpu.VMEM),
        scratch_shapes=[pltpu.SemaphoreType.DMA((7,)), pltpu.SemaphoreType.DMA((7,)), pltpu.SemaphoreType.DMA(())],
        compiler_params=_cp(),
    )(v)


ROW_TILES = (512, 256, 352, 128)


def _add_half(g, xr, c, name):
    _, _, r, n = g.shape
    tr = _pick(r, ROW_TILES)

    def kern(c_ref, g_ref, x_ref, o_ref):
        o_ref[...] = (g_ref[0] + x_ref[...]).astype(BF16)

    return pl.pallas_call(
        kern, name=name,
        grid_spec=pltpu.PrefetchScalarGridSpec(
            num_scalar_prefetch=1, grid=(4, r // tr),
            in_specs=[pl.BlockSpec((1, 1, tr, n), lambda k, i, c_ref: (k, c_ref[0], i, 0)),
                      pl.BlockSpec((1, tr, n), lambda k, i, c_ref: (k, i, 0))],
            out_specs=pl.BlockSpec((1, tr, n), lambda k, i, c_ref: (k, i, 0))),
        out_shape=jax.ShapeDtypeStruct(xr.shape, BF16), compiler_params=_cp(("parallel", "parallel")),
    )(c.reshape(1).astype(jnp.int32), g, xr)


def _sum_blocks(y, name, tr=None):
    n, r, cols = y.shape
    tr = tr or _pick(r, ROW_TILES)
    tr = tr if r % tr == 0 else r

    def kern(y_ref, o_ref):
        acc = y_ref[0].astype(F32)
        for k in range(1, n):
            acc = acc + y_ref[k].astype(F32)
        o_ref[...] = acc

    return pl.pallas_call(
        kern, name=name, grid=(r // tr,), in_specs=[pl.BlockSpec((n, tr, cols), lambda i: (0, i, 0))],
        out_specs=pl.BlockSpec((tr, cols), lambda i: (i, 0)), out_shape=jax.ShapeDtypeStruct((r, cols), F32),
        compiler_params=_cp(("parallel",)),
    )(y)


def _adamw(w, g, m, v, name, tr=256):
    r, n = w.shape
    tr = tr if r % tr == 0 else r

    def kern(w_ref, g_ref, m_ref, v_ref, d_ref, nm_ref, nv_ref):
        gg = g_ref[...]
        nm = ADAM_B1 * m_ref[...] + (1.0 - ADAM_B1) * gg
        nv = ADAM_B2 * v_ref[...] + (1.0 - ADAM_B2) * (gg * gg)
        m_hat = nm / (1.0 - ADAM_B1 ** ADAM_STEP)
        v_hat = nv / (1.0 - ADAM_B2 ** ADAM_STEP)
        d_ref[...] = -ADAM_LR * (m_hat / (jnp.sqrt(v_hat) + ADAM_EPS) + ADAM_WD * w_ref[...])
        nm_ref[...] = nm
        nv_ref[...] = nv

    blk = pl.BlockSpec((tr, n), lambda i: (i, 0))
    return pl.pallas_call(
        kern, name=name, grid=(r // tr,), in_specs=[blk] * 4, out_specs=[blk] * 3,
        out_shape=[jax.ShapeDtypeStruct((r, n), F32)] * 3, compiler_params=_cp(("parallel",)),
    )(w, g, m, v)


def _rows_of(n):
    return -(-n // (LANES * SUBLANES)) * SUBLANES


def _pack(arrs, lead, dtype, mult):
    ls = arrs[0].shape[:lead]
    blocks, total = [], 0
    for a in arrs:
        f = a.astype(dtype).reshape(ls + (-1,))
        n = f.shape[-1]
        rows = _rows_of(n)
        if rows * LANES != n:
            f = jnp.concatenate([f, jnp.zeros(ls + (rows * LANES - n,), dtype)], axis=-1)
        blocks.append(f.reshape(ls + (rows, LANES)))
        total += rows
    if total % mult:
        blocks.append(jnp.zeros(ls + (mult - total % mult, LANES), dtype))
    return jnp.concatenate(blocks, axis=lead)


def _unpack(buf, lead, shapes):
    ls = buf.shape[:lead]
    out, off = [], 0
    for sh in shapes:
        n = int(np.prod(sh))
        rows = _rows_of(n)
        flat = lax.slice_in_dim(buf, off, off + rows, axis=lead).reshape(ls + (rows * LANES,))
        if rows * LANES != n:
            flat = lax.slice_in_dim(flat, 0, n, axis=lead)
        out.append(flat.reshape(ls + tuple(sh)))
        off += rows
    return out


BIG = ("ab_w_in", "ab_w_out", "cd_w_in", "cd_w_out", "ffn_w_up", "ffn_w_down")
BIG_COLS = ("ab_w_in", "cd_w_in", "ffn_w_up")
SMALL_SHARDED = ("ssd_conv_w", "rg_conv_w", "rg_conv_b", "rg_ba", "rg_bx", "rg_lambda", "ffn_conv_w", "ln_g", "ln_b")
WEIGHTS = ("ab_w_in", "ssd_conv_w", "ssd_conv_b", "ssd_dt_bias", "ssd_a_log", "ssd_d", "ssd_norm_w", "hg_lower", "hg_norm_w",
           "ab_w_out", "cd_w_in", "swa_sinks", "rg_conv_w", "rg_conv_b", "rg_wa", "rg_ba", "rg_wx", "rg_bx", "rg_lambda",
           "cd_w_out", "ffn_w_up", "ffn_conv_w", "ffn_conv_b", "ffn_w_down", "ln_g", "ln_b")
SMALL = tuple(n for n in WEIGHTS if n not in BIG)


def _full_from_shards(name, g):
    if name in BIG_COLS or name in SMALL_SHARDED:
        return jnp.concatenate([g[k] for k in range(4)], axis=-1)
    return jnp.concatenate([g[k] for k in range(4)], axis=1)


def _layer_shards(name, f):
    k, n = f.shape
    if name in BIG_COLS:
        return jnp.transpose(f.reshape(k, 4, n // 4), (1, 0, 2))
    return f.reshape(4, k // 4, n)


def kernel(x, ab_w_in, ssd_conv_w, ssd_conv_b, ssd_dt_bias, ssd_a_log, ssd_d, ssd_norm_w, hg_lower, hg_norm_w, ab_w_out, cd_w_in, swa_sinks, rg_conv_w, rg_conv_b, rg_wa, rg_ba, rg_wx, rg_bx, rg_lambda, cd_w_out, ffn_w_up, ffn_conv_w, ffn_conv_b, ffn_w_down, ln_g, ln_b, loss_target, m_ab_w_in, m_ssd_conv_w, m_ssd_conv_b, m_ssd_dt_bias, m_ssd_a_log, m_ssd_d, m_ssd_norm_w, m_hg_lower, m_hg_norm_w, m_ab_w_out, m_cd_w_in, m_swa_sinks, m_rg_conv_w, m_rg_conv_b, m_rg_wa, m_rg_ba, m_rg_wx, m_rg_bx, m_rg_lambda, m_cd_w_out, m_ffn_w_up, m_ffn_conv_w, m_ffn_conv_b, m_ffn_w_down, m_ln_g, m_ln_b, v_ab_w_in, v_ssd_conv_w, v_ssd_conv_b, v_ssd_dt_bias, v_ssd_a_log, v_ssd_d, v_ssd_norm_w, v_hg_lower, v_hg_norm_w, v_ab_w_out, v_cd_w_in, v_swa_sinks, v_rg_conv_w, v_rg_conv_b, v_rg_wa, v_rg_ba, v_rg_wx, v_rg_bx, v_rg_lambda, v_cd_w_out, v_ffn_w_up, v_ffn_conv_w, v_ffn_conv_b, v_ffn_w_down, v_ln_g, v_ln_b):
    args = locals()
    w = {n: args[n] for n in WEIGHTS}
    mom = {n: args["m_" + n] for n in WEIGHTS}
    var = {n: args["v_" + n] for n in WEIGHTS}
    cx, cy, cc = lax.axis_index("x"), lax.axis_index("y"), lax.axis_index("c")
    chip = 2 * cx + cy

    halves = [w[n].astype(BF16).reshape((2, w[n].shape[0] // 2) + w[n].shape[1:]) for n in BIG]
    gathered = _gather_chips(halves, "gather_weights")
    full = {n: g.reshape((4,) + w[n].shape) for n, g in zip(BIG, gathered)}
    sp = _pack([w[n] for n in SMALL_SHARDED], 0, F32, SUBLANES)
    sg = _gather_devices(sp, "gather_small").reshape(4, 2, sp.shape[0], LANES)[:, 0]
    for n, g in zip(SMALL_SHARDED, _unpack(sg, 1, [w[n].shape for n in SMALL_SHARDED])):
        full[n] = _full_from_shards(n, g)
    for n in SMALL:
        full.setdefault(n, w[n])

    loss, grad_x, grads = _local_step(x[0], loss_target[0], full)
    loss = lax.psum(loss, ("x", "y", "c"))

    g4 = []
    for n in BIG:
        st = jnp.stack([_layer_shards(n, f) for f in grads[n]], axis=1)
        g4.append(st.reshape(4, 2, st.shape[1] // 2 * st.shape[2], st.shape[3]))
    hbs = [_add_half(g, xr, cc, "add_halves_" + n) for n, g, xr in zip(BIG, g4, _swap_halves(g4, "swap_halves"))]
    own = [_sum_blocks(y, "sum_chips_" + n) for n, y in zip(BIG, _scatter_chips(hbs, "scatter_chips"))]
    gout = {n: o.reshape(w[n].shape) for n, o in zip(BIG, _share_half(own, "share_half"))}

    small_shapes = [grads[n].shape for n in SMALL]
    gs = _pack([grads[n] for n in SMALL], 0, F32, SUBLANES)
    gsum = _sum_blocks(_gather_devices(gs, "gather_small_grads").reshape(8, gs.shape[0], LANES), "sum_small")
    for n, g in zip(SMALL, _unpack(gsum, 0, small_shapes)):
        if n in SMALL_SHARDED:
            width = w[n].shape[-1]
            g = lax.dynamic_slice_in_dim(g, chip * width, width, axis=g.ndim - 1)
        gout[n] = g

    delta, new_m, new_v = {}, {}, {}
    for n in BIG:
        sh = w[n].shape
        two = lambda a: a.reshape(-1, sh[-1])
        d, nm, nv = _adamw(two(w[n]), two(gout[n]), two(mom[n]), two(var[n]), "adamw_" + n)
        delta[n], new_m[n], new_v[n] = d.reshape(sh), nm.reshape(sh), nv.reshape(sh)
    local_shapes = [w[n].shape for n in SMALL]
    packs = [_pack([d[n] for n in SMALL], 0, F32, SUBLANES) for d in (w, gout, mom, var)]
    for dst, buf in zip((delta, new_m, new_v), _adamw(*packs, "adamw_small")):
        dst.update(zip(SMALL, _unpack(buf, 0, local_shapes)))

    return (loss, grad_x[None], *[gout[n] for n in WEIGHTS], *[delta[n] for n in WEIGHTS],
            *[new_m[n] for n in WEIGHTS], *[new_v[n] for n in WEIGHTS])
```

```python
import functools

import numpy as np
import jax
import jax.numpy as jnp
from jax import lax
from jax.experimental import pallas as pl
from jax.experimental.pallas import tpu as pltpu

F32 = jnp.float32
BF16 = jnp.bfloat16

D_MODEL = 1024
DEPTH = 4
SSD_HEADS = 8
SSD_D_INNER = 512
SSD_CONV_DIM = 768
SSD_CHUNK = 128
HG_CHUNK = 64
HG_HEADS = 4
SWA_BLOCK = 128
RG_WIDTH = 512
FFN_DIM = 2816
LN_EPS = 1e-5
RMS_EPS = 1e-6
MASK_VALUE = -1e9
ALPHA = (2 * DEPTH) ** 0.25
RG_C = 8.0
ADAM_LR, ADAM_B1, ADAM_B2, ADAM_EPS, ADAM_WD, ADAM_STEP = 0.001, 0.9, 0.999, 1e-08, 0.01, 10

LANES = 128
SUBLANES = 8
HALO = 16
CONV_CB = 256
VMEM_LIMIT = 56 * 1024 * 1024

AB_Z, AB_HQ, AB_HF, AB_HI, AB_HG, AB_XBC, AB_DT, AB_COLS = 0, 512, 1024, 1536, 2048, 2560, 3328, 3456
CD_Q, CD_GATE, CD_XR, CD_K, CD_V, CD_COLS = 0, 512, 1024, 1536, 1664, 1792


def _cp(sem=None):
    kw = dict(vmem_limit_bytes=VMEM_LIMIT)
    if sem is not None:
        kw["dimension_semantics"] = sem
    return pltpu.CompilerParams(**kw)


def _dot(a, b, dims=(((1,), (0,)), ((), ())), precision=None):
    return lax.dot_general(a, b, dims, precision=precision, preferred_element_type=F32)


NN = (((1,), (0,)), ((), ()))
NT = (((1,), (1,)), ((), ()))
TN = (((0,), (0,)), ((), ()))


@functools.partial(jax.custom_vjp, nondiff_argnums=(2,))
def _bdot(a, b, dims=NN):
    return _dot(a.astype(BF16), b.astype(BF16), dims)


def _bdot_fwd(a, b, dims):
    return _bdot(a, b, dims), (a, b)


def _bdot_bwd(dims, res, ct):
    a, b = res
    ab, bb, cb = a.astype(BF16), b.astype(BF16), ct.astype(BF16)
    if dims == NN:
        da, db = _dot(cb, bb, NT), _dot(ab, cb, TN)
    elif dims == NT:
        da, db = _dot(cb, bb, NN), _dot(cb, ab, TN)
    else:
        da, db = _dot(bb, cb, NT), _dot(ab, cb, NN)
    return da.astype(a.dtype), db.astype(b.dtype)


_bdot.defvjp(_bdot_fwd, _bdot_bwd)


WIDE_TILES = (1408, 1152, 1024, 896, 768, 512, 384, 256, 128)
MM_BLOCK_BYTES = 6 * 1024 * 1024


def _pick(n, cands):
    for c in cands:
        if n % c == 0:
            return c
    return n


def _mm(pairs, n_out, out_dtype, name, trans_b=False, tm=512):
    m = pairs[0][0].shape[0]
    a_row_bytes = sum(p[4] * p[0].dtype.itemsize for p in pairs)
    b_col_bytes = sum(p[4] * p[2].dtype.itemsize for p in pairs)
    tm = min(2 * tm if 2 * tm * a_row_bytes <= 2 * MM_BLOCK_BYTES else tm, m)
    tn = _pick(n_out, [c for c in WIDE_TILES if c * b_col_bytes <= MM_BLOCK_BYTES])
    n = len(pairs)

    def kern(*refs):
        acc = None
        for i in range(n):
            p = _bdot(refs[i][...], refs[n + i][...], NT if trans_b else NN)
            acc = p if acc is None else acc + p
        refs[2 * n][...] = acc.astype(out_dtype)

    in_specs = []
    for a, acb, b, bcb, k in pairs:
        in_specs.append(pl.BlockSpec((tm, k), functools.partial(lambda i, j, c: (i, c), c=acb)))
    for a, acb, b, bcb, k in pairs:
        if callable(bcb):
            in_specs.append(bcb(tn))
        elif trans_b:
            in_specs.append(pl.BlockSpec((tn, k), functools.partial(lambda i, j, c: (j, c), c=bcb)))
        else:
            in_specs.append(pl.BlockSpec((k, tn), functools.partial(lambda i, j, c: (c, j), c=bcb)))
    return pl.pallas_call(
        kern, name=name, grid=(m // tm, n_out // tn), in_specs=in_specs,
        out_specs=pl.BlockSpec((tm, tn), lambda i, j: (i, j)),
        out_shape=jax.ShapeDtypeStruct((m, n_out), out_dtype),
        compiler_params=_cp(("parallel", "arbitrary")),
    )(*[p[0] for p in pairs], *[p[2] for p in pairs])


def _shard_as_col_tile(g, layer):
    def spec(tn):
        assert tn == g.shape[3]
        return pl.BlockSpec((None, None, g.shape[2], tn), lambda i, j: (j, layer, 0, 0))
    return spec


def _shard_transposed(g, chip, layer):
    return lambda tn: pl.BlockSpec((None, None, tn, g.shape[3]), lambda i, j: (chip, layer, j, 0))


def _mm_tn(a, a_cb, ka, g, g_cb, ng, name, tm=2048):
    m = a.shape[0]
    tm = min(tm, m)
    tk = _pick(ka, (1024, 1408, 512, 256, 128))
    tn = _pick(ng, WIDE_TILES)
    nm = m // tm

    def kern(a_ref, g_ref, o_ref):
        p = _bdot(a_ref[...], g_ref[...], TN)

        @pl.when(pl.program_id(2) == 0)
        def _():
            o_ref[...] = p

        @pl.when(pl.program_id(2) > 0)
        def _():
            o_ref[...] += p

    ak, gk = ka // tk, ng // tn
    return pl.pallas_call(
        kern, name=name, grid=(ak, gk, nm),
        in_specs=[pl.BlockSpec((tm, tk), lambda i, j, r: (r, a_cb * ak + i)),
                  pl.BlockSpec((tm, tn), lambda i, j, r: (r, g_cb * gk + j))],
        out_specs=pl.BlockSpec((tk, tn), lambda i, j, r: (i, j)),
        out_shape=jax.ShapeDtypeStruct((ka, ng), F32),
        compiler_params=_cp(("parallel", "parallel", "arbitrary")),
    )(a, g)


def _mm_ln(pairs, x, g, b, name, tm=512):
    m, d = x.shape
    tm = min(tm, m)
    n = len(pairs)

    def kern(*refs):
        x_ref, g_ref, b_ref, y_ref, yb_ref, r_ref = refs[2 * n:]
        r = ALPHA * x_ref[...]
        for i in range(n):
            r = r + _bdot(refs[i][...], refs[n + i][...])
        mu = jnp.mean(r, -1, keepdims=True)
        xc = r - mu
        var = jnp.mean(xc * xc, -1, keepdims=True)
        y = xc * lax.rsqrt(var + LN_EPS) * g_ref[...] + b_ref[...]
        y_ref[...] = y
        yb_ref[...] = y.astype(BF16)
        r_ref[...] = r

    row = pl.BlockSpec((tm, d), lambda i: (i, 0))
    vec = pl.BlockSpec((1, d), lambda i: (0, 0))
    in_specs = [pl.BlockSpec((tm, k), functools.partial(lambda i, c: (i, c), c=acb)) for _, acb, _, _, k in pairs]
    in_specs += [pl.BlockSpec((k, d), functools.partial(lambda i, c: (c, 0), c=bcb)) for _, _, _, bcb, k in pairs]
    return pl.pallas_call(
        kern, name=name, grid=(m // tm,), in_specs=in_specs + [row, vec, vec], out_specs=[row, row, row],
        out_shape=[jax.ShapeDtypeStruct((m, d), F32), jax.ShapeDtypeStruct((m, d), BF16), jax.ShapeDtypeStruct((m, d), F32)],
        compiler_params=_cp(("parallel",)),
    )(*[p[0] for p in pairs], *[p[2] for p in pairs], x, g.reshape(1, d), b.reshape(1, d))


def _ln_bwd(dya, dyb, ca, cb, r, g, name, tm=512):
    s, d = r.shape
    tm = min(tm, s)

    def kern(a_ref, b_ref, r_ref, g_ref, dr_ref, dg_ref, db_ref):
        dy = ca * a_ref[...] + cb * b_ref[...]
        rr = r_ref[...]
        mu = jnp.mean(rr, -1, keepdims=True)
        xc = rr - mu
        var = jnp.mean(xc * xc, -1, keepdims=True)
        rstd = lax.rsqrt(var + LN_EPS)
        xhat = xc * rstd
        dxh = dy * g_ref[...]
        dr_ref[...] = rstd * (dxh - jnp.mean(dxh, -1, keepdims=True) - xhat * jnp.mean(dxh * xhat, -1, keepdims=True))
        dg = jnp.sum(dy * xhat, 0, keepdims=True)
        db = jnp.sum(dy, 0, keepdims=True)

        @pl.when(pl.program_id(0) == 0)
        def _():
            dg_ref[...] = dg
            db_ref[...] = db

        @pl.when(pl.program_id(0) > 0)
        def _():
            dg_ref[...] += dg
            db_ref[...] += db

    row = pl.BlockSpec((tm, d), lambda i: (i, 0))
    vec = pl.BlockSpec((1, d), lambda i: (0, 0))
    dr, dg, db = pl.pallas_call(
        kern, name=name, grid=(s // tm,), in_specs=[row, row, row, vec], out_specs=[row, vec, vec],
        out_shape=[jax.ShapeDtypeStruct((s, d), F32), jax.ShapeDtypeStruct((1, d), F32), jax.ShapeDtypeStruct((1, d), F32)],
        compiler_params=_cp(("arbitrary",)),
    )(dya, dyb, r, g.reshape(1, d))
    return dr, dg[0], db[0]


def _sqerr(y, t, name, tm=256):
    s, d = y.shape
    tm = min(tm, s)

    def kern(y_ref, t_ref, o_ref):
        e = y_ref[...] - t_ref[...]
        p = jnp.sum(e * e, 0, keepdims=True)

        @pl.when(pl.program_id(0) == 0)
        def _():
            o_ref[...] = p

        @pl.when(pl.program_id(0) > 0)
        def _():
            o_ref[...] += p

    row = pl.BlockSpec((tm, d), lambda i: (i, 0))
    return pl.pallas_call(
        kern, name=name, grid=(s // tm,), in_specs=[row, row], out_specs=pl.BlockSpec((1, d), lambda i: (0, 0)),
        out_shape=jax.ShapeDtypeStruct((1, d), F32), compiler_params=_cp(("arbitrary",)),
    )(y, t)


def _axpby(a, b, ca, cb, name, tm=256):
    s, d = a.shape
    tm = min(tm, s)

    def kern(a_ref, b_ref, o_ref):
        o_ref[...] = ca * a_ref[...] + cb * b_ref[...]

    row = pl.BlockSpec((tm, d), lambda i: (i, 0))
    return pl.pallas_call(
        kern, name=name, grid=(s // tm,), in_specs=[row, row], out_specs=row,
        out_shape=jax.ShapeDtypeStruct((s, d), F32), compiler_params=_cp(("parallel",)),
    )(a, b)


def _shift_down(xx, s, n):
    if s == 0:
        return xx[HALO:HALO + n]
    return pltpu.roll(xx, s, axis=0)[HALO:HALO + n]


def _shift_up(yy, s, n):
    if s == 0:
        return yy[0:n]
    return pltpu.roll(yy, yy.shape[0] - s, axis=0)[0:n]


def _prev_halo(tm):
    return lambda i: jnp.maximum(i * (tm // HALO) - 1, 0)


def _dwconv_fwd(x, x_cb0, c, w8, b, k, name, tm=1024):
    s = x.shape[0]
    tm = min(tm, s)
    ph = _prev_halo(tm)

    def kern(x_ref, h_ref, w_ref, b_ref, y_ref):
        halo = jnp.where(pl.program_id(0) == 0, 0.0, h_ref[...].astype(F32))
        xx = jnp.concatenate([halo, x_ref[...].astype(F32)], axis=0)
        w = w_ref[...]
        acc = b_ref[...] + w[k - 1:k] * xx[HALO:]
        for j in range(k - 1):
            acc = acc + w[j:j + 1] * _shift_down(xx, k - 1 - j, tm)
        y_ref[...] = acc

    return pl.pallas_call(
        kern, name=name, grid=(s // tm, c // CONV_CB),
        in_specs=[pl.BlockSpec((tm, CONV_CB), lambda i, j: (i, x_cb0 + j)),
                  pl.BlockSpec((HALO, CONV_CB), lambda i, j: (ph(i), x_cb0 + j)),
                  pl.BlockSpec((SUBLANES, CONV_CB), lambda i, j: (0, j)),
                  pl.BlockSpec((1, CONV_CB), lambda i, j: (0, j))],
        out_specs=pl.BlockSpec((tm, CONV_CB), lambda i, j: (i, j)),
        out_shape=jax.ShapeDtypeStruct((s, c), F32), compiler_params=_cp(("parallel", "parallel")),
    )(x, x, w8, b.reshape(1, c))


def _dwconv_bwd(dy, x, x_cb0, c, w8, k, name, tm=1024):
    s = x.shape[0]
    tm = min(tm, s)
    nt = s // tm
    ph = _prev_halo(tm)
    nh = lambda i: jnp.minimum((i + 1) * (tm // HALO), s // HALO - 1)

    def kern(dy_ref, dyn_ref, x_ref, h_ref, w_ref, dx_ref, dw_ref, db_ref):
        i = pl.program_id(1)
        halo = jnp.where(i == 0, 0.0, h_ref[...].astype(F32))
        xx = jnp.concatenate([halo, x_ref[...].astype(F32)], axis=0)
        dyt = dy_ref[...]
        nxt = jnp.where(i == nt - 1, 0.0, dyn_ref[...])
        dyy = jnp.concatenate([dyt, nxt], axis=0)
        w = w_ref[...]
        dx = w[k - 1:k] * dyt
        rows = [jnp.sum(dyt * _shift_down(xx, k - 1 - j, tm), 0, keepdims=True) for j in range(k)]
        for j in range(k - 1):
            dx = dx + w[j:j + 1] * _shift_up(dyy, k - 1 - j, tm)
        dx_ref[...] = dx.astype(BF16)
        dw = jnp.concatenate(rows + [jnp.zeros((SUBLANES - k, CONV_CB), F32)], axis=0)
        db = jnp.sum(dyt, 0, keepdims=True)

        @pl.when(i == 0)
        def _():
            dw_ref[...] = dw
            db_ref[...] = db

        @pl.when(i > 0)
        def _():
            dw_ref[...] += dw
            db_ref[...] += db

    dx, dw, db = pl.pallas_call(
        kern, name=name, grid=(c // CONV_CB, nt),
        in_specs=[pl.BlockSpec((tm, CONV_CB), lambda j, i: (i, j)),
                  pl.BlockSpec((HALO, CONV_CB), lambda j, i: (nh(i), j)),
                  pl.BlockSpec((tm, CONV_CB), lambda j, i: (i, x_cb0 + j)),
                  pl.BlockSpec((HALO, CONV_CB), lambda j, i: (ph(i), x_cb0 + j)),
                  pl.BlockSpec((SUBLANES, CONV_CB), lambda j, i: (0, j))],
        out_specs=[pl.BlockSpec((tm, CONV_CB), lambda j, i: (i, j)),
                   pl.BlockSpec((SUBLANES, CONV_CB), lambda j, i: (0, j)),
                   pl.BlockSpec((1, CONV_CB), lambda j, i: (0, j))],
        out_shape=[jax.ShapeDtypeStruct((s, c), BF16), jax.ShapeDtypeStruct((SUBLANES, c), F32),
                   jax.ShapeDtypeStruct((1, c), F32)],
        compiler_params=_cp(("parallel", "arbitrary")),
    )(dy, dy, x, x, w8)
    return dx, dw, db[0]


def _silu(x):
    return x * jax.nn.sigmoid(x)


def _ffn_gate_fwd(hu, w8, b, name, tm=2048):
    s = hu.shape[0]
    tm = min(tm, s)
    nb = FFN_DIM // CONV_CB
    ph = _prev_halo(tm)
    k = 3

    def conv(x_ref, h_ref, w_ref, b_ref):
        halo = jnp.where(pl.program_id(0) == 0, 0.0, h_ref[...].astype(F32))
        xx = jnp.concatenate([halo, x_ref[...].astype(F32)], axis=0)
        w = w_ref[...]
        acc = b_ref[...] + w[k - 1:k] * xx[HALO:]
        for j in range(k - 1):
            acc = acc + w[j:j + 1] * _shift_down(xx, k - 1 - j, tm)
        return acc

    def kern(g_ref, gh_ref, u_ref, uh_ref, wg_ref, wu_ref, bg_ref, bu_ref, a_ref, hg_ref, hu_ref):
        g = conv(g_ref, gh_ref, wg_ref, bg_ref)
        u = conv(u_ref, uh_ref, wu_ref, bu_ref)
        a_ref[...] = (_silu(g) * u).astype(BF16)
        hg_ref[...] = g.astype(BF16)
        hu_ref[...] = u.astype(BF16)

    main = lambda off: pl.BlockSpec((tm, CONV_CB), lambda i, j: (i, off + j))
    halo = lambda off: pl.BlockSpec((HALO, CONV_CB), lambda i, j: (ph(i), off + j))
    wsp = lambda off: pl.BlockSpec((SUBLANES, CONV_CB), lambda i, j: (0, off + j))
    bsp = lambda off: pl.BlockSpec((1, CONV_CB), lambda i, j: (0, off + j))
    b2 = b.reshape(1, 2 * FFN_DIM)
    half = jax.ShapeDtypeStruct((s, FFN_DIM), BF16)
    return pl.pallas_call(
        kern, name=name, grid=(s // tm, nb),
        in_specs=[main(0), halo(0), main(nb), halo(nb), wsp(0), wsp(nb), bsp(0), bsp(nb)],
        out_specs=[main(0), main(0), main(0)], out_shape=[half, half, half],
        compiler_params=_cp(("parallel", "parallel")),
    )(hu, hu, hu, hu, w8, w8, b2, b2)


def _ffn_gate_bwd(hu, hg, hv, da, w8, name, tm=1024):
    s = hu.shape[0]
    tm = min(tm, s)
    nt = s // tm
    nb = FFN_DIM // CONV_CB
    nh = lambda i: jnp.minimum((i + 1) * (tm // HALO), s // HALO - 1)
    k = 3

    def kern(xg_ref, xu_ref, g_ref, gn_ref, u_ref, un_ref, da_ref, dan_ref, wg_ref, wu_ref,
             dg_ref, du_ref, dwg_ref, dwu_ref, dbg_ref, dbu_ref):
        i = pl.program_id(1)

        def ext(x_ref, n_ref):
            return jnp.concatenate([x_ref[...].astype(F32), n_ref[...].astype(F32)], axis=0)

        g, u = ext(g_ref, gn_ref), ext(u_ref, un_ref)
        dae = jnp.concatenate([da_ref[...].astype(F32), jnp.where(i == nt - 1, 0.0, dan_ref[...].astype(F32))], axis=0)
        sg = jax.nn.sigmoid(g)
        dhg = dae * u * (sg * (1.0 + g * (1.0 - sg)))
        dhu = dae * (g * sg)

        def back(dh, x, w):
            taps = [dh[0:tm] if j == k - 1 else _shift_up(dh, k - 1 - j, tm) for j in range(k)]
            dx = w[0:1] * taps[0]
            for j in range(1, k):
                dx = dx + w[j:j + 1] * taps[j]
            rows = [jnp.sum(taps[j] * x, 0, keepdims=True) for j in range(k)]
            dw = jnp.concatenate(rows + [jnp.zeros((SUBLANES - k, CONV_CB), F32)], axis=0)
            return dx, dw, jnp.sum(dh[0:tm], 0, keepdims=True)

        dxg, dwg, dbg = back(dhg, xg_ref[...].astype(F32), wg_ref[...])
        dxu, dwu, dbu = back(dhu, xu_ref[...].astype(F32), wu_ref[...])
        dg_ref[...] = dxg.astype(BF16)
        du_ref[...] = dxu.astype(BF16)

        @pl.when(i == 0)
        def _():
            dwg_ref[...] = dwg
            dwu_ref[...] = dwu
            dbg_ref[...] = dbg
            dbu_ref[...] = dbu

        @pl.when(i > 0)
        def _():
            dwg_ref[...] += dwg
            dwu_ref[...] += dwu
            dbg_ref[...] += dbg
            dbu_ref[...] += dbu

    main = lambda off: pl.BlockSpec((tm, CONV_CB), lambda j, i: (i, off + j))
    nxt = lambda off: pl.BlockSpec((HALO, CONV_CB), lambda j, i: (nh(i), off + j))
    wsp = lambda off: pl.BlockSpec((SUBLANES, CONV_CB), lambda j, i: (0, off + j))
    bsp = lambda off: pl.BlockSpec((1, CONV_CB), lambda j, i: (0, off + j))
    outs = pl.pallas_call(
        kern, name=name, grid=(nb, nt),
        in_specs=[main(0), main(nb), main(0), nxt(0), main(0), nxt(0), main(0), nxt(0), wsp(0), wsp(nb)],
        out_specs=[main(0), main(0), wsp(0), wsp(0), bsp(0), bsp(0)],
        out_shape=[jax.ShapeDtypeStruct((s, FFN_DIM), BF16), jax.ShapeDtypeStruct((s, FFN_DIM), BF16),
                   jax.ShapeDtypeStruct((SUBLANES, FFN_DIM), F32), jax.ShapeDtypeStruct((SUBLANES, FFN_DIM), F32),
                   jax.ShapeDtypeStruct((1, FFN_DIM), F32), jax.ShapeDtypeStruct((1, FFN_DIM), F32)],
        compiler_params=_cp(("parallel", "arbitrary")),
    )(hu, hu, hg, hg, hv, hv, da, da, w8, w8)
    dg, du, dwg, dwu, dbg, dbu = outs
    return dg, du, jnp.concatenate([dwg, dwu], axis=1), jnp.concatenate([dbg[0], dbu[0]])


def _iota(shape, dim):
    return lax.broadcasted_iota(jnp.int32, shape, dim)


def _in_range(idx, start, size):
    return jnp.logical_and(idx >= start, idx < start + size)


@functools.partial(jax.custom_vjp, nondiff_argnums=(2, 3))
def _zdot(a, b, dims, z):
    zz, x = ((a, b) if z == 0 else (b, a))
    zz = zz.astype(BF16)
    xh = x.astype(BF16)
    r1 = x - xh.astype(F32)
    xm = r1.astype(BF16)
    xl = (r1 - xm.astype(F32)).astype(BF16)
    f = (lambda p: _dot(zz, p, dims)) if z == 0 else (lambda p: _dot(p, zz, dims))
    return f(xh) + f(xm) + f(xl)


def _zdot_fwd(a, b, dims, z):
    return _zdot(a, b, dims, z), (a if z == 0 else b)


def _zdot_bwd(dims, z, zz, ct):
    zb = zz.astype(BF16)
    ch = ct.astype(BF16)
    cm = (ct - ch.astype(F32)).astype(BF16)
    if dims == NN and z == 0:
        f = lambda c: _dot(zb, c, TN)
    elif dims == NN and z == 1:
        f = lambda c: _dot(c, zb, NT)
    else:
        f = lambda c: _dot(c, zb, TN)
    dx = f(ch) + f(cm)
    return (jnp.zeros_like(zz), dx) if z == 0 else (dx, jnp.zeros_like(zz))


_zdot.defvjp(_zdot_fwd, _zdot_bwd)


def _expand8(v, e):
    return jnp.sum(_zdot(jnp.broadcast_to(v, (SUBLANES, v.shape[1])), e, NN, 1), 0, keepdims=True) * (1.0 / SUBLANES)


def _ssd_consts():
    tri = np.tril(np.ones((SSD_CHUNK, SSD_CHUNK), np.float32))
    e = np.zeros((LANES, SSD_D_INNER), np.float32)
    for h in range(SSD_HEADS):
        e[h, 64 * h:64 * h + 64] = 1.0
    bd = np.zeros((LANES, SSD_D_INNER), np.float32)
    bd[:64, :256] = 1.0
    bd[64:, 256:] = 1.0
    return jnp.asarray(tri), jnp.asarray(e), jnp.asarray(bd)


def _ssd_chunk(tri, e, bd, z, xs_p, bm_p, cm_p, dtr, st, dtb, alog, dsk, nw):
    n = z.shape[0]
    lane128, lane512 = _iota((1, LANES), 1), _iota((1, SSD_D_INNER), 1)
    sub8 = _iota((SUBLANES, 1), 0)
    tril = _iota((n, 1), 0) >= _iota((1, n), 1)
    head_rows = jnp.where(sub8 == lane128, 1.0, 0.0)
    xs, bm, cm = _silu(xs_p), _silu(bm_p), _silu(cm_p)
    dt = jax.nn.softplus(dtr + dtb)
    da = dt * (-jnp.exp(alog))
    acs = _zdot(tri, da, NN, 0)
    acst = _zdot(head_rows, acs, NT, 0)
    xc = xs * _zdot(dt, e, NN, 1)
    tot = jnp.sum(da, 0, keepdims=True)
    y = _bdot(cm, st) * _zdot(jnp.exp(acs), e, NN, 1)
    st_new = st * _expand8(jnp.exp(tot), e) + bd * _bdot(bm, xc * _zdot(jnp.exp(tot - acs), e, NN, 1), TN)
    for g in range(2):
        cb = _bdot(jnp.where(_in_range(lane128, 64 * g, 64), cm, 0.0), bm, NT)
        for h in range(4 * g, 4 * g + 4):
            col = jnp.sum(jnp.where(lane128 == h, acs, 0.0), 1, keepdims=True)
            row = jnp.sum(jnp.where(sub8 == h, acst, 0.0), 0, keepdims=True)
            dm = jnp.where(tril, jnp.exp(jnp.minimum(col - row, 0.0)), 0.0)
            y = y + _bdot(cb * dm, jnp.where(_in_range(lane512, 64 * h, 64), xc, 0.0))
    y = (y + _expand8(dsk, e) * xs) * _silu(z)
    ysq = y * y
    g0 = lane512 < 256
    ms0 = jnp.sum(jnp.where(g0, ysq, 0.0), 1, keepdims=True) * (1.0 / 256)
    ms1 = jnp.sum(jnp.where(g0, 0.0, ysq), 1, keepdims=True) * (1.0 / 256)
    rs = jnp.where(g0, lax.rsqrt(ms0 + RMS_EPS), lax.rsqrt(ms1 + RMS_EPS))
    return y * rs * nw, st_new


SSD_STEP = 2


def _ssd_rows(u):
    return slice(SSD_CHUNK * u, SSD_CHUNK * (u + 1))


def _ssd_specs(rev, ng):
    ci = (lambda i: ng - 1 - i) if rev else (lambda i: i)
    n = SSD_STEP * SSD_CHUNK
    act = [pl.BlockSpec((n, 512), lambda i: (ci(i), AB_Z // 512)),
           pl.BlockSpec((n, 512), lambda i: (ci(i), 0)),
           pl.BlockSpec((n, LANES), lambda i: (ci(i), 4)),
           pl.BlockSpec((n, LANES), lambda i: (ci(i), 5)),
           pl.BlockSpec((n, LANES), lambda i: (ci(i), AB_DT // LANES))]
    const = [pl.BlockSpec((SSD_CHUNK, SSD_CHUNK), lambda i: (0, 0)), pl.BlockSpec((LANES, 512), lambda i: (0, 0)),
             pl.BlockSpec((LANES, 512), lambda i: (0, 0))]
    par = [pl.BlockSpec((1, LANES), lambda i: (0, 0))] * 3 + [pl.BlockSpec((1, 512), lambda i: (0, 0))]
    return ci, act, const, par


def _ssd_fwd(h, xbcc, dtb, alog, dsk, nw, name):
    s = h.shape[0]
    nc = s // SSD_CHUNK
    ng = nc // SSD_STEP
    ci, act, const, par = _ssd_specs(False, ng)

    def kern(z_ref, xs_ref, bm_ref, cm_ref, dt_ref, tri_ref, e_ref, bd_ref, dtb_ref, al_ref, dsk_ref, nw_ref,
             y_ref, sts_ref, st):
        @pl.when(pl.program_id(0) == 0)
        def _():
            st[...] = jnp.zeros_like(st)

        for u in range(SSD_STEP):
            rows = _ssd_rows(u)
            sts_ref[u] = st[...]
            y, stn = _ssd_chunk(tri_ref[...], e_ref[...], bd_ref[...], z_ref[rows, :], xs_ref[rows, :], bm_ref[rows, :],
                                cm_ref[rows, :], dt_ref[rows, :], st[...], dtb_ref[...], al_ref[...], dsk_ref[...], nw_ref[...])
            y_ref[rows, :] = y.astype(BF16)
            st[...] = stn

    return pl.pallas_call(
        kern, name=name, grid=(ng,), in_specs=act + const + par,
        out_specs=[pl.BlockSpec((SSD_STEP * SSD_CHUNK, 512), lambda i: (i, 0)),
                   pl.BlockSpec((SSD_STEP, LANES, 512), lambda i: (i, 0, 0))],
        out_shape=[jax.ShapeDtypeStruct((s, 512), BF16), jax.ShapeDtypeStruct((nc, LANES, 512), F32)],
        scratch_shapes=[pltpu.VMEM((LANES, 512), F32)], compiler_params=_cp(("arbitrary",)),
    )(h, xbcc, xbcc, xbcc, h, *_ssd_consts(), dtb, alog, dsk, nw)


def _acc(ref, val, first):
    @pl.when(first)
    def _():
        ref[...] = val

    @pl.when(jnp.logical_not(first))
    def _():
        ref[...] += val


def _ssd_bwd(dy, dy_cb, h, xbcc, sts, dtb, alog, dsk, nw, name):
    s = h.shape[0]
    ng = s // SSD_CHUNK // SSD_STEP
    n = SSD_STEP * SSD_CHUNK
    ci, act, const, par = _ssd_specs(True, ng)

    def kern(dy_ref, sts_ref, z_ref, xs_ref, bm_ref, cm_ref, dt_ref, tri_ref, e_ref, bd_ref, dtb_ref, al_ref, dsk_ref,
             nw_ref, dz_ref, dx_ref, ddt_ref, ddtb_ref, dal_ref, ddsk_ref, dnw_ref, dst):
        @pl.when(pl.program_id(0) == 0)
        def _():
            dst[...] = jnp.zeros_like(dst)
            for ref in (ddtb_ref, dal_ref, ddsk_ref, dnw_ref):
                ref[...] = jnp.zeros_like(ref)

        f = functools.partial(_ssd_chunk, tri_ref[...], e_ref[...], bd_ref[...])
        for u in range(SSD_STEP - 1, -1, -1):
            rows = _ssd_rows(u)
            _, pull = jax.vjp(f, z_ref[rows, :], xs_ref[rows, :], bm_ref[rows, :], cm_ref[rows, :], dt_ref[rows, :],
                              sts_ref[u], dtb_ref[...], al_ref[...], dsk_ref[...], nw_ref[...])
            dz, dxs, dbm, dcm, ddt, dsti, ddtb, dal, ddsk, dnw = pull((dy_ref[rows, :], dst[...]))
            dz_ref[rows, :] = dz.astype(BF16)
            dx_ref[rows, 0:512] = dxs
            dx_ref[rows, 512:640] = dbm
            dx_ref[rows, 640:768] = dcm
            ddt_ref[rows, :] = ddt.astype(BF16)
            dst[...] = dsti
            ddtb_ref[...] += ddtb
            dal_ref[...] += dal
            ddsk_ref[...] += ddsk
            dnw_ref[...] += dnw

    vec = pl.BlockSpec((1, LANES), lambda i: (0, 0))
    return pl.pallas_call(
        kern, name=name, grid=(ng,),
        in_specs=[pl.BlockSpec((n, 512), lambda i: (ci(i), dy_cb)), pl.BlockSpec((SSD_STEP, LANES, 512), lambda i: (ci(i), 0, 0))]
        + act + const + par,
        out_specs=[pl.BlockSpec((n, 512), lambda i: (ci(i), 0)), pl.BlockSpec((n, SSD_CONV_DIM), lambda i: (ci(i), 0)),
                   pl.BlockSpec((n, LANES), lambda i: (ci(i), 0)), vec, vec, vec, pl.BlockSpec((1, 512), lambda i: (0, 0))],
        out_shape=[jax.ShapeDtypeStruct((s, 512), BF16), jax.ShapeDtypeStruct((s, SSD_CONV_DIM), F32),
                   jax.ShapeDtypeStruct((s, LANES), BF16), jax.ShapeDtypeStruct((1, LANES), F32),
                   jax.ShapeDtypeStruct((1, LANES), F32), jax.ShapeDtypeStruct((1, LANES), F32),
                   jax.ShapeDtypeStruct((1, 512), F32)],
        scratch_shapes=[pltpu.VMEM((LANES, 512), F32)], compiler_params=_cp(("arbitrary",)),
    )(dy, sts, h, xbcc, xbcc, xbcc, h, *_ssd_consts(), dtb, alog, dsk, nw)


HG_LEVELS = 6


HG_SUMS = 1 + 2 * HG_LEVELS


def _hg_consts():
    n = HG_CHUNK
    t = np.arange(n)
    mats = [np.tril(np.ones((n, n), np.float32))]
    dq, dk, mk = [], [], []
    for lv in range(HG_LEVELS):
        b = 1 << lv
        start = (t // b) * b
        end = start + b - 1
        dq.append(((t[None, :] >= start[:, None]) & (t[None, :] <= t[:, None])).astype(np.float32))
        dk.append(((t[None, :] > t[:, None]) & (t[None, :] <= end[:, None])).astype(np.float32))
        mk.append((((t // b) % 2 == 1)[:, None] & ((t // b)[None, :] == (t // b)[:, None] - 1)).astype(np.float32))
    c1 = np.concatenate(mats + dq + dk, axis=0)
    return (jnp.asarray(c1, BF16), jnp.asarray(np.concatenate([c1, c1, c1], axis=1), BF16), jnp.asarray(np.stack(mk)))


@functools.partial(jax.custom_vjp, nondiff_argnums=(1, 2))
def _split_grid(x, nr, nc):
    br, bc = x.shape[0] // nr, x.shape[1] // nc
    return tuple(x[i * br:(i + 1) * br, j * bc:(j + 1) * bc] for i in range(nr) for j in range(nc))


def _split_grid_fwd(x, nr, nc):
    return _split_grid(x, nr, nc), None


def _split_grid_bwd(nr, nc, _, cts):
    rows = [cts[i * nc] if nc == 1 else jnp.concatenate(cts[i * nc:(i + 1) * nc], axis=1) for i in range(nr)]
    return (rows[0] if nr == 1 else jnp.concatenate(rows, axis=0),)


_split_grid.defvjp(_split_grid_fwd, _split_grid_bwd)


@jax.custom_vjp
def _level_sums(c1, c3, g):
    gh = g.astype(BF16)
    r1 = g - gh.astype(F32)
    gm = r1.astype(BF16)
    gl = (r1 - gm.astype(F32)).astype(BF16)
    return _dot(c3, jnp.concatenate([gh, gm, gl], axis=0), NN)


def _level_sums_fwd(c1, c3, g):
    return _level_sums(c1, c3, g), (c1, c3)


def _level_sums_bwd(res, ct):
    c1, c3 = res
    ch = ct.astype(BF16)
    cm = (ct - ch.astype(F32)).astype(BF16)
    return jnp.zeros_like(c1), jnp.zeros_like(c3), _dot(c1, ch, TN) + _dot(c1, cm, TN)


_level_sums.defvjp(_level_sums_fwd, _level_sums_bwd)


def _hg_chunk(c1, c3, masks, hq, hf, hi, hg, sts, lb, nw):
    q = _silu(hq)
    g = jnp.log(lb + (1.0 - lb) * jax.nn.sigmoid(hf))
    k = (1.0 - lb) * jax.nn.sigmoid(-hf)
    parts = _split_grid(_level_sums(c1, c3, g), HG_SUMS, 1)
    bc = parts[0]
    tot = jnp.sum(g, 0, keepdims=True)
    qe = _split_grid(q * jnp.exp(bc), 1, HG_HEADS)
    kd = _split_grid(k * jnp.exp(tot - bc), 1, HG_HEADS)
    etot = _split_grid(jnp.exp(tot), 1, HG_HEADS)
    qh, kh, vh, gh = [_split_grid(a, 1, HG_HEADS) for a in (q, k, hi, hg)]
    att = [None] * HG_HEADS
    for lv in range(HG_LEVELS):
        qt = _split_grid(q * jnp.exp(parts[1 + lv]), 1, HG_HEADS)
        kt = _split_grid(k * jnp.exp(parts[1 + HG_LEVELS + lv]), 1, HG_HEADS)
        for hd in range(HG_HEADS):
            t = masks[lv] * _bdot(qt[hd], kt[hd], NT)
            att[hd] = t if att[hd] is None else att[hd] + t
    outs, stn = [], []
    for hd in range(HG_HEADS):
        o = (_bdot(att[hd], vh[hd]) + jnp.sum(qh[hd] * kh[hd], 1, keepdims=True) * vh[hd] + _bdot(qe[hd], sts[hd], NT))
        stn.append(sts[hd] * etot[hd] + _bdot(vh[hd], kd[hd], TN))
        outs.append(o * lax.rsqrt(jnp.mean(o * o, 1, keepdims=True) + RMS_EPS) * nw * _silu(gh[hd]))
    return tuple(outs), tuple(stn)


HG_STEP = 4


def _hg_specs(rev, ng):
    ci = (lambda i: ng - 1 - i) if rev else (lambda i: i)
    n = HG_CHUNK
    act = [pl.BlockSpec((HG_STEP * n, 512), functools.partial(lambda i, c: (ci(i), c), c=c))
           for c in (AB_HQ // 512, AB_HF // 512, AB_HI // 512, AB_HG // 512)]
    other = [pl.BlockSpec((HG_SUMS * n, n), lambda i: (0, 0)), pl.BlockSpec((HG_SUMS * n, 3 * n), lambda i: (0, 0)),
             pl.BlockSpec((HG_LEVELS, n, n), lambda i: (0, 0, 0)), pl.BlockSpec((1, 512), lambda i: (0, 0)),
             pl.BlockSpec((1, LANES), lambda i: (0, 0))]
    return ci, act, other


def _head_rows(hd):
    return slice(LANES * hd, LANES * (hd + 1))


def _chunk_rows(u):
    return slice(HG_CHUNK * u, HG_CHUNK * (u + 1))


def _hg_fwd(h, lb, nw, name):
    s = h.shape[0]
    nc = s // HG_CHUNK
    ng = nc // HG_STEP
    ci, act, other = _hg_specs(False, ng)

    def kern(q_ref, f_ref, i_ref, g_ref, c1_ref, c3_ref, mk_ref, lb_ref, nw_ref, y_ref, sts_ref, st):
        @pl.when(pl.program_id(0) == 0)
        def _():
            st[...] = jnp.zeros_like(st)

        masks = [mk_ref[lv] for lv in range(HG_LEVELS)]
        for u in range(HG_STEP):
            rows = _chunk_rows(u)
            sts_ref[u] = st[...]
            outs, stn = _hg_chunk(c1_ref[...], c3_ref[...], masks, q_ref[rows, :], f_ref[rows, :], i_ref[rows, :],
                                  g_ref[rows, :], tuple(st[_head_rows(hd), :] for hd in range(HG_HEADS)), lb_ref[...],
                                  nw_ref[...])
            for hd in range(HG_HEADS):
                y_ref[rows, _head_rows(hd)] = outs[hd].astype(BF16)
                st[_head_rows(hd), :] = stn[hd]

    return pl.pallas_call(
        kern, name=name, grid=(ng,), in_specs=act + other,
        out_specs=[pl.BlockSpec((HG_STEP * HG_CHUNK, 512), lambda i: (i, 0)),
                   pl.BlockSpec((HG_STEP, 512, LANES), lambda i: (i, 0, 0))],
        out_shape=[jax.ShapeDtypeStruct((s, 512), BF16), jax.ShapeDtypeStruct((nc, 512, LANES), F32)],
        scratch_shapes=[pltpu.VMEM((512, LANES), F32)], compiler_params=_cp(("arbitrary",)),
    )(h, h, h, h, *_hg_consts(), lb, nw)


def _hg_bwd(dy, dy_cb, h, sts, lb, nw, name):
    s = h.shape[0]
    ng = s // HG_CHUNK // HG_STEP
    n = HG_STEP * HG_CHUNK
    ci, act, other = _hg_specs(True, ng)

    def kern(dy_ref, sts_ref, q_ref, f_ref, i_ref, g_ref, c1_ref, c3_ref, mk_ref, lb_ref, nw_ref,
             dq_ref, df_ref, di_ref, dg_ref, dlb_ref, dnw_ref, dst):
        @pl.when(pl.program_id(0) == 0)
        def _():
            dst[...] = jnp.zeros_like(dst)
            dlb_ref[...] = jnp.zeros_like(dlb_ref)
            dnw_ref[...] = jnp.zeros_like(dnw_ref)

        masks = [mk_ref[lv] for lv in range(HG_LEVELS)]
        heads = range(HG_HEADS)
        for u in range(HG_STEP - 1, -1, -1):
            rows = _chunk_rows(u)
            _, pull = jax.vjp(functools.partial(_hg_chunk, c1_ref[...], c3_ref[...], masks), q_ref[rows, :], f_ref[rows, :],
                              i_ref[rows, :], g_ref[rows, :], tuple(sts_ref[u, _head_rows(hd), :] for hd in heads),
                              lb_ref[...], nw_ref[...])
            dq, df, di, dg, dsti, dlb, dnw = pull((tuple(dy_ref[rows, _head_rows(hd)] for hd in heads),
                                                   tuple(dst[_head_rows(hd), :] for hd in heads)))
            dq_ref[rows, :] = dq.astype(BF16)
            df_ref[rows, :] = df.astype(BF16)
            di_ref[rows, :] = di.astype(BF16)
            dg_ref[rows, :] = dg.astype(BF16)
            for hd in heads:
                dst[_head_rows(hd), :] = dsti[hd]
            dlb_ref[...] += dlb
            dnw_ref[...] += dnw

    blk = pl.BlockSpec((n, 512), lambda i: (ci(i), 0))
    return pl.pallas_call(
        kern, name=name, grid=(ng,),
        in_specs=[pl.BlockSpec((n, 512), lambda i: (ci(i), dy_cb)), pl.BlockSpec((HG_STEP, 512, LANES), lambda i: (ci(i), 0, 0))]
        + act + other,
        out_specs=[blk, blk, blk, blk, pl.BlockSpec((1, 512), lambda i: (0, 0)), pl.BlockSpec((1, LANES), lambda i: (0, 0))],
        out_shape=[jax.ShapeDtypeStruct((s, 512), BF16)] * 4 + [jax.ShapeDtypeStruct((1, 512), F32),
                                                                jax.ShapeDtypeStruct((1, LANES), F32)],
        scratch_shapes=[pltpu.VMEM((512, LANES), F32)], compiler_params=_cp(("arbitrary",)),
    )(dy, sts, h, h, h, h, *_hg_consts(), lb, nw)


def _swa_consts():
    n = SWA_BLOCK
    r = np.arange(8 * n)
    t, blk = r % n, r // n
    lanes = np.arange(LANES)
    own = (t[:, None] >= lanes[None, :]).astype(np.float32)
    prev = (lanes[None, :] > t[:, None]).astype(np.float32)
    sel = (lanes[None, :] == (blk // 2 + 4 * (blk % 2))[:, None]).astype(np.float32)
    grp = ((lanes[None, :] // 64) == (blk % 2)[:, None]).astype(np.float32)
    return tuple(jnp.asarray(a) for a in (own, prev, sel, grp))


def _swa_block(own, prev, sel, grp, q, k0, v0, k1, v1, srow):
    qs = _split_grid(q, 1, 4)
    qa = jnp.concatenate([qs[p] for p in range(4) for _ in range(2)], axis=0) * grp
    s0 = jnp.where(prev > 0.0, _bdot(qa, k0, NT) * 0.125, MASK_VALUE)
    s1 = jnp.where(own > 0.0, _bdot(qa, k1, NT) * 0.125, MASK_VALUE)
    sk = jnp.sum(sel * srow, 1, keepdims=True)
    m = lax.stop_gradient(jnp.maximum(jnp.maximum(jnp.max(s0, 1, keepdims=True), jnp.max(s1, 1, keepdims=True)), sk))
    p0, p1 = jnp.exp(s0 - m), jnp.exp(s1 - m)
    inv = 1.0 / (jnp.sum(p0, 1, keepdims=True) + jnp.sum(p1, 1, keepdims=True) + jnp.exp(sk - m))
    o = _split_grid((_bdot(p0 * inv, v0) + _bdot(p1 * inv, v1)) * grp, 8, 1)
    return tuple(o[2 * p] + o[2 * p + 1] for p in range(4))


SWA_STEP = 2


def _swa_specs():
    n = SWA_BLOCK
    pv = lambda i: jnp.maximum(SWA_STEP * i - 1, 0)
    mask = pl.BlockSpec((8 * n, LANES), lambda i: (0, 0))
    return [mask, mask, mask, mask, pl.BlockSpec((SWA_STEP * n, 512), lambda i: (i, CD_Q // 512)),
            pl.BlockSpec((n, LANES), lambda i: (pv(i), CD_K // LANES)), pl.BlockSpec((n, LANES), lambda i: (pv(i), CD_V // LANES)),
            pl.BlockSpec((SWA_STEP * n, LANES), lambda i: (i, CD_K // LANES)),
            pl.BlockSpec((SWA_STEP * n, LANES), lambda i: (i, CD_V // LANES)), pl.BlockSpec((1, LANES), lambda i: (0, 0))]


def _swa_args(refs, u, notfirst):
    own, prev, sel, grp = (r[...] for r in refs[:4])
    q_ref, k0_ref, v0_ref, k_ref, v_ref, s_ref = refs[4:]
    rows = slice(SWA_BLOCK * u, SWA_BLOCK * (u + 1))
    if u == 0:
        k0, v0, prev = k0_ref[...], v0_ref[...], prev * notfirst.astype(F32)
    else:
        before = slice(SWA_BLOCK * (u - 1), SWA_BLOCK * u)
        k0, v0 = k_ref[before, :], v_ref[before, :]
    return (own, prev, sel, grp), (q_ref[rows, :], k0, v0, k_ref[rows, :], v_ref[rows, :], s_ref[...])


def _swa_fwd(hc, srow, name):
    s = hc.shape[0]
    n = SWA_STEP * SWA_BLOCK

    def kern(*refs):
        y_ref = refs[10]
        for u in range(SWA_STEP):
            consts, acts = _swa_args(refs[:10], u, pl.program_id(0) > 0)
            outs = _swa_block(*consts, *acts)
            for p in range(4):
                y_ref[SWA_BLOCK * u:SWA_BLOCK * (u + 1), LANES * p:LANES * (p + 1)] = outs[p].astype(BF16)

    return pl.pallas_call(
        kern, name=name, grid=(s // n,), in_specs=_swa_specs(), out_specs=pl.BlockSpec((n, 512), lambda i: (i, 0)),
        out_shape=jax.ShapeDtypeStruct((s, 512), BF16), compiler_params=_cp(("parallel",)),
    )(*_swa_consts(), hc, hc, hc, hc, hc, srow)


def _swa_bwd(dy, dy_cb, hc, srow, name):
    s = hc.shape[0]
    n = SWA_STEP * SWA_BLOCK
    nb = SWA_BLOCK

    def kern(dy_ref, *refs):
        dq_ref, dk_ref, dv_ref, ds_ref = refs[10:]
        i = pl.program_id(0)

        @pl.when(i == 0)
        def _():
            dk_ref[...] = jnp.zeros_like(dk_ref)
            dv_ref[...] = jnp.zeros_like(dv_ref)
            ds_ref[...] = jnp.zeros_like(ds_ref)

        for u in range(SWA_STEP):
            blk = SWA_STEP * i + u
            rows = slice(nb * u, nb * (u + 1))
            r1 = pl.ds(pl.multiple_of(blk * nb, nb), nb)
            r0 = pl.ds(pl.multiple_of(jnp.maximum(blk - 1, 0) * nb, nb), nb)
            consts, acts = _swa_args(refs[:10], u, i > 0)
            _, pull = jax.vjp(functools.partial(_swa_block, *consts), *acts)
            dq, dk0, dv0, dk1, dv1, dsr = pull(tuple(dy_ref[rows, LANES * p:LANES * (p + 1)] for p in range(4)))
            dq_ref[rows, :] = dq.astype(BF16)
            dk_ref[r0, :] += dk0
            dv_ref[r0, :] += dv0
            dk_ref[r1, :] += dk1
            dv_ref[r1, :] += dv1
            ds_ref[...] += dsr

    full = pl.BlockSpec((s, LANES), lambda i: (0, 0))
    return pl.pallas_call(
        kern, name=name, grid=(s // n,), in_specs=[pl.BlockSpec((n, 512), lambda i: (i, dy_cb))] + _swa_specs(),
        out_specs=[pl.BlockSpec((n, 512), lambda i: (i, 0)), full, full, pl.BlockSpec((1, LANES), lambda i: (0, 0))],
        out_shape=[jax.ShapeDtypeStruct((s, 512), BF16), jax.ShapeDtypeStruct((s, LANES), F32),
                   jax.ShapeDtypeStruct((s, LANES), F32), jax.ShapeDtypeStruct((1, LANES), F32)],
        compiler_params=_cp(("arbitrary",)),
    )(dy, *_swa_consts(), hc, hc, hc, hc, hc, srow)


RG_TILE = 512


def _neg_expm1(x):
    ser = x * (1 + x / 2 * (1 + x / 3 * (1 + x / 4 * (1 + x / 5 * (1 + x / 6 * (1 + x / 7 * (1 + x / 8)))))))
    return jnp.where(x > -0.25, -ser, 1.0 - jnp.exp(x))


def _rg_gates(xc, wa, wx, ba, bx, lam):
    r = jax.nn.sigmoid(_bdot(xc, wa) + ba)
    i = jax.nn.sigmoid(_bdot(xc, wx) + bx)
    log_a = -RG_C * r * jax.nn.softplus(-lam)
    u = jnp.sqrt(jnp.maximum(_neg_expm1(2.0 * log_a), 0.0)) * (i * xc)
    return jnp.exp(log_a), u


def _rg_out(hs, gate):
    return hs * jax.nn.gelu(gate)


def _rows_to_tile(rows):
    sub = _iota((SUBLANES, 1), 0)
    out = jnp.broadcast_to(rows[0], (SUBLANES, rows[0].shape[1]))
    for j in range(1, SUBLANES):
        out = jnp.where(sub == j, rows[j], out)
    return out


def _rg_fwd(xc, hc, wa, wx, ba, bx, lam, name):
    s = xc.shape[0]
    tm = min(RG_TILE, s)

    def kern(x_ref, g_ref, wa_ref, wx_ref, ba_ref, bx_ref, lam_ref, y_ref, h_ref, a_s, u_s, carry):
        @pl.when(pl.program_id(0) == 0)
        def _():
            carry[...] = jnp.zeros_like(carry)

        a, u = _rg_gates(x_ref[...], wa_ref[...], wx_ref[...], ba_ref[...], bx_ref[...], lam_ref[...])
        a_s[...] = a
        u_s[...] = u

        def body(gi, hp):
            r0 = pl.multiple_of(gi * SUBLANES, SUBLANES)
            ab, ub = a_s[pl.ds(r0, SUBLANES), :], u_s[pl.ds(r0, SUBLANES), :]
            rows = []
            for j in range(SUBLANES):
                hp = ab[j:j + 1] * hp + ub[j:j + 1]
                rows.append(hp)
            h_ref[pl.ds(r0, SUBLANES), :] = _rows_to_tile(rows)
            return hp

        carry[0:1, :] = lax.fori_loop(0, tm // SUBLANES, body, carry[0:1, :])
        y_ref[...] = _rg_out(h_ref[...], g_ref[...]).astype(BF16)

    row = pl.BlockSpec((tm, 512), lambda i: (i, 0))
    mat = pl.BlockSpec((512, 512), lambda i: (0, 0))
    vec = pl.BlockSpec((1, 512), lambda i: (0, 0))
    return pl.pallas_call(
        kern, name=name, grid=(s // tm,),
        in_specs=[row, pl.BlockSpec((tm, 512), lambda i: (i, CD_GATE // 512)), mat, mat, vec, vec, vec],
        out_specs=[row, row],
        out_shape=[jax.ShapeDtypeStruct((s, 512), BF16), jax.ShapeDtypeStruct((s, 512), F32)],
        scratch_shapes=[pltpu.VMEM((tm, 512), F32), pltpu.VMEM((tm, 512), F32), pltpu.VMEM((SUBLANES, 512), F32)],
        compiler_params=_cp(("arbitrary",)),
    )(xc, hc, wa, wx, ba, bx, lam)


def _rg_bwd(dy, dy_cb, xc, hc, hs, wa, wx, ba, bx, lam, name):
    s = xc.shape[0]
    tm = min(RG_TILE, s)
    nt = s // tm
    ti = lambda i: nt - 1 - i

    def kern(dy_ref, x_ref, g_ref, h_ref, hp_ref, wa_ref, wx_ref, ba_ref, bx_ref, lam_ref,
             dx_ref, dg_ref, dwa_ref, dwx_ref, dba_ref, dbx_ref, dlam_ref, a_s, d_s, g_s, cg, ca):
        first = pl.program_id(0) == 0

        @pl.when(first)
        def _():
            cg[...] = jnp.zeros_like(cg)
            ca[...] = jnp.zeros_like(ca)

        (a, _), pull = jax.vjp(_rg_gates, x_ref[...], wa_ref[...].astype(F32), wx_ref[...].astype(F32), ba_ref[...],
                               bx_ref[...], lam_ref[...])
        hs_t = h_ref[...]
        _, pull_out = jax.vjp(_rg_out, hs_t, g_ref[...])
        dho, dgate = pull_out(dy_ref[...])
        dg_ref[...] = dgate.astype(BF16)
        a_s[...] = a
        d_s[...] = dho

        def body(k, c):
            gn, an = c
            r0 = pl.multiple_of((tm // SUBLANES - 1 - k) * SUBLANES, SUBLANES)
            ab, db = a_s[pl.ds(r0, SUBLANES), :], d_s[pl.ds(r0, SUBLANES), :]
            rows = [None] * SUBLANES
            for j in range(SUBLANES - 1, -1, -1):
                gn = db[j:j + 1] + an * gn
                an = ab[j:j + 1]
                rows[j] = gn
            g_s[pl.ds(r0, SUBLANES), :] = _rows_to_tile(rows)
            return gn, an

        gn, an = lax.fori_loop(0, tm // SUBLANES, body, (cg[0:1, :], ca[0:1, :]))
        cg[0:1, :] = gn
        ca[0:1, :] = an
        g = g_s[...]
        halo = jnp.where(pl.program_id(0) == nt - 1, 0.0, hp_ref[...])
        hprev = pltpu.roll(jnp.concatenate([halo, hs_t], axis=0), 1, axis=0)[SUBLANES:]
        dxc, dwa, dwx, dba, dbx, dlam = pull((g * hprev, g))
        dx_ref[...] = dxc
        _acc(dwa_ref, dwa, first)
        _acc(dwx_ref, dwx, first)
        _acc(dba_ref, dba, first)
        _acc(dbx_ref, dbx, first)
        _acc(dlam_ref, dlam, first)

    row = pl.BlockSpec((tm, 512), lambda i: (ti(i), 0))
    mat = pl.BlockSpec((512, 512), lambda i: (0, 0))
    vec = pl.BlockSpec((1, 512), lambda i: (0, 0))
    hprev = pl.BlockSpec((SUBLANES, 512), lambda i: (jnp.maximum(ti(i) * (tm // SUBLANES) - 1, 0), 0))
    return pl.pallas_call(
        kern, name=name, grid=(nt,),
        in_specs=[pl.BlockSpec((tm, 512), lambda i: (ti(i), dy_cb)), row,
                  pl.BlockSpec((tm, 512), lambda i: (ti(i), CD_GATE // 512)), row, hprev, mat, mat, vec, vec, vec],
        out_specs=[row, row, mat, mat, vec, vec, vec],
        out_shape=[jax.ShapeDtypeStruct((s, 512), F32), jax.ShapeDtypeStruct((s, 512), BF16),
                   jax.ShapeDtypeStruct((512, 512), F32), jax.ShapeDtypeStruct((512, 512), F32),
                   jax.ShapeDtypeStruct((1, 512), F32), jax.ShapeDtypeStruct((1, 512), F32), jax.ShapeDtypeStruct((1, 512), F32)],
        scratch_shapes=[pltpu.VMEM((tm, 512), F32)] * 3 + [pltpu.VMEM((SUBLANES, 512), F32)] * 2,
        compiler_params=_cp(("arbitrary",)),
    )(dy, xc, hc, hs, hs, wa, wx, ba, bx, lam)


AB_REF = dict(z=(0, 512), xbc=(512, 768), dt=(1280, 8), hq=(1288, 512), hf=(1800, 512), hi=(2312, 512), hg=(2824, 512))
AB_INT = dict(z=AB_Z, hq=AB_HQ, hf=AB_HF, hi=AB_HI, hg=AB_HG, xbc=AB_XBC, dt=AB_DT)
CD_REF = dict(q=(0, 512), k=(512, 128), v=(640, 128), gate=(768, 512), xr=(1280, 512))


def _cols(w, start, size):
    return lax.slice_in_dim(w, start, start + size, axis=w.ndim - 1)


def _pad_lanes(v):
    return jnp.concatenate([v.astype(F32), jnp.zeros((LANES - v.shape[0],), F32)]).reshape(1, LANES)


def _taker(g, layer, axis):
    width = g.shape[2 + axis]

    def take(start, size):
        out = []
        while size > 0:
            k, off = divmod(start, width)
            n = min(size, width - off)
            out.append(lax.slice_in_dim(g[k, layer], off, off + n, axis=axis))
            start, size = start + n, size - n
        return out

    return take


def _ab_in_to_int(g, layer):
    take = _taker(g, layer, 1)
    parts = sum([take(*AB_REF[k]) for k in ("z", "hq", "hf", "hi", "hg", "xbc", "dt")], [])
    return jnp.concatenate(parts + [jnp.zeros((g.shape[2], AB_COLS - AB_DT - 8), g.dtype)], axis=1)


def _ab_in_to_ref(w):
    return jnp.concatenate([_cols(w, AB_INT[k], AB_REF[k][1]) for k in ("z", "xbc", "dt", "hq", "hf", "hi", "hg")], axis=-1)


def _pairs_to_ref(w, axis):
    sl = lambda i: lax.slice_in_dim(w, 64 * i, 64 * i + 64, axis=axis)
    return jnp.concatenate([sl(2 * h) for h in range(4)] + [sl(2 * h + 1) for h in range(4)], axis=axis)


PAIR_ORDER = tuple(h for p in range(4) for h in (p, 4 + p))


def _cd_in_to_int(g, layer):
    take = _taker(g, layer, 1)
    q = sum([take(64 * h, 64) for h in PAIR_ORDER], [])
    return jnp.concatenate(q + sum([take(*CD_REF[k]) for k in ("gate", "xr", "k", "v")], []), axis=1)


def _cd_out_to_int(g, layer):
    take = _taker(g, layer, 0)
    return jnp.concatenate(sum([take(64 * h, 64) for h in PAIR_ORDER], []) + take(512, 512), axis=0)


def _whole(g, layer, axis):
    return jnp.concatenate([g[k, layer] for k in range(4)], axis=axis)


def _cd_in_to_ref(w):
    q = _pairs_to_ref(_cols(w, CD_Q, 512), w.ndim - 1)
    return jnp.concatenate([q, _cols(w, CD_K, 128), _cols(w, CD_V, 128), _cols(w, CD_GATE, 512), _cols(w, CD_XR, 512)], axis=-1)


def _block_diag(w):
    tiled = jnp.concatenate([w.reshape(RG_WIDTH, 64)] * 8, axis=1)
    return tiled * _BLOCK_DIAG_MASK.astype(w.dtype)


_BLOCK_DIAG_MASK = np.kron(np.eye(8, dtype=np.float32), np.ones((64, 64), np.float32))


def _diag_blocks(m):
    return jnp.sum((m * _BLOCK_DIAG_MASK).reshape(8, 64, 8, 64), axis=2)


def _conv8(w):
    return jnp.concatenate([w, jnp.zeros((SUBLANES - w.shape[0], w.shape[1]), w.dtype)], axis=0)


def _hg_lower_bounds(hg_lower):
    sm = jax.nn.softmax(hg_lower.astype(F32), axis=0)
    return jnp.clip(jnp.cumsum(sm, axis=0) - sm[0], 0.0, 1.0)


def _local_step(x, target, w):
    s = x.shape[0]
    gup = w["ffn_w_up"]
    wdowns = [_whole(w["ffn_w_down"], l, 0) for l in range(DEPTH)]
    lb_all, lb_pull = jax.vjp(_hg_lower_bounds, w["hg_lower"])
    grads = {k: [None] * (v.shape[1] if k in BIG else v.shape[0]) for k, v in w.items()}
    saved = []
    xb = x.astype(BF16)
    for l in range(DEPTH):
        j = l // 2
        t = f"l{l}"
        sv = dict(x=x, xb=xb)
        if l % 2 == 0:
            win = _ab_in_to_int(w["ab_w_in"], j)
            wout = _whole(w["ab_w_out"], j, 0)
            h = _mm([(xb, 0, win, 0, D_MODEL)], AB_COLS, F32, t + "_in")
            w8 = _conv8(w["ssd_conv_w"][j])
            xbcc = _dwconv_fwd(h, AB_XBC // CONV_CB, SSD_CONV_DIM, w8, w["ssd_conv_b"][j], 4, t + "_conv")
            sp = (_pad_lanes(w["ssd_dt_bias"][j]), _pad_lanes(w["ssd_a_log"][j]), _pad_lanes(w["ssd_d"][j]),
                  w["ssd_norm_w"][j].reshape(1, 512))
            ya, ssts = _ssd_fwd(h, xbcc, *sp, t + "_ssd")
            hp = (lb_all[j].reshape(1, 512), w["hg_norm_w"][j].reshape(1, LANES))
            yb, hsts = _hg_fwd(h, *hp, t + "_hg")
            sv.update(win=win, wout=wout, h=h, w8=w8, xbcc=xbcc, sp=sp, ssts=ssts, hp=hp, hsts=hsts)
        else:
            win = _cd_in_to_int(w["cd_w_in"], j)
            wout = _cd_out_to_int(w["cd_w_out"], j)
            h = _mm([(xb, 0, win, 0, D_MODEL)], CD_COLS, F32, t + "_in")
            srow = _pad_lanes(w["swa_sinks"][j])
            ya = _swa_fwd(h, srow, t + "_swa")
            w8 = _conv8(w["rg_conv_w"][j])
            xc = _dwconv_fwd(h, CD_XR // CONV_CB, RG_WIDTH, w8, w["rg_conv_b"][j], 4, t + "_conv")
            rp = (_block_diag(w["rg_wa"][j]).astype(BF16), _block_diag(w["rg_wx"][j]).astype(BF16),
                  w["rg_ba"][j].reshape(1, 512), w["rg_bx"][j].reshape(1, 512), w["rg_lambda"][j].reshape(1, 512))
            yb, hs = _rg_fwd(xc, h, *rp, t + "_rg")
            sv.update(win=win, wout=wout, h=h, w8=w8, srow=srow, xc=xc, rp=rp, hs=hs)
        x1, x1b, r1 = _mm_ln([(ya, 0, wout, 0, 512), (yb, 0, wout, 1, 512)], x, w["ln_g"][l, 0], w["ln_b"][l, 0], t + "_out_ln")
        wdown = wdowns[l]
        fw8 = _conv8(w["ffn_conv_w"][l])
        hu = _mm([(x1b, 0, gup, _shard_as_col_tile(gup, l), D_MODEL)], 2 * FFN_DIM, BF16, t + "_up")
        a, hg, hv = _ffn_gate_fwd(hu, fw8, w["ffn_conv_b"][l], t + "_gate")
        x, xb, r2 = _mm_ln([(a, 0, wdown, 0, FFN_DIM)], x1, w["ln_g"][l, 1], w["ln_b"][l, 1], t + "_down_ln")
        sv.update(ya=ya, yb=yb, r1=r1, x1b=x1b, hu=hu, hg=hg, hv=hv, a=a, r2=r2, fw8=fw8)
        saved.append(sv)

    loss = 0.5 * jnp.sum(_sqerr(x, target, "sqerr")) / D_MODEL
    dya, dyb, ca, cb = x, target, 1.0 / D_MODEL, -1.0 / D_MODEL
    for l in range(DEPTH - 1, -1, -1):
        j = l // 2
        t = f"l{l}b"
        sv = saved[l]
        wdown = wdowns[l]
        dr2, dg2, db2 = _ln_bwd(dya, dyb, ca, cb, sv["r2"], w["ln_g"][l, 1], t + "_ln2")
        da = _mm([(dr2, 0, wdown, 0, D_MODEL)], FFN_DIM, BF16, t + "_down", trans_b=True)
        grads["ffn_w_down"][l] = _mm_tn(sv["a"], 0, FFN_DIM, dr2, 0, D_MODEL, t + "_wdown")
        dhg, dhu, dfw8, dfb = _ffn_gate_bwd(sv["hu"], sv["hg"], sv["hv"], da, sv["fw8"], t + "_gate")
        grads["ffn_conv_w"][l] = dfw8[:3]
        grads["ffn_conv_b"][l] = dfb
        quarter = gup.shape[3]
        dx1 = _mm([(dh, half, gup, _shard_transposed(gup, 2 * gi + half, l), quarter)
                   for gi, dh in enumerate((dhg, dhu)) for half in range(2)], D_MODEL, F32, t + "_up", trans_b=True)
        grads["ffn_w_up"][l] = jnp.concatenate([_mm_tn(sv["x1b"], 0, D_MODEL, dhg, 0, FFN_DIM, t + "_wup_g"),
                                                _mm_tn(sv["x1b"], 0, D_MODEL, dhu, 0, FFN_DIM, t + "_wup_u")], axis=1)
        dr1, dg1, db1 = _ln_bwd(dr2, dx1, ALPHA, 1.0, sv["r1"], w["ln_g"][l, 0], t + "_ln1")
        grads["ln_g"][l] = jnp.stack([dg1, dg2])
        grads["ln_b"][l] = jnp.stack([db1, db2])
        win, wout, h = sv["win"], sv["wout"], sv["h"]
        dycat = _mm([(dr1, 0, wout, 0, D_MODEL)], D_MODEL, F32, t + "_out", trans_b=True)
        dwout = jnp.concatenate([_mm_tn(sv["ya"], 0, 512, dr1, 0, D_MODEL, t + "_wout_a"),
                                 _mm_tn(sv["yb"], 0, 512, dr1, 0, D_MODEL, t + "_wout_b")], axis=0)
        if l % 2 == 0:
            dz, dxbcc, ddt, ddtb, dal, ddsk, dnw = _ssd_bwd(dycat, 0, h, sv["xbcc"], sv["ssts"], *sv["sp"], t + "_ssd")
            dxbc, dcw, dcb = _dwconv_bwd(dxbcc, h, AB_XBC // CONV_CB, SSD_CONV_DIM, sv["w8"], 4, t + "_conv")
            dq, df, di, dg, dlb, dhnw = _hg_bwd(dycat, 1, h, sv["hsts"], *sv["hp"], t + "_hg")
            grads["ab_w_out"][j] = dwout
            grads["ssd_conv_w"][j], grads["ssd_conv_b"][j] = dcw[:4], dcb
            grads["ssd_dt_bias"][j], grads["ssd_a_log"][j], grads["ssd_d"][j] = ddtb[0, :8], dal[0, :8], ddsk[0, :8]
            grads["ssd_norm_w"][j], grads["hg_norm_w"][j] = dnw[0], dhnw[0]
            grads["hg_lower"][j] = dlb[0]
            pieces = [(dz, 0, AB_Z, 512), (dq, 0, AB_HQ, 512), (df, 0, AB_HF, 512), (di, 0, AB_HI, 512), (dg, 0, AB_HG, 512),
                      (dxbc, 0, AB_XBC, 256), (dxbc, 1, AB_XBC + 256, 256), (dxbc, 2, AB_XBC + 512, 256), (ddt, 0, AB_DT, LANES)]
            dwin = [_mm_tn(sv["xb"], 0, D_MODEL, p, 0, p.shape[1], t + f"_win{i}")
                    for i, p in enumerate((dz, dq, df, di, dg, dxbc, ddt))]
            grads["ab_w_in"][j] = _ab_in_to_ref(jnp.concatenate(dwin, axis=1))
        else:
            dq, dk, dv, dsr = _swa_bwd(dycat, 0, h, sv["srow"], t + "_swa")
            dxc, dgate, dwa, dwx, dba, dbx, dlam = _rg_bwd(dycat, 1, sv["xc"], h, sv["hs"], *sv["rp"], t + "_rg")
            dxr, dcw, dcb = _dwconv_bwd(dxc, h, CD_XR // CONV_CB, RG_WIDTH, sv["w8"], 4, t + "_conv")
            grads["cd_w_out"][j] = jnp.concatenate([_pairs_to_ref(dwout[:512], 0), dwout[512:]], axis=0)
            grads["swa_sinks"][j] = dsr[0, :8]
            grads["rg_conv_w"][j], grads["rg_conv_b"][j] = dcw[:4], dcb
            grads["rg_wa"][j], grads["rg_wx"][j] = _diag_blocks(dwa), _diag_blocks(dwx)
            grads["rg_ba"][j], grads["rg_bx"][j], grads["rg_lambda"][j] = dba[0], dbx[0], dlam[0]
            pieces = [(dq, 0, CD_Q, 512), (dgate, 0, CD_GATE, 512), (dxr, 0, CD_XR, 512), (dk, 0, CD_K, LANES), (dv, 0, CD_V, LANES)]
            dwin = [_mm_tn(sv["xb"], 0, D_MODEL, p, 0, p.shape[1], t + f"_win{i}")
                    for i, p in enumerate((dq, dgate, dxr, dk, dv))]
            grads["cd_w_in"][j] = _cd_in_to_ref(jnp.concatenate(dwin, axis=1))
        dx0 = _mm([(p, pcb, win, off // k, k) for p, pcb, off, k in pieces], D_MODEL, F32, t + "_in", trans_b=True)
        dya, dyb, ca, cb = dr1, dx0, ALPHA, 1.0
    grad_x = _axpby(dya, dyb, ca, cb, "grad_x")
    out = {k: (v if k in BIG else jnp.stack(v)) for k, v in grads.items()}
    (out["hg_lower"],) = lb_pull(out["hg_lower"])
    return loss, grad_x, out


MESH = pl.DeviceIdType.MESH
ANY = pl.BlockSpec(memory_space=pl.ANY)


def _place():
    x, y, c = lax.axis_index("x"), lax.axis_index("y"), lax.axis_index("c")
    return x, y, c, [(1 - x, y), (x, 1 - y), (1 - x, 1 - y)]


def _rcopy(src, dst, send_sems, recv_sems, k, to):
    return pltpu.make_async_remote_copy(src_ref=src, dst_ref=dst, send_sem=send_sems.at[k], recv_sem=recv_sems.at[k],
                                        device_id=to, device_id_type=MESH)


def _zeros_index(ndim):
    return (0,) * ndim


def _gather_chips(parts, name):
    n = len(parts)

    def body(*refs):
        w_refs, out_refs, (send_sems, recv_sems) = refs[:n], refs[n:2 * n], refs[2 * n:]
        x, y, c, chips = _place()
        me, sib = 2 * x + y, (x, y, 1 - c)
        first = [_rcopy(w_refs[a].at[c], out_refs[a].at[me, c], send_sems, recv_sems, 6 * a + k, (cx, cy, c))
                 for a in range(n) for k, (cx, cy) in enumerate(chips)]
        for cp in first:
            cp.start()
        passed = []
        for a in range(n):
            for k, (cx, cy) in enumerate(chips):
                blk = out_refs[a].at[2 * cx + cy, c]
                _rcopy(blk, blk, send_sems, recv_sems, 6 * a + k, (cx, cy, c)).wait_recv()
                passed.append(_rcopy(blk, blk, send_sems, recv_sems, 6 * a + 3 + k, sib))
                passed[-1].start()
        for a in range(n):
            for k, (cx, cy) in enumerate(chips):
                blk = out_refs[a].at[2 * cx + cy, 1 - c]
                _rcopy(blk, blk, send_sems, recv_sems, 6 * a + 3 + k, sib).wait_recv()
        for cp in first + passed:
            cp.wait_send()

    outs = pl.pallas_call(
        body, name=name, in_specs=[ANY] * n, out_specs=[ANY] * n,
        out_shape=[jax.ShapeDtypeStruct((4,) + p.shape, p.dtype) for p in parts],
        scratch_shapes=[pltpu.SemaphoreType.DMA((6 * n,)), pltpu.SemaphoreType.DMA((6 * n,))],
    )(*parts)
    me = 2 * lax.axis_index("x") + lax.axis_index("y")
    return [lax.dynamic_update_slice(o, p[None], (me,) + _zeros_index(p.ndim)) for o, p in zip(outs, parts)]


def _swap_halves(gs, name):
    n = len(gs)

    def body(*refs):
        g_refs, x_refs, (send_sems, recv_sems) = refs[:n], refs[n:2 * n], refs[2 * n:]
        x, y, c, _ = _place()
        cps = [_rcopy(g_refs[a].at[:, 1 - c], x_refs[a], send_sems, recv_sems, a, (x, y, 1 - c)) for a in range(n)]
        for cp in cps:
            cp.start()
        for cp in cps:
            cp.wait()

    return pl.pallas_call(
        body, name=name, in_specs=[ANY] * n, out_specs=[ANY] * n,
        out_shape=[jax.ShapeDtypeStruct((g.shape[0],) + g.shape[2:], g.dtype) for g in gs],
        scratch_shapes=[pltpu.SemaphoreType.DMA((n,)), pltpu.SemaphoreType.DMA((n,))],
    )(*gs)


def _scatter_chips(hbs, name):
    n = len(hbs)

    def body(*refs):
        h_refs, y_refs, (send_sems, recv_sems) = refs[:n], refs[n:2 * n], refs[2 * n:]
        x, y, c, chips = _place()
        me = 2 * x + y
        sends = [_rcopy(h_refs[a].at[2 * cx + cy], y_refs[a].at[me], send_sems, recv_sems, 3 * a + k, (cx, cy, c))
                 for a in range(n) for k, (cx, cy) in enumerate(chips)]
        for cp in sends:
            cp.start()
        for a in range(n):
            for k, (cx, cy) in enumerate(chips):
                blk = y_refs[a].at[2 * cx + cy]
                _rcopy(blk, blk, send_sems, recv_sems, 3 * a + k, (cx, cy, c)).wait_recv()
        for cp in sends:
            cp.wait_send()

    outs = pl.pallas_call(
        body, name=name, in_specs=[ANY] * n, out_specs=[ANY] * n,
        out_shape=[jax.ShapeDtypeStruct(h.shape, h.dtype) for h in hbs],
        scratch_shapes=[pltpu.SemaphoreType.DMA((3 * n,)), pltpu.SemaphoreType.DMA((3 * n,))],
    )(*hbs)
    me = 2 * lax.axis_index("x") + lax.axis_index("y")
    return [lax.dynamic_update_slice(o, lax.dynamic_slice_in_dim(h, me, 1, axis=0), (me,) + _zeros_index(h.ndim - 1))
            for o, h in zip(outs, hbs)]


def _share_half(fs, name):
    n = len(fs)

    def body(*refs):
        f_refs, out_refs, (send_sems, recv_sems) = refs[:n], refs[n:2 * n], refs[2 * n:]
        x, y, c, _ = _place()
        sib = (x, y, 1 - c)
        cps = [_rcopy(f_refs[a], out_refs[a].at[c], send_sems, recv_sems, a, sib) for a in range(n)]
        for cp in cps:
            cp.start()
        for a in range(n):
            _rcopy(f_refs[a], out_refs[a].at[1 - c], send_sems, recv_sems, a, sib).wait_recv()
        for cp in cps:
            cp.wait_send()

    outs = pl.pallas_call(
        body, name=name, in_specs=[ANY] * n, out_specs=[ANY] * n,
        out_shape=[jax.ShapeDtypeStruct((2,) + f.shape, f.dtype) for f in fs],
        scratch_shapes=[pltpu.SemaphoreType.DMA((n,)), pltpu.SemaphoreType.DMA((n,))],
    )(*fs)
    c = lax.axis_index("c")
    return [lax.dynamic_update_slice(o, f[None], (c,) + _zeros_index(f.ndim)) for o, f in zip(outs, fs)]


def _gather_devices(v, name):
    m_per, n = v.shape

    def body(x_ref, out_ref, send_sems, recv_sems, local_sem):
        x, y, c, chips = _place()
        me, sib = (x, y, c), (x, y, 1 - c)

        def rows(px, py, pc):
            return out_ref.at[pl.ds((4 * px + 2 * py + pc) * m_per, m_per), :]

        def copy(k, block, to, src=None):
            return _rcopy(rows(*block) if src is None else src, rows(*block), send_sems, recv_sems, k, to)

        mine = pltpu.make_async_copy(x_ref, rows(*me), local_sem)
        mine.start()
        first = [copy(0, me, sib, src=x_ref)]
        first += [copy(1 + j, me, (*chip, c), src=x_ref) for j, chip in enumerate(chips)]
        for cp in first:
            cp.start()
        passed = [copy(4 + j, (*chip, c), sib) for j, chip in enumerate(chips)]
        for j, chip in enumerate(chips):
            copy(1 + j, (*chip, c), me).wait_recv()
            passed[j].start()
        copy(0, sib, me).wait_recv()
        for j, chip in enumerate(chips):
            copy(4 + j, (*chip, 1 - c), me).wait_recv()
        for cp in first + passed:
            cp.wait_send()
        mine.wait()

    return pl.pallas_call(
        body, name=name, out_shape=jax.ShapeDtypeStruct((8 * m_per, n), v.dtype),
        in_specs=[pl.BlockSpec(memory_space=pltpu.VMEM)], out_specs=pl.BlockSpec(memory_space=pltpu.VMEM),
        scratch_shapes=[pltpu.SemaphoreType.DMA((7,)), pltpu.SemaphoreType.DMA((7,)), pltpu.SemaphoreType.DMA(())],
        compiler_params=_cp(),
    )(v)


ROW_TILES = (512, 256, 352, 128)


def _add_half(g, xr, c, name):
    _, _, r, n = g.shape
    tr = _pick(r, ROW_TILES)

    def kern(c_ref, g_ref, x_ref, o_ref):
        o_ref[...] = (g_ref[0] + x_ref[...]).astype(BF16)

    return pl.pallas_call(
        kern, name=name,
        grid_spec=pltpu.PrefetchScalarGridSpec(
            num_scalar_prefetch=1, grid=(4, r // tr),
            in_specs=[pl.BlockSpec((1, 1, tr, n), lambda k, i, c_ref: (k, c_ref[0], i, 0)),
                      pl.BlockSpec((1, tr, n), lambda k, i, c_ref: (k, i, 0))],
            out_specs=pl.BlockSpec((1, tr, n), lambda k, i, c_ref: (k, i, 0))),
        out_shape=jax.ShapeDtypeStruct(xr.shape, BF16), compiler_params=_cp(("parallel", "parallel")),
    )(c.reshape(1).astype(jnp.int32), g, xr)


def _sum_blocks(y, name, tr=None):
    n, r, cols = y.shape
    tr = tr or _pick(r, ROW_TILES)
    tr = tr if r % tr == 0 else r

    def kern(y_ref, o_ref):
        acc = y_ref[0].astype(F32)
        for k in range(1, n):
            acc = acc + y_ref[k].astype(F32)
        o_ref[...] = acc

    return pl.pallas_call(
        kern, name=name, grid=(r // tr,), in_specs=[pl.BlockSpec((n, tr, cols), lambda i: (0, i, 0))],
        out_specs=pl.BlockSpec((tr, cols), lambda i: (i, 0)), out_shape=jax.ShapeDtypeStruct((r, cols), F32),
        compiler_params=_cp(("parallel",)),
    )(y)


def _adamw(w, g, m, v, name, tr=256):
    r, n = w.shape
    tr = tr if r % tr == 0 else r

    def kern(w_ref, g_ref, m_ref, v_ref, d_ref, nm_ref, nv_ref):
        gg = g_ref[...]
        nm = ADAM_B1 * m_ref[...] + (1.0 - ADAM_B1) * gg
        nv = ADAM_B2 * v_ref[...] + (1.0 - ADAM_B2) * (gg * gg)
        m_hat = nm / (1.0 - ADAM_B1 ** ADAM_STEP)
        v_hat = nv / (1.0 - ADAM_B2 ** ADAM_STEP)
        d_ref[...] = -ADAM_LR * (m_hat / (jnp.sqrt(v_hat) + ADAM_EPS) + ADAM_WD * w_ref[...])
        nm_ref[...] = nm
        nv_ref[...] = nv

    blk = pl.BlockSpec((tr, n), lambda i: (i, 0))
    return pl.pallas_call(
        kern, name=name, grid=(r // tr,), in_specs=[blk] * 4, out_specs=[blk] * 3,
        out_shape=[jax.ShapeDtypeStruct((r, n), F32)] * 3, compiler_params=_cp(("parallel",)),
    )(w, g, m, v)


def _rows_of(n):
    return -(-n // (LANES * SUBLANES)) * SUBLANES


def _pack(arrs, lead, dtype, mult):
    ls = arrs[0].shape[:lead]
    blocks, total = [], 0
    for a in arrs:
        f = a.astype(dtype).reshape(ls + (-1,))
        n = f.shape[-1]
        rows = _rows_of(n)
        if rows * LANES != n:
            f = jnp.concatenate([f, jnp.zeros(ls + (rows * LANES - n,), dtype)], axis=-1)
        blocks.append(f.reshape(ls + (rows, LANES)))
        total += rows
    if total % mult:
        blocks.append(jnp.zeros(ls + (mult - total % mult, LANES), dtype))
    return jnp.concatenate(blocks, axis=lead)


def _unpack(buf, lead, shapes):
    ls = buf.shape[:lead]
    out, off = [], 0
    for sh in shapes:
        n = int(np.prod(sh))
        rows = _rows_of(n)
        flat = lax.slice_in_dim(buf, off, off + rows, axis=lead).reshape(ls + (rows * LANES,))
        if rows * LANES != n:
            flat = lax.slice_in_dim(flat, 0, n, axis=lead)
        out.append(flat.reshape(ls + tuple(sh)))
        off += rows
    return out


BIG = ("ab_w_in", "ab_w_out", "cd_w_in", "cd_w_out", "ffn_w_up", "ffn_w_down")
BIG_COLS = ("ab_w_in", "cd_w_in", "ffn_w_up")
SMALL_SHARDED = ("ssd_conv_w", "rg_conv_w", "rg_conv_b", "rg_ba", "rg_bx", "rg_lambda", "ffn_conv_w", "ln_g", "ln_b")
WEIGHTS = ("ab_w_in", "ssd_conv_w", "ssd_conv_b", "ssd_dt_bias", "ssd_a_log", "ssd_d", "ssd_norm_w", "hg_lower", "hg_norm_w",
           "ab_w_out", "cd_w_in", "swa_sinks", "rg_conv_w", "rg_conv_b", "rg_wa", "rg_ba", "rg_wx", "rg_bx", "rg_lambda",
           "cd_w_out", "ffn_w_up", "ffn_conv_w", "ffn_conv_b", "ffn_w_down", "ln_g", "ln_b")
SMALL = tuple(n for n in WEIGHTS if n not in BIG)


def _full_from_shards(name, g):
    if name in BIG_COLS or name in SMALL_SHARDED:
        return jnp.concatenate([g[k] for k in range(4)], axis=-1)
    return jnp.concatenate([g[k] for k in range(4)], axis=1)


def _layer_shards(name, f):
    k, n = f.shape
    if name in BIG_COLS:
        return jnp.transpose(f.reshape(k, 4, n // 4), (1, 0, 2))
    return f.reshape(4, k // 4, n)


def kernel(x, ab_w_in, ssd_conv_w, ssd_conv_b, ssd_dt_bias, ssd_a_log, ssd_d, ssd_norm_w, hg_lower, hg_norm_w, ab_w_out, cd_w_in, swa_sinks, rg_conv_w, rg_conv_b, rg_wa, rg_ba, rg_wx, rg_bx, rg_lambda, cd_w_out, ffn_w_up, ffn_conv_w, ffn_conv_b, ffn_w_down, ln_g, ln_b, loss_target, m_ab_w_in, m_ssd_conv_w, m_ssd_conv_b, m_ssd_dt_bias, m_ssd_a_log, m_ssd_d, m_ssd_norm_w, m_hg_lower, m_hg_norm_w, m_ab_w_out, m_cd_w_in, m_swa_sinks, m_rg_conv_w, m_rg_conv_b, m_rg_wa, m_rg_ba, m_rg_wx, m_rg_bx, m_rg_lambda, m_cd_w_out, m_ffn_w_up, m_ffn_conv_w, m_ffn_conv_b, m_ffn_w_down, m_ln_g, m_ln_b, v_ab_w_in, v_ssd_conv_w, v_ssd_conv_b, v_ssd_dt_bias, v_ssd_a_log, v_ssd_d, v_ssd_norm_w, v_hg_lower, v_hg_norm_w, v_ab_w_out, v_cd_w_in, v_swa_sinks, v_rg_conv_w, v_rg_conv_b, v_rg_wa, v_rg_ba, v_rg_wx, v_rg_bx, v_rg_lambda, v_cd_w_out, v_ffn_w_up, v_ffn_conv_w, v_ffn_conv_b, v_ffn_w_down, v_ln_g, v_ln_b):
    args = locals()
    w = {n: args[n] for n in WEIGHTS}
    mom = {n: args["m_" + n] for n in WEIGHTS}
    var = {n: args["v_" + n] for n in WEIGHTS}
    cx, cy, cc = lax.axis_index("x"), lax.axis_index("y"), lax.axis_index("c")
    chip = 2 * cx + cy

    halves = [w[n].astype(BF16).reshape((2, w[n].shape[0] // 2) + w[n].shape[1:]) for n in BIG]
    gathered = _gather_chips(halves, "gather_weights")
    full = {n: g.reshape((4,) + w[n].shape) for n, g in zip(BIG, gathered)}
    sp = _pack([w[n] for n in SMALL_SHARDED], 0, F32, SUBLANES)
    sg = _gather_devices(sp, "gather_small").reshape(4, 2, sp.shape[0], LANES)[:, 0]
    for n, g in zip(SMALL_SHARDED, _unpack(sg, 1, [w[n].shape for n in SMALL_SHARDED])):
        full[n] = _full_from_shards(n, g)
    for n in SMALL:
        full.setdefault(n, w[n])

    loss, grad_x, grads = _local_step(x[0], loss_target[0], full)
    loss = lax.psum(loss, ("x", "y", "c"))

    g4 = []
    for n in BIG:
        st = jnp.stack([_layer_shards(n, f) for f in grads[n]], axis=1)
        g4.append(st.reshape(4, 2, st.shape[1] // 2 * st.shape[2], st.shape[3]))
    hbs = [_add_half(g, xr, cc, "add_halves_" + n) for n, g, xr in zip(BIG, g4, _swap_halves(g4, "swap_halves"))]
    own = [_sum_blocks(y, "sum_chips_" + n) for n, y in zip(BIG, _scatter_chips(hbs, "scatter_chips"))]
    gout = {n: o.reshape(w[n].shape) for n, o in zip(BIG, _share_half(own, "share_half"))}

    small_shapes = [grads[n].shape for n in SMALL]
    gs = _pack([grads[n] for n in SMALL], 0, F32, SUBLANES)
    gsum = _sum_blocks(_gather_devices(gs, "gather_small_grads").reshape(8, gs.shape[0], LANES), "sum_small")
    for n, g in zip(SMALL, _unpack(gsum, 0, small_shapes)):
        if n in SMALL_SHARDED:
            width = w[n].shape[-1]
            g = lax.dynamic_slice_in_dim(g, chip * width, width, axis=g.ndim - 1)
        gout[n] = g

    delta, new_m, new_v = {}, {}, {}
    for n in BIG:
        sh = w[n].shape
        two = lambda a: a.reshape(-1, sh[-1])
        d, nm, nv = _adamw(two(w[n]), two(gout[n]), two(mom[n]), two(var[n]), "adamw_" + n)
        delta[n], new_m[n], new_v[n] = d.reshape(sh), nm.reshape(sh), nv.reshape(sh)
    local_shapes = [w[n].shape for n in SMALL]
    packs = [_pack([d[n] for n in SMALL], 0, F32, SUBLANES) for d in (w, gout, mom, var)]
    for dst, buf in zip((delta, new_m, new_v), _adamw(*packs, "adamw_small")):
        dst.update(zip(SMALL, _unpack(buf, 0, local_shapes)))

    return (loss, grad_x[None], *[gout[n] for n in WEIGHTS], *[delta[n] for n in WEIGHTS],
            *[new_m[n] for n in WEIGHTS], *[new_v[n] for n in WEIGHTS])
```

```python
import functools

import numpy as np
import jax
import jax.numpy as jnp
from jax import lax
from jax.experimental import pallas as pl
from jax.experimental.pallas import tpu as pltpu

F32 = jnp.float32
BF16 = jnp.bfloat16

D_MODEL = 1024
DEPTH = 4
SSD_HEADS = 8
SSD_D_INNER = 512
SSD_CONV_DIM = 768
SSD_CHUNK = 128
HG_CHUNK = 64
HG_HEADS = 4
SWA_BLOCK = 128
RG_WIDTH = 512
FFN_DIM = 2816
LN_EPS = 1e-5
RMS_EPS = 1e-6
MASK_VALUE = -1e9
ALPHA = (2 * DEPTH) ** 0.25
RG_C = 8.0
ADAM_LR, ADAM_B1, ADAM_B2, ADAM_EPS, ADAM_WD, ADAM_STEP = 0.001, 0.9, 0.999, 1e-08, 0.01, 10

LANES = 128
SUBLANES = 8
HALO = 16
CONV_CB = 256
VMEM_LIMIT = 56 * 1024 * 1024

AB_Z, AB_HQ, AB_HF, AB_HI, AB_HG, AB_XBC, AB_DT, AB_COLS = 0, 512, 1024, 1536, 2048, 2560, 3328, 3456
CD_Q, CD_GATE, CD_XR, CD_K, CD_V, CD_COLS = 0, 512, 1024, 1536, 1664, 1792


def _cp(sem=None):
    kw = dict(vmem_limit_bytes=VMEM_LIMIT)
    if sem is not None:
        kw["dimension_semantics"] = sem
    return pltpu.CompilerParams(**kw)


def _dot(a, b, dims=(((1,), (0,)), ((), ())), precision=None):
    return lax.dot_general(a, b, dims, precision=precision, preferred_element_type=F32)


NN = (((1,), (0,)), ((), ()))
NT = (((1,), (1,)), ((), ()))
TN = (((0,), (0,)), ((), ()))


@functools.partial(jax.custom_vjp, nondiff_argnums=(2,))
def _bdot(a, b, dims=NN):
    return _dot(a.astype(BF16), b.astype(BF16), dims)


def _bdot_fwd(a, b, dims):
    return _bdot(a, b, dims), (a, b)


def _bdot_bwd(dims, res, ct):
    a, b = res
    ab, bb, cb = a.astype(BF16), b.astype(BF16), ct.astype(BF16)
    if dims == NN:
        da, db = _dot(cb, bb, NT), _dot(ab, cb, TN)
    elif dims == NT:
        da, db = _dot(cb, bb, NN), _dot(cb, ab, TN)
    else:
        da, db = _dot(bb, cb, NT), _dot(ab, cb, NN)
    return da.astype(a.dtype), db.astype(b.dtype)


_bdot.defvjp(_bdot_fwd, _bdot_bwd)


WIDE_TILES = (1408, 1152, 1024, 896, 768, 512, 384, 256, 128)
MM_BLOCK_BYTES = 6 * 1024 * 1024


def _pick(n, cands):
    for c in cands:
        if n % c == 0:
            return c
    return n


def _mm(pairs, n_out, out_dtype, name, trans_b=False, tm=512):
    m = pairs[0][0].shape[0]
    a_row_bytes = sum(p[4] * p[0].dtype.itemsize for p in pairs)
    b_col_bytes = sum(p[4] * p[2].dtype.itemsize for p in pairs)
    tm = min(2 * tm if 2 * tm * a_row_bytes <= 2 * MM_BLOCK_BYTES else tm, m)
    tn = _pick(n_out, [c for c in WIDE_TILES if c * b_col_bytes <= MM_BLOCK_BYTES])
    n = len(pairs)

    def kern(*refs):
        acc = None
        for i in range(n):
            p = _bdot(refs[i][...], refs[n + i][...], NT if trans_b else NN)
            acc = p if acc is None else acc + p
        refs[2 * n][...] = acc.astype(out_dtype)

    in_specs = []
    for a, acb, b, bcb, k in pairs:
        in_specs.append(pl.BlockSpec((tm, k), functools.partial(lambda i, j, c: (i, c), c=acb)))
    for a, acb, b, bcb, k in pairs:
        if callable(bcb):
            in_specs.append(bcb(tn))
        elif trans_b:
            in_specs.append(pl.BlockSpec((tn, k), functools.partial(lambda i, j, c: (j, c), c=bcb)))
        else:
            in_specs.append(pl.BlockSpec((k, tn), functools.partial(lambda i, j, c: (c, j), c=bcb)))
    return pl.pallas_call(
        kern, name=name, grid=(m // tm, n_out // tn), in_specs=in_specs,
        out_specs=pl.BlockSpec((tm, tn), lambda i, j: (i, j)),
        out_shape=jax.ShapeDtypeStruct((m, n_out), out_dtype),
        compiler_params=_cp(("parallel", "arbitrary")),
    )(*[p[0] for p in pairs], *[p[2] for p in pairs])


def _shard_as_col_tile(g, layer):
    def spec(tn):
        assert tn == g.shape[3]
        return pl.BlockSpec((None, None, g.shape[2], tn), lambda i, j: (j, layer, 0, 0))
    return spec


def _shard_transposed(g, chip, layer):
    return lambda tn: pl.BlockSpec((None, None, tn, g.shape[3]), lambda i, j: (chip, layer, j, 0))


def _mm_tn(a, a_cb, ka, g, g_cb, ng, name, tm=2048, column_tiles_apart=False):
    m = a.shape[0]
    tm = min(tm, m)
    tk = _pick(ka, (1024, 1408, 512, 256, 128))
    tn = _pick(ng, WIDE_TILES)
    nm = m // tm

    def kern(a_ref, g_ref, o_ref):
        p = _bdot(a_ref[...], g_ref[...], TN)

        @pl.when(pl.program_id(2) == 0)
        def _():
            o_ref[...] = p

        @pl.when(pl.program_id(2) > 0)
        def _():
            o_ref[...] += p

    ak, gk = ka // tk, ng // tn
    if column_tiles_apart:
        out_spec = pl.BlockSpec((None, tk, tn), lambda i, j, r: (j, i, 0))
        out_shape = jax.ShapeDtypeStruct((gk, ka, tn), F32)
    else:
        out_spec = pl.BlockSpec((tk, tn), lambda i, j, r: (i, j))
        out_shape = jax.ShapeDtypeStruct((ka, ng), F32)
    return pl.pallas_call(
        kern, name=name, grid=(ak, gk, nm),
        in_specs=[pl.BlockSpec((tm, tk), lambda i, j, r: (r, a_cb * ak + i)),
                  pl.BlockSpec((tm, tn), lambda i, j, r: (r, g_cb * gk + j))],
        out_specs=out_spec, out_shape=out_shape, compiler_params=_cp(("parallel", "parallel", "arbitrary")),
    )(a, g)


def _mm_ln(pairs, x, g, b, name, tm=512):
    m, d = x.shape
    tm = min(tm, m)
    n = len(pairs)

    def kern(*refs):
        x_ref, g_ref, b_ref, y_ref, yb_ref, r_ref = refs[2 * n:]
        r = ALPHA * x_ref[...]
        for i in range(n):
            r = r + _bdot(refs[i][...], refs[n + i][...])
        mu = jnp.mean(r, -1, keepdims=True)
        xc = r - mu
        var = jnp.mean(xc * xc, -1, keepdims=True)
        y = xc * lax.rsqrt(var + LN_EPS) * g_ref[...] + b_ref[...]
        y_ref[...] = y
        yb_ref[...] = y.astype(BF16)
        r_ref[...] = r

    row = pl.BlockSpec((tm, d), lambda i: (i, 0))
    vec = pl.BlockSpec((1, d), lambda i: (0, 0))
    in_specs = [pl.BlockSpec((tm, k), functools.partial(lambda i, c: (i, c), c=acb)) for _, acb, _, _, k in pairs]
    in_specs += [pl.BlockSpec((k, d), functools.partial(lambda i, c: (c, 0), c=bcb)) for _, _, _, bcb, k in pairs]
    return pl.pallas_call(
        kern, name=name, grid=(m // tm,), in_specs=in_specs + [row, vec, vec], out_specs=[row, row, row],
        out_shape=[jax.ShapeDtypeStruct((m, d), F32), jax.ShapeDtypeStruct((m, d), BF16), jax.ShapeDtypeStruct((m, d), F32)],
        compiler_params=_cp(("parallel",)),
    )(*[p[0] for p in pairs], *[p[2] for p in pairs], x, g.reshape(1, d), b.reshape(1, d))


def _ln_bwd(dya, dyb, ca, cb, r, g, name, tm=512):
    s, d = r.shape
    tm = min(tm, s)

    def kern(a_ref, b_ref, r_ref, g_ref, dr_ref, dg_ref, db_ref):
        dy = ca * a_ref[...] + cb * b_ref[...]
        rr = r_ref[...]
        mu = jnp.mean(rr, -1, keepdims=True)
        xc = rr - mu
        var = jnp.mean(xc * xc, -1, keepdims=True)
        rstd = lax.rsqrt(var + LN_EPS)
        xhat = xc * rstd
        dxh = dy * g_ref[...]
        dr_ref[...] = rstd * (dxh - jnp.mean(dxh, -1, keepdims=True) - xhat * jnp.mean(dxh * xhat, -1, keepdims=True))
        dg = jnp.sum(dy * xhat, 0, keepdims=True)
        db = jnp.sum(dy, 0, keepdims=True)

        @pl.when(pl.program_id(0) == 0)
        def _():
            dg_ref[...] = dg
            db_ref[...] = db

        @pl.when(pl.program_id(0) > 0)
        def _():
            dg_ref[...] += dg
            db_ref[...] += db

    row = pl.BlockSpec((tm, d), lambda i: (i, 0))
    vec = pl.BlockSpec((1, d), lambda i: (0, 0))
    dr, dg, db = pl.pallas_call(
        kern, name=name, grid=(s // tm,), in_specs=[row, row, row, vec], out_specs=[row, vec, vec],
        out_shape=[jax.ShapeDtypeStruct((s, d), F32), jax.ShapeDtypeStruct((1, d), F32), jax.ShapeDtypeStruct((1, d), F32)],
        compiler_params=_cp(("arbitrary",)),
    )(dya, dyb, r, g.reshape(1, d))
    return dr, dg[0], db[0]


def _sqerr(y, t, name, tm=256):
    s, d = y.shape
    tm = min(tm, s)

    def kern(y_ref, t_ref, o_ref):
        e = y_ref[...] - t_ref[...]
        p = jnp.sum(e * e, 0, keepdims=True)

        @pl.when(pl.program_id(0) == 0)
        def _():
            o_ref[...] = p

        @pl.when(pl.program_id(0) > 0)
        def _():
            o_ref[...] += p

    row = pl.BlockSpec((tm, d), lambda i: (i, 0))
    return pl.pallas_call(
        kern, name=name, grid=(s // tm,), in_specs=[row, row], out_specs=pl.BlockSpec((1, d), lambda i: (0, 0)),
        out_shape=jax.ShapeDtypeStruct((1, d), F32), compiler_params=_cp(("arbitrary",)),
    )(y, t)


def _axpby(a, b, ca, cb, name, tm=256):
    s, d = a.shape
    tm = min(tm, s)

    def kern(a_ref, b_ref, o_ref):
        o_ref[...] = ca * a_ref[...] + cb * b_ref[...]

    row = pl.BlockSpec((tm, d), lambda i: (i, 0))
    return pl.pallas_call(
        kern, name=name, grid=(s // tm,), in_specs=[row, row], out_specs=row,
        out_shape=jax.ShapeDtypeStruct((s, d), F32), compiler_params=_cp(("parallel",)),
    )(a, b)


def _shift_down(xx, s, n):
    if s == 0:
        return xx[HALO:HALO + n]
    return pltpu.roll(xx, s, axis=0)[HALO:HALO + n]


def _shift_up(yy, s, n):
    if s == 0:
        return yy[0:n]
    return pltpu.roll(yy, yy.shape[0] - s, axis=0)[0:n]


def _prev_halo(tm):
    return lambda i: jnp.maximum(i * (tm // HALO) - 1, 0)


def _dwconv_fwd(x, x_cb0, c, w8, b, k, name, tm=1024):
    s = x.shape[0]
    tm = min(tm, s)
    ph = _prev_halo(tm)

    def kern(x_ref, h_ref, w_ref, b_ref, y_ref):
        halo = jnp.where(pl.program_id(0) == 0, 0.0, h_ref[...].astype(F32))
        xx = jnp.concatenate([halo, x_ref[...].astype(F32)], axis=0)
        w = w_ref[...]
        acc = b_ref[...] + w[k - 1:k] * xx[HALO:]
        for j in range(k - 1):
            acc = acc + w[j:j + 1] * _shift_down(xx, k - 1 - j, tm)
        y_ref[...] = acc

    return pl.pallas_call(
        kern, name=name, grid=(s // tm, c // CONV_CB),
        in_specs=[pl.BlockSpec((tm, CONV_CB), lambda i, j: (i, x_cb0 + j)),
                  pl.BlockSpec((HALO, CONV_CB), lambda i, j: (ph(i), x_cb0 + j)),
                  pl.BlockSpec((SUBLANES, CONV_CB), lambda i, j: (0, j)),
                  pl.BlockSpec((1, CONV_CB), lambda i, j: (0, j))],
        out_specs=pl.BlockSpec((tm, CONV_CB), lambda i, j: (i, j)),
        out_shape=jax.ShapeDtypeStruct((s, c), F32), compiler_params=_cp(("parallel", "parallel")),
    )(x, x, w8, b.reshape(1, c))


def _dwconv_bwd(dy, x, x_cb0, c, w8, k, name, tm=1024):
    s = x.shape[0]
    tm = min(tm, s)
    nt = s // tm
    ph = _prev_halo(tm)
    nh = lambda i: jnp.minimum((i + 1) * (tm // HALO), s // HALO - 1)

    def kern(dy_ref, dyn_ref, x_ref, h_ref, w_ref, dx_ref, dw_ref, db_ref):
        i = pl.program_id(1)
        halo = jnp.where(i == 0, 0.0, h_ref[...].astype(F32))
        xx = jnp.concatenate([halo, x_ref[...].astype(F32)], axis=0)
        dyt = dy_ref[...]
        nxt = jnp.where(i == nt - 1, 0.0, dyn_ref[...])
        dyy = jnp.concatenate([dyt, nxt], axis=0)
        w = w_ref[...]
        dx = w[k - 1:k] * dyt
        rows = [jnp.sum(dyt * _shift_down(xx, k - 1 - j, tm), 0, keepdims=True) for j in range(k)]
        for j in range(k - 1):
            dx = dx + w[j:j + 1] * _shift_up(dyy, k - 1 - j, tm)
        dx_ref[...] = dx.astype(BF16)
        dw = jnp.concatenate(rows + [jnp.zeros((SUBLANES - k, CONV_CB), F32)], axis=0)
        db = jnp.sum(dyt, 0, keepdims=True)

        @pl.when(i == 0)
        def _():
            dw_ref[...] = dw
            db_ref[...] = db

        @pl.when(i > 0)
        def _():
            dw_ref[...] += dw
            db_ref[...] += db

    dx, dw, db = pl.pallas_call(
        kern, name=name, grid=(c // CONV_CB, nt),
        in_specs=[pl.BlockSpec((tm, CONV_CB), lambda j, i: (i, j)),
                  pl.BlockSpec((HALO, CONV_CB), lambda j, i: (nh(i), j)),
                  pl.BlockSpec((tm, CONV_CB), lambda j, i: (i, x_cb0 + j)),
                  pl.BlockSpec((HALO, CONV_CB), lambda j, i: (ph(i), x_cb0 + j)),
                  pl.BlockSpec((SUBLANES, CONV_CB), lambda j, i: (0, j))],
        out_specs=[pl.BlockSpec((tm, CONV_CB), lambda j, i: (i, j)),
                   pl.BlockSpec((SUBLANES, CONV_CB), lambda j, i: (0, j)),
                   pl.BlockSpec((1, CONV_CB), lambda j, i: (0, j))],
        out_shape=[jax.ShapeDtypeStruct((s, c), BF16), jax.ShapeDtypeStruct((SUBLANES, c), F32),
                   jax.ShapeDtypeStruct((1, c), F32)],
        compiler_params=_cp(("parallel", "arbitrary")),
    )(dy, dy, x, x, w8)
    return dx, dw, db[0]


def _silu(x):
    return x * jax.nn.sigmoid(x)


def _ffn_gate_fwd(hu, w8, b, name, tm=2048):
    s = hu.shape[0]
    tm = min(tm, s)
    nb = FFN_DIM // CONV_CB
    ph = _prev_halo(tm)
    k = 3

    def conv(x_ref, h_ref, w_ref, b_ref):
        halo = jnp.where(pl.program_id(0) == 0, 0.0, h_ref[...].astype(F32))
        xx = jnp.concatenate([halo, x_ref[...].astype(F32)], axis=0)
        w = w_ref[...]
        acc = b_ref[...] + w[k - 1:k] * xx[HALO:]
        for j in range(k - 1):
            acc = acc + w[j:j + 1] * _shift_down(xx, k - 1 - j, tm)
        return acc

    def kern(g_ref, gh_ref, u_ref, uh_ref, wg_ref, wu_ref, bg_ref, bu_ref, a_ref, hg_ref, hu_ref):
        g = conv(g_ref, gh_ref, wg_ref, bg_ref)
        u = conv(u_ref, uh_ref, wu_ref, bu_ref)
        a_ref[...] = (_silu(g) * u).astype(BF16)
        hg_ref[...] = g.astype(BF16)
        hu_ref[...] = u.astype(BF16)

    main = lambda off: pl.BlockSpec((tm, CONV_CB), lambda i, j: (i, off + j))
    halo = lambda off: pl.BlockSpec((HALO, CONV_CB), lambda i, j: (ph(i), off + j))
    wsp = lambda off: pl.BlockSpec((SUBLANES, CONV_CB), lambda i, j: (0, off + j))
    bsp = lambda off: pl.BlockSpec((1, CONV_CB), lambda i, j: (0, off + j))
    b2 = b.reshape(1, 2 * FFN_DIM)
    half = jax.ShapeDtypeStruct((s, FFN_DIM), BF16)
    return pl.pallas_call(
        kern, name=name, grid=(s // tm, nb),
        in_specs=[main(0), halo(0), main(nb), halo(nb), wsp(0), wsp(nb), bsp(0), bsp(nb)],
        out_specs=[main(0), main(0), main(0)], out_shape=[half, half, half],
        compiler_params=_cp(("parallel", "parallel")),
    )(hu, hu, hu, hu, w8, w8, b2, b2)


def _ffn_gate_bwd(hu, hg, hv, da, w8, name, tm=1024):
    s = hu.shape[0]
    tm = min(tm, s)
    nt = s // tm
    nb = FFN_DIM // CONV_CB
    nh = lambda i: jnp.minimum((i + 1) * (tm // HALO), s // HALO - 1)
    k = 3

    def kern(xg_ref, xu_ref, g_ref, gn_ref, u_ref, un_ref, da_ref, dan_ref, wg_ref, wu_ref,
             dg_ref, du_ref, dwg_ref, dwu_ref, dbg_ref, dbu_ref):
        i = pl.program_id(1)

        def ext(x_ref, n_ref):
            return jnp.concatenate([x_ref[...].astype(F32), n_ref[...].astype(F32)], axis=0)

        g, u = ext(g_ref, gn_ref), ext(u_ref, un_ref)
        dae = jnp.concatenate([da_ref[...].astype(F32), jnp.where(i == nt - 1, 0.0, dan_ref[...].astype(F32))], axis=0)
        sg = jax.nn.sigmoid(g)
        dhg = dae * u * (sg * (1.0 + g * (1.0 - sg)))
        dhu = dae * (g * sg)

        def back(dh, x, w):
            taps = [dh[0:tm] if j == k - 1 else _shift_up(dh, k - 1 - j, tm) for j in range(k)]
            dx = w[0:1] * taps[0]
            for j in range(1, k):
                dx = dx + w[j:j + 1] * taps[j]
            rows = [jnp.sum(taps[j] * x, 0, keepdims=True) for j in range(k)]
            dw = jnp.concatenate(rows + [jnp.zeros((SUBLANES - k, CONV_CB), F32)], axis=0)
            return dx, dw, jnp.sum(dh[0:tm], 0, keepdims=True)

        dxg, dwg, dbg = back(dhg, xg_ref[...].astype(F32), wg_ref[...])
        dxu, dwu, dbu = back(dhu, xu_ref[...].astype(F32), wu_ref[...])
        dg_ref[...] = dxg.astype(BF16)
        du_ref[...] = dxu.astype(BF16)

        @pl.when(i == 0)
        def _():
            dwg_ref[...] = dwg
            dwu_ref[...] = dwu
            dbg_ref[...] = dbg
            dbu_ref[...] = dbu

        @pl.when(i > 0)
        def _():
            dwg_ref[...] += dwg
            dwu_ref[...] += dwu
            dbg_ref[...] += dbg
            dbu_ref[...] += dbu

    main = lambda off: pl.BlockSpec((tm, CONV_CB), lambda j, i: (i, off + j))
    nxt = lambda off: pl.BlockSpec((HALO, CONV_CB), lambda j, i: (nh(i), off + j))
    wsp = lambda off: pl.BlockSpec((SUBLANES, CONV_CB), lambda j, i: (0, off + j))
    bsp = lambda off: pl.BlockSpec((1, CONV_CB), lambda j, i: (0, off + j))
    outs = pl.pallas_call(
        kern, name=name, grid=(nb, nt),
        in_specs=[main(0), main(nb), main(0), nxt(0), main(0), nxt(0), main(0), nxt(0), wsp(0), wsp(nb)],
        out_specs=[main(0), main(0), wsp(0), wsp(0), bsp(0), bsp(0)],
        out_shape=[jax.ShapeDtypeStruct((s, FFN_DIM), BF16), jax.ShapeDtypeStruct((s, FFN_DIM), BF16),
                   jax.ShapeDtypeStruct((SUBLANES, FFN_DIM), F32), jax.ShapeDtypeStruct((SUBLANES, FFN_DIM), F32),
                   jax.ShapeDtypeStruct((1, FFN_DIM), F32), jax.ShapeDtypeStruct((1, FFN_DIM), F32)],
        compiler_params=_cp(("parallel", "arbitrary")),
    )(hu, hu, hg, hg, hv, hv, da, da, w8, w8)
    dg, du, dwg, dwu, dbg, dbu = outs
    return dg, du, jnp.concatenate([dwg, dwu], axis=1), jnp.concatenate([dbg[0], dbu[0]])


def _iota(shape, dim):
    return lax.broadcasted_iota(jnp.int32, shape, dim)


def _in_range(idx, start, size):
    return jnp.logical_and(idx >= start, idx < start + size)


@functools.partial(jax.custom_vjp, nondiff_argnums=(2, 3))
def _zdot(a, b, dims, z):
    zz, x = ((a, b) if z == 0 else (b, a))
    zz = zz.astype(BF16)
    xh = x.astype(BF16)
    r1 = x - xh.astype(F32)
    xm = r1.astype(BF16)
    xl = (r1 - xm.astype(F32)).astype(BF16)
    f = (lambda p: _dot(zz, p, dims)) if z == 0 else (lambda p: _dot(p, zz, dims))
    return f(xh) + f(xm) + f(xl)


def _zdot_fwd(a, b, dims, z):
    return _zdot(a, b, dims, z), (a if z == 0 else b)


def _zdot_bwd(dims, z, zz, ct):
    zb = zz.astype(BF16)
    ch = ct.astype(BF16)
    cm = (ct - ch.astype(F32)).astype(BF16)
    if dims == NN and z == 0:
        f = lambda c: _dot(zb, c, TN)
    elif dims == NN and z == 1:
        f = lambda c: _dot(c, zb, NT)
    else:
        f = lambda c: _dot(c, zb, TN)
    dx = f(ch) + f(cm)
    return (jnp.zeros_like(zz), dx) if z == 0 else (dx, jnp.zeros_like(zz))


_zdot.defvjp(_zdot_fwd, _zdot_bwd)


def _expand8(v, e):
    return jnp.sum(_zdot(jnp.broadcast_to(v, (SUBLANES, v.shape[1])), e, NN, 1), 0, keepdims=True) * (1.0 / SUBLANES)


def _ssd_consts():
    tri = np.tril(np.ones((SSD_CHUNK, SSD_CHUNK), np.float32))
    e = np.zeros((LANES, SSD_D_INNER), np.float32)
    for h in range(SSD_HEADS):
        e[h, 64 * h:64 * h + 64] = 1.0
    bd = np.zeros((LANES, SSD_D_INNER), np.float32)
    bd[:64, :256] = 1.0
    bd[64:, 256:] = 1.0
    return jnp.asarray(tri), jnp.asarray(e), jnp.asarray(bd)


def _ssd_chunk(tri, e, bd, z, xs_p, bm_p, cm_p, dtr, st, dtb, alog, dsk, nw):
    n = z.shape[0]
    lane128, lane512 = _iota((1, LANES), 1), _iota((1, SSD_D_INNER), 1)
    sub8 = _iota((SUBLANES, 1), 0)
    tril = _iota((n, 1), 0) >= _iota((1, n), 1)
    head_rows = jnp.where(sub8 == lane128, 1.0, 0.0)
    xs, bm, cm = _silu(xs_p), _silu(bm_p), _silu(cm_p)
    dt = jax.nn.softplus(dtr + dtb)
    da = dt * (-jnp.exp(alog))
    acs = _zdot(tri, da, NN, 0)
    acst = _zdot(head_rows, acs, NT, 0)
    xc = xs * _zdot(dt, e, NN, 1)
    tot = jnp.sum(da, 0, keepdims=True)
    y = _bdot(cm, st) * _zdot(jnp.exp(acs), e, NN, 1)
    st_new = st * _expand8(jnp.exp(tot), e) + bd * _bdot(bm, xc * _zdot(jnp.exp(tot - acs), e, NN, 1), TN)
    for g in range(2):
        cb = _bdot(jnp.where(_in_range(lane128, 64 * g, 64), cm, 0.0), bm, NT)
        for h in range(4 * g, 4 * g + 4):
            col = jnp.sum(jnp.where(lane128 == h, acs, 0.0), 1, keepdims=True)
            row = jnp.sum(jnp.where(sub8 == h, acst, 0.0), 0, keepdims=True)
            dm = jnp.where(tril, jnp.exp(jnp.minimum(col - row, 0.0)), 0.0)
            y = y + _bdot(cb * dm, jnp.where(_in_range(lane512, 64 * h, 64), xc, 0.0))
    y = (y + _expand8(dsk, e) * xs) * _silu(z)
    ysq = y * y
    g0 = lane512 < 256
    ms0 = jnp.sum(jnp.where(g0, ysq, 0.0), 1, keepdims=True) * (1.0 / 256)
    ms1 = jnp.sum(jnp.where(g0, 0.0, ysq), 1, keepdims=True) * (1.0 / 256)
    rs = jnp.where(g0, lax.rsqrt(ms0 + RMS_EPS), lax.rsqrt(ms1 + RMS_EPS))
    return y * rs * nw, st_new


SSD_STEP = 2


def _ssd_rows(u):
    return slice(SSD_CHUNK * u, SSD_CHUNK * (u + 1))


def _ssd_specs(rev, ng):
    ci = (lambda i: ng - 1 - i) if rev else (lambda i: i)
    n = SSD_STEP * SSD_CHUNK
    act = [pl.BlockSpec((n, 512), lambda i: (ci(i), AB_Z // 512)),
           pl.BlockSpec((n, 512), lambda i: (ci(i), 0)),
           pl.BlockSpec((n, LANES), lambda i: (ci(i), 4)),
           pl.BlockSpec((n, LANES), lambda i: (ci(i), 5)),
           pl.BlockSpec((n, LANES), lambda i: (ci(i), AB_DT // LANES))]
    const = [pl.BlockSpec((SSD_CHUNK, SSD_CHUNK), lambda i: (0, 0)), pl.BlockSpec((LANES, 512), lambda i: (0, 0)),
             pl.BlockSpec((LANES, 512), lambda i: (0, 0))]
    par = [pl.BlockSpec((1, LANES), lambda i: (0, 0))] * 3 + [pl.BlockSpec((1, 512), lambda i: (0, 0))]
    return ci, act, const, par


def _ssd_fwd(h, xbcc, dtb, alog, dsk, nw, name):
    s = h.shape[0]
    nc = s // SSD_CHUNK
    ng = nc // SSD_STEP
    ci, act, const, par = _ssd_specs(False, ng)

    def kern(z_ref, xs_ref, bm_ref, cm_ref, dt_ref, tri_ref, e_ref, bd_ref, dtb_ref, al_ref, dsk_ref, nw_ref,
             y_ref, sts_ref, st):
        @pl.when(pl.program_id(0) == 0)
        def _():
            st[...] = jnp.zeros_like(st)

        for u in range(SSD_STEP):
            rows = _ssd_rows(u)
            sts_ref[u] = st[...]
            y, stn = _ssd_chunk(tri_ref[...], e_ref[...], bd_ref[...], z_ref[rows, :], xs_ref[rows, :], bm_ref[rows, :],
                                cm_ref[rows, :], dt_ref[rows, :], st[...], dtb_ref[...], al_ref[...], dsk_ref[...], nw_ref[...])
            y_ref[rows, :] = y.astype(BF16)
            st[...] = stn

    return pl.pallas_call(
        kern, name=name, grid=(ng,), in_specs=act + const + par,
        out_specs=[pl.BlockSpec((SSD_STEP * SSD_CHUNK, 512), lambda i: (i, 0)),
                   pl.BlockSpec((SSD_STEP, LANES, 512), lambda i: (i, 0, 0))],
        out_shape=[jax.ShapeDtypeStruct((s, 512), BF16), jax.ShapeDtypeStruct((nc, LANES, 512), F32)],
        scratch_shapes=[pltpu.VMEM((LANES, 512), F32)], compiler_params=_cp(("arbitrary",)),
    )(h, xbcc, xbcc, xbcc, h, *_ssd_consts(), dtb, alog, dsk, nw)


def _acc(ref, val, first):
    @pl.when(first)
    def _():
        ref[...] = val

    @pl.when(jnp.logical_not(first))
    def _():
        ref[...] += val


def _ssd_bwd(dy, dy_cb, h, xbcc, sts, dtb, alog, dsk, nw, name):
    s = h.shape[0]
    ng = s // SSD_CHUNK // SSD_STEP
    n = SSD_STEP * SSD_CHUNK
    ci, act, const, par = _ssd_specs(True, ng)

    def kern(dy_ref, sts_ref, z_ref, xs_ref, bm_ref, cm_ref, dt_ref, tri_ref, e_ref, bd_ref, dtb_ref, al_ref, dsk_ref,
             nw_ref, dz_ref, dx_ref, ddt_ref, ddtb_ref, dal_ref, ddsk_ref, dnw_ref, dst):
        @pl.when(pl.program_id(0) == 0)
        def _():
            dst[...] = jnp.zeros_like(dst)
            for ref in (ddtb_ref, dal_ref, ddsk_ref, dnw_ref):
                ref[...] = jnp.zeros_like(ref)

        f = functools.partial(_ssd_chunk, tri_ref[...], e_ref[...], bd_ref[...])
        for u in range(SSD_STEP - 1, -1, -1):
            rows = _ssd_rows(u)
            _, pull = jax.vjp(f, z_ref[rows, :], xs_ref[rows, :], bm_ref[rows, :], cm_ref[rows, :], dt_ref[rows, :],
                              sts_ref[u], dtb_ref[...], al_ref[...], dsk_ref[...], nw_ref[...])
            dz, dxs, dbm, dcm, ddt, dsti, ddtb, dal, ddsk, dnw = pull((dy_ref[rows, :], dst[...]))
            dz_ref[rows, :] = dz.astype(BF16)
            dx_ref[rows, 0:512] = dxs
            dx_ref[rows, 512:640] = dbm
            dx_ref[rows, 640:768] = dcm
            ddt_ref[rows, :] = ddt.astype(BF16)
            dst[...] = dsti
            ddtb_ref[...] += ddtb
            dal_ref[...] += dal
            ddsk_ref[...] += ddsk
            dnw_ref[...] += dnw

    vec = pl.BlockSpec((1, LANES), lambda i: (0, 0))
    return pl.pallas_call(
        kern, name=name, grid=(ng,),
        in_specs=[pl.BlockSpec((n, 512), lambda i: (ci(i), dy_cb)), pl.BlockSpec((SSD_STEP, LANES, 512), lambda i: (ci(i), 0, 0))]
        + act + const + par,
        out_specs=[pl.BlockSpec((n, 512), lambda i: (ci(i), 0)), pl.BlockSpec((n, SSD_CONV_DIM), lambda i: (ci(i), 0)),
                   pl.BlockSpec((n, LANES), lambda i: (ci(i), 0)), vec, vec, vec, pl.BlockSpec((1, 512), lambda i: (0, 0))],
        out_shape=[jax.ShapeDtypeStruct((s, 512), BF16), jax.ShapeDtypeStruct((s, SSD_CONV_DIM), F32),
                   jax.ShapeDtypeStruct((s, LANES), BF16), jax.ShapeDtypeStruct((1, LANES), F32),
                   jax.ShapeDtypeStruct((1, LANES), F32), jax.ShapeDtypeStruct((1, LANES), F32),
                   jax.ShapeDtypeStruct((1, 512), F32)],
        scratch_shapes=[pltpu.VMEM((LANES, 512), F32)], compiler_params=_cp(("arbitrary",)),
    )(dy, sts, h, xbcc, xbcc, xbcc, h, *_ssd_consts(), dtb, alog, dsk, nw)


HG_LEVELS = 6


HG_SUMS = 1 + 2 * HG_LEVELS


def _hg_consts():
    n = HG_CHUNK
    t = np.arange(n)
    mats = [np.tril(np.ones((n, n), np.float32))]
    dq, dk, mk = [], [], []
    for lv in range(HG_LEVELS):
        b = 1 << lv
        start = (t // b) * b
        end = start + b - 1
        dq.append(((t[None, :] >= start[:, None]) & (t[None, :] <= t[:, None])).astype(np.float32))
        dk.append(((t[None, :] > t[:, None]) & (t[None, :] <= end[:, None])).astype(np.float32))
        mk.append((((t // b) % 2 == 1)[:, None] & ((t // b)[None, :] == (t // b)[:, None] - 1)).astype(np.float32))
    c1 = np.concatenate(mats + dq + dk, axis=0)
    return (jnp.asarray(c1, BF16), jnp.asarray(np.concatenate([c1, c1, c1], axis=1), BF16), jnp.asarray(np.stack(mk)))


@functools.partial(jax.custom_vjp, nondiff_argnums=(1, 2))
def _split_grid(x, nr, nc):
    br, bc = x.shape[0] // nr, x.shape[1] // nc
    return tuple(x[i * br:(i + 1) * br, j * bc:(j + 1) * bc] for i in range(nr) for j in range(nc))


def _split_grid_fwd(x, nr, nc):
    return _split_grid(x, nr, nc), None


def _split_grid_bwd(nr, nc, _, cts):
    rows = [cts[i * nc] if nc == 1 else jnp.concatenate(cts[i * nc:(i + 1) * nc], axis=1) for i in range(nr)]
    return (rows[0] if nr == 1 else jnp.concatenate(rows, axis=0),)


_split_grid.defvjp(_split_grid_fwd, _split_grid_bwd)


@jax.custom_vjp
def _level_sums(c1, c3, g):
    gh = g.astype(BF16)
    r1 = g - gh.astype(F32)
    gm = r1.astype(BF16)
    gl = (r1 - gm.astype(F32)).astype(BF16)
    return _dot(c3, jnp.concatenate([gh, gm, gl], axis=0), NN)


def _level_sums_fwd(c1, c3, g):
    return _level_sums(c1, c3, g), (c1, c3)


def _level_sums_bwd(res, ct):
    c1, c3 = res
    ch = ct.astype(BF16)
    cm = (ct - ch.astype(F32)).astype(BF16)
    return jnp.zeros_like(c1), jnp.zeros_like(c3), _dot(c1, ch, TN) + _dot(c1, cm, TN)


_level_sums.defvjp(_level_sums_fwd, _level_sums_bwd)


def _hg_chunk(c1, c3, masks, hq, hf, hi, hg, sts, lb, nw):
    q = _silu(hq)
    g = jnp.log(lb + (1.0 - lb) * jax.nn.sigmoid(hf))
    k = (1.0 - lb) * jax.nn.sigmoid(-hf)
    parts = _split_grid(_level_sums(c1, c3, g), HG_SUMS, 1)
    bc = parts[0]
    tot = jnp.sum(g, 0, keepdims=True)
    qe = _split_grid(q * jnp.exp(bc), 1, HG_HEADS)
    kd = _split_grid(k * jnp.exp(tot - bc), 1, HG_HEADS)
    etot = _split_grid(jnp.exp(tot), 1, HG_HEADS)
    qh, kh, vh, gh = [_split_grid(a, 1, HG_HEADS) for a in (q, k, hi, hg)]
    att = [None] * HG_HEADS
    for lv in range(HG_LEVELS):
        qt = _split_grid(q * jnp.exp(parts[1 + lv]), 1, HG_HEADS)
        kt = _split_grid(k * jnp.exp(parts[1 + HG_LEVELS + lv]), 1, HG_HEADS)
        for hd in range(HG_HEADS):
            t = masks[lv] * _bdot(qt[hd], kt[hd], NT)
            att[hd] = t if att[hd] is None else att[hd] + t
    outs, stn = [], []
    for hd in range(HG_HEADS):
        o = (_bdot(att[hd], vh[hd]) + jnp.sum(qh[hd] * kh[hd], 1, keepdims=True) * vh[hd] + _bdot(qe[hd], sts[hd], NT))
        stn.append(sts[hd] * etot[hd] + _bdot(vh[hd], kd[hd], TN))
        outs.append(o * lax.rsqrt(jnp.mean(o * o, 1, keepdims=True) + RMS_EPS) * nw * _silu(gh[hd]))
    return tuple(outs), tuple(stn)


HG_STEP = 4


def _hg_specs(rev, ng):
    ci = (lambda i: ng - 1 - i) if rev else (lambda i: i)
    n = HG_CHUNK
    act = [pl.BlockSpec((HG_STEP * n, 512), functools.partial(lambda i, c: (ci(i), c), c=c))
           for c in (AB_HQ // 512, AB_HF // 512, AB_HI // 512, AB_HG // 512)]
    other = [pl.BlockSpec((HG_SUMS * n, n), lambda i: (0, 0)), pl.BlockSpec((HG_SUMS * n, 3 * n), lambda i: (0, 0)),
             pl.BlockSpec((HG_LEVELS, n, n), lambda i: (0, 0, 0)), pl.BlockSpec((1, 512), lambda i: (0, 0)),
             pl.BlockSpec((1, LANES), lambda i: (0, 0))]
    return ci, act, other


def _head_rows(hd):
    return slice(LANES * hd, LANES * (hd + 1))


def _chunk_rows(u):
    return slice(HG_CHUNK * u, HG_CHUNK * (u + 1))


def _hg_fwd(h, lb, nw, name):
    s = h.shape[0]
    nc = s // HG_CHUNK
    ng = nc // HG_STEP
    ci, act, other = _hg_specs(False, ng)

    def kern(q_ref, f_ref, i_ref, g_ref, c1_ref, c3_ref, mk_ref, lb_ref, nw_ref, y_ref, sts_ref, st):
        @pl.when(pl.program_id(0) == 0)
        def _():
            st[...] = jnp.zeros_like(st)

        masks = [mk_ref[lv] for lv in range(HG_LEVELS)]
        for u in range(HG_STEP):
            rows = _chunk_rows(u)
            sts_ref[u] = st[...]
            outs, stn = _hg_chunk(c1_ref[...], c3_ref[...], masks, q_ref[rows, :], f_ref[rows, :], i_ref[rows, :],
                                  g_ref[rows, :], tuple(st[_head_rows(hd), :] for hd in range(HG_HEADS)), lb_ref[...],
                                  nw_ref[...])
            for hd in range(HG_HEADS):
                y_ref[rows, _head_rows(hd)] = outs[hd].astype(BF16)
                st[_head_rows(hd), :] = stn[hd]

    return pl.pallas_call(
        kern, name=name, grid=(ng,), in_specs=act + other,
        out_specs=[pl.BlockSpec((HG_STEP * HG_CHUNK, 512), lambda i: (i, 0)),
                   pl.BlockSpec((HG_STEP, 512, LANES), lambda i: (i, 0, 0))],
        out_shape=[jax.ShapeDtypeStruct((s, 512), BF16), jax.ShapeDtypeStruct((nc, 512, LANES), F32)],
        scratch_shapes=[pltpu.VMEM((512, LANES), F32)], compiler_params=_cp(("arbitrary",)),
    )(h, h, h, h, *_hg_consts(), lb, nw)


def _hg_bwd(dy, dy_cb, h, sts, lb, nw, name):
    s = h.shape[0]
    ng = s // HG_CHUNK // HG_STEP
    n = HG_STEP * HG_CHUNK
    ci, act, other = _hg_specs(True, ng)

    def kern(dy_ref, sts_ref, q_ref, f_ref, i_ref, g_ref, c1_ref, c3_ref, mk_ref, lb_ref, nw_ref,
             dq_ref, df_ref, di_ref, dg_ref, dlb_ref, dnw_ref, dst):
        @pl.when(pl.program_id(0) == 0)
        def _():
            dst[...] = jnp.zeros_like(dst)
            dlb_ref[...] = jnp.zeros_like(dlb_ref)
            dnw_ref[...] = jnp.zeros_like(dnw_ref)

        masks = [mk_ref[lv] for lv in range(HG_LEVELS)]
        heads = range(HG_HEADS)
        for u in range(HG_STEP - 1, -1, -1):
            rows = _chunk_rows(u)
            _, pull = jax.vjp(functools.partial(_hg_chunk, c1_ref[...], c3_ref[...], masks), q_ref[rows, :], f_ref[rows, :],
                              i_ref[rows, :], g_ref[rows, :], tuple(sts_ref[u, _head_rows(hd), :] for hd in heads),
                              lb_ref[...], nw_ref[...])
            dq, df, di, dg, dsti, dlb, dnw = pull((tuple(dy_ref[rows, _head_rows(hd)] for hd in heads),
                                                   tuple(dst[_head_rows(hd), :] for hd in heads)))
            dq_ref[rows, :] = dq.astype(BF16)
            df_ref[rows, :] = df.astype(BF16)
            di_ref[rows, :] = di.astype(BF16)
            dg_ref[rows, :] = dg.astype(BF16)
            for hd in heads:
                dst[_head_rows(hd), :] = dsti[hd]
            dlb_ref[...] += dlb
            dnw_ref[...] += dnw

    blk = pl.BlockSpec((n, 512), lambda i: (ci(i), 0))
    return pl.pallas_call(
        kern, name=name, grid=(ng,),
        in_specs=[pl.BlockSpec((n, 512), lambda i: (ci(i), dy_cb)), pl.BlockSpec((HG_STEP, 512, LANES), lambda i: (ci(i), 0, 0))]
        + act + other,
        out_specs=[blk, blk, blk, blk, pl.BlockSpec((1, 512), lambda i: (0, 0)), pl.BlockSpec((1, LANES), lambda i: (0, 0))],
        out_shape=[jax.ShapeDtypeStruct((s, 512), BF16)] * 4 + [jax.ShapeDtypeStruct((1, 512), F32),
                                                                jax.ShapeDtypeStruct((1, LANES), F32)],
        scratch_shapes=[pltpu.VMEM((512, LANES), F32)], compiler_params=_cp(("arbitrary",)),
    )(dy, sts, h, h, h, h, *_hg_consts(), lb, nw)


def _swa_consts():
    n = SWA_BLOCK
    r = np.arange(8 * n)
    t, blk = r % n, r // n
    lanes = np.arange(LANES)
    own = (t[:, None] >= lanes[None, :]).astype(np.float32)
    prev = (lanes[None, :] > t[:, None]).astype(np.float32)
    sel = (lanes[None, :] == (blk // 2 + 4 * (blk % 2))[:, None]).astype(np.float32)
    grp = ((lanes[None, :] // 64) == (blk % 2)[:, None]).astype(np.float32)
    return tuple(jnp.asarray(a) for a in (own, prev, sel, grp))


def _swa_block(own, prev, sel, grp, q, k0, v0, k1, v1, srow):
    qs = _split_grid(q, 1, 4)
    qa = jnp.concatenate([qs[p] for p in range(4) for _ in range(2)], axis=0) * grp
    s0 = jnp.where(prev > 0.0, _bdot(qa, k0, NT) * 0.125, MASK_VALUE)
    s1 = jnp.where(own > 0.0, _bdot(qa, k1, NT) * 0.125, MASK_VALUE)
    sk = jnp.sum(sel * srow, 1, keepdims=True)
    m = lax.stop_gradient(jnp.maximum(jnp.maximum(jnp.max(s0, 1, keepdims=True), jnp.max(s1, 1, keepdims=True)), sk))
    p0, p1 = jnp.exp(s0 - m), jnp.exp(s1 - m)
    inv = 1.0 / (jnp.sum(p0, 1, keepdims=True) + jnp.sum(p1, 1, keepdims=True) + jnp.exp(sk - m))
    o = _split_grid((_bdot(p0 * inv, v0) + _bdot(p1 * inv, v1)) * grp, 8, 1)
    return tuple(o[2 * p] + o[2 * p + 1] for p in range(4))


SWA_STEP = 2


def _swa_specs():
    n = SWA_BLOCK
    pv = lambda i: jnp.maximum(SWA_STEP * i - 1, 0)
    mask = pl.BlockSpec((8 * n, LANES), lambda i: (0, 0))
    return [mask, mask, mask, mask, pl.BlockSpec((SWA_STEP * n, 512), lambda i: (i, CD_Q // 512)),
            pl.BlockSpec((n, LANES), lambda i: (pv(i), CD_K // LANES)), pl.BlockSpec((n, LANES), lambda i: (pv(i), CD_V // LANES)),
            pl.BlockSpec((SWA_STEP * n, LANES), lambda i: (i, CD_K // LANES)),
            pl.BlockSpec((SWA_STEP * n, LANES), lambda i: (i, CD_V // LANES)), pl.BlockSpec((1, LANES), lambda i: (0, 0))]


def _swa_args(refs, u, notfirst):
    own, prev, sel, grp = (r[...] for r in refs[:4])
    q_ref, k0_ref, v0_ref, k_ref, v_ref, s_ref = refs[4:]
    rows = slice(SWA_BLOCK * u, SWA_BLOCK * (u + 1))
    if u == 0:
        k0, v0, prev = k0_ref[...], v0_ref[...], prev * notfirst.astype(F32)
    else:
        before = slice(SWA_BLOCK * (u - 1), SWA_BLOCK * u)
        k0, v0 = k_ref[before, :], v_ref[before, :]
    return (own, prev, sel, grp), (q_ref[rows, :], k0, v0, k_ref[rows, :], v_ref[rows, :], s_ref[...])


def _swa_fwd(hc, srow, name):
    s = hc.shape[0]
    n = SWA_STEP * SWA_BLOCK

    def kern(*refs):
        y_ref = refs[10]
        for u in range(SWA_STEP):
            consts, acts = _swa_args(refs[:10], u, pl.program_id(0) > 0)
            outs = _swa_block(*consts, *acts)
            for p in range(4):
                y_ref[SWA_BLOCK * u:SWA_BLOCK * (u + 1), LANES * p:LANES * (p + 1)] = outs[p].astype(BF16)

    return pl.pallas_call(
        kern, name=name, grid=(s // n,), in_specs=_swa_specs(), out_specs=pl.BlockSpec((n, 512), lambda i: (i, 0)),
        out_shape=jax.ShapeDtypeStruct((s, 512), BF16), compiler_params=_cp(("parallel",)),
    )(*_swa_consts(), hc, hc, hc, hc, hc, srow)


def _swa_bwd(dy, dy_cb, hc, srow, name):
    s = hc.shape[0]
    n = SWA_STEP * SWA_BLOCK
    nb = SWA_BLOCK

    def kern(dy_ref, *refs):
        dq_ref, dk_ref, dv_ref, ds_ref = refs[10:]
        i = pl.program_id(0)

        @pl.when(i == 0)
        def _():
            dk_ref[...] = jnp.zeros_like(dk_ref)
            dv_ref[...] = jnp.zeros_like(dv_ref)
            ds_ref[...] = jnp.zeros_like(ds_ref)

        for u in range(SWA_STEP):
            blk = SWA_STEP * i + u
            rows = slice(nb * u, nb * (u + 1))
            r1 = pl.ds(pl.multiple_of(blk * nb, nb), nb)
            r0 = pl.ds(pl.multiple_of(jnp.maximum(blk - 1, 0) * nb, nb), nb)
            consts, acts = _swa_args(refs[:10], u, i > 0)
            _, pull = jax.vjp(functools.partial(_swa_block, *consts), *acts)
            dq, dk0, dv0, dk1, dv1, dsr = pull(tuple(dy_ref[rows, LANES * p:LANES * (p + 1)] for p in range(4)))
            dq_ref[rows, :] = dq.astype(BF16)
            dk_ref[r0, :] += dk0
            dv_ref[r0, :] += dv0
            dk_ref[r1, :] += dk1
            dv_ref[r1, :] += dv1
            ds_ref[...] += dsr

    full = pl.BlockSpec((s, LANES), lambda i: (0, 0))
    return pl.pallas_call(
        kern, name=name, grid=(s // n,), in_specs=[pl.BlockSpec((n, 512), lambda i: (i, dy_cb))] + _swa_specs(),
        out_specs=[pl.BlockSpec((n, 512), lambda i: (i, 0)), full, full, pl.BlockSpec((1, LANES), lambda i: (0, 0))],
        out_shape=[jax.ShapeDtypeStruct((s, 512), BF16), jax.ShapeDtypeStruct((s, LANES), F32),
                   jax.ShapeDtypeStruct((s, LANES), F32), jax.ShapeDtypeStruct((1, LANES), F32)],
        compiler_params=_cp(("arbitrary",)),
    )(dy, *_swa_consts(), hc, hc, hc, hc, hc, srow)


RG_TILE = 512


def _neg_expm1(x):
    ser = x * (1 + x / 2 * (1 + x / 3 * (1 + x / 4 * (1 + x / 5 * (1 + x / 6 * (1 + x / 7 * (1 + x / 8)))))))
    return jnp.where(x > -0.25, -ser, 1.0 - jnp.exp(x))


def _rg_gates(xc, wa, wx, ba, bx, lam):
    r = jax.nn.sigmoid(_bdot(xc, wa) + ba)
    i = jax.nn.sigmoid(_bdot(xc, wx) + bx)
    log_a = -RG_C * r * jax.nn.softplus(-lam)
    u = jnp.sqrt(jnp.maximum(_neg_expm1(2.0 * log_a), 0.0)) * (i * xc)
    return jnp.exp(log_a), u


def _rg_out(hs, gate):
    return hs * jax.nn.gelu(gate)


def _rows_to_tile(rows):
    sub = _iota((SUBLANES, 1), 0)
    out = jnp.broadcast_to(rows[0], (SUBLANES, rows[0].shape[1]))
    for j in range(1, SUBLANES):
        out = jnp.where(sub == j, rows[j], out)
    return out


def _rg_fwd(xc, hc, wa, wx, ba, bx, lam, name):
    s = xc.shape[0]
    tm = min(RG_TILE, s)

    def kern(x_ref, g_ref, wa_ref, wx_ref, ba_ref, bx_ref, lam_ref, y_ref, h_ref, a_s, u_s, carry):
        @pl.when(pl.program_id(0) == 0)
        def _():
            carry[...] = jnp.zeros_like(carry)

        a, u = _rg_gates(x_ref[...], wa_ref[...], wx_ref[...], ba_ref[...], bx_ref[...], lam_ref[...])
        a_s[...] = a
        u_s[...] = u

        def body(gi, hp):
            r0 = pl.multiple_of(gi * SUBLANES, SUBLANES)
            ab, ub = a_s[pl.ds(r0, SUBLANES), :], u_s[pl.ds(r0, SUBLANES), :]
            rows = []
            for j in range(SUBLANES):
                hp = ab[j:j + 1] * hp + ub[j:j + 1]
                rows.append(hp)
            h_ref[pl.ds(r0, SUBLANES), :] = _rows_to_tile(rows)
            return hp

        carry[0:1, :] = lax.fori_loop(0, tm // SUBLANES, body, carry[0:1, :])
        y_ref[...] = _rg_out(h_ref[...], g_ref[...]).astype(BF16)

    row = pl.BlockSpec((tm, 512), lambda i: (i, 0))
    mat = pl.BlockSpec((512, 512), lambda i: (0, 0))
    vec = pl.BlockSpec((1, 512), lambda i: (0, 0))
    return pl.pallas_call(
        kern, name=name, grid=(s // tm,),
        in_specs=[row, pl.BlockSpec((tm, 512), lambda i: (i, CD_GATE // 512)), mat, mat, vec, vec, vec],
        out_specs=[row, row],
        out_shape=[jax.ShapeDtypeStruct((s, 512), BF16), jax.ShapeDtypeStruct((s, 512), F32)],
        scratch_shapes=[pltpu.VMEM((tm, 512), F32), pltpu.VMEM((tm, 512), F32), pltpu.VMEM((SUBLANES, 512), F32)],
        compiler_params=_cp(("arbitrary",)),
    )(xc, hc, wa, wx, ba, bx, lam)


def _rg_bwd(dy, dy_cb, xc, hc, hs, wa, wx, ba, bx, lam, name):
    s = xc.shape[0]
    tm = min(RG_TILE, s)
    nt = s // tm
    ti = lambda i: nt - 1 - i

    def kern(dy_ref, x_ref, g_ref, h_ref, hp_ref, wa_ref, wx_ref, ba_ref, bx_ref, lam_ref,
             dx_ref, dg_ref, dwa_ref, dwx_ref, dba_ref, dbx_ref, dlam_ref, a_s, d_s, g_s, cg, ca):
        first = pl.program_id(0) == 0

        @pl.when(first)
        def _():
            cg[...] = jnp.zeros_like(cg)
            ca[...] = jnp.zeros_like(ca)

        (a, _), pull = jax.vjp(_rg_gates, x_ref[...], wa_ref[...].astype(F32), wx_ref[...].astype(F32), ba_ref[...],
                               bx_ref[...], lam_ref[...])
        hs_t = h_ref[...]
        _, pull_out = jax.vjp(_rg_out, hs_t, g_ref[...])
        dho, dgate = pull_out(dy_ref[...])
        dg_ref[...] = dgate.astype(BF16)
        a_s[...] = a
        d_s[...] = dho

        def body(k, c):
            gn, an = c
            r0 = pl.multiple_of((tm // SUBLANES - 1 - k) * SUBLANES, SUBLANES)
            ab, db = a_s[pl.ds(r0, SUBLANES), :], d_s[pl.ds(r0, SUBLANES), :]
            rows = [None] * SUBLANES
            for j in range(SUBLANES - 1, -1, -1):
                gn = db[j:j + 1] + an * gn
                an = ab[j:j + 1]
                rows[j] = gn
            g_s[pl.ds(r0, SUBLANES), :] = _rows_to_tile(rows)
            return gn, an

        gn, an = lax.fori_loop(0, tm // SUBLANES, body, (cg[0:1, :], ca[0:1, :]))
        cg[0:1, :] = gn
        ca[0:1, :] = an
        g = g_s[...]
        halo = jnp.where(pl.program_id(0) == nt - 1, 0.0, hp_ref[...])
        hprev = pltpu.roll(jnp.concatenate([halo, hs_t], axis=0), 1, axis=0)[SUBLANES:]
        dxc, dwa, dwx, dba, dbx, dlam = pull((g * hprev, g))
        dx_ref[...] = dxc
        _acc(dwa_ref, dwa, first)
        _acc(dwx_ref, dwx, first)
        _acc(dba_ref, dba, first)
        _acc(dbx_ref, dbx, first)
        _acc(dlam_ref, dlam, first)

    row = pl.BlockSpec((tm, 512), lambda i: (ti(i), 0))
    mat = pl.BlockSpec((512, 512), lambda i: (0, 0))
    vec = pl.BlockSpec((1, 512), lambda i: (0, 0))
    hprev = pl.BlockSpec((SUBLANES, 512), lambda i: (jnp.maximum(ti(i) * (tm // SUBLANES) - 1, 0), 0))
    return pl.pallas_call(
        kern, name=name, grid=(nt,),
        in_specs=[pl.BlockSpec((tm, 512), lambda i: (ti(i), dy_cb)), row,
                  pl.BlockSpec((tm, 512), lambda i: (ti(i), CD_GATE // 512)), row, hprev, mat, mat, vec, vec, vec],
        out_specs=[row, row, mat, mat, vec, vec, vec],
        out_shape=[jax.ShapeDtypeStruct((s, 512), F32), jax.ShapeDtypeStruct((s, 512), BF16),
                   jax.ShapeDtypeStruct((512, 512), F32), jax.ShapeDtypeStruct((512, 512), F32),
                   jax.ShapeDtypeStruct((1, 512), F32), jax.ShapeDtypeStruct((1, 512), F32), jax.ShapeDtypeStruct((1, 512), F32)],
        scratch_shapes=[pltpu.VMEM((tm, 512), F32)] * 3 + [pltpu.VMEM((SUBLANES, 512), F32)] * 2,
        compiler_params=_cp(("arbitrary",)),
    )(dy, xc, hc, hs, hs, wa, wx, ba, bx, lam)


AB_REF = dict(z=(0, 512), xbc=(512, 768), dt=(1280, 8), hq=(1288, 512), hf=(1800, 512), hi=(2312, 512), hg=(2824, 512))
AB_INT = dict(z=AB_Z, hq=AB_HQ, hf=AB_HF, hi=AB_HI, hg=AB_HG, xbc=AB_XBC, dt=AB_DT)
CD_REF = dict(q=(0, 512), k=(512, 128), v=(640, 128), gate=(768, 512), xr=(1280, 512))


def _cols(w, start, size):
    return lax.slice_in_dim(w, start, start + size, axis=w.ndim - 1)


def _pad_lanes(v):
    return jnp.concatenate([v.astype(F32), jnp.zeros((LANES - v.shape[0],), F32)]).reshape(1, LANES)


def _taker(g, layer, axis):
    width = g.shape[2 + axis]

    def take(start, size):
        out = []
        while size > 0:
            k, off = divmod(start, width)
            n = min(size, width - off)
            out.append(lax.slice_in_dim(g[k, layer], off, off + n, axis=axis))
            start, size = start + n, size - n
        return out

    return take


def _ab_in_to_int(g, layer):
    take = _taker(g, layer, 1)
    parts = sum([take(*AB_REF[k]) for k in ("z", "hq", "hf", "hi", "hg", "xbc", "dt")], [])
    return jnp.concatenate(parts + [jnp.zeros((g.shape[2], AB_COLS - AB_DT - 8), g.dtype)], axis=1)


def _ab_in_to_ref(w):
    return jnp.concatenate([_cols(w, AB_INT[k], AB_REF[k][1]) for k in ("z", "xbc", "dt", "hq", "hf", "hi", "hg")], axis=-1)


def _pairs_to_ref(w, axis):
    sl = lambda i: lax.slice_in_dim(w, 64 * i, 64 * i + 64, axis=axis)
    return jnp.concatenate([sl(2 * h) for h in range(4)] + [sl(2 * h + 1) for h in range(4)], axis=axis)


PAIR_ORDER = tuple(h for p in range(4) for h in (p, 4 + p))


def _cd_in_to_int(g, layer):
    take = _taker(g, layer, 1)
    q = sum([take(64 * h, 64) for h in PAIR_ORDER], [])
    return jnp.concatenate(q + sum([take(*CD_REF[k]) for k in ("gate", "xr", "k", "v")], []), axis=1)


def _cd_out_to_int(g, layer):
    take = _taker(g, layer, 0)
    return jnp.concatenate(sum([take(64 * h, 64) for h in PAIR_ORDER], []) + take(512, 512), axis=0)


def _whole(g, layer, axis):
    return jnp.concatenate([g[k, layer] for k in range(4)], axis=axis)


def _cd_in_to_ref(w):
    q = _pairs_to_ref(_cols(w, CD_Q, 512), w.ndim - 1)
    return jnp.concatenate([q, _cols(w, CD_K, 128), _cols(w, CD_V, 128), _cols(w, CD_GATE, 512), _cols(w, CD_XR, 512)], axis=-1)


def _block_diag(w):
    tiled = jnp.concatenate([w.reshape(RG_WIDTH, 64)] * 8, axis=1)
    return tiled * _BLOCK_DIAG_MASK.astype(w.dtype)


_BLOCK_DIAG_MASK = np.kron(np.eye(8, dtype=np.float32), np.ones((64, 64), np.float32))


def _diag_blocks(m):
    return jnp.sum((m * _BLOCK_DIAG_MASK).reshape(8, 64, 8, 64), axis=2)


def _conv8(w):
    return jnp.concatenate([w, jnp.zeros((SUBLANES - w.shape[0], w.shape[1]), w.dtype)], axis=0)


def _hg_lower_bounds(hg_lower):
    sm = jax.nn.softmax(hg_lower.astype(F32), axis=0)
    return jnp.clip(jnp.cumsum(sm, axis=0) - sm[0], 0.0, 1.0)


def _local_step(x, target, w):
    s = x.shape[0]
    gup = w["ffn_w_up"]
    wdowns = [_whole(w["ffn_w_down"], l, 0) for l in range(DEPTH)]
    lb_all, lb_pull = jax.vjp(_hg_lower_bounds, w["hg_lower"])
    grads = {k: [None] * (v.shape[1] if k in BIG else v.shape[0]) for k, v in w.items()}
    saved = []
    xb = x.astype(BF16)
    for l in range(DEPTH):
        j = l // 2
        t = f"l{l}"
        sv = dict(x=x, xb=xb)
        if l % 2 == 0:
            win = _ab_in_to_int(w["ab_w_in"], j)
            wout = _whole(w["ab_w_out"], j, 0)
            h = _mm([(xb, 0, win, 0, D_MODEL)], AB_COLS, F32, t + "_in")
            w8 = _conv8(w["ssd_conv_w"][j])
            xbcc = _dwconv_fwd(h, AB_XBC // CONV_CB, SSD_CONV_DIM, w8, w["ssd_conv_b"][j], 4, t + "_conv")
            sp = (_pad_lanes(w["ssd_dt_bias"][j]), _pad_lanes(w["ssd_a_log"][j]), _pad_lanes(w["ssd_d"][j]),
                  w["ssd_norm_w"][j].reshape(1, 512))
            ya, ssts = _ssd_fwd(h, xbcc, *sp, t + "_ssd")
            hp = (lb_all[j].reshape(1, 512), w["hg_norm_w"][j].reshape(1, LANES))
            yb, hsts = _hg_fwd(h, *hp, t + "_hg")
            sv.update(win=win, wout=wout, h=h, w8=w8, xbcc=xbcc, sp=sp, ssts=ssts, hp=hp, hsts=hsts)
        else:
            win = _cd_in_to_int(w["cd_w_in"], j)
            wout = _cd_out_to_int(w["cd_w_out"], j)
            h = _mm([(xb, 0, win, 0, D_MODEL)], CD_COLS, F32, t + "_in")
            srow = _pad_lanes(w["swa_sinks"][j])
            ya = _swa_fwd(h, srow, t + "_swa")
            w8 = _conv8(w["rg_conv_w"][j])
            xc = _dwconv_fwd(h, CD_XR // CONV_CB, RG_WIDTH, w8, w["rg_conv_b"][j], 4, t + "_conv")
            rp = (_block_diag(w["rg_wa"][j]).astype(BF16), _block_diag(w["rg_wx"][j]).astype(BF16),
                  w["rg_ba"][j].reshape(1, 512), w["rg_bx"][j].reshape(1, 512), w["rg_lambda"][j].reshape(1, 512))
            yb, hs = _rg_fwd(xc, h, *rp, t + "_rg")
            sv.update(win=win, wout=wout, h=h, w8=w8, srow=srow, xc=xc, rp=rp, hs=hs)
        x1, x1b, r1 = _mm_ln([(ya, 0, wout, 0, 512), (yb, 0, wout, 1, 512)], x, w["ln_g"][l, 0], w["ln_b"][l, 0], t + "_out_ln")
        wdown = wdowns[l]
        fw8 = _conv8(w["ffn_conv_w"][l])
        hu = _mm([(x1b, 0, gup, _shard_as_col_tile(gup, l), D_MODEL)], 2 * FFN_DIM, BF16, t + "_up")
        a, hg, hv = _ffn_gate_fwd(hu, fw8, w["ffn_conv_b"][l], t + "_gate")
        x, xb, r2 = _mm_ln([(a, 0, wdown, 0, FFN_DIM)], x1, w["ln_g"][l, 1], w["ln_b"][l, 1], t + "_down_ln")
        sv.update(ya=ya, yb=yb, r1=r1, x1b=x1b, hu=hu, hg=hg, hv=hv, a=a, r2=r2, fw8=fw8)
        saved.append(sv)

    loss = 0.5 * jnp.sum(_sqerr(x, target, "sqerr")) / D_MODEL
    dya, dyb, ca, cb = x, target, 1.0 / D_MODEL, -1.0 / D_MODEL
    for l in range(DEPTH - 1, -1, -1):
        j = l // 2
        t = f"l{l}b"
        sv = saved[l]
        wdown = wdowns[l]
        dr2, dg2, db2 = _ln_bwd(dya, dyb, ca, cb, sv["r2"], w["ln_g"][l, 1], t + "_ln2")
        da = _mm([(dr2, 0, wdown, 0, D_MODEL)], FFN_DIM, BF16, t + "_down", trans_b=True)
        grads["ffn_w_down"][l] = _mm_tn(sv["a"], 0, FFN_DIM, dr2, 0, D_MODEL, t + "_wdown")
        dhg, dhu, dfw8, dfb = _ffn_gate_bwd(sv["hu"], sv["hg"], sv["hv"], da, sv["fw8"], t + "_gate")
        grads["ffn_conv_w"][l] = dfw8[:3]
        grads["ffn_conv_b"][l] = dfb
        quarter = gup.shape[3]
        dx1 = _mm([(dh, half, gup, _shard_transposed(gup, 2 * gi + half, l), quarter)
                   for gi, dh in enumerate((dhg, dhu)) for half in range(2)], D_MODEL, F32, t + "_up", trans_b=True)
        grads["ffn_w_up"][l] = jnp.concatenate(
            [_mm_tn(sv["x1b"], 0, D_MODEL, dh, 0, FFN_DIM, t + nm, column_tiles_apart=True)
             for dh, nm in ((dhg, "_wup_g"), (dhu, "_wup_u"))], axis=0)
        dr1, dg1, db1 = _ln_bwd(dr2, dx1, ALPHA, 1.0, sv["r1"], w["ln_g"][l, 0], t + "_ln1")
        grads["ln_g"][l] = jnp.stack([dg1, dg2])
        grads["ln_b"][l] = jnp.stack([db1, db2])
        win, wout, h = sv["win"], sv["wout"], sv["h"]
        dycat = _mm([(dr1, 0, wout, 0, D_MODEL)], D_MODEL, F32, t + "_out", trans_b=True)
        dwout = jnp.concatenate([_mm_tn(sv["ya"], 0, 512, dr1, 0, D_MODEL, t + "_wout_a"),
                                 _mm_tn(sv["yb"], 0, 512, dr1, 0, D_MODEL, t + "_wout_b")], axis=0)
        if l % 2 == 0:
            dz, dxbcc, ddt, ddtb, dal, ddsk, dnw = _ssd_bwd(dycat, 0, h, sv["xbcc"], sv["ssts"], *sv["sp"], t + "_ssd")
            dxbc, dcw, dcb = _dwconv_bwd(dxbcc, h, AB_XBC // CONV_CB, SSD_CONV_DIM, sv["w8"], 4, t + "_conv")
            dq, df, di, dg, dlb, dhnw = _hg_bwd(dycat, 1, h, sv["hsts"], *sv["hp"], t + "_hg")
            grads["ab_w_out"][j] = dwout
            grads["ssd_conv_w"][j], grads["ssd_conv_b"][j] = dcw[:4], dcb
            grads["ssd_dt_bias"][j], grads["ssd_a_log"][j], grads["ssd_d"][j] = ddtb[0, :8], dal[0, :8], ddsk[0, :8]
            grads["ssd_norm_w"][j], grads["hg_norm_w"][j] = dnw[0], dhnw[0]
            grads["hg_lower"][j] = dlb[0]
            pieces = [(dz, 0, AB_Z, 512), (dq, 0, AB_HQ, 512), (df, 0, AB_HF, 512), (di, 0, AB_HI, 512), (dg, 0, AB_HG, 512),
                      (dxbc, 0, AB_XBC, 256), (dxbc, 1, AB_XBC + 256, 256), (dxbc, 2, AB_XBC + 512, 256), (ddt, 0, AB_DT, LANES)]
            dwin = [_mm_tn(sv["xb"], 0, D_MODEL, p, 0, p.shape[1], t + f"_win{i}")
                    for i, p in enumerate((dz, dq, df, di, dg, dxbc, ddt))]
            grads["ab_w_in"][j] = _ab_in_to_ref(jnp.concatenate(dwin, axis=1))
        else:
            dq, dk, dv, dsr = _swa_bwd(dycat, 0, h, sv["srow"], t + "_swa")
            dxc, dgate, dwa, dwx, dba, dbx, dlam = _rg_bwd(dycat, 1, sv["xc"], h, sv["hs"], *sv["rp"], t + "_rg")
            dxr, dcw, dcb = _dwconv_bwd(dxc, h, CD_XR // CONV_CB, RG_WIDTH, sv["w8"], 4, t + "_conv")
            grads["cd_w_out"][j] = jnp.concatenate([_pairs_to_ref(dwout[:512], 0), dwout[512:]], axis=0)
            grads["swa_sinks"][j] = dsr[0, :8]
            grads["rg_conv_w"][j], grads["rg_conv_b"][j] = dcw[:4], dcb
            grads["rg_wa"][j], grads["rg_wx"][j] = _diag_blocks(dwa), _diag_blocks(dwx)
            grads["rg_ba"][j], grads["rg_bx"][j], grads["rg_lambda"][j] = dba[0], dbx[0], dlam[0]
            pieces = [(dq, 0, CD_Q, 512), (dgate, 0, CD_GATE, 512), (dxr, 0, CD_XR, 512), (dk, 0, CD_K, LANES), (dv, 0, CD_V, LANES)]
            dwin = [_mm_tn(sv["xb"], 0, D_MODEL, p, 0, p.shape[1], t + f"_win{i}")
                    for i, p in enumerate((dq, dgate, dxr, dk, dv))]
            grads["cd_w_in"][j] = _cd_in_to_ref(jnp.concatenate(dwin, axis=1))
        dx0 = _mm([(p, pcb, win, off // k, k) for p, pcb, off, k in pieces], D_MODEL, F32, t + "_in", trans_b=True)
        dya, dyb, ca, cb = dr1, dx0, ALPHA, 1.0
    grad_x = _axpby(dya, dyb, ca, cb, "grad_x")
    out = {k: (v if k in BIG else jnp.stack(v)) for k, v in grads.items()}
    (out["hg_lower"],) = lb_pull(out["hg_lower"])
    return loss, grad_x, out


MESH = pl.DeviceIdType.MESH
ANY = pl.BlockSpec(memory_space=pl.ANY)


def _place():
    x, y, c = lax.axis_index("x"), lax.axis_index("y"), lax.axis_index("c")
    return x, y, c, [(1 - x, y), (x, 1 - y), (1 - x, 1 - y)]


def _rcopy(src, dst, send_sems, recv_sems, k, to):
    return pltpu.make_async_remote_copy(src_ref=src, dst_ref=dst, send_sem=send_sems.at[k], recv_sem=recv_sems.at[k],
                                        device_id=to, device_id_type=MESH)


def _zeros_index(ndim):
    return (0,) * ndim


def _gather_chips(parts, name):
    n = len(parts)

    def body(*refs):
        w_refs, out_refs, (send_sems, recv_sems) = refs[:n], refs[n:2 * n], refs[2 * n:]
        x, y, c, chips = _place()
        me, sib = 2 * x + y, (x, y, 1 - c)
        first = [_rcopy(w_refs[a].at[c], out_refs[a].at[me, c], send_sems, recv_sems, 6 * a + k, (cx, cy, c))
                 for a in range(n) for k, (cx, cy) in enumerate(chips)]
        for cp in first:
            cp.start()
        passed = []
        for a in range(n):
            for k, (cx, cy) in enumerate(chips):
                blk = out_refs[a].at[2 * cx + cy, c]
                _rcopy(blk, blk, send_sems, recv_sems, 6 * a + k, (cx, cy, c)).wait_recv()
                passed.append(_rcopy(blk, blk, send_sems, recv_sems, 6 * a + 3 + k, sib))
                passed[-1].start()
        for a in range(n):
            for k, (cx, cy) in enumerate(chips):
                blk = out_refs[a].at[2 * cx + cy, 1 - c]
                _rcopy(blk, blk, send_sems, recv_sems, 6 * a + 3 + k, sib).wait_recv()
        for cp in first + passed:
            cp.wait_send()

    outs = pl.pallas_call(
        body, name=name, in_specs=[ANY] * n, out_specs=[ANY] * n,
        out_shape=[jax.ShapeDtypeStruct((4,) + p.shape, p.dtype) for p in parts],
        scratch_shapes=[pltpu.SemaphoreType.DMA((6 * n,)), pltpu.SemaphoreType.DMA((6 * n,))],
    )(*parts)
    me = 2 * lax.axis_index("x") + lax.axis_index("y")
    return [lax.dynamic_update_slice(o, p[None], (me,) + _zeros_index(p.ndim)) for o, p in zip(outs, parts)]


def _swap_halves(gs, name):
    n = len(gs)

    def body(*refs):
        g_refs, x_refs, (send_sems, recv_sems) = refs[:n], refs[n:2 * n], refs[2 * n:]
        x, y, c, _ = _place()
        cps = [_rcopy(g_refs[a].at[:, 1 - c], x_refs[a], send_sems, recv_sems, a, (x, y, 1 - c)) for a in range(n)]
        for cp in cps:
            cp.start()
        for cp in cps:
            cp.wait()

    return pl.pallas_call(
        body, name=name, in_specs=[ANY] * n, out_specs=[ANY] * n,
        out_shape=[jax.ShapeDtypeStruct((g.shape[0],) + g.shape[2:], g.dtype) for g in gs],
        scratch_shapes=[pltpu.SemaphoreType.DMA((n,)), pltpu.SemaphoreType.DMA((n,))],
    )(*gs)


def _scatter_chips(hbs, name):
    n = len(hbs)

    def body(*refs):
        h_refs, y_refs, (send_sems, recv_sems) = refs[:n], refs[n:2 * n], refs[2 * n:]
        x, y, c, chips = _place()
        me = 2 * x + y
        sends = [_rcopy(h_refs[a].at[2 * cx + cy], y_refs[a].at[me], send_sems, recv_sems, 3 * a + k, (cx, cy, c))
                 for a in range(n) for k, (cx, cy) in enumerate(chips)]
        for cp in sends:
            cp.start()
        for a in range(n):
            for k, (cx, cy) in enumerate(chips):
                blk = y_refs[a].at[2 * cx + cy]
                _rcopy(blk, blk, send_sems, recv_sems, 3 * a + k, (cx, cy, c)).wait_recv()
        for cp in sends:
            cp.wait_send()

    outs = pl.pallas_call(
        body, name=name, in_specs=[ANY] * n, out_specs=[ANY] * n,
        out_shape=[jax.ShapeDtypeStruct(h.shape, h.dtype) for h in hbs],
        scratch_shapes=[pltpu.SemaphoreType.DMA((3 * n,)), pltpu.SemaphoreType.DMA((3 * n,))],
    )(*hbs)
    me = 2 * lax.axis_index("x") + lax.axis_index("y")
    return [lax.dynamic_update_slice(o, lax.dynamic_slice_in_dim(h, me, 1, axis=0), (me,) + _zeros_index(h.ndim - 1))
            for o, h in zip(outs, hbs)]


def _share_half(fs, name):
    n = len(fs)

    def body(*refs):
        f_refs, out_refs, (send_sems, recv_sems) = refs[:n], refs[n:2 * n], refs[2 * n:]
        x, y, c, _ = _place()
        sib = (x, y, 1 - c)
        cps = [_rcopy(f_refs[a], out_refs[a].at[c], send_sems, recv_sems, a, sib) for a in range(n)]
        for cp in cps:
            cp.start()
        for a in range(n):
            _rcopy(f_refs[a], out_refs[a].at[1 - c], send_sems, recv_sems, a, sib).wait_recv()
        for cp in cps:
            cp.wait_send()

    outs = pl.pallas_call(
        body, name=name, in_specs=[ANY] * n, out_specs=[ANY] * n,
        out_shape=[jax.ShapeDtypeStruct((2,) + f.shape, f.dtype) for f in fs],
        scratch_shapes=[pltpu.SemaphoreType.DMA((n,)), pltpu.SemaphoreType.DMA((n,))],
    )(*fs)
    c = lax.axis_index("c")
    return [lax.dynamic_update_slice(o, f[None], (c,) + _zeros_index(f.ndim)) for o, f in zip(outs, fs)]


def _gather_devices(v, name):
    m_per, n = v.shape

    def body(x_ref, out_ref, send_sems, recv_sems, local_sem):
        x, y, c, chips = _place()
        me, sib = (x, y, c), (x, y, 1 - c)

        def rows(px, py, pc):
            return out_ref.at[pl.ds((4 * px + 2 * py + pc) * m_per, m_per), :]

        def copy(k, block, to, src=None):
            return _rcopy(rows(*block) if src is None else src, rows(*block), send_sems, recv_sems, k, to)

        mine = pltpu.make_async_copy(x_ref, rows(*me), local_sem)
        mine.start()
        first = [copy(0, me, sib, src=x_ref)]
        first += [copy(1 + j, me, (*chip, c), src=x_ref) for j, chip in enumerate(chips)]
        for cp in first:
            cp.start()
        passed = [copy(4 + j, (*chip, c), sib) for j, chip in enumerate(chips)]
        for j, chip in enumerate(chips):
            copy(1 + j, (*chip, c), me).wait_recv()
            passed[j].start()
        copy(0, sib, me).wait_recv()
        for j, chip in enumerate(chips):
            copy(4 + j, (*chip, 1 - c), me).wait_recv()
        for cp in first + passed:
            cp.wait_send()
        mine.wait()

    return pl.pallas_call(
        body, name=name, out_shape=jax.ShapeDtypeStruct((8 * m_per, n), v.dtype),
        in_specs=[pl.BlockSpec(memory_space=pltpu.VMEM)], out_specs=pl.BlockSpec(memory_space=pltpu.VMEM),
        scratch_shapes=[pltpu.SemaphoreType.DMA((7,)), pltpu.SemaphoreType.DMA((7,)), pltpu.SemaphoreType.DMA(())],
        compiler_params=_cp(),
    )(v)


ROW_TILES = (512, 256, 352, 128)


def _add_half(g, xr, c, name):
    _, _, r, n = g.shape
    tr = _pick(r, ROW_TILES)

    def kern(c_ref, g_ref, x_ref, o_ref):
        o_ref[...] = (g_ref[0] + x_ref[...]).astype(BF16)

    return pl.pallas_call(
        kern, name=name,
        grid_spec=pltpu.PrefetchScalarGridSpec(
            num_scalar_prefetch=1, grid=(4, r // tr),
            in_specs=[pl.BlockSpec((1, 1, tr, n), lambda k, i, c_ref: (k, c_ref[0], i, 0)),
                      pl.BlockSpec((1, tr, n), lambda k, i, c_ref: (k, i, 0))],
            out_specs=pl.BlockSpec((1, tr, n), lambda k, i, c_ref: (k, i, 0))),
        out_shape=jax.ShapeDtypeStruct(xr.shape, BF16), compiler_params=_cp(("parallel", "parallel")),
    )(c.reshape(1).astype(jnp.int32), g, xr)


def _sum_blocks(y, name, tr=None):
    n, r, cols = y.shape
    tr = tr or _pick(r, ROW_TILES)
    tr = tr if r % tr == 0 else r

    def kern(y_ref, o_ref):
        acc = y_ref[0].astype(F32)
        for k in range(1, n):
            acc = acc + y_ref[k].astype(F32)
        o_ref[...] = acc

    return pl.pallas_call(
        kern, name=name, grid=(r // tr,), in_specs=[pl.BlockSpec((n, tr, cols), lambda i: (0, i, 0))],
        out_specs=pl.BlockSpec((tr, cols), lambda i: (i, 0)), out_shape=jax.ShapeDtypeStruct((r, cols), F32),
        compiler_params=_cp(("parallel",)),
    )(y)


def _adamw(w, g, m, v, name, tr=256):
    r, n = w.shape
    tr = tr if r % tr == 0 else r

    def kern(w_ref, g_ref, m_ref, v_ref, d_ref, nm_ref, nv_ref):
        gg = g_ref[...]
        nm = ADAM_B1 * m_ref[...] + (1.0 - ADAM_B1) * gg
        nv = ADAM_B2 * v_ref[...] + (1.0 - ADAM_B2) * (gg * gg)
        m_hat = nm / (1.0 - ADAM_B1 ** ADAM_STEP)
        v_hat = nv / (1.0 - ADAM_B2 ** ADAM_STEP)
        d_ref[...] = -ADAM_LR * (m_hat / (jnp.sqrt(v_hat) + ADAM_EPS) + ADAM_WD * w_ref[...])
        nm_ref[...] = nm
        nv_ref[...] = nv

    blk = pl.BlockSpec((tr, n), lambda i: (i, 0))
    return pl.pallas_call(
        kern, name=name, grid=(r // tr,), in_specs=[blk] * 4, out_specs=[blk] * 3,
        out_shape=[jax.ShapeDtypeStruct((r, n), F32)] * 3, compiler_params=_cp(("parallel",)),
    )(w, g, m, v)


def _rows_of(n):
    return -(-n // (LANES * SUBLANES)) * SUBLANES


def _pack(arrs, lead, dtype, mult):
    ls = arrs[0].shape[:lead]
    blocks, total = [], 0
    for a in arrs:
        f = a.astype(dtype).reshape(ls + (-1,))
        n = f.shape[-1]
        rows = _rows_of(n)
        if rows * LANES != n:
            f = jnp.concatenate([f, jnp.zeros(ls + (rows * LANES - n,), dtype)], axis=-1)
        blocks.append(f.reshape(ls + (rows, LANES)))
        total += rows
    if total % mult:
        blocks.append(jnp.zeros(ls + (mult - total % mult, LANES), dtype))
    return jnp.concatenate(blocks, axis=lead)


def _unpack(buf, lead, shapes):
    ls = buf.shape[:lead]
    out, off = [], 0
    for sh in shapes:
        n = int(np.prod(sh))
        rows = _rows_of(n)
        flat = lax.slice_in_dim(buf, off, off + rows, axis=lead).reshape(ls + (rows * LANES,))
        if rows * LANES != n:
            flat = lax.slice_in_dim(flat, 0, n, axis=lead)
        out.append(flat.reshape(ls + tuple(sh)))
        off += rows
    return out


BIG = ("ab_w_in", "ab_w_out", "cd_w_in", "cd_w_out", "ffn_w_up", "ffn_w_down")
BIG_COLS = ("ab_w_in", "cd_w_in", "ffn_w_up")
SMALL_SHARDED = ("ssd_conv_w", "rg_conv_w", "rg_conv_b", "rg_ba", "rg_bx", "rg_lambda", "ffn_conv_w", "ln_g", "ln_b")
WEIGHTS = ("ab_w_in", "ssd_conv_w", "ssd_conv_b", "ssd_dt_bias", "ssd_a_log", "ssd_d", "ssd_norm_w", "hg_lower", "hg_norm_w",
           "ab_w_out", "cd_w_in", "swa_sinks", "rg_conv_w", "rg_conv_b", "rg_wa", "rg_ba", "rg_wx", "rg_bx", "rg_lambda",
           "cd_w_out", "ffn_w_up", "ffn_conv_w", "ffn_conv_b", "ffn_w_down", "ln_g", "ln_b")
SMALL = tuple(n for n in WEIGHTS if n not in BIG)


def _full_from_shards(name, g):
    if name in BIG_COLS or name in SMALL_SHARDED:
        return jnp.concatenate([g[k] for k in range(4)], axis=-1)
    return jnp.concatenate([g[k] for k in range(4)], axis=1)


def _layer_shards(name, f):
    if f.ndim == 3:
        return f
    k, n = f.shape
    if name in BIG_COLS:
        return jnp.transpose(f.reshape(k, 4, n // 4), (1, 0, 2))
    return f.reshape(4, k // 4, n)


def kernel(x, ab_w_in, ssd_conv_w, ssd_conv_b, ssd_dt_bias, ssd_a_log, ssd_d, ssd_norm_w, hg_lower, hg_norm_w, ab_w_out, cd_w_in, swa_sinks, rg_conv_w, rg_conv_b, rg_wa, rg_ba, rg_wx, rg_bx, rg_lambda, cd_w_out, ffn_w_up, ffn_conv_w, ffn_conv_b, ffn_w_down, ln_g, ln_b, loss_target, m_ab_w_in, m_ssd_conv_w, m_ssd_conv_b, m_ssd_dt_bias, m_ssd_a_log, m_ssd_d, m_ssd_norm_w, m_hg_lower, m_hg_norm_w, m_ab_w_out, m_cd_w_in, m_swa_sinks, m_rg_conv_w, m_rg_conv_b, m_rg_wa, m_rg_ba, m_rg_wx, m_rg_bx, m_rg_lambda, m_cd_w_out, m_ffn_w_up, m_ffn_conv_w, m_ffn_conv_b, m_ffn_w_down, m_ln_g, m_ln_b, v_ab_w_in, v_ssd_conv_w, v_ssd_conv_b, v_ssd_dt_bias, v_ssd_a_log, v_ssd_d, v_ssd_norm_w, v_hg_lower, v_hg_norm_w, v_ab_w_out, v_cd_w_in, v_swa_sinks, v_rg_conv_w, v_rg_conv_b, v_rg_wa, v_rg_ba, v_rg_wx, v_rg_bx, v_rg_lambda, v_cd_w_out, v_ffn_w_up, v_ffn_conv_w, v_ffn_conv_b, v_ffn_w_down, v_ln_g, v_ln_b):
    args = locals()
    w = {n: args[n] for n in WEIGHTS}
    mom = {n: args["m_" + n] for n in WEIGHTS}
    var = {n: args["v_" + n] for n in WEIGHTS}
    cx, cy, cc = lax.axis_index("x"), lax.axis_index("y"), lax.axis_index("c")
    chip = 2 * cx + cy

    halves = [w[n].astype(BF16).reshape((2, w[n].shape[0] // 2) + w[n].shape[1:]) for n in BIG]
    gathered = _gather_chips(halves, "gather_weights")
    full = {n: g.reshape((4,) + w[n].shape) for n, g in zip(BIG, gathered)}
    sp = _pack([w[n] for n in SMALL_SHARDED], 0, F32, SUBLANES)
    sg = _gather_devices(sp, "gather_small").reshape(4, 2, sp.shape[0], LANES)[:, 0]
    for n, g in zip(SMALL_SHARDED, _unpack(sg, 1, [w[n].shape for n in SMALL_SHARDED])):
        full[n] = _full_from_shards(n, g)
    for n in SMALL:
        full.setdefault(n, w[n])

    loss, grad_x, grads = _local_step(x[0], loss_target[0], full)
    loss = lax.psum(loss, ("x", "y", "c"))

    g4 = []
    for n in BIG:
        st = jnp.stack([_layer_shards(n, f) for f in grads[n]], axis=1)
        g4.append(st.reshape(4, 2, st.shape[1] // 2 * st.shape[2], st.shape[3]))
    hbs = [_add_half(g, xr, cc, "add_halves_" + n) for n, g, xr in zip(BIG, g4, _swap_halves(g4, "swap_halves"))]
    own = [_sum_blocks(y, "sum_chips_" + n) for n, y in zip(BIG, _scatter_chips(hbs, "scatter_chips"))]
    gout = {n: o.reshape(w[n].shape) for n, o in zip(BIG, _share_half(own, "share_half"))}

    small_shapes = [grads[n].shape for n in SMALL]
    gs = _pack([grads[n] for n in SMALL], 0, F32, SUBLANES)
    gsum = _sum_blocks(_gather_devices(gs, "gather_small_grads").reshape(8, gs.shape[0], LANES), "sum_small")
    for n, g in zip(SMALL, _unpack(gsum, 0, small_shapes)):
        if n in SMALL_SHARDED:
            width = w[n].shape[-1]
            g = lax.dynamic_slice_in_dim(g, chip * width, width, axis=g.ndim - 1)
        gout[n] = g

    delta, new_m, new_v = {}, {}, {}
    for n in BIG:
        sh = w[n].shape
        two = lambda a: a.reshape(-1, sh[-1])
        d, nm, nv = _adamw(two(w[n]), two(gout[n]), two(mom[n]), two(var[n]), "adamw_" + n)
        delta[n], new_m[n], new_v[n] = d.reshape(sh), nm.reshape(sh), nv.reshape(sh)
    local_shapes = [w[n].shape for n in SMALL]
    packs = [_pack([d[n] for n in SMALL], 0, F32, SUBLANES) for d in (w, gout, mom, var)]
    for dst, buf in zip((delta, new_m, new_v), _adamw(*packs, "adamw_small")):
        dst.update(zip(SMALL, _unpack(buf, 0, local_shapes)))

    return (loss, grad_x[None], *[gout[n] for n in WEIGHTS], *[delta[n] for n in WEIGHTS],
            *[new_m[n] for n in WEIGHTS], *[new_v[n] for n in WEIGHTS])
```

```python
import functools

import numpy as np
import jax
import jax.numpy as jnp
from jax import lax
from jax.experimental import pallas as pl
from jax.experimental.pallas import tpu as pltpu

F32 = jnp.float32
BF16 = jnp.bfloat16

D_MODEL = 1024
DEPTH = 4
SSD_HEADS = 8
SSD_D_INNER = 512
SSD_CONV_DIM = 768
SSD_CHUNK = 128
HG_CHUNK = 64
HG_HEADS = 4
SWA_BLOCK = 128
RG_WIDTH = 512
FFN_DIM = 2816
LN_EPS = 1e-5
RMS_EPS = 1e-6
MASK_VALUE = -1e9
ALPHA = (2 * DEPTH) ** 0.25
RG_C = 8.0
ADAM_LR, ADAM_B1, ADAM_B2, ADAM_EPS, ADAM_WD, ADAM_STEP = 0.001, 0.9, 0.999, 1e-08, 0.01, 10

LANES = 128
SUBLANES = 8
HALO = 16
CONV_CB = 256
VMEM_LIMIT = 56 * 1024 * 1024

AB_Z, AB_HQ, AB_HF, AB_HI, AB_HG, AB_XBC, AB_DT, AB_COLS = 0, 512, 1024, 1536, 2048, 2560, 3328, 3456
CD_Q, CD_GATE, CD_XR, CD_K, CD_V, CD_COLS = 0, 512, 1024, 1536, 1664, 1792


def _cp(sem=None):
    kw = dict(vmem_limit_bytes=VMEM_LIMIT)
    if sem is not None:
        kw["dimension_semantics"] = sem
    return pltpu.CompilerParams(**kw)


def _dot(a, b, dims=(((1,), (0,)), ((), ())), precision=None):
    return lax.dot_general(a, b, dims, precision=precision, preferred_element_type=F32)


NN = (((1,), (0,)), ((), ()))
NT = (((1,), (1,)), ((), ()))
TN = (((0,), (0,)), ((), ()))


@functools.partial(jax.custom_vjp, nondiff_argnums=(2,))
def _bdot(a, b, dims=NN):
    return _dot(a.astype(BF16), b.astype(BF16), dims)


def _bdot_fwd(a, b, dims):
    return _bdot(a, b, dims), (a, b)


def _bdot_bwd(dims, res, ct):
    a, b = res
    ab, bb, cb = a.astype(BF16), b.astype(BF16), ct.astype(BF16)
    if dims == NN:
        da, db = _dot(cb, bb, NT), _dot(ab, cb, TN)
    elif dims == NT:
        da, db = _dot(cb, bb, NN), _dot(cb, ab, TN)
    else:
        da, db = _dot(bb, cb, NT), _dot(ab, cb, NN)
    return da.astype(a.dtype), db.astype(b.dtype)


_bdot.defvjp(_bdot_fwd, _bdot_bwd)


WIDE_TILES = (1408, 1152, 1024, 896, 768, 512, 384, 256, 128)
MM_BLOCK_BYTES = 6 * 1024 * 1024


def _pick(n, cands):
    for c in cands:
        if n % c == 0:
            return c
    return n


def _mm(pairs, n_out, out_dtype, name, trans_b=False, tm=512):
    m = pairs[0][0].shape[0]
    a_row_bytes = sum(p[4] * p[0].dtype.itemsize for p in pairs)
    b_col_bytes = sum(p[4] * p[2].dtype.itemsize for p in pairs)
    tm = min(2 * tm if 2 * tm * a_row_bytes <= 2 * MM_BLOCK_BYTES else tm, m)
    tn = _pick(n_out, [c for c in WIDE_TILES if c * b_col_bytes <= MM_BLOCK_BYTES])
    n = len(pairs)

    def kern(*refs):
        acc = None
        for i in range(n):
            p = _bdot(refs[i][...], refs[n + i][...], NT if trans_b else NN)
            acc = p if acc is None else acc + p
        refs[2 * n][...] = acc.astype(out_dtype)

    in_specs = []
    for a, acb, b, bcb, k in pairs:
        in_specs.append(pl.BlockSpec((tm, k), functools.partial(lambda i, j, c: (i, c), c=acb)))
    for a, acb, b, bcb, k in pairs:
        if callable(bcb):
            in_specs.append(bcb(tn))
        elif trans_b:
            in_specs.append(pl.BlockSpec((tn, k), functools.partial(lambda i, j, c: (j, c), c=bcb)))
        else:
            in_specs.append(pl.BlockSpec((k, tn), functools.partial(lambda i, j, c: (c, j), c=bcb)))
    return pl.pallas_call(
        kern, name=name, grid=(m // tm, n_out // tn), in_specs=in_specs,
        out_specs=pl.BlockSpec((tm, tn), lambda i, j: (i, j)),
        out_shape=jax.ShapeDtypeStruct((m, n_out), out_dtype),
        compiler_params=_cp(("parallel", "arbitrary")),
    )(*[p[0] for p in pairs], *[p[2] for p in pairs])


def _shard_as_col_tile(g, layer):
    def spec(tn):
        assert tn == g.shape[3]
        return pl.BlockSpec((None, None, g.shape[2], tn), lambda i, j: (j, layer, 0, 0))
    return spec


def _shard_transposed(g, chip, layer):
    return lambda tn: pl.BlockSpec((None, None, tn, g.shape[3]), lambda i, j: (chip, layer, j, 0))


def _mm_tn(a, a_cb, ka, g, g_cb, ng, name, tm=2048):
    m = a.shape[0]
    tm = min(tm, m)
    tk = _pick(ka, (1024, 1408, 512, 256, 128))
    tn = _pick(ng, WIDE_TILES)
    nm = m // tm

    def kern(a_ref, g_ref, o_ref):
        p = _bdot(a_ref[...], g_ref[...], TN)

        @pl.when(pl.program_id(2) == 0)
        def _():
            o_ref[...] = p

        @pl.when(pl.program_id(2) > 0)
        def _():
            o_ref[...] += p

    ak, gk = ka // tk, ng // tn
    return pl.pallas_call(
        kern, name=name, grid=(ak, gk, nm),
        in_specs=[pl.BlockSpec((tm, tk), lambda i, j, r: (r, a_cb * ak + i)),
                  pl.BlockSpec((tm, tn), lambda i, j, r: (r, g_cb * gk + j))],
        out_specs=pl.BlockSpec((tk, tn), lambda i, j, r: (i, j)),
        out_shape=jax.ShapeDtypeStruct((ka, ng), F32),
        compiler_params=_cp(("parallel", "parallel", "arbitrary")),
    )(a, g)


def _mm_ln(pairs, x, g, b, name, tm=512):
    m, d = x.shape
    tm = min(tm, m)
    n = len(pairs)

    def kern(*refs):
        x_ref, g_ref, b_ref, y_ref, yb_ref, r_ref = refs[2 * n:]
        r = ALPHA * x_ref[...]
        for i in range(n):
            r = r + _bdot(refs[i][...], refs[n + i][...])
        mu = jnp.mean(r, -1, keepdims=True)
        xc = r - mu
        var = jnp.mean(xc * xc, -1, keepdims=True)
        y = xc * lax.rsqrt(var + LN_EPS) * g_ref[...] + b_ref[...]
        y_ref[...] = y
        yb_ref[...] = y.astype(BF16)
        r_ref[...] = r

    row = pl.BlockSpec((tm, d), lambda i: (i, 0))
    vec = pl.BlockSpec((1, d), lambda i: (0, 0))
    in_specs = [pl.BlockSpec((tm, k), functools.partial(lambda i, c: (i, c), c=acb)) for _, acb, _, _, k in pairs]
    in_specs += [pl.BlockSpec((k, d), functools.partial(lambda i, c: (c, 0), c=bcb)) for _, _, _, bcb, k in pairs]
    return pl.pallas_call(
        kern, name=name, grid=(m // tm,), in_specs=in_specs + [row, vec, vec], out_specs=[row, row, row],
        out_shape=[jax.ShapeDtypeStruct((m, d), F32), jax.ShapeDtypeStruct((m, d), BF16), jax.ShapeDtypeStruct((m, d), F32)],
        compiler_params=_cp(("parallel",)),
    )(*[p[0] for p in pairs], *[p[2] for p in pairs], x, g.reshape(1, d), b.reshape(1, d))


def _ln_bwd(dya, dyb, ca, cb, r, g, wt, out_dtype, name, tm=512):
    s, d = r.shape
    n = wt.shape[0]
    tm = min(tm, s)

    def kern(a_ref, b_ref, r_ref, g_ref, w_ref, dr_ref, drb_ref, dg_ref, db_ref, p_ref):
        dy = ca * a_ref[...] + cb * b_ref[...]
        rr = r_ref[...]
        mu = jnp.mean(rr, -1, keepdims=True)
        xc = rr - mu
        var = jnp.mean(xc * xc, -1, keepdims=True)
        rstd = lax.rsqrt(var + LN_EPS)
        xhat = xc * rstd
        dxh = dy * g_ref[...]
        dr = rstd * (dxh - jnp.mean(dxh, -1, keepdims=True) - xhat * jnp.mean(dxh * xhat, -1, keepdims=True))
        dr_ref[...] = dr
        drb = dr.astype(BF16)
        drb_ref[...] = drb
        p_ref[...] = _dot(drb, w_ref[...], NT).astype(out_dtype)
        dg = jnp.sum(dy * xhat, 0, keepdims=True)
        db = jnp.sum(dy, 0, keepdims=True)

        @pl.when(pl.program_id(0) == 0)
        def _():
            dg_ref[...] = dg
            db_ref[...] = db

        @pl.when(pl.program_id(0) > 0)
        def _():
            dg_ref[...] += dg
            db_ref[...] += db

    row = pl.BlockSpec((tm, d), lambda i: (i, 0))
    vec = pl.BlockSpec((1, d), lambda i: (0, 0))
    dr, drb, dg, db, prod = pl.pallas_call(
        kern, name=name, grid=(s // tm,),
        in_specs=[row, row, row, vec, pl.BlockSpec((n, d), lambda i: (0, 0))],
        out_specs=[row, row, vec, vec, pl.BlockSpec((tm, n), lambda i: (i, 0))],
        out_shape=[jax.ShapeDtypeStruct((s, d), F32), jax.ShapeDtypeStruct((s, d), BF16), jax.ShapeDtypeStruct((1, d), F32),
                   jax.ShapeDtypeStruct((1, d), F32), jax.ShapeDtypeStruct((s, n), out_dtype)],
        compiler_params=_cp(("arbitrary",)),
    )(dya, dyb, r, g.reshape(1, d), wt)
    return dr, drb, dg[0], db[0], prod


def _sqerr(y, t, name, tm=256):
    s, d = y.shape
    tm = min(tm, s)

    def kern(y_ref, t_ref, o_ref):
        e = y_ref[...] - t_ref[...]
        p = jnp.sum(e * e, 0, keepdims=True)

        @pl.when(pl.program_id(0) == 0)
        def _():
            o_ref[...] = p

        @pl.when(pl.program_id(0) > 0)
        def _():
            o_ref[...] += p

    row = pl.BlockSpec((tm, d), lambda i: (i, 0))
    return pl.pallas_call(
        kern, name=name, grid=(s // tm,), in_specs=[row, row], out_specs=pl.BlockSpec((1, d), lambda i: (0, 0)),
        out_shape=jax.ShapeDtypeStruct((1, d), F32), compiler_params=_cp(("arbitrary",)),
    )(y, t)


def _axpby(a, b, ca, cb, name, tm=256):
    s, d = a.shape
    tm = min(tm, s)

    def kern(a_ref, b_ref, o_ref):
        o_ref[...] = ca * a_ref[...] + cb * b_ref[...]

    row = pl.BlockSpec((tm, d), lambda i: (i, 0))
    return pl.pallas_call(
        kern, name=name, grid=(s // tm,), in_specs=[row, row], out_specs=row,
        out_shape=jax.ShapeDtypeStruct((s, d), F32), compiler_params=_cp(("parallel",)),
    )(a, b)


def _shift_down(xx, s, n):
    if s == 0:
        return xx[HALO:HALO + n]
    return pltpu.roll(xx, s, axis=0)[HALO:HALO + n]


def _shift_up(yy, s, n):
    if s == 0:
        return yy[0:n]
    return pltpu.roll(yy, yy.shape[0] - s, axis=0)[0:n]


def _prev_halo(tm):
    return lambda i: jnp.maximum(i * (tm // HALO) - 1, 0)


def _dwconv_fwd(x, x_cb0, c, w8, b, k, name, tm=1024):
    s = x.shape[0]
    tm = min(tm, s)
    ph = _prev_halo(tm)

    def kern(x_ref, h_ref, w_ref, b_ref, y_ref):
        halo = jnp.where(pl.program_id(0) == 0, 0.0, h_ref[...].astype(F32))
        xx = jnp.concatenate([halo, x_ref[...].astype(F32)], axis=0)
        w = w_ref[...]
        acc = b_ref[...] + w[k - 1:k] * xx[HALO:]
        for j in range(k - 1):
            acc = acc + w[j:j + 1] * _shift_down(xx, k - 1 - j, tm)
        y_ref[...] = acc

    return pl.pallas_call(
        kern, name=name, grid=(s // tm, c // CONV_CB),
        in_specs=[pl.BlockSpec((tm, CONV_CB), lambda i, j: (i, x_cb0 + j)),
                  pl.BlockSpec((HALO, CONV_CB), lambda i, j: (ph(i), x_cb0 + j)),
                  pl.BlockSpec((SUBLANES, CONV_CB), lambda i, j: (0, j)),
                  pl.BlockSpec((1, CONV_CB), lambda i, j: (0, j))],
        out_specs=pl.BlockSpec((tm, CONV_CB), lambda i, j: (i, j)),
        out_shape=jax.ShapeDtypeStruct((s, c), F32), compiler_params=_cp(("parallel", "parallel")),
    )(x, x, w8, b.reshape(1, c))


def _dwconv_bwd(dy, x, x_cb0, c, w8, k, name, tm=1024):
    s = x.shape[0]
    tm = min(tm, s)
    nt = s // tm
    ph = _prev_halo(tm)
    nh = lambda i: jnp.minimum((i + 1) * (tm // HALO), s // HALO - 1)

    def kern(dy_ref, dyn_ref, x_ref, h_ref, w_ref, dx_ref, dw_ref, db_ref):
        i = pl.program_id(1)
        halo = jnp.where(i == 0, 0.0, h_ref[...].astype(F32))
        xx = jnp.concatenate([halo, x_ref[...].astype(F32)], axis=0)
        dyt = dy_ref[...]
        nxt = jnp.where(i == nt - 1, 0.0, dyn_ref[...])
        dyy = jnp.concatenate([dyt, nxt], axis=0)
        w = w_ref[...]
        dx = w[k - 1:k] * dyt
        rows = [jnp.sum(dyt * _shift_down(xx, k - 1 - j, tm), 0, keepdims=True) for j in range(k)]
        for j in range(k - 1):
            dx = dx + w[j:j + 1] * _shift_up(dyy, k - 1 - j, tm)
        dx_ref[...] = dx.astype(BF16)
        dw = jnp.concatenate(rows + [jnp.zeros((SUBLANES - k, CONV_CB), F32)], axis=0)
        db = jnp.sum(dyt, 0, keepdims=True)

        @pl.when(i == 0)
        def _():
            dw_ref[...] = dw
            db_ref[...] = db

        @pl.when(i > 0)
        def _():
            dw_ref[...] += dw
            db_ref[...] += db

    dx, dw, db = pl.pallas_call(
        kern, name=name, grid=(c // CONV_CB, nt),
        in_specs=[pl.BlockSpec((tm, CONV_CB), lambda j, i: (i, j)),
                  pl.BlockSpec((HALO, CONV_CB), lambda j, i: (nh(i), j)),
                  pl.BlockSpec((tm, CONV_CB), lambda j, i: (i, x_cb0 + j)),
                  pl.BlockSpec((HALO, CONV_CB), lambda j, i: (ph(i), x_cb0 + j)),
                  pl.BlockSpec((SUBLANES, CONV_CB), lambda j, i: (0, j))],
        out_specs=[pl.BlockSpec((tm, CONV_CB), lambda j, i: (i, j)),
                   pl.BlockSpec((SUBLANES, CONV_CB), lambda j, i: (0, j)),
                   pl.BlockSpec((1, CONV_CB), lambda j, i: (0, j))],
        out_shape=[jax.ShapeDtypeStruct((s, c), BF16), jax.ShapeDtypeStruct((SUBLANES, c), F32),
                   jax.ShapeDtypeStruct((1, c), F32)],
        compiler_params=_cp(("parallel", "arbitrary")),
    )(dy, dy, x, x, w8)
    return dx, dw, db[0]


def _silu(x):
    return x * jax.nn.sigmoid(x)


def _ffn_gate_fwd(hu, w8, b, name, tm=2048):
    s = hu.shape[0]
    tm = min(tm, s)
    nb = FFN_DIM // CONV_CB
    ph = _prev_halo(tm)
    k = 3

    def conv(x_ref, h_ref, w_ref, b_ref):
        halo = jnp.where(pl.program_id(0) == 0, 0.0, h_ref[...].astype(F32))
        xx = jnp.concatenate([halo, x_ref[...].astype(F32)], axis=0)
        w = w_ref[...]
        acc = b_ref[...] + w[k - 1:k] * xx[HALO:]
        for j in range(k - 1):
            acc = acc + w[j:j + 1] * _shift_down(xx, k - 1 - j, tm)
        return acc

    def kern(g_ref, gh_ref, u_ref, uh_ref, wg_ref, wu_ref, bg_ref, bu_ref, a_ref, hg_ref, hu_ref):
        g = conv(g_ref, gh_ref, wg_ref, bg_ref)
        u = conv(u_ref, uh_ref, wu_ref, bu_ref)
        a_ref[...] = (_silu(g) * u).astype(BF16)
        hg_ref[...] = g.astype(BF16)
        hu_ref[...] = u.astype(BF16)

    main = lambda off: pl.BlockSpec((tm, CONV_CB), lambda i, j: (i, off + j))
    halo = lambda off: pl.BlockSpec((HALO, CONV_CB), lambda i, j: (ph(i), off + j))
    wsp = lambda off: pl.BlockSpec((SUBLANES, CONV_CB), lambda i, j: (0, off + j))
    bsp = lambda off: pl.BlockSpec((1, CONV_CB), lambda i, j: (0, off + j))
    b2 = b.reshape(1, 2 * FFN_DIM)
    half = jax.ShapeDtypeStruct((s, FFN_DIM), BF16)
    return pl.pallas_call(
        kern, name=name, grid=(s // tm, nb),
        in_specs=[main(0), halo(0), main(nb), halo(nb), wsp(0), wsp(nb), bsp(0), bsp(nb)],
        out_specs=[main(0), main(0), main(0)], out_shape=[half, half, half],
        compiler_params=_cp(("parallel", "parallel")),
    )(hu, hu, hu, hu, w8, w8, b2, b2)


def _ffn_gate_bwd(hu, hg, hv, da, w8, name, tm=1024):
    s = hu.shape[0]
    tm = min(tm, s)
    nt = s // tm
    nb = FFN_DIM // CONV_CB
    nh = lambda i: jnp.minimum((i + 1) * (tm // HALO), s // HALO - 1)
    k = 3

    def kern(xg_ref, xu_ref, g_ref, gn_ref, u_ref, un_ref, da_ref, dan_ref, wg_ref, wu_ref,
             dg_ref, du_ref, dwg_ref, dwu_ref, dbg_ref, dbu_ref):
        i = pl.program_id(1)

        def ext(x_ref, n_ref):
            return jnp.concatenate([x_ref[...].astype(F32), n_ref[...].astype(F32)], axis=0)

        g, u = ext(g_ref, gn_ref), ext(u_ref, un_ref)
        dae = jnp.concatenate([da_ref[...].astype(F32), jnp.where(i == nt - 1, 0.0, dan_ref[...].astype(F32))], axis=0)
        sg = jax.nn.sigmoid(g)
        dhg = dae * u * (sg * (1.0 + g * (1.0 - sg)))
        dhu = dae * (g * sg)

        def back(dh, x, w):
            taps = [dh[0:tm] if j == k - 1 else _shift_up(dh, k - 1 - j, tm) for j in range(k)]
            dx = w[0:1] * taps[0]
            for j in range(1, k):
                dx = dx + w[j:j + 1] * taps[j]
            rows = [jnp.sum(taps[j] * x, 0, keepdims=True) for j in range(k)]
            dw = jnp.concatenate(rows + [jnp.zeros((SUBLANES - k, CONV_CB), F32)], axis=0)
            return dx, dw, jnp.sum(dh[0:tm], 0, keepdims=True)

        dxg, dwg, dbg = back(dhg, xg_ref[...].astype(F32), wg_ref[...])
        dxu, dwu, dbu = back(dhu, xu_ref[...].astype(F32), wu_ref[...])
        dg_ref[...] = dxg.astype(BF16)
        du_ref[...] = dxu.astype(BF16)

        @pl.when(i == 0)
        def _():
            dwg_ref[...] = dwg
            dwu_ref[...] = dwu
            dbg_ref[...] = dbg
            dbu_ref[...] = dbu

        @pl.when(i > 0)
        def _():
            dwg_ref[...] += dwg
            dwu_ref[...] += dwu
            dbg_ref[...] += dbg
            dbu_ref[...] += dbu

    main = lambda off: pl.BlockSpec((tm, CONV_CB), lambda j, i: (i, off + j))
    nxt = lambda off: pl.BlockSpec((HALO, CONV_CB), lambda j, i: (nh(i), off + j))
    wsp = lambda off: pl.BlockSpec((SUBLANES, CONV_CB), lambda j, i: (0, off + j))
    bsp = lambda off: pl.BlockSpec((1, CONV_CB), lambda j, i: (0, off + j))
    outs = pl.pallas_call(
        kern, name=name, grid=(nb, nt),
        in_specs=[main(0), main(nb), main(0), nxt(0), main(0), nxt(0), main(0), nxt(0), wsp(0), wsp(nb)],
        out_specs=[main(0), main(0), wsp(0), wsp(0), bsp(0), bsp(0)],
        out_shape=[jax.ShapeDtypeStruct((s, FFN_DIM), BF16), jax.ShapeDtypeStruct((s, FFN_DIM), BF16),
                   jax.ShapeDtypeStruct((SUBLANES, FFN_DIM), F32), jax.ShapeDtypeStruct((SUBLANES, FFN_DIM), F32),
                   jax.ShapeDtypeStruct((1, FFN_DIM), F32), jax.ShapeDtypeStruct((1, FFN_DIM), F32)],
        compiler_params=_cp(("parallel", "arbitrary")),
    )(hu, hu, hg, hg, hv, hv, da, da, w8, w8)
    dg, du, dwg, dwu, dbg, dbu = outs
    return dg, du, jnp.concatenate([dwg, dwu], axis=1), jnp.concatenate([dbg[0], dbu[0]])


def _iota(shape, dim):
    return lax.broadcasted_iota(jnp.int32, shape, dim)


def _in_range(idx, start, size):
    return jnp.logical_and(idx >= start, idx < start + size)


@functools.partial(jax.custom_vjp, nondiff_argnums=(2, 3))
def _zdot(a, b, dims, z):
    zz, x = ((a, b) if z == 0 else (b, a))
    zz = zz.astype(BF16)
    xh = x.astype(BF16)
    r1 = x - xh.astype(F32)
    xm = r1.astype(BF16)
    xl = (r1 - xm.astype(F32)).astype(BF16)
    f = (lambda p: _dot(zz, p, dims)) if z == 0 else (lambda p: _dot(p, zz, dims))
    return f(xh) + f(xm) + f(xl)


def _zdot_fwd(a, b, dims, z):
    return _zdot(a, b, dims, z), (a if z == 0 else b)


def _zdot_bwd(dims, z, zz, ct):
    zb = zz.astype(BF16)
    ch = ct.astype(BF16)
    cm = (ct - ch.astype(F32)).astype(BF16)
    if dims == NN and z == 0:
        f = lambda c: _dot(zb, c, TN)
    elif dims == NN and z == 1:
        f = lambda c: _dot(c, zb, NT)
    else:
        f = lambda c: _dot(c, zb, TN)
    dx = f(ch) + f(cm)
    return (jnp.zeros_like(zz), dx) if z == 0 else (dx, jnp.zeros_like(zz))


_zdot.defvjp(_zdot_fwd, _zdot_bwd)


def _expand8(v, e):
    return jnp.sum(_zdot(jnp.broadcast_to(v, (SUBLANES, v.shape[1])), e, NN, 1), 0, keepdims=True) * (1.0 / SUBLANES)


def _ssd_consts():
    tri = np.tril(np.ones((SSD_CHUNK, SSD_CHUNK), np.float32))
    e = np.zeros((LANES, SSD_D_INNER), np.float32)
    for h in range(SSD_HEADS):
        e[h, 64 * h:64 * h + 64] = 1.0
    bd = np.zeros((LANES, SSD_D_INNER), np.float32)
    bd[:64, :256] = 1.0
    bd[64:, 256:] = 1.0
    return jnp.asarray(tri), jnp.asarray(e), jnp.asarray(bd)


def _ssd_chunk(tri, e, bd, z, xs_p, bm_p, cm_p, dtr, st, dtb, alog, dsk, nw):
    n = z.shape[0]
    lane128, lane512 = _iota((1, LANES), 1), _iota((1, SSD_D_INNER), 1)
    sub8 = _iota((SUBLANES, 1), 0)
    tril = _iota((n, 1), 0) >= _iota((1, n), 1)
    head_rows = jnp.where(sub8 == lane128, 1.0, 0.0)
    xs, bm, cm = _silu(xs_p), _silu(bm_p), _silu(cm_p)
    dt = jax.nn.softplus(dtr + dtb)
    da = dt * (-jnp.exp(alog))
    acs = _zdot(tri, da, NN, 0)
    acst = _zdot(head_rows, acs, NT, 0)
    xc = xs * _zdot(dt, e, NN, 1)
    tot = jnp.sum(da, 0, keepdims=True)
    y = _bdot(cm, st) * _zdot(jnp.exp(acs), e, NN, 1)
    st_new = st * _expand8(jnp.exp(tot), e) + bd * _bdot(bm, xc * _zdot(jnp.exp(tot - acs), e, NN, 1), TN)
    for g in range(2):
        cb = _bdot(jnp.where(_in_range(lane128, 64 * g, 64), cm, 0.0), bm, NT)
        for h in range(4 * g, 4 * g + 4):
            col = jnp.sum(jnp.where(lane128 == h, acs, 0.0), 1, keepdims=True)
            row = jnp.sum(jnp.where(sub8 == h, acst, 0.0), 0, keepdims=True)
            dm = jnp.where(tril, jnp.exp(jnp.minimum(col - row, 0.0)), 0.0)
            y = y + _bdot(cb * dm, jnp.where(_in_range(lane512, 64 * h, 64), xc, 0.0))
    y = (y + _expand8(dsk, e) * xs) * _silu(z)
    ysq = y * y
    g0 = lane512 < 256
    ms0 = jnp.sum(jnp.where(g0, ysq, 0.0), 1, keepdims=True) * (1.0 / 256)
    ms1 = jnp.sum(jnp.where(g0, 0.0, ysq), 1, keepdims=True) * (1.0 / 256)
    rs = jnp.where(g0, lax.rsqrt(ms0 + RMS_EPS), lax.rsqrt(ms1 + RMS_EPS))
    return y * rs * nw, st_new


SSD_STEP = 2


def _ssd_rows(u):
    return slice(SSD_CHUNK * u, SSD_CHUNK * (u + 1))


def _ssd_specs(rev, ng):
    ci = (lambda i: ng - 1 - i) if rev else (lambda i: i)
    n = SSD_STEP * SSD_CHUNK
    act = [pl.BlockSpec((n, 512), lambda i: (ci(i), AB_Z // 512)),
           pl.BlockSpec((n, 512), lambda i: (ci(i), 0)),
           pl.BlockSpec((n, LANES), lambda i: (ci(i), 4)),
           pl.BlockSpec((n, LANES), lambda i: (ci(i), 5)),
           pl.BlockSpec((n, LANES), lambda i: (ci(i), AB_DT // LANES))]
    const = [pl.BlockSpec((SSD_CHUNK, SSD_CHUNK), lambda i: (0, 0)), pl.BlockSpec((LANES, 512), lambda i: (0, 0)),
             pl.BlockSpec((LANES, 512), lambda i: (0, 0))]
    par = [pl.BlockSpec((1, LANES), lambda i: (0, 0))] * 3 + [pl.BlockSpec((1, 512), lambda i: (0, 0))]
    return ci, act, const, par


def _ssd_fwd(h, xbcc, dtb, alog, dsk, nw, name):
    s = h.shape[0]
    nc = s // SSD_CHUNK
    ng = nc // SSD_STEP
    ci, act, const, par = _ssd_specs(False, ng)

    def kern(z_ref, xs_ref, bm_ref, cm_ref, dt_ref, tri_ref, e_ref, bd_ref, dtb_ref, al_ref, dsk_ref, nw_ref,
             y_ref, sts_ref, st):
        @pl.when(pl.program_id(0) == 0)
        def _():
            st[...] = jnp.zeros_like(st)

        for u in range(SSD_STEP):
            rows = _ssd_rows(u)
            sts_ref[u] = st[...]
            y, stn = _ssd_chunk(tri_ref[...], e_ref[...], bd_ref[...], z_ref[rows, :], xs_ref[rows, :], bm_ref[rows, :],
                                cm_ref[rows, :], dt_ref[rows, :], st[...], dtb_ref[...], al_ref[...], dsk_ref[...], nw_ref[...])
            y_ref[rows, :] = y.astype(BF16)
            st[...] = stn

    return pl.pallas_call(
        kern, name=name, grid=(ng,), in_specs=act + const + par,
        out_specs=[pl.BlockSpec((SSD_STEP * SSD_CHUNK, 512), lambda i: (i, 0)),
                   pl.BlockSpec((SSD_STEP, LANES, 512), lambda i: (i, 0, 0))],
        out_shape=[jax.ShapeDtypeStruct((s, 512), BF16), jax.ShapeDtypeStruct((nc, LANES, 512), F32)],
        scratch_shapes=[pltpu.VMEM((LANES, 512), F32)], compiler_params=_cp(("arbitrary",)),
    )(h, xbcc, xbcc, xbcc, h, *_ssd_consts(), dtb, alog, dsk, nw)


def _acc(ref, val, first):
    @pl.when(first)
    def _():
        ref[...] = val

    @pl.when(jnp.logical_not(first))
    def _():
        ref[...] += val


def _ssd_bwd(dy, dy_cb, h, xbcc, sts, dtb, alog, dsk, nw, name):
    s = h.shape[0]
    ng = s // SSD_CHUNK // SSD_STEP
    n = SSD_STEP * SSD_CHUNK
    ci, act, const, par = _ssd_specs(True, ng)

    def kern(dy_ref, sts_ref, z_ref, xs_ref, bm_ref, cm_ref, dt_ref, tri_ref, e_ref, bd_ref, dtb_ref, al_ref, dsk_ref,
             nw_ref, dz_ref, dx_ref, ddt_ref, ddtb_ref, dal_ref, ddsk_ref, dnw_ref, dst):
        @pl.when(pl.program_id(0) == 0)
        def _():
            dst[...] = jnp.zeros_like(dst)
            for ref in (ddtb_ref, dal_ref, ddsk_ref, dnw_ref):
                ref[...] = jnp.zeros_like(ref)

        f = functools.partial(_ssd_chunk, tri_ref[...], e_ref[...], bd_ref[...])
        for u in range(SSD_STEP - 1, -1, -1):
            rows = _ssd_rows(u)
            _, pull = jax.vjp(f, z_ref[rows, :], xs_ref[rows, :], bm_ref[rows, :], cm_ref[rows, :], dt_ref[rows, :],
                              sts_ref[u], dtb_ref[...], al_ref[...], dsk_ref[...], nw_ref[...])
            dz, dxs, dbm, dcm, ddt, dsti, ddtb, dal, ddsk, dnw = pull((dy_ref[rows, :], dst[...]))
            dz_ref[rows, :] = dz.astype(BF16)
            dx_ref[rows, 0:512] = dxs
            dx_ref[rows, 512:640] = dbm
            dx_ref[rows, 640:768] = dcm
            ddt_ref[rows, :] = ddt.astype(BF16)
            dst[...] = dsti
            ddtb_ref[...] += ddtb
            dal_ref[...] += dal
            ddsk_ref[...] += ddsk
            dnw_ref[...] += dnw

    vec = pl.BlockSpec((1, LANES), lambda i: (0, 0))
    return pl.pallas_call(
        kern, name=name, grid=(ng,),
        in_specs=[pl.BlockSpec((n, 512), lambda i: (ci(i), dy_cb)), pl.BlockSpec((SSD_STEP, LANES, 512), lambda i: (ci(i), 0, 0))]
        + act + const + par,
        out_specs=[pl.BlockSpec((n, 512), lambda i: (ci(i), 0)), pl.BlockSpec((n, SSD_CONV_DIM), lambda i: (ci(i), 0)),
                   pl.BlockSpec((n, LANES), lambda i: (ci(i), 0)), vec, vec, vec, pl.BlockSpec((1, 512), lambda i: (0, 0))],
        out_shape=[jax.ShapeDtypeStruct((s, 512), BF16), jax.ShapeDtypeStruct((s, SSD_CONV_DIM), F32),
                   jax.ShapeDtypeStruct((s, LANES), BF16), jax.ShapeDtypeStruct((1, LANES), F32),
                   jax.ShapeDtypeStruct((1, LANES), F32), jax.ShapeDtypeStruct((1, LANES), F32),
                   jax.ShapeDtypeStruct((1, 512), F32)],
        scratch_shapes=[pltpu.VMEM((LANES, 512), F32)], compiler_params=_cp(("arbitrary",)),
    )(dy, sts, h, xbcc, xbcc, xbcc, h, *_ssd_consts(), dtb, alog, dsk, nw)


HG_LEVELS = 6


HG_SUMS = 1 + 2 * HG_LEVELS


def _hg_consts():
    n = HG_CHUNK
    t = np.arange(n)
    mats = [np.tril(np.ones((n, n), np.float32))]
    dq, dk, mk = [], [], []
    for lv in range(HG_LEVELS):
        b = 1 << lv
        start = (t // b) * b
        end = start + b - 1
        dq.append(((t[None, :] >= start[:, None]) & (t[None, :] <= t[:, None])).astype(np.float32))
        dk.append(((t[None, :] > t[:, None]) & (t[None, :] <= end[:, None])).astype(np.float32))
        mk.append((((t // b) % 2 == 1)[:, None] & ((t // b)[None, :] == (t // b)[:, None] - 1)).astype(np.float32))
    c1 = np.concatenate(mats + dq + dk, axis=0)
    return (jnp.asarray(c1, BF16), jnp.asarray(np.concatenate([c1, c1, c1], axis=1), BF16), jnp.asarray(np.stack(mk)))


@functools.partial(jax.custom_vjp, nondiff_argnums=(1, 2))
def _split_grid(x, nr, nc):
    br, bc = x.shape[0] // nr, x.shape[1] // nc
    return tuple(x[i * br:(i + 1) * br, j * bc:(j + 1) * bc] for i in range(nr) for j in range(nc))


def _split_grid_fwd(x, nr, nc):
    return _split_grid(x, nr, nc), None


def _split_grid_bwd(nr, nc, _, cts):
    rows = [cts[i * nc] if nc == 1 else jnp.concatenate(cts[i * nc:(i + 1) * nc], axis=1) for i in range(nr)]
    return (rows[0] if nr == 1 else jnp.concatenate(rows, axis=0),)


_split_grid.defvjp(_split_grid_fwd, _split_grid_bwd)


@jax.custom_vjp
def _level_sums(c1, c3, g):
    gh = g.astype(BF16)
    r1 = g - gh.astype(F32)
    gm = r1.astype(BF16)
    gl = (r1 - gm.astype(F32)).astype(BF16)
    return _dot(c3, jnp.concatenate([gh, gm, gl], axis=0), NN)


def _level_sums_fwd(c1, c3, g):
    return _level_sums(c1, c3, g), (c1, c3)


def _level_sums_bwd(res, ct):
    c1, c3 = res
    ch = ct.astype(BF16)
    cm = (ct - ch.astype(F32)).astype(BF16)
    return jnp.zeros_like(c1), jnp.zeros_like(c3), _dot(c1, ch, TN) + _dot(c1, cm, TN)


_level_sums.defvjp(_level_sums_fwd, _level_sums_bwd)


def _hg_chunk(c1, c3, masks, hq, hf, hi, hg, sts, lb, nw):
    q = _silu(hq)
    g = jnp.log(lb + (1.0 - lb) * jax.nn.sigmoid(hf))
    k = (1.0 - lb) * jax.nn.sigmoid(-hf)
    parts = _split_grid(_level_sums(c1, c3, g), HG_SUMS, 1)
    bc = parts[0]
    tot = jnp.sum(g, 0, keepdims=True)
    qe = _split_grid(q * jnp.exp(bc), 1, HG_HEADS)
    kd = _split_grid(k * jnp.exp(tot - bc), 1, HG_HEADS)
    etot = _split_grid(jnp.exp(tot), 1, HG_HEADS)
    qh, kh, vh, gh = [_split_grid(a, 1, HG_HEADS) for a in (q, k, hi, hg)]
    att = [None] * HG_HEADS
    for lv in range(HG_LEVELS):
        qt = _split_grid(q * jnp.exp(parts[1 + lv]), 1, HG_HEADS)
        kt = _split_grid(k * jnp.exp(parts[1 + HG_LEVELS + lv]), 1, HG_HEADS)
        for hd in range(HG_HEADS):
            t = masks[lv] * _bdot(qt[hd], kt[hd], NT)
            att[hd] = t if att[hd] is None else att[hd] + t
    outs, stn = [], []
    for hd in range(HG_HEADS):
        o = (_bdot(att[hd], vh[hd]) + jnp.sum(qh[hd] * kh[hd], 1, keepdims=True) * vh[hd] + _bdot(qe[hd], sts[hd], NT))
        stn.append(sts[hd] * etot[hd] + _bdot(vh[hd], kd[hd], TN))
        outs.append(o * lax.rsqrt(jnp.mean(o * o, 1, keepdims=True) + RMS_EPS) * nw * _silu(gh[hd]))
    return tuple(outs), tuple(stn)


HG_STEP = 4


def _hg_specs(rev, ng):
    ci = (lambda i: ng - 1 - i) if rev else (lambda i: i)
    n = HG_CHUNK
    act = [pl.BlockSpec((HG_STEP * n, 512), functools.partial(lambda i, c: (ci(i), c), c=c))
           for c in (AB_HQ // 512, AB_HF // 512, AB_HI // 512, AB_HG // 512)]
    other = [pl.BlockSpec((HG_SUMS * n, n), lambda i: (0, 0)), pl.BlockSpec((HG_SUMS * n, 3 * n), lambda i: (0, 0)),
             pl.BlockSpec((HG_LEVELS, n, n), lambda i: (0, 0, 0)), pl.BlockSpec((1, 512), lambda i: (0, 0)),
             pl.BlockSpec((1, LANES), lambda i: (0, 0))]
    return ci, act, other


def _head_rows(hd):
    return slice(LANES * hd, LANES * (hd + 1))


def _chunk_rows(u):
    return slice(HG_CHUNK * u, HG_CHUNK * (u + 1))


def _hg_fwd(h, lb, nw, name):
    s = h.shape[0]
    nc = s // HG_CHUNK
    ng = nc // HG_STEP
    ci, act, other = _hg_specs(False, ng)

    def kern(q_ref, f_ref, i_ref, g_ref, c1_ref, c3_ref, mk_ref, lb_ref, nw_ref, y_ref, sts_ref, st):
        @pl.when(pl.program_id(0) == 0)
        def _():
            st[...] = jnp.zeros_like(st)

        masks = [mk_ref[lv] for lv in range(HG_LEVELS)]
        for u in range(HG_STEP):
            rows = _chunk_rows(u)
            sts_ref[u] = st[...]
            outs, stn = _hg_chunk(c1_ref[...], c3_ref[...], masks, q_ref[rows, :], f_ref[rows, :], i_ref[rows, :],
                                  g_ref[rows, :], tuple(st[_head_rows(hd), :] for hd in range(HG_HEADS)), lb_ref[...],
                                  nw_ref[...])
            for hd in range(HG_HEADS):
                y_ref[rows, _head_rows(hd)] = outs[hd].astype(BF16)
                st[_head_rows(hd), :] = stn[hd]

    return pl.pallas_call(
        kern, name=name, grid=(ng,), in_specs=act + other,
        out_specs=[pl.BlockSpec((HG_STEP * HG_CHUNK, 512), lambda i: (i, 0)),
                   pl.BlockSpec((HG_STEP, 512, LANES), lambda i: (i, 0, 0))],
        out_shape=[jax.ShapeDtypeStruct((s, 512), BF16), jax.ShapeDtypeStruct((nc, 512, LANES), F32)],
        scratch_shapes=[pltpu.VMEM((512, LANES), F32)], compiler_params=_cp(("arbitrary",)),
    )(h, h, h, h, *_hg_consts(), lb, nw)


def _hg_bwd(dy, dy_cb, h, sts, lb, nw, name):
    s = h.shape[0]
    ng = s // HG_CHUNK // HG_STEP
    n = HG_STEP * HG_CHUNK
    ci, act, other = _hg_specs(True, ng)

    def kern(dy_ref, sts_ref, q_ref, f_ref, i_ref, g_ref, c1_ref, c3_ref, mk_ref, lb_ref, nw_ref,
             dq_ref, df_ref, di_ref, dg_ref, dlb_ref, dnw_ref, dst):
        @pl.when(pl.program_id(0) == 0)
        def _():
            dst[...] = jnp.zeros_like(dst)
            dlb_ref[...] = jnp.zeros_like(dlb_ref)
            dnw_ref[...] = jnp.zeros_like(dnw_ref)

        masks = [mk_ref[lv] for lv in range(HG_LEVELS)]
        heads = range(HG_HEADS)
        for u in range(HG_STEP - 1, -1, -1):
            rows = _chunk_rows(u)
            _, pull = jax.vjp(functools.partial(_hg_chunk, c1_ref[...], c3_ref[...], masks), q_ref[rows, :], f_ref[rows, :],
                              i_ref[rows, :], g_ref[rows, :], tuple(sts_ref[u, _head_rows(hd), :] for hd in heads),
                              lb_ref[...], nw_ref[...])
            dq, df, di, dg, dsti, dlb, dnw = pull((tuple(dy_ref[rows, _head_rows(hd)] for hd in heads),
                                                   tuple(dst[_head_rows(hd), :] for hd in heads)))
            dq_ref[rows, :] = dq.astype(BF16)
            df_ref[rows, :] = df.astype(BF16)
            di_ref[rows, :] = di.astype(BF16)
            dg_ref[rows, :] = dg.astype(BF16)
            for hd in heads:
                dst[_head_rows(hd), :] = dsti[hd]
            dlb_ref[...] += dlb
            dnw_ref[...] += dnw

    blk = pl.BlockSpec((n, 512), lambda i: (ci(i), 0))
    return pl.pallas_call(
        kern, name=name, grid=(ng,),
        in_specs=[pl.BlockSpec((n, 512), lambda i: (ci(i), dy_cb)), pl.BlockSpec((HG_STEP, 512, LANES), lambda i: (ci(i), 0, 0))]
        + act + other,
        out_specs=[blk, blk, blk, blk, pl.BlockSpec((1, 512), lambda i: (0, 0)), pl.BlockSpec((1, LANES), lambda i: (0, 0))],
        out_shape=[jax.ShapeDtypeStruct((s, 512), BF16)] * 4 + [jax.ShapeDtypeStruct((1, 512), F32),
                                                                jax.ShapeDtypeStruct((1, LANES), F32)],
        scratch_shapes=[pltpu.VMEM((512, LANES), F32)], compiler_params=_cp(("arbitrary",)),
    )(dy, sts, h, h, h, h, *_hg_consts(), lb, nw)


def _swa_consts():
    n = SWA_BLOCK
    r = np.arange(8 * n)
    t, blk = r % n, r // n
    lanes = np.arange(LANES)
    own = (t[:, None] >= lanes[None, :]).astype(np.float32)
    prev = (lanes[None, :] > t[:, None]).astype(np.float32)
    sel = (lanes[None, :] == (blk // 2 + 4 * (blk % 2))[:, None]).astype(np.float32)
    grp = ((lanes[None, :] // 64) == (blk % 2)[:, None]).astype(np.float32)
    return tuple(jnp.asarray(a) for a in (own, prev, sel, grp))


def _swa_block(own, prev, sel, grp, q, k0, v0, k1, v1, srow):
    qs = _split_grid(q, 1, 4)
    qa = jnp.concatenate([qs[p] for p in range(4) for _ in range(2)], axis=0) * grp
    s0 = jnp.where(prev > 0.0, _bdot(qa, k0, NT) * 0.125, MASK_VALUE)
    s1 = jnp.where(own > 0.0, _bdot(qa, k1, NT) * 0.125, MASK_VALUE)
    sk = jnp.sum(sel * srow, 1, keepdims=True)
    m = lax.stop_gradient(jnp.maximum(jnp.maximum(jnp.max(s0, 1, keepdims=True), jnp.max(s1, 1, keepdims=True)), sk))
    p0, p1 = jnp.exp(s0 - m), jnp.exp(s1 - m)
    inv = 1.0 / (jnp.sum(p0, 1, keepdims=True) + jnp.sum(p1, 1, keepdims=True) + jnp.exp(sk - m))
    o = _split_grid((_bdot(p0 * inv, v0) + _bdot(p1 * inv, v1)) * grp, 8, 1)
    return tuple(o[2 * p] + o[2 * p + 1] for p in range(4))


SWA_STEP = 2


def _swa_specs():
    n = SWA_BLOCK
    pv = lambda i: jnp.maximum(SWA_STEP * i - 1, 0)
    mask = pl.BlockSpec((8 * n, LANES), lambda i: (0, 0))
    return [mask, mask, mask, mask, pl.BlockSpec((SWA_STEP * n, 512), lambda i: (i, CD_Q // 512)),
            pl.BlockSpec((n, LANES), lambda i: (pv(i), CD_K // LANES)), pl.BlockSpec((n, LANES), lambda i: (pv(i), CD_V // LANES)),
            pl.BlockSpec((SWA_STEP * n, LANES), lambda i: (i, CD_K // LANES)),
            pl.BlockSpec((SWA_STEP * n, LANES), lambda i: (i, CD_V // LANES)), pl.BlockSpec((1, LANES), lambda i: (0, 0))]


def _swa_args(refs, u, notfirst):
    own, prev, sel, grp = (r[...] for r in refs[:4])
    q_ref, k0_ref, v0_ref, k_ref, v_ref, s_ref = refs[4:]
    rows = slice(SWA_BLOCK * u, SWA_BLOCK * (u + 1))
    if u == 0:
        k0, v0, prev = k0_ref[...], v0_ref[...], prev * notfirst.astype(F32)
    else:
        before = slice(SWA_BLOCK * (u - 1), SWA_BLOCK * u)
        k0, v0 = k_ref[before, :], v_ref[before, :]
    return (own, prev, sel, grp), (q_ref[rows, :], k0, v0, k_ref[rows, :], v_ref[rows, :], s_ref[...])


def _swa_fwd(hc, srow, name):
    s = hc.shape[0]
    n = SWA_STEP * SWA_BLOCK

    def kern(*refs):
        y_ref = refs[10]
        for u in range(SWA_STEP):
            consts, acts = _swa_args(refs[:10], u, pl.program_id(0) > 0)
            outs = _swa_block(*consts, *acts)
            for p in range(4):
                y_ref[SWA_BLOCK * u:SWA_BLOCK * (u + 1), LANES * p:LANES * (p + 1)] = outs[p].astype(BF16)

    return pl.pallas_call(
        kern, name=name, grid=(s // n,), in_specs=_swa_specs(), out_specs=pl.BlockSpec((n, 512), lambda i: (i, 0)),
        out_shape=jax.ShapeDtypeStruct((s, 512), BF16), compiler_params=_cp(("parallel",)),
    )(*_swa_consts(), hc, hc, hc, hc, hc, srow)


def _swa_bwd(dy, dy_cb, hc, srow, name):
    s = hc.shape[0]
    n = SWA_STEP * SWA_BLOCK
    nb = SWA_BLOCK

    def kern(dy_ref, *refs):
        dq_ref, dk_ref, dv_ref, ds_ref = refs[10:]
        i = pl.program_id(0)

        @pl.when(i == 0)
        def _():
            dk_ref[...] = jnp.zeros_like(dk_ref)
            dv_ref[...] = jnp.zeros_like(dv_ref)
            ds_ref[...] = jnp.zeros_like(ds_ref)

        for u in range(SWA_STEP):
            blk = SWA_STEP * i + u
            rows = slice(nb * u, nb * (u + 1))
            r1 = pl.ds(pl.multiple_of(blk * nb, nb), nb)
            r0 = pl.ds(pl.multiple_of(jnp.maximum(blk - 1, 0) * nb, nb), nb)
            consts, acts = _swa_args(refs[:10], u, i > 0)
            _, pull = jax.vjp(functools.partial(_swa_block, *consts), *acts)
            dq, dk0, dv0, dk1, dv1, dsr = pull(tuple(dy_ref[rows, LANES * p:LANES * (p + 1)] for p in range(4)))
            dq_ref[rows, :] = dq.astype(BF16)
            dk_ref[r0, :] += dk0
            dv_ref[r0, :] += dv0
            dk_ref[r1, :] += dk1
            dv_ref[r1, :] += dv1
            ds_ref[...] += dsr

    full = pl.BlockSpec((s, LANES), lambda i: (0, 0))
    return pl.pallas_call(
        kern, name=name, grid=(s // n,), in_specs=[pl.BlockSpec((n, 512), lambda i: (i, dy_cb))] + _swa_specs(),
        out_specs=[pl.BlockSpec((n, 512), lambda i: (i, 0)), full, full, pl.BlockSpec((1, LANES), lambda i: (0, 0))],
        out_shape=[jax.ShapeDtypeStruct((s, 512), BF16), jax.ShapeDtypeStruct((s, LANES), F32),
                   jax.ShapeDtypeStruct((s, LANES), F32), jax.ShapeDtypeStruct((1, LANES), F32)],
        compiler_params=_cp(("arbitrary",)),
    )(dy, *_swa_consts(), hc, hc, hc, hc, hc, srow)


RG_TILE = 512


def _neg_expm1(x):
    ser = x * (1 + x / 2 * (1 + x / 3 * (1 + x / 4 * (1 + x / 5 * (1 + x / 6 * (1 + x / 7 * (1 + x / 8)))))))
    return jnp.where(x > -0.25, -ser, 1.0 - jnp.exp(x))


def _rg_gates(xc, wa, wx, ba, bx, lam):
    r = jax.nn.sigmoid(_bdot(xc, wa) + ba)
    i = jax.nn.sigmoid(_bdot(xc, wx) + bx)
    log_a = -RG_C * r * jax.nn.softplus(-lam)
    u = jnp.sqrt(jnp.maximum(_neg_expm1(2.0 * log_a), 0.0)) * (i * xc)
    return jnp.exp(log_a), u


def _rg_out(hs, gate):
    return hs * jax.nn.gelu(gate)


def _rows_to_tile(rows):
    sub = _iota((SUBLANES, 1), 0)
    out = jnp.broadcast_to(rows[0], (SUBLANES, rows[0].shape[1]))
    for j in range(1, SUBLANES):
        out = jnp.where(sub == j, rows[j], out)
    return out


def _rg_fwd(xc, hc, wa, wx, ba, bx, lam, name):
    s = xc.shape[0]
    tm = min(RG_TILE, s)

    def kern(x_ref, g_ref, wa_ref, wx_ref, ba_ref, bx_ref, lam_ref, y_ref, h_ref, a_s, u_s, carry):
        @pl.when(pl.program_id(0) == 0)
        def _():
            carry[...] = jnp.zeros_like(carry)

        a, u = _rg_gates(x_ref[...], wa_ref[...], wx_ref[...], ba_ref[...], bx_ref[...], lam_ref[...])
        a_s[...] = a
        u_s[...] = u

        def body(gi, hp):
            r0 = pl.multiple_of(gi * SUBLANES, SUBLANES)
            ab, ub = a_s[pl.ds(r0, SUBLANES), :], u_s[pl.ds(r0, SUBLANES), :]
            rows = []
            for j in range(SUBLANES):
                hp = ab[j:j + 1] * hp + ub[j:j + 1]
                rows.append(hp)
            h_ref[pl.ds(r0, SUBLANES), :] = _rows_to_tile(rows)
            return hp

        carry[0:1, :] = lax.fori_loop(0, tm // SUBLANES, body, carry[0:1, :])
        y_ref[...] = _rg_out(h_ref[...], g_ref[...]).astype(BF16)

    row = pl.BlockSpec((tm, 512), lambda i: (i, 0))
    mat = pl.BlockSpec((512, 512), lambda i: (0, 0))
    vec = pl.BlockSpec((1, 512), lambda i: (0, 0))
    return pl.pallas_call(
        kern, name=name, grid=(s // tm,),
        in_specs=[row, pl.BlockSpec((tm, 512), lambda i: (i, CD_GATE // 512)), mat, mat, vec, vec, vec],
        out_specs=[row, row],
        out_shape=[jax.ShapeDtypeStruct((s, 512), BF16), jax.ShapeDtypeStruct((s, 512), F32)],
        scratch_shapes=[pltpu.VMEM((tm, 512), F32), pltpu.VMEM((tm, 512), F32), pltpu.VMEM((SUBLANES, 512), F32)],
        compiler_params=_cp(("arbitrary",)),
    )(xc, hc, wa, wx, ba, bx, lam)


def _rg_bwd(dy, dy_cb, xc, hc, hs, wa, wx, ba, bx, lam, name):
    s = xc.shape[0]
    tm = min(RG_TILE, s)
    nt = s // tm
    ti = lambda i: nt - 1 - i

    def kern(dy_ref, x_ref, g_ref, h_ref, hp_ref, wa_ref, wx_ref, ba_ref, bx_ref, lam_ref,
             dx_ref, dg_ref, dwa_ref, dwx_ref, dba_ref, dbx_ref, dlam_ref, a_s, d_s, g_s, cg, ca):
        first = pl.program_id(0) == 0

        @pl.when(first)
        def _():
            cg[...] = jnp.zeros_like(cg)
            ca[...] = jnp.zeros_like(ca)

        (a, _), pull = jax.vjp(_rg_gates, x_ref[...], wa_ref[...].astype(F32), wx_ref[...].astype(F32), ba_ref[...],
                               bx_ref[...], lam_ref[...])
        hs_t = h_ref[...]
        _, pull_out = jax.vjp(_rg_out, hs_t, g_ref[...])
        dho, dgate = pull_out(dy_ref[...])
        dg_ref[...] = dgate.astype(BF16)
        a_s[...] = a
        d_s[...] = dho

        def body(k, c):
            gn, an = c
            r0 = pl.multiple_of((tm // SUBLANES - 1 - k) * SUBLANES, SUBLANES)
            ab, db = a_s[pl.ds(r0, SUBLANES), :], d_s[pl.ds(r0, SUBLANES), :]
            rows = [None] * SUBLANES
            for j in range(SUBLANES - 1, -1, -1):
                gn = db[j:j + 1] + an * gn
                an = ab[j:j + 1]
                rows[j] = gn
            g_s[pl.ds(r0, SUBLANES), :] = _rows_to_tile(rows)
            return gn, an

        gn, an = lax.fori_loop(0, tm // SUBLANES, body, (cg[0:1, :], ca[0:1, :]))
        cg[0:1, :] = gn
        ca[0:1, :] = an
        g = g_s[...]
        halo = jnp.where(pl.program_id(0) == nt - 1, 0.0, hp_ref[...])
        hprev = pltpu.roll(jnp.concatenate([halo, hs_t], axis=0), 1, axis=0)[SUBLANES:]
        dxc, dwa, dwx, dba, dbx, dlam = pull((g * hprev, g))
        dx_ref[...] = dxc
        _acc(dwa_ref, dwa, first)
        _acc(dwx_ref, dwx, first)
        _acc(dba_ref, dba, first)
        _acc(dbx_ref, dbx, first)
        _acc(dlam_ref, dlam, first)

    row = pl.BlockSpec((tm, 512), lambda i: (ti(i), 0))
    mat = pl.BlockSpec((512, 512), lambda i: (0, 0))
    vec = pl.BlockSpec((1, 512), lambda i: (0, 0))
    hprev = pl.BlockSpec((SUBLANES, 512), lambda i: (jnp.maximum(ti(i) * (tm // SUBLANES) - 1, 0), 0))
    return pl.pallas_call(
        kern, name=name, grid=(nt,),
        in_specs=[pl.BlockSpec((tm, 512), lambda i: (ti(i), dy_cb)), row,
                  pl.BlockSpec((tm, 512), lambda i: (ti(i), CD_GATE // 512)), row, hprev, mat, mat, vec, vec, vec],
        out_specs=[row, row, mat, mat, vec, vec, vec],
        out_shape=[jax.ShapeDtypeStruct((s, 512), F32), jax.ShapeDtypeStruct((s, 512), BF16),
                   jax.ShapeDtypeStruct((512, 512), F32), jax.ShapeDtypeStruct((512, 512), F32),
                   jax.ShapeDtypeStruct((1, 512), F32), jax.ShapeDtypeStruct((1, 512), F32), jax.ShapeDtypeStruct((1, 512), F32)],
        scratch_shapes=[pltpu.VMEM((tm, 512), F32)] * 3 + [pltpu.VMEM((SUBLANES, 512), F32)] * 2,
        compiler_params=_cp(("arbitrary",)),
    )(dy, xc, hc, hs, hs, wa, wx, ba, bx, lam)


AB_REF = dict(z=(0, 512), xbc=(512, 768), dt=(1280, 8), hq=(1288, 512), hf=(1800, 512), hi=(2312, 512), hg=(2824, 512))
AB_INT = dict(z=AB_Z, hq=AB_HQ, hf=AB_HF, hi=AB_HI, hg=AB_HG, xbc=AB_XBC, dt=AB_DT)
CD_REF = dict(q=(0, 512), k=(512, 128), v=(640, 128), gate=(768, 512), xr=(1280, 512))


def _cols(w, start, size):
    return lax.slice_in_dim(w, start, start + size, axis=w.ndim - 1)


def _pad_lanes(v):
    return jnp.concatenate([v.astype(F32), jnp.zeros((LANES - v.shape[0],), F32)]).reshape(1, LANES)


def _taker(g, layer, axis):
    width = g.shape[2 + axis]

    def take(start, size):
        out = []
        while size > 0:
            k, off = divmod(start, width)
            n = min(size, width - off)
            out.append(lax.slice_in_dim(g[k, layer], off, off + n, axis=axis))
            start, size = start + n, size - n
        return out

    return take


def _ab_in_to_int(g, layer):
    take = _taker(g, layer, 1)
    parts = sum([take(*AB_REF[k]) for k in ("z", "hq", "hf", "hi", "hg", "xbc", "dt")], [])
    return jnp.concatenate(parts + [jnp.zeros((g.shape[2], AB_COLS - AB_DT - 8), g.dtype)], axis=1)


def _ab_in_to_ref(w):
    return jnp.concatenate([_cols(w, AB_INT[k], AB_REF[k][1]) for k in ("z", "xbc", "dt", "hq", "hf", "hi", "hg")], axis=-1)


def _pairs_to_ref(w, axis):
    sl = lambda i: lax.slice_in_dim(w, 64 * i, 64 * i + 64, axis=axis)
    return jnp.concatenate([sl(2 * h) for h in range(4)] + [sl(2 * h + 1) for h in range(4)], axis=axis)


PAIR_ORDER = tuple(h for p in range(4) for h in (p, 4 + p))


def _cd_in_to_int(g, layer):
    take = _taker(g, layer, 1)
    q = sum([take(64 * h, 64) for h in PAIR_ORDER], [])
    return jnp.concatenate(q + sum([take(*CD_REF[k]) for k in ("gate", "xr", "k", "v")], []), axis=1)


def _cd_out_to_int(g, layer):
    take = _taker(g, layer, 0)
    return jnp.concatenate(sum([take(64 * h, 64) for h in PAIR_ORDER], []) + take(512, 512), axis=0)


def _whole(g, layer, axis):
    return jnp.concatenate([g[k, layer] for k in range(4)], axis=axis)


def _cd_in_to_ref(w):
    q = _pairs_to_ref(_cols(w, CD_Q, 512), w.ndim - 1)
    return jnp.concatenate([q, _cols(w, CD_K, 128), _cols(w, CD_V, 128), _cols(w, CD_GATE, 512), _cols(w, CD_XR, 512)], axis=-1)


def _block_diag(w):
    tiled = jnp.concatenate([w.reshape(RG_WIDTH, 64)] * 8, axis=1)
    return tiled * _BLOCK_DIAG_MASK.astype(w.dtype)


_BLOCK_DIAG_MASK = np.kron(np.eye(8, dtype=np.float32), np.ones((64, 64), np.float32))


def _diag_blocks(m):
    return jnp.sum((m * _BLOCK_DIAG_MASK).reshape(8, 64, 8, 64), axis=2)


def _conv8(w):
    return jnp.concatenate([w, jnp.zeros((SUBLANES - w.shape[0], w.shape[1]), w.dtype)], axis=0)


def _hg_lower_bounds(hg_lower):
    sm = jax.nn.softmax(hg_lower.astype(F32), axis=0)
    return jnp.clip(jnp.cumsum(sm, axis=0) - sm[0], 0.0, 1.0)


def _local_step(x, target, w):
    s = x.shape[0]
    gup = w["ffn_w_up"]
    wdowns = [_whole(w["ffn_w_down"], l, 0) for l in range(DEPTH)]
    lb_all, lb_pull = jax.vjp(_hg_lower_bounds, w["hg_lower"])
    grads = {k: [None] * (v.shape[1] if k in BIG else v.shape[0]) for k, v in w.items()}
    saved = []
    xb = x.astype(BF16)
    for l in range(DEPTH):
        j = l // 2
        t = f"l{l}"
        sv = dict(x=x, xb=xb)
        if l % 2 == 0:
            win = _ab_in_to_int(w["ab_w_in"], j)
            wout = _whole(w["ab_w_out"], j, 0)
            h = _mm([(xb, 0, win, 0, D_MODEL)], AB_COLS, F32, t + "_in")
            w8 = _conv8(w["ssd_conv_w"][j])
            xbcc = _dwconv_fwd(h, AB_XBC // CONV_CB, SSD_CONV_DIM, w8, w["ssd_conv_b"][j], 4, t + "_conv")
            sp = (_pad_lanes(w["ssd_dt_bias"][j]), _pad_lanes(w["ssd_a_log"][j]), _pad_lanes(w["ssd_d"][j]),
                  w["ssd_norm_w"][j].reshape(1, 512))
            ya, ssts = _ssd_fwd(h, xbcc, *sp, t + "_ssd")
            hp = (lb_all[j].reshape(1, 512), w["hg_norm_w"][j].reshape(1, LANES))
            yb, hsts = _hg_fwd(h, *hp, t + "_hg")
            sv.update(win=win, wout=wout, h=h, w8=w8, xbcc=xbcc, sp=sp, ssts=ssts, hp=hp, hsts=hsts)
        else:
            win = _cd_in_to_int(w["cd_w_in"], j)
            wout = _cd_out_to_int(w["cd_w_out"], j)
            h = _mm([(xb, 0, win, 0, D_MODEL)], CD_COLS, F32, t + "_in")
            srow = _pad_lanes(w["swa_sinks"][j])
            ya = _swa_fwd(h, srow, t + "_swa")
            w8 = _conv8(w["rg_conv_w"][j])
            xc = _dwconv_fwd(h, CD_XR // CONV_CB, RG_WIDTH, w8, w["rg_conv_b"][j], 4, t + "_conv")
            rp = (_block_diag(w["rg_wa"][j]).astype(BF16), _block_diag(w["rg_wx"][j]).astype(BF16),
                  w["rg_ba"][j].reshape(1, 512), w["rg_bx"][j].reshape(1, 512), w["rg_lambda"][j].reshape(1, 512))
            yb, hs = _rg_fwd(xc, h, *rp, t + "_rg")
            sv.update(win=win, wout=wout, h=h, w8=w8, srow=srow, xc=xc, rp=rp, hs=hs)
        x1, x1b, r1 = _mm_ln([(ya, 0, wout, 0, 512), (yb, 0, wout, 1, 512)], x, w["ln_g"][l, 0], w["ln_b"][l, 0], t + "_out_ln")
        wdown = wdowns[l]
        fw8 = _conv8(w["ffn_conv_w"][l])
        hu = _mm([(x1b, 0, gup, _shard_as_col_tile(gup, l), D_MODEL)], 2 * FFN_DIM, BF16, t + "_up")
        a, hg, hv = _ffn_gate_fwd(hu, fw8, w["ffn_conv_b"][l], t + "_gate")
        x, xb, r2 = _mm_ln([(a, 0, wdown, 0, FFN_DIM)], x1, w["ln_g"][l, 1], w["ln_b"][l, 1], t + "_down_ln")
        sv.update(ya=ya, yb=yb, r1=r1, x1b=x1b, hu=hu, hg=hg, hv=hv, a=a, r2=r2, fw8=fw8)
        saved.append(sv)

    loss = 0.5 * jnp.sum(_sqerr(x, target, "sqerr")) / D_MODEL
    dya, dyb, ca, cb = x, target, 1.0 / D_MODEL, -1.0 / D_MODEL
    for l in range(DEPTH - 1, -1, -1):
        j = l // 2
        t = f"l{l}b"
        sv = saved[l]
        wdown = wdowns[l]
        dr2, dr2b, dg2, db2, da = _ln_bwd(dya, dyb, ca, cb, sv["r2"], w["ln_g"][l, 1], wdown, BF16, t + "_ln2_down")
        grads["ffn_w_down"][l] = _mm_tn(sv["a"], 0, FFN_DIM, dr2b, 0, D_MODEL, t + "_wdown")
        dhg, dhu, dfw8, dfb = _ffn_gate_bwd(sv["hu"], sv["hg"], sv["hv"], da, sv["fw8"], t + "_gate")
        grads["ffn_conv_w"][l] = dfw8[:3]
        grads["ffn_conv_b"][l] = dfb
        quarter = gup.shape[3]
        dx1 = _mm([(dh, half, gup, _shard_transposed(gup, 2 * gi + half, l), quarter)
                   for gi, dh in enumerate((dhg, dhu)) for half in range(2)], D_MODEL, F32, t + "_up", trans_b=True)
        grads["ffn_w_up"][l] = jnp.concatenate([_mm_tn(sv["x1b"], 0, D_MODEL, dhg, 0, FFN_DIM, t + "_wup_g"),
                                                _mm_tn(sv["x1b"], 0, D_MODEL, dhu, 0, FFN_DIM, t + "_wup_u")], axis=1)
        win, wout, h = sv["win"], sv["wout"], sv["h"]
        dr1, dr1b, dg1, db1, dycat = _ln_bwd(dr2, dx1, ALPHA, 1.0, sv["r1"], w["ln_g"][l, 0], wout, F32, t + "_ln1_out")
        grads["ln_g"][l] = jnp.stack([dg1, dg2])
        grads["ln_b"][l] = jnp.stack([db1, db2])
        dwout = jnp.concatenate([_mm_tn(sv["ya"], 0, 512, dr1b, 0, D_MODEL, t + "_wout_a"),
                                 _mm_tn(sv["yb"], 0, 512, dr1b, 0, D_MODEL, t + "_wout_b")], axis=0)
        if l % 2 == 0:
            dz, dxbcc, ddt, ddtb, dal, ddsk, dnw = _ssd_bwd(dycat, 0, h, sv["xbcc"], sv["ssts"], *sv["sp"], t + "_ssd")
            dxbc, dcw, dcb = _dwconv_bwd(dxbcc, h, AB_XBC // CONV_CB, SSD_CONV_DIM, sv["w8"], 4, t + "_conv")
            dq, df, di, dg, dlb, dhnw = _hg_bwd(dycat, 1, h, sv["hsts"], *sv["hp"], t + "_hg")
            grads["ab_w_out"][j] = dwout
            grads["ssd_conv_w"][j], grads["ssd_conv_b"][j] = dcw[:4], dcb
            grads["ssd_dt_bias"][j], grads["ssd_a_log"][j], grads["ssd_d"][j] = ddtb[0, :8], dal[0, :8], ddsk[0, :8]
            grads["ssd_norm_w"][j], grads["hg_norm_w"][j] = dnw[0], dhnw[0]
            grads["hg_lower"][j] = dlb[0]
            pieces = [(dz, 0, AB_Z, 512), (dq, 0, AB_HQ, 512), (df, 0, AB_HF, 512), (di, 0, AB_HI, 512), (dg, 0, AB_HG, 512),
                      (dxbc, 0, AB_XBC, 256), (dxbc, 1, AB_XBC + 256, 256), (dxbc, 2, AB_XBC + 512, 256), (ddt, 0, AB_DT, LANES)]
            dwin = [_mm_tn(sv["xb"], 0, D_MODEL, p, 0, p.shape[1], t + f"_win{i}")
                    for i, p in enumerate((dz, dq, df, di, dg, dxbc, ddt))]
            grads["ab_w_in"][j] = _ab_in_to_ref(jnp.concatenate(dwin, axis=1))
        else:
            dq, dk, dv, dsr = _swa_bwd(dycat, 0, h, sv["srow"], t + "_swa")
            dxc, dgate, dwa, dwx, dba, dbx, dlam = _rg_bwd(dycat, 1, sv["xc"], h, sv["hs"], *sv["rp"], t + "_rg")
            dxr, dcw, dcb = _dwconv_bwd(dxc, h, CD_XR // CONV_CB, RG_WIDTH, sv["w8"], 4, t + "_conv")
            grads["cd_w_out"][j] = jnp.concatenate([_pairs_to_ref(dwout[:512], 0), dwout[512:]], axis=0)
            grads["swa_sinks"][j] = dsr[0, :8]
            grads["rg_conv_w"][j], grads["rg_conv_b"][j] = dcw[:4], dcb
            grads["rg_wa"][j], grads["rg_wx"][j] = _diag_blocks(dwa), _diag_blocks(dwx)
            grads["rg_ba"][j], grads["rg_bx"][j], grads["rg_lambda"][j] = dba[0], dbx[0], dlam[0]
            pieces = [(dq, 0, CD_Q, 512), (dgate, 0, CD_GATE, 512), (dxr, 0, CD_XR, 512), (dk, 0, CD_K, LANES), (dv, 0, CD_V, LANES)]
            dwin = [_mm_tn(sv["xb"], 0, D_MODEL, p, 0, p.shape[1], t + f"_win{i}")
                    for i, p in enumerate((dq, dgate, dxr, dk, dv))]
            grads["cd_w_in"][j] = _cd_in_to_ref(jnp.concatenate(dwin, axis=1))
        dx0 = _mm([(p, pcb, win, off // k, k) for p, pcb, off, k in pieces], D_MODEL, F32, t + "_in", trans_b=True)
        dya, dyb, ca, cb = dr1, dx0, ALPHA, 1.0
    grad_x = _axpby(dya, dyb, ca, cb, "grad_x")
    out = {k: (v if k in BIG else jnp.stack(v)) for k, v in grads.items()}
    (out["hg_lower"],) = lb_pull(out["hg_lower"])
    return loss, grad_x, out


MESH = pl.DeviceIdType.MESH
ANY = pl.BlockSpec(memory_space=pl.ANY)


def _place():
    x, y, c = lax.axis_index("x"), lax.axis_index("y"), lax.axis_index("c")
    return x, y, c, [(1 - x, y), (x, 1 - y), (1 - x, 1 - y)]


def _rcopy(src, dst, send_sems, recv_sems, k, to):
    return pltpu.make_async_remote_copy(src_ref=src, dst_ref=dst, send_sem=send_sems.at[k], recv_sem=recv_sems.at[k],
                                        device_id=to, device_id_type=MESH)


def _zeros_index(ndim):
    return (0,) * ndim


def _gather_chips(parts, name):
    n = len(parts)

    def body(*refs):
        w_refs, out_refs, (send_sems, recv_sems) = refs[:n], refs[n:2 * n], refs[2 * n:]
        x, y, c, chips = _place()
        me, sib = 2 * x + y, (x, y, 1 - c)
        first = [_rcopy(w_refs[a].at[c], out_refs[a].at[me, c], send_sems, recv_sems, 6 * a + k, (cx, cy, c))
                 for a in range(n) for k, (cx, cy) in enumerate(chips)]
        for cp in first:
            cp.start()
        passed = []
        for a in range(n):
            for k, (cx, cy) in enumerate(chips):
                blk = out_refs[a].at[2 * cx + cy, c]
                _rcopy(blk, blk, send_sems, recv_sems, 6 * a + k, (cx, cy, c)).wait_recv()
                passed.append(_rcopy(blk, blk, send_sems, recv_sems, 6 * a + 3 + k, sib))
                passed[-1].start()
        for a in range(n):
            for k, (cx, cy) in enumerate(chips):
                blk = out_refs[a].at[2 * cx + cy, 1 - c]
                _rcopy(blk, blk, send_sems, recv_sems, 6 * a + 3 + k, sib).wait_recv()
        for cp in first + passed:
            cp.wait_send()

    outs = pl.pallas_call(
        body, name=name, in_specs=[ANY] * n, out_specs=[ANY] * n,
        out_shape=[jax.ShapeDtypeStruct((4,) + p.shape, p.dtype) for p in parts],
        scratch_shapes=[pltpu.SemaphoreType.DMA((6 * n,)), pltpu.SemaphoreType.DMA((6 * n,))],
    )(*parts)
    me = 2 * lax.axis_index("x") + lax.axis_index("y")
    return [lax.dynamic_update_slice(o, p[None], (me,) + _zeros_index(p.ndim)) for o, p in zip(outs, parts)]


def _swap_halves(gs, name):
    n = len(gs)

    def body(*refs):
        g_refs, x_refs, (send_sems, recv_sems) = refs[:n], refs[n:2 * n], refs[2 * n:]
        x, y, c, _ = _place()
        cps = [_rcopy(g_refs[a].at[:, 1 - c], x_refs[a], send_sems, recv_sems, a, (x, y, 1 - c)) for a in range(n)]
        for cp in cps:
            cp.start()
        for cp in cps:
            cp.wait()

    return pl.pallas_call(
        body, name=name, in_specs=[ANY] * n, out_specs=[ANY] * n,
        out_shape=[jax.ShapeDtypeStruct((g.shape[0],) + g.shape[2:], g.dtype) for g in gs],
        scratch_shapes=[pltpu.SemaphoreType.DMA((n,)), pltpu.SemaphoreType.DMA((n,))],
    )(*gs)


def _scatter_chips(hbs, name):
    n = len(hbs)

    def body(*refs):
        h_refs, y_refs, (send_sems, recv_sems) = refs[:n], refs[n:2 * n], refs[2 * n:]
        x, y, c, chips = _place()
        me = 2 * x + y
        sends = [_rcopy(h_refs[a].at[2 * cx + cy], y_refs[a].at[me], send_sems, recv_sems, 3 * a + k, (cx, cy, c))
                 for a in range(n) for k, (cx, cy) in enumerate(chips)]
        for cp in sends:
            cp.start()
        for a in range(n):
            for k, (cx, cy) in enumerate(chips):
                blk = y_refs[a].at[2 * cx + cy]
                _rcopy(blk, blk, send_sems, recv_sems, 3 * a + k, (cx, cy, c)).wait_recv()
        for cp in sends:
            cp.wait_send()

    outs = pl.pallas_call(
        body, name=name, in_specs=[ANY] * n, out_specs=[ANY] * n,
        out_shape=[jax.ShapeDtypeStruct(h.shape, h.dtype) for h in hbs],
        scratch_shapes=[pltpu.SemaphoreType.DMA((3 * n,)), pltpu.SemaphoreType.DMA((3 * n,))],
    )(*hbs)
    me = 2 * lax.axis_index("x") + lax.axis_index("y")
    return [lax.dynamic_update_slice(o, lax.dynamic_slice_in_dim(h, me, 1, axis=0), (me,) + _zeros_index(h.ndim - 1))
            for o, h in zip(outs, hbs)]


def _share_half(fs, name):
    n = len(fs)

    def body(*refs):
        f_refs, out_refs, (send_sems, recv_sems) = refs[:n], refs[n:2 * n], refs[2 * n:]
        x, y, c, _ = _place()
        sib = (x, y, 1 - c)
        cps = [_rcopy(f_refs[a], out_refs[a].at[c], send_sems, recv_sems, a, sib) for a in range(n)]
        for cp in cps:
            cp.start()
        for a in range(n):
            _rcopy(f_refs[a], out_refs[a].at[1 - c], send_sems, recv_sems, a, sib).wait_recv()
        for cp in cps:
            cp.wait_send()

    outs = pl.pallas_call(
        body, name=name, in_specs=[ANY] * n, out_specs=[ANY] * n,
        out_shape=[jax.ShapeDtypeStruct((2,) + f.shape, f.dtype) for f in fs],
        scratch_shapes=[pltpu.SemaphoreType.DMA((n,)), pltpu.SemaphoreType.DMA((n,))],
    )(*fs)
    c = lax.axis_index("c")
    return [lax.dynamic_update_slice(o, f[None], (c,) + _zeros_index(f.ndim)) for o, f in zip(outs, fs)]


def _gather_devices(v, name):
    m_per, n = v.shape

    def body(x_ref, out_ref, send_sems, recv_sems, local_sem):
        x, y, c, chips = _place()
        me, sib = (x, y, c), (x, y, 1 - c)

        def rows(px, py, pc):
            return out_ref.at[pl.ds((4 * px + 2 * py + pc) * m_per, m_per), :]

        def copy(k, block, to, src=None):
            return _rcopy(rows(*block) if src is None else src, rows(*block), send_sems, recv_sems, k, to)

        mine = pltpu.make_async_copy(x_ref, rows(*me), local_sem)
        mine.start()
        first = [copy(0, me, sib, src=x_ref)]
        first += [copy(1 + j, me, (*chip, c), src=x_ref) for j, chip in enumerate(chips)]
        for cp in first:
            cp.start()
        passed = [copy(4 + j, (*chip, c), sib) for j, chip in enumerate(chips)]
        for j, chip in enumerate(chips):
            copy(1 + j, (*chip, c), me).wait_recv()
            passed[j].start()
        copy(0, sib, me).wait_recv()
        for j, chip in enumerate(chips):
            copy(4 + j, (*chip, 1 - c), me).wait_recv()
        for cp in first + passed:
            cp.wait_send()
        mine.wait()

    return pl.pallas_call(
        body, name=name, out_shape=jax.ShapeDtypeStruct((8 * m_per, n), v.dtype),
        in_specs=[pl.BlockSpec(memory_space=pltpu.VMEM)], out_specs=pl.BlockSpec(memory_space=pltpu.VMEM),
        scratch_shapes=[pltpu.SemaphoreType.DMA((7,)), pltpu.SemaphoreType.DMA((7,)), pltpu.SemaphoreType.DMA(())],
        compiler_params=_cp(),
    )(v)


ROW_TILES = (512, 256, 352, 128)


def _add_half(g, xr, c, name):
    _, _, r, n = g.shape
    tr = _pick(r, ROW_TILES)

    def kern(c_ref, g_ref, x_ref, o_ref):
        o_ref[...] = (g_ref[0] + x_ref[...]).astype(BF16)

    return pl.pallas_call(
        kern, name=name,
        grid_spec=pltpu.PrefetchScalarGridSpec(
            num_scalar_prefetch=1, grid=(4, r // tr),
            in_specs=[pl.BlockSpec((1, 1, tr, n), lambda k, i, c_ref: (k, c_ref[0], i, 0)),
                      pl.BlockSpec((1, tr, n), lambda k, i, c_ref: (k, i, 0))],
            out_specs=pl.BlockSpec((1, tr, n), lambda k, i, c_ref: (k, i, 0))),
        out_shape=jax.ShapeDtypeStruct(xr.shape, BF16), compiler_params=_cp(("parallel", "parallel")),
    )(c.reshape(1).astype(jnp.int32), g, xr)


def _sum_blocks(y, name, tr=None):
    n, r, cols = y.shape
    tr = tr or _pick(r, ROW_TILES)
    tr = tr if r % tr == 0 else r

    def kern(y_ref, o_ref):
        acc = y_ref[0].astype(F32)
        for k in range(1, n):
            acc = acc + y_ref[k].astype(F32)
        o_ref[...] = acc

    return pl.pallas_call(
        kern, name=name, grid=(r // tr,), in_specs=[pl.BlockSpec((n, tr, cols), lambda i: (0, i, 0))],
        out_specs=pl.BlockSpec((tr, cols), lambda i: (i, 0)), out_shape=jax.ShapeDtypeStruct((r, cols), F32),
        compiler_params=_cp(("parallel",)),
    )(y)


def _adamw(w, g, m, v, name, tr=256):
    r, n = w.shape
    tr = tr if r % tr == 0 else r

    def kern(w_ref, g_ref, m_ref, v_ref, d_ref, nm_ref, nv_ref):
        gg = g_ref[...]
        nm = ADAM_B1 * m_ref[...] + (1.0 - ADAM_B1) * gg
        nv = ADAM_B2 * v_ref[...] + (1.0 - ADAM_B2) * (gg * gg)
        m_hat = nm / (1.0 - ADAM_B1 ** ADAM_STEP)
        v_hat = nv / (1.0 - ADAM_B2 ** ADAM_STEP)
        d_ref[...] = -ADAM_LR * (m_hat / (jnp.sqrt(v_hat) + ADAM_EPS) + ADAM_WD * w_ref[...])
        nm_ref[...] = nm
        nv_ref[...] = nv

    blk = pl.BlockSpec((tr, n), lambda i: (i, 0))
    return pl.pallas_call(
        kern, name=name, grid=(r // tr,), in_specs=[blk] * 4, out_specs=[blk] * 3,
        out_shape=[jax.ShapeDtypeStruct((r, n), F32)] * 3, compiler_params=_cp(("parallel",)),
    )(w, g, m, v)


def _rows_of(n):
    return -(-n // (LANES * SUBLANES)) * SUBLANES


def _pack(arrs, lead, dtype, mult):
    ls = arrs[0].shape[:lead]
    blocks, total = [], 0
    for a in arrs:
        f = a.astype(dtype).reshape(ls + (-1,))
        n = f.shape[-1]
        rows = _rows_of(n)
        if rows * LANES != n:
            f = jnp.concatenate([f, jnp.zeros(ls + (rows * LANES - n,), dtype)], axis=-1)
        blocks.append(f.reshape(ls + (rows, LANES)))
        total += rows
    if total % mult:
        blocks.append(jnp.zeros(ls + (mult - total % mult, LANES), dtype))
    return jnp.concatenate(blocks, axis=lead)


def _unpack(buf, lead, shapes):
    ls = buf.shape[:lead]
    out, off = [], 0
    for sh in shapes:
        n = int(np.prod(sh))
        rows = _rows_of(n)
        flat = lax.slice_in_dim(buf, off, off + rows, axis=lead).reshape(ls + (rows * LANES,))
        if rows * LANES != n:
            flat = lax.slice_in_dim(flat, 0, n, axis=lead)
        out.append(flat.reshape(ls + tuple(sh)))
        off += rows
    return out


BIG = ("ab_w_in", "ab_w_out", "cd_w_in", "cd_w_out", "ffn_w_up", "ffn_w_down")
BIG_COLS = ("ab_w_in", "cd_w_in", "ffn_w_up")
SMALL_SHARDED = ("ssd_conv_w", "rg_conv_w", "rg_conv_b", "rg_ba", "rg_bx", "rg_lambda", "ffn_conv_w", "ln_g", "ln_b")
WEIGHTS = ("ab_w_in", "ssd_conv_w", "ssd_conv_b", "ssd_dt_bias", "ssd_a_log", "ssd_d", "ssd_norm_w", "hg_lower", "hg_norm_w",
           "ab_w_out", "cd_w_in", "swa_sinks", "rg_conv_w", "rg_conv_b", "rg_wa", "rg_ba", "rg_wx", "rg_bx", "rg_lambda",
           "cd_w_out", "ffn_w_up", "ffn_conv_w", "ffn_conv_b", "ffn_w_down", "ln_g", "ln_b")
SMALL = tuple(n for n in WEIGHTS if n not in BIG)


def _full_from_shards(name, g):
    if name in BIG_COLS or name in SMALL_SHARDED:
        return jnp.concatenate([g[k] for k in range(4)], axis=-1)
    return jnp.concatenate([g[k] for k in range(4)], axis=1)


def _layer_shards(name, f):
    k, n = f.shape
    if name in BIG_COLS:
        return jnp.transpose(f.reshape(k, 4, n // 4), (1, 0, 2))
    return f.reshape(4, k // 4, n)


def kernel(x, ab_w_in, ssd_conv_w, ssd_conv_b, ssd_dt_bias, ssd_a_log, ssd_d, ssd_norm_w, hg_lower, hg_norm_w, ab_w_out, cd_w_in, swa_sinks, rg_conv_w, rg_conv_b, rg_wa, rg_ba, rg_wx, rg_bx, rg_lambda, cd_w_out, ffn_w_up, ffn_conv_w, ffn_conv_b, ffn_w_down, ln_g, ln_b, loss_target, m_ab_w_in, m_ssd_conv_w, m_ssd_conv_b, m_ssd_dt_bias, m_ssd_a_log, m_ssd_d, m_ssd_norm_w, m_hg_lower, m_hg_norm_w, m_ab_w_out, m_cd_w_in, m_swa_sinks, m_rg_conv_w, m_rg_conv_b, m_rg_wa, m_rg_ba, m_rg_wx, m_rg_bx, m_rg_lambda, m_cd_w_out, m_ffn_w_up, m_ffn_conv_w, m_ffn_conv_b, m_ffn_w_down, m_ln_g, m_ln_b, v_ab_w_in, v_ssd_conv_w, v_ssd_conv_b, v_ssd_dt_bias, v_ssd_a_log, v_ssd_d, v_ssd_norm_w, v_hg_lower, v_hg_norm_w, v_ab_w_out, v_cd_w_in, v_swa_sinks, v_rg_conv_w, v_rg_conv_b, v_rg_wa, v_rg_ba, v_rg_wx, v_rg_bx, v_rg_lambda, v_cd_w_out, v_ffn_w_up, v_ffn_conv_w, v_ffn_conv_b, v_ffn_w_down, v_ln_g, v_ln_b):
    args = locals()
    w = {n: args[n] for n in WEIGHTS}
    mom = {n: args["m_" + n] for n in WEIGHTS}
    var = {n: args["v_" + n] for n in WEIGHTS}
    cx, cy, cc = lax.axis_index("x"), lax.axis_index("y"), lax.axis_index("c")
    chip = 2 * cx + cy

    halves = [w[n].astype(BF16).reshape((2, w[n].shape[0] // 2) + w[n].shape[1:]) for n in BIG]
    gathered = _gather_chips(halves, "gather_weights")
    full = {n: g.reshape((4,) + w[n].shape) for n, g in zip(BIG, gathered)}
    sp = _pack([w[n] for n in SMALL_SHARDED], 0, F32, SUBLANES)
    sg = _gather_devices(sp, "gather_small").reshape(4, 2, sp.shape[0], LANES)[:, 0]
    for n, g in zip(SMALL_SHARDED, _unpack(sg, 1, [w[n].shape for n in SMALL_SHARDED])):
        full[n] = _full_from_shards(n, g)
    for n in SMALL:
        full.setdefault(n, w[n])

    loss, grad_x, grads = _local_step(x[0], loss_target[0], full)
    loss = lax.psum(loss, ("x", "y", "c"))

    g4 = []
    for n in BIG:
        st = jnp.stack([_layer_shards(n, f) for f in grads[n]], axis=1)
        g4.append(st.reshape(4, 2, st.shape[1] // 2 * st.shape[2], st.shape[3]))
    hbs = [_add_half(g, xr, cc, "add_halves_" + n) for n, g, xr in zip(BIG, g4, _swap_halves(g4, "swap_halves"))]
    own = [_sum_blocks(y, "sum_chips_" + n) for n, y in zip(BIG, _scatter_chips(hbs, "scatter_chips"))]
    gout = {n: o.reshape(w[n].shape) for n, o in zip(BIG, _share_half(own, "share_half"))}

    small_shapes = [grads[n].shape for n in SMALL]
    gs = _pack([grads[n] for n in SMALL], 0, F32, SUBLANES)
    gsum = _sum_blocks(_gather_devices(gs, "gather_small_grads").reshape(8, gs.shape[0], LANES), "sum_small")
    for n, g in zip(SMALL, _unpack(gsum, 0, small_shapes)):
        if n in SMALL_SHARDED:
            width = w[n].shape[-1]
            g = lax.dynamic_slice_in_dim(g, chip * width, width, axis=g.ndim - 1)
        gout[n] = g

    delta, new_m, new_v = {}, {}, {}
    for n in BIG:
        sh = w[n].shape
        two = lambda a: a.reshape(-1, sh[-1])
        d, nm, nv = _adamw(two(w[n]), two(gout[n]), two(mom[n]), two(var[n]), "adamw_" + n)
        delta[n], new_m[n], new_v[n] = d.reshape(sh), nm.reshape(sh), nv.reshape(sh)
    local_shapes = [w[n].shape for n in SMALL]
    packs = [_pack([d[n] for n in SMALL], 0, F32, SUBLANES) for d in (w, gout, mom, var)]
    for dst, buf in zip((delta, new_m, new_v), _adamw(*packs, "adamw_small")):
        dst.update(zip(SMALL, _unpack(buf, 0, local_shapes)))

    return (loss, grad_x[None], *[gout[n] for n in WEIGHTS], *[delta[n] for n in WEIGHTS],
            *[new_m[n] for n in WEIGHTS], *[new_v[n] for n in WEIGHTS])
```

```python
import functools

import numpy as np
import jax
import jax.numpy as jnp
from jax import lax
from jax.experimental import pallas as pl
from jax.experimental.pallas import tpu as pltpu

F32 = jnp.float32
BF16 = jnp.bfloat16

D_MODEL = 1024
DEPTH = 4
SSD_HEADS = 8
SSD_D_INNER = 512
SSD_CONV_DIM = 768
SSD_CHUNK = 128
HG_CHUNK = 64
HG_HEADS = 4
SWA_BLOCK = 128
RG_WIDTH = 512
FFN_DIM = 2816
LN_EPS = 1e-5
RMS_EPS = 1e-6
MASK_VALUE = -1e9
ALPHA = (2 * DEPTH) ** 0.25
RG_C = 8.0
ADAM_LR, ADAM_B1, ADAM_B2, ADAM_EPS, ADAM_WD, ADAM_STEP = 0.001, 0.9, 0.999, 1e-08, 0.01, 10

LANES = 128
SUBLANES = 8
HALO = 16
CONV_CB = 256
VMEM_LIMIT = 56 * 1024 * 1024

AB_Z, AB_HQ, AB_HF, AB_HI, AB_HG, AB_XBC, AB_DT, AB_COLS = 0, 512, 1024, 1536, 2048, 2560, 3328, 3456
CD_Q, CD_GATE, CD_XR, CD_K, CD_V, CD_COLS = 0, 512, 1024, 1536, 1664, 1792


def _cp(sem=None):
    kw = dict(vmem_limit_bytes=VMEM_LIMIT)
    if sem is not None:
        kw["dimension_semantics"] = sem
    return pltpu.CompilerParams(**kw)


def _dot(a, b, dims=(((1,), (0,)), ((), ())), precision=None):
    return lax.dot_general(a, b, dims, precision=precision, preferred_element_type=F32)


NN = (((1,), (0,)), ((), ()))
NT = (((1,), (1,)), ((), ()))
TN = (((0,), (0,)), ((), ()))


@functools.partial(jax.custom_vjp, nondiff_argnums=(2,))
def _bdot(a, b, dims=NN):
    return _dot(a.astype(BF16), b.astype(BF16), dims)


def _bdot_fwd(a, b, dims):
    return _bdot(a, b, dims), (a, b)


def _bdot_bwd(dims, res, ct):
    a, b = res
    ab, bb, cb = a.astype(BF16), b.astype(BF16), ct.astype(BF16)
    if dims == NN:
        da, db = _dot(cb, bb, NT), _dot(ab, cb, TN)
    elif dims == NT:
        da, db = _dot(cb, bb, NN), _dot(cb, ab, TN)
    else:
        da, db = _dot(bb, cb, NT), _dot(ab, cb, NN)
    return da.astype(a.dtype), db.astype(b.dtype)


_bdot.defvjp(_bdot_fwd, _bdot_bwd)


WIDE_TILES = (1408, 1152, 1024, 896, 768, 512, 384, 256, 128)
MM_BLOCK_BYTES = 6 * 1024 * 1024


def _pick(n, cands):
    for c in cands:
        if n % c == 0:
            return c
    return n


def _mm(pairs, n_out, out_dtype, name, trans_b=False, tm=512):
    m = pairs[0][0].shape[0]
    a_row_bytes = sum(p[4] * p[0].dtype.itemsize for p in pairs)
    b_col_bytes = sum(p[4] * p[2].dtype.itemsize for p in pairs)
    tm = min(2 * tm if 2 * tm * a_row_bytes <= 2 * MM_BLOCK_BYTES else tm, m)
    tn = _pick(n_out, [c for c in WIDE_TILES if c * b_col_bytes <= MM_BLOCK_BYTES])
    n = len(pairs)

    def kern(*refs):
        acc = None
        for i in range(n):
            p = _bdot(refs[i][...], refs[n + i][...], NT if trans_b else NN)
            acc = p if acc is None else acc + p
        refs[2 * n][...] = acc.astype(out_dtype)

    in_specs = []
    for a, acb, b, bcb, k in pairs:
        in_specs.append(pl.BlockSpec((tm, k), functools.partial(lambda i, j, c: (i, c), c=acb)))
    for a, acb, b, bcb, k in pairs:
        if callable(bcb):
            in_specs.append(bcb(tn))
        elif trans_b:
            in_specs.append(pl.BlockSpec((tn, k), functools.partial(lambda i, j, c: (j, c), c=bcb)))
        else:
            in_specs.append(pl.BlockSpec((k, tn), functools.partial(lambda i, j, c: (c, j), c=bcb)))
    return pl.pallas_call(
        kern, name=name, grid=(m // tm, n_out // tn), in_specs=in_specs,
        out_specs=pl.BlockSpec((tm, tn), lambda i, j: (i, j)),
        out_shape=jax.ShapeDtypeStruct((m, n_out), out_dtype),
        compiler_params=_cp(("parallel", "arbitrary")),
    )(*[p[0] for p in pairs], *[p[2] for p in pairs])


def _shard_as_col_tile(g, layer):
    def spec(tn):
        assert tn == g.shape[3]
        return pl.BlockSpec((None, None, g.shape[2], tn), lambda i, j: (j, layer, 0, 0))
    return spec


def _shard_transposed(g, chip, layer):
    return lambda tn: pl.BlockSpec((None, None, tn, g.shape[3]), lambda i, j: (chip, layer, j, 0))


def _mm_tn(a, a_cb, ka, g, g_cb, ng, name, tm=2048):
    m = a.shape[0]
    tm = min(tm, m)
    tk = _pick(ka, (1024, 1408, 512, 256, 128))
    tn = _pick(ng, WIDE_TILES)
    nm = m // tm

    def kern(a_ref, g_ref, o_ref):
        p = _bdot(a_ref[...], g_ref[...], TN)

        @pl.when(pl.program_id(2) == 0)
        def _():
            o_ref[...] = p

        @pl.when(pl.program_id(2) > 0)
        def _():
            o_ref[...] += p

    ak, gk = ka // tk, ng // tn
    return pl.pallas_call(
        kern, name=name, grid=(ak, gk, nm),
        in_specs=[pl.BlockSpec((tm, tk), lambda i, j, r: (r, a_cb * ak + i)),
                  pl.BlockSpec((tm, tn), lambda i, j, r: (r, g_cb * gk + j))],
        out_specs=pl.BlockSpec((tk, tn), lambda i, j, r: (i, j)),
        out_shape=jax.ShapeDtypeStruct((ka, ng), F32),
        compiler_params=_cp(("parallel", "parallel", "arbitrary")),
    )(a, g)


def _mm_ln(pairs, x, g, b, name, tm=512):
    m, d = x.shape
    tm = min(tm, m)
    n = len(pairs)

    def kern(*refs):
        x_ref, g_ref, b_ref, y_ref, yb_ref, r_ref = refs[2 * n:]
        r = ALPHA * x_ref[...]
        for i in range(n):
            r = r + _bdot(refs[i][...], refs[n + i][...])
        mu = jnp.mean(r, -1, keepdims=True)
        xc = r - mu
        var = jnp.mean(xc * xc, -1, keepdims=True)
        y = xc * lax.rsqrt(var + LN_EPS) * g_ref[...] + b_ref[...]
        y_ref[...] = y
        yb_ref[...] = y.astype(BF16)
        r_ref[...] = r

    row = pl.BlockSpec((tm, d), lambda i: (i, 0))
    vec = pl.BlockSpec((1, d), lambda i: (0, 0))
    in_specs = [pl.BlockSpec((tm, k), functools.partial(lambda i, c: (i, c), c=acb)) for _, acb, _, _, k in pairs]
    in_specs += [pl.BlockSpec((k, d), functools.partial(lambda i, c: (c, 0), c=bcb)) for _, _, _, bcb, k in pairs]
    return pl.pallas_call(
        kern, name=name, grid=(m // tm,), in_specs=in_specs + [row, vec, vec], out_specs=[row, row, row],
        out_shape=[jax.ShapeDtypeStruct((m, d), F32), jax.ShapeDtypeStruct((m, d), BF16), jax.ShapeDtypeStruct((m, d), F32)],
        compiler_params=_cp(("parallel",)),
    )(*[p[0] for p in pairs], *[p[2] for p in pairs], x, g.reshape(1, d), b.reshape(1, d))


def _ln_bwd(dya, dyb, ca, cb, r, g, wt, out_dtype, name, tm=512):
    s, d = r.shape
    n = wt.shape[0]
    tm = min(tm, s)

    def kern(a_ref, b_ref, r_ref, g_ref, w_ref, dr_ref, drb_ref, dg_ref, db_ref, p_ref):
        dy = ca * a_ref[...] + cb * b_ref[...]
        rr = r_ref[...]
        mu = jnp.mean(rr, -1, keepdims=True)
        xc = rr - mu
        var = jnp.mean(xc * xc, -1, keepdims=True)
        rstd = lax.rsqrt(var + LN_EPS)
        xhat = xc * rstd
        dxh = dy * g_ref[...]
        dr = rstd * (dxh - jnp.mean(dxh, -1, keepdims=True) - xhat * jnp.mean(dxh * xhat, -1, keepdims=True))
        dr_ref[...] = dr
        drb = dr.astype(BF16)
        drb_ref[...] = drb
        p_ref[...] = _dot(drb, w_ref[...], NT).astype(out_dtype)
        dg = jnp.sum(dy * xhat, 0, keepdims=True)
        db = jnp.sum(dy, 0, keepdims=True)

        @pl.when(pl.program_id(0) == 0)
        def _():
            dg_ref[...] = dg
            db_ref[...] = db

        @pl.when(pl.program_id(0) > 0)
        def _():
            dg_ref[...] += dg
            db_ref[...] += db

    row = pl.BlockSpec((tm, d), lambda i: (i, 0))
    vec = pl.BlockSpec((1, d), lambda i: (0, 0))
    dr, drb, dg, db, prod = pl.pallas_call(
        kern, name=name, grid=(s // tm,),
        in_specs=[row, row, row, vec, pl.BlockSpec((n, d), lambda i: (0, 0))],
        out_specs=[row, row, vec, vec, pl.BlockSpec((tm, n), lambda i: (i, 0))],
        out_shape=[jax.ShapeDtypeStruct((s, d), F32), jax.ShapeDtypeStruct((s, d), BF16), jax.ShapeDtypeStruct((1, d), F32),
                   jax.ShapeDtypeStruct((1, d), F32), jax.ShapeDtypeStruct((s, n), out_dtype)],
        compiler_params=_cp(("arbitrary",)),
    )(dya, dyb, r, g.reshape(1, d), wt)
    return dr, drb, dg[0], db[0], prod


def _sqerr(y, t, name, tm=256):
    s, d = y.shape
    tm = min(tm, s)

    def kern(y_ref, t_ref, o_ref):
        e = y_ref[...] - t_ref[...]
        p = jnp.sum(e * e, 0, keepdims=True)

        @pl.when(pl.program_id(0) == 0)
        def _():
            o_ref[...] = p

        @pl.when(pl.program_id(0) > 0)
        def _():
            o_ref[...] += p

    row = pl.BlockSpec((tm, d), lambda i: (i, 0))
    return pl.pallas_call(
        kern, name=name, grid=(s // tm,), in_specs=[row, row], out_specs=pl.BlockSpec((1, d), lambda i: (0, 0)),
        out_shape=jax.ShapeDtypeStruct((1, d), F32), compiler_params=_cp(("arbitrary",)),
    )(y, t)


def _axpby(a, b, ca, cb, name, tm=256):
    s, d = a.shape
    tm = min(tm, s)

    def kern(a_ref, b_ref, o_ref):
        o_ref[...] = ca * a_ref[...] + cb * b_ref[...]

    row = pl.BlockSpec((tm, d), lambda i: (i, 0))
    return pl.pallas_call(
        kern, name=name, grid=(s // tm,), in_specs=[row, row], out_specs=row,
        out_shape=jax.ShapeDtypeStruct((s, d), F32), compiler_params=_cp(("parallel",)),
    )(a, b)


def _shift_down(xx, s, n):
    if s == 0:
        return xx[HALO:HALO + n]
    return pltpu.roll(xx, s, axis=0)[HALO:HALO + n]


def _shift_up(yy, s, n):
    if s == 0:
        return yy[0:n]
    return pltpu.roll(yy, yy.shape[0] - s, axis=0)[0:n]


def _prev_halo(tm):
    return lambda i: jnp.maximum(i * (tm // HALO) - 1, 0)


def _dwconv_fwd(x, x_cb0, c, w8, b, k, name, tm=1024):
    s = x.shape[0]
    tm = min(tm, s)
    ph = _prev_halo(tm)

    def kern(x_ref, h_ref, w_ref, b_ref, y_ref):
        halo = jnp.where(pl.program_id(0) == 0, 0.0, h_ref[...].astype(F32))
        xx = jnp.concatenate([halo, x_ref[...].astype(F32)], axis=0)
        w = w_ref[...]
        acc = b_ref[...] + w[k - 1:k] * xx[HALO:]
        for j in range(k - 1):
            acc = acc + w[j:j + 1] * _shift_down(xx, k - 1 - j, tm)
        y_ref[...] = acc

    return pl.pallas_call(
        kern, name=name, grid=(s // tm, c // CONV_CB),
        in_specs=[pl.BlockSpec((tm, CONV_CB), lambda i, j: (i, x_cb0 + j)),
                  pl.BlockSpec((HALO, CONV_CB), lambda i, j: (ph(i), x_cb0 + j)),
                  pl.BlockSpec((SUBLANES, CONV_CB), lambda i, j: (0, j)),
                  pl.BlockSpec((1, CONV_CB), lambda i, j: (0, j))],
        out_specs=pl.BlockSpec((tm, CONV_CB), lambda i, j: (i, j)),
        out_shape=jax.ShapeDtypeStruct((s, c), F32), compiler_params=_cp(("parallel", "parallel")),
    )(x, x, w8, b.reshape(1, c))


def _dwconv_bwd(dy, x, x_cb0, c, w8, k, name, tm=1024):
    s = x.shape[0]
    tm = min(tm, s)
    nt = s // tm
    ph = _prev_halo(tm)
    nh = lambda i: jnp.minimum((i + 1) * (tm // HALO), s // HALO - 1)

    def kern(dy_ref, dyn_ref, x_ref, h_ref, w_ref, dx_ref, dw_ref, db_ref):
        i = pl.program_id(1)
        halo = jnp.where(i == 0, 0.0, h_ref[...].astype(F32))
        xx = jnp.concatenate([halo, x_ref[...].astype(F32)], axis=0)
        dyt = dy_ref[...]
        nxt = jnp.where(i == nt - 1, 0.0, dyn_ref[...])
        dyy = jnp.concatenate([dyt, nxt], axis=0)
        w = w_ref[...]
        dx = w[k - 1:k] * dyt
        rows = [jnp.sum(dyt * _shift_down(xx, k - 1 - j, tm), 0, keepdims=True) for j in range(k)]
        for j in range(k - 1):
            dx = dx + w[j:j + 1] * _shift_up(dyy, k - 1 - j, tm)
        dx_ref[...] = dx.astype(BF16)
        dw = jnp.concatenate(rows + [jnp.zeros((SUBLANES - k, CONV_CB), F32)], axis=0)
        db = jnp.sum(dyt, 0, keepdims=True)

        @pl.when(i == 0)
        def _():
            dw_ref[...] = dw
            db_ref[...] = db

        @pl.when(i > 0)
        def _():
            dw_ref[...] += dw
            db_ref[...] += db

    dx, dw, db = pl.pallas_call(
        kern, name=name, grid=(c // CONV_CB, nt),
        in_specs=[pl.BlockSpec((tm, CONV_CB), lambda j, i: (i, j)),
                  pl.BlockSpec((HALO, CONV_CB), lambda j, i: (nh(i), j)),
                  pl.BlockSpec((tm, CONV_CB), lambda j, i: (i, x_cb0 + j)),
                  pl.BlockSpec((HALO, CONV_CB), lambda j, i: (ph(i), x_cb0 + j)),
                  pl.BlockSpec((SUBLANES, CONV_CB), lambda j, i: (0, j))],
        out_specs=[pl.BlockSpec((tm, CONV_CB), lambda j, i: (i, j)),
                   pl.BlockSpec((SUBLANES, CONV_CB), lambda j, i: (0, j)),
                   pl.BlockSpec((1, CONV_CB), lambda j, i: (0, j))],
        out_shape=[jax.ShapeDtypeStruct((s, c), BF16), jax.ShapeDtypeStruct((SUBLANES, c), F32),
                   jax.ShapeDtypeStruct((1, c), F32)],
        compiler_params=_cp(("parallel", "arbitrary")),
    )(dy, dy, x, x, w8)
    return dx, dw, db[0]


def _silu(x):
    return x * jax.nn.sigmoid(x)


def _ffn_gate_fwd(hu, w8, b, name, tm=2048):
    s = hu.shape[0]
    tm = min(tm, s)
    nb = FFN_DIM // CONV_CB
    ph = _prev_halo(tm)
    k = 3

    def conv(x_ref, h_ref, w_ref, b_ref):
        halo = jnp.where(pl.program_id(0) == 0, 0.0, h_ref[...].astype(F32))
        xx = jnp.concatenate([halo, x_ref[...].astype(F32)], axis=0)
        w = w_ref[...]
        acc = b_ref[...] + w[k - 1:k] * xx[HALO:]
        for j in range(k - 1):
            acc = acc + w[j:j + 1] * _shift_down(xx, k - 1 - j, tm)
        return acc

    def kern(g_ref, gh_ref, u_ref, uh_ref, wg_ref, wu_ref, bg_ref, bu_ref, a_ref, hg_ref, hu_ref):
        g = conv(g_ref, gh_ref, wg_ref, bg_ref)
        u = conv(u_ref, uh_ref, wu_ref, bu_ref)
        a_ref[...] = (_silu(g) * u).astype(BF16)
        hg_ref[...] = g.astype(BF16)
        hu_ref[...] = u.astype(BF16)

    main = lambda off: pl.BlockSpec((tm, CONV_CB), lambda i, j: (i, off + j))
    halo = lambda off: pl.BlockSpec((HALO, CONV_CB), lambda i, j: (ph(i), off + j))
    wsp = lambda off: pl.BlockSpec((SUBLANES, CONV_CB), lambda i, j: (0, off + j))
    bsp = lambda off: pl.BlockSpec((1, CONV_CB), lambda i, j: (0, off + j))
    b2 = b.reshape(1, 2 * FFN_DIM)
    half = jax.ShapeDtypeStruct((s, FFN_DIM), BF16)
    return pl.pallas_call(
        kern, name=name, grid=(s // tm, nb),
        in_specs=[main(0), halo(0), main(nb), halo(nb), wsp(0), wsp(nb), bsp(0), bsp(nb)],
        out_specs=[main(0), main(0), main(0)], out_shape=[half, half, half],
        compiler_params=_cp(("parallel", "parallel")),
    )(hu, hu, hu, hu, w8, w8, b2, b2)


def _ffn_gate_bwd(hu, hg, hv, da, w8, name, tm=2048):
    s = hu.shape[0]
    tm = min(tm, s)
    nt = s // tm
    nb = FFN_DIM // CONV_CB
    nh = lambda i: jnp.minimum((i + 1) * (tm // HALO), s // HALO - 1)
    k = 3

    def kern(xg_ref, xu_ref, g_ref, gn_ref, u_ref, un_ref, da_ref, dan_ref, wg_ref, wu_ref,
             dg_ref, du_ref, dwg_ref, dwu_ref, dbg_ref, dbu_ref):
        i = pl.program_id(1)

        def ext(x_ref, n_ref):
            return jnp.concatenate([x_ref[...].astype(F32), n_ref[...].astype(F32)], axis=0)

        g, u = ext(g_ref, gn_ref), ext(u_ref, un_ref)
        dae = jnp.concatenate([da_ref[...].astype(F32), jnp.where(i == nt - 1, 0.0, dan_ref[...].astype(F32))], axis=0)
        sg = jax.nn.sigmoid(g)
        dhg = dae * u * (sg * (1.0 + g * (1.0 - sg)))
        dhu = dae * (g * sg)

        def back(dh, x, w):
            taps = [dh[0:tm] if j == k - 1 else _shift_up(dh, k - 1 - j, tm) for j in range(k)]
            dx = w[0:1] * taps[0]
            for j in range(1, k):
                dx = dx + w[j:j + 1] * taps[j]
            rows = [jnp.sum(taps[j] * x, 0, keepdims=True) for j in range(k)]
            dw = jnp.concatenate(rows + [jnp.zeros((SUBLANES - k, CONV_CB), F32)], axis=0)
            return dx, dw, jnp.sum(dh[0:tm], 0, keepdims=True)

        dxg, dwg, dbg = back(dhg, xg_ref[...].astype(F32), wg_ref[...])
        dxu, dwu, dbu = back(dhu, xu_ref[...].astype(F32), wu_ref[...])
        dg_ref[...] = dxg.astype(BF16)
        du_ref[...] = dxu.astype(BF16)

        @pl.when(i == 0)
        def _():
            dwg_ref[...] = dwg
            dwu_ref[...] = dwu
            dbg_ref[...] = dbg
            dbu_ref[...] = dbu

        @pl.when(i > 0)
        def _():
            dwg_ref[...] += dwg
            dwu_ref[...] += dwu
            dbg_ref[...] += dbg
            dbu_ref[...] += dbu

    main = lambda off: pl.BlockSpec((tm, CONV_CB), lambda j, i: (i, off + j))
    nxt = lambda off: pl.BlockSpec((HALO, CONV_CB), lambda j, i: (nh(i), off + j))
    wsp = lambda off: pl.BlockSpec((SUBLANES, CONV_CB), lambda j, i: (0, off + j))
    bsp = lambda off: pl.BlockSpec((1, CONV_CB), lambda j, i: (0, off + j))
    outs = pl.pallas_call(
        kern, name=name, grid=(nb, nt),
        in_specs=[main(0), main(nb), main(0), nxt(0), main(0), nxt(0), main(0), nxt(0), wsp(0), wsp(nb)],
        out_specs=[main(0), main(0), wsp(0), wsp(0), bsp(0), bsp(0)],
        out_shape=[jax.ShapeDtypeStruct((s, FFN_DIM), BF16), jax.ShapeDtypeStruct((s, FFN_DIM), BF16),
                   jax.ShapeDtypeStruct((SUBLANES, FFN_DIM), F32), jax.ShapeDtypeStruct((SUBLANES, FFN_DIM), F32),
                   jax.ShapeDtypeStruct((1, FFN_DIM), F32), jax.ShapeDtypeStruct((1, FFN_DIM), F32)],
        compiler_params=_cp(("parallel", "arbitrary")),
    )(hu, hu, hg, hg, hv, hv, da, da, w8, w8)
    dg, du, dwg, dwu, dbg, dbu = outs
    return dg, du, jnp.concatenate([dwg, dwu], axis=1), jnp.concatenate([dbg[0], dbu[0]])


def _iota(shape, dim):
    return lax.broadcasted_iota(jnp.int32, shape, dim)


def _in_range(idx, start, size):
    return jnp.logical_and(idx >= start, idx < start + size)


@functools.partial(jax.custom_vjp, nondiff_argnums=(2, 3))
def _zdot(a, b, dims, z):
    zz, x = ((a, b) if z == 0 else (b, a))
    zz = zz.astype(BF16)
    xh = x.astype(BF16)
    r1 = x - xh.astype(F32)
    xm = r1.astype(BF16)
    xl = (r1 - xm.astype(F32)).astype(BF16)
    f = (lambda p: _dot(zz, p, dims)) if z == 0 else (lambda p: _dot(p, zz, dims))
    return f(xh) + f(xm) + f(xl)


def _zdot_fwd(a, b, dims, z):
    return _zdot(a, b, dims, z), (a if z == 0 else b)


def _zdot_bwd(dims, z, zz, ct):
    zb = zz.astype(BF16)
    ch = ct.astype(BF16)
    cm = (ct - ch.astype(F32)).astype(BF16)
    if dims == NN and z == 0:
        f = lambda c: _dot(zb, c, TN)
    elif dims == NN and z == 1:
        f = lambda c: _dot(c, zb, NT)
    else:
        f = lambda c: _dot(c, zb, TN)
    dx = f(ch) + f(cm)
    return (jnp.zeros_like(zz), dx) if z == 0 else (dx, jnp.zeros_like(zz))


_zdot.defvjp(_zdot_fwd, _zdot_bwd)


def _expand8(v, e):
    return jnp.sum(_zdot(jnp.broadcast_to(v, (SUBLANES, v.shape[1])), e, NN, 1), 0, keepdims=True) * (1.0 / SUBLANES)


def _ssd_consts():
    tri = np.tril(np.ones((SSD_CHUNK, SSD_CHUNK), np.float32))
    e = np.zeros((LANES, SSD_D_INNER), np.float32)
    for h in range(SSD_HEADS):
        e[h, 64 * h:64 * h + 64] = 1.0
    bd = np.zeros((LANES, SSD_D_INNER), np.float32)
    bd[:64, :256] = 1.0
    bd[64:, 256:] = 1.0
    return jnp.asarray(tri), jnp.asarray(e), jnp.asarray(bd)


def _ssd_chunk(tri, e, bd, z, xs_p, bm_p, cm_p, dtr, st, dtb, alog, dsk, nw):
    n = z.shape[0]
    lane128, lane512 = _iota((1, LANES), 1), _iota((1, SSD_D_INNER), 1)
    sub8 = _iota((SUBLANES, 1), 0)
    tril = _iota((n, 1), 0) >= _iota((1, n), 1)
    head_rows = jnp.where(sub8 == lane128, 1.0, 0.0)
    xs, bm, cm = _silu(xs_p), _silu(bm_p), _silu(cm_p)
    dt = jax.nn.softplus(dtr + dtb)
    da = dt * (-jnp.exp(alog))
    acs = _zdot(tri, da, NN, 0)
    acst = _zdot(head_rows, acs, NT, 0)
    xc = xs * _zdot(dt, e, NN, 1)
    tot = jnp.sum(da, 0, keepdims=True)
    y = _bdot(cm, st) * _zdot(jnp.exp(acs), e, NN, 1)
    st_new = st * _expand8(jnp.exp(tot), e) + bd * _bdot(bm, xc * _zdot(jnp.exp(tot - acs), e, NN, 1), TN)
    for g in range(2):
        cb = _bdot(jnp.where(_in_range(lane128, 64 * g, 64), cm, 0.0), bm, NT)
        for h in range(4 * g, 4 * g + 4):
            col = jnp.sum(jnp.where(lane128 == h, acs, 0.0), 1, keepdims=True)
            row = jnp.sum(jnp.where(sub8 == h, acst, 0.0), 0, keepdims=True)
            dm = jnp.where(tril, jnp.exp(jnp.minimum(col - row, 0.0)), 0.0)
            y = y + _bdot(cb * dm, jnp.where(_in_range(lane512, 64 * h, 64), xc, 0.0))
    y = (y + _expand8(dsk, e) * xs) * _silu(z)
    ysq = y * y
    g0 = lane512 < 256
    ms0 = jnp.sum(jnp.where(g0, ysq, 0.0), 1, keepdims=True) * (1.0 / 256)
    ms1 = jnp.sum(jnp.where(g0, 0.0, ysq), 1, keepdims=True) * (1.0 / 256)
    rs = jnp.where(g0, lax.rsqrt(ms0 + RMS_EPS), lax.rsqrt(ms1 + RMS_EPS))
    return y * rs * nw, st_new


SSD_STEP = 4


def _ssd_rows(u):
    return slice(SSD_CHUNK * u, SSD_CHUNK * (u + 1))


def _ssd_specs(rev, ng):
    ci = (lambda i: ng - 1 - i) if rev else (lambda i: i)
    n = SSD_STEP * SSD_CHUNK
    act = [pl.BlockSpec((n, 512), lambda i: (ci(i), AB_Z // 512)),
           pl.BlockSpec((n, 512), lambda i: (ci(i), 0)),
           pl.BlockSpec((n, LANES), lambda i: (ci(i), 4)),
           pl.BlockSpec((n, LANES), lambda i: (ci(i), 5)),
           pl.BlockSpec((n, LANES), lambda i: (ci(i), AB_DT // LANES))]
    const = [pl.BlockSpec((SSD_CHUNK, SSD_CHUNK), lambda i: (0, 0)), pl.BlockSpec((LANES, 512), lambda i: (0, 0)),
             pl.BlockSpec((LANES, 512), lambda i: (0, 0))]
    par = [pl.BlockSpec((1, LANES), lambda i: (0, 0))] * 3 + [pl.BlockSpec((1, 512), lambda i: (0, 0))]
    return ci, act, const, par


def _ssd_fwd(h, xbcc, dtb, alog, dsk, nw, name):
    s = h.shape[0]
    nc = s // SSD_CHUNK
    ng = nc // SSD_STEP
    ci, act, const, par = _ssd_specs(False, ng)

    def kern(z_ref, xs_ref, bm_ref, cm_ref, dt_ref, tri_ref, e_ref, bd_ref, dtb_ref, al_ref, dsk_ref, nw_ref,
             y_ref, sts_ref, st):
        @pl.when(pl.program_id(0) == 0)
        def _():
            st[...] = jnp.zeros_like(st)

        for u in range(SSD_STEP):
            rows = _ssd_rows(u)
            sts_ref[u] = st[...]
            y, stn = _ssd_chunk(tri_ref[...], e_ref[...], bd_ref[...], z_ref[rows, :], xs_ref[rows, :], bm_ref[rows, :],
                                cm_ref[rows, :], dt_ref[rows, :], st[...], dtb_ref[...], al_ref[...], dsk_ref[...], nw_ref[...])
            y_ref[rows, :] = y.astype(BF16)
            st[...] = stn

    return pl.pallas_call(
        kern, name=name, grid=(ng,), in_specs=act + const + par,
        out_specs=[pl.BlockSpec((SSD_STEP * SSD_CHUNK, 512), lambda i: (i, 0)),
                   pl.BlockSpec((SSD_STEP, LANES, 512), lambda i: (i, 0, 0))],
        out_shape=[jax.ShapeDtypeStruct((s, 512), BF16), jax.ShapeDtypeStruct((nc, LANES, 512), F32)],
        scratch_shapes=[pltpu.VMEM((LANES, 512), F32)], compiler_params=_cp(("arbitrary",)),
    )(h, xbcc, xbcc, xbcc, h, *_ssd_consts(), dtb, alog, dsk, nw)


def _acc(ref, val, first):
    @pl.when(first)
    def _():
        ref[...] = val

    @pl.when(jnp.logical_not(first))
    def _():
        ref[...] += val


def _ssd_bwd(dy, dy_cb, h, xbcc, sts, dtb, alog, dsk, nw, name):
    s = h.shape[0]
    ng = s // SSD_CHUNK // SSD_STEP
    n = SSD_STEP * SSD_CHUNK
    ci, act, const, par = _ssd_specs(True, ng)

    def kern(dy_ref, sts_ref, z_ref, xs_ref, bm_ref, cm_ref, dt_ref, tri_ref, e_ref, bd_ref, dtb_ref, al_ref, dsk_ref,
             nw_ref, dz_ref, dx_ref, ddt_ref, ddtb_ref, dal_ref, ddsk_ref, dnw_ref, dst):
        @pl.when(pl.program_id(0) == 0)
        def _():
            dst[...] = jnp.zeros_like(dst)
            for ref in (ddtb_ref, dal_ref, ddsk_ref, dnw_ref):
                ref[...] = jnp.zeros_like(ref)

        f = functools.partial(_ssd_chunk, tri_ref[...], e_ref[...], bd_ref[...])
        for u in range(SSD_STEP - 1, -1, -1):
            rows = _ssd_rows(u)
            _, pull = jax.vjp(f, z_ref[rows, :], xs_ref[rows, :], bm_ref[rows, :], cm_ref[rows, :], dt_ref[rows, :],
                              sts_ref[u], dtb_ref[...], al_ref[...], dsk_ref[...], nw_ref[...])
            dz, dxs, dbm, dcm, ddt, dsti, ddtb, dal, ddsk, dnw = pull((dy_ref[rows, :], dst[...]))
            dz_ref[rows, :] = dz.astype(BF16)
            dx_ref[rows, 0:512] = dxs
            dx_ref[rows, 512:640] = dbm
            dx_ref[rows, 640:768] = dcm
            ddt_ref[rows, :] = ddt.astype(BF16)
            dst[...] = dsti
            ddtb_ref[...] += ddtb
            dal_ref[...] += dal
            ddsk_ref[...] += ddsk
            dnw_ref[...] += dnw

    vec = pl.BlockSpec((1, LANES), lambda i: (0, 0))
    return pl.pallas_call(
        kern, name=name, grid=(ng,),
        in_specs=[pl.BlockSpec((n, 512), lambda i: (ci(i), dy_cb)), pl.BlockSpec((SSD_STEP, LANES, 512), lambda i: (ci(i), 0, 0))]
        + act + const + par,
        out_specs=[pl.BlockSpec((n, 512), lambda i: (ci(i), 0)), pl.BlockSpec((n, SSD_CONV_DIM), lambda i: (ci(i), 0)),
                   pl.BlockSpec((n, LANES), lambda i: (ci(i), 0)), vec, vec, vec, pl.BlockSpec((1, 512), lambda i: (0, 0))],
        out_shape=[jax.ShapeDtypeStruct((s, 512), BF16), jax.ShapeDtypeStruct((s, SSD_CONV_DIM), F32),
                   jax.ShapeDtypeStruct((s, LANES), BF16), jax.ShapeDtypeStruct((1, LANES), F32),
                   jax.ShapeDtypeStruct((1, LANES), F32), jax.ShapeDtypeStruct((1, LANES), F32),
                   jax.ShapeDtypeStruct((1, 512), F32)],
        scratch_shapes=[pltpu.VMEM((LANES, 512), F32)], compiler_params=_cp(("arbitrary",)),
    )(dy, sts, h, xbcc, xbcc, xbcc, h, *_ssd_consts(), dtb, alog, dsk, nw)


HG_LEVELS = 6


HG_SUMS = 1 + 2 * HG_LEVELS


def _hg_consts():
    n = HG_CHUNK
    t = np.arange(n)
    mats = [np.tril(np.ones((n, n), np.float32))]
    dq, dk, mk = [], [], []
    for lv in range(HG_LEVELS):
        b = 1 << lv
        start = (t // b) * b
        end = start + b - 1
        dq.append(((t[None, :] >= start[:, None]) & (t[None, :] <= t[:, None])).astype(np.float32))
        dk.append(((t[None, :] > t[:, None]) & (t[None, :] <= end[:, None])).astype(np.float32))
        mk.append((((t // b) % 2 == 1)[:, None] & ((t // b)[None, :] == (t // b)[:, None] - 1)).astype(np.float32))
    c1 = np.concatenate(mats + dq + dk, axis=0)
    return (jnp.asarray(c1, BF16), jnp.asarray(np.concatenate([c1, c1, c1], axis=1), BF16), jnp.asarray(np.stack(mk)))


@functools.partial(jax.custom_vjp, nondiff_argnums=(1, 2))
def _split_grid(x, nr, nc):
    br, bc = x.shape[0] // nr, x.shape[1] // nc
    return tuple(x[i * br:(i + 1) * br, j * bc:(j + 1) * bc] for i in range(nr) for j in range(nc))


def _split_grid_fwd(x, nr, nc):
    return _split_grid(x, nr, nc), None


def _split_grid_bwd(nr, nc, _, cts):
    rows = [cts[i * nc] if nc == 1 else jnp.concatenate(cts[i * nc:(i + 1) * nc], axis=1) for i in range(nr)]
    return (rows[0] if nr == 1 else jnp.concatenate(rows, axis=0),)


_split_grid.defvjp(_split_grid_fwd, _split_grid_bwd)


@jax.custom_vjp
def _level_sums(c1, c3, g):
    gh = g.astype(BF16)
    r1 = g - gh.astype(F32)
    gm = r1.astype(BF16)
    gl = (r1 - gm.astype(F32)).astype(BF16)
    return _dot(c3, jnp.concatenate([gh, gm, gl], axis=0), NN)


def _level_sums_fwd(c1, c3, g):
    return _level_sums(c1, c3, g), (c1, c3)


def _level_sums_bwd(res, ct):
    c1, c3 = res
    ch = ct.astype(BF16)
    cm = (ct - ch.astype(F32)).astype(BF16)
    return jnp.zeros_like(c1), jnp.zeros_like(c3), _dot(c1, ch, TN) + _dot(c1, cm, TN)


_level_sums.defvjp(_level_sums_fwd, _level_sums_bwd)


def _hg_chunk(c1, c3, masks, hq, hf, hi, hg, sts, lb, nw):
    q = _silu(hq)
    g = jnp.log(lb + (1.0 - lb) * jax.nn.sigmoid(hf))
    k = (1.0 - lb) * jax.nn.sigmoid(-hf)
    parts = _split_grid(_level_sums(c1, c3, g), HG_SUMS, 1)
    bc = parts[0]
    tot = jnp.sum(g, 0, keepdims=True)
    qe = _split_grid(q * jnp.exp(bc), 1, HG_HEADS)
    kd = _split_grid(k * jnp.exp(tot - bc), 1, HG_HEADS)
    etot = _split_grid(jnp.exp(tot), 1, HG_HEADS)
    qh, kh, vh, gh = [_split_grid(a, 1, HG_HEADS) for a in (q, k, hi, hg)]
    att = [None] * HG_HEADS
    for lv in range(HG_LEVELS):
        qt = _split_grid(q * jnp.exp(parts[1 + lv]), 1, HG_HEADS)
        kt = _split_grid(k * jnp.exp(parts[1 + HG_LEVELS + lv]), 1, HG_HEADS)
        for hd in range(HG_HEADS):
            t = masks[lv] * _bdot(qt[hd], kt[hd], NT)
            att[hd] = t if att[hd] is None else att[hd] + t
    outs, stn = [], []
    for hd in range(HG_HEADS):
        o = (_bdot(att[hd], vh[hd]) + jnp.sum(qh[hd] * kh[hd], 1, keepdims=True) * vh[hd] + _bdot(qe[hd], sts[hd], NT))
        stn.append(sts[hd] * etot[hd] + _bdot(vh[hd], kd[hd], TN))
        outs.append(o * lax.rsqrt(jnp.mean(o * o, 1, keepdims=True) + RMS_EPS) * nw * _silu(gh[hd]))
    return tuple(outs), tuple(stn)


HG_STEP = 4


def _hg_specs(rev, ng):
    ci = (lambda i: ng - 1 - i) if rev else (lambda i: i)
    n = HG_CHUNK
    act = [pl.BlockSpec((HG_STEP * n, 512), functools.partial(lambda i, c: (ci(i), c), c=c))
           for c in (AB_HQ // 512, AB_HF // 512, AB_HI // 512, AB_HG // 512)]
    other = [pl.BlockSpec((HG_SUMS * n, n), lambda i: (0, 0)), pl.BlockSpec((HG_SUMS * n, 3 * n), lambda i: (0, 0)),
             pl.BlockSpec((HG_LEVELS, n, n), lambda i: (0, 0, 0)), pl.BlockSpec((1, 512), lambda i: (0, 0)),
             pl.BlockSpec((1, LANES), lambda i: (0, 0))]
    return ci, act, other


def _head_rows(hd):
    return slice(LANES * hd, LANES * (hd + 1))


def _chunk_rows(u):
    return slice(HG_CHUNK * u, HG_CHUNK * (u + 1))


def _hg_fwd(h, lb, nw, name):
    s = h.shape[0]
    nc = s // HG_CHUNK
    ng = nc // HG_STEP
    ci, act, other = _hg_specs(False, ng)

    def kern(q_ref, f_ref, i_ref, g_ref, c1_ref, c3_ref, mk_ref, lb_ref, nw_ref, y_ref, sts_ref, st):
        @pl.when(pl.program_id(0) == 0)
        def _():
            st[...] = jnp.zeros_like(st)

        masks = [mk_ref[lv] for lv in range(HG_LEVELS)]
        for u in range(HG_STEP):
            rows = _chunk_rows(u)
            sts_ref[u] = st[...]
            outs, stn = _hg_chunk(c1_ref[...], c3_ref[...], masks, q_ref[rows, :], f_ref[rows, :], i_ref[rows, :],
                                  g_ref[rows, :], tuple(st[_head_rows(hd), :] for hd in range(HG_HEADS)), lb_ref[...],
                                  nw_ref[...])
            for hd in range(HG_HEADS):
                y_ref[rows, _head_rows(hd)] = outs[hd].astype(BF16)
                st[_head_rows(hd), :] = stn[hd]

    return pl.pallas_call(
        kern, name=name, grid=(ng,), in_specs=act + other,
        out_specs=[pl.BlockSpec((HG_STEP * HG_CHUNK, 512), lambda i: (i, 0)),
                   pl.BlockSpec((HG_STEP, 512, LANES), lambda i: (i, 0, 0))],
        out_shape=[jax.ShapeDtypeStruct((s, 512), BF16), jax.ShapeDtypeStruct((nc, 512, LANES), F32)],
        scratch_shapes=[pltpu.VMEM((512, LANES), F32)], compiler_params=_cp(("arbitrary",)),
    )(h, h, h, h, *_hg_consts(), lb, nw)


def _hg_bwd(dy, dy_cb, h, sts, lb, nw, name):
    s = h.shape[0]
    ng = s // HG_CHUNK // HG_STEP
    n = HG_STEP * HG_CHUNK
    ci, act, other = _hg_specs(True, ng)

    def kern(dy_ref, sts_ref, q_ref, f_ref, i_ref, g_ref, c1_ref, c3_ref, mk_ref, lb_ref, nw_ref,
             dq_ref, df_ref, di_ref, dg_ref, dlb_ref, dnw_ref, dst):
        @pl.when(pl.program_id(0) == 0)
        def _():
            dst[...] = jnp.zeros_like(dst)
            dlb_ref[...] = jnp.zeros_like(dlb_ref)
            dnw_ref[...] = jnp.zeros_like(dnw_ref)

        masks = [mk_ref[lv] for lv in range(HG_LEVELS)]
        heads = range(HG_HEADS)
        for u in range(HG_STEP - 1, -1, -1):
            rows = _chunk_rows(u)
            _, pull = jax.vjp(functools.partial(_hg_chunk, c1_ref[...], c3_ref[...], masks), q_ref[rows, :], f_ref[rows, :],
                              i_ref[rows, :], g_ref[rows, :], tuple(sts_ref[u, _head_rows(hd), :] for hd in heads),
                              lb_ref[...], nw_ref[...])
            dq, df, di, dg, dsti, dlb, dnw = pull((tuple(dy_ref[rows, _head_rows(hd)] for hd in heads),
                                                   tuple(dst[_head_rows(hd), :] for hd in heads)))
            dq_ref[rows, :] = dq.astype(BF16)
            df_ref[rows, :] = df.astype(BF16)
            di_ref[rows, :] = di.astype(BF16)
            dg_ref[rows, :] = dg.astype(BF16)
            for hd in heads:
                dst[_head_rows(hd), :] = dsti[hd]
            dlb_ref[...] += dlb
            dnw_ref[...] += dnw

    blk = pl.BlockSpec((n, 512), lambda i: (ci(i), 0))
    return pl.pallas_call(
        kern, name=name, grid=(ng,),
        in_specs=[pl.BlockSpec((n, 512), lambda i: (ci(i), dy_cb)), pl.BlockSpec((HG_STEP, 512, LANES), lambda i: (ci(i), 0, 0))]
        + act + other,
        out_specs=[blk, blk, blk, blk, pl.BlockSpec((1, 512), lambda i: (0, 0)), pl.BlockSpec((1, LANES), lambda i: (0, 0))],
        out_shape=[jax.ShapeDtypeStruct((s, 512), BF16)] * 4 + [jax.ShapeDtypeStruct((1, 512), F32),
                                                                jax.ShapeDtypeStruct((1, LANES), F32)],
        scratch_shapes=[pltpu.VMEM((512, LANES), F32)], compiler_params=_cp(("arbitrary",)),
    )(dy, sts, h, h, h, h, *_hg_consts(), lb, nw)


def _swa_consts():
    n = SWA_BLOCK
    r = np.arange(8 * n)
    t, blk = r % n, r // n
    lanes = np.arange(LANES)
    own = (t[:, None] >= lanes[None, :]).astype(np.float32)
    prev = (lanes[None, :] > t[:, None]).astype(np.float32)
    sel = (lanes[None, :] == (blk // 2 + 4 * (blk % 2))[:, None]).astype(np.float32)
    grp = ((lanes[None, :] // 64) == (blk % 2)[:, None]).astype(np.float32)
    return tuple(jnp.asarray(a) for a in (own, prev, sel, grp))


def _swa_block(own, prev, sel, grp, q, k0, v0, k1, v1, srow):
    qs = _split_grid(q, 1, 4)
    qa = jnp.concatenate([qs[p] for p in range(4) for _ in range(2)], axis=0) * grp
    s0 = jnp.where(prev > 0.0, _bdot(qa, k0, NT) * 0.125, MASK_VALUE)
    s1 = jnp.where(own > 0.0, _bdot(qa, k1, NT) * 0.125, MASK_VALUE)
    sk = jnp.sum(sel * srow, 1, keepdims=True)
    m = lax.stop_gradient(jnp.maximum(jnp.maximum(jnp.max(s0, 1, keepdims=True), jnp.max(s1, 1, keepdims=True)), sk))
    p0, p1 = jnp.exp(s0 - m), jnp.exp(s1 - m)
    inv = 1.0 / (jnp.sum(p0, 1, keepdims=True) + jnp.sum(p1, 1, keepdims=True) + jnp.exp(sk - m))
    o = _split_grid((_bdot(p0 * inv, v0) + _bdot(p1 * inv, v1)) * grp, 8, 1)
    return tuple(o[2 * p] + o[2 * p + 1] for p in range(4))


SWA_STEP = 4


def _swa_specs():
    n = SWA_BLOCK
    pv = lambda i: jnp.maximum(SWA_STEP * i - 1, 0)
    mask = pl.BlockSpec((8 * n, LANES), lambda i: (0, 0))
    return [mask, mask, mask, mask, pl.BlockSpec((SWA_STEP * n, 512), lambda i: (i, CD_Q // 512)),
            pl.BlockSpec((n, LANES), lambda i: (pv(i), CD_K // LANES)), pl.BlockSpec((n, LANES), lambda i: (pv(i), CD_V // LANES)),
            pl.BlockSpec((SWA_STEP * n, LANES), lambda i: (i, CD_K // LANES)),
            pl.BlockSpec((SWA_STEP * n, LANES), lambda i: (i, CD_V // LANES)), pl.BlockSpec((1, LANES), lambda i: (0, 0))]


def _swa_args(refs, u, notfirst):
    own, prev, sel, grp = (r[...] for r in refs[:4])
    q_ref, k0_ref, v0_ref, k_ref, v_ref, s_ref = refs[4:]
    rows = slice(SWA_BLOCK * u, SWA_BLOCK * (u + 1))
    if u == 0:
        k0, v0, prev = k0_ref[...], v0_ref[...], prev * notfirst.astype(F32)
    else:
        before = slice(SWA_BLOCK * (u - 1), SWA_BLOCK * u)
        k0, v0 = k_ref[before, :], v_ref[before, :]
    return (own, prev, sel, grp), (q_ref[rows, :], k0, v0, k_ref[rows, :], v_ref[rows, :], s_ref[...])


def _swa_fwd(hc, srow, name):
    s = hc.shape[0]
    n = SWA_STEP * SWA_BLOCK

    def kern(*refs):
        y_ref = refs[10]
        for u in range(SWA_STEP):
            consts, acts = _swa_args(refs[:10], u, pl.program_id(0) > 0)
            outs = _swa_block(*consts, *acts)
            for p in range(4):
                y_ref[SWA_BLOCK * u:SWA_BLOCK * (u + 1), LANES * p:LANES * (p + 1)] = outs[p].astype(BF16)

    return pl.pallas_call(
        kern, name=name, grid=(s // n,), in_specs=_swa_specs(), out_specs=pl.BlockSpec((n, 512), lambda i: (i, 0)),
        out_shape=jax.ShapeDtypeStruct((s, 512), BF16), compiler_params=_cp(("parallel",)),
    )(*_swa_consts(), hc, hc, hc, hc, hc, srow)


def _swa_bwd(dy, dy_cb, hc, srow, name):
    s = hc.shape[0]
    n = SWA_STEP * SWA_BLOCK
    nb = SWA_BLOCK

    def kern(dy_ref, *refs):
        dq_ref, dk_ref, dv_ref, ds_ref = refs[10:]
        i = pl.program_id(0)

        @pl.when(i == 0)
        def _():
            dk_ref[...] = jnp.zeros_like(dk_ref)
            dv_ref[...] = jnp.zeros_like(dv_ref)
            ds_ref[...] = jnp.zeros_like(ds_ref)

        for u in range(SWA_STEP):
            blk = SWA_STEP * i + u
            rows = slice(nb * u, nb * (u + 1))
            r1 = pl.ds(pl.multiple_of(blk * nb, nb), nb)
            r0 = pl.ds(pl.multiple_of(jnp.maximum(blk - 1, 0) * nb, nb), nb)
            consts, acts = _swa_args(refs[:10], u, i > 0)
            _, pull = jax.vjp(functools.partial(_swa_block, *consts), *acts)
            dq, dk0, dv0, dk1, dv1, dsr = pull(tuple(dy_ref[rows, LANES * p:LANES * (p + 1)] for p in range(4)))
            dq_ref[rows, :] = dq.astype(BF16)
            dk_ref[r0, :] += dk0
            dv_ref[r0, :] += dv0
            dk_ref[r1, :] += dk1
            dv_ref[r1, :] += dv1
            ds_ref[...] += dsr

    full = pl.BlockSpec((s, LANES), lambda i: (0, 0))
    return pl.pallas_call(
        kern, name=name, grid=(s // n,), in_specs=[pl.BlockSpec((n, 512), lambda i: (i, dy_cb))] + _swa_specs(),
        out_specs=[pl.BlockSpec((n, 512), lambda i: (i, 0)), full, full, pl.BlockSpec((1, LANES), lambda i: (0, 0))],
        out_shape=[jax.ShapeDtypeStruct((s, 512), BF16), jax.ShapeDtypeStruct((s, LANES), F32),
                   jax.ShapeDtypeStruct((s, LANES), F32), jax.ShapeDtypeStruct((1, LANES), F32)],
        compiler_params=_cp(("arbitrary",)),
    )(dy, *_swa_consts(), hc, hc, hc, hc, hc, srow)


RG_TILE = 512


def _neg_expm1(x):
    ser = x * (1 + x / 2 * (1 + x / 3 * (1 + x / 4 * (1 + x / 5 * (1 + x / 6 * (1 + x / 7 * (1 + x / 8)))))))
    return jnp.where(x > -0.25, -ser, 1.0 - jnp.exp(x))


def _rg_gates(xc, wa, wx, ba, bx, lam):
    r = jax.nn.sigmoid(_bdot(xc, wa) + ba)
    i = jax.nn.sigmoid(_bdot(xc, wx) + bx)
    log_a = -RG_C * r * jax.nn.softplus(-lam)
    u = jnp.sqrt(jnp.maximum(_neg_expm1(2.0 * log_a), 0.0)) * (i * xc)
    return jnp.exp(log_a), u


def _rg_out(hs, gate):
    return hs * jax.nn.gelu(gate)


def _rows_to_tile(rows):
    sub = _iota((SUBLANES, 1), 0)
    out = jnp.broadcast_to(rows[0], (SUBLANES, rows[0].shape[1]))
    for j in range(1, SUBLANES):
        out = jnp.where(sub == j, rows[j], out)
    return out


def _rg_fwd(xc, hc, wa, wx, ba, bx, lam, name):
    s = xc.shape[0]
    tm = min(RG_TILE, s)

    def kern(x_ref, g_ref, wa_ref, wx_ref, ba_ref, bx_ref, lam_ref, y_ref, h_ref, a_s, u_s, carry):
        @pl.when(pl.program_id(0) == 0)
        def _():
            carry[...] = jnp.zeros_like(carry)

        a, u = _rg_gates(x_ref[...], wa_ref[...], wx_ref[...], ba_ref[...], bx_ref[...], lam_ref[...])
        a_s[...] = a
        u_s[...] = u

        def body(gi, hp):
            r0 = pl.multiple_of(gi * SUBLANES, SUBLANES)
            ab, ub = a_s[pl.ds(r0, SUBLANES), :], u_s[pl.ds(r0, SUBLANES), :]
            rows = []
            for j in range(SUBLANES):
                hp = ab[j:j + 1] * hp + ub[j:j + 1]
                rows.append(hp)
            h_ref[pl.ds(r0, SUBLANES), :] = _rows_to_tile(rows)
            return hp

        carry[0:1, :] = lax.fori_loop(0, tm // SUBLANES, body, carry[0:1, :])
        y_ref[...] = _rg_out(h_ref[...], g_ref[...]).astype(BF16)

    row = pl.BlockSpec((tm, 512), lambda i: (i, 0))
    mat = pl.BlockSpec((512, 512), lambda i: (0, 0))
    vec = pl.BlockSpec((1, 512), lambda i: (0, 0))
    return pl.pallas_call(
        kern, name=name, grid=(s // tm,),
        in_specs=[row, pl.BlockSpec((tm, 512), lambda i: (i, CD_GATE // 512)), mat, mat, vec, vec, vec],
        out_specs=[row, row],
        out_shape=[jax.ShapeDtypeStruct((s, 512), BF16), jax.ShapeDtypeStruct((s, 512), F32)],
        scratch_shapes=[pltpu.VMEM((tm, 512), F32), pltpu.VMEM((tm, 512), F32), pltpu.VMEM((SUBLANES, 512), F32)],
        compiler_params=_cp(("arbitrary",)),
    )(xc, hc, wa, wx, ba, bx, lam)


def _rg_bwd(dy, dy_cb, xc, hc, hs, wa, wx, ba, bx, lam, name):
    s = xc.shape[0]
    tm = min(RG_TILE, s)
    nt = s // tm
    ti = lambda i: nt - 1 - i

    def kern(dy_ref, x_ref, g_ref, h_ref, hp_ref, wa_ref, wx_ref, ba_ref, bx_ref, lam_ref,
             dx_ref, dg_ref, dwa_ref, dwx_ref, dba_ref, dbx_ref, dlam_ref, a_s, d_s, g_s, cg, ca):
        first = pl.program_id(0) == 0

        @pl.when(first)
        def _():
            cg[...] = jnp.zeros_like(cg)
            ca[...] = jnp.zeros_like(ca)

        (a, _), pull = jax.vjp(_rg_gates, x_ref[...], wa_ref[...].astype(F32), wx_ref[...].astype(F32), ba_ref[...],
                               bx_ref[...], lam_ref[...])
        hs_t = h_ref[...]
        _, pull_out = jax.vjp(_rg_out, hs_t, g_ref[...])
        dho, dgate = pull_out(dy_ref[...])
        dg_ref[...] = dgate.astype(BF16)
        a_s[...] = a
        d_s[...] = dho

        def body(k, c):
            gn, an = c
            r0 = pl.multiple_of((tm // SUBLANES - 1 - k) * SUBLANES, SUBLANES)
            ab, db = a_s[pl.ds(r0, SUBLANES), :], d_s[pl.ds(r0, SUBLANES), :]
            rows = [None] * SUBLANES
            for j in range(SUBLANES - 1, -1, -1):
                gn = db[j:j + 1] + an * gn
                an = ab[j:j + 1]
                rows[j] = gn
            g_s[pl.ds(r0, SUBLANES), :] = _rows_to_tile(rows)
            return gn, an

        gn, an = lax.fori_loop(0, tm // SUBLANES, body, (cg[0:1, :], ca[0:1, :]))
        cg[0:1, :] = gn
        ca[0:1, :] = an
        g = g_s[...]
        halo = jnp.where(pl.program_id(0) == nt - 1, 0.0, hp_ref[...])
        hprev = pltpu.roll(jnp.concatenate([halo, hs_t], axis=0), 1, axis=0)[SUBLANES:]
        dxc, dwa, dwx, dba, dbx, dlam = pull((g * hprev, g))
        dx_ref[...] = dxc
        _acc(dwa_ref, dwa, first)
        _acc(dwx_ref, dwx, first)
        _acc(dba_ref, dba, first)
        _acc(dbx_ref, dbx, first)
        _acc(dlam_ref, dlam, first)

    row = pl.BlockSpec((tm, 512), lambda i: (ti(i), 0))
    mat = pl.BlockSpec((512, 512), lambda i: (0, 0))
    vec = pl.BlockSpec((1, 512), lambda i: (0, 0))
    hprev = pl.BlockSpec((SUBLANES, 512), lambda i: (jnp.maximum(ti(i) * (tm // SUBLANES) - 1, 0), 0))
    return pl.pallas_call(
        kern, name=name, grid=(nt,),
        in_specs=[pl.BlockSpec((tm, 512), lambda i: (ti(i), dy_cb)), row,
                  pl.BlockSpec((tm, 512), lambda i: (ti(i), CD_GATE // 512)), row, hprev, mat, mat, vec, vec, vec],
        out_specs=[row, row, mat, mat, vec, vec, vec],
        out_shape=[jax.ShapeDtypeStruct((s, 512), F32), jax.ShapeDtypeStruct((s, 512), BF16),
                   jax.ShapeDtypeStruct((512, 512), F32), jax.ShapeDtypeStruct((512, 512), F32),
                   jax.ShapeDtypeStruct((1, 512), F32), jax.ShapeDtypeStruct((1, 512), F32), jax.ShapeDtypeStruct((1, 512), F32)],
        scratch_shapes=[pltpu.VMEM((tm, 512), F32)] * 3 + [pltpu.VMEM((SUBLANES, 512), F32)] * 2,
        compiler_params=_cp(("arbitrary",)),
    )(dy, xc, hc, hs, hs, wa, wx, ba, bx, lam)


AB_REF = dict(z=(0, 512), xbc=(512, 768), dt=(1280, 8), hq=(1288, 512), hf=(1800, 512), hi=(2312, 512), hg=(2824, 512))
AB_INT = dict(z=AB_Z, hq=AB_HQ, hf=AB_HF, hi=AB_HI, hg=AB_HG, xbc=AB_XBC, dt=AB_DT)
CD_REF = dict(q=(0, 512), k=(512, 128), v=(640, 128), gate=(768, 512), xr=(1280, 512))


def _cols(w, start, size):
    return lax.slice_in_dim(w, start, start + size, axis=w.ndim - 1)


def _pad_lanes(v):
    return jnp.concatenate([v.astype(F32), jnp.zeros((LANES - v.shape[0],), F32)]).reshape(1, LANES)


def _taker(g, layer, axis):
    width = g.shape[2 + axis]

    def take(start, size):
        out = []
        while size > 0:
            k, off = divmod(start, width)
            n = min(size, width - off)
            out.append(lax.slice_in_dim(g[k, layer], off, off + n, axis=axis))
            start, size = start + n, size - n
        return out

    return take


def _ab_in_to_int(g, layer):
    take = _taker(g, layer, 1)
    parts = sum([take(*AB_REF[k]) for k in ("z", "hq", "hf", "hi", "hg", "xbc", "dt")], [])
    return jnp.concatenate(parts + [jnp.zeros((g.shape[2], AB_COLS - AB_DT - 8), g.dtype)], axis=1)


def _ab_in_to_ref(w):
    return jnp.concatenate([_cols(w, AB_INT[k], AB_REF[k][1]) for k in ("z", "xbc", "dt", "hq", "hf", "hi", "hg")], axis=-1)


def _pairs_to_ref(w, axis):
    sl = lambda i: lax.slice_in_dim(w, 64 * i, 64 * i + 64, axis=axis)
    return jnp.concatenate([sl(2 * h) for h in range(4)] + [sl(2 * h + 1) for h in range(4)], axis=axis)


PAIR_ORDER = tuple(h for p in range(4) for h in (p, 4 + p))


def _cd_in_to_int(g, layer):
    take = _taker(g, layer, 1)
    q = sum([take(64 * h, 64) for h in PAIR_ORDER], [])
    return jnp.concatenate(q + sum([take(*CD_REF[k]) for k in ("gate", "xr", "k", "v")], []), axis=1)


def _cd_out_to_int(g, layer):
    take = _taker(g, layer, 0)
    return jnp.concatenate(sum([take(64 * h, 64) for h in PAIR_ORDER], []) + take(512, 512), axis=0)


def _whole(g, layer, axis):
    return jnp.concatenate([g[k, layer] for k in range(4)], axis=axis)


def _cd_in_to_ref(w):
    q = _pairs_to_ref(_cols(w, CD_Q, 512), w.ndim - 1)
    return jnp.concatenate([q, _cols(w, CD_K, 128), _cols(w, CD_V, 128), _cols(w, CD_GATE, 512), _cols(w, CD_XR, 512)], axis=-1)


def _block_diag(w):
    tiled = jnp.concatenate([w.reshape(RG_WIDTH, 64)] * 8, axis=1)
    return tiled * _BLOCK_DIAG_MASK.astype(w.dtype)


_BLOCK_DIAG_MASK = np.kron(np.eye(8, dtype=np.float32), np.ones((64, 64), np.float32))


def _diag_blocks(m):
    return jnp.sum((m * _BLOCK_DIAG_MASK).reshape(8, 64, 8, 64), axis=2)


def _conv8(w):
    return jnp.concatenate([w, jnp.zeros((SUBLANES - w.shape[0], w.shape[1]), w.dtype)], axis=0)


def _hg_lower_bounds(hg_lower):
    sm = jax.nn.softmax(hg_lower.astype(F32), axis=0)
    return jnp.clip(jnp.cumsum(sm, axis=0) - sm[0], 0.0, 1.0)


def _local_step(x, target, w):
    s = x.shape[0]
    gup = w["ffn_w_up"]
    wdowns = [_whole(w["ffn_w_down"], l, 0) for l in range(DEPTH)]
    lb_all, lb_pull = jax.vjp(_hg_lower_bounds, w["hg_lower"])
    grads = {k: [None] * (v.shape[1] if k in BIG else v.shape[0]) for k, v in w.items()}
    saved = []
    xb = x.astype(BF16)
    for l in range(DEPTH):
        j = l // 2
        t = f"l{l}"
        sv = dict(x=x, xb=xb)
        if l % 2 == 0:
            win = _ab_in_to_int(w["ab_w_in"], j)
            wout = _whole(w["ab_w_out"], j, 0)
            h = _mm([(xb, 0, win, 0, D_MODEL)], AB_COLS, F32, t + "_in")
            w8 = _conv8(w["ssd_conv_w"][j])
            xbcc = _dwconv_fwd(h, AB_XBC // CONV_CB, SSD_CONV_DIM, w8, w["ssd_conv_b"][j], 4, t + "_conv")
            sp = (_pad_lanes(w["ssd_dt_bias"][j]), _pad_lanes(w["ssd_a_log"][j]), _pad_lanes(w["ssd_d"][j]),
                  w["ssd_norm_w"][j].reshape(1, 512))
            ya, ssts = _ssd_fwd(h, xbcc, *sp, t + "_ssd")
            hp = (lb_all[j].reshape(1, 512), w["hg_norm_w"][j].reshape(1, LANES))
            yb, hsts = _hg_fwd(h, *hp, t + "_hg")
            sv.update(win=win, wout=wout, h=h, w8=w8, xbcc=xbcc, sp=sp, ssts=ssts, hp=hp, hsts=hsts)
        else:
            win = _cd_in_to_int(w["cd_w_in"], j)
            wout = _cd_out_to_int(w["cd_w_out"], j)
            h = _mm([(xb, 0, win, 0, D_MODEL)], CD_COLS, F32, t + "_in")
            srow = _pad_lanes(w["swa_sinks"][j])
            ya = _swa_fwd(h, srow, t + "_swa")
            w8 = _conv8(w["rg_conv_w"][j])
            xc = _dwconv_fwd(h, CD_XR // CONV_CB, RG_WIDTH, w8, w["rg_conv_b"][j], 4, t + "_conv")
            rp = (_block_diag(w["rg_wa"][j]).astype(BF16), _block_diag(w["rg_wx"][j]).astype(BF16),
                  w["rg_ba"][j].reshape(1, 512), w["rg_bx"][j].reshape(1, 512), w["rg_lambda"][j].reshape(1, 512))
            yb, hs = _rg_fwd(xc, h, *rp, t + "_rg")
            sv.update(win=win, wout=wout, h=h, w8=w8, srow=srow, xc=xc, rp=rp, hs=hs)
        x1, x1b, r1 = _mm_ln([(ya, 0, wout, 0, 512), (yb, 0, wout, 1, 512)], x, w["ln_g"][l, 0], w["ln_b"][l, 0], t + "_out_ln")
        wdown = wdowns[l]
        fw8 = _conv8(w["ffn_conv_w"][l])
        hu = _mm([(x1b, 0, gup, _shard_as_col_tile(gup, l), D_MODEL)], 2 * FFN_DIM, BF16, t + "_up")
        a, hg, hv = _ffn_gate_fwd(hu, fw8, w["ffn_conv_b"][l], t + "_gate")
        x, xb, r2 = _mm_ln([(a, 0, wdown, 0, FFN_DIM)], x1, w["ln_g"][l, 1], w["ln_b"][l, 1], t + "_down_ln")
        sv.update(ya=ya, yb=yb, r1=r1, x1b=x1b, hu=hu, hg=hg, hv=hv, a=a, r2=r2, fw8=fw8)
        saved.append(sv)

    loss = 0.5 * jnp.sum(_sqerr(x, target, "sqerr")) / D_MODEL
    dya, dyb, ca, cb = x, target, 1.0 / D_MODEL, -1.0 / D_MODEL
    for l in range(DEPTH - 1, -1, -1):
        j = l // 2
        t = f"l{l}b"
        sv = saved[l]
        wdown = wdowns[l]
        dr2, dr2b, dg2, db2, da = _ln_bwd(dya, dyb, ca, cb, sv["r2"], w["ln_g"][l, 1], wdown, BF16, t + "_ln2_down")
        grads["ffn_w_down"][l] = _mm_tn(sv["a"], 0, FFN_DIM, dr2b, 0, D_MODEL, t + "_wdown")
        dhg, dhu, dfw8, dfb = _ffn_gate_bwd(sv["hu"], sv["hg"], sv["hv"], da, sv["fw8"], t + "_gate")
        grads["ffn_conv_w"][l] = dfw8[:3]
        grads["ffn_conv_b"][l] = dfb
        quarter = gup.shape[3]
        dx1 = _mm([(dh, half, gup, _shard_transposed(gup, 2 * gi + half, l), quarter)
                   for gi, dh in enumerate((dhg, dhu)) for half in range(2)], D_MODEL, F32, t + "_up", trans_b=True)
        grads["ffn_w_up"][l] = jnp.concatenate([_mm_tn(sv["x1b"], 0, D_MODEL, dhg, 0, FFN_DIM, t + "_wup_g"),
                                                _mm_tn(sv["x1b"], 0, D_MODEL, dhu, 0, FFN_DIM, t + "_wup_u")], axis=1)
        win, wout, h = sv["win"], sv["wout"], sv["h"]
        dr1, dr1b, dg1, db1, dycat = _ln_bwd(dr2, dx1, ALPHA, 1.0, sv["r1"], w["ln_g"][l, 0], wout, F32, t + "_ln1_out")
        grads["ln_g"][l] = jnp.stack([dg1, dg2])
        grads["ln_b"][l] = jnp.stack([db1, db2])
        dwout = jnp.concatenate([_mm_tn(sv["ya"], 0, 512, dr1b, 0, D_MODEL, t + "_wout_a"),
                                 _mm_tn(sv["yb"], 0, 512, dr1b, 0, D_MODEL, t + "_wout_b")], axis=0)
        if l % 2 == 0:
            dz, dxbcc, ddt, ddtb, dal, ddsk, dnw = _ssd_bwd(dycat, 0, h, sv["xbcc"], sv["ssts"], *sv["sp"], t + "_ssd")
            dxbc, dcw, dcb = _dwconv_bwd(dxbcc, h, AB_XBC // CONV_CB, SSD_CONV_DIM, sv["w8"], 4, t + "_conv")
            dq, df, di, dg, dlb, dhnw = _hg_bwd(dycat, 1, h, sv["hsts"], *sv["hp"], t + "_hg")
            grads["ab_w_out"][j] = dwout
            grads["ssd_conv_w"][j], grads["ssd_conv_b"][j] = dcw[:4], dcb
            grads["ssd_dt_bias"][j], grads["ssd_a_log"][j], grads["ssd_d"][j] = ddtb[0, :8], dal[0, :8], ddsk[0, :8]
            grads["ssd_norm_w"][j], grads["hg_norm_w"][j] = dnw[0], dhnw[0]
            grads["hg_lower"][j] = dlb[0]
            pieces = [(dz, 0, AB_Z, 512), (dq, 0, AB_HQ, 512), (df, 0, AB_HF, 512), (di, 0, AB_HI, 512), (dg, 0, AB_HG, 512),
                      (dxbc, 0, AB_XBC, 256), (dxbc, 1, AB_XBC + 256, 256), (dxbc, 2, AB_XBC + 512, 256), (ddt, 0, AB_DT, LANES)]
            dwin = [_mm_tn(sv["xb"], 0, D_MODEL, p, 0, p.shape[1], t + f"_win{i}")
                    for i, p in enumerate((dz, dq, df, di, dg, dxbc, ddt))]
            grads["ab_w_in"][j] = _ab_in_to_ref(jnp.concatenate(dwin, axis=1))
        else:
            dq, dk, dv, dsr = _swa_bwd(dycat, 0, h, sv["srow"], t + "_swa")
            dxc, dgate, dwa, dwx, dba, dbx, dlam = _rg_bwd(dycat, 1, sv["xc"], h, sv["hs"], *sv["rp"], t + "_rg")
            dxr, dcw, dcb = _dwconv_bwd(dxc, h, CD_XR // CONV_CB, RG_WIDTH, sv["w8"], 4, t + "_conv")
            grads["cd_w_out"][j] = jnp.concatenate([_pairs_to_ref(dwout[:512], 0), dwout[512:]], axis=0)
            grads["swa_sinks"][j] = dsr[0, :8]
            grads["rg_conv_w"][j], grads["rg_conv_b"][j] = dcw[:4], dcb
            grads["rg_wa"][j], grads["rg_wx"][j] = _diag_blocks(dwa), _diag_blocks(dwx)
            grads["rg_ba"][j], grads["rg_bx"][j], grads["rg_lambda"][j] = dba[0], dbx[0], dlam[0]
            pieces = [(dq, 0, CD_Q, 512), (dgate, 0, CD_GATE, 512), (dxr, 0, CD_XR, 512), (dk, 0, CD_K, LANES), (dv, 0, CD_V, LANES)]
            dwin = [_mm_tn(sv["xb"], 0, D_MODEL, p, 0, p.shape[1], t + f"_win{i}")
                    for i, p in enumerate((dq, dgate, dxr, dk, dv))]
            grads["cd_w_in"][j] = _cd_in_to_ref(jnp.concatenate(dwin, axis=1))
        dx0 = _mm([(p, pcb, win, off // k, k) for p, pcb, off, k in pieces], D_MODEL, F32, t + "_in", trans_b=True)
        dya, dyb, ca, cb = dr1, dx0, ALPHA, 1.0
    grad_x = _axpby(dya, dyb, ca, cb, "grad_x")
    out = {k: (v if k in BIG else jnp.stack(v)) for k, v in grads.items()}
    (out["hg_lower"],) = lb_pull(out["hg_lower"])
    return loss, grad_x, out


MESH = pl.DeviceIdType.MESH
ANY = pl.BlockSpec(memory_space=pl.ANY)


def _place():
    x, y, c = lax.axis_index("x"), lax.axis_index("y"), lax.axis_index("c")
    return x, y, c, [(1 - x, y), (x, 1 - y), (1 - x, 1 - y)]


def _rcopy(src, dst, send_sems, recv_sems, k, to):
    return pltpu.make_async_remote_copy(src_ref=src, dst_ref=dst, send_sem=send_sems.at[k], recv_sem=recv_sems.at[k],
                                        device_id=to, device_id_type=MESH)


def _zeros_index(ndim):
    return (0,) * ndim


def _gather_chips(parts, name):
    n = len(parts)

    def body(*refs):
        w_refs, out_refs, (send_sems, recv_sems) = refs[:n], refs[n:2 * n], refs[2 * n:]
        x, y, c, chips = _place()
        me, sib = 2 * x + y, (x, y, 1 - c)
        first = [_rcopy(w_refs[a].at[c], out_refs[a].at[me, c], send_sems, recv_sems, 6 * a + k, (cx, cy, c))
                 for a in range(n) for k, (cx, cy) in enumerate(chips)]
        for cp in first:
            cp.start()
        passed = []
        for a in range(n):
            for k, (cx, cy) in enumerate(chips):
                blk = out_refs[a].at[2 * cx + cy, c]
                _rcopy(blk, blk, send_sems, recv_sems, 6 * a + k, (cx, cy, c)).wait_recv()
                passed.append(_rcopy(blk, blk, send_sems, recv_sems, 6 * a + 3 + k, sib))
                passed[-1].start()
        for a in range(n):
            for k, (cx, cy) in enumerate(chips):
                blk = out_refs[a].at[2 * cx + cy, 1 - c]
                _rcopy(blk, blk, send_sems, recv_sems, 6 * a + 3 + k, sib).wait_recv()
        for cp in first + passed:
            cp.wait_send()

    outs = pl.pallas_call(
        body, name=name, in_specs=[ANY] * n, out_specs=[ANY] * n,
        out_shape=[jax.ShapeDtypeStruct((4,) + p.shape, p.dtype) for p in parts],
        scratch_shapes=[pltpu.SemaphoreType.DMA((6 * n,)), pltpu.SemaphoreType.DMA((6 * n,))],
    )(*parts)
    me = 2 * lax.axis_index("x") + lax.axis_index("y")
    return [lax.dynamic_update_slice(o, p[None], (me,) + _zeros_index(p.ndim)) for o, p in zip(outs, parts)]


def _swap_halves(gs, name):
    n = len(gs)

    def body(*refs):
        g_refs, x_refs, (send_sems, recv_sems) = refs[:n], refs[n:2 * n], refs[2 * n:]
        x, y, c, _ = _place()
        cps = [_rcopy(g_refs[a].at[:, 1 - c], x_refs[a], send_sems, recv_sems, a, (x, y, 1 - c)) for a in range(n)]
        for cp in cps:
            cp.start()
        for cp in cps:
            cp.wait()

    return pl.pallas_call(
        body, name=name, in_specs=[ANY] * n, out_specs=[ANY] * n,
        out_shape=[jax.ShapeDtypeStruct((g.shape[0],) + g.shape[2:], g.dtype) for g in gs],
        scratch_shapes=[pltpu.SemaphoreType.DMA((n,)), pltpu.SemaphoreType.DMA((n,))],
    )(*gs)


def _scatter_chips(hbs, name):
    n = len(hbs)

    def body(*refs):
        h_refs, y_refs, (send_sems, recv_sems) = refs[:n], refs[n:2 * n], refs[2 * n:]
        x, y, c, chips = _place()
        me = 2 * x + y
        sends = [_rcopy(h_refs[a].at[2 * cx + cy], y_refs[a].at[me], send_sems, recv_sems, 3 * a + k, (cx, cy, c))
                 for a in range(n) for k, (cx, cy) in enumerate(chips)]
        for cp in sends:
            cp.start()
        for a in range(n):
            for k, (cx, cy) in enumerate(chips):
                blk = y_refs[a].at[2 * cx + cy]
                _rcopy(blk, blk, send_sems, recv_sems, 3 * a + k, (cx, cy, c)).wait_recv()
        for cp in sends:
            cp.wait_send()

    outs = pl.pallas_call(
        body, name=name, in_specs=[ANY] * n, out_specs=[ANY] * n,
        out_shape=[jax.ShapeDtypeStruct(h.shape, h.dtype) for h in hbs],
        scratch_shapes=[pltpu.SemaphoreType.DMA((3 * n,)), pltpu.SemaphoreType.DMA((3 * n,))],
    )(*hbs)
    me = 2 * lax.axis_index("x") + lax.axis_index("y")
    return [lax.dynamic_update_slice(o, lax.dynamic_slice_in_dim(h, me, 1, axis=0), (me,) + _zeros_index(h.ndim - 1))
            for o, h in zip(outs, hbs)]


def _share_half(fs, name):
    n = len(fs)

    def body(*refs):
        f_refs, out_refs, (send_sems, recv_sems) = refs[:n], refs[n:2 * n], refs[2 * n:]
        x, y, c, _ = _place()
        sib = (x, y, 1 - c)
        cps = [_rcopy(f_refs[a], out_refs[a].at[c], send_sems, recv_sems, a, sib) for a in range(n)]
        for cp in cps:
            cp.start()
        for a in range(n):
            _rcopy(f_refs[a], out_refs[a].at[1 - c], send_sems, recv_sems, a, sib).wait_recv()
        for cp in cps:
            cp.wait_send()

    outs = pl.pallas_call(
        body, name=name, in_specs=[ANY] * n, out_specs=[ANY] * n,
        out_shape=[jax.ShapeDtypeStruct((2,) + f.shape, f.dtype) for f in fs],
        scratch_shapes=[pltpu.SemaphoreType.DMA((n,)), pltpu.SemaphoreType.DMA((n,))],
    )(*fs)
    c = lax.axis_index("c")
    return [lax.dynamic_update_slice(o, f[None], (c,) + _zeros_index(f.ndim)) for o, f in zip(outs, fs)]


def _gather_devices(v, name):
    m_per, n = v.shape

    def body(x_ref, out_ref, send_sems, recv_sems, local_sem):
        x, y, c, chips = _place()
        me, sib = (x, y, c), (x, y, 1 - c)

        def rows(px, py, pc):
            return out_ref.at[pl.ds((4 * px + 2 * py + pc) * m_per, m_per), :]

        def copy(k, block, to, src=None):
            return _rcopy(rows(*block) if src is None else src, rows(*block), send_sems, recv_sems, k, to)

        mine = pltpu.make_async_copy(x_ref, rows(*me), local_sem)
        mine.start()
        first = [copy(0, me, sib, src=x_ref)]
        first += [copy(1 + j, me, (*chip, c), src=x_ref) for j, chip in enumerate(chips)]
        for cp in first:
            cp.start()
        passed = [copy(4 + j, (*chip, c), sib) for j, chip in enumerate(chips)]
        for j, chip in enumerate(chips):
            copy(1 + j, (*chip, c), me).wait_recv()
            passed[j].start()
        copy(0, sib, me).wait_recv()
        for j, chip in enumerate(chips):
            copy(4 + j, (*chip, 1 - c), me).wait_recv()
        for cp in first + passed:
            cp.wait_send()
        mine.wait()

    return pl.pallas_call(
        body, name=name, out_shape=jax.ShapeDtypeStruct((8 * m_per, n), v.dtype),
        in_specs=[pl.BlockSpec(memory_space=pltpu.VMEM)], out_specs=pl.BlockSpec(memory_space=pltpu.VMEM),
        scratch_shapes=[pltpu.SemaphoreType.DMA((7,)), pltpu.SemaphoreType.DMA((7,)), pltpu.SemaphoreType.DMA(())],
        compiler_params=_cp(),
    )(v)


ROW_TILES = (512, 256, 352, 128)


def _add_half(g, xr, c, name):
    _, _, r, n = g.shape
    tr = _pick(r, ROW_TILES)

    def kern(c_ref, g_ref, x_ref, o_ref):
        o_ref[...] = (g_ref[0] + x_ref[...]).astype(BF16)

    return pl.pallas_call(
        kern, name=name,
        grid_spec=pltpu.PrefetchScalarGridSpec(
            num_scalar_prefetch=1, grid=(4, r // tr),
            in_specs=[pl.BlockSpec((1, 1, tr, n), lambda k, i, c_ref: (k, c_ref[0], i, 0)),
                      pl.BlockSpec((1, tr, n), lambda k, i, c_ref: (k, i, 0))],
            out_specs=pl.BlockSpec((1, tr, n), lambda k, i, c_ref: (k, i, 0))),
        out_shape=jax.ShapeDtypeStruct(xr.shape, BF16), compiler_params=_cp(("parallel", "parallel")),
    )(c.reshape(1).astype(jnp.int32), g, xr)


def _sum_blocks(y, name, tr=None):
    n, r, cols = y.shape
    tr = tr or _pick(r, ROW_TILES)
    tr = tr if r % tr == 0 else r

    def kern(y_ref, o_ref):
        acc = y_ref[0].astype(F32)
        for k in range(1, n):
            acc = acc + y_ref[k].astype(F32)
        o_ref[...] = acc

    return pl.pallas_call(
        kern, name=name, grid=(r // tr,), in_specs=[pl.BlockSpec((n, tr, cols), lambda i: (0, i, 0))],
        out_specs=pl.BlockSpec((tr, cols), lambda i: (i, 0)), out_shape=jax.ShapeDtypeStruct((r, cols), F32),
        compiler_params=_cp(("parallel",)),
    )(y)


def _adamw(w, g, m, v, name, tr=256):
    r, n = w.shape
    tr = tr if r % tr == 0 else r

    def kern(w_ref, g_ref, m_ref, v_ref, d_ref, nm_ref, nv_ref):
        gg = g_ref[...]
        nm = ADAM_B1 * m_ref[...] + (1.0 - ADAM_B1) * gg
        nv = ADAM_B2 * v_ref[...] + (1.0 - ADAM_B2) * (gg * gg)
        m_hat = nm / (1.0 - ADAM_B1 ** ADAM_STEP)
        v_hat = nv / (1.0 - ADAM_B2 ** ADAM_STEP)
        d_ref[...] = -ADAM_LR * (m_hat / (jnp.sqrt(v_hat) + ADAM_EPS) + ADAM_WD * w_ref[...])
        nm_ref[...] = nm
        nv_ref[...] = nv

    blk = pl.BlockSpec((tr, n), lambda i: (i, 0))
    return pl.pallas_call(
        kern, name=name, grid=(r // tr,), in_specs=[blk] * 4, out_specs=[blk] * 3,
        out_shape=[jax.ShapeDtypeStruct((r, n), F32)] * 3, compiler_params=_cp(("parallel",)),
    )(w, g, m, v)


def _rows_of(n):
    return -(-n // (LANES * SUBLANES)) * SUBLANES


def _pack(arrs, lead, dtype, mult):
    ls = arrs[0].shape[:lead]
    blocks, total = [], 0
    for a in arrs:
        f = a.astype(dtype).reshape(ls + (-1,))
        n = f.shape[-1]
        rows = _rows_of(n)
        if rows * LANES != n:
            f = jnp.concatenate([f, jnp.zeros(ls + (rows * LANES - n,), dtype)], axis=-1)
        blocks.append(f.reshape(ls + (rows, LANES)))
        total += rows
    if total % mult:
        blocks.append(jnp.zeros(ls + (mult - total % mult, LANES), dtype))
    return jnp.concatenate(blocks, axis=lead)


def _unpack(buf, lead, shapes):
    ls = buf.shape[:lead]
    out, off = [], 0
    for sh in shapes:
        n = int(np.prod(sh))
        rows = _rows_of(n)
        flat = lax.slice_in_dim(buf, off, off + rows, axis=lead).reshape(ls + (rows * LANES,))
        if rows * LANES != n:
            flat = lax.slice_in_dim(flat, 0, n, axis=lead)
        out.append(flat.reshape(ls + tuple(sh)))
        off += rows
    return out


BIG = ("ab_w_in", "ab_w_out", "cd_w_in", "cd_w_out", "ffn_w_up", "ffn_w_down")
BIG_COLS = ("ab_w_in", "cd_w_in", "ffn_w_up")
SMALL_SHARDED = ("ssd_conv_w", "rg_conv_w", "rg_conv_b", "rg_ba", "rg_bx", "rg_lambda", "ffn_conv_w", "ln_g", "ln_b")
WEIGHTS = ("ab_w_in", "ssd_conv_w", "ssd_conv_b", "ssd_dt_bias", "ssd_a_log", "ssd_d", "ssd_norm_w", "hg_lower", "hg_norm_w",
           "ab_w_out", "cd_w_in", "swa_sinks", "rg_conv_w", "rg_conv_b", "rg_wa", "rg_ba", "rg_wx", "rg_bx", "rg_lambda",
           "cd_w_out", "ffn_w_up", "ffn_conv_w", "ffn_conv_b", "ffn_w_down", "ln_g", "ln_b")
SMALL = tuple(n for n in WEIGHTS if n not in BIG)


def _full_from_shards(name, g):
    if name in BIG_COLS or name in SMALL_SHARDED:
        return jnp.concatenate([g[k] for k in range(4)], axis=-1)
    return jnp.concatenate([g[k] for k in range(4)], axis=1)


def _layer_shards(name, f):
    k, n = f.shape
    if name in BIG_COLS:
        return jnp.transpose(f.reshape(k, 4, n // 4), (1, 0, 2))
    return f.reshape(4, k // 4, n)


def kernel(x, ab_w_in, ssd_conv_w, ssd_conv_b, ssd_dt_bias, ssd_a_log, ssd_d, ssd_norm_w, hg_lower, hg_norm_w, ab_w_out, cd_w_in, swa_sinks, rg_conv_w, rg_conv_b, rg_wa, rg_ba, rg_wx, rg_bx, rg_lambda, cd_w_out, ffn_w_up, ffn_conv_w, ffn_conv_b, ffn_w_down, ln_g, ln_b, loss_target, m_ab_w_in, m_ssd_conv_w, m_ssd_conv_b, m_ssd_dt_bias, m_ssd_a_log, m_ssd_d, m_ssd_norm_w, m_hg_lower, m_hg_norm_w, m_ab_w_out, m_cd_w_in, m_swa_sinks, m_rg_conv_w, m_rg_conv_b, m_rg_wa, m_rg_ba, m_rg_wx, m_rg_bx, m_rg_lambda, m_cd_w_out, m_ffn_w_up, m_ffn_conv_w, m_ffn_conv_b, m_ffn_w_down, m_ln_g, m_ln_b, v_ab_w_in, v_ssd_conv_w, v_ssd_conv_b, v_ssd_dt_bias, v_ssd_a_log, v_ssd_d, v_ssd_norm_w, v_hg_lower, v_hg_norm_w, v_ab_w_out, v_cd_w_in, v_swa_sinks, v_rg_conv_w, v_rg_conv_b, v_rg_wa, v_rg_ba, v_rg_wx, v_rg_bx, v_rg_lambda, v_cd_w_out, v_ffn_w_up, v_ffn_conv_w, v_ffn_conv_b, v_ffn_w_down, v_ln_g, v_ln_b):
    args = locals()
    w = {n: args[n] for n in WEIGHTS}
    mom = {n: args["m_" + n] for n in WEIGHTS}
    var = {n: args["v_" + n] for n in WEIGHTS}
    cx, cy, cc = lax.axis_index("x"), lax.axis_index("y"), lax.axis_index("c")
    chip = 2 * cx + cy

    halves = [w[n].astype(BF16).reshape((2, w[n].shape[0] // 2) + w[n].shape[1:]) for n in BIG]
    gathered = _gather_chips(halves, "gather_weights")
    full = {n: g.reshape((4,) + w[n].shape) for n, g in zip(BIG, gathered)}
    sp = _pack([w[n] for n in SMALL_SHARDED], 0, F32, SUBLANES)
    sg = _gather_devices(sp, "gather_small").reshape(4, 2, sp.shape[0], LANES)[:, 0]
    for n, g in zip(SMALL_SHARDED, _unpack(sg, 1, [w[n].shape for n in SMALL_SHARDED])):
        full[n] = _full_from_shards(n, g)
    for n in SMALL:
        full.setdefault(n, w[n])

    loss, grad_x, grads = _local_step(x[0], loss_target[0], full)
    loss = lax.psum(loss, ("x", "y", "c"))

    g4 = []
    for n in BIG:
        st = jnp.stack([_layer_shards(n, f) for f in grads[n]], axis=1)
        g4.append(st.reshape(4, 2, st.shape[1] // 2 * st.shape[2], st.shape[3]))
    hbs = [_add_half(g, xr, cc, "add_halves_" + n) for n, g, xr in zip(BIG, g4, _swap_halves(g4, "swap_halves"))]
    own = [_sum_blocks(y, "sum_chips_" + n) for n, y in zip(BIG, _scatter_chips(hbs, "scatter_chips"))]
    gout = {n: o.reshape(w[n].shape) for n, o in zip(BIG, _share_half(own, "share_half"))}

    small_shapes = [grads[n].shape for n in SMALL]
    gs = _pack([grads[n] for n in SMALL], 0, F32, SUBLANES)
    gsum = _sum_blocks(_gather_devices(gs, "gather_small_grads").reshape(8, gs.shape[0], LANES), "sum_small")
    for n, g in zip(SMALL, _unpack(gsum, 0, small_shapes)):
        if n in SMALL_SHARDED:
            width = w[n].shape[-1]
            g = lax.dynamic_slice_in_dim(g, chip * width, width, axis=g.ndim - 1)
        gout[n] = g

    delta, new_m, new_v = {}, {}, {}
    for n in BIG:
        sh = w[n].shape
        two = lambda a: a.reshape(-1, sh[-1])
        d, nm, nv = _adamw(two(w[n]), two(gout[n]), two(mom[n]), two(var[n]), "adamw_" + n)
        delta[n], new_m[n], new_v[n] = d.reshape(sh), nm.reshape(sh), nv.reshape(sh)
    local_shapes = [w[n].shape for n in SMALL]
    packs = [_pack([d[n] for n in SMALL], 0, F32, SUBLANES) for d in (w, gout, mom, var)]
    for dst, buf in zip((delta, new_m, new_v), _adamw(*packs, "adamw_small")):
        dst.update(zip(SMALL, _unpack(buf, 0, local_shapes)))

    return (loss, grad_x[None], *[gout[n] for n in WEIGHTS], *[delta[n] for n in WEIGHTS],
            *[new_m[n] for n in WEIGHTS], *[new_v[n] for n in WEIGHTS])
```

```python
import functools

import numpy as np
import jax
import jax.numpy as jnp
from jax import lax
from jax.experimental import pallas as pl
from jax.experimental.pallas import tpu as pltpu

F32 = jnp.float32
BF16 = jnp.bfloat16

D_MODEL = 1024
DEPTH = 4
SSD_HEADS = 8
SSD_D_INNER = 512
SSD_CONV_DIM = 768
SSD_CHUNK = 128
HG_CHUNK = 64
HG_HEADS = 4
SWA_BLOCK = 128
RG_WIDTH = 512
FFN_DIM = 2816
LN_EPS = 1e-5
RMS_EPS = 1e-6
MASK_VALUE = -1e9
ALPHA = (2 * DEPTH) ** 0.25
RG_C = 8.0
ADAM_LR, ADAM_B1, ADAM_B2, ADAM_EPS, ADAM_WD, ADAM_STEP = 0.001, 0.9, 0.999, 1e-08, 0.01, 10

LANES = 128
SUBLANES = 8
HALO = 16
CONV_CB = 256
VMEM_LIMIT = 56 * 1024 * 1024

AB_Z, AB_HQ, AB_HF, AB_HI, AB_HG, AB_XBC, AB_DT, AB_COLS = 0, 512, 1024, 1536, 2048, 2560, 3328, 3456
CD_Q, CD_GATE, CD_XR, CD_K, CD_V, CD_COLS = 0, 512, 1024, 1536, 1664, 1792


def _cp(sem=None):
    kw = dict(vmem_limit_bytes=VMEM_LIMIT)
    if sem is not None:
        kw["dimension_semantics"] = sem
    return pltpu.CompilerParams(**kw)


def _dot(a, b, dims=(((1,), (0,)), ((), ())), precision=None):
    return lax.dot_general(a, b, dims, precision=precision, preferred_element_type=F32)


NN = (((1,), (0,)), ((), ()))
NT = (((1,), (1,)), ((), ()))
TN = (((0,), (0,)), ((), ()))


@functools.partial(jax.custom_vjp, nondiff_argnums=(2,))
def _bdot(a, b, dims=NN):
    return _dot(a.astype(BF16), b.astype(BF16), dims)


def _bdot_fwd(a, b, dims):
    return _bdot(a, b, dims), (a, b)


def _bdot_bwd(dims, res, ct):
    a, b = res
    ab, bb, cb = a.astype(BF16), b.astype(BF16), ct.astype(BF16)
    if dims == NN:
        da, db = _dot(cb, bb, NT), _dot(ab, cb, TN)
    elif dims == NT:
        da, db = _dot(cb, bb, NN), _dot(cb, ab, TN)
    else:
        da, db = _dot(bb, cb, NT), _dot(ab, cb, NN)
    return da.astype(a.dtype), db.astype(b.dtype)


_bdot.defvjp(_bdot_fwd, _bdot_bwd)


WIDE_TILES = (1408, 1152, 1024, 896, 768, 512, 384, 256, 128)
MM_BLOCK_BYTES = 6 * 1024 * 1024


def _pick(n, cands):
    for c in cands:
        if n % c == 0:
            return c
    return n


def _mm(pairs, n_out, out_dtype, name, trans_b=False, tm=512):
    m = pairs[0][0].shape[0]
    a_row_bytes = sum(p[4] * p[0].dtype.itemsize for p in pairs)
    b_col_bytes = sum(p[4] * p[2].dtype.itemsize for p in pairs)
    tm = min(2 * tm if 2 * tm * a_row_bytes <= 2 * MM_BLOCK_BYTES else tm, m)
    tn = _pick(n_out, [c for c in WIDE_TILES if c * b_col_bytes <= MM_BLOCK_BYTES])
    n = len(pairs)

    def kern(*refs):
        acc = None
        for i in range(n):
            p = _bdot(refs[i][...], refs[n + i][...], NT if trans_b else NN)
            acc = p if acc is None else acc + p
        refs[2 * n][...] = acc.astype(out_dtype)

    in_specs = []
    for a, acb, b, bcb, k in pairs:
        in_specs.append(pl.BlockSpec((tm, k), functools.partial(lambda i, j, c: (i, c), c=acb)))
    for a, acb, b, bcb, k in pairs:
        if callable(bcb):
            in_specs.append(bcb(tn))
        elif trans_b:
            in_specs.append(pl.BlockSpec((tn, k), functools.partial(lambda i, j, c: (j, c), c=bcb)))
        else:
            in_specs.append(pl.BlockSpec((k, tn), functools.partial(lambda i, j, c: (c, j), c=bcb)))
    return pl.pallas_call(
        kern, name=name, grid=(m // tm, n_out // tn), in_specs=in_specs,
        out_specs=pl.BlockSpec((tm, tn), lambda i, j: (i, j)),
        out_shape=jax.ShapeDtypeStruct((m, n_out), out_dtype),
        compiler_params=_cp(("parallel", "arbitrary")),
    )(*[p[0] for p in pairs], *[p[2] for p in pairs])


def _shard_as_col_tile(g, layer):
    def spec(tn):
        assert tn == g.shape[3]
        return pl.BlockSpec((None, None, g.shape[2], tn), lambda i, j: (j, layer, 0, 0))
    return spec


def _shard_transposed(g, chip, layer):
    return lambda tn: pl.BlockSpec((None, None, tn, g.shape[3]), lambda i, j: (chip, layer, j, 0))


def _mm_tn(a, a_cb, ka, g, g_cb, ng, name, tm=2048):
    m = a.shape[0]
    tm = min(tm, m)
    tk = _pick(ka, (1024, 1408, 512, 256, 128))
    tn = _pick(ng, WIDE_TILES)
    nm = m // tm

    def kern(a_ref, g_ref, o_ref):
        p = _bdot(a_ref[...], g_ref[...], TN)

        @pl.when(pl.program_id(2) == 0)
        def _():
            o_ref[...] = p

        @pl.when(pl.program_id(2) > 0)
        def _():
            o_ref[...] += p

    ak, gk = ka // tk, ng // tn
    return pl.pallas_call(
        kern, name=name, grid=(ak, gk, nm),
        in_specs=[pl.BlockSpec((tm, tk), lambda i, j, r: (r, a_cb * ak + i)),
                  pl.BlockSpec((tm, tn), lambda i, j, r: (r, g_cb * gk + j))],
        out_specs=pl.BlockSpec((tk, tn), lambda i, j, r: (i, j)),
        out_shape=jax.ShapeDtypeStruct((ka, ng), F32),
        compiler_params=_cp(("parallel", "parallel", "arbitrary")),
    )(a, g)


def _mm_ln(pairs, x, g, b, name, tm=512):
    m, d = x.shape
    tm = min(tm, m)
    n = len(pairs)

    def kern(*refs):
        x_ref, g_ref, b_ref, y_ref, yb_ref, r_ref = refs[2 * n:]
        r = ALPHA * x_ref[...]
        for i in range(n):
            r = r + _bdot(refs[i][...], refs[n + i][...])
        mu = jnp.mean(r, -1, keepdims=True)
        xc = r - mu
        var = jnp.mean(xc * xc, -1, keepdims=True)
        y = xc * lax.rsqrt(var + LN_EPS) * g_ref[...] + b_ref[...]
        y_ref[...] = y
        yb_ref[...] = y.astype(BF16)
        r_ref[...] = r

    row = pl.BlockSpec((tm, d), lambda i: (i, 0))
    vec = pl.BlockSpec((1, d), lambda i: (0, 0))
    in_specs = [pl.BlockSpec((tm, k), functools.partial(lambda i, c: (i, c), c=acb)) for _, acb, _, _, k in pairs]
    in_specs += [pl.BlockSpec((k, d), functools.partial(lambda i, c: (c, 0), c=bcb)) for _, _, _, bcb, k in pairs]
    return pl.pallas_call(
        kern, name=name, grid=(m // tm,), in_specs=in_specs + [row, vec, vec], out_specs=[row, row, row],
        out_shape=[jax.ShapeDtypeStruct((m, d), F32), jax.ShapeDtypeStruct((m, d), BF16), jax.ShapeDtypeStruct((m, d), F32)],
        compiler_params=_cp(("parallel",)),
    )(*[p[0] for p in pairs], *[p[2] for p in pairs], x, g.reshape(1, d), b.reshape(1, d))


def _ln_bwd(dya, dyb, ca, cb, r, g, wt, out_dtype, name, tm=512):
    s, d = r.shape
    n = wt.shape[0]
    tm = min(tm, s)

    def kern(a_ref, b_ref, r_ref, g_ref, w_ref, dr_ref, drb_ref, dg_ref, db_ref, p_ref):
        dy = ca * a_ref[...] + cb * b_ref[...]
        rr = r_ref[...]
        mu = jnp.mean(rr, -1, keepdims=True)
        xc = rr - mu
        var = jnp.mean(xc * xc, -1, keepdims=True)
        rstd = lax.rsqrt(var + LN_EPS)
        xhat = xc * rstd
        dxh = dy * g_ref[...]
        dr = rstd * (dxh - jnp.mean(dxh, -1, keepdims=True) - xhat * jnp.mean(dxh * xhat, -1, keepdims=True))
        dr_ref[...] = dr
        drb = dr.astype(BF16)
        drb_ref[...] = drb
        p_ref[...] = _dot(drb, w_ref[...], NT).astype(out_dtype)
        dg = jnp.sum(dy * xhat, 0, keepdims=True)
        db = jnp.sum(dy, 0, keepdims=True)

        @pl.when(pl.program_id(0) == 0)
        def _():
            dg_ref[...] = dg
            db_ref[...] = db

        @pl.when(pl.program_id(0) > 0)
        def _():
            dg_ref[...] += dg
            db_ref[...] += db

    row = pl.BlockSpec((tm, d), lambda i: (i, 0))
    vec = pl.BlockSpec((1, d), lambda i: (0, 0))
    dr, drb, dg, db, prod = pl.pallas_call(
        kern, name=name, grid=(s // tm,),
        in_specs=[row, row, row, vec, pl.BlockSpec((n, d), lambda i: (0, 0))],
        out_specs=[row, row, vec, vec, pl.BlockSpec((tm, n), lambda i: (i, 0))],
        out_shape=[jax.ShapeDtypeStruct((s, d), F32), jax.ShapeDtypeStruct((s, d), BF16), jax.ShapeDtypeStruct((1, d), F32),
                   jax.ShapeDtypeStruct((1, d), F32), jax.ShapeDtypeStruct((s, n), out_dtype)],
        compiler_params=_cp(("arbitrary",)),
    )(dya, dyb, r, g.reshape(1, d), wt)
    return dr, drb, dg[0], db[0], prod


def _sqerr(y, t, name, tm=256):
    s, d = y.shape
    tm = min(tm, s)

    def kern(y_ref, t_ref, o_ref):
        e = y_ref[...] - t_ref[...]
        p = jnp.sum(e * e, 0, keepdims=True)

        @pl.when(pl.program_id(0) == 0)
        def _():
            o_ref[...] = p

        @pl.when(pl.program_id(0) > 0)
        def _():
            o_ref[...] += p

    row = pl.BlockSpec((tm, d), lambda i: (i, 0))
    return pl.pallas_call(
        kern, name=name, grid=(s // tm,), in_specs=[row, row], out_specs=pl.BlockSpec((1, d), lambda i: (0, 0)),
        out_shape=jax.ShapeDtypeStruct((1, d), F32), compiler_params=_cp(("arbitrary",)),
    )(y, t)


def _axpby(a, b, ca, cb, name, tm=256):
    s, d = a.shape
    tm = min(tm, s)

    def kern(a_ref, b_ref, o_ref):
        o_ref[...] = ca * a_ref[...] + cb * b_ref[...]

    row = pl.BlockSpec((tm, d), lambda i: (i, 0))
    return pl.pallas_call(
        kern, name=name, grid=(s // tm,), in_specs=[row, row], out_specs=row,
        out_shape=jax.ShapeDtypeStruct((s, d), F32), compiler_params=_cp(("parallel",)),
    )(a, b)


def _shift_down(xx, s, n):
    if s == 0:
        return xx[HALO:HALO + n]
    return pltpu.roll(xx, s, axis=0)[HALO:HALO + n]


def _shift_up(yy, s, n):
    if s == 0:
        return yy[0:n]
    return pltpu.roll(yy, yy.shape[0] - s, axis=0)[0:n]


def _prev_halo(tm):
    return lambda i: jnp.maximum(i * (tm // HALO) - 1, 0)


def _dwconv_fwd(x, x_cb0, c, w8, b, k, name, tm=1024):
    s = x.shape[0]
    tm = min(tm, s)
    ph = _prev_halo(tm)

    def kern(x_ref, h_ref, w_ref, b_ref, y_ref):
        halo = jnp.where(pl.program_id(0) == 0, 0.0, h_ref[...].astype(F32))
        xx = jnp.concatenate([halo, x_ref[...].astype(F32)], axis=0)
        w = w_ref[...]
        acc = b_ref[...] + w[k - 1:k] * xx[HALO:]
        for j in range(k - 1):
            acc = acc + w[j:j + 1] * _shift_down(xx, k - 1 - j, tm)
        y_ref[...] = acc

    return pl.pallas_call(
        kern, name=name, grid=(s // tm, c // CONV_CB),
        in_specs=[pl.BlockSpec((tm, CONV_CB), lambda i, j: (i, x_cb0 + j)),
                  pl.BlockSpec((HALO, CONV_CB), lambda i, j: (ph(i), x_cb0 + j)),
                  pl.BlockSpec((SUBLANES, CONV_CB), lambda i, j: (0, j)),
                  pl.BlockSpec((1, CONV_CB), lambda i, j: (0, j))],
        out_specs=pl.BlockSpec((tm, CONV_CB), lambda i, j: (i, j)),
        out_shape=jax.ShapeDtypeStruct((s, c), F32), compiler_params=_cp(("parallel", "parallel")),
    )(x, x, w8, b.reshape(1, c))


def _dwconv_bwd(dy, x, x_cb0, c, w8, k, name, tm=1024):
    s = x.shape[0]
    tm = min(tm, s)
    nt = s // tm
    ph = _prev_halo(tm)
    nh = lambda i: jnp.minimum((i + 1) * (tm // HALO), s // HALO - 1)

    def kern(dy_ref, dyn_ref, x_ref, h_ref, w_ref, dx_ref, dw_ref, db_ref):
        i = pl.program_id(1)
        halo = jnp.where(i == 0, 0.0, h_ref[...].astype(F32))
        xx = jnp.concatenate([halo, x_ref[...].astype(F32)], axis=0)
        dyt = dy_ref[...]
        nxt = jnp.where(i == nt - 1, 0.0, dyn_ref[...])
        dyy = jnp.concatenate([dyt, nxt], axis=0)
        w = w_ref[...]
        dx = w[k - 1:k] * dyt
        rows = [jnp.sum(dyt * _shift_down(xx, k - 1 - j, tm), 0, keepdims=True) for j in range(k)]
        for j in range(k - 1):
            dx = dx + w[j:j + 1] * _shift_up(dyy, k - 1 - j, tm)
        dx_ref[...] = dx.astype(BF16)
        dw = jnp.concatenate(rows + [jnp.zeros((SUBLANES - k, CONV_CB), F32)], axis=0)
        db = jnp.sum(dyt, 0, keepdims=True)

        @pl.when(i == 0)
        def _():
            dw_ref[...] = dw
            db_ref[...] = db

        @pl.when(i > 0)
        def _():
            dw_ref[...] += dw
            db_ref[...] += db

    dx, dw, db = pl.pallas_call(
        kern, name=name, grid=(c // CONV_CB, nt),
        in_specs=[pl.BlockSpec((tm, CONV_CB), lambda j, i: (i, j)),
                  pl.BlockSpec((HALO, CONV_CB), lambda j, i: (nh(i), j)),
                  pl.BlockSpec((tm, CONV_CB), lambda j, i: (i, x_cb0 + j)),
                  pl.BlockSpec((HALO, CONV_CB), lambda j, i: (ph(i), x_cb0 + j)),
                  pl.BlockSpec((SUBLANES, CONV_CB), lambda j, i: (0, j))],
        out_specs=[pl.BlockSpec((tm, CONV_CB), lambda j, i: (i, j)),
                   pl.BlockSpec((SUBLANES, CONV_CB), lambda j, i: (0, j)),
                   pl.BlockSpec((1, CONV_CB), lambda j, i: (0, j))],
        out_shape=[jax.ShapeDtypeStruct((s, c), BF16), jax.ShapeDtypeStruct((SUBLANES, c), F32),
                   jax.ShapeDtypeStruct((1, c), F32)],
        compiler_params=_cp(("parallel", "arbitrary")),
    )(dy, dy, x, x, w8)
    return dx, dw, db[0]


def _silu(x):
    return x * jax.nn.sigmoid(x)


def _ffn_gate_fwd(hu, w8, b, name, tm=2048):
    s = hu.shape[0]
    tm = min(tm, s)
    nb = FFN_DIM // CONV_CB
    ph = _prev_halo(tm)
    k = 3

    rc = min(GATE_ROWS, tm)

    def conv(x_ref, h_ref, w, b, r0):
        if r0 == 0:
            halo = jnp.where(pl.program_id(0) == 0, 0.0, h_ref[...].astype(F32))
            xx = jnp.concatenate([halo, x_ref[0:rc, :].astype(F32)], axis=0)
        else:
            xx = x_ref[r0 - HALO:r0 + rc, :].astype(F32)
        acc = b + w[k - 1:k] * xx[HALO:]
        for j in range(k - 1):
            acc = acc + w[j:j + 1] * _shift_down(xx, k - 1 - j, rc)
        return acc

    def kern(g_ref, gh_ref, u_ref, uh_ref, wg_ref, wu_ref, bg_ref, bu_ref, a_ref, hg_ref, hu_ref):
        wg, wu, bg, bu = wg_ref[...], wu_ref[...], bg_ref[...], bu_ref[...]
        for c in range(tm // rc):
            r0 = c * rc
            g = conv(g_ref, gh_ref, wg, bg, r0)
            u = conv(u_ref, uh_ref, wu, bu, r0)
            a_ref[r0:r0 + rc, :] = (_silu(g) * u).astype(BF16)
            hg_ref[r0:r0 + rc, :] = g.astype(BF16)
            hu_ref[r0:r0 + rc, :] = u.astype(BF16)

    main = lambda off: pl.BlockSpec((tm, CONV_CB), lambda i, j: (i, off + j))
    halo = lambda off: pl.BlockSpec((HALO, CONV_CB), lambda i, j: (ph(i), off + j))
    wsp = lambda off: pl.BlockSpec((SUBLANES, CONV_CB), lambda i, j: (0, off + j))
    bsp = lambda off: pl.BlockSpec((1, CONV_CB), lambda i, j: (0, off + j))
    b2 = b.reshape(1, 2 * FFN_DIM)
    half = jax.ShapeDtypeStruct((s, FFN_DIM), BF16)
    return pl.pallas_call(
        kern, name=name, grid=(s // tm, nb),
        in_specs=[main(0), halo(0), main(nb), halo(nb), wsp(0), wsp(nb), bsp(0), bsp(nb)],
        out_specs=[main(0), main(0), main(0)], out_shape=[half, half, half],
        compiler_params=_cp(("parallel", "parallel")),
    )(hu, hu, hu, hu, w8, w8, b2, b2)


GATE_ROWS = 128


def _ffn_gate_bwd(hu, hg, hv, da, w8, name, tm=2048):
    s = hu.shape[0]
    tm = min(tm, s)
    nt = s // tm
    nb = FFN_DIM // CONV_CB
    nh = lambda i: jnp.minimum((i + 1) * (tm // HALO), s // HALO - 1)
    k = 3

    rc = min(GATE_ROWS, tm)

    def kern(xg_ref, xu_ref, g_ref, gn_ref, u_ref, un_ref, da_ref, dan_ref, wg_ref, wu_ref,
             dg_ref, du_ref, dwg_ref, dwu_ref, dbg_ref, dbu_ref):
        i = pl.program_id(1)
        wg, wu = wg_ref[...], wu_ref[...]

        def back(dh, x, w):
            taps = [dh[0:rc] if j == k - 1 else _shift_up(dh, k - 1 - j, rc) for j in range(k)]
            dx = w[0:1] * taps[0]
            for j in range(1, k):
                dx = dx + w[j:j + 1] * taps[j]
            rows = [jnp.sum(taps[j] * x, 0, keepdims=True) for j in range(k)]
            dw = jnp.concatenate(rows + [jnp.zeros((SUBLANES - k, CONV_CB), F32)], axis=0)
            return dx, dw, jnp.sum(dh[0:rc], 0, keepdims=True)

        sums = None
        for c in range(tm // rc):
            r0, last = c * rc, c == tm // rc - 1

            def ld(ref, nref):
                if last:
                    return jnp.concatenate([ref[r0:, :].astype(F32), nref], axis=0)
                return ref[r0:r0 + rc + HALO, :].astype(F32)

            g, u = ld(g_ref, gn_ref[...].astype(F32)), ld(u_ref, un_ref[...].astype(F32))
            dae = ld(da_ref, jnp.where(i == nt - 1, 0.0, dan_ref[...].astype(F32)))
            sg = jax.nn.sigmoid(g)
            dhg = dae * u * (sg * (1.0 + g * (1.0 - sg)))
            dhu = dae * (g * sg)
            dxg, dwg, dbg = back(dhg, xg_ref[r0:r0 + rc, :].astype(F32), wg)
            dxu, dwu, dbu = back(dhu, xu_ref[r0:r0 + rc, :].astype(F32), wu)
            dg_ref[r0:r0 + rc, :] = dxg.astype(BF16)
            du_ref[r0:r0 + rc, :] = dxu.astype(BF16)
            part = (dwg, dwu, dbg, dbu)
            sums = part if sums is None else tuple(a + b for a, b in zip(sums, part))
        dwg, dwu, dbg, dbu = sums

        @pl.when(i == 0)
        def _():
            dwg_ref[...] = dwg
            dwu_ref[...] = dwu
            dbg_ref[...] = dbg
            dbu_ref[...] = dbu

        @pl.when(i > 0)
        def _():
            dwg_ref[...] += dwg
            dwu_ref[...] += dwu
            dbg_ref[...] += dbg
            dbu_ref[...] += dbu

    main = lambda off: pl.BlockSpec((tm, CONV_CB), lambda j, i: (i, off + j))
    nxt = lambda off: pl.BlockSpec((HALO, CONV_CB), lambda j, i: (nh(i), off + j))
    wsp = lambda off: pl.BlockSpec((SUBLANES, CONV_CB), lambda j, i: (0, off + j))
    bsp = lambda off: pl.BlockSpec((1, CONV_CB), lambda j, i: (0, off + j))
    outs = pl.pallas_call(
        kern, name=name, grid=(nb, nt),
        in_specs=[main(0), main(nb), main(0), nxt(0), main(0), nxt(0), main(0), nxt(0), wsp(0), wsp(nb)],
        out_specs=[main(0), main(0), wsp(0), wsp(0), bsp(0), bsp(0)],
        out_shape=[jax.ShapeDtypeStruct((s, FFN_DIM), BF16), jax.ShapeDtypeStruct((s, FFN_DIM), BF16),
                   jax.ShapeDtypeStruct((SUBLANES, FFN_DIM), F32), jax.ShapeDtypeStruct((SUBLANES, FFN_DIM), F32),
                   jax.ShapeDtypeStruct((1, FFN_DIM), F32), jax.ShapeDtypeStruct((1, FFN_DIM), F32)],
        compiler_params=_cp(("parallel", "arbitrary")),
    )(hu, hu, hg, hg, hv, hv, da, da, w8, w8)
    dg, du, dwg, dwu, dbg, dbu = outs
    return dg, du, jnp.concatenate([dwg, dwu], axis=1), jnp.concatenate([dbg[0], dbu[0]])


def _iota(shape, dim):
    return lax.broadcasted_iota(jnp.int32, shape, dim)


def _in_range(idx, start, size):
    return jnp.logical_and(idx >= start, idx < start + size)


@functools.partial(jax.custom_vjp, nondiff_argnums=(2, 3))
def _zdot(a, b, dims, z):
    zz, x = ((a, b) if z == 0 else (b, a))
    zz = zz.astype(BF16)
    xh = x.astype(BF16)
    r1 = x - xh.astype(F32)
    xm = r1.astype(BF16)
    xl = (r1 - xm.astype(F32)).astype(BF16)
    f = (lambda p: _dot(zz, p, dims)) if z == 0 else (lambda p: _dot(p, zz, dims))
    return f(xh) + f(xm) + f(xl)


def _zdot_fwd(a, b, dims, z):
    return _zdot(a, b, dims, z), (a if z == 0 else b)


def _zdot_bwd(dims, z, zz, ct):
    zb = zz.astype(BF16)
    ch = ct.astype(BF16)
    cm = (ct - ch.astype(F32)).astype(BF16)
    if dims == NN and z == 0:
        f = lambda c: _dot(zb, c, TN)
    elif dims == NN and z == 1:
        f = lambda c: _dot(c, zb, NT)
    else:
        f = lambda c: _dot(c, zb, TN)
    dx = f(ch) + f(cm)
    return (jnp.zeros_like(zz), dx) if z == 0 else (dx, jnp.zeros_like(zz))


_zdot.defvjp(_zdot_fwd, _zdot_bwd)


def _expand8(v, e):
    return jnp.sum(_zdot(jnp.broadcast_to(v, (SUBLANES, v.shape[1])), e, NN, 1), 0, keepdims=True) * (1.0 / SUBLANES)


def _ssd_consts():
    tri = np.tril(np.ones((SSD_CHUNK, SSD_CHUNK), np.float32))
    e = np.zeros((LANES, SSD_D_INNER), np.float32)
    for h in range(SSD_HEADS):
        e[h, 64 * h:64 * h + 64] = 1.0
    bd = np.zeros((LANES, SSD_D_INNER), np.float32)
    bd[:64, :256] = 1.0
    bd[64:, 256:] = 1.0
    return jnp.asarray(tri), jnp.asarray(e), jnp.asarray(bd)


def _ssd_chunk(tri, e, bd, z, xs_p, bm_p, cm_p, dtr, st, dtb, alog, dsk, nw):
    n = z.shape[0]
    lane128, lane512 = _iota((1, LANES), 1), _iota((1, SSD_D_INNER), 1)
    sub8 = _iota((SUBLANES, 1), 0)
    tril = _iota((n, 1), 0) >= _iota((1, n), 1)
    head_rows = jnp.where(sub8 == lane128, 1.0, 0.0)
    xs, bm, cm = _silu(xs_p), _silu(bm_p), _silu(cm_p)
    dt = jax.nn.softplus(dtr + dtb)
    da = dt * (-jnp.exp(alog))
    acs = _zdot(tri, da, NN, 0)
    acst = _zdot(head_rows, acs, NT, 0)
    xc = xs * _zdot(dt, e, NN, 1)
    tot = jnp.sum(da, 0, keepdims=True)
    y = _bdot(cm, st) * _zdot(jnp.exp(acs), e, NN, 1)
    st_new = st * _expand8(jnp.exp(tot), e) + bd * _bdot(bm, xc * _zdot(jnp.exp(tot - acs), e, NN, 1), TN)
    for g in range(2):
        cb = _bdot(jnp.where(_in_range(lane128, 64 * g, 64), cm, 0.0), bm, NT)
        for h in range(4 * g, 4 * g + 4):
            col = jnp.sum(jnp.where(lane128 == h, acs, 0.0), 1, keepdims=True)
            row = jnp.sum(jnp.where(sub8 == h, acst, 0.0), 0, keepdims=True)
            dm = jnp.where(tril, jnp.exp(jnp.minimum(col - row, 0.0)), 0.0)
            y = y + _bdot(cb * dm, jnp.where(_in_range(lane512, 64 * h, 64), xc, 0.0))
    y = (y + _expand8(dsk, e) * xs) * _silu(z)
    ysq = y * y
    g0 = lane512 < 256
    ms0 = jnp.sum(jnp.where(g0, ysq, 0.0), 1, keepdims=True) * (1.0 / 256)
    ms1 = jnp.sum(jnp.where(g0, 0.0, ysq), 1, keepdims=True) * (1.0 / 256)
    rs = jnp.where(g0, lax.rsqrt(ms0 + RMS_EPS), lax.rsqrt(ms1 + RMS_EPS))
    return y * rs * nw, st_new


SSD_STEP = 4


def _ssd_rows(u):
    return slice(SSD_CHUNK * u, SSD_CHUNK * (u + 1))


def _ssd_specs(rev, ng):
    ci = (lambda i: ng - 1 - i) if rev else (lambda i: i)
    n = SSD_STEP * SSD_CHUNK
    act = [pl.BlockSpec((n, 512), lambda i: (ci(i), AB_Z // 512)),
           pl.BlockSpec((n, 512), lambda i: (ci(i), 0)),
           pl.BlockSpec((n, LANES), lambda i: (ci(i), 4)),
           pl.BlockSpec((n, LANES), lambda i: (ci(i), 5)),
           pl.BlockSpec((n, LANES), lambda i: (ci(i), AB_DT // LANES))]
    const = [pl.BlockSpec((SSD_CHUNK, SSD_CHUNK), lambda i: (0, 0)), pl.BlockSpec((LANES, 512), lambda i: (0, 0)),
             pl.BlockSpec((LANES, 512), lambda i: (0, 0))]
    par = [pl.BlockSpec((1, LANES), lambda i: (0, 0))] * 3 + [pl.BlockSpec((1, 512), lambda i: (0, 0))]
    return ci, act, const, par


def _ssd_fwd(h, xbcc, dtb, alog, dsk, nw, name):
    s = h.shape[0]
    nc = s // SSD_CHUNK
    ng = nc // SSD_STEP
    ci, act, const, par = _ssd_specs(False, ng)

    def kern(z_ref, xs_ref, bm_ref, cm_ref, dt_ref, tri_ref, e_ref, bd_ref, dtb_ref, al_ref, dsk_ref, nw_ref,
             y_ref, sts_ref, st):
        @pl.when(pl.program_id(0) == 0)
        def _():
            st[...] = jnp.zeros_like(st)

        for u in range(SSD_STEP):
            rows = _ssd_rows(u)
            sts_ref[u] = st[...]
            y, stn = _ssd_chunk(tri_ref[...], e_ref[...], bd_ref[...], z_ref[rows, :], xs_ref[rows, :], bm_ref[rows, :],
                                cm_ref[rows, :], dt_ref[rows, :], st[...], dtb_ref[...], al_ref[...], dsk_ref[...], nw_ref[...])
            y_ref[rows, :] = y.astype(BF16)
            st[...] = stn

    return pl.pallas_call(
        kern, name=name, grid=(ng,), in_specs=act + const + par,
        out_specs=[pl.BlockSpec((SSD_STEP * SSD_CHUNK, 512), lambda i: (i, 0)),
                   pl.BlockSpec((SSD_STEP, LANES, 512), lambda i: (i, 0, 0))],
        out_shape=[jax.ShapeDtypeStruct((s, 512), BF16), jax.ShapeDtypeStruct((nc, LANES, 512), F32)],
        scratch_shapes=[pltpu.VMEM((LANES, 512), F32)], compiler_params=_cp(("arbitrary",)),
    )(h, xbcc, xbcc, xbcc, h, *_ssd_consts(), dtb, alog, dsk, nw)


def _acc(ref, val, first):
    @pl.when(first)
    def _():
        ref[...] = val

    @pl.when(jnp.logical_not(first))
    def _():
        ref[...] += val


def _ssd_bwd(dy, dy_cb, h, xbcc, sts, dtb, alog, dsk, nw, name):
    s = h.shape[0]
    ng = s // SSD_CHUNK // SSD_STEP
    n = SSD_STEP * SSD_CHUNK
    ci, act, const, par = _ssd_specs(True, ng)

    def kern(dy_ref, sts_ref, z_ref, xs_ref, bm_ref, cm_ref, dt_ref, tri_ref, e_ref, bd_ref, dtb_ref, al_ref, dsk_ref,
             nw_ref, dz_ref, dx_ref, ddt_ref, ddtb_ref, dal_ref, ddsk_ref, dnw_ref, dst):
        @pl.when(pl.program_id(0) == 0)
        def _():
            dst[...] = jnp.zeros_like(dst)
            for ref in (ddtb_ref, dal_ref, ddsk_ref, dnw_ref):
                ref[...] = jnp.zeros_like(ref)

        f = functools.partial(_ssd_chunk, tri_ref[...], e_ref[...], bd_ref[...])
        for u in range(SSD_STEP - 1, -1, -1):
            rows = _ssd_rows(u)
            _, pull = jax.vjp(f, z_ref[rows, :], xs_ref[rows, :], bm_ref[rows, :], cm_ref[rows, :], dt_ref[rows, :],
                              sts_ref[u], dtb_ref[...], al_ref[...], dsk_ref[...], nw_ref[...])
            dz, dxs, dbm, dcm, ddt, dsti, ddtb, dal, ddsk, dnw = pull((dy_ref[rows, :], dst[...]))
            dz_ref[rows, :] = dz.astype(BF16)
            dx_ref[rows, 0:512] = dxs
            dx_ref[rows, 512:640] = dbm
            dx_ref[rows, 640:768] = dcm
            ddt_ref[rows, :] = ddt.astype(BF16)
            dst[...] = dsti
            ddtb_ref[...] += ddtb
            dal_ref[...] += dal
            ddsk_ref[...] += ddsk
            dnw_ref[...] += dnw

    vec = pl.BlockSpec((1, LANES), lambda i: (0, 0))
    return pl.pallas_call(
        kern, name=name, grid=(ng,),
        in_specs=[pl.BlockSpec((n, 512), lambda i: (ci(i), dy_cb)), pl.BlockSpec((SSD_STEP, LANES, 512), lambda i: (ci(i), 0, 0))]
        + act + const + par,
        out_specs=[pl.BlockSpec((n, 512), lambda i: (ci(i), 0)), pl.BlockSpec((n, SSD_CONV_DIM), lambda i: (ci(i), 0)),
                   pl.BlockSpec((n, LANES), lambda i: (ci(i), 0)), vec, vec, vec, pl.BlockSpec((1, 512), lambda i: (0, 0))],
        out_shape=[jax.ShapeDtypeStruct((s, 512), BF16), jax.ShapeDtypeStruct((s, SSD_CONV_DIM), F32),
                   jax.ShapeDtypeStruct((s, LANES), BF16), jax.ShapeDtypeStruct((1, LANES), F32),
                   jax.ShapeDtypeStruct((1, LANES), F32), jax.ShapeDtypeStruct((1, LANES), F32),
                   jax.ShapeDtypeStruct((1, 512), F32)],
        scratch_shapes=[pltpu.VMEM((LANES, 512), F32)], compiler_params=_cp(("arbitrary",)),
    )(dy, sts, h, xbcc, xbcc, xbcc, h, *_ssd_consts(), dtb, alog, dsk, nw)


HG_LEVELS = 6


HG_SUMS = 1 + 2 * HG_LEVELS


def _hg_consts():
    n = HG_CHUNK
    t = np.arange(n)
    mats = [np.tril(np.ones((n, n), np.float32))]
    dq, dk, mk = [], [], []
    for lv in range(HG_LEVELS):
        b = 1 << lv
        start = (t // b) * b
        end = start + b - 1
        dq.append(((t[None, :] >= start[:, None]) & (t[None, :] <= t[:, None])).astype(np.float32))
        dk.append(((t[None, :] > t[:, None]) & (t[None, :] <= end[:, None])).astype(np.float32))
        mk.append((((t // b) % 2 == 1)[:, None] & ((t // b)[None, :] == (t // b)[:, None] - 1)).astype(np.float32))
    c1 = np.concatenate(mats + dq + dk, axis=0)
    return (jnp.asarray(c1, BF16), jnp.asarray(np.concatenate([c1, c1, c1], axis=1), BF16), jnp.asarray(np.stack(mk)))


@functools.partial(jax.custom_vjp, nondiff_argnums=(1, 2))
def _split_grid(x, nr, nc):
    br, bc = x.shape[0] // nr, x.shape[1] // nc
    return tuple(x[i * br:(i + 1) * br, j * bc:(j + 1) * bc] for i in range(nr) for j in range(nc))


def _split_grid_fwd(x, nr, nc):
    return _split_grid(x, nr, nc), None


def _split_grid_bwd(nr, nc, _, cts):
    rows = [cts[i * nc] if nc == 1 else jnp.concatenate(cts[i * nc:(i + 1) * nc], axis=1) for i in range(nr)]
    return (rows[0] if nr == 1 else jnp.concatenate(rows, axis=0),)


_split_grid.defvjp(_split_grid_fwd, _split_grid_bwd)


@jax.custom_vjp
def _level_sums(c1, c3, g):
    gh = g.astype(BF16)
    r1 = g - gh.astype(F32)
    gm = r1.astype(BF16)
    gl = (r1 - gm.astype(F32)).astype(BF16)
    return _dot(c3, jnp.concatenate([gh, gm, gl], axis=0), NN)


def _level_sums_fwd(c1, c3, g):
    return _level_sums(c1, c3, g), (c1, c3)


def _level_sums_bwd(res, ct):
    c1, c3 = res
    ch = ct.astype(BF16)
    cm = (ct - ch.astype(F32)).astype(BF16)
    return jnp.zeros_like(c1), jnp.zeros_like(c3), _dot(c1, ch, TN) + _dot(c1, cm, TN)


_level_sums.defvjp(_level_sums_fwd, _level_sums_bwd)


def _hg_chunk(c1, c3, masks, hq, hf, hi, hg, sts, lb, nw):
    q = _silu(hq)
    g = jnp.log(lb + (1.0 - lb) * jax.nn.sigmoid(hf))
    k = (1.0 - lb) * jax.nn.sigmoid(-hf)
    parts = _split_grid(_level_sums(c1, c3, g), HG_SUMS, 1)
    bc = parts[0]
    tot = jnp.sum(g, 0, keepdims=True)
    qe = _split_grid(q * jnp.exp(bc), 1, HG_HEADS)
    kd = _split_grid(k * jnp.exp(tot - bc), 1, HG_HEADS)
    etot = _split_grid(jnp.exp(tot), 1, HG_HEADS)
    qh, kh, vh, gh = [_split_grid(a, 1, HG_HEADS) for a in (q, k, hi, hg)]
    att = [None] * HG_HEADS
    for lv in range(HG_LEVELS):
        qt = _split_grid(q * jnp.exp(parts[1 + lv]), 1, HG_HEADS)
        kt = _split_grid(k * jnp.exp(parts[1 + HG_LEVELS + lv]), 1, HG_HEADS)
        for hd in range(HG_HEADS):
            t = masks[lv] * _bdot(qt[hd], kt[hd], NT)
            att[hd] = t if att[hd] is None else att[hd] + t
    outs, stn = [], []
    for hd in range(HG_HEADS):
        o = (_bdot(att[hd], vh[hd]) + jnp.sum(qh[hd] * kh[hd], 1, keepdims=True) * vh[hd] + _bdot(qe[hd], sts[hd], NT))
        stn.append(sts[hd] * etot[hd] + _bdot(vh[hd], kd[hd], TN))
        outs.append(o * lax.rsqrt(jnp.mean(o * o, 1, keepdims=True) + RMS_EPS) * nw * _silu(gh[hd]))
    return tuple(outs), tuple(stn)


HG_STEP = 4


def _hg_specs(rev, ng):
    ci = (lambda i: ng - 1 - i) if rev else (lambda i: i)
    n = HG_CHUNK
    act = [pl.BlockSpec((HG_STEP * n, 512), functools.partial(lambda i, c: (ci(i), c), c=c))
           for c in (AB_HQ // 512, AB_HF // 512, AB_HI // 512, AB_HG // 512)]
    other = [pl.BlockSpec((HG_SUMS * n, n), lambda i: (0, 0)), pl.BlockSpec((HG_SUMS * n, 3 * n), lambda i: (0, 0)),
             pl.BlockSpec((HG_LEVELS, n, n), lambda i: (0, 0, 0)), pl.BlockSpec((1, 512), lambda i: (0, 0)),
             pl.BlockSpec((1, LANES), lambda i: (0, 0))]
    return ci, act, other


def _head_rows(hd):
    return slice(LANES * hd, LANES * (hd + 1))


def _chunk_rows(u):
    return slice(HG_CHUNK * u, HG_CHUNK * (u + 1))


def _hg_fwd(h, lb, nw, name):
    s = h.shape[0]
    nc = s // HG_CHUNK
    ng = nc // HG_STEP
    ci, act, other = _hg_specs(False, ng)

    def kern(q_ref, f_ref, i_ref, g_ref, c1_ref, c3_ref, mk_ref, lb_ref, nw_ref, y_ref, sts_ref, st):
        @pl.when(pl.program_id(0) == 0)
        def _():
            st[...] = jnp.zeros_like(st)

        masks = [mk_ref[lv] for lv in range(HG_LEVELS)]
        for u in range(HG_STEP):
            rows = _chunk_rows(u)
            sts_ref[u] = st[...]
            outs, stn = _hg_chunk(c1_ref[...], c3_ref[...], masks, q_ref[rows, :], f_ref[rows, :], i_ref[rows, :],
                                  g_ref[rows, :], tuple(st[_head_rows(hd), :] for hd in range(HG_HEADS)), lb_ref[...],
                                  nw_ref[...])
            for hd in range(HG_HEADS):
                y_ref[rows, _head_rows(hd)] = outs[hd].astype(BF16)
                st[_head_rows(hd), :] = stn[hd]

    return pl.pallas_call(
        kern, name=name, grid=(ng,), in_specs=act + other,
        out_specs=[pl.BlockSpec((HG_STEP * HG_CHUNK, 512), lambda i: (i, 0)),
                   pl.BlockSpec((HG_STEP, 512, LANES), lambda i: (i, 0, 0))],
        out_shape=[jax.ShapeDtypeStruct((s, 512), BF16), jax.ShapeDtypeStruct((nc, 512, LANES), F32)],
        scratch_shapes=[pltpu.VMEM((512, LANES), F32)], compiler_params=_cp(("arbitrary",)),
    )(h, h, h, h, *_hg_consts(), lb, nw)


def _hg_bwd(dy, dy_cb, h, sts, lb, nw, name):
    s = h.shape[0]
    ng = s // HG_CHUNK // HG_STEP
    n = HG_STEP * HG_CHUNK
    ci, act, other = _hg_specs(True, ng)

    def kern(dy_ref, sts_ref, q_ref, f_ref, i_ref, g_ref, c1_ref, c3_ref, mk_ref, lb_ref, nw_ref,
             dq_ref, df_ref, di_ref, dg_ref, dlb_ref, dnw_ref, dst):
        @pl.when(pl.program_id(0) == 0)
        def _():
            dst[...] = jnp.zeros_like(dst)
            dlb_ref[...] = jnp.zeros_like(dlb_ref)
            dnw_ref[...] = jnp.zeros_like(dnw_ref)

        masks = [mk_ref[lv] for lv in range(HG_LEVELS)]
        heads = range(HG_HEADS)
        for u in range(HG_STEP - 1, -1, -1):
            rows = _chunk_rows(u)
            _, pull = jax.vjp(functools.partial(_hg_chunk, c1_ref[...], c3_ref[...], masks), q_ref[rows, :], f_ref[rows, :],
                              i_ref[rows, :], g_ref[rows, :], tuple(sts_ref[u, _head_rows(hd), :] for hd in heads),
                              lb_ref[...], nw_ref[...])
            dq, df, di, dg, dsti, dlb, dnw = pull((tuple(dy_ref[rows, _head_rows(hd)] for hd in heads),
                                                   tuple(dst[_head_rows(hd), :] for hd in heads)))
            dq_ref[rows, :] = dq.astype(BF16)
            df_ref[rows, :] = df.astype(BF16)
            di_ref[rows, :] = di.astype(BF16)
            dg_ref[rows, :] = dg.astype(BF16)
            for hd in heads:
                dst[_head_rows(hd), :] = dsti[hd]
            dlb_ref[...] += dlb
            dnw_ref[...] += dnw

    blk = pl.BlockSpec((n, 512), lambda i: (ci(i), 0))
    return pl.pallas_call(
        kern, name=name, grid=(ng,),
        in_specs=[pl.BlockSpec((n, 512), lambda i: (ci(i), dy_cb)), pl.BlockSpec((HG_STEP, 512, LANES), lambda i: (ci(i), 0, 0))]
        + act + other,
        out_specs=[blk, blk, blk, blk, pl.BlockSpec((1, 512), lambda i: (0, 0)), pl.BlockSpec((1, LANES), lambda i: (0, 0))],
        out_shape=[jax.ShapeDtypeStruct((s, 512), BF16)] * 4 + [jax.ShapeDtypeStruct((1, 512), F32),
                                                                jax.ShapeDtypeStruct((1, LANES), F32)],
        scratch_shapes=[pltpu.VMEM((512, LANES), F32)], compiler_params=_cp(("arbitrary",)),
    )(dy, sts, h, h, h, h, *_hg_consts(), lb, nw)


def _swa_consts():
    n = SWA_BLOCK
    r = np.arange(8 * n)
    t, blk = r % n, r // n
    lanes = np.arange(LANES)
    own = (t[:, None] >= lanes[None, :]).astype(np.float32)
    prev = (lanes[None, :] > t[:, None]).astype(np.float32)
    sel = (lanes[None, :] == (blk // 2 + 4 * (blk % 2))[:, None]).astype(np.float32)
    grp = ((lanes[None, :] // 64) == (blk % 2)[:, None]).astype(np.float32)
    return tuple(jnp.asarray(a) for a in (own, prev, sel, grp))


def _swa_block(own, prev, sel, grp, q, k0, v0, k1, v1, srow):
    qs = _split_grid(q, 1, 4)
    qa = jnp.concatenate([qs[p] for p in range(4) for _ in range(2)], axis=0) * grp
    s0 = jnp.where(prev > 0.0, _bdot(qa, k0, NT) * 0.125, MASK_VALUE)
    s1 = jnp.where(own > 0.0, _bdot(qa, k1, NT) * 0.125, MASK_VALUE)
    sk = jnp.sum(sel * srow, 1, keepdims=True)
    m = lax.stop_gradient(jnp.maximum(jnp.maximum(jnp.max(s0, 1, keepdims=True), jnp.max(s1, 1, keepdims=True)), sk))
    p0, p1 = jnp.exp(s0 - m), jnp.exp(s1 - m)
    inv = 1.0 / (jnp.sum(p0, 1, keepdims=True) + jnp.sum(p1, 1, keepdims=True) + jnp.exp(sk - m))
    o = _split_grid((_bdot(p0 * inv, v0) + _bdot(p1 * inv, v1)) * grp, 8, 1)
    return tuple(o[2 * p] + o[2 * p + 1] for p in range(4))


SWA_STEP = 4


def _swa_specs():
    n = SWA_BLOCK
    pv = lambda i: jnp.maximum(SWA_STEP * i - 1, 0)
    mask = pl.BlockSpec((8 * n, LANES), lambda i: (0, 0))
    return [mask, mask, mask, mask, pl.BlockSpec((SWA_STEP * n, 512), lambda i: (i, CD_Q // 512)),
            pl.BlockSpec((n, LANES), lambda i: (pv(i), CD_K // LANES)), pl.BlockSpec((n, LANES), lambda i: (pv(i), CD_V // LANES)),
            pl.BlockSpec((SWA_STEP * n, LANES), lambda i: (i, CD_K // LANES)),
            pl.BlockSpec((SWA_STEP * n, LANES), lambda i: (i, CD_V // LANES)), pl.BlockSpec((1, LANES), lambda i: (0, 0))]


def _swa_args(refs, u, notfirst):
    own, prev, sel, grp = (r[...] for r in refs[:4])
    q_ref, k0_ref, v0_ref, k_ref, v_ref, s_ref = refs[4:]
    rows = slice(SWA_BLOCK * u, SWA_BLOCK * (u + 1))
    if u == 0:
        k0, v0, prev = k0_ref[...], v0_ref[...], prev * notfirst.astype(F32)
    else:
        before = slice(SWA_BLOCK * (u - 1), SWA_BLOCK * u)
        k0, v0 = k_ref[before, :], v_ref[before, :]
    return (own, prev, sel, grp), (q_ref[rows, :], k0, v0, k_ref[rows, :], v_ref[rows, :], s_ref[...])


def _swa_fwd(hc, srow, name):
    s = hc.shape[0]
    n = SWA_STEP * SWA_BLOCK

    def kern(*refs):
        y_ref = refs[10]
        for u in range(SWA_STEP):
            consts, acts = _swa_args(refs[:10], u, pl.program_id(0) > 0)
            outs = _swa_block(*consts, *acts)
            for p in range(4):
                y_ref[SWA_BLOCK * u:SWA_BLOCK * (u + 1), LANES * p:LANES * (p + 1)] = outs[p].astype(BF16)

    return pl.pallas_call(
        kern, name=name, grid=(s // n,), in_specs=_swa_specs(), out_specs=pl.BlockSpec((n, 512), lambda i: (i, 0)),
        out_shape=jax.ShapeDtypeStruct((s, 512), BF16), compiler_params=_cp(("parallel",)),
    )(*_swa_consts(), hc, hc, hc, hc, hc, srow)


def _swa_bwd(dy, dy_cb, hc, srow, name):
    s = hc.shape[0]
    n = SWA_STEP * SWA_BLOCK
    nb = SWA_BLOCK

    def kern(dy_ref, *refs):
        dq_ref, dk_ref, dv_ref, ds_ref = refs[10:]
        i = pl.program_id(0)

        @pl.when(i == 0)
        def _():
            dk_ref[...] = jnp.zeros_like(dk_ref)
            dv_ref[...] = jnp.zeros_like(dv_ref)
            ds_ref[...] = jnp.zeros_like(ds_ref)

        for u in range(SWA_STEP):
            blk = SWA_STEP * i + u
            rows = slice(nb * u, nb * (u + 1))
            r1 = pl.ds(pl.multiple_of(blk * nb, nb), nb)
            r0 = pl.ds(pl.multiple_of(jnp.maximum(blk - 1, 0) * nb, nb), nb)
            consts, acts = _swa_args(refs[:10], u, i > 0)
            _, pull = jax.vjp(functools.partial(_swa_block, *consts), *acts)
            dq, dk0, dv0, dk1, dv1, dsr = pull(tuple(dy_ref[rows, LANES * p:LANES * (p + 1)] for p in range(4)))
            dq_ref[rows, :] = dq.astype(BF16)
            dk_ref[r0, :] += dk0
            dv_ref[r0, :] += dv0
            dk_ref[r1, :] += dk1
            dv_ref[r1, :] += dv1
            ds_ref[...] += dsr

    full = pl.BlockSpec((s, LANES), lambda i: (0, 0))
    return pl.pallas_call(
        kern, name=name, grid=(s // n,), in_specs=[pl.BlockSpec((n, 512), lambda i: (i, dy_cb))] + _swa_specs(),
        out_specs=[pl.BlockSpec((n, 512), lambda i: (i, 0)), full, full, pl.BlockSpec((1, LANES), lambda i: (0, 0))],
        out_shape=[jax.ShapeDtypeStruct((s, 512), BF16), jax.ShapeDtypeStruct((s, LANES), F32),
                   jax.ShapeDtypeStruct((s, LANES), F32), jax.ShapeDtypeStruct((1, LANES), F32)],
        compiler_params=_cp(("arbitrary",)),
    )(dy, *_swa_consts(), hc, hc, hc, hc, hc, srow)


RG_TILE = 512


def _neg_expm1(x):
    ser = x * (1 + x / 2 * (1 + x / 3 * (1 + x / 4 * (1 + x / 5 * (1 + x / 6 * (1 + x / 7 * (1 + x / 8)))))))
    return jnp.where(x > -0.25, -ser, 1.0 - jnp.exp(x))


def _rg_gates(xc, wa, wx, ba, bx, lam):
    r = jax.nn.sigmoid(_bdot(xc, wa) + ba)
    i = jax.nn.sigmoid(_bdot(xc, wx) + bx)
    log_a = -RG_C * r * jax.nn.softplus(-lam)
    u = jnp.sqrt(jnp.maximum(_neg_expm1(2.0 * log_a), 0.0)) * (i * xc)
    return jnp.exp(log_a), u


def _rg_out(hs, gate):
    return hs * jax.nn.gelu(gate)


def _rows_to_tile(rows):
    sub = _iota((SUBLANES, 1), 0)
    out = jnp.broadcast_to(rows[0], (SUBLANES, rows[0].shape[1]))
    for j in range(1, SUBLANES):
        out = jnp.where(sub == j, rows[j], out)
    return out


def _rg_fwd(xc, hc, wa, wx, ba, bx, lam, name):
    s = xc.shape[0]
    tm = min(RG_TILE, s)

    def kern(x_ref, g_ref, wa_ref, wx_ref, ba_ref, bx_ref, lam_ref, y_ref, h_ref, a_s, u_s, carry):
        @pl.when(pl.program_id(0) == 0)
        def _():
            carry[...] = jnp.zeros_like(carry)

        a, u = _rg_gates(x_ref[...], wa_ref[...], wx_ref[...], ba_ref[...], bx_ref[...], lam_ref[...])
        a_s[...] = a
        u_s[...] = u

        def body(gi, hp):
            r0 = pl.multiple_of(gi * SUBLANES, SUBLANES)
            ab, ub = a_s[pl.ds(r0, SUBLANES), :], u_s[pl.ds(r0, SUBLANES), :]
            rows = []
            for j in range(SUBLANES):
                hp = ab[j:j + 1] * hp + ub[j:j + 1]
                rows.append(hp)
            h_ref[pl.ds(r0, SUBLANES), :] = _rows_to_tile(rows)
            return hp

        carry[0:1, :] = lax.fori_loop(0, tm // SUBLANES, body, carry[0:1, :])
        y_ref[...] = _rg_out(h_ref[...], g_ref[...]).astype(BF16)

    row = pl.BlockSpec((tm, 512), lambda i: (i, 0))
    mat = pl.BlockSpec((512, 512), lambda i: (0, 0))
    vec = pl.BlockSpec((1, 512), lambda i: (0, 0))
    return pl.pallas_call(
        kern, name=name, grid=(s // tm,),
        in_specs=[row, pl.BlockSpec((tm, 512), lambda i: (i, CD_GATE // 512)), mat, mat, vec, vec, vec],
        out_specs=[row, row],
        out_shape=[jax.ShapeDtypeStruct((s, 512), BF16), jax.ShapeDtypeStruct((s, 512), F32)],
        scratch_shapes=[pltpu.VMEM((tm, 512), F32), pltpu.VMEM((tm, 512), F32), pltpu.VMEM((SUBLANES, 512), F32)],
        compiler_params=_cp(("arbitrary",)),
    )(xc, hc, wa, wx, ba, bx, lam)


def _rg_bwd(dy, dy_cb, xc, hc, hs, wa, wx, ba, bx, lam, name):
    s = xc.shape[0]
    tm = min(RG_TILE, s)
    nt = s // tm
    ti = lambda i: nt - 1 - i

    def kern(dy_ref, x_ref, g_ref, h_ref, hp_ref, wa_ref, wx_ref, ba_ref, bx_ref, lam_ref,
             dx_ref, dg_ref, dwa_ref, dwx_ref, dba_ref, dbx_ref, dlam_ref, a_s, d_s, g_s, cg, ca):
        first = pl.program_id(0) == 0

        @pl.when(first)
        def _():
            cg[...] = jnp.zeros_like(cg)
            ca[...] = jnp.zeros_like(ca)

        (a, _), pull = jax.vjp(_rg_gates, x_ref[...], wa_ref[...].astype(F32), wx_ref[...].astype(F32), ba_ref[...],
                               bx_ref[...], lam_ref[...])
        hs_t = h_ref[...]
        _, pull_out = jax.vjp(_rg_out, hs_t, g_ref[...])
        dho, dgate = pull_out(dy_ref[...])
        dg_ref[...] = dgate.astype(BF16)
        a_s[...] = a
        d_s[...] = dho

        def body(k, c):
            gn, an = c
            r0 = pl.multiple_of((tm // SUBLANES - 1 - k) * SUBLANES, SUBLANES)
            ab, db = a_s[pl.ds(r0, SUBLANES), :], d_s[pl.ds(r0, SUBLANES), :]
            rows = [None] * SUBLANES
            for j in range(SUBLANES - 1, -1, -1):
                gn = db[j:j + 1] + an * gn
                an = ab[j:j + 1]
                rows[j] = gn
            g_s[pl.ds(r0, SUBLANES), :] = _rows_to_tile(rows)
            return gn, an

        gn, an = lax.fori_loop(0, tm // SUBLANES, body, (cg[0:1, :], ca[0:1, :]))
        cg[0:1, :] = gn
        ca[0:1, :] = an
        g = g_s[...]
        halo = jnp.where(pl.program_id(0) == nt - 1, 0.0, hp_ref[...])
        hprev = pltpu.roll(jnp.concatenate([halo, hs_t], axis=0), 1, axis=0)[SUBLANES:]
        dxc, dwa, dwx, dba, dbx, dlam = pull((g * hprev, g))
        dx_ref[...] = dxc
        _acc(dwa_ref, dwa, first)
        _acc(dwx_ref, dwx, first)
        _acc(dba_ref, dba, first)
        _acc(dbx_ref, dbx, first)
        _acc(dlam_ref, dlam, first)

    row = pl.BlockSpec((tm, 512), lambda i: (ti(i), 0))
    mat = pl.BlockSpec((512, 512), lambda i: (0, 0))
    vec = pl.BlockSpec((1, 512), lambda i: (0, 0))
    hprev = pl.BlockSpec((SUBLANES, 512), lambda i: (jnp.maximum(ti(i) * (tm // SUBLANES) - 1, 0), 0))
    return pl.pallas_call(
        kern, name=name, grid=(nt,),
        in_specs=[pl.BlockSpec((tm, 512), lambda i: (ti(i), dy_cb)), row,
                  pl.BlockSpec((tm, 512), lambda i: (ti(i), CD_GATE // 512)), row, hprev, mat, mat, vec, vec, vec],
        out_specs=[row, row, mat, mat, vec, vec, vec],
        out_shape=[jax.ShapeDtypeStruct((s, 512), F32), jax.ShapeDtypeStruct((s, 512), BF16),
                   jax.ShapeDtypeStruct((512, 512), F32), jax.ShapeDtypeStruct((512, 512), F32),
                   jax.ShapeDtypeStruct((1, 512), F32), jax.ShapeDtypeStruct((1, 512), F32), jax.ShapeDtypeStruct((1, 512), F32)],
        scratch_shapes=[pltpu.VMEM((tm, 512), F32)] * 3 + [pltpu.VMEM((SUBLANES, 512), F32)] * 2,
        compiler_params=_cp(("arbitrary",)),
    )(dy, xc, hc, hs, hs, wa, wx, ba, bx, lam)


AB_REF = dict(z=(0, 512), xbc=(512, 768), dt=(1280, 8), hq=(1288, 512), hf=(1800, 512), hi=(2312, 512), hg=(2824, 512))
AB_INT = dict(z=AB_Z, hq=AB_HQ, hf=AB_HF, hi=AB_HI, hg=AB_HG, xbc=AB_XBC, dt=AB_DT)
CD_REF = dict(q=(0, 512), k=(512, 128), v=(640, 128), gate=(768, 512), xr=(1280, 512))


def _cols(w, start, size):
    return lax.slice_in_dim(w, start, start + size, axis=w.ndim - 1)


def _pad_lanes(v):
    return jnp.concatenate([v.astype(F32), jnp.zeros((LANES - v.shape[0],), F32)]).reshape(1, LANES)


def _taker(g, layer, axis):
    width = g.shape[2 + axis]

    def take(start, size):
        out = []
        while size > 0:
            k, off = divmod(start, width)
            n = min(size, width - off)
            out.append(lax.slice_in_dim(g[k, layer], off, off + n, axis=axis))
            start, size = start + n, size - n
        return out

    return take


def _ab_in_to_int(g, layer):
    take = _taker(g, layer, 1)
    parts = sum([take(*AB_REF[k]) for k in ("z", "hq", "hf", "hi", "hg", "xbc", "dt")], [])
    return jnp.concatenate(parts + [jnp.zeros((g.shape[2], AB_COLS - AB_DT - 8), g.dtype)], axis=1)


def _ab_in_to_ref(w):
    return jnp.concatenate([_cols(w, AB_INT[k], AB_REF[k][1]) for k in ("z", "xbc", "dt", "hq", "hf", "hi", "hg")], axis=-1)


def _pairs_to_ref(w, axis):
    sl = lambda i: lax.slice_in_dim(w, 64 * i, 64 * i + 64, axis=axis)
    return jnp.concatenate([sl(2 * h) for h in range(4)] + [sl(2 * h + 1) for h in range(4)], axis=axis)


PAIR_ORDER = tuple(h for p in range(4) for h in (p, 4 + p))


def _cd_in_to_int(g, layer):
    take = _taker(g, layer, 1)
    q = sum([take(64 * h, 64) for h in PAIR_ORDER], [])
    return jnp.concatenate(q + sum([take(*CD_REF[k]) for k in ("gate", "xr", "k", "v")], []), axis=1)


def _cd_out_to_int(g, layer):
    take = _taker(g, layer, 0)
    return jnp.concatenate(sum([take(64 * h, 64) for h in PAIR_ORDER], []) + take(512, 512), axis=0)


def _whole(g, layer, axis):
    return jnp.concatenate([g[k, layer] for k in range(4)], axis=axis)


def _cd_in_to_ref(w):
    q = _pairs_to_ref(_cols(w, CD_Q, 512), w.ndim - 1)
    return jnp.concatenate([q, _cols(w, CD_K, 128), _cols(w, CD_V, 128), _cols(w, CD_GATE, 512), _cols(w, CD_XR, 512)], axis=-1)


def _block_diag(w):
    tiled = jnp.concatenate([w.reshape(RG_WIDTH, 64)] * 8, axis=1)
    return tiled * _BLOCK_DIAG_MASK.astype(w.dtype)


_BLOCK_DIAG_MASK = np.kron(np.eye(8, dtype=np.float32), np.ones((64, 64), np.float32))


def _diag_blocks(m):
    return jnp.sum((m * _BLOCK_DIAG_MASK).reshape(8, 64, 8, 64), axis=2)


def _conv8(w):
    return jnp.concatenate([w, jnp.zeros((SUBLANES - w.shape[0], w.shape[1]), w.dtype)], axis=0)


def _hg_lower_bounds(hg_lower):
    sm = jax.nn.softmax(hg_lower.astype(F32), axis=0)
    return jnp.clip(jnp.cumsum(sm, axis=0) - sm[0], 0.0, 1.0)


def _local_step(x, target, w):
    s = x.shape[0]
    gup = w["ffn_w_up"]
    wdowns = [_whole(w["ffn_w_down"], l, 0) for l in range(DEPTH)]
    lb_all, lb_pull = jax.vjp(_hg_lower_bounds, w["hg_lower"])
    grads = {k: [None] * (v.shape[1] if k in BIG else v.shape[0]) for k, v in w.items()}
    saved = []
    xb = x.astype(BF16)
    for l in range(DEPTH):
        j = l // 2
        t = f"l{l}"
        sv = dict(x=x, xb=xb)
        if l % 2 == 0:
            win = _ab_in_to_int(w["ab_w_in"], j)
            wout = _whole(w["ab_w_out"], j, 0)
            h = _mm([(xb, 0, win, 0, D_MODEL)], AB_COLS, F32, t + "_in")
            w8 = _conv8(w["ssd_conv_w"][j])
            xbcc = _dwconv_fwd(h, AB_XBC // CONV_CB, SSD_CONV_DIM, w8, w["ssd_conv_b"][j], 4, t + "_conv")
            sp = (_pad_lanes(w["ssd_dt_bias"][j]), _pad_lanes(w["ssd_a_log"][j]), _pad_lanes(w["ssd_d"][j]),
                  w["ssd_norm_w"][j].reshape(1, 512))
            ya, ssts = _ssd_fwd(h, xbcc, *sp, t + "_ssd")
            hp = (lb_all[j].reshape(1, 512), w["hg_norm_w"][j].reshape(1, LANES))
            yb, hsts = _hg_fwd(h, *hp, t + "_hg")
            sv.update(win=win, wout=wout, h=h, w8=w8, xbcc=xbcc, sp=sp, ssts=ssts, hp=hp, hsts=hsts)
        else:
            win = _cd_in_to_int(w["cd_w_in"], j)
            wout = _cd_out_to_int(w["cd_w_out"], j)
            h = _mm([(xb, 0, win, 0, D_MODEL)], CD_COLS, F32, t + "_in")
            srow = _pad_lanes(w["swa_sinks"][j])
            ya = _swa_fwd(h, srow, t + "_swa")
            w8 = _conv8(w["rg_conv_w"][j])
            xc = _dwconv_fwd(h, CD_XR // CONV_CB, RG_WIDTH, w8, w["rg_conv_b"][j], 4, t + "_conv")
            rp = (_block_diag(w["rg_wa"][j]).astype(BF16), _block_diag(w["rg_wx"][j]).astype(BF16),
                  w["rg_ba"][j].reshape(1, 512), w["rg_bx"][j].reshape(1, 512), w["rg_lambda"][j].reshape(1, 512))
            yb, hs = _rg_fwd(xc, h, *rp, t + "_rg")
            sv.update(win=win, wout=wout, h=h, w8=w8, srow=srow, xc=xc, rp=rp, hs=hs)
        x1, x1b, r1 = _mm_ln([(ya, 0, wout, 0, 512), (yb, 0, wout, 1, 512)], x, w["ln_g"][l, 0], w["ln_b"][l, 0], t + "_out_ln")
        wdown = wdowns[l]
        fw8 = _conv8(w["ffn_conv_w"][l])
        hu = _mm([(x1b, 0, gup, _shard_as_col_tile(gup, l), D_MODEL)], 2 * FFN_DIM, BF16, t + "_up")
        a, hg, hv = _ffn_gate_fwd(hu, fw8, w["ffn_conv_b"][l], t + "_gate")
        x, xb, r2 = _mm_ln([(a, 0, wdown, 0, FFN_DIM)], x1, w["ln_g"][l, 1], w["ln_b"][l, 1], t + "_down_ln")
        sv.update(ya=ya, yb=yb, r1=r1, x1b=x1b, hu=hu, hg=hg, hv=hv, a=a, r2=r2, fw8=fw8)
        saved.append(sv)

    loss = 0.5 * jnp.sum(_sqerr(x, target, "sqerr")) / D_MODEL
    dya, dyb, ca, cb = x, target, 1.0 / D_MODEL, -1.0 / D_MODEL
    for l in range(DEPTH - 1, -1, -1):
        j = l // 2
        t = f"l{l}b"
        sv = saved[l]
        wdown = wdowns[l]
        dr2, dr2b, dg2, db2, da = _ln_bwd(dya, dyb, ca, cb, sv["r2"], w["ln_g"][l, 1], wdown, BF16, t + "_ln2_down")
        grads["ffn_w_down"][l] = _mm_tn(sv["a"], 0, FFN_DIM, dr2b, 0, D_MODEL, t + "_wdown")
        dhg, dhu, dfw8, dfb = _ffn_gate_bwd(sv["hu"], sv["hg"], sv["hv"], da, sv["fw8"], t + "_gate")
        grads["ffn_conv_w"][l] = dfw8[:3]
        grads["ffn_conv_b"][l] = dfb
        quarter = gup.shape[3]
        dx1 = _mm([(dh, half, gup, _shard_transposed(gup, 2 * gi + half, l), quarter)
                   for gi, dh in enumerate((dhg, dhu)) for half in range(2)], D_MODEL, F32, t + "_up", trans_b=True)
        grads["ffn_w_up"][l] = jnp.concatenate([_mm_tn(sv["x1b"], 0, D_MODEL, dhg, 0, FFN_DIM, t + "_wup_g"),
                                                _mm_tn(sv["x1b"], 0, D_MODEL, dhu, 0, FFN_DIM, t + "_wup_u")], axis=1)
        win, wout, h = sv["win"], sv["wout"], sv["h"]
        dr1, dr1b, dg1, db1, dycat = _ln_bwd(dr2, dx1, ALPHA, 1.0, sv["r1"], w["ln_g"][l, 0], wout, F32, t + "_ln1_out")
        grads["ln_g"][l] = jnp.stack([dg1, dg2])
        grads["ln_b"][l] = jnp.stack([db1, db2])
        dwout = jnp.concatenate([_mm_tn(sv["ya"], 0, 512, dr1b, 0, D_MODEL, t + "_wout_a"),
                                 _mm_tn(sv["yb"], 0, 512, dr1b, 0, D_MODEL, t + "_wout_b")], axis=0)
        if l % 2 == 0:
            dz, dxbcc, ddt, ddtb, dal, ddsk, dnw = _ssd_bwd(dycat, 0, h, sv["xbcc"], sv["ssts"], *sv["sp"], t + "_ssd")
            dxbc, dcw, dcb = _dwconv_bwd(dxbcc, h, AB_XBC // CONV_CB, SSD_CONV_DIM, sv["w8"], 4, t + "_conv")
            dq, df, di, dg, dlb, dhnw = _hg_bwd(dycat, 1, h, sv["hsts"], *sv["hp"], t + "_hg")
            grads["ab_w_out"][j] = dwout
            grads["ssd_conv_w"][j], grads["ssd_conv_b"][j] = dcw[:4], dcb
            grads["ssd_dt_bias"][j], grads["ssd_a_log"][j], grads["ssd_d"][j] = ddtb[0, :8], dal[0, :8], ddsk[0, :8]
            grads["ssd_norm_w"][j], grads["hg_norm_w"][j] = dnw[0], dhnw[0]
            grads["hg_lower"][j] = dlb[0]
            pieces = [(dz, 0, AB_Z, 512), (dq, 0, AB_HQ, 512), (df, 0, AB_HF, 512), (di, 0, AB_HI, 512), (dg, 0, AB_HG, 512),
                      (dxbc, 0, AB_XBC, 256), (dxbc, 1, AB_XBC + 256, 256), (dxbc, 2, AB_XBC + 512, 256), (ddt, 0, AB_DT, LANES)]
            dwin = [_mm_tn(sv["xb"], 0, D_MODEL, p, 0, p.shape[1], t + f"_win{i}")
                    for i, p in enumerate((dz, dq, df, di, dg, dxbc, ddt))]
            grads["ab_w_in"][j] = _ab_in_to_ref(jnp.concatenate(dwin, axis=1))
        else:
            dq, dk, dv, dsr = _swa_bwd(dycat, 0, h, sv["srow"], t + "_swa")
            dxc, dgate, dwa, dwx, dba, dbx, dlam = _rg_bwd(dycat, 1, sv["xc"], h, sv["hs"], *sv["rp"], t + "_rg")
            dxr, dcw, dcb = _dwconv_bwd(dxc, h, CD_XR // CONV_CB, RG_WIDTH, sv["w8"], 4, t + "_conv")
            grads["cd_w_out"][j] = jnp.concatenate([_pairs_to_ref(dwout[:512], 0), dwout[512:]], axis=0)
            grads["swa_sinks"][j] = dsr[0, :8]
            grads["rg_conv_w"][j], grads["rg_conv_b"][j] = dcw[:4], dcb
            grads["rg_wa"][j], grads["rg_wx"][j] = _diag_blocks(dwa), _diag_blocks(dwx)
            grads["rg_ba"][j], grads["rg_bx"][j], grads["rg_lambda"][j] = dba[0], dbx[0], dlam[0]
            pieces = [(dq, 0, CD_Q, 512), (dgate, 0, CD_GATE, 512), (dxr, 0, CD_XR, 512), (dk, 0, CD_K, LANES), (dv, 0, CD_V, LANES)]
            dwin = [_mm_tn(sv["xb"], 0, D_MODEL, p, 0, p.shape[1], t + f"_win{i}")
                    for i, p in enumerate((dq, dgate, dxr, dk, dv))]
            grads["cd_w_in"][j] = _cd_in_to_ref(jnp.concatenate(dwin, axis=1))
        dx0 = _mm([(p, pcb, win, off // k, k) for p, pcb, off, k in pieces], D_MODEL, F32, t + "_in", trans_b=True)
        dya, dyb, ca, cb = dr1, dx0, ALPHA, 1.0
    grad_x = _axpby(dya, dyb, ca, cb, "grad_x")
    out = {k: (v if k in BIG else jnp.stack(v)) for k, v in grads.items()}
    (out["hg_lower"],) = lb_pull(out["hg_lower"])
    return loss, grad_x, out


MESH = pl.DeviceIdType.MESH
ANY = pl.BlockSpec(memory_space=pl.ANY)


def _place():
    x, y, c = lax.axis_index("x"), lax.axis_index("y"), lax.axis_index("c")
    return x, y, c, [(1 - x, y), (x, 1 - y), (1 - x, 1 - y)]


def _rcopy(src, dst, send_sems, recv_sems, k, to):
    return pltpu.make_async_remote_copy(src_ref=src, dst_ref=dst, send_sem=send_sems.at[k], recv_sem=recv_sems.at[k],
                                        device_id=to, device_id_type=MESH)


def _zeros_index(ndim):
    return (0,) * ndim


def _gather_chips(parts, name):
    n = len(parts)

    def body(*refs):
        w_refs, out_refs, (send_sems, recv_sems) = refs[:n], refs[n:2 * n], refs[2 * n:]
        x, y, c, chips = _place()
        me, sib = 2 * x + y, (x, y, 1 - c)
        first = [_rcopy(w_refs[a].at[c], out_refs[a].at[me, c], send_sems, recv_sems, 6 * a + k, (cx, cy, c))
                 for a in range(n) for k, (cx, cy) in enumerate(chips)]
        for cp in first:
            cp.start()
        passed = []
        for a in range(n):
            for k, (cx, cy) in enumerate(chips):
                blk = out_refs[a].at[2 * cx + cy, c]
                _rcopy(blk, blk, send_sems, recv_sems, 6 * a + k, (cx, cy, c)).wait_recv()
                passed.append(_rcopy(blk, blk, send_sems, recv_sems, 6 * a + 3 + k, sib))
                passed[-1].start()
        for a in range(n):
            for k, (cx, cy) in enumerate(chips):
                blk = out_refs[a].at[2 * cx + cy, 1 - c]
                _rcopy(blk, blk, send_sems, recv_sems, 6 * a + 3 + k, sib).wait_recv()
        for cp in first + passed:
            cp.wait_send()

    outs = pl.pallas_call(
        body, name=name, in_specs=[ANY] * n, out_specs=[ANY] * n,
        out_shape=[jax.ShapeDtypeStruct((4,) + p.shape, p.dtype) for p in parts],
        scratch_shapes=[pltpu.SemaphoreType.DMA((6 * n,)), pltpu.SemaphoreType.DMA((6 * n,))],
    )(*parts)
    me = 2 * lax.axis_index("x") + lax.axis_index("y")
    return [lax.dynamic_update_slice(o, p[None], (me,) + _zeros_index(p.ndim)) for o, p in zip(outs, parts)]


def _swap_halves(gs, name):
    n = len(gs)

    def body(*refs):
        g_refs, x_refs, (send_sems, recv_sems) = refs[:n], refs[n:2 * n], refs[2 * n:]
        x, y, c, _ = _place()
        cps = [_rcopy(g_refs[a].at[:, 1 - c], x_refs[a], send_sems, recv_sems, a, (x, y, 1 - c)) for a in range(n)]
        for cp in cps:
            cp.start()
        for cp in cps:
            cp.wait()

    return pl.pallas_call(
        body, name=name, in_specs=[ANY] * n, out_specs=[ANY] * n,
        out_shape=[jax.ShapeDtypeStruct((g.shape[0],) + g.shape[2:], g.dtype) for g in gs],
        scratch_shapes=[pltpu.SemaphoreType.DMA((n,)), pltpu.SemaphoreType.DMA((n,))],
    )(*gs)


def _scatter_chips(hbs, name):
    n = len(hbs)

    def body(*refs):
        h_refs, y_refs, (send_sems, recv_sems) = refs[:n], refs[n:2 * n], refs[2 * n:]
        x, y, c, chips = _place()
        me = 2 * x + y
        sends = [_rcopy(h_refs[a].at[2 * cx + cy], y_refs[a].at[me], send_sems, recv_sems, 3 * a + k, (cx, cy, c))
                 for a in range(n) for k, (cx, cy) in enumerate(chips)]
        for cp in sends:
            cp.start()
        for a in range(n):
            for k, (cx, cy) in enumerate(chips):
                blk = y_refs[a].at[2 * cx + cy]
                _rcopy(blk, blk, send_sems, recv_sems, 3 * a + k, (cx, cy, c)).wait_recv()
        for cp in sends:
            cp.wait_send()

    outs = pl.pallas_call(
        body, name=name, in_specs=[ANY] * n, out_specs=[ANY] * n,
        out_shape=[jax.ShapeDtypeStruct(h.shape, h.dtype) for h in hbs],
        scratch_shapes=[pltpu.SemaphoreType.DMA((3 * n,)), pltpu.SemaphoreType.DMA((3 * n,))],
    )(*hbs)
    me = 2 * lax.axis_index("x") + lax.axis_index("y")
    return [lax.dynamic_update_slice(o, lax.dynamic_slice_in_dim(h, me, 1, axis=0), (me,) + _zeros_index(h.ndim - 1))
            for o, h in zip(outs, hbs)]


def _share_half(fs, name):
    n = len(fs)

    def body(*refs):
        f_refs, out_refs, (send_sems, recv_sems) = refs[:n], refs[n:2 * n], refs[2 * n:]
        x, y, c, _ = _place()
        sib = (x, y, 1 - c)
        cps = [_rcopy(f_refs[a], out_refs[a].at[c], send_sems, recv_sems, a, sib) for a in range(n)]
        for cp in cps:
            cp.start()
        for a in range(n):
            _rcopy(f_refs[a], out_refs[a].at[1 - c], send_sems, recv_sems, a, sib).wait_recv()
        for cp in cps:
            cp.wait_send()

    outs = pl.pallas_call(
        body, name=name, in_specs=[ANY] * n, out_specs=[ANY] * n,
        out_shape=[jax.ShapeDtypeStruct((2,) + f.shape, f.dtype) for f in fs],
        scratch_shapes=[pltpu.SemaphoreType.DMA((n,)), pltpu.SemaphoreType.DMA((n,))],
    )(*fs)
    c = lax.axis_index("c")
    return [lax.dynamic_update_slice(o, f[None], (c,) + _zeros_index(f.ndim)) for o, f in zip(outs, fs)]


def _gather_devices(v, name):
    m_per, n = v.shape

    def body(x_ref, out_ref, send_sems, recv_sems, local_sem):
        x, y, c, chips = _place()
        me, sib = (x, y, c), (x, y, 1 - c)

        def rows(px, py, pc):
            return out_ref.at[pl.ds((4 * px + 2 * py + pc) * m_per, m_per), :]

        def copy(k, block, to, src=None):
            return _rcopy(rows(*block) if src is None else src, rows(*block), send_sems, recv_sems, k, to)

        mine = pltpu.make_async_copy(x_ref, rows(*me), local_sem)
        mine.start()
        first = [copy(0, me, sib, src=x_ref)]
        first += [copy(1 + j, me, (*chip, c), src=x_ref) for j, chip in enumerate(chips)]
        for cp in first:
            cp.start()
        passed = [copy(4 + j, (*chip, c), sib) for j, chip in enumerate(chips)]
        for j, chip in enumerate(chips):
            copy(1 + j, (*chip, c), me).wait_recv()
            passed[j].start()
        copy(0, sib, me).wait_recv()
        for j, chip in enumerate(chips):
            copy(4 + j, (*chip, 1 - c), me).wait_recv()
        for cp in first + passed:
            cp.wait_send()
        mine.wait()

    return pl.pallas_call(
        body, name=name, out_shape=jax.ShapeDtypeStruct((8 * m_per, n), v.dtype),
        in_specs=[pl.BlockSpec(memory_space=pltpu.VMEM)], out_specs=pl.BlockSpec(memory_space=pltpu.VMEM),
        scratch_shapes=[pltpu.SemaphoreType.DMA((7,)), pltpu.SemaphoreType.DMA((7,)), pltpu.SemaphoreType.DMA(())],
        compiler_params=_cp(),
    )(v)


ROW_TILES = (512, 256, 352, 128)


def _add_half(g, xr, c, name):
    _, _, r, n = g.shape
    tr = _pick(r, ROW_TILES)

    def kern(c_ref, g_ref, x_ref, o_ref):
        o_ref[...] = (g_ref[0] + x_ref[...]).astype(BF16)

    return pl.pallas_call(
        kern, name=name,
        grid_spec=pltpu.PrefetchScalarGridSpec(
            num_scalar_prefetch=1, grid=(4, r // tr),
            in_specs=[pl.BlockSpec((1, 1, tr, n), lambda k, i, c_ref: (k, c_ref[0], i, 0)),
                      pl.BlockSpec((1, tr, n), lambda k, i, c_ref: (k, i, 0))],
            out_specs=pl.BlockSpec((1, tr, n), lambda k, i, c_ref: (k, i, 0))),
        out_shape=jax.ShapeDtypeStruct(xr.shape, BF16), compiler_params=_cp(("parallel", "parallel")),
    )(c.reshape(1).astype(jnp.int32), g, xr)


def _sum_blocks(y, name, tr=None):
    n, r, cols = y.shape
    tr = tr or _pick(r, ROW_TILES)
    tr = tr if r % tr == 0 else r

    def kern(y_ref, o_ref):
        acc = y_ref[0].astype(F32)
        for k in range(1, n):
            acc = acc + y_ref[k].astype(F32)
        o_ref[...] = acc

    return pl.pallas_call(
        kern, name=name, grid=(r // tr,), in_specs=[pl.BlockSpec((n, tr, cols), lambda i: (0, i, 0))],
        out_specs=pl.BlockSpec((tr, cols), lambda i: (i, 0)), out_shape=jax.ShapeDtypeStruct((r, cols), F32),
        compiler_params=_cp(("parallel",)),
    )(y)


def _adamw(w, g, m, v, name, tr=256):
    r, n = w.shape
    tr = tr if r % tr == 0 else r

    def kern(w_ref, g_ref, m_ref, v_ref, d_ref, nm_ref, nv_ref):
        gg = g_ref[...]
        nm = ADAM_B1 * m_ref[...] + (1.0 - ADAM_B1) * gg
        nv = ADAM_B2 * v_ref[...] + (1.0 - ADAM_B2) * (gg * gg)
        m_hat = nm / (1.0 - ADAM_B1 ** ADAM_STEP)
        v_hat = nv / (1.0 - ADAM_B2 ** ADAM_STEP)
        d_ref[...] = -ADAM_LR * (m_hat / (jnp.sqrt(v_hat) + ADAM_EPS) + ADAM_WD * w_ref[...])
        nm_ref[...] = nm
        nv_ref[...] = nv

    blk = pl.BlockSpec((tr, n), lambda i: (i, 0))
    return pl.pallas_call(
        kern, name=name, grid=(r // tr,), in_specs=[blk] * 4, out_specs=[blk] * 3,
        out_shape=[jax.ShapeDtypeStruct((r, n), F32)] * 3, compiler_params=_cp(("parallel",)),
    )(w, g, m, v)


def _rows_of(n):
    return -(-n // (LANES * SUBLANES)) * SUBLANES


def _pack(arrs, lead, dtype, mult):
    ls = arrs[0].shape[:lead]
    blocks, total = [], 0
    for a in arrs:
        f = a.astype(dtype).reshape(ls + (-1,))
        n = f.shape[-1]
        rows = _rows_of(n)
        if rows * LANES != n:
            f = jnp.concatenate([f, jnp.zeros(ls + (rows * LANES - n,), dtype)], axis=-1)
        blocks.append(f.reshape(ls + (rows, LANES)))
        total += rows
    if total % mult:
        blocks.append(jnp.zeros(ls + (mult - total % mult, LANES), dtype))
    return jnp.concatenate(blocks, axis=lead)


def _unpack(buf, lead, shapes):
    ls = buf.shape[:lead]
    out, off = [], 0
    for sh in shapes:
        n = int(np.prod(sh))
        rows = _rows_of(n)
        flat = lax.slice_in_dim(buf, off, off + rows, axis=lead).reshape(ls + (rows * LANES,))
        if rows * LANES != n:
            flat = lax.slice_in_dim(flat, 0, n, axis=lead)
        out.append(flat.reshape(ls + tuple(sh)))
        off += rows
    return out


BIG = ("ab_w_in", "ab_w_out", "cd_w_in", "cd_w_out", "ffn_w_up", "ffn_w_down")
BIG_COLS = ("ab_w_in", "cd_w_in", "ffn_w_up")
SMALL_SHARDED = ("ssd_conv_w", "rg_conv_w", "rg_conv_b", "rg_ba", "rg_bx", "rg_lambda", "ffn_conv_w", "ln_g", "ln_b")
WEIGHTS = ("ab_w_in", "ssd_conv_w", "ssd_conv_b", "ssd_dt_bias", "ssd_a_log", "ssd_d", "ssd_norm_w", "hg_lower", "hg_norm_w",
           "ab_w_out", "cd_w_in", "swa_sinks", "rg_conv_w", "rg_conv_b", "rg_wa", "rg_ba", "rg_wx", "rg_bx", "rg_lambda",
           "cd_w_out", "ffn_w_up", "ffn_conv_w", "ffn_conv_b", "ffn_w_down", "ln_g", "ln_b")
SMALL = tuple(n for n in WEIGHTS if n not in BIG)


def _full_from_shards(name, g):
    if name in BIG_COLS or name in SMALL_SHARDED:
        return jnp.concatenate([g[k] for k in range(4)], axis=-1)
    return jnp.concatenate([g[k] for k in range(4)], axis=1)


def _layer_shards(name, f):
    k, n = f.shape
    if name in BIG_COLS:
        return jnp.transpose(f.reshape(k, 4, n // 4), (1, 0, 2))
    return f.reshape(4, k // 4, n)


def kernel(x, ab_w_in, ssd_conv_w, ssd_conv_b, ssd_dt_bias, ssd_a_log, ssd_d, ssd_norm_w, hg_lower, hg_norm_w, ab_w_out, cd_w_in, swa_sinks, rg_conv_w, rg_conv_b, rg_wa, rg_ba, rg_wx, rg_bx, rg_lambda, cd_w_out, ffn_w_up, ffn_conv_w, ffn_conv_b, ffn_w_down, ln_g, ln_b, loss_target, m_ab_w_in, m_ssd_conv_w, m_ssd_conv_b, m_ssd_dt_bias, m_ssd_a_log, m_ssd_d, m_ssd_norm_w, m_hg_lower, m_hg_norm_w, m_ab_w_out, m_cd_w_in, m_swa_sinks, m_rg_conv_w, m_rg_conv_b, m_rg_wa, m_rg_ba, m_rg_wx, m_rg_bx, m_rg_lambda, m_cd_w_out, m_ffn_w_up, m_ffn_conv_w, m_ffn_conv_b, m_ffn_w_down, m_ln_g, m_ln_b, v_ab_w_in, v_ssd_conv_w, v_ssd_conv_b, v_ssd_dt_bias, v_ssd_a_log, v_ssd_d, v_ssd_norm_w, v_hg_lower, v_hg_norm_w, v_ab_w_out, v_cd_w_in, v_swa_sinks, v_rg_conv_w, v_rg_conv_b, v_rg_wa, v_rg_ba, v_rg_wx, v_rg_bx, v_rg_lambda, v_cd_w_out, v_ffn_w_up, v_ffn_conv_w, v_ffn_conv_b, v_ffn_w_down, v_ln_g, v_ln_b):
    args = locals()
    w = {n: args[n] for n in WEIGHTS}
    mom = {n: args["m_" + n] for n in WEIGHTS}
    var = {n: args["v_" + n] for n in WEIGHTS}
    cx, cy, cc = lax.axis_index("x"), lax.axis_index("y"), lax.axis_index("c")
    chip = 2 * cx + cy

    halves = [w[n].astype(BF16).reshape((2, w[n].shape[0] // 2) + w[n].shape[1:]) for n in BIG]
    gathered = _gather_chips(halves, "gather_weights")
    full = {n: g.reshape((4,) + w[n].shape) for n, g in zip(BIG, gathered)}
    sp = _pack([w[n] for n in SMALL_SHARDED], 0, F32, SUBLANES)
    sg = _gather_devices(sp, "gather_small").reshape(4, 2, sp.shape[0], LANES)[:, 0]
    for n, g in zip(SMALL_SHARDED, _unpack(sg, 1, [w[n].shape for n in SMALL_SHARDED])):
        full[n] = _full_from_shards(n, g)
    for n in SMALL:
        full.setdefault(n, w[n])

    loss, grad_x, grads = _local_step(x[0], loss_target[0], full)
    loss = lax.psum(loss, ("x", "y", "c"))

    g4 = []
    for n in BIG:
        st = jnp.stack([_layer_shards(n, f) for f in grads[n]], axis=1)
        g4.append(st.reshape(4, 2, st.shape[1] // 2 * st.shape[2], st.shape[3]))
    hbs = [_add_half(g, xr, cc, "add_halves_" + n) for n, g, xr in zip(BIG, g4, _swap_halves(g4, "swap_halves"))]
    own = [_sum_blocks(y, "sum_chips_" + n) for n, y in zip(BIG, _scatter_chips(hbs, "scatter_chips"))]
    gout = {n: o.reshape(w[n].shape) for n, o in zip(BIG, _share_half(own, "share_half"))}

    small_shapes = [grads[n].shape for n in SMALL]
    gs = _pack([grads[n] for n in SMALL], 0, F32, SUBLANES)
    gsum = _sum_blocks(_gather_devices(gs, "gather_small_grads").reshape(8, gs.shape[0], LANES), "sum_small")
    for n, g in zip(SMALL, _unpack(gsum, 0, small_shapes)):
        if n in SMALL_SHARDED:
            width = w[n].shape[-1]
            g = lax.dynamic_slice_in_dim(g, chip * width, width, axis=g.ndim - 1)
        gout[n] = g

    delta, new_m, new_v = {}, {}, {}
    for n in BIG:
        sh = w[n].shape
        two = lambda a: a.reshape(-1, sh[-1])
        d, nm, nv = _adamw(two(w[n]), two(gout[n]), two(mom[n]), two(var[n]), "adamw_" + n)
        delta[n], new_m[n], new_v[n] = d.reshape(sh), nm.reshape(sh), nv.reshape(sh)
    local_shapes = [w[n].shape for n in SMALL]
    packs = [_pack([d[n] for n in SMALL], 0, F32, SUBLANES) for d in (w, gout, mom, var)]
    for dst, buf in zip((delta, new_m, new_v), _adamw(*packs, "adamw_small")):
        dst.update(zip(SMALL, _unpack(buf, 0, local_shapes)))

    return (loss, grad_x[None], *[gout[n] for n in WEIGHTS], *[delta[n] for n in WEIGHTS],
            *[new_m[n] for n in WEIGHTS], *[new_v[n] for n in WEIGHTS])
```

```python
import functools

import numpy as np
import jax
import jax.numpy as jnp
from jax import lax
from jax.experimental import pallas as pl
from jax.experimental.pallas import tpu as pltpu

F32 = jnp.float32
BF16 = jnp.bfloat16

D_MODEL = 1024
DEPTH = 4
SSD_HEADS = 8
SSD_D_INNER = 512
SSD_CONV_DIM = 768
SSD_CHUNK = 128
HG_CHUNK = 64
HG_HEADS = 4
SWA_BLOCK = 128
RG_WIDTH = 512
FFN_DIM = 2816
LN_EPS = 1e-5
RMS_EPS = 1e-6
MASK_VALUE = -1e9
ALPHA = (2 * DEPTH) ** 0.25
RG_C = 8.0
ADAM_LR, ADAM_B1, ADAM_B2, ADAM_EPS, ADAM_WD, ADAM_STEP = 0.001, 0.9, 0.999, 1e-08, 0.01, 10

LANES = 128
SUBLANES = 8
HALO = 16
CONV_CB = 256
VMEM_LIMIT = 56 * 1024 * 1024

AB_Z, AB_HQ, AB_HF, AB_HI, AB_HG, AB_XBC, AB_DT, AB_COLS = 0, 512, 1024, 1536, 2048, 2560, 3328, 3456
CD_Q, CD_GATE, CD_XR, CD_K, CD_V, CD_COLS = 0, 512, 1024, 1536, 1664, 1792


def _cp(sem=None):
    kw = dict(vmem_limit_bytes=VMEM_LIMIT)
    if sem is not None:
        kw["dimension_semantics"] = sem
    return pltpu.CompilerParams(**kw)


def _dot(a, b, dims=(((1,), (0,)), ((), ())), precision=None):
    return lax.dot_general(a, b, dims, precision=precision, preferred_element_type=F32)


NN = (((1,), (0,)), ((), ()))
NT = (((1,), (1,)), ((), ()))
TN = (((0,), (0,)), ((), ()))


@functools.partial(jax.custom_vjp, nondiff_argnums=(2,))
def _bdot(a, b, dims=NN):
    return _dot(a.astype(BF16), b.astype(BF16), dims)


def _bdot_fwd(a, b, dims):
    return _bdot(a, b, dims), (a, b)


def _bdot_bwd(dims, res, ct):
    a, b = res
    ab, bb, cb = a.astype(BF16), b.astype(BF16), ct.astype(BF16)
    if dims == NN:
        da, db = _dot(cb, bb, NT), _dot(ab, cb, TN)
    elif dims == NT:
        da, db = _dot(cb, bb, NN), _dot(cb, ab, TN)
    else:
        da, db = _dot(bb, cb, NT), _dot(ab, cb, NN)
    return da.astype(a.dtype), db.astype(b.dtype)


_bdot.defvjp(_bdot_fwd, _bdot_bwd)


WIDE_TILES = (1408, 1152, 1024, 896, 768, 512, 384, 256, 128)
MM_BLOCK_BYTES = 6 * 1024 * 1024


def _pick(n, cands):
    for c in cands:
        if n % c == 0:
            return c
    return n


def _mm(pairs, n_out, out_dtype, name, trans_b=False, tm=512):
    m = pairs[0][0].shape[0]
    a_row_bytes = sum(p[4] * p[0].dtype.itemsize for p in pairs)
    b_col_bytes = sum(p[4] * p[2].dtype.itemsize for p in pairs)
    tm = min(2 * tm if 2 * tm * a_row_bytes <= 2 * MM_BLOCK_BYTES else tm, m)
    tn = _pick(n_out, [c for c in WIDE_TILES if c * b_col_bytes <= MM_BLOCK_BYTES])
    n = len(pairs)

    def kern(*refs):
        acc = None
        for i in range(n):
            p = _bdot(refs[i][...], refs[n + i][...], NT if trans_b else NN)
            acc = p if acc is None else acc + p
        refs[2 * n][...] = acc.astype(out_dtype)

    in_specs = []
    for a, acb, b, bcb, k in pairs:
        in_specs.append(pl.BlockSpec((tm, k), functools.partial(lambda i, j, c: (i, c), c=acb)))
    for a, acb, b, bcb, k in pairs:
        if callable(bcb):
            in_specs.append(bcb(tn))
        elif trans_b:
            in_specs.append(pl.BlockSpec((tn, k), functools.partial(lambda i, j, c: (j, c), c=bcb)))
        else:
            in_specs.append(pl.BlockSpec((k, tn), functools.partial(lambda i, j, c: (c, j), c=bcb)))
    return pl.pallas_call(
        kern, name=name, grid=(m // tm, n_out // tn), in_specs=in_specs,
        out_specs=pl.BlockSpec((tm, tn), lambda i, j: (i, j)),
        out_shape=jax.ShapeDtypeStruct((m, n_out), out_dtype),
        compiler_params=_cp(("parallel", "arbitrary")),
    )(*[p[0] for p in pairs], *[p[2] for p in pairs])


def _shard_as_col_tile(g, layer):
    def spec(tn):
        assert tn == g.shape[3]
        return pl.BlockSpec((None, None, g.shape[2], tn), lambda i, j: (j, layer, 0, 0))
    return spec


def _shard_transposed(g, chip, layer):
    return lambda tn: pl.BlockSpec((None, None, tn, g.shape[3]), lambda i, j: (chip, layer, j, 0))


def _mm_tn(a, a_cb, ka, g, g_cb, ng, name, tm=2048):
    m = a.shape[0]
    tm = min(tm, m)
    tk = _pick(ka, (1024, 1408, 512, 256, 128))
    tn = _pick(ng, WIDE_TILES)
    nm = m // tm

    def kern(a_ref, g_ref, o_ref):
        p = _bdot(a_ref[...], g_ref[...], TN)

        @pl.when(pl.program_id(2) == 0)
        def _():
            o_ref[...] = p

        @pl.when(pl.program_id(2) > 0)
        def _():
            o_ref[...] += p

    ak, gk = ka // tk, ng // tn
    return pl.pallas_call(
        kern, name=name, grid=(ak, gk, nm),
        in_specs=[pl.BlockSpec((tm, tk), lambda i, j, r: (r, a_cb * ak + i)),
                  pl.BlockSpec((tm, tn), lambda i, j, r: (r, g_cb * gk + j))],
        out_specs=pl.BlockSpec((tk, tn), lambda i, j, r: (i, j)),
        out_shape=jax.ShapeDtypeStruct((ka, ng), F32),
        compiler_params=_cp(("parallel", "parallel", "arbitrary")),
    )(a, g)


def _mm_ln(pairs, x, g, b, name, tm=512):
    m, d = x.shape
    tm = min(tm, m)
    n = len(pairs)

    def kern(*refs):
        x_ref, g_ref, b_ref, y_ref, yb_ref, r_ref = refs[2 * n:]
        r = ALPHA * x_ref[...]
        for i in range(n):
            r = r + _bdot(refs[i][...], refs[n + i][...])
        mu = jnp.mean(r, -1, keepdims=True)
        xc = r - mu
        var = jnp.mean(xc * xc, -1, keepdims=True)
        y = xc * lax.rsqrt(var + LN_EPS) * g_ref[...] + b_ref[...]
        y_ref[...] = y
        yb_ref[...] = y.astype(BF16)
        r_ref[...] = r

    row = pl.BlockSpec((tm, d), lambda i: (i, 0))
    vec = pl.BlockSpec((1, d), lambda i: (0, 0))
    in_specs = [pl.BlockSpec((tm, k), functools.partial(lambda i, c: (i, c), c=acb)) for _, acb, _, _, k in pairs]
    in_specs += [pl.BlockSpec((k, d), functools.partial(lambda i, c: (c, 0), c=bcb)) for _, _, _, bcb, k in pairs]
    return pl.pallas_call(
        kern, name=name, grid=(m // tm,), in_specs=in_specs + [row, vec, vec], out_specs=[row, row, row],
        out_shape=[jax.ShapeDtypeStruct((m, d), F32), jax.ShapeDtypeStruct((m, d), BF16), jax.ShapeDtypeStruct((m, d), F32)],
        compiler_params=_cp(("parallel",)),
    )(*[p[0] for p in pairs], *[p[2] for p in pairs], x, g.reshape(1, d), b.reshape(1, d))


def _ln_bwd(dya, dyb, ca, cb, r, g, wt, out_dtype, name, tm=512):
    s, d = r.shape
    n = wt.shape[0]
    tm = min(tm, s)

    def kern(a_ref, b_ref, r_ref, g_ref, w_ref, dr_ref, drb_ref, dg_ref, db_ref, p_ref):
        dy = ca * a_ref[...] + cb * b_ref[...]
        rr = r_ref[...]
        mu = jnp.mean(rr, -1, keepdims=True)
        xc = rr - mu
        var = jnp.mean(xc * xc, -1, keepdims=True)
        rstd = lax.rsqrt(var + LN_EPS)
        xhat = xc * rstd
        dxh = dy * g_ref[...]
        dr = rstd * (dxh - jnp.mean(dxh, -1, keepdims=True) - xhat * jnp.mean(dxh * xhat, -1, keepdims=True))
        dr_ref[...] = dr
        drb = dr.astype(BF16)
        drb_ref[...] = drb
        p_ref[...] = _dot(drb, w_ref[...], NT).astype(out_dtype)
        dg = jnp.sum(dy * xhat, 0, keepdims=True)
        db = jnp.sum(dy, 0, keepdims=True)

        @pl.when(pl.program_id(0) == 0)
        def _():
            dg_ref[...] = dg
            db_ref[...] = db

        @pl.when(pl.program_id(0) > 0)
        def _():
            dg_ref[...] += dg
            db_ref[...] += db

    row = pl.BlockSpec((tm, d), lambda i: (i, 0))
    vec = pl.BlockSpec((1, d), lambda i: (0, 0))
    dr, drb, dg, db, prod = pl.pallas_call(
        kern, name=name, grid=(s // tm,),
        in_specs=[row, row, row, vec, pl.BlockSpec((n, d), lambda i: (0, 0))],
        out_specs=[row, row, vec, vec, pl.BlockSpec((tm, n), lambda i: (i, 0))],
        out_shape=[jax.ShapeDtypeStruct((s, d), F32), jax.ShapeDtypeStruct((s, d), BF16), jax.ShapeDtypeStruct((1, d), F32),
                   jax.ShapeDtypeStruct((1, d), F32), jax.ShapeDtypeStruct((s, n), out_dtype)],
        compiler_params=_cp(("arbitrary",)),
    )(dya, dyb, r, g.reshape(1, d), wt)
    return dr, drb, dg[0], db[0], prod


def _sqerr(y, t, name, tm=256):
    s, d = y.shape
    tm = min(tm, s)

    def kern(y_ref, t_ref, o_ref):
        e = y_ref[...] - t_ref[...]
        p = jnp.sum(e * e, 0, keepdims=True)

        @pl.when(pl.program_id(0) == 0)
        def _():
            o_ref[...] = p

        @pl.when(pl.program_id(0) > 0)
        def _():
            o_ref[...] += p

    row = pl.BlockSpec((tm, d), lambda i: (i, 0))
    return pl.pallas_call(
        kern, name=name, grid=(s // tm,), in_specs=[row, row], out_specs=pl.BlockSpec((1, d), lambda i: (0, 0)),
        out_shape=jax.ShapeDtypeStruct((1, d), F32), compiler_params=_cp(("arbitrary",)),
    )(y, t)


def _axpby(a, b, ca, cb, name, tm=256):
    s, d = a.shape
    tm = min(tm, s)

    def kern(a_ref, b_ref, o_ref):
        o_ref[...] = ca * a_ref[...] + cb * b_ref[...]

    row = pl.BlockSpec((tm, d), lambda i: (i, 0))
    return pl.pallas_call(
        kern, name=name, grid=(s // tm,), in_specs=[row, row], out_specs=row,
        out_shape=jax.ShapeDtypeStruct((s, d), F32), compiler_params=_cp(("parallel",)),
    )(a, b)


def _shift_down(xx, s, n):
    if s == 0:
        return xx[HALO:HALO + n]
    return pltpu.roll(xx, s, axis=0)[HALO:HALO + n]


def _shift_up(yy, s, n):
    if s == 0:
        return yy[0:n]
    return pltpu.roll(yy, yy.shape[0] - s, axis=0)[0:n]


def _prev_halo(tm):
    return lambda i: jnp.maximum(i * (tm // HALO) - 1, 0)


CONV_ROWS = 128


def _dwconv_fwd(x, x_cb0, c, w8, b, k, name, tm=1024):
    s = x.shape[0]
    tm = min(tm, s)
    ph = _prev_halo(tm)

    rc = min(CONV_ROWS, tm)

    def kern(x_ref, h_ref, w_ref, b_ref, y_ref):
        w, b = w_ref[...], b_ref[...]
        for c0 in range(0, tm, rc):
            if c0 == 0:
                halo = jnp.where(pl.program_id(0) == 0, 0.0, h_ref[...].astype(F32))
                xx = jnp.concatenate([halo, x_ref[0:rc, :].astype(F32)], axis=0)
            else:
                xx = x_ref[c0 - HALO:c0 + rc, :].astype(F32)
            acc = b + w[k - 1:k] * xx[HALO:]
            for j in range(k - 1):
                acc = acc + w[j:j + 1] * _shift_down(xx, k - 1 - j, rc)
            y_ref[c0:c0 + rc, :] = acc

    return pl.pallas_call(
        kern, name=name, grid=(s // tm, c // CONV_CB),
        in_specs=[pl.BlockSpec((tm, CONV_CB), lambda i, j: (i, x_cb0 + j)),
                  pl.BlockSpec((HALO, CONV_CB), lambda i, j: (ph(i), x_cb0 + j)),
                  pl.BlockSpec((SUBLANES, CONV_CB), lambda i, j: (0, j)),
                  pl.BlockSpec((1, CONV_CB), lambda i, j: (0, j))],
        out_specs=pl.BlockSpec((tm, CONV_CB), lambda i, j: (i, j)),
        out_shape=jax.ShapeDtypeStruct((s, c), F32), compiler_params=_cp(("parallel", "parallel")),
    )(x, x, w8, b.reshape(1, c))


def _dwconv_bwd(dy, x, x_cb0, c, w8, k, name, tm=1024):
    s = x.shape[0]
    tm = min(tm, s)
    nt = s // tm
    ph = _prev_halo(tm)
    nh = lambda i: jnp.minimum((i + 1) * (tm // HALO), s // HALO - 1)

    rc = min(CONV_ROWS, tm)

    def kern(dy_ref, dyn_ref, x_ref, h_ref, w_ref, dx_ref, dw_ref, db_ref):
        i = pl.program_id(1)
        w = w_ref[...]
        dw = db = None
        for c0 in range(0, tm, rc):
            if c0 == tm - rc:
                dyy = jnp.concatenate([dy_ref[c0:, :], jnp.where(i == nt - 1, 0.0, dyn_ref[...])], axis=0)
            else:
                dyy = dy_ref[c0:c0 + rc + HALO, :]
            xc = x_ref[c0:c0 + rc, :].astype(F32)
            taps = [dyy[0:rc] if j == k - 1 else _shift_up(dyy, k - 1 - j, rc) for j in range(k)]
            dx = w[0:1] * taps[0]
            for j in range(1, k):
                dx = dx + w[j:j + 1] * taps[j]
            dx_ref[c0:c0 + rc, :] = dx.astype(BF16)
            rows = [jnp.sum(taps[j] * xc, 0, keepdims=True) for j in range(k)]
            dwc = jnp.concatenate(rows + [jnp.zeros((SUBLANES - k, CONV_CB), F32)], axis=0)
            dbc = jnp.sum(dyy[0:rc], 0, keepdims=True)
            dw, db = (dwc, dbc) if dw is None else (dw + dwc, db + dbc)

        @pl.when(i == 0)
        def _():
            dw_ref[...] = dw
            db_ref[...] = db

        @pl.when(i > 0)
        def _():
            dw_ref[...] += dw
            db_ref[...] += db

    dx, dw, db = pl.pallas_call(
        kern, name=name, grid=(c // CONV_CB, nt),
        in_specs=[pl.BlockSpec((tm, CONV_CB), lambda j, i: (i, j)),
                  pl.BlockSpec((HALO, CONV_CB), lambda j, i: (nh(i), j)),
                  pl.BlockSpec((tm, CONV_CB), lambda j, i: (i, x_cb0 + j)),
                  pl.BlockSpec((HALO, CONV_CB), lambda j, i: (ph(i), x_cb0 + j)),
                  pl.BlockSpec((SUBLANES, CONV_CB), lambda j, i: (0, j))],
        out_specs=[pl.BlockSpec((tm, CONV_CB), lambda j, i: (i, j)),
                   pl.BlockSpec((SUBLANES, CONV_CB), lambda j, i: (0, j)),
                   pl.BlockSpec((1, CONV_CB), lambda j, i: (0, j))],
        out_shape=[jax.ShapeDtypeStruct((s, c), BF16), jax.ShapeDtypeStruct((SUBLANES, c), F32),
                   jax.ShapeDtypeStruct((1, c), F32)],
        compiler_params=_cp(("parallel", "arbitrary")),
    )(dy, dy, x, x, w8)
    return dx, dw, db[0]


def _silu(x):
    return x * jax.nn.sigmoid(x)


def _ffn_gate_fwd(hu, w8, b, name, tm=2048):
    s = hu.shape[0]
    tm = min(tm, s)
    nb = FFN_DIM // CONV_CB
    ph = _prev_halo(tm)
    k = 3

    rc = min(GATE_ROWS, tm)

    def conv(x_ref, h_ref, w, b, r0):
        if r0 == 0:
            halo = jnp.where(pl.program_id(0) == 0, 0.0, h_ref[...].astype(F32))
            xx = jnp.concatenate([halo, x_ref[0:rc, :].astype(F32)], axis=0)
        else:
            xx = x_ref[r0 - HALO:r0 + rc, :].astype(F32)
        acc = b + w[k - 1:k] * xx[HALO:]
        for j in range(k - 1):
            acc = acc + w[j:j + 1] * _shift_down(xx, k - 1 - j, rc)
        return acc

    def kern(g_ref, gh_ref, u_ref, uh_ref, wg_ref, wu_ref, bg_ref, bu_ref, a_ref, hg_ref, hu_ref):
        wg, wu, bg, bu = wg_ref[...], wu_ref[...], bg_ref[...], bu_ref[...]
        for c in range(tm // rc):
            r0 = c * rc
            g = conv(g_ref, gh_ref, wg, bg, r0)
            u = conv(u_ref, uh_ref, wu, bu, r0)
            a_ref[r0:r0 + rc, :] = (_silu(g) * u).astype(BF16)
            hg_ref[r0:r0 + rc, :] = g.astype(BF16)
            hu_ref[r0:r0 + rc, :] = u.astype(BF16)

    main = lambda off: pl.BlockSpec((tm, CONV_CB), lambda i, j: (i, off + j))
    halo = lambda off: pl.BlockSpec((HALO, CONV_CB), lambda i, j: (ph(i), off + j))
    wsp = lambda off: pl.BlockSpec((SUBLANES, CONV_CB), lambda i, j: (0, off + j))
    bsp = lambda off: pl.BlockSpec((1, CONV_CB), lambda i, j: (0, off + j))
    b2 = b.reshape(1, 2 * FFN_DIM)
    half = jax.ShapeDtypeStruct((s, FFN_DIM), BF16)
    return pl.pallas_call(
        kern, name=name, grid=(s // tm, nb),
        in_specs=[main(0), halo(0), main(nb), halo(nb), wsp(0), wsp(nb), bsp(0), bsp(nb)],
        out_specs=[main(0), main(0), main(0)], out_shape=[half, half, half],
        compiler_params=_cp(("parallel", "parallel")),
    )(hu, hu, hu, hu, w8, w8, b2, b2)


GATE_ROWS = 128


def _ffn_gate_bwd(hu, hg, hv, da, w8, name, tm=2048):
    s = hu.shape[0]
    tm = min(tm, s)
    nt = s // tm
    nb = FFN_DIM // CONV_CB
    nh = lambda i: jnp.minimum((i + 1) * (tm // HALO), s // HALO - 1)
    k = 3

    rc = min(GATE_ROWS, tm)

    def kern(xg_ref, xu_ref, g_ref, gn_ref, u_ref, un_ref, da_ref, dan_ref, wg_ref, wu_ref,
             dg_ref, du_ref, dwg_ref, dwu_ref, dbg_ref, dbu_ref):
        i = pl.program_id(1)
        wg, wu = wg_ref[...], wu_ref[...]

        def back(dh, x, w):
            taps = [dh[0:rc] if j == k - 1 else _shift_up(dh, k - 1 - j, rc) for j in range(k)]
            dx = w[0:1] * taps[0]
            for j in range(1, k):
                dx = dx + w[j:j + 1] * taps[j]
            rows = [jnp.sum(taps[j] * x, 0, keepdims=True) for j in range(k)]
            dw = jnp.concatenate(rows + [jnp.zeros((SUBLANES - k, CONV_CB), F32)], axis=0)
            return dx, dw, jnp.sum(dh[0:rc], 0, keepdims=True)

        sums = None
        for c in range(tm // rc):
            r0, last = c * rc, c == tm // rc - 1

            def ld(ref, nref):
                if last:
                    return jnp.concatenate([ref[r0:, :].astype(F32), nref], axis=0)
                return ref[r0:r0 + rc + HALO, :].astype(F32)

            g, u = ld(g_ref, gn_ref[...].astype(F32)), ld(u_ref, un_ref[...].astype(F32))
            dae = ld(da_ref, jnp.where(i == nt - 1, 0.0, dan_ref[...].astype(F32)))
            sg = jax.nn.sigmoid(g)
            dhg = dae * u * (sg * (1.0 + g * (1.0 - sg)))
            dhu = dae * (g * sg)
            dxg, dwg, dbg = back(dhg, xg_ref[r0:r0 + rc, :].astype(F32), wg)
            dxu, dwu, dbu = back(dhu, xu_ref[r0:r0 + rc, :].astype(F32), wu)
            dg_ref[r0:r0 + rc, :] = dxg.astype(BF16)
            du_ref[r0:r0 + rc, :] = dxu.astype(BF16)
            part = (dwg, dwu, dbg, dbu)
            sums = part if sums is None else tuple(a + b for a, b in zip(sums, part))
        dwg, dwu, dbg, dbu = sums

        @pl.when(i == 0)
        def _():
            dwg_ref[...] = dwg
            dwu_ref[...] = dwu
            dbg_ref[...] = dbg
            dbu_ref[...] = dbu

        @pl.when(i > 0)
        def _():
            dwg_ref[...] += dwg
            dwu_ref[...] += dwu
            dbg_ref[...] += dbg
            dbu_ref[...] += dbu

    main = lambda off: pl.BlockSpec((tm, CONV_CB), lambda j, i: (i, off + j))
    nxt = lambda off: pl.BlockSpec((HALO, CONV_CB), lambda j, i: (nh(i), off + j))
    wsp = lambda off: pl.BlockSpec((SUBLANES, CONV_CB), lambda j, i: (0, off + j))
    bsp = lambda off: pl.BlockSpec((1, CONV_CB), lambda j, i: (0, off + j))
    outs = pl.pallas_call(
        kern, name=name, grid=(nb, nt),
        in_specs=[main(0), main(nb), main(0), nxt(0), main(0), nxt(0), main(0), nxt(0), wsp(0), wsp(nb)],
        out_specs=[main(0), main(0), wsp(0), wsp(0), bsp(0), bsp(0)],
        out_shape=[jax.ShapeDtypeStruct((s, FFN_DIM), BF16), jax.ShapeDtypeStruct((s, FFN_DIM), BF16),
                   jax.ShapeDtypeStruct((SUBLANES, FFN_DIM), F32), jax.ShapeDtypeStruct((SUBLANES, FFN_DIM), F32),
                   jax.ShapeDtypeStruct((1, FFN_DIM), F32), jax.ShapeDtypeStruct((1, FFN_DIM), F32)],
        compiler_params=_cp(("parallel", "arbitrary")),
    )(hu, hu, hg, hg, hv, hv, da, da, w8, w8)
    dg, du, dwg, dwu, dbg, dbu = outs
    return dg, du, jnp.concatenate([dwg, dwu], axis=1), jnp.concatenate([dbg[0], dbu[0]])


def _iota(shape, dim):
    return lax.broadcasted_iota(jnp.int32, shape, dim)


def _in_range(idx, start, size):
    return jnp.logical_and(idx >= start, idx < start + size)


@functools.partial(jax.custom_vjp, nondiff_argnums=(2, 3))
def _zdot(a, b, dims, z):
    zz, x = ((a, b) if z == 0 else (b, a))
    zz = zz.astype(BF16)
    xh = x.astype(BF16)
    r1 = x - xh.astype(F32)
    xm = r1.astype(BF16)
    xl = (r1 - xm.astype(F32)).astype(BF16)
    f = (lambda p: _dot(zz, p, dims)) if z == 0 else (lambda p: _dot(p, zz, dims))
    return f(xh) + f(xm) + f(xl)


def _zdot_fwd(a, b, dims, z):
    return _zdot(a, b, dims, z), (a if z == 0 else b)


def _zdot_bwd(dims, z, zz, ct):
    zb = zz.astype(BF16)
    ch = ct.astype(BF16)
    cm = (ct - ch.astype(F32)).astype(BF16)
    if dims == NN and z == 0:
        f = lambda c: _dot(zb, c, TN)
    elif dims == NN and z == 1:
        f = lambda c: _dot(c, zb, NT)
    else:
        f = lambda c: _dot(c, zb, TN)
    dx = f(ch) + f(cm)
    return (jnp.zeros_like(zz), dx) if z == 0 else (dx, jnp.zeros_like(zz))


_zdot.defvjp(_zdot_fwd, _zdot_bwd)


def _expand8(v, e):
    return jnp.sum(_zdot(jnp.broadcast_to(v, (SUBLANES, v.shape[1])), e, NN, 1), 0, keepdims=True) * (1.0 / SUBLANES)


def _ssd_consts():
    tri = np.tril(np.ones((SSD_CHUNK, SSD_CHUNK), np.float32))
    e = np.zeros((LANES, SSD_D_INNER), np.float32)
    for h in range(SSD_HEADS):
        e[h, 64 * h:64 * h + 64] = 1.0
    bd = np.zeros((LANES, SSD_D_INNER), np.float32)
    bd[:64, :256] = 1.0
    bd[64:, 256:] = 1.0
    return jnp.asarray(tri), jnp.asarray(e), jnp.asarray(bd)


def _ssd_chunk(tri, e, bd, z, xs_p, bm_p, cm_p, dtr, st, dtb, alog, dsk, nw):
    n = z.shape[0]
    lane128, lane512 = _iota((1, LANES), 1), _iota((1, SSD_D_INNER), 1)
    sub8 = _iota((SUBLANES, 1), 0)
    tril = _iota((n, 1), 0) >= _iota((1, n), 1)
    head_rows = jnp.where(sub8 == lane128, 1.0, 0.0)
    xs, bm, cm = _silu(xs_p), _silu(bm_p), _silu(cm_p)
    dt = jax.nn.softplus(dtr + dtb)
    da = dt * (-jnp.exp(alog))
    acs = _zdot(tri, da, NN, 0)
    acst = _zdot(head_rows, acs, NT, 0)
    xc = xs * _zdot(dt, e, NN, 1)
    tot = jnp.sum(da, 0, keepdims=True)
    y = _bdot(cm, st) * _zdot(jnp.exp(acs), e, NN, 1)
    st_new = st * _expand8(jnp.exp(tot), e) + bd * _bdot(bm, xc * _zdot(jnp.exp(tot - acs), e, NN, 1), TN)
    for g in range(2):
        cb = _bdot(jnp.where(_in_range(lane128, 64 * g, 64), cm, 0.0), bm, NT)
        for h in range(4 * g, 4 * g + 4):
            col = jnp.sum(jnp.where(lane128 == h, acs, 0.0), 1, keepdims=True)
            row = jnp.sum(jnp.where(sub8 == h, acst, 0.0), 0, keepdims=True)
            dm = jnp.where(tril, jnp.exp(jnp.minimum(col - row, 0.0)), 0.0)
            y = y + _bdot(cb * dm, jnp.where(_in_range(lane512, 64 * h, 64), xc, 0.0))
    y = (y + _expand8(dsk, e) * xs) * _silu(z)
    ysq = y * y
    g0 = lane512 < 256
    ms0 = jnp.sum(jnp.where(g0, ysq, 0.0), 1, keepdims=True) * (1.0 / 256)
    ms1 = jnp.sum(jnp.where(g0, 0.0, ysq), 1, keepdims=True) * (1.0 / 256)
    rs = jnp.where(g0, lax.rsqrt(ms0 + RMS_EPS), lax.rsqrt(ms1 + RMS_EPS))
    return y * rs * nw, st_new


SSD_STEP = 4


def _ssd_rows(u):
    return slice(SSD_CHUNK * u, SSD_CHUNK * (u + 1))


def _ssd_specs(rev, ng):
    ci = (lambda i: ng - 1 - i) if rev else (lambda i: i)
    n = SSD_STEP * SSD_CHUNK
    act = [pl.BlockSpec((n, 512), lambda i: (ci(i), AB_Z // 512)),
           pl.BlockSpec((n, 512), lambda i: (ci(i), 0)),
           pl.BlockSpec((n, LANES), lambda i: (ci(i), 4)),
           pl.BlockSpec((n, LANES), lambda i: (ci(i), 5)),
           pl.BlockSpec((n, LANES), lambda i: (ci(i), AB_DT // LANES))]
    const = [pl.BlockSpec((SSD_CHUNK, SSD_CHUNK), lambda i: (0, 0)), pl.BlockSpec((LANES, 512), lambda i: (0, 0)),
             pl.BlockSpec((LANES, 512), lambda i: (0, 0))]
    par = [pl.BlockSpec((1, LANES), lambda i: (0, 0))] * 3 + [pl.BlockSpec((1, 512), lambda i: (0, 0))]
    return ci, act, const, par


def _ssd_fwd(h, xbcc, dtb, alog, dsk, nw, name):
    s = h.shape[0]
    nc = s // SSD_CHUNK
    ng = nc // SSD_STEP
    ci, act, const, par = _ssd_specs(False, ng)

    def kern(z_ref, xs_ref, bm_ref, cm_ref, dt_ref, tri_ref, e_ref, bd_ref, dtb_ref, al_ref, dsk_ref, nw_ref,
             y_ref, sts_ref, st):
        @pl.when(pl.program_id(0) == 0)
        def _():
            st[...] = jnp.zeros_like(st)

        for u in range(SSD_STEP):
            rows = _ssd_rows(u)
            sts_ref[u] = st[...]
            y, stn = _ssd_chunk(tri_ref[...], e_ref[...], bd_ref[...], z_ref[rows, :], xs_ref[rows, :], bm_ref[rows, :],
                                cm_ref[rows, :], dt_ref[rows, :], st[...], dtb_ref[...], al_ref[...], dsk_ref[...], nw_ref[...])
            y_ref[rows, :] = y.astype(BF16)
            st[...] = stn

    return pl.pallas_call(
        kern, name=name, grid=(ng,), in_specs=act + const + par,
        out_specs=[pl.BlockSpec((SSD_STEP * SSD_CHUNK, 512), lambda i: (i, 0)),
                   pl.BlockSpec((SSD_STEP, LANES, 512), lambda i: (i, 0, 0))],
        out_shape=[jax.ShapeDtypeStruct((s, 512), BF16), jax.ShapeDtypeStruct((nc, LANES, 512), F32)],
        scratch_shapes=[pltpu.VMEM((LANES, 512), F32)], compiler_params=_cp(("arbitrary",)),
    )(h, xbcc, xbcc, xbcc, h, *_ssd_consts(), dtb, alog, dsk, nw)


def _acc(ref, val, first):
    @pl.when(first)
    def _():
        ref[...] = val

    @pl.when(jnp.logical_not(first))
    def _():
        ref[...] += val


def _ssd_bwd(dy, dy_cb, h, xbcc, sts, dtb, alog, dsk, nw, name):
    s = h.shape[0]
    ng = s // SSD_CHUNK // SSD_STEP
    n = SSD_STEP * SSD_CHUNK
    ci, act, const, par = _ssd_specs(True, ng)

    def kern(dy_ref, sts_ref, z_ref, xs_ref, bm_ref, cm_ref, dt_ref, tri_ref, e_ref, bd_ref, dtb_ref, al_ref, dsk_ref,
             nw_ref, dz_ref, dx_ref, ddt_ref, ddtb_ref, dal_ref, ddsk_ref, dnw_ref, dst):
        @pl.when(pl.program_id(0) == 0)
        def _():
            dst[...] = jnp.zeros_like(dst)
            for ref in (ddtb_ref, dal_ref, ddsk_ref, dnw_ref):
                ref[...] = jnp.zeros_like(ref)

        f = functools.partial(_ssd_chunk, tri_ref[...], e_ref[...], bd_ref[...])
        for u in range(SSD_STEP - 1, -1, -1):
            rows = _ssd_rows(u)
            _, pull = jax.vjp(f, z_ref[rows, :], xs_ref[rows, :], bm_ref[rows, :], cm_ref[rows, :], dt_ref[rows, :],
                              sts_ref[u], dtb_ref[...], al_ref[...], dsk_ref[...], nw_ref[...])
            dz, dxs, dbm, dcm, ddt, dsti, ddtb, dal, ddsk, dnw = pull((dy_ref[rows, :], dst[...]))
            dz_ref[rows, :] = dz.astype(BF16)
            dx_ref[rows, 0:512] = dxs
            dx_ref[rows, 512:640] = dbm
            dx_ref[rows, 640:768] = dcm
            ddt_ref[rows, :] = ddt.astype(BF16)
            dst[...] = dsti
            ddtb_ref[...] += ddtb
            dal_ref[...] += dal
            ddsk_ref[...] += ddsk
            dnw_ref[...] += dnw

    vec = pl.BlockSpec((1, LANES), lambda i: (0, 0))
    return pl.pallas_call(
        kern, name=name, grid=(ng,),
        in_specs=[pl.BlockSpec((n, 512), lambda i: (ci(i), dy_cb)), pl.BlockSpec((SSD_STEP, LANES, 512), lambda i: (ci(i), 0, 0))]
        + act + const + par,
        out_specs=[pl.BlockSpec((n, 512), lambda i: (ci(i), 0)), pl.BlockSpec((n, SSD_CONV_DIM), lambda i: (ci(i), 0)),
                   pl.BlockSpec((n, LANES), lambda i: (ci(i), 0)), vec, vec, vec, pl.BlockSpec((1, 512), lambda i: (0, 0))],
        out_shape=[jax.ShapeDtypeStruct((s, 512), BF16), jax.ShapeDtypeStruct((s, SSD_CONV_DIM), F32),
                   jax.ShapeDtypeStruct((s, LANES), BF16), jax.ShapeDtypeStruct((1, LANES), F32),
                   jax.ShapeDtypeStruct((1, LANES), F32), jax.ShapeDtypeStruct((1, LANES), F32),
                   jax.ShapeDtypeStruct((1, 512), F32)],
        scratch_shapes=[pltpu.VMEM((LANES, 512), F32)], compiler_params=_cp(("arbitrary",)),
    )(dy, sts, h, xbcc, xbcc, xbcc, h, *_ssd_consts(), dtb, alog, dsk, nw)


HG_LEVELS = 6


HG_SUMS = 1 + 2 * HG_LEVELS


def _hg_consts():
    n = HG_CHUNK
    t = np.arange(n)
    mats = [np.tril(np.ones((n, n), np.float32))]
    dq, dk, mk = [], [], []
    for lv in range(HG_LEVELS):
        b = 1 << lv
        start = (t // b) * b
        end = start + b - 1
        dq.append(((t[None, :] >= start[:, None]) & (t[None, :] <= t[:, None])).astype(np.float32))
        dk.append(((t[None, :] > t[:, None]) & (t[None, :] <= end[:, None])).astype(np.float32))
        mk.append((((t // b) % 2 == 1)[:, None] & ((t // b)[None, :] == (t // b)[:, None] - 1)).astype(np.float32))
    c1 = np.concatenate(mats + dq + dk, axis=0)
    return (jnp.asarray(c1, BF16), jnp.asarray(np.concatenate([c1, c1, c1], axis=1), BF16), jnp.asarray(np.stack(mk)))


@functools.partial(jax.custom_vjp, nondiff_argnums=(1, 2))
def _split_grid(x, nr, nc):
    br, bc = x.shape[0] // nr, x.shape[1] // nc
    return tuple(x[i * br:(i + 1) * br, j * bc:(j + 1) * bc] for i in range(nr) for j in range(nc))


def _split_grid_fwd(x, nr, nc):
    return _split_grid(x, nr, nc), None


def _split_grid_bwd(nr, nc, _, cts):
    rows = [cts[i * nc] if nc == 1 else jnp.concatenate(cts[i * nc:(i + 1) * nc], axis=1) for i in range(nr)]
    return (rows[0] if nr == 1 else jnp.concatenate(rows, axis=0),)


_split_grid.defvjp(_split_grid_fwd, _split_grid_bwd)


@jax.custom_vjp
def _level_sums(c1, c3, g):
    gh = g.astype(BF16)
    r1 = g - gh.astype(F32)
    gm = r1.astype(BF16)
    gl = (r1 - gm.astype(F32)).astype(BF16)
    return _dot(c3, jnp.concatenate([gh, gm, gl], axis=0), NN)


def _level_sums_fwd(c1, c3, g):
    return _level_sums(c1, c3, g), (c1, c3)


def _level_sums_bwd(res, ct):
    c1, c3 = res
    ch = ct.astype(BF16)
    cm = (ct - ch.astype(F32)).astype(BF16)
    return jnp.zeros_like(c1), jnp.zeros_like(c3), _dot(c1, ch, TN) + _dot(c1, cm, TN)


_level_sums.defvjp(_level_sums_fwd, _level_sums_bwd)


def _hg_chunk(c1, c3, masks, hq, hf, hi, hg, sts, lb, nw):
    q = _silu(hq)
    g = jnp.log(lb + (1.0 - lb) * jax.nn.sigmoid(hf))
    k = (1.0 - lb) * jax.nn.sigmoid(-hf)
    parts = _split_grid(_level_sums(c1, c3, g), HG_SUMS, 1)
    bc = parts[0]
    tot = jnp.sum(g, 0, keepdims=True)
    qe = _split_grid(q * jnp.exp(bc), 1, HG_HEADS)
    kd = _split_grid(k * jnp.exp(tot - bc), 1, HG_HEADS)
    etot = _split_grid(jnp.exp(tot), 1, HG_HEADS)
    qh, kh, vh, gh = [_split_grid(a, 1, HG_HEADS) for a in (q, k, hi, hg)]
    att = [None] * HG_HEADS
    for lv in range(HG_LEVELS):
        qt = _split_grid(q * jnp.exp(parts[1 + lv]), 1, HG_HEADS)
        kt = _split_grid(k * jnp.exp(parts[1 + HG_LEVELS + lv]), 1, HG_HEADS)
        for hd in range(HG_HEADS):
            t = masks[lv] * _bdot(qt[hd], kt[hd], NT)
            att[hd] = t if att[hd] is None else att[hd] + t
    outs, stn = [], []
    for hd in range(HG_HEADS):
        o = (_bdot(att[hd], vh[hd]) + jnp.sum(qh[hd] * kh[hd], 1, keepdims=True) * vh[hd] + _bdot(qe[hd], sts[hd], NT))
        stn.append(sts[hd] * etot[hd] + _bdot(vh[hd], kd[hd], TN))
        outs.append(o * lax.rsqrt(jnp.mean(o * o, 1, keepdims=True) + RMS_EPS) * nw * _silu(gh[hd]))
    return tuple(outs), tuple(stn)


HG_STEP = 4


def _hg_specs(rev, ng):
    ci = (lambda i: ng - 1 - i) if rev else (lambda i: i)
    n = HG_CHUNK
    act = [pl.BlockSpec((HG_STEP * n, 512), functools.partial(lambda i, c: (ci(i), c), c=c))
           for c in (AB_HQ // 512, AB_HF // 512, AB_HI // 512, AB_HG // 512)]
    other = [pl.BlockSpec((HG_SUMS * n, n), lambda i: (0, 0)), pl.BlockSpec((HG_SUMS * n, 3 * n), lambda i: (0, 0)),
             pl.BlockSpec((HG_LEVELS, n, n), lambda i: (0, 0, 0)), pl.BlockSpec((1, 512), lambda i: (0, 0)),
             pl.BlockSpec((1, LANES), lambda i: (0, 0))]
    return ci, act, other


def _head_rows(hd):
    return slice(LANES * hd, LANES * (hd + 1))


def _chunk_rows(u):
    return slice(HG_CHUNK * u, HG_CHUNK * (u + 1))


def _hg_fwd(h, lb, nw, name):
    s = h.shape[0]
    nc = s // HG_CHUNK
    ng = nc // HG_STEP
    ci, act, other = _hg_specs(False, ng)

    def kern(q_ref, f_ref, i_ref, g_ref, c1_ref, c3_ref, mk_ref, lb_ref, nw_ref, y_ref, sts_ref, st):
        @pl.when(pl.program_id(0) == 0)
        def _():
            st[...] = jnp.zeros_like(st)

        masks = [mk_ref[lv] for lv in range(HG_LEVELS)]
        for u in range(HG_STEP):
            rows = _chunk_rows(u)
            sts_ref[u] = st[...]
            outs, stn = _hg_chunk(c1_ref[...], c3_ref[...], masks, q_ref[rows, :], f_ref[rows, :], i_ref[rows, :],
                                  g_ref[rows, :], tuple(st[_head_rows(hd), :] for hd in range(HG_HEADS)), lb_ref[...],
                                  nw_ref[...])
            for hd in range(HG_HEADS):
                y_ref[rows, _head_rows(hd)] = outs[hd].astype(BF16)
                st[_head_rows(hd), :] = stn[hd]

    return pl.pallas_call(
        kern, name=name, grid=(ng,), in_specs=act + other,
        out_specs=[pl.BlockSpec((HG_STEP * HG_CHUNK, 512), lambda i: (i, 0)),
                   pl.BlockSpec((HG_STEP, 512, LANES), lambda i: (i, 0, 0))],
        out_shape=[jax.ShapeDtypeStruct((s, 512), BF16), jax.ShapeDtypeStruct((nc, 512, LANES), F32)],
        scratch_shapes=[pltpu.VMEM((512, LANES), F32)], compiler_params=_cp(("arbitrary",)),
    )(h, h, h, h, *_hg_consts(), lb, nw)


def _hg_bwd(dy, dy_cb, h, sts, lb, nw, name):
    s = h.shape[0]
    ng = s // HG_CHUNK // HG_STEP
    n = HG_STEP * HG_CHUNK
    ci, act, other = _hg_specs(True, ng)

    def kern(dy_ref, sts_ref, q_ref, f_ref, i_ref, g_ref, c1_ref, c3_ref, mk_ref, lb_ref, nw_ref,
             dq_ref, df_ref, di_ref, dg_ref, dlb_ref, dnw_ref, dst):
        @pl.when(pl.program_id(0) == 0)
        def _():
            dst[...] = jnp.zeros_like(dst)
            dlb_ref[...] = jnp.zeros_like(dlb_ref)
            dnw_ref[...] = jnp.zeros_like(dnw_ref)

        masks = [mk_ref[lv] for lv in range(HG_LEVELS)]
        heads = range(HG_HEADS)
        for u in range(HG_STEP - 1, -1, -1):
            rows = _chunk_rows(u)
            _, pull = jax.vjp(functools.partial(_hg_chunk, c1_ref[...], c3_ref[...], masks), q_ref[rows, :], f_ref[rows, :],
                              i_ref[rows, :], g_ref[rows, :], tuple(sts_ref[u, _head_rows(hd), :] for hd in heads),
                              lb_ref[...], nw_ref[...])
            dq, df, di, dg, dsti, dlb, dnw = pull((tuple(dy_ref[rows, _head_rows(hd)] for hd in heads),
                                                   tuple(dst[_head_rows(hd), :] for hd in heads)))
            dq_ref[rows, :] = dq.astype(BF16)
            df_ref[rows, :] = df.astype(BF16)
            di_ref[rows, :] = di.astype(BF16)
            dg_ref[rows, :] = dg.astype(BF16)
            for hd in heads:
                dst[_head_rows(hd), :] = dsti[hd]
            dlb_ref[...] += dlb
            dnw_ref[...] += dnw

    blk = pl.BlockSpec((n, 512), lambda i: (ci(i), 0))
    return pl.pallas_call(
        kern, name=name, grid=(ng,),
        in_specs=[pl.BlockSpec((n, 512), lambda i: (ci(i), dy_cb)), pl.BlockSpec((HG_STEP, 512, LANES), lambda i: (ci(i), 0, 0))]
        + act + other,
        out_specs=[blk, blk, blk, blk, pl.BlockSpec((1, 512), lambda i: (0, 0)), pl.BlockSpec((1, LANES), lambda i: (0, 0))],
        out_shape=[jax.ShapeDtypeStruct((s, 512), BF16)] * 4 + [jax.ShapeDtypeStruct((1, 512), F32),
                                                                jax.ShapeDtypeStruct((1, LANES), F32)],
        scratch_shapes=[pltpu.VMEM((512, LANES), F32)], compiler_params=_cp(("arbitrary",)),
    )(dy, sts, h, h, h, h, *_hg_consts(), lb, nw)


def _swa_consts():
    n = SWA_BLOCK
    r = np.arange(8 * n)
    t, blk = r % n, r // n
    lanes = np.arange(LANES)
    own = (t[:, None] >= lanes[None, :]).astype(np.float32)
    prev = (lanes[None, :] > t[:, None]).astype(np.float32)
    sel = (lanes[None, :] == (blk // 2 + 4 * (blk % 2))[:, None]).astype(np.float32)
    grp = ((lanes[None, :] // 64) == (blk % 2)[:, None]).astype(np.float32)
    return tuple(jnp.asarray(a) for a in (own, prev, sel, grp))


def _swa_block(own, prev, sel, grp, q, k0, v0, k1, v1, srow):
    qs = _split_grid(q, 1, 4)
    qa = jnp.concatenate([qs[p] for p in range(4) for _ in range(2)], axis=0) * grp
    s0 = jnp.where(prev > 0.0, _bdot(qa, k0, NT) * 0.125, MASK_VALUE)
    s1 = jnp.where(own > 0.0, _bdot(qa, k1, NT) * 0.125, MASK_VALUE)
    sk = jnp.sum(sel * srow, 1, keepdims=True)
    m = lax.stop_gradient(jnp.maximum(jnp.maximum(jnp.max(s0, 1, keepdims=True), jnp.max(s1, 1, keepdims=True)), sk))
    p0, p1 = jnp.exp(s0 - m), jnp.exp(s1 - m)
    inv = 1.0 / (jnp.sum(p0, 1, keepdims=True) + jnp.sum(p1, 1, keepdims=True) + jnp.exp(sk - m))
    o = _split_grid((_bdot(p0 * inv, v0) + _bdot(p1 * inv, v1)) * grp, 8, 1)
    return tuple(o[2 * p] + o[2 * p + 1] for p in range(4))


SWA_STEP = 4


def _swa_specs():
    n = SWA_BLOCK
    pv = lambda i: jnp.maximum(SWA_STEP * i - 1, 0)
    mask = pl.BlockSpec((8 * n, LANES), lambda i: (0, 0))
    return [mask, mask, mask, mask, pl.BlockSpec((SWA_STEP * n, 512), lambda i: (i, CD_Q // 512)),
            pl.BlockSpec((n, LANES), lambda i: (pv(i), CD_K // LANES)), pl.BlockSpec((n, LANES), lambda i: (pv(i), CD_V // LANES)),
            pl.BlockSpec((SWA_STEP * n, LANES), lambda i: (i, CD_K // LANES)),
            pl.BlockSpec((SWA_STEP * n, LANES), lambda i: (i, CD_V // LANES)), pl.BlockSpec((1, LANES), lambda i: (0, 0))]


def _swa_args(refs, u, notfirst):
    own, prev, sel, grp = (r[...] for r in refs[:4])
    q_ref, k0_ref, v0_ref, k_ref, v_ref, s_ref = refs[4:]
    rows = slice(SWA_BLOCK * u, SWA_BLOCK * (u + 1))
    if u == 0:
        k0, v0, prev = k0_ref[...], v0_ref[...], prev * notfirst.astype(F32)
    else:
        before = slice(SWA_BLOCK * (u - 1), SWA_BLOCK * u)
        k0, v0 = k_ref[before, :], v_ref[before, :]
    return (own, prev, sel, grp), (q_ref[rows, :], k0, v0, k_ref[rows, :], v_ref[rows, :], s_ref[...])


def _swa_fwd(hc, srow, name):
    s = hc.shape[0]
    n = SWA_STEP * SWA_BLOCK

    def kern(*refs):
        y_ref = refs[10]
        for u in range(SWA_STEP):
            consts, acts = _swa_args(refs[:10], u, pl.program_id(0) > 0)
            outs = _swa_block(*consts, *acts)
            for p in range(4):
                y_ref[SWA_BLOCK * u:SWA_BLOCK * (u + 1), LANES * p:LANES * (p + 1)] = outs[p].astype(BF16)

    return pl.pallas_call(
        kern, name=name, grid=(s // n,), in_specs=_swa_specs(), out_specs=pl.BlockSpec((n, 512), lambda i: (i, 0)),
        out_shape=jax.ShapeDtypeStruct((s, 512), BF16), compiler_params=_cp(("parallel",)),
    )(*_swa_consts(), hc, hc, hc, hc, hc, srow)


def _swa_bwd(dy, dy_cb, hc, srow, name):
    s = hc.shape[0]
    n = SWA_STEP * SWA_BLOCK
    nb = SWA_BLOCK

    def kern(dy_ref, *refs):
        dq_ref, dk_ref, dv_ref, ds_ref = refs[10:]
        i = pl.program_id(0)

        @pl.when(i == 0)
        def _():
            dk_ref[...] = jnp.zeros_like(dk_ref)
            dv_ref[...] = jnp.zeros_like(dv_ref)
            ds_ref[...] = jnp.zeros_like(ds_ref)

        for u in range(SWA_STEP):
            blk = SWA_STEP * i + u
            rows = slice(nb * u, nb * (u + 1))
            r1 = pl.ds(pl.multiple_of(blk * nb, nb), nb)
            r0 = pl.ds(pl.multiple_of(jnp.maximum(blk - 1, 0) * nb, nb), nb)
            consts, acts = _swa_args(refs[:10], u, i > 0)
            _, pull = jax.vjp(functools.partial(_swa_block, *consts), *acts)
            dq, dk0, dv0, dk1, dv1, dsr = pull(tuple(dy_ref[rows, LANES * p:LANES * (p + 1)] for p in range(4)))
            dq_ref[rows, :] = dq.astype(BF16)
            dk_ref[r0, :] += dk0
            dv_ref[r0, :] += dv0
            dk_ref[r1, :] += dk1
            dv_ref[r1, :] += dv1
            ds_ref[...] += dsr

    full = pl.BlockSpec((s, LANES), lambda i: (0, 0))
    return pl.pallas_call(
        kern, name=name, grid=(s // n,), in_specs=[pl.BlockSpec((n, 512), lambda i: (i, dy_cb))] + _swa_specs(),
        out_specs=[pl.BlockSpec((n, 512), lambda i: (i, 0)), full, full, pl.BlockSpec((1, LANES), lambda i: (0, 0))],
        out_shape=[jax.ShapeDtypeStruct((s, 512), BF16), jax.ShapeDtypeStruct((s, LANES), F32),
                   jax.ShapeDtypeStruct((s, LANES), F32), jax.ShapeDtypeStruct((1, LANES), F32)],
        compiler_params=_cp(("arbitrary",)),
    )(dy, *_swa_consts(), hc, hc, hc, hc, hc, srow)


RG_TILE = 512


def _neg_expm1(x):
    ser = x * (1 + x / 2 * (1 + x / 3 * (1 + x / 4 * (1 + x / 5 * (1 + x / 6 * (1 + x / 7 * (1 + x / 8)))))))
    return jnp.where(x > -0.25, -ser, 1.0 - jnp.exp(x))


def _rg_gates(xc, wa, wx, ba, bx, lam):
    r = jax.nn.sigmoid(_bdot(xc, wa) + ba)
    i = jax.nn.sigmoid(_bdot(xc, wx) + bx)
    log_a = -RG_C * r * jax.nn.softplus(-lam)
    u = jnp.sqrt(jnp.maximum(_neg_expm1(2.0 * log_a), 0.0)) * (i * xc)
    return jnp.exp(log_a), u


def _rg_out(hs, gate):
    return hs * jax.nn.gelu(gate)


def _rows_to_tile(rows):
    sub = _iota((SUBLANES, 1), 0)
    out = jnp.broadcast_to(rows[0], (SUBLANES, rows[0].shape[1]))
    for j in range(1, SUBLANES):
        out = jnp.where(sub == j, rows[j], out)
    return out


def _rg_fwd(xc, hc, wa, wx, ba, bx, lam, name):
    s = xc.shape[0]
    tm = min(RG_TILE, s)

    def kern(x_ref, g_ref, wa_ref, wx_ref, ba_ref, bx_ref, lam_ref, y_ref, h_ref, a_s, u_s, carry):
        @pl.when(pl.program_id(0) == 0)
        def _():
            carry[...] = jnp.zeros_like(carry)

        a, u = _rg_gates(x_ref[...], wa_ref[...], wx_ref[...], ba_ref[...], bx_ref[...], lam_ref[...])
        a_s[...] = a
        u_s[...] = u

        def body(gi, hp):
            r0 = pl.multiple_of(gi * SUBLANES, SUBLANES)
            ab, ub = a_s[pl.ds(r0, SUBLANES), :], u_s[pl.ds(r0, SUBLANES), :]
            rows = []
            for j in range(SUBLANES):
                hp = ab[j:j + 1] * hp + ub[j:j + 1]
                rows.append(hp)
            h_ref[pl.ds(r0, SUBLANES), :] = _rows_to_tile(rows)
            return hp

        carry[0:1, :] = lax.fori_loop(0, tm // SUBLANES, body, carry[0:1, :])
        y_ref[...] = _rg_out(h_ref[...], g_ref[...]).astype(BF16)

    row = pl.BlockSpec((tm, 512), lambda i: (i, 0))
    mat = pl.BlockSpec((512, 512), lambda i: (0, 0))
    vec = pl.BlockSpec((1, 512), lambda i: (0, 0))
    return pl.pallas_call(
        kern, name=name, grid=(s // tm,),
        in_specs=[row, pl.BlockSpec((tm, 512), lambda i: (i, CD_GATE // 512)), mat, mat, vec, vec, vec],
        out_specs=[row, row],
        out_shape=[jax.ShapeDtypeStruct((s, 512), BF16), jax.ShapeDtypeStruct((s, 512), F32)],
        scratch_shapes=[pltpu.VMEM((tm, 512), F32), pltpu.VMEM((tm, 512), F32), pltpu.VMEM((SUBLANES, 512), F32)],
        compiler_params=_cp(("arbitrary",)),
    )(xc, hc, wa, wx, ba, bx, lam)


def _rg_bwd(dy, dy_cb, xc, hc, hs, wa, wx, ba, bx, lam, name):
    s = xc.shape[0]
    tm = min(RG_TILE, s)
    nt = s // tm
    ti = lambda i: nt - 1 - i

    def kern(dy_ref, x_ref, g_ref, h_ref, hp_ref, wa_ref, wx_ref, ba_ref, bx_ref, lam_ref,
             dx_ref, dg_ref, dwa_ref, dwx_ref, dba_ref, dbx_ref, dlam_ref, a_s, d_s, g_s, cg, ca):
        first = pl.program_id(0) == 0

        @pl.when(first)
        def _():
            cg[...] = jnp.zeros_like(cg)
            ca[...] = jnp.zeros_like(ca)

        (a, _), pull = jax.vjp(_rg_gates, x_ref[...], wa_ref[...].astype(F32), wx_ref[...].astype(F32), ba_ref[...],
                               bx_ref[...], lam_ref[...])
        hs_t = h_ref[...]
        _, pull_out = jax.vjp(_rg_out, hs_t, g_ref[...])
        dho, dgate = pull_out(dy_ref[...])
        dg_ref[...] = dgate.astype(BF16)
        a_s[...] = a
        d_s[...] = dho

        def body(k, c):
            gn, an = c
            r0 = pl.multiple_of((tm // SUBLANES - 1 - k) * SUBLANES, SUBLANES)
            ab, db = a_s[pl.ds(r0, SUBLANES), :], d_s[pl.ds(r0, SUBLANES), :]
            rows = [None] * SUBLANES
            for j in range(SUBLANES - 1, -1, -1):
                gn = db[j:j + 1] + an * gn
                an = ab[j:j + 1]
                rows[j] = gn
            g_s[pl.ds(r0, SUBLANES), :] = _rows_to_tile(rows)
            return gn, an

        gn, an = lax.fori_loop(0, tm // SUBLANES, body, (cg[0:1, :], ca[0:1, :]))
        cg[0:1, :] = gn
        ca[0:1, :] = an
        g = g_s[...]
        halo = jnp.where(pl.program_id(0) == nt - 1, 0.0, hp_ref[...])
        hprev = pltpu.roll(jnp.concatenate([halo, hs_t], axis=0), 1, axis=0)[SUBLANES:]
        dxc, dwa, dwx, dba, dbx, dlam = pull((g * hprev, g))
        dx_ref[...] = dxc
        _acc(dwa_ref, dwa, first)
        _acc(dwx_ref, dwx, first)
        _acc(dba_ref, dba, first)
        _acc(dbx_ref, dbx, first)
        _acc(dlam_ref, dlam, first)

    row = pl.BlockSpec((tm, 512), lambda i: (ti(i), 0))
    mat = pl.BlockSpec((512, 512), lambda i: (0, 0))
    vec = pl.BlockSpec((1, 512), lambda i: (0, 0))
    hprev = pl.BlockSpec((SUBLANES, 512), lambda i: (jnp.maximum(ti(i) * (tm // SUBLANES) - 1, 0), 0))
    return pl.pallas_call(
        kern, name=name, grid=(nt,),
        in_specs=[pl.BlockSpec((tm, 512), lambda i: (ti(i), dy_cb)), row,
                  pl.BlockSpec((tm, 512), lambda i: (ti(i), CD_GATE // 512)), row, hprev, mat, mat, vec, vec, vec],
        out_specs=[row, row, mat, mat, vec, vec, vec],
        out_shape=[jax.ShapeDtypeStruct((s, 512), F32), jax.ShapeDtypeStruct((s, 512), BF16),
                   jax.ShapeDtypeStruct((512, 512), F32), jax.ShapeDtypeStruct((512, 512), F32),
                   jax.ShapeDtypeStruct((1, 512), F32), jax.ShapeDtypeStruct((1, 512), F32), jax.ShapeDtypeStruct((1, 512), F32)],
        scratch_shapes=[pltpu.VMEM((tm, 512), F32)] * 3 + [pltpu.VMEM((SUBLANES, 512), F32)] * 2,
        compiler_params=_cp(("arbitrary",)),
    )(dy, xc, hc, hs, hs, wa, wx, ba, bx, lam)


AB_REF = dict(z=(0, 512), xbc=(512, 768), dt=(1280, 8), hq=(1288, 512), hf=(1800, 512), hi=(2312, 512), hg=(2824, 512))
AB_INT = dict(z=AB_Z, hq=AB_HQ, hf=AB_HF, hi=AB_HI, hg=AB_HG, xbc=AB_XBC, dt=AB_DT)
CD_REF = dict(q=(0, 512), k=(512, 128), v=(640, 128), gate=(768, 512), xr=(1280, 512))


def _cols(w, start, size):
    return lax.slice_in_dim(w, start, start + size, axis=w.ndim - 1)


def _pad_lanes(v):
    return jnp.concatenate([v.astype(F32), jnp.zeros((LANES - v.shape[0],), F32)]).reshape(1, LANES)


def _taker(g, layer, axis):
    width = g.shape[2 + axis]

    def take(start, size):
        out = []
        while size > 0:
            k, off = divmod(start, width)
            n = min(size, width - off)
            out.append(lax.slice_in_dim(g[k, layer], off, off + n, axis=axis))
            start, size = start + n, size - n
        return out

    return take


def _ab_in_to_int(g, layer):
    take = _taker(g, layer, 1)
    parts = sum([take(*AB_REF[k]) for k in ("z", "hq", "hf", "hi", "hg", "xbc", "dt")], [])
    return jnp.concatenate(parts + [jnp.zeros((g.shape[2], AB_COLS - AB_DT - 8), g.dtype)], axis=1)


def _ab_in_to_ref(w):
    return jnp.concatenate([_cols(w, AB_INT[k], AB_REF[k][1]) for k in ("z", "xbc", "dt", "hq", "hf", "hi", "hg")], axis=-1)


def _pairs_to_ref(w, axis):
    sl = lambda i: lax.slice_in_dim(w, 64 * i, 64 * i + 64, axis=axis)
    return jnp.concatenate([sl(2 * h) for h in range(4)] + [sl(2 * h + 1) for h in range(4)], axis=axis)


PAIR_ORDER = tuple(h for p in range(4) for h in (p, 4 + p))


def _cd_in_to_int(g, layer):
    take = _taker(g, layer, 1)
    q = sum([take(64 * h, 64) for h in PAIR_ORDER], [])
    return jnp.concatenate(q + sum([take(*CD_REF[k]) for k in ("gate", "xr", "k", "v")], []), axis=1)


def _cd_out_to_int(g, layer):
    take = _taker(g, layer, 0)
    return jnp.concatenate(sum([take(64 * h, 64) for h in PAIR_ORDER], []) + take(512, 512), axis=0)


def _whole(g, layer, axis):
    return jnp.concatenate([g[k, layer] for k in range(4)], axis=axis)


def _cd_in_to_ref(w):
    q = _pairs_to_ref(_cols(w, CD_Q, 512), w.ndim - 1)
    return jnp.concatenate([q, _cols(w, CD_K, 128), _cols(w, CD_V, 128), _cols(w, CD_GATE, 512), _cols(w, CD_XR, 512)], axis=-1)


def _block_diag(w):
    tiled = jnp.concatenate([w.reshape(RG_WIDTH, 64)] * 8, axis=1)
    return tiled * _BLOCK_DIAG_MASK.astype(w.dtype)


_BLOCK_DIAG_MASK = np.kron(np.eye(8, dtype=np.float32), np.ones((64, 64), np.float32))


def _diag_blocks(m):
    return jnp.sum((m * _BLOCK_DIAG_MASK).reshape(8, 64, 8, 64), axis=2)


def _conv8(w):
    return jnp.concatenate([w, jnp.zeros((SUBLANES - w.shape[0], w.shape[1]), w.dtype)], axis=0)


def _hg_lower_bounds(hg_lower):
    sm = jax.nn.softmax(hg_lower.astype(F32), axis=0)
    return jnp.clip(jnp.cumsum(sm, axis=0) - sm[0], 0.0, 1.0)


def _local_step(x, target, w):
    s = x.shape[0]
    gup = w["ffn_w_up"]
    wdowns = [_whole(w["ffn_w_down"], l, 0) for l in range(DEPTH)]
    lb_all, lb_pull = jax.vjp(_hg_lower_bounds, w["hg_lower"])
    grads = {k: [None] * (v.shape[1] if k in BIG else v.shape[0]) for k, v in w.items()}
    saved = []
    xb = x.astype(BF16)
    for l in range(DEPTH):
        j = l // 2
        t = f"l{l}"
        sv = dict(x=x, xb=xb)
        if l % 2 == 0:
            win = _ab_in_to_int(w["ab_w_in"], j)
            wout = _whole(w["ab_w_out"], j, 0)
            h = _mm([(xb, 0, win, 0, D_MODEL)], AB_COLS, F32, t + "_in")
            w8 = _conv8(w["ssd_conv_w"][j])
            xbcc = _dwconv_fwd(h, AB_XBC // CONV_CB, SSD_CONV_DIM, w8, w["ssd_conv_b"][j], 4, t + "_conv")
            sp = (_pad_lanes(w["ssd_dt_bias"][j]), _pad_lanes(w["ssd_a_log"][j]), _pad_lanes(w["ssd_d"][j]),
                  w["ssd_norm_w"][j].reshape(1, 512))
            ya, ssts = _ssd_fwd(h, xbcc, *sp, t + "_ssd")
            hp = (lb_all[j].reshape(1, 512), w["hg_norm_w"][j].reshape(1, LANES))
            yb, hsts = _hg_fwd(h, *hp, t + "_hg")
            sv.update(win=win, wout=wout, h=h, w8=w8, xbcc=xbcc, sp=sp, ssts=ssts, hp=hp, hsts=hsts)
        else:
            win = _cd_in_to_int(w["cd_w_in"], j)
            wout = _cd_out_to_int(w["cd_w_out"], j)
            h = _mm([(xb, 0, win, 0, D_MODEL)], CD_COLS, F32, t + "_in")
            srow = _pad_lanes(w["swa_sinks"][j])
            ya = _swa_fwd(h, srow, t + "_swa")
            w8 = _conv8(w["rg_conv_w"][j])
            xc = _dwconv_fwd(h, CD_XR // CONV_CB, RG_WIDTH, w8, w["rg_conv_b"][j], 4, t + "_conv")
            rp = (_block_diag(w["rg_wa"][j]).astype(BF16), _block_diag(w["rg_wx"][j]).astype(BF16),
                  w["rg_ba"][j].reshape(1, 512), w["rg_bx"][j].reshape(1, 512), w["rg_lambda"][j].reshape(1, 512))
            yb, hs = _rg_fwd(xc, h, *rp, t + "_rg")
            sv.update(win=win, wout=wout, h=h, w8=w8, srow=srow, xc=xc, rp=rp, hs=hs)
        x1, x1b, r1 = _mm_ln([(ya, 0, wout, 0, 512), (yb, 0, wout, 1, 512)], x, w["ln_g"][l, 0], w["ln_b"][l, 0], t + "_out_ln")
        wdown = wdowns[l]
        fw8 = _conv8(w["ffn_conv_w"][l])
        hu = _mm([(x1b, 0, gup, _shard_as_col_tile(gup, l), D_MODEL)], 2 * FFN_DIM, BF16, t + "_up")
        a, hg, hv = _ffn_gate_fwd(hu, fw8, w["ffn_conv_b"][l], t + "_gate")
        x, xb, r2 = _mm_ln([(a, 0, wdown, 0, FFN_DIM)], x1, w["ln_g"][l, 1], w["ln_b"][l, 1], t + "_down_ln")
        sv.update(ya=ya, yb=yb, r1=r1, x1b=x1b, hu=hu, hg=hg, hv=hv, a=a, r2=r2, fw8=fw8)
        saved.append(sv)

    loss = 0.5 * jnp.sum(_sqerr(x, target, "sqerr")) / D_MODEL
    dya, dyb, ca, cb = x, target, 1.0 / D_MODEL, -1.0 / D_MODEL
    for l in range(DEPTH - 1, -1, -1):
        j = l // 2
        t = f"l{l}b"
        sv = saved[l]
        wdown = wdowns[l]
        dr2, dr2b, dg2, db2, da = _ln_bwd(dya, dyb, ca, cb, sv["r2"], w["ln_g"][l, 1], wdown, BF16, t + "_ln2_down")
        grads["ffn_w_down"][l] = _mm_tn(sv["a"], 0, FFN_DIM, dr2b, 0, D_MODEL, t + "_wdown")
        dhg, dhu, dfw8, dfb = _ffn_gate_bwd(sv["hu"], sv["hg"], sv["hv"], da, sv["fw8"], t + "_gate")
        grads["ffn_conv_w"][l] = dfw8[:3]
        grads["ffn_conv_b"][l] = dfb
        quarter = gup.shape[3]
        dx1 = _mm([(dh, half, gup, _shard_transposed(gup, 2 * gi + half, l), quarter)
                   for gi, dh in enumerate((dhg, dhu)) for half in range(2)], D_MODEL, F32, t + "_up", trans_b=True)
        grads["ffn_w_up"][l] = jnp.concatenate([_mm_tn(sv["x1b"], 0, D_MODEL, dhg, 0, FFN_DIM, t + "_wup_g"),
                                                _mm_tn(sv["x1b"], 0, D_MODEL, dhu, 0, FFN_DIM, t + "_wup_u")], axis=1)
        win, wout, h = sv["win"], sv["wout"], sv["h"]
        dr1, dr1b, dg1, db1, dycat = _ln_bwd(dr2, dx1, ALPHA, 1.0, sv["r1"], w["ln_g"][l, 0], wout, F32, t + "_ln1_out")
        grads["ln_g"][l] = jnp.stack([dg1, dg2])
        grads["ln_b"][l] = jnp.stack([db1, db2])
        dwout = jnp.concatenate([_mm_tn(sv["ya"], 0, 512, dr1b, 0, D_MODEL, t + "_wout_a"),
                                 _mm_tn(sv["yb"], 0, 512, dr1b, 0, D_MODEL, t + "_wout_b")], axis=0)
        if l % 2 == 0:
            dz, dxbcc, ddt, ddtb, dal, ddsk, dnw = _ssd_bwd(dycat, 0, h, sv["xbcc"], sv["ssts"], *sv["sp"], t + "_ssd")
            dxbc, dcw, dcb = _dwconv_bwd(dxbcc, h, AB_XBC // CONV_CB, SSD_CONV_DIM, sv["w8"], 4, t + "_conv")
            dq, df, di, dg, dlb, dhnw = _hg_bwd(dycat, 1, h, sv["hsts"], *sv["hp"], t + "_hg")
            grads["ab_w_out"][j] = dwout
            grads["ssd_conv_w"][j], grads["ssd_conv_b"][j] = dcw[:4], dcb
            grads["ssd_dt_bias"][j], grads["ssd_a_log"][j], grads["ssd_d"][j] = ddtb[0, :8], dal[0, :8], ddsk[0, :8]
            grads["ssd_norm_w"][j], grads["hg_norm_w"][j] = dnw[0], dhnw[0]
            grads["hg_lower"][j] = dlb[0]
            pieces = [(dz, 0, AB_Z, 512), (dq, 0, AB_HQ, 512), (df, 0, AB_HF, 512), (di, 0, AB_HI, 512), (dg, 0, AB_HG, 512),
                      (dxbc, 0, AB_XBC, 256), (dxbc, 1, AB_XBC + 256, 256), (dxbc, 2, AB_XBC + 512, 256), (ddt, 0, AB_DT, LANES)]
            dwin = [_mm_tn(sv["xb"], 0, D_MODEL, p, 0, p.shape[1], t + f"_win{i}")
                    for i, p in enumerate((dz, dq, df, di, dg, dxbc, ddt))]
            grads["ab_w_in"][j] = _ab_in_to_ref(jnp.concatenate(dwin, axis=1))
        else:
            dq, dk, dv, dsr = _swa_bwd(dycat, 0, h, sv["srow"], t + "_swa")
            dxc, dgate, dwa, dwx, dba, dbx, dlam = _rg_bwd(dycat, 1, sv["xc"], h, sv["hs"], *sv["rp"], t + "_rg")
            dxr, dcw, dcb = _dwconv_bwd(dxc, h, CD_XR // CONV_CB, RG_WIDTH, sv["w8"], 4, t + "_conv")
            grads["cd_w_out"][j] = jnp.concatenate([_pairs_to_ref(dwout[:512], 0), dwout[512:]], axis=0)
            grads["swa_sinks"][j] = dsr[0, :8]
            grads["rg_conv_w"][j], grads["rg_conv_b"][j] = dcw[:4], dcb
            grads["rg_wa"][j], grads["rg_wx"][j] = _diag_blocks(dwa), _diag_blocks(dwx)
            grads["rg_ba"][j], grads["rg_bx"][j], grads["rg_lambda"][j] = dba[0], dbx[0], dlam[0]
            pieces = [(dq, 0, CD_Q, 512), (dgate, 0, CD_GATE, 512), (dxr, 0, CD_XR, 512), (dk, 0, CD_K, LANES), (dv, 0, CD_V, LANES)]
            dwin = [_mm_tn(sv["xb"], 0, D_MODEL, p, 0, p.shape[1], t + f"_win{i}")
                    for i, p in enumerate((dq, dgate, dxr, dk, dv))]
            grads["cd_w_in"][j] = _cd_in_to_ref(jnp.concatenate(dwin, axis=1))
        dx0 = _mm([(p, pcb, win, off // k, k) for p, pcb, off, k in pieces], D_MODEL, F32, t + "_in", trans_b=True)
        dya, dyb, ca, cb = dr1, dx0, ALPHA, 1.0
    grad_x = _axpby(dya, dyb, ca, cb, "grad_x")
    out = {k: (v if k in BIG else jnp.stack(v)) for k, v in grads.items()}
    (out["hg_lower"],) = lb_pull(out["hg_lower"])
    return loss, grad_x, out


MESH = pl.DeviceIdType.MESH
ANY = pl.BlockSpec(memory_space=pl.ANY)


def _place():
    x, y, c = lax.axis_index("x"), lax.axis_index("y"), lax.axis_index("c")
    return x, y, c, [(1 - x, y), (x, 1 - y), (1 - x, 1 - y)]


def _rcopy(src, dst, send_sems, recv_sems, k, to):
    return pltpu.make_async_remote_copy(src_ref=src, dst_ref=dst, send_sem=send_sems.at[k], recv_sem=recv_sems.at[k],
                                        device_id=to, device_id_type=MESH)


def _zeros_index(ndim):
    return (0,) * ndim


def _gather_chips(parts, name):
    n = len(parts)

    def body(*refs):
        w_refs, out_refs, (send_sems, recv_sems) = refs[:n], refs[n:2 * n], refs[2 * n:]
        x, y, c, chips = _place()
        me, sib = 2 * x + y, (x, y, 1 - c)
        first = [_rcopy(w_refs[a].at[c], out_refs[a].at[me, c], send_sems, recv_sems, 6 * a + k, (cx, cy, c))
                 for a in range(n) for k, (cx, cy) in enumerate(chips)]
        for cp in first:
            cp.start()
        passed = []
        for a in range(n):
            for k, (cx, cy) in enumerate(chips):
                blk = out_refs[a].at[2 * cx + cy, c]
                _rcopy(blk, blk, send_sems, recv_sems, 6 * a + k, (cx, cy, c)).wait_recv()
                passed.append(_rcopy(blk, blk, send_sems, recv_sems, 6 * a + 3 + k, sib))
                passed[-1].start()
        for a in range(n):
            for k, (cx, cy) in enumerate(chips):
                blk = out_refs[a].at[2 * cx + cy, 1 - c]
                _rcopy(blk, blk, send_sems, recv_sems, 6 * a + 3 + k, sib).wait_recv()
        for cp in first + passed:
            cp.wait_send()

    outs = pl.pallas_call(
        body, name=name, in_specs=[ANY] * n, out_specs=[ANY] * n,
        out_shape=[jax.ShapeDtypeStruct((4,) + p.shape, p.dtype) for p in parts],
        scratch_shapes=[pltpu.SemaphoreType.DMA((6 * n,)), pltpu.SemaphoreType.DMA((6 * n,))],
    )(*parts)
    me = 2 * lax.axis_index("x") + lax.axis_index("y")
    return [lax.dynamic_update_slice(o, p[None], (me,) + _zeros_index(p.ndim)) for o, p in zip(outs, parts)]


def _swap_halves(gs, name):
    n = len(gs)

    def body(*refs):
        g_refs, x_refs, (send_sems, recv_sems) = refs[:n], refs[n:2 * n], refs[2 * n:]
        x, y, c, _ = _place()
        cps = [_rcopy(g_refs[a].at[:, 1 - c], x_refs[a], send_sems, recv_sems, a, (x, y, 1 - c)) for a in range(n)]
        for cp in cps:
            cp.start()
        for cp in cps:
            cp.wait()

    return pl.pallas_call(
        body, name=name, in_specs=[ANY] * n, out_specs=[ANY] * n,
        out_shape=[jax.ShapeDtypeStruct((g.shape[0],) + g.shape[2:], g.dtype) for g in gs],
        scratch_shapes=[pltpu.SemaphoreType.DMA((n,)), pltpu.SemaphoreType.DMA((n,))],
    )(*gs)


def _scatter_chips(hbs, name):
    n = len(hbs)

    def body(*refs):
        h_refs, y_refs, (send_sems, recv_sems) = refs[:n], refs[n:2 * n], refs[2 * n:]
        x, y, c, chips = _place()
        me = 2 * x + y
        sends = [_rcopy(h_refs[a].at[2 * cx + cy], y_refs[a].at[me], send_sems, recv_sems, 3 * a + k, (cx, cy, c))
                 for a in range(n) for k, (cx, cy) in enumerate(chips)]
        for cp in sends:
            cp.start()
        for a in range(n):
            for k, (cx, cy) in enumerate(chips):
                blk = y_refs[a].at[2 * cx + cy]
                _rcopy(blk, blk, send_sems, recv_sems, 3 * a + k, (cx, cy, c)).wait_recv()
        for cp in sends:
            cp.wait_send()

    outs = pl.pallas_call(
        body, name=name, in_specs=[ANY] * n, out_specs=[ANY] * n,
        out_shape=[jax.ShapeDtypeStruct(h.shape, h.dtype) for h in hbs],
        scratch_shapes=[pltpu.SemaphoreType.DMA((3 * n,)), pltpu.SemaphoreType.DMA((3 * n,))],
    )(*hbs)
    me = 2 * lax.axis_index("x") + lax.axis_index("y")
    return [lax.dynamic_update_slice(o, lax.dynamic_slice_in_dim(h, me, 1, axis=0), (me,) + _zeros_index(h.ndim - 1))
            for o, h in zip(outs, hbs)]


def _share_half(fs, name):
    n = len(fs)

    def body(*refs):
        f_refs, out_refs, (send_sems, recv_sems) = refs[:n], refs[n:2 * n], refs[2 * n:]
        x, y, c, _ = _place()
        sib = (x, y, 1 - c)
        cps = [_rcopy(f_refs[a], out_refs[a].at[c], send_sems, recv_sems, a, sib) for a in range(n)]
        for cp in cps:
            cp.start()
        for a in range(n):
            _rcopy(f_refs[a], out_refs[a].at[1 - c], send_sems, recv_sems, a, sib).wait_recv()
        for cp in cps:
            cp.wait_send()

    outs = pl.pallas_call(
        body, name=name, in_specs=[ANY] * n, out_specs=[ANY] * n,
        out_shape=[jax.ShapeDtypeStruct((2,) + f.shape, f.dtype) for f in fs],
        scratch_shapes=[pltpu.SemaphoreType.DMA((n,)), pltpu.SemaphoreType.DMA((n,))],
    )(*fs)
    c = lax.axis_index("c")
    return [lax.dynamic_update_slice(o, f[None], (c,) + _zeros_index(f.ndim)) for o, f in zip(outs, fs)]


def _gather_devices(v, name):
    m_per, n = v.shape

    def body(x_ref, out_ref, send_sems, recv_sems, local_sem):
        x, y, c, chips = _place()
        me, sib = (x, y, c), (x, y, 1 - c)

        def rows(px, py, pc):
            return out_ref.at[pl.ds((4 * px + 2 * py + pc) * m_per, m_per), :]

        def copy(k, block, to, src=None):
            return _rcopy(rows(*block) if src is None else src, rows(*block), send_sems, recv_sems, k, to)

        mine = pltpu.make_async_copy(x_ref, rows(*me), local_sem)
        mine.start()
        first = [copy(0, me, sib, src=x_ref)]
        first += [copy(1 + j, me, (*chip, c), src=x_ref) for j, chip in enumerate(chips)]
        for cp in first:
            cp.start()
        passed = [copy(4 + j, (*chip, c), sib) for j, chip in enumerate(chips)]
        for j, chip in enumerate(chips):
            copy(1 + j, (*chip, c), me).wait_recv()
            passed[j].start()
        copy(0, sib, me).wait_recv()
        for j, chip in enumerate(chips):
            copy(4 + j, (*chip, 1 - c), me).wait_recv()
        for cp in first + passed:
            cp.wait_send()
        mine.wait()

    return pl.pallas_call(
        body, name=name, out_shape=jax.ShapeDtypeStruct((8 * m_per, n), v.dtype),
        in_specs=[pl.BlockSpec(memory_space=pltpu.VMEM)], out_specs=pl.BlockSpec(memory_space=pltpu.VMEM),
        scratch_shapes=[pltpu.SemaphoreType.DMA((7,)), pltpu.SemaphoreType.DMA((7,)), pltpu.SemaphoreType.DMA(())],
        compiler_params=_cp(),
    )(v)


ROW_TILES = (512, 256, 352, 128)


def _add_half(g, xr, c, name):
    _, _, r, n = g.shape
    tr = _pick(r, ROW_TILES)

    def kern(c_ref, g_ref, x_ref, o_ref):
        o_ref[...] = (g_ref[0] + x_ref[...]).astype(BF16)

    return pl.pallas_call(
        kern, name=name,
        grid_spec=pltpu.PrefetchScalarGridSpec(
            num_scalar_prefetch=1, grid=(4, r // tr),
            in_specs=[pl.BlockSpec((1, 1, tr, n), lambda k, i, c_ref: (k, c_ref[0], i, 0)),
                      pl.BlockSpec((1, tr, n), lambda k, i, c_ref: (k, i, 0))],
            out_specs=pl.BlockSpec((1, tr, n), lambda k, i, c_ref: (k, i, 0))),
        out_shape=jax.ShapeDtypeStruct(xr.shape, BF16), compiler_params=_cp(("parallel", "parallel")),
    )(c.reshape(1).astype(jnp.int32), g, xr)


def _sum_blocks(y, name, tr=None):
    n, r, cols = y.shape
    tr = tr or _pick(r, ROW_TILES)
    tr = tr if r % tr == 0 else r

    def kern(y_ref, o_ref):
        acc = y_ref[0].astype(F32)
        for k in range(1, n):
            acc = acc + y_ref[k].astype(F32)
        o_ref[...] = acc

    return pl.pallas_call(
        kern, name=name, grid=(r // tr,), in_specs=[pl.BlockSpec((n, tr, cols), lambda i: (0, i, 0))],
        out_specs=pl.BlockSpec((tr, cols), lambda i: (i, 0)), out_shape=jax.ShapeDtypeStruct((r, cols), F32),
        compiler_params=_cp(("parallel",)),
    )(y)


def _adamw(w, g, m, v, name, tr=256):
    r, n = w.shape
    tr = tr if r % tr == 0 else r

    def kern(w_ref, g_ref, m_ref, v_ref, d_ref, nm_ref, nv_ref):
        gg = g_ref[...]
        nm = ADAM_B1 * m_ref[...] + (1.0 - ADAM_B1) * gg
        nv = ADAM_B2 * v_ref[...] + (1.0 - ADAM_B2) * (gg * gg)
        m_hat = nm / (1.0 - ADAM_B1 ** ADAM_STEP)
        v_hat = nv / (1.0 - ADAM_B2 ** ADAM_STEP)
        d_ref[...] = -ADAM_LR * (m_hat / (jnp.sqrt(v_hat) + ADAM_EPS) + ADAM_WD * w_ref[...])
        nm_ref[...] = nm
        nv_ref[...] = nv

    blk = pl.BlockSpec((tr, n), lambda i: (i, 0))
    return pl.pallas_call(
        kern, name=name, grid=(r // tr,), in_specs=[blk] * 4, out_specs=[blk] * 3,
        out_shape=[jax.ShapeDtypeStruct((r, n), F32)] * 3, compiler_params=_cp(("parallel",)),
    )(w, g, m, v)


def _rows_of(n):
    return -(-n // (LANES * SUBLANES)) * SUBLANES


def _pack(arrs, lead, dtype, mult):
    ls = arrs[0].shape[:lead]
    blocks, total = [], 0
    for a in arrs:
        f = a.astype(dtype).reshape(ls + (-1,))
        n = f.shape[-1]
        rows = _rows_of(n)
        if rows * LANES != n:
            f = jnp.concatenate([f, jnp.zeros(ls + (rows * LANES - n,), dtype)], axis=-1)
        blocks.append(f.reshape(ls + (rows, LANES)))
        total += rows
    if total % mult:
        blocks.append(jnp.zeros(ls + (mult - total % mult, LANES), dtype))
    return jnp.concatenate(blocks, axis=lead)


def _unpack(buf, lead, shapes):
    ls = buf.shape[:lead]
    out, off = [], 0
    for sh in shapes:
        n = int(np.prod(sh))
        rows = _rows_of(n)
        flat = lax.slice_in_dim(buf, off, off + rows, axis=lead).reshape(ls + (rows * LANES,))
        if rows * LANES != n:
            flat = lax.slice_in_dim(flat, 0, n, axis=lead)
        out.append(flat.reshape(ls + tuple(sh)))
        off += rows
    return out


BIG = ("ab_w_in", "ab_w_out", "cd_w_in", "cd_w_out", "ffn_w_up", "ffn_w_down")
BIG_COLS = ("ab_w_in", "cd_w_in", "ffn_w_up")
SMALL_SHARDED = ("ssd_conv_w", "rg_conv_w", "rg_conv_b", "rg_ba", "rg_bx", "rg_lambda", "ffn_conv_w", "ln_g", "ln_b")
WEIGHTS = ("ab_w_in", "ssd_conv_w", "ssd_conv_b", "ssd_dt_bias", "ssd_a_log", "ssd_d", "ssd_norm_w", "hg_lower", "hg_norm_w",
           "ab_w_out", "cd_w_in", "swa_sinks", "rg_conv_w", "rg_conv_b", "rg_wa", "rg_ba", "rg_wx", "rg_bx", "rg_lambda",
           "cd_w_out", "ffn_w_up", "ffn_conv_w", "ffn_conv_b", "ffn_w_down", "ln_g", "ln_b")
SMALL = tuple(n for n in WEIGHTS if n not in BIG)


def _full_from_shards(name, g):
    if name in BIG_COLS or name in SMALL_SHARDED:
        return jnp.concatenate([g[k] for k in range(4)], axis=-1)
    return jnp.concatenate([g[k] for k in range(4)], axis=1)


def _layer_shards(name, f):
    k, n = f.shape
    if name in BIG_COLS:
        return jnp.transpose(f.reshape(k, 4, n // 4), (1, 0, 2))
    return f.reshape(4, k // 4, n)


def kernel(x, ab_w_in, ssd_conv_w, ssd_conv_b, ssd_dt_bias, ssd_a_log, ssd_d, ssd_norm_w, hg_lower, hg_norm_w, ab_w_out, cd_w_in, swa_sinks, rg_conv_w, rg_conv_b, rg_wa, rg_ba, rg_wx, rg_bx, rg_lambda, cd_w_out, ffn_w_up, ffn_conv_w, ffn_conv_b, ffn_w_down, ln_g, ln_b, loss_target, m_ab_w_in, m_ssd_conv_w, m_ssd_conv_b, m_ssd_dt_bias, m_ssd_a_log, m_ssd_d, m_ssd_norm_w, m_hg_lower, m_hg_norm_w, m_ab_w_out, m_cd_w_in, m_swa_sinks, m_rg_conv_w, m_rg_conv_b, m_rg_wa, m_rg_ba, m_rg_wx, m_rg_bx, m_rg_lambda, m_cd_w_out, m_ffn_w_up, m_ffn_conv_w, m_ffn_conv_b, m_ffn_w_down, m_ln_g, m_ln_b, v_ab_w_in, v_ssd_conv_w, v_ssd_conv_b, v_ssd_dt_bias, v_ssd_a_log, v_ssd_d, v_ssd_norm_w, v_hg_lower, v_hg_norm_w, v_ab_w_out, v_cd_w_in, v_swa_sinks, v_rg_conv_w, v_rg_conv_b, v_rg_wa, v_rg_ba, v_rg_wx, v_rg_bx, v_rg_lambda, v_cd_w_out, v_ffn_w_up, v_ffn_conv_w, v_ffn_conv_b, v_ffn_w_down, v_ln_g, v_ln_b):
    args = locals()
    w = {n: args[n] for n in WEIGHTS}
    mom = {n: args["m_" + n] for n in WEIGHTS}
    var = {n: args["v_" + n] for n in WEIGHTS}
    cx, cy, cc = lax.axis_index("x"), lax.axis_index("y"), lax.axis_index("c")
    chip = 2 * cx + cy

    halves = [w[n].astype(BF16).reshape((2, w[n].shape[0] // 2) + w[n].shape[1:]) for n in BIG]
    gathered = _gather_chips(halves, "gather_weights")
    full = {n: g.reshape((4,) + w[n].shape) for n, g in zip(BIG, gathered)}
    sp = _pack([w[n] for n in SMALL_SHARDED], 0, F32, SUBLANES)
    sg = _gather_devices(sp, "gather_small").reshape(4, 2, sp.shape[0], LANES)[:, 0]
    for n, g in zip(SMALL_SHARDED, _unpack(sg, 1, [w[n].shape for n in SMALL_SHARDED])):
        full[n] = _full_from_shards(n, g)
    for n in SMALL:
        full.setdefault(n, w[n])

    loss, grad_x, grads = _local_step(x[0], loss_target[0], full)
    loss = lax.psum(loss, ("x", "y", "c"))

    g4 = []
    for n in BIG:
        st = jnp.stack([_layer_shards(n, f) for f in grads[n]], axis=1)
        g4.append(st.reshape(4, 2, st.shape[1] // 2 * st.shape[2], st.shape[3]))
    hbs = [_add_half(g, xr, cc, "add_halves_" + n) for n, g, xr in zip(BIG, g4, _swap_halves(g4, "swap_halves"))]
    own = [_sum_blocks(y, "sum_chips_" + n) for n, y in zip(BIG, _scatter_chips(hbs, "scatter_chips"))]
    gout = {n: o.reshape(w[n].shape) for n, o in zip(BIG, _share_half(own, "share_half"))}

    small_shapes = [grads[n].shape for n in SMALL]
    gs = _pack([grads[n] for n in SMALL], 0, F32, SUBLANES)
    gsum = _sum_blocks(_gather_devices(gs, "gather_small_grads").reshape(8, gs.shape[0], LANES), "sum_small")
    for n, g in zip(SMALL, _unpack(gsum, 0, small_shapes)):
        if n in SMALL_SHARDED:
            width = w[n].shape[-1]
            g = lax.dynamic_slice_in_dim(g, chip * width, width, axis=g.ndim - 1)
        gout[n] = g

    delta, new_m, new_v = {}, {}, {}
    for n in BIG:
        sh = w[n].shape
        two = lambda a: a.reshape(-1, sh[-1])
        d, nm, nv = _adamw(two(w[n]), two(gout[n]), two(mom[n]), two(var[n]), "adamw_" + n)
        delta[n], new_m[n], new_v[n] = d.reshape(sh), nm.reshape(sh), nv.reshape(sh)
    local_shapes = [w[n].shape for n in SMALL]
    packs = [_pack([d[n] for n in SMALL], 0, F32, SUBLANES) for d in (w, gout, mom, var)]
    for dst, buf in zip((delta, new_m, new_v), _adamw(*packs, "adamw_small")):
        dst.update(zip(SMALL, _unpack(buf, 0, local_shapes)))

    return (loss, grad_x[None], *[gout[n] for n in WEIGHTS], *[delta[n] for n in WEIGHTS],
            *[new_m[n] for n in WEIGHTS], *[new_v[n] for n in WEIGHTS])
```
